```python
import jax, jax.numpy as jnp
from jax import lax
import numpy as np

D_MODEL = 2048
BATCH = 8
SEQ = 8192
DEPTH = 1

CHUNK = 64
D_MIX = D_MODEL
D_CONV = D_MIX // 2
CONV_GROUPS = 8
CONV_WIDTH = 3
D_GLA_V = D_MIX - D_CONV
GLA_HEADS = 4
GLA_DV = D_GLA_V // GLA_HEADS
GLA_DK = GLA_DV // 2
D_GLA_K = GLA_HEADS * GLA_DK
GATE_RANK = 16
GATE_NORMALIZER = 16.0
D_FF = 4 * D_MODEL
EPS = 1e-6
IN_SIZES = (D_CONV, D_CONV, D_CONV, D_GLA_K, D_GLA_K, D_GLA_V, D_GLA_V, GATE_RANK)
D_IN = sum(IN_SIZES)

kernel_name = "hymba_conv_gla_sqrelu_block"


def rmsnorm(x, g):
    xf = x.astype(jnp.float32)
    y = xf * lax.rsqrt(jnp.mean(xf * xf, axis=-1, keepdims=True) + EPS)
    return (y * g.astype(jnp.float32)).astype(x.dtype)


def group_rms(x, groups):
    xf = x.astype(jnp.float32).reshape(x.shape[:-1] + (groups, x.shape[-1] // groups))
    xf = xf * lax.rsqrt(jnp.mean(xf * xf, axis=-1, keepdims=True) + EPS)
    return xf.reshape(x.shape)


def short_conv_mixer(b_gate, c_gate, h, conv_w, conv_g):
    seq = h.shape[1]
    u = c_gate * h
    up = jnp.pad(u, ((0, 0), (CONV_WIDTH - 1, 0), (0, 0)))
    conv = sum(up[:, k:k + seq, :] * conv_w[:, k] for k in range(CONV_WIDTH))
    y = b_gate * conv
    return (group_rms(y, CONV_GROUPS) * conv_g.astype(jnp.float32)).astype(h.dtype)


def gla_chunk_causal(q, k, v, log_a):
    bsz, seq, nh, dk = q.shape
    dv = v.shape[-1]
    nc = seq // CHUNK
    f32 = jnp.float32
    q = q.astype(f32).reshape(bsz, nc, CHUNK, nh, dk) * (dk ** -0.5)
    k = k.astype(f32).reshape(bsz, nc, CHUNK, nh, dk)
    v = v.astype(f32).reshape(bsz, nc, CHUNK, nh, dv)
    la = log_a.astype(f32).reshape(bsz, nc, CHUNK, nh, dk)
    b_cum = jnp.cumsum(la, axis=2)
    b_end = b_cum[:, :, -1:]
    k_dec = k * jnp.exp(b_end - b_cum)
    kv = jnp.einsum('bnshk,bnshv->bnhkv', k_dec, v)
    decay = jnp.exp(b_end[:, :, 0])

    def step(state, xs):
        q_c, dec_c, kv_c = xs
        state = dec_c[..., None] * state + kv_c
        o_c = jnp.einsum('bthk,bhkv->bthv', q_c, state)
        return state, o_c

    s0 = jnp.zeros((bsz, nh, dk, dv), f32)
    xs = (jnp.moveaxis(q, 1, 0), jnp.moveaxis(decay, 1, 0), jnp.moveaxis(kv, 1, 0))
    _, o = lax.scan(step, s0, xs)
    return jnp.moveaxis(o, 0, 1).reshape(bsz, seq, nh, dv)


def _fwd_setup_inputs(seed: int = 0) -> dict:
    key = jax.random.key(seed)
    ks = jax.random.split(key, 14)
    f32 = jnp.float32
    nrm = lambda k, shape, s: jax.random.normal(k, shape, f32) * s
    gain = lambda k, shape: 1.0 + 0.02 * jax.random.normal(k, shape, f32)
    return {
        "x": jax.random.normal(ks[0], (BATCH, SEQ, D_MODEL), f32),
        "norm1_g": gain(ks[1], (DEPTH, D_MODEL)),
        "w_in": nrm(ks[2], (DEPTH, D_MODEL, D_IN), D_MODEL ** -0.5),
        "w_gate_up": nrm(ks[3], (DEPTH, GATE_RANK, D_GLA_K), GATE_RANK ** -0.5),
        "b_gate": nrm(ks[4], (DEPTH, D_GLA_K), 0.1),
        "conv_w": nrm(ks[5], (DEPTH, D_CONV, CONV_WIDTH), CONV_WIDTH ** -0.5),
        "conv_norm_g": gain(ks[6], (DEPTH, D_CONV)),
        "gla_norm_g": gain(ks[7], (DEPTH, GLA_DV)),
        "w_out": nrm(ks[8], (DEPTH, D_MIX, D_MODEL), D_MIX ** -0.5),
        "norm2_g": gain(ks[9], (DEPTH, D_MODEL)),
        "w_ff1": nrm(ks[10], (DEPTH, D_MODEL, D_FF), D_MODEL ** -0.5),
        "w_ff2": nrm(ks[11], (DEPTH, D_FF, D_MODEL), D_FF ** -0.5),
        "norm_f_g": gain(ks[12], (D_MODEL,)),
    }


def _fwd_reference(x, norm1_g, w_in, w_gate_up, b_gate, conv_w, conv_norm_g, gla_norm_g,
              w_out, norm2_g, w_ff1, w_ff2, norm_f_g):
    bsz, seq, _ = x.shape
    split_at = [int(i) for i in np.cumsum(IN_SIZES)[:-1]]
    for l in range(DEPTH):
        u = rmsnorm(x, norm1_g[l])
        z = u @ w_in[l]
        cb, cc, ch, q, k, v, og, a_low = jnp.split(z, split_at, axis=-1)
        y_conv = short_conv_mixer(cb, cc, ch, conv_w[l], conv_norm_g[l])
        log_a = jax.nn.log_sigmoid(a_low @ w_gate_up[l] + b_gate[l]) / GATE_NORMALIZER
        o = gla_chunk_causal(q.reshape(bsz, seq, GLA_HEADS, GLA_DK),
                             k.reshape(bsz, seq, GLA_HEADS, GLA_DK),
                             v.reshape(bsz, seq, GLA_HEADS, GLA_DV),
                             log_a.reshape(bsz, seq, GLA_HEADS, GLA_DK))
        o = o * lax.rsqrt(jnp.mean(o * o, axis=-1, keepdims=True) + EPS)
        o = o * gla_norm_g[l].astype(jnp.float32) * jax.nn.silu(
            og.astype(jnp.float32).reshape(bsz, seq, GLA_HEADS, GLA_DV))
        y_gla = o.reshape(bsz, seq, D_GLA_V).astype(x.dtype)
        y = jnp.concatenate([y_conv, y_gla], axis=-1)
        x = x + y @ w_out[l]
        h = rmsnorm(x, norm2_g[l])
        x = x + jnp.square(jax.nn.relu(h @ w_ff1[l])) @ w_ff2[l]
    return rmsnorm(x, norm_f_g)


import jax as _jax
import jax.numpy as _jnp

TWIN_FORMAT = 'train_step'
FWD_PARAMS = ['x', 'norm1_g', 'w_in', 'w_gate_up', 'b_gate', 'conv_w', 'conv_norm_g', 'gla_norm_g', 'w_out', 'norm2_g', 'w_ff1', 'w_ff2', 'norm_f_g']
TWIN_WEIGHTS = ['norm1_g', 'w_in', 'w_gate_up', 'b_gate', 'conv_w', 'conv_norm_g', 'gla_norm_g', 'w_out', 'norm2_g', 'w_ff1', 'w_ff2', 'norm_f_g']
TWIN_DIFF_INPUT = 'x'
TWIN_INPUTS = ['x', 'norm1_g', 'w_in', 'w_gate_up', 'b_gate', 'conv_w', 'conv_norm_g', 'gla_norm_g', 'w_out', 'norm2_g', 'w_ff1', 'w_ff2', 'norm_f_g', 'loss_target', 'm_norm1_g', 'm_w_in', 'm_w_gate_up', 'm_b_gate', 'm_conv_w', 'm_conv_norm_g', 'm_gla_norm_g', 'm_w_out', 'm_norm2_g', 'm_w_ff1', 'm_w_ff2', 'm_norm_f_g', 'v_norm1_g', 'v_w_in', 'v_w_gate_up', 'v_b_gate', 'v_conv_w', 'v_conv_norm_g', 'v_gla_norm_g', 'v_w_out', 'v_norm2_g', 'v_w_ff1', 'v_w_ff2', 'v_norm_f_g']
TWIN_OUTPUTS = ['loss', 'grad_x', 'grad_norm1_g', 'grad_w_in', 'grad_w_gate_up', 'grad_b_gate', 'grad_conv_w', 'grad_conv_norm_g', 'grad_gla_norm_g', 'grad_w_out', 'grad_norm2_g', 'grad_w_ff1', 'grad_w_ff2', 'grad_norm_f_g', 'delta_norm1_g', 'delta_w_in', 'delta_w_gate_up', 'delta_b_gate', 'delta_conv_w', 'delta_conv_norm_g', 'delta_gla_norm_g', 'delta_w_out', 'delta_norm2_g', 'delta_w_ff1', 'delta_w_ff2', 'delta_norm_f_g', 'new_m_norm1_g', 'new_m_w_in', 'new_m_w_gate_up', 'new_m_b_gate', 'new_m_conv_w', 'new_m_conv_norm_g', 'new_m_gla_norm_g', 'new_m_w_out', 'new_m_norm2_g', 'new_m_w_ff1', 'new_m_w_ff2', 'new_m_norm_f_g', 'new_v_norm1_g', 'new_v_w_in', 'new_v_w_gate_up', 'new_v_b_gate', 'new_v_conv_w', 'new_v_conv_norm_g', 'new_v_gla_norm_g', 'new_v_w_out', 'new_v_norm2_g', 'new_v_w_ff1', 'new_v_w_ff2', 'new_v_norm_f_g']
TWIN_LEAF_KINDS = {'loss': 'loss', 'grad_x': 'grad_x', 'grad_norm1_g': 'grad_w', 'grad_w_in': 'grad_w', 'grad_w_gate_up': 'grad_w', 'grad_b_gate': 'grad_w', 'grad_conv_w': 'grad_w', 'grad_conv_norm_g': 'grad_w', 'grad_gla_norm_g': 'grad_w', 'grad_w_out': 'grad_w', 'grad_norm2_g': 'grad_w', 'grad_w_ff1': 'grad_w', 'grad_w_ff2': 'grad_w', 'grad_norm_f_g': 'grad_w', 'delta_norm1_g': 'delta_w', 'delta_w_in': 'delta_w', 'delta_w_gate_up': 'delta_w', 'delta_b_gate': 'delta_w', 'delta_conv_w': 'delta_w', 'delta_conv_norm_g': 'delta_w', 'delta_gla_norm_g': 'delta_w', 'delta_w_out': 'delta_w', 'delta_norm2_g': 'delta_w', 'delta_w_ff1': 'delta_w', 'delta_w_ff2': 'delta_w', 'delta_norm_f_g': 'delta_w', 'new_m_norm1_g': 'new_m', 'new_m_w_in': 'new_m', 'new_m_w_gate_up': 'new_m', 'new_m_b_gate': 'new_m', 'new_m_conv_w': 'new_m', 'new_m_conv_norm_g': 'new_m', 'new_m_gla_norm_g': 'new_m', 'new_m_w_out': 'new_m', 'new_m_norm2_g': 'new_m', 'new_m_w_ff1': 'new_m', 'new_m_w_ff2': 'new_m', 'new_m_norm_f_g': 'new_m', 'new_v_norm1_g': 'new_v', 'new_v_w_in': 'new_v', 'new_v_w_gate_up': 'new_v', 'new_v_b_gate': 'new_v', 'new_v_conv_w': 'new_v', 'new_v_conv_norm_g': 'new_v', 'new_v_gla_norm_g': 'new_v', 'new_v_w_out': 'new_v', 'new_v_norm2_g': 'new_v', 'new_v_w_ff1': 'new_v', 'new_v_w_ff2': 'new_v', 'new_v_norm_f_g': 'new_v'}


def _forward(args):
    return _fwd_reference(*[args[k] for k in FWD_PARAMS])


def _output_shape():
    def fwd():
        inp = _fwd_setup_inputs(0)
        return _fwd_reference(*[inp[k] for k in FWD_PARAMS])
    out = _jax.eval_shape(fwd)
    return out.shape, out.dtype

N_MICROBATCH = 1
ADAM_LR = 0.001
ADAM_B1 = 0.9
ADAM_B2 = 0.999
ADAM_EPS = 1e-08
ADAM_WD = 0.01
ADAM_STEP = 10
PER_EXAMPLE_BATCH_AXIS = {'x': 0, 'loss_target': 0}
SHARED_INPUTS = []
_WEIGHT_DTYPES = {'norm1_g': _jnp.float32, 'w_in': _jnp.float32, 'w_gate_up': _jnp.float32, 'b_gate': _jnp.float32, 'conv_w': _jnp.float32, 'conv_norm_g': _jnp.float32, 'gla_norm_g': _jnp.float32, 'w_out': _jnp.float32, 'norm2_g': _jnp.float32, 'w_ff1': _jnp.float32, 'w_ff2': _jnp.float32, 'norm_f_g': _jnp.float32}
MOMENT_SCALE = {'norm1_g': 1.619059e-01, 'w_in': 9.165564e-02, 'w_gate_up': 1.060179e-02, 'b_gate': 5.196746e-02, 'conv_w': 1.096081e-01, 'conv_norm_g': 1.151533e-01, 'gla_norm_g': 1.236191e-01, 'w_out': 8.593757e-02, 'norm2_g': 1.002890e-01, 'w_ff1': 4.975955e-02, 'w_ff2': 1.023914e-01, 'norm_f_g': 3.217163e+01}


def _to_microbatches(a, axis):
    t = _jnp.moveaxis(a, axis, 0)
    t = t.reshape((N_MICROBATCH, t.shape[0] // N_MICROBATCH) + t.shape[1:])
    return _jnp.moveaxis(t, 1, axis + 1)


def setup_inputs(seed: int = 0) -> dict:
    inp = _fwd_setup_inputs(seed)
    key = _jax.random.fold_in(_jax.random.key(seed), 7919)
    shape, _ = _output_shape()
    out = dict(inp)
    out["loss_target"] = _jax.random.normal(_jax.random.fold_in(key, 0), shape, _jnp.float32)
    for i, name in enumerate(TWIN_WEIGHTS):
        w = inp[name].astype(_jnp.float32)
        if MOMENT_SCALE is None:
            s = _jnp.sqrt(_jnp.mean(_jnp.square(w)) + 1e-30)
        else:
            s = MOMENT_SCALE[name]
        km, kv = _jax.random.split(_jax.random.fold_in(key, i + 1))
        out[name] = w
        out["m_" + name] = s * _jax.random.normal(km, w.shape, _jnp.float32)
        out["v_" + name] = (s * s) * _jax.random.uniform(kv, w.shape, _jnp.float32, 0.5, 1.5)
    if N_MICROBATCH > 1:
        for name, axis in PER_EXAMPLE_BATCH_AXIS.items():
            out[name] = _to_microbatches(out[name], axis)
    return {'x': out['x'], 'norm1_g': out['norm1_g'], 'w_in': out['w_in'], 'w_gate_up': out['w_gate_up'], 'b_gate': out['b_gate'], 'conv_w': out['conv_w'], 'conv_norm_g': out['conv_norm_g'], 'gla_norm_g': out['gla_norm_g'], 'w_out': out['w_out'], 'norm2_g': out['norm2_g'], 'w_ff1': out['w_ff1'], 'w_ff2': out['w_ff2'], 'norm_f_g': out['norm_f_g'], 'loss_target': out['loss_target'], 'm_norm1_g': out['m_norm1_g'], 'm_w_in': out['m_w_in'], 'm_w_gate_up': out['m_w_gate_up'], 'm_b_gate': out['m_b_gate'], 'm_conv_w': out['m_conv_w'], 'm_conv_norm_g': out['m_conv_norm_g'], 'm_gla_norm_g': out['m_gla_norm_g'], 'm_w_out': out['m_w_out'], 'm_norm2_g': out['m_norm2_g'], 'm_w_ff1': out['m_w_ff1'], 'm_w_ff2': out['m_w_ff2'], 'm_norm_f_g': out['m_norm_f_g'], 'v_norm1_g': out['v_norm1_g'], 'v_w_in': out['v_w_in'], 'v_w_gate_up': out['v_w_gate_up'], 'v_b_gate': out['v_b_gate'], 'v_conv_w': out['v_conv_w'], 'v_conv_norm_g': out['v_conv_norm_g'], 'v_gla_norm_g': out['v_gla_norm_g'], 'v_w_out': out['v_w_out'], 'v_norm2_g': out['v_norm2_g'], 'v_w_ff1': out['v_w_ff1'], 'v_w_ff2': out['v_w_ff2'], 'v_norm_f_g': out['v_norm_f_g']}


def _loss(weights, diff, rest, loss_target):
    with _jax.named_scope("forward"):
        args = {**rest, TWIN_DIFF_INPUT: diff, **{k: w.astype(_WEIGHT_DTYPES[k]) for k, w in weights.items()}}
        y = _forward(args)
    with _jax.named_scope("loss_head"):
        err = _jnp.square(y.astype(_jnp.float32) - loss_target)
        return 0.5 * _jnp.sum(_jnp.mean(err, axis=-1)) if err.ndim else 0.5 * err


def _adamw(w, g, m, v):
    m = ADAM_B1 * m + (1.0 - ADAM_B1) * g
    v = ADAM_B2 * v + (1.0 - ADAM_B2) * _jnp.square(g)
    m_hat = m / (1.0 - ADAM_B1 ** ADAM_STEP)
    v_hat = v / (1.0 - ADAM_B2 ** ADAM_STEP)
    delta = -ADAM_LR * (m_hat / (_jnp.sqrt(v_hat) + ADAM_EPS) + ADAM_WD * w)
    return delta, m, v


def reference(x, norm1_g, w_in, w_gate_up, b_gate, conv_w, conv_norm_g, gla_norm_g, w_out, norm2_g, w_ff1, w_ff2, norm_f_g, loss_target, m_norm1_g, m_w_in, m_w_gate_up, m_b_gate, m_conv_w, m_conv_norm_g, m_gla_norm_g, m_w_out, m_norm2_g, m_w_ff1, m_w_ff2, m_norm_f_g, v_norm1_g, v_w_in, v_w_gate_up, v_b_gate, v_conv_w, v_conv_norm_g, v_gla_norm_g, v_w_out, v_norm2_g, v_w_ff1, v_w_ff2, v_norm_f_g):
    given = dict(x=x, norm1_g=norm1_g, w_in=w_in, w_gate_up=w_gate_up, b_gate=b_gate, conv_w=conv_w, conv_norm_g=conv_norm_g, gla_norm_g=gla_norm_g, w_out=w_out, norm2_g=norm2_g, w_ff1=w_ff1, w_ff2=w_ff2, norm_f_g=norm_f_g, loss_target=loss_target, m_norm1_g=m_norm1_g, m_w_in=m_w_in, m_w_gate_up=m_w_gate_up, m_b_gate=m_b_gate, m_conv_w=m_conv_w, m_conv_norm_g=m_conv_norm_g, m_gla_norm_g=m_gla_norm_g, m_w_out=m_w_out, m_norm2_g=m_norm2_g, m_w_ff1=m_w_ff1, m_w_ff2=m_w_ff2, m_norm_f_g=m_norm_f_g, v_norm1_g=v_norm1_g, v_w_in=v_w_in, v_w_gate_up=v_w_gate_up, v_b_gate=v_b_gate, v_conv_w=v_conv_w, v_conv_norm_g=v_conv_norm_g, v_gla_norm_g=v_gla_norm_g, v_w_out=v_w_out, v_norm2_g=v_norm2_g, v_w_ff1=v_w_ff1, v_w_ff2=v_w_ff2, v_norm_f_g=v_norm_f_g)
    weights = {n: given[n] for n in TWIN_WEIGHTS}
    shared = {n: given[n] for n in SHARED_INPUTS}
    per_example = {n: given[n] for n in ['x']}
    grad_fn = _jax.value_and_grad(_loss, argnums=(0, 1))

    def one_microbatch(ex, loss_target):
        ex = dict(ex)
        diff = ex.pop(TWIN_DIFF_INPUT)
        return grad_fn(weights, diff, {**shared, **ex}, loss_target)

    if N_MICROBATCH == 1:
        loss, (grad_w, grad_x) = one_microbatch(per_example, given["loss_target"])
    else:
        def body(carry, xs):
            loss_sum, grad_sum = carry
            l_k, (gw_k, gx_k) = one_microbatch(xs[0], xs[1])
            with _jax.named_scope("update"):
                return (loss_sum + l_k, _jax.tree.map(_jnp.add, grad_sum, gw_k)), gx_k

        init = (_jnp.zeros((), _jnp.float32), _jax.tree.map(_jnp.zeros_like, weights))
        (loss, grad_w), grad_x = _jax.lax.scan(body, init, (per_example, given["loss_target"]))
    with _jax.named_scope("update"):
        delta_w, new_m, new_v = {}, {}, {}
        for n in TWIN_WEIGHTS:
            delta_w[n], new_m[n], new_v[n] = _adamw(weights[n], grad_w[n], given["m_" + n], given["v_" + n])
    return (loss, grad_x, *[grad_w[n] for n in TWIN_WEIGHTS], *[delta_w[n] for n in TWIN_WEIGHTS],
            *[new_m[n] for n in TWIN_WEIGHTS], *[new_v[n] for n in TWIN_WEIGHTS])
```

```python
import functools

import jax
import jax.numpy as jnp
from jax import lax
from jax.experimental import pallas as pl
from jax.experimental.pallas import tpu as pltpu

F32 = jnp.float32
BF16 = jnp.bfloat16
MESH = pl.DeviceIdType.MESH

EPS = 1e-6
CHUNK = 64
HEADS = 4
DK = 128
DV = 256
D_CONV = 1024
GROUP = 128
GATE_RANK = 16
LANES = 128
SUBLANES = 8
N_CHIPS = 4
N_DEV = 8
CHIP_MASKS = ((1, 0), (0, 1), (1, 1))

ADAM_LR = 0.001
ADAM_B1 = 0.9
ADAM_B2 = 0.999
ADAM_EPS = 1e-08
ADAM_WD = 0.01
ADAM_STEP = 10

VMEM_LIMIT = 56 * 1024 * 1024


def _params(*sem):
    return pltpu.CompilerParams(dimension_semantics=tuple(sem), vmem_limit_bytes=VMEM_LIMIT)


def _rowsum8(v):
    r, c = v.shape
    return jnp.sum(v.reshape(r // SUBLANES, SUBLANES, c), axis=0)


def _matmul(a, b, *, ta=False, tb=False, tm, tn, tk, out_dtypes, name, extras=(), epilogue=None,
            out_shapes=None, out_specs=None):
    if ta:
        k_dim, m_dim = a.shape
    else:
        m_dim, k_dim = a.shape
    n_dim = b.shape[0] if tb else b.shape[1]
    assert (b.shape[1] if tb else b.shape[0]) == k_dim
    assert m_dim % tm == 0 and n_dim % tn == 0 and k_dim % tk == 0, (name, a.shape, b.shape)
    nk = k_dim // tk
    n_ex, n_out = len(extras), len(out_dtypes)
    dims = (((0 if ta else 1,), (1 if tb else 0,)), ((), ()))

    def body(*refs):
        a_ref, b_ref = refs[0], refs[1]
        ex_refs = refs[2:2 + n_ex]
        o_refs = refs[2 + n_ex:2 + n_ex + n_out]
        part = lax.dot_general(a_ref[...].astype(BF16), b_ref[...].astype(BF16), dims,
                               preferred_element_type=F32)

        def finish(acc):
            outs = epilogue(acc, *[e[...] for e in ex_refs]) if epilogue is not None else (acc,)
            for o_ref, o in zip(o_refs, outs):
                o_ref[...] = o.astype(o_ref.dtype)

        if nk == 1:
            finish(part)
        else:
            acc_ref = refs[-1]
            k = pl.program_id(2)

            @pl.when(k == 0)
            def _():
                acc_ref[...] = part

            @pl.when(k > 0)
            def _():
                acc_ref[...] += part

            @pl.when(k == nk - 1)
            def _():
                finish(acc_ref[...])

    a_spec = (pl.BlockSpec((tk, tm), lambda i, j, k: (k, i)) if ta
              else pl.BlockSpec((tm, tk), lambda i, j, k: (i, k)))
    b_spec = (pl.BlockSpec((tn, tk), lambda i, j, k: (j, k)) if tb
              else pl.BlockSpec((tk, tn), lambda i, j, k: (k, j)))
    io_spec = pl.BlockSpec((tm, tn), lambda i, j, k: (i, j))
    if out_shapes is None:
        out_shapes = [jax.ShapeDtypeStruct((m_dim, n_dim), dt) for dt in out_dtypes]
    if out_specs is None:
        out_specs = [io_spec] * n_out
    res = pl.pallas_call(
        body,
        grid=(m_dim // tm, n_dim // tn, nk),
        in_specs=[a_spec, b_spec] + [io_spec] * n_ex,
        out_specs=out_specs,
        out_shape=out_shapes,
        scratch_shapes=[pltpu.VMEM((tm, tn), F32)] if nk > 1 else [],
        compiler_params=_params("parallel", "parallel", "arbitrary"),
        name=name,
    )(a, b, *extras)
    return res


def _add_epilogue(acc, r):
    return (acc + r,)


def _rms_fwd(x, g, *, name, tm=512):
    s_len, d = x.shape

    def body(x_ref, g_ref, o_ref):
        xv = x_ref[...]
        r = lax.rsqrt(jnp.mean(xv * xv, axis=-1, keepdims=True) + EPS)
        o_ref[...] = (xv * r * g_ref[...]).astype(o_ref.dtype)

    return pl.pallas_call(
        body, grid=(s_len // tm,),
        in_specs=[pl.BlockSpec((tm, d), lambda i: (i, 0)), pl.BlockSpec((1, d), lambda i: (0, 0))],
        out_specs=pl.BlockSpec((tm, d), lambda i: (i, 0)),
        out_shape=jax.ShapeDtypeStruct((s_len, d), BF16),
        compiler_params=_params("parallel"), name=name,
    )(x, g)


def _rms_bwd(dn, x, g, res, *, name, tm=512):
    s_len, d = x.shape
    n = s_len // tm

    def body(dn_ref, x_ref, g_ref, res_ref, dx_ref, dg_ref, acc_ref):
        i = pl.program_id(0)
        xv = x_ref[...]
        dnv = dn_ref[...]
        r = lax.rsqrt(jnp.mean(xv * xv, axis=-1, keepdims=True) + EPS)
        xh = xv * r

        @pl.when(i == 0)
        def _():
            acc_ref[...] = jnp.zeros_like(acc_ref)

        acc_ref[...] += _rowsum8(dnv * xh)
        dxh = dnv * g_ref[...]
        dx_ref[...] = r * (dxh - xh * jnp.mean(dxh * xh, axis=-1, keepdims=True)) + res_ref[...]

        @pl.when(i == n - 1)
        def _():
            dg_ref[...] = jnp.sum(acc_ref[...], axis=0, keepdims=True)

    row = pl.BlockSpec((tm, d), lambda i: (i, 0))
    vec = pl.BlockSpec((1, d), lambda i: (0, 0))
    return pl.pallas_call(
        body, grid=(n,),
        in_specs=[row, row, vec, row],
        out_specs=[row, vec],
        out_shape=[jax.ShapeDtypeStruct((s_len, d), F32), jax.ShapeDtypeStruct((1, d), F32)],
        scratch_shapes=[pltpu.VMEM((SUBLANES, d), F32)],
        compiler_params=_params("arbitrary"), name=name,
    )(dn, x, g, res)


def _loss_head(x3, g, target, *, tm=512):
    s_len, d = x3.shape
    n = s_len // tm

    def body(x_ref, g_ref, t_ref, dx_ref, dxb_ref, dg_ref, loss_ref, accg_ref, accl_ref):
        i = pl.program_id(0)
        xv = x_ref[...]
        gv = g_ref[...]
        r = lax.rsqrt(jnp.mean(xv * xv, axis=-1, keepdims=True) + EPS)
        xh = xv * r
        err = xh * gv - t_ref[...]

        @pl.when(i == 0)
        def _():
            accg_ref[...] = jnp.zeros_like(accg_ref)
            accl_ref[...] = jnp.zeros_like(accl_ref)

        accl_ref[...] += _rowsum8(err * err)
        dn = err * (1.0 / d)
        accg_ref[...] += _rowsum8(dn * xh)
        dxh = dn * gv
        dx = r * (dxh - xh * jnp.mean(dxh * xh, axis=-1, keepdims=True))
        dx_ref[...] = dx
        dxb_ref[...] = dx.astype(BF16)

        @pl.when(i == n - 1)
        def _():
            dg_ref[...] = jnp.sum(accg_ref[...], axis=0, keepdims=True)
            tot = jnp.sum(jnp.sum(accl_ref[...], axis=0, keepdims=True), axis=1, keepdims=True)
            loss_ref[...] = jnp.broadcast_to(tot * (0.5 / d), (1, LANES))

    row = pl.BlockSpec((tm, d), lambda i: (i, 0))
    vec = pl.BlockSpec((1, d), lambda i: (0, 0))
    return pl.pallas_call(
        body, grid=(n,),
        in_specs=[row, vec, row],
        out_specs=[row, row, vec, pl.BlockSpec((1, LANES), lambda i: (0, 0))],
        out_shape=[jax.ShapeDtypeStruct((s_len, d), F32), jax.ShapeDtypeStruct((s_len, d), BF16),
                   jax.ShapeDtypeStruct((1, d), F32), jax.ShapeDtypeStruct((1, LANES), F32)],
        scratch_shapes=[pltpu.VMEM((SUBLANES, d), F32), pltpu.VMEM((SUBLANES, d), F32)],
        compiler_params=_params("arbitrary"), name="loss_head",
    )(x3, g, target)


def _shift_down(v, k, rows_before, row):
    out = pltpu.roll(v, k, axis=0)
    for j in range(k):
        out = jnp.where(row == j, rows_before[j], out)
    return out


def _shift_up(v, k, rows_after, row):
    t = v.shape[0]
    out = pltpu.roll(v, t - k, axis=0)
    for j in range(k):
        out = jnp.where(row == t - k + j, rows_after[j], out)
    return out


def _conv_fwd(z, w_t, gain, *, ts=512):
    s_len = z.shape[0]
    n_grp = D_CONV // GROUP

    def body(cb_ref, cc_ref, ch_ref, w_ref, g_ref, y_ref, carry_ref):
        i = pl.program_id(0)

        @pl.when(i == 0)
        def _():
            carry_ref[...] = jnp.zeros_like(carry_ref)

        row = lax.broadcasted_iota(jnp.int32, (ts, GROUP), 0)
        for g in range(n_grp):
            sl = slice(g * GROUP, (g + 1) * GROUP)
            uu = cc_ref[:, sl] * ch_ref[:, sl]
            p2 = carry_ref[6:7, sl]
            p1 = carry_ref[7:8, sl]
            u1 = _shift_down(uu, 1, [p1], row)
            u2 = _shift_down(uu, 2, [p2, p1], row)
            conv = w_ref[0:1, sl] * u2 + w_ref[1:2, sl] * u1 + w_ref[2:3, sl] * uu
            y = cb_ref[:, sl] * conv
            carry_ref[:, sl] = uu[ts - SUBLANES:ts, :]
            rg = lax.rsqrt(jnp.mean(y * y, axis=-1, keepdims=True) + EPS)
            y_ref[:, sl] = (y * rg * g_ref[:, sl]).astype(BF16)

    def col(j):
        return pl.BlockSpec((ts, D_CONV), lambda i, j=j: (i, j))

    small = lambda r: pl.BlockSpec((r, D_CONV), lambda i: (0, 0))
    return pl.pallas_call(
        body, grid=(s_len // ts,),
        in_specs=[col(0), col(1), col(2), small(3), small(1)],
        out_specs=col(0),
        out_shape=jax.ShapeDtypeStruct((s_len, 2 * D_CONV), BF16),
        scratch_shapes=[pltpu.VMEM((SUBLANES, D_CONV), F32)],
        compiler_params=_params("arbitrary"), name="conv_fwd",
    )(z, z, z, w_t, gain)


def _conv_bwd(dy, z, w_t, gain, *, ts=512):
    s_len = z.shape[0]
    n = s_len // ts
    n_grp = D_CONV // GROUP
    halo_blocks = ts // SUBLANES

    def body(dy_ref, cb_ref, cc_ref, ch_ref, hcc_ref, hch_ref, w_ref, g_ref,
             dz_ref, dw_ref, dg_ref, carry_ref, accw_ref, accg_ref):
        i = pl.program_id(0)
        first_tile = (n - 1 - i) == 0

        @pl.when(i == 0)
        def _():
            carry_ref[...] = jnp.zeros_like(carry_ref)
            accw_ref[...] = jnp.zeros_like(accw_ref)
            accg_ref[...] = jnp.zeros_like(accg_ref)

        row = lax.broadcasted_iota(jnp.int32, (ts, GROUP), 0)
        keep = jnp.where(first_tile, 0.0, 1.0)
        for g in range(n_grp):
            sl = slice(g * GROUP, (g + 1) * GROUP)
            cc = cc_ref[:, sl]
            ch = ch_ref[:, sl]
            cb = cb_ref[:, sl]
            uu = cc * ch
            p2 = hcc_ref[6:7, sl] * hch_ref[6:7, sl] * keep
            p1 = hcc_ref[7:8, sl] * hch_ref[7:8, sl] * keep
            u1 = _shift_down(uu, 1, [p1], row)
            u2 = _shift_down(uu, 2, [p2, p1], row)
            w0, w1, w2 = w_ref[0:1, sl], w_ref[1:2, sl], w_ref[2:3, sl]
            conv = w0 * u2 + w1 * u1 + w2 * uu
            y = cb * conv
            rg = lax.rsqrt(jnp.mean(y * y, axis=-1, keepdims=True) + EPS)
            yh = y * rg
            dyv = dy_ref[:, sl]
            accg_ref[:, sl] += _rowsum8(dyv * yh)
            dyn = dyv * g_ref[:, sl]
            dpre = rg * (dyn - yh * jnp.mean(dyn * yh, axis=-1, keepdims=True))
            dz_ref[:, sl] = dpre * conv
            dconv = dpre * cb
            accw_ref[0:8, sl] += _rowsum8(dconv * u2)
            accw_ref[8:16, sl] += _rowsum8(dconv * u1)
            accw_ref[16:24, sl] += _rowsum8(dconv * uu)
            n0 = carry_ref[0:1, sl]
            n1 = carry_ref[1:2, sl]
            d1 = _shift_up(dconv, 1, [n0], row)
            d2 = _shift_up(dconv, 2, [n0, n1], row)
            duu = w2 * dconv + w1 * d1 + w0 * d2
            carry_ref[:, sl] = dconv[0:SUBLANES, :]
            dz_ref[:, D_CONV + g * GROUP:D_CONV + (g + 1) * GROUP] = duu * ch
            dz_ref[:, 2 * D_CONV + g * GROUP:2 * D_CONV + (g + 1) * GROUP] = duu * cc

        @pl.when(i == n - 1)
        def _():
            for k in range(3):
                dw_ref[k:k + 1, :] = jnp.sum(accw_ref[8 * k:8 * k + 8, :], axis=0, keepdims=True)
            dg_ref[...] = jnp.sum(accg_ref[...], axis=0, keepdims=True)

    def col(j):
        return pl.BlockSpec((ts, D_CONV), lambda i, j=j: (n - 1 - i, j))

    def halo(j):
        return pl.BlockSpec((SUBLANES, D_CONV),
                            lambda i, j=j: (jnp.maximum((n - 1 - i) * halo_blocks - 1, 0), j))

    small = lambda r: pl.BlockSpec((r, D_CONV), lambda i: (0, 0))
    return pl.pallas_call(
        body, grid=(n,),
        in_specs=[col(0), col(0), col(1), col(2), halo(1), halo(2), small(3), small(1)],
        out_specs=[pl.BlockSpec((ts, 3 * D_CONV), lambda i: (n - 1 - i, 0)), small(3), small(1)],
        out_shape=[jax.ShapeDtypeStruct((s_len, 6 * D_CONV), F32),
                   jax.ShapeDtypeStruct((3, D_CONV), F32), jax.ShapeDtypeStruct((1, D_CONV), F32)],
        scratch_shapes=[pltpu.VMEM((SUBLANES, D_CONV), F32), pltpu.VMEM((24, D_CONV), F32),
                        pltpu.VMEM((SUBLANES, D_CONV), F32)],
        compiler_params=_params("arbitrary"), name="conv_bwd",
    )(dy, z, z, z, z, z, w_t, gain)


def _split3(v):
    hi = v.astype(BF16)
    r1 = v - hi.astype(F32)
    mid = r1.astype(BF16)
    lo = (r1 - mid.astype(F32)).astype(BF16)
    return jnp.concatenate([hi, mid, lo], axis=1)


def _tri_sum(tri, v):
    w = v.shape[1]
    dd = jnp.dot(tri, _split3(v), preferred_element_type=F32)
    return dd[:, :w] + dd[:, w:2 * w] + dd[:, 2 * w:]


def _strict_tri(upper):
    r = lax.broadcasted_iota(jnp.int32, (CHUNK, CHUNK), 0)
    c = lax.broadcasted_iota(jnp.int32, (CHUNK, CHUNK), 1)
    return jnp.where((c > r) if upper else (c < r), 1.0, 0.0).astype(BF16)


def _sigmoid(v):
    return 1.0 / (1.0 + jnp.exp(-v))


def _gla_fwd(z, alow, wgu, bg, gg, y_in, *, ts=512):
    s_len = z.shape[0]
    nch = ts // CHUNK
    scale = DK ** -0.5

    def body(q_ref, k_ref, v_ref, og_ref, al_ref, wgu_ref, bg_ref, gg_ref, yin_ref,
             y_ref, o_ref, la_ref, st_ref, state_ref):
        del yin_ref
        i = pl.program_id(0)

        @pl.when(i == 0)
        def _():
            state_ref[...] = jnp.zeros_like(state_ref)

        pre = jnp.dot(al_ref[...].astype(BF16), wgu_ref[...], preferred_element_type=F32) + bg_ref[...]
        la_ref[...] = (jnp.minimum(pre, 0.0) - jnp.log(1.0 + jnp.exp(-jnp.abs(pre)))) * (1.0 / 16.0)
        upper = _strict_tri(True)
        ggv = gg_ref[...]

        def chunk(cl, carry):
            rows = pl.ds(pl.multiple_of(cl * CHUNK, CHUNK), CHUNK)
            la_c = la_ref[rows, :]
            e_dec = jnp.exp(_tri_sum(upper, la_c))
            dec = jnp.exp(jnp.sum(la_c, axis=0, keepdims=True))
            kd = k_ref[rows, :] * e_dec
            qs = q_ref[rows, :] * scale
            for h in range(HEADS):
                ks = slice(h * DK, (h + 1) * DK)
                vs = slice(h * DV, (h + 1) * DV)
                kv_t = lax.dot_general(v_ref[rows, vs].astype(BF16), kd[:, ks].astype(BF16),
                                       (((0,), (0,)), ((), ())), preferred_element_type=F32)
                st = state_ref[h] * dec[:, ks] + kv_t
                state_ref[h] = st
                st_ref[cl, h] = st
                o_h = lax.dot_general(qs[:, ks].astype(BF16), st.astype(BF16),
                                      (((1,), (1,)), ((), ())), preferred_element_type=F32)
                o_ref[rows, vs] = o_h
                ro = lax.rsqrt(jnp.mean(o_h * o_h, axis=-1, keepdims=True) + EPS)
                og_h = og_ref[rows, vs]
                y_ref[rows, vs] = (o_h * ro * ggv * (og_h * _sigmoid(og_h))).astype(BF16)
            return carry

        lax.fori_loop(0, nch, chunk, 0)

    def zcol(width, j):
        return pl.BlockSpec((ts, width), lambda i, j=j: (i, j))

    full = lambda shape: pl.BlockSpec(shape, lambda i: tuple(0 for _ in shape))
    return pl.pallas_call(
        body, grid=(s_len // ts,),
        in_specs=[zcol(512, 6), zcol(512, 7), zcol(1024, 4), zcol(1024, 5), zcol(LANES, 0),
                  full((LANES, 512)), full((1, 512)), full((1, DV)),
                  pl.BlockSpec(memory_space=pl.ANY)],
        out_specs=[zcol(1024, 1), zcol(1024, 0), zcol(512, 0),
                   pl.BlockSpec((nch, HEADS, DV, DK), lambda i: (i, 0, 0, 0))],
        out_shape=[jax.ShapeDtypeStruct((s_len, 2048), BF16), jax.ShapeDtypeStruct((s_len, 1024), F32),
                   jax.ShapeDtypeStruct((s_len, 512), F32),
                   jax.ShapeDtypeStruct((s_len // CHUNK, HEADS, DV, DK), F32)],
        scratch_shapes=[pltpu.VMEM((HEADS, DV, DK), F32)],
        input_output_aliases={8: 0},
        compiler_params=_params("arbitrary"), name="gla_fwd",
    )(z, z, z, z, alow, wgu, bg, gg, y_in)


def _gla_bwd(dy, z, o, la, st, alow, wgu, gg, dz_in, *, ts=512):
    s_len = z.shape[0]
    n = s_len // ts
    nch = ts // CHUNK
    scale = DK ** -0.5

    def body(dy_ref, q_ref, k_ref, v_ref, og_ref, o_ref, la_ref, st_ref, stp_ref, al_ref, wgu_ref,
             gg_ref, dzin_ref, dz_ref, dal_ref, dwgu_ref, dbg_ref, dgg_ref,
             gt_ref, decn_ref, accw_ref, accb_ref, accg_ref, dla_ref, tt_ref, dbe_ref):
        del dzin_ref
        i = pl.program_id(0)
        first_tile = (n - 1 - i) == 0

        @pl.when(i == 0)
        def _():
            gt_ref[...] = jnp.zeros_like(gt_ref)
            decn_ref[...] = jnp.ones_like(decn_ref)
            accw_ref[...] = jnp.zeros_like(accw_ref)
            accb_ref[...] = jnp.zeros_like(accb_ref)
            accg_ref[...] = jnp.zeros_like(accg_ref)

        upper = _strict_tri(True)
        lower = _strict_tri(False)
        ggv = gg_ref[...]
        keep = jnp.where(first_tile, 0.0, 1.0)

        def chunk(jrev, carry):
            cl = nch - 1 - jrev
            rows = pl.ds(pl.multiple_of(cl * CHUNK, CHUNK), CHUNK)
            la_c = la_ref[rows, :]
            e_dec = jnp.exp(_tri_sum(upper, la_c))
            dec = jnp.exp(jnp.sum(la_c, axis=0, keepdims=True))
            kd = k_ref[rows, :] * e_dec
            qs = q_ref[rows, :] * scale
            decn = decn_ref[0:1, :]
            has_prev = jnp.where(cl > 0, 1.0, 0.0)
            prev_idx = jnp.maximum(cl - 1, 0)
            for h in range(HEADS):
                ks = slice(h * DK, (h + 1) * DK)
                vs = slice(h * DV, (h + 1) * DV)
                o_h = o_ref[rows, vs]
                og_h = og_ref[rows, vs]
                dy_h = dy_ref[rows, vs]
                ro = lax.rsqrt(jnp.mean(o_h * o_h, axis=-1, keepdims=True) + EPS)
                oh = o_h * ro
                sig = _sigmoid(og_h)
                sil = og_h * sig
                accg_ref[...] += _rowsum8(dy_h * oh * sil)
                dz_ref[rows, 2048 + h * DV:2048 + (h + 1) * DV] = (
                    dy_h * oh * ggv * sig * (1.0 + og_h * (1.0 - sig)))
                don = dy_h * ggv * sil
                do = ro * (don - oh * jnp.mean(don * oh, axis=-1, keepdims=True))
                dob = do.astype(BF16)
                s_c = st_ref[cl, h]
                dqs = jnp.dot(dob, s_c.astype(BF16), preferred_element_type=F32)
                dz_ref[rows, ks] = dqs * scale
                gt = gt_ref[h] * decn[:, ks] + lax.dot_general(
                    dob, qs[:, ks].astype(BF16), (((0,), (0,)), ((), ())), preferred_element_type=F32)
                gt_ref[h] = gt
                gb = gt.astype(BF16)
                kd_h = kd[:, ks]
                dkd = jnp.dot(v_ref[rows, vs].astype(BF16), gb, preferred_element_type=F32)
                dz_ref[rows, 1024 + h * DV:1024 + (h + 1) * DV] = lax.dot_general(
                    kd_h.astype(BF16), gb, (((1,), (1,)), ((), ())), preferred_element_type=F32)
                s_prev = has_prev * st_ref[prev_idx, h] + (1.0 - has_prev) * keep * stp_ref[0, h]
                ddec = jnp.sum(gt * s_prev, axis=0, keepdims=True)
                dz_ref[rows, 512 + h * DK:512 + (h + 1) * DK] = dkd * e_dec[:, ks]
                tt_ref[:, ks] = dkd * kd_h
                dbe_ref[0:1, ks] = ddec * dec[:, ks]
            dla_ref[rows, :] = _tri_sum(lower, tt_ref[...]) + dbe_ref[0:1, :]
            decn_ref[0:1, :] = dec
            return carry

        lax.fori_loop(0, nch, chunk, 0)

        dpre = dla_ref[...] * (1.0 / 16.0) * (1.0 - jnp.exp(16.0 * la_ref[...]))
        accb_ref[...] += _rowsum8(dpre)
        dpb = dpre.astype(BF16)
        accw_ref[...] += lax.dot_general(al_ref[...].astype(BF16), dpb, (((0,), (0,)), ((), ())),
                                         preferred_element_type=F32)
        dal_ref[...] = lax.dot_general(dpb, wgu_ref[...], (((1,), (1,)), ((), ())),
                                       preferred_element_type=F32)

        @pl.when(i == n - 1)
        def _():
            dwgu_ref[...] = accw_ref[...]
            dbg_ref[...] = jnp.sum(accb_ref[...], axis=0, keepdims=True)
            dgg_ref[...] = jnp.sum(accg_ref[...], axis=0, keepdims=True)

    def zcol(width, j):
        return pl.BlockSpec((ts, width), lambda i, j=j: (n - 1 - i, j))

    full = lambda shape: pl.BlockSpec(shape, lambda i: tuple(0 for _ in shape))
    return pl.pallas_call(
        body, grid=(n,),
        in_specs=[zcol(1024, 1), zcol(512, 6), zcol(512, 7), zcol(1024, 4), zcol(1024, 5),
                  zcol(1024, 0), zcol(512, 0),
                  pl.BlockSpec((nch, HEADS, DV, DK), lambda i: (n - 1 - i, 0, 0, 0)),
                  pl.BlockSpec((1, HEADS, DV, DK),
                               lambda i: (jnp.maximum((n - 1 - i) * nch - 1, 0), 0, 0, 0)),
                  zcol(LANES, 0), full((LANES, 512)), full((1, DV)),
                  pl.BlockSpec(memory_space=pl.ANY)],
        out_specs=[zcol(3072, 1), zcol(LANES, 0), full((LANES, 512)), full((1, 512)), full((1, DV))],
        out_shape=[jax.ShapeDtypeStruct((s_len, 6144), F32), jax.ShapeDtypeStruct((s_len, LANES), F32),
                   jax.ShapeDtypeStruct((LANES, 512), F32), jax.ShapeDtypeStruct((1, 512), F32),
                   jax.ShapeDtypeStruct((1, DV), F32)],
        scratch_shapes=[pltpu.VMEM((HEADS, DV, DK), F32), pltpu.VMEM((SUBLANES, 512), F32),
                        pltpu.VMEM((LANES, 512), F32), pltpu.VMEM((SUBLANES, 512), F32),
                        pltpu.VMEM((SUBLANES, DV), F32), pltpu.VMEM((ts, 512), F32),
                        pltpu.VMEM((CHUNK, 512), F32), pltpu.VMEM((SUBLANES, 512), F32)],
        input_output_aliases={12: 0},
        compiler_params=_params("arbitrary"), name="gla_bwd",
    )(dy, z, z, z, z, o, la, st, st, alow, wgu, gg, dz_in)


def _adamw(w, g, m, v, *, name):
    rows, cols = w.shape
    tr = rows
    for cand in (256, 128, 64, 32, 16, 8):
        if rows % cand == 0 and rows > cand:
            tr = cand
            break

    def body(w_ref, g_ref, m_ref, v_ref, d_ref, nm_ref, nv_ref):
        gv = g_ref[...]
        m2 = ADAM_B1 * m_ref[...] + (1.0 - ADAM_B1) * gv
        v2 = ADAM_B2 * v_ref[...] + (1.0 - ADAM_B2) * jnp.square(gv)
        m_hat = m2 / (1.0 - ADAM_B1 ** ADAM_STEP)
        v_hat = v2 / (1.0 - ADAM_B2 ** ADAM_STEP)
        d_ref[...] = -ADAM_LR * (m_hat / (jnp.sqrt(v_hat) + ADAM_EPS) + ADAM_WD * w_ref[...])
        nm_ref[...] = m2
        nv_ref[...] = v2

    blk = pl.BlockSpec((tr, cols), lambda i: (i, 0))
    shp = jax.ShapeDtypeStruct((rows, cols), F32)
    return pl.pallas_call(
        body, grid=(rows // tr,), in_specs=[blk] * 4, out_specs=[blk] * 3, out_shape=[shp] * 3,
        compiler_params=_params("parallel"), name=name,
    )(w, g, m, v)


def _my_place():
    return lax.axis_index("x"), lax.axis_index("y"), lax.axis_index("c")


def _flip(v, bit):
    return 1 - v if bit else v


def _allgather_small(buf, *, reduce, name):
    rows = buf.shape[0]

    def body(in_ref, out_ref, gat_ref, send_sems, recv_sems):
        x, y, c = _my_place()
        me = 4 * x + 2 * y + c
        gat_ref[me] = in_ref[...]
        copies = []
        for m in range(1, N_DEV):
            peer = (_flip(x, m & 4), _flip(y, m & 2), _flip(c, m & 1))
            cp = pltpu.make_async_remote_copy(
                src_ref=in_ref, dst_ref=gat_ref.at[me],
                send_sem=send_sems.at[m - 1], recv_sem=recv_sems.at[m - 1],
                device_id=peer, device_id_type=MESH)
            cp.start()
            copies.append(cp)
        for m in range(1, N_DEV):
            px, py, pc = _flip(x, m & 4), _flip(y, m & 2), _flip(c, m & 1)
            src_slot = gat_ref.at[4 * px + 2 * py + pc]
            pltpu.make_async_remote_copy(
                src_ref=src_slot, dst_ref=src_slot,
                send_sem=send_sems.at[m - 1], recv_sem=recv_sems.at[m - 1],
                device_id=(px, py, pc), device_id_type=MESH).wait_recv()
        for cp in copies:
            cp.wait_send()
        if reduce:
            tot = gat_ref[0]
            for d in range(1, N_DEV):
                tot = tot + gat_ref[d]
            out_ref[...] = tot
        else:
            out_ref[...] = gat_ref[...]

    out_shape = (rows, LANES) if reduce else (N_DEV, rows, LANES)
    return pl.pallas_call(
        body,
        in_specs=[pl.BlockSpec(memory_space=pltpu.VMEM)],
        out_specs=pl.BlockSpec(memory_space=pltpu.VMEM),
        out_shape=jax.ShapeDtypeStruct(out_shape, F32),
        scratch_shapes=[pltpu.VMEM((N_DEV, rows, LANES), F32),
                        pltpu.SemaphoreType.DMA((N_DEV - 1,)), pltpu.SemaphoreType.DMA((N_DEV - 1,))],
        compiler_params=pltpu.CompilerParams(has_side_effects=True),
        name=name,
    )(buf)


def _gather_weights(shards):
    n_w = len(shards)
    n_m = len(CHIP_MASKS)

    def body(*refs):
        in_refs = refs[:n_w]
        out_refs = refs[n_w:2 * n_w]
        send_sems, recv_sems, loc_sems = refs[2 * n_w:]
        x, y, c = _my_place()
        chip = 2 * x + y
        sends = []
        for w in range(n_w):
            for mi, (mx, my) in enumerate(CHIP_MASKS):
                k = w * n_m + mi
                cp = pltpu.make_async_remote_copy(
                    src_ref=in_refs[w].at[c], dst_ref=out_refs[w].at[chip, c],
                    send_sem=send_sems.at[k], recv_sem=recv_sems.at[k],
                    device_id=(_flip(x, mx), _flip(y, my), c), device_id_type=MESH)
                cp.start()
                sends.append(cp)
        locs = []
        for w in range(n_w):
            lc = pltpu.make_async_copy(in_refs[w], out_refs[w].at[chip], loc_sems.at[w])
            lc.start()
            locs.append(lc)
        for w in range(n_w):
            for mi, (mx, my) in enumerate(CHIP_MASKS):
                k = w * n_m + mi
                px, py = _flip(x, mx), _flip(y, my)
                landed = out_refs[w].at[2 * px + py, c]
                pltpu.make_async_remote_copy(
                    src_ref=landed, dst_ref=landed, send_sem=send_sems.at[k], recv_sem=recv_sems.at[k],
                    device_id=(px, py, c), device_id_type=MESH).wait_recv()
                fw = pltpu.make_async_remote_copy(
                    src_ref=landed, dst_ref=landed,
                    send_sem=send_sems.at[n_w * n_m + k], recv_sem=recv_sems.at[n_w * n_m + k],
                    device_id=(x, y, 1 - c), device_id_type=MESH)
                fw.start()
                sends.append(fw)
        for w in range(n_w):
            for mi, (mx, my) in enumerate(CHIP_MASKS):
                k = n_w * n_m + w * n_m + mi
                px, py = _flip(x, mx), _flip(y, my)
                landed = out_refs[w].at[2 * px + py, 1 - c]
                pltpu.make_async_remote_copy(
                    src_ref=landed, dst_ref=landed, send_sem=send_sems.at[k], recv_sem=recv_sems.at[k],
                    device_id=(x, y, 1 - c), device_id_type=MESH).wait_recv()
        for cp in sends:
            cp.wait_send()
        for lc in locs:
            lc.wait()

    any_spec = pl.BlockSpec(memory_space=pl.ANY)
    return pl.pallas_call(
        body,
        in_specs=[any_spec] * n_w, out_specs=[any_spec] * n_w,
        out_shape=[jax.ShapeDtypeStruct((N_CHIPS,) + s.shape, s.dtype) for s in shards],
        scratch_shapes=[pltpu.SemaphoreType.DMA((2 * n_w * n_m,)), pltpu.SemaphoreType.DMA((2 * n_w * n_m,)),
                        pltpu.SemaphoreType.DMA((n_w,))],
        compiler_params=pltpu.CompilerParams(has_side_effects=True),
        name="gather_weights",
    )(*shards)


def _swap_halves(grads):
    n_w = len(grads)

    def body(*refs):
        in_refs = refs[:n_w]
        out_refs = refs[n_w:2 * n_w]
        send_sems, recv_sems = refs[2 * n_w:]
        x, y, c = _my_place()
        copies = []
        for w in range(n_w):
            cp = pltpu.make_async_remote_copy(
                src_ref=in_refs[w].at[:, 1 - c], dst_ref=out_refs[w],
                send_sem=send_sems.at[w], recv_sem=recv_sems.at[w],
                device_id=(x, y, 1 - c), device_id_type=MESH)
            cp.start()
            copies.append(cp)
        for cp in copies:
            cp.wait()

    any_spec = pl.BlockSpec(memory_space=pl.ANY)
    return pl.pallas_call(
        body, in_specs=[any_spec] * n_w, out_specs=[any_spec] * n_w,
        out_shape=[jax.ShapeDtypeStruct((g.shape[0],) + g.shape[2:], g.dtype) for g in grads],
        scratch_shapes=[pltpu.SemaphoreType.DMA((n_w,)), pltpu.SemaphoreType.DMA((n_w,))],
        compiler_params=pltpu.CompilerParams(has_side_effects=True),
        name="swap_halves",
    )(*grads)


def _add_own_half(g, other, core, *, name):
    n_chip, _, rows, cols = g.shape
    tr = 256

    def body(core_ref, g_ref, o_ref, out_ref):
        del core_ref
        out_ref[...] = (g_ref[...].astype(F32) + o_ref[...].astype(F32)).astype(BF16)

    grid_spec = pltpu.PrefetchScalarGridSpec(
        num_scalar_prefetch=1, grid=(n_chip, rows // tr),
        in_specs=[pl.BlockSpec((None, None, tr, cols), lambda j, r, core_ref: (j, core_ref[0], r, 0)),
                  pl.BlockSpec((None, tr, cols), lambda j, r, core_ref: (j, r, 0))],
        out_specs=pl.BlockSpec((None, tr, cols), lambda j, r, core_ref: (j, r, 0)))
    return pl.pallas_call(
        body, grid_spec=grid_spec, out_shape=jax.ShapeDtypeStruct((n_chip, rows, cols), BF16),
        compiler_params=_params("parallel", "parallel"), name=name,
    )(core, g, other)


def _exchange_chips(pieces):
    n_w = len(pieces)
    n_m = len(CHIP_MASKS)

    def body(*refs):
        in_refs = refs[:n_w]
        out_refs = refs[n_w:2 * n_w]
        send_sems, recv_sems, loc_sems = refs[2 * n_w:]
        x, y, c = _my_place()
        chip = 2 * x + y
        copies = []
        for w in range(n_w):
            for mi, (mx, my) in enumerate(CHIP_MASKS):
                k = w * n_m + mi
                px, py = _flip(x, mx), _flip(y, my)
                cp = pltpu.make_async_remote_copy(
                    src_ref=in_refs[w].at[2 * px + py], dst_ref=out_refs[w].at[chip],
                    send_sem=send_sems.at[k], recv_sem=recv_sems.at[k],
                    device_id=(px, py, c), device_id_type=MESH)
                cp.start()
                copies.append(cp)
        locs = []
        for w in range(n_w):
            lc = pltpu.make_async_copy(in_refs[w].at[chip], out_refs[w].at[chip], loc_sems.at[w])
            lc.start()
            locs.append(lc)
        for w in range(n_w):
            for mi, (mx, my) in enumerate(CHIP_MASKS):
                k = w * n_m + mi
                px, py = _flip(x, mx), _flip(y, my)
                landed = out_refs[w].at[2 * px + py]
                pltpu.make_async_remote_copy(
                    src_ref=landed, dst_ref=landed, send_sem=send_sems.at[k], recv_sem=recv_sems.at[k],
                    device_id=(px, py, c), device_id_type=MESH).wait_recv()
        for cp in copies:
            cp.wait_send()
        for lc in locs:
            lc.wait()

    any_spec = pl.BlockSpec(memory_space=pl.ANY)
    return pl.pallas_call(
        body, in_specs=[any_spec] * n_w, out_specs=[any_spec] * n_w,
        out_shape=[jax.ShapeDtypeStruct(p.shape, p.dtype) for p in pieces],
        scratch_shapes=[pltpu.SemaphoreType.DMA((n_w * n_m,)), pltpu.SemaphoreType.DMA((n_w * n_m,)),
                        pltpu.SemaphoreType.DMA((n_w,))],
        compiler_params=pltpu.CompilerParams(has_side_effects=True),
        name="exchange_chips",
    )(*pieces)


def _sum_chips(parts, *, name):
    n_chip, rows, cols = parts.shape
    tr = 256

    def body(p_ref, out_ref):
        tot = p_ref[0].astype(F32)
        for j in range(1, n_chip):
            tot = tot + p_ref[j].astype(F32)
        out_ref[...] = tot

    return pl.pallas_call(
        body, grid=(rows // tr,),
        in_specs=[pl.BlockSpec((n_chip, tr, cols), lambda r: (0, r, 0))],
        out_specs=pl.BlockSpec((tr, cols), lambda r: (r, 0)),
        out_shape=jax.ShapeDtypeStruct((rows, cols), F32),
        compiler_params=_params("parallel"), name=name,
    )(parts)


def _join_halves(halves):
    n_w = len(halves)

    def body(*refs):
        in_refs = refs[:n_w]
        out_refs = refs[n_w:2 * n_w]
        send_sems, recv_sems, loc_sems = refs[2 * n_w:]
        x, y, c = _my_place()
        copies, locs = [], []
        for w in range(n_w):
            cp = pltpu.make_async_remote_copy(
                src_ref=in_refs[w], dst_ref=out_refs[w].at[c],
                send_sem=send_sems.at[w], recv_sem=recv_sems.at[w],
                device_id=(x, y, 1 - c), device_id_type=MESH)
            cp.start()
            copies.append(cp)
            lc = pltpu.make_async_copy(in_refs[w], out_refs[w].at[c], loc_sems.at[w])
            lc.start()
            locs.append(lc)
        for w in range(n_w):
            theirs = out_refs[w].at[1 - c]
            pltpu.make_async_remote_copy(
                src_ref=theirs, dst_ref=theirs, send_sem=send_sems.at[w], recv_sem=recv_sems.at[w],
                device_id=(x, y, 1 - c), device_id_type=MESH).wait_recv()
        for cp in copies:
            cp.wait_send()
        for lc in locs:
            lc.wait()

    any_spec = pl.BlockSpec(memory_space=pl.ANY)
    return pl.pallas_call(
        body, in_specs=[any_spec] * n_w, out_specs=[any_spec] * n_w,
        out_shape=[jax.ShapeDtypeStruct((2,) + h.shape, h.dtype) for h in halves],
        scratch_shapes=[pltpu.SemaphoreType.DMA((n_w,)), pltpu.SemaphoreType.DMA((n_w,)),
                        pltpu.SemaphoreType.DMA((n_w,))],
        compiler_params=pltpu.CompilerParams(has_side_effects=True),
        name="join_halves",
    )(*halves)


def _pack(pieces):
    flat, spans, off = [], [], 0
    for p in pieces:
        v = p.reshape(-1).astype(F32)
        pad = (-v.shape[0]) % LANES
        if pad:
            v = jnp.concatenate([v, jnp.zeros((pad,), F32)])
        spans.append((off, p.size))
        off += v.shape[0]
        flat.append(v)
    tail = (-off) % (SUBLANES * LANES)
    if tail:
        flat.append(jnp.zeros((tail,), F32))
    return jnp.concatenate(flat).reshape(-1, LANES), spans


def _unpack(buf, span, shape):
    off, size = span
    return buf.reshape(-1)[off:off + size].reshape(shape)


def kernel(x, norm1_g, w_in, w_gate_up, b_gate, conv_w, conv_norm_g, gla_norm_g, w_out, norm2_g, w_ff1, w_ff2, norm_f_g, loss_target, m_norm1_g, m_w_in, m_w_gate_up, m_b_gate, m_conv_w, m_conv_norm_g, m_gla_norm_g, m_w_out, m_norm2_g, m_w_ff1, m_w_ff2, m_norm_f_g, v_norm1_g, v_w_in, v_w_gate_up, v_b_gate, v_conv_w, v_conv_norm_g, v_gla_norm_g, v_w_out, v_norm2_g, v_w_ff1, v_w_ff2, v_norm_f_g):
    xs = x[0]
    target = loss_target[0]
    s_len, d = xs.shape
    d_in = w_in.shape[2] * N_CHIPS
    d_main = d_in - GATE_RANK
    d_ff = w_ff1.shape[2] * N_CHIPS
    cx, cy, cc = _my_place()
    chip = 2 * cx + cy
    core = jnp.reshape(cc, (1,)).astype(jnp.int32)

    def halves(w2d):
        return w2d.astype(BF16).reshape(2, w2d.shape[0] // 2, w2d.shape[1])

    wi_all, wo_all, w1_all, w2_all = _gather_weights(
        [halves(w_in[0]), halves(w_out[0]), halves(w_ff1[0]), halves(w_ff2[0])])
    wi_all = wi_all.reshape(N_CHIPS, d, d_in // N_CHIPS)
    wi_full = jnp.concatenate([wi_all[j] for j in range(N_CHIPS)], axis=1)
    w_main = wi_full[:, :d_main]
    w_alow = jnp.concatenate([wi_full[:, d_main:], jnp.zeros((d, LANES - GATE_RANK), BF16)], axis=1)
    wo_full = wo_all.reshape(d, d)
    w1_all = w1_all.reshape(N_CHIPS, d, d_ff // N_CHIPS)
    w1_full = jnp.concatenate([w1_all[j] for j in range(N_CHIPS)], axis=1)
    w2_full = w2_all.reshape(d_ff, d)

    small_w, spans_w = _pack([w_gate_up[0], conv_w[0]])
    small_all = _allgather_small(small_w, reduce=False, name="gather_small_weights")
    chips_first = [small_all[2 * j] for j in range(N_CHIPS)]
    wgu_full = jnp.concatenate(
        [_unpack(b, spans_w[0], w_gate_up.shape[1:]) for b in chips_first], axis=1)
    convw_full = jnp.concatenate(
        [_unpack(b, spans_w[1], conv_w.shape[1:]) for b in chips_first], axis=0)
    wgu_pad = jnp.concatenate(
        [wgu_full, jnp.zeros((LANES - GATE_RANK, wgu_full.shape[1]), F32)], axis=0).astype(BF16)
    convw_t = convw_full.T

    (loss_part, grad_x, g_norm1, g_wi_main, g_wi_gate, g_wgu_pad, g_bg, g_convw_t, g_convg, g_gg, g_wo,
     g_norm2, g_w1, g_w2, g_normf) = _block_fwd_bwd(
        xs, target, norm1_g, w_main, w_alow, wgu_pad, b_gate, convw_t, conv_norm_g, gla_norm_g, wo_full,
        norm2_g, w1_full, w2_full, norm_f_g.reshape(1, d))

    n_sh = d_in // N_CHIPS
    g_wi = jnp.concatenate([g_wi_main, g_wi_gate[:, :GATE_RANK]], axis=1)
    g_wi = g_wi.reshape(d, N_CHIPS, n_sh).transpose(1, 0, 2)
    grads = [g_wi.reshape(N_CHIPS, 2, d // 2, n_sh),
             g_wo.reshape(N_CHIPS, 2, d // (2 * N_CHIPS), d),
             g_w1.reshape(N_CHIPS, 2, d // 2, d_ff // N_CHIPS),
             g_w2.reshape(N_CHIPS, 2, d_ff // (2 * N_CHIPS), d)]
    names = ["w_in", "w_out", "w_ff1", "w_ff2"]
    theirs = _swap_halves(grads)
    pre = [_add_own_half(g, t, core, name="pre_reduce_" + nm) for g, t, nm in zip(grads, theirs, names)]
    landed = _exchange_chips(pre)
    mine = [_sum_chips(l, name="reduce_" + nm) for l, nm in zip(landed, names)]
    joined = _join_halves(mine)
    g_big = [j.reshape(2 * j.shape[1], j.shape[2]) for j in joined]

    small_g, spans_g = _pack([g_norm1, g_wgu_pad[:GATE_RANK], g_bg, g_convw_t, g_convg, g_gg, g_norm2,
                              g_normf, loss_part[:, :1]])
    tot = _allgather_small(small_g, reduce=True, name="reduce_small_grads")
    t_norm1 = _unpack(tot, spans_g[0], (1, d))
    t_wgu = _unpack(tot, spans_g[1], (GATE_RANK, HEADS * DK))
    t_bg = _unpack(tot, spans_g[2], (1, HEADS * DK))
    t_convw = _unpack(tot, spans_g[3], (3, D_CONV)).T
    t_convg = _unpack(tot, spans_g[4], (1, D_CONV))
    t_gg = _unpack(tot, spans_g[5], (1, DV))
    t_norm2 = _unpack(tot, spans_g[6], (1, d))
    t_normf = _unpack(tot, spans_g[7], (1, d))
    loss = _unpack(tot, spans_g[8], ())
    n_gu = w_gate_up.shape[2]
    n_cw = conv_w.shape[1]
    t_wgu = lax.dynamic_slice(t_wgu, (0, chip * n_gu), (GATE_RANK, n_gu))
    t_convw = lax.dynamic_slice(t_convw, (chip * n_cw, 0), (n_cw, 3))

    order = ["norm1_g", "w_in", "w_gate_up", "b_gate", "conv_w", "conv_norm_g", "gla_norm_g", "w_out",
             "norm2_g", "w_ff1", "w_ff2", "norm_f_g"]
    weights = dict(norm1_g=norm1_g, w_in=w_in, w_gate_up=w_gate_up, b_gate=b_gate, conv_w=conv_w,
                   conv_norm_g=conv_norm_g, gla_norm_g=gla_norm_g, w_out=w_out, norm2_g=norm2_g,
                   w_ff1=w_ff1, w_ff2=w_ff2, norm_f_g=norm_f_g)
    moms = dict(norm1_g=m_norm1_g, w_in=m_w_in, w_gate_up=m_w_gate_up, b_gate=m_b_gate, conv_w=m_conv_w,
                conv_norm_g=m_conv_norm_g, gla_norm_g=m_gla_norm_g, w_out=m_w_out, norm2_g=m_norm2_g,
                w_ff1=m_w_ff1, w_ff2=m_w_ff2, norm_f_g=m_norm_f_g)
    vels = dict(norm1_g=v_norm1_g, w_in=v_w_in, w_gate_up=v_w_gate_up, b_gate=v_b_gate, conv_w=v_conv_w,
                conv_norm_g=v_conv_norm_g, gla_norm_g=v_gla_norm_g, w_out=v_w_out, norm2_g=v_norm2_g,
                w_ff1=v_w_ff1, w_ff2=v_w_ff2, norm_f_g=v_norm_f_g)
    grads2d = dict(norm1_g=t_norm1, w_in=g_big[0], w_gate_up=t_wgu, b_gate=t_bg, conv_w=t_convw,
                   conv_norm_g=t_convg, gla_norm_g=t_gg, w_out=g_big[1], norm2_g=t_norm2,
                   w_ff1=g_big[2], w_ff2=g_big[3], norm_f_g=t_normf)
    out_g, out_d, out_m, out_v = [], [], [], []
    for nm in order:
        w = weights[nm]
        g2 = grads2d[nm]
        dlt, nm_, nv_ = _adamw(w.reshape(g2.shape), g2, moms[nm].reshape(g2.shape),
                               vels[nm].reshape(g2.shape), name="adamw_" + nm)
        out_g.append(g2.reshape(w.shape))
        out_d.append(dlt.reshape(w.shape))
        out_m.append(nm_.reshape(w.shape))
        out_v.append(nv_.reshape(w.shape))
    return (loss, grad_x.reshape(x.shape), *out_g, *out_d, *out_m, *out_v)


def _block_fwd_bwd(xs, target, norm1_g, w_main, w_alow, wgu_pad, b_gate, convw_t, conv_norm_g, gla_norm_g,
                   wo_full, norm2_g, w1_full, w2_full, norm_f_g):
    s_len, d = xs.shape
    d_ff = w1_full.shape[1]
    u = _rms_fwd(xs, norm1_g, name="norm1_fwd")
    (z,) = _matmul(u, w_main, tm=1024, tn=1024, tk=d, out_dtypes=[F32], name="in_proj")
    (alow,) = _matmul(u, w_alow, tm=1024, tn=LANES, tk=d, out_dtypes=[F32], name="in_proj_gate")
    y0 = _conv_fwd(z, convw_t, conv_norm_g)
    y, o, la, st = _gla_fwd(z, alow, wgu_pad, b_gate, gla_norm_g, y0)
    (x2,) = _matmul(y, wo_full, tm=1024, tn=1024, tk=d, out_dtypes=[F32], extras=(xs,),
                    epilogue=_add_epilogue, name="out_proj")
    h = _rms_fwd(x2, norm2_g, name="norm2_fwd")
    a, p = _matmul(h, w1_full, tm=1024, tn=1024, tk=d, out_dtypes=[F32, BF16],
                   epilogue=lambda acc: (acc, jnp.square(jnp.maximum(acc, 0.0))), name="ff1")
    (x3,) = _matmul(p, w2_full, tm=1024, tn=1024, tk=2048, out_dtypes=[F32], extras=(x2,),
                    epilogue=_add_epilogue, name="ff2")
    dx3, dx3b, g_normf, loss_part = _loss_head(x3, norm_f_g.reshape(1, d), target)

    (da,) = _matmul(dx3b, w2_full, tb=True, tm=1024, tn=1024, tk=d, out_dtypes=[BF16], extras=(a,),
                    epilogue=lambda acc, av: (acc * (2.0 * jnp.maximum(av, 0.0)),), name="ff2_dx")
    (g_w2,) = _matmul(p, dx3b, ta=True, tm=2048, tn=2048, tk=512, out_dtypes=[BF16], name="ff2_dw")
    (g_w1,) = _matmul(
        h, da, ta=True, tm=d, tn=d_ff // N_CHIPS, tk=512, out_dtypes=[BF16], name="ff1_dw",
        out_shapes=[jax.ShapeDtypeStruct((N_CHIPS, d, d_ff // N_CHIPS), BF16)],
        out_specs=[pl.BlockSpec((None, d, d_ff // N_CHIPS), lambda i, j, k: (j, 0, 0))])
    (dh,) = _matmul(da, w1_full, tb=True, tm=1024, tn=1024, tk=2048, out_dtypes=[F32], name="ff1_dx")
    dx2, g_norm2 = _rms_bwd(dh, x2, norm2_g, dx3, name="norm2_bwd")
    (dy,) = _matmul(dx2, wo_full, tb=True, tm=1024, tn=1024, tk=d, out_dtypes=[F32], name="out_proj_dx")
    (g_wo,) = _matmul(y, dx2, ta=True, tm=d, tn=d, tk=512, out_dtypes=[BF16], name="out_proj_dw")
    dz0, g_convw_t, g_convg = _conv_bwd(dy, z, convw_t, conv_norm_g)
    dz, dalow, g_wgu_pad, g_bg, g_gg = _gla_bwd(dy, z, o, la, st, alow, wgu_pad, gla_norm_g, dz0)
    (du_gate,) = _matmul(dalow, w_alow, tb=True, tm=1024, tn=1024, tk=LANES, out_dtypes=[F32],
                         name="in_proj_gate_dx")
    (du,) = _matmul(dz, w_main, tb=True, tm=1024, tn=1024, tk=2048, out_dtypes=[F32], extras=(du_gate,),
                    epilogue=_add_epilogue, name="in_proj_dx")
    grad_x, g_norm1 = _rms_bwd(du, xs, norm1_g, dx2, name="norm1_bwd")
    (g_wi_main,) = _matmul(u, dz, ta=True, tm=d, tn=1024, tk=512, out_dtypes=[BF16], name="in_proj_dw")
    (g_wi_gate,) = _matmul(u, dalow, ta=True, tm=d, tn=LANES, tk=512, out_dtypes=[BF16],
                           name="in_proj_gate_dw")
    return (loss_part, grad_x, g_norm1, g_wi_main, g_wi_gate, g_wgu_pad, g_bg, g_convw_t, g_convg, g_gg,
            g_wo, g_norm2, g_w1, g_w2, g_normf)
```

```python
import functools

import jax
import jax.numpy as jnp
from jax import lax
from jax.experimental import pallas as pl
from jax.experimental.pallas import tpu as pltpu

F32 = jnp.float32
BF16 = jnp.bfloat16
MESH = pl.DeviceIdType.MESH

EPS = 1e-6
CHUNK = 64
HEADS = 4
DK = 128
DV = 256
D_CONV = 1024
GROUP = 128
GATE_RANK = 16
LANES = 128
SUBLANES = 8
N_CHIPS = 4
N_DEV = 8
CHIP_MASKS = ((1, 0), (0, 1), (1, 1))

ADAM_LR = 0.001
ADAM_B1 = 0.9
ADAM_B2 = 0.999
ADAM_EPS = 1e-08
ADAM_WD = 0.01
ADAM_STEP = 10

VMEM_LIMIT = 56 * 1024 * 1024


def _params(*sem):
    return pltpu.CompilerParams(dimension_semantics=tuple(sem), vmem_limit_bytes=VMEM_LIMIT)


def _rowsum8(v):
    r, c = v.shape
    return jnp.sum(v.reshape(r // SUBLANES, SUBLANES, c), axis=0)


class _Comm:
    def __init__(self, ins, outs, aliases, n_sems, first, last):
        self.ins = list(ins)
        self.outs = list(outs)
        self.aliases = dict(aliases)
        self.n_sems = n_sems
        self.first = first
        self.last = last


def _call(body, *, grid, in_specs, out_specs, out_shape, operands, name, scratch_shapes=(), sem=None,
          aliases=None, comm=None):
    aliases = dict(aliases or {})
    if comm is None:
        return pl.pallas_call(
            body, grid=grid, in_specs=list(in_specs), out_specs=list(out_specs), out_shape=list(out_shape),
            scratch_shapes=list(scratch_shapes), input_output_aliases=aliases,
            compiler_params=_params(*(sem or ("arbitrary",) * len(grid))), name=name)(*operands)
    n_in, n_out, n_scr = len(in_specs), len(out_specs), len(scratch_shapes)
    n_ci, n_co = len(comm.ins), len(comm.outs)

    def full_body(*refs):
        ins = refs[:n_in]
        c_ins = refs[n_in:n_in + n_ci]
        o0 = n_in + n_ci
        outs = refs[o0:o0 + n_out]
        c_outs = refs[o0 + n_out:o0 + n_out + n_co]
        s0 = o0 + n_out + n_co
        scr = refs[s0:s0 + n_scr]
        send_sems, recv_sems = refs[s0 + n_scr:]
        ids = [pl.program_id(ax) for ax in range(len(grid))]
        is_first = functools.reduce(jnp.logical_and, [i == 0 for i in ids])
        is_last = functools.reduce(jnp.logical_and, [i == g - 1 for i, g in zip(ids, grid)])

        @pl.when(is_first)
        def _():
            comm.first(c_ins, c_outs, send_sems, recv_sems)

        body(*ins, *outs, *scr)

        @pl.when(is_last)
        def _():
            comm.last(c_ins, c_outs, send_sems, recv_sems)

    any_spec = pl.BlockSpec(memory_space=pl.ANY)
    for i_in, i_out in comm.aliases.items():
        aliases[n_in + i_in] = n_out + i_out
    return pl.pallas_call(
        full_body, grid=grid,
        in_specs=list(in_specs) + [any_spec] * n_ci,
        out_specs=list(out_specs) + [any_spec] * n_co,
        out_shape=list(out_shape) + comm.outs,
        scratch_shapes=list(scratch_shapes) + [pltpu.SemaphoreType.DMA((comm.n_sems,)),
                                               pltpu.SemaphoreType.DMA((comm.n_sems,))],
        input_output_aliases=aliases,
        compiler_params=pltpu.CompilerParams(dimension_semantics=("arbitrary",) * len(grid),
                                             vmem_limit_bytes=VMEM_LIMIT, has_side_effects=True),
        name=name)(*operands, *comm.ins)


def _matmul(a, b, *, ta=False, tb=False, tm, tn, tk, out_dtypes, name, extras=(), epilogue=None,
            out_shapes=None, out_specs=None, b_spec=None, n_dim=None, comm=None):
    if ta:
        k_dim, m_dim = a.shape
    else:
        m_dim, k_dim = a.shape
    if n_dim is None:
        n_dim = b.shape[0] if tb else b.shape[1]
        assert (b.shape[1] if tb else b.shape[0]) == k_dim
    assert m_dim % tm == 0 and n_dim % tn == 0 and k_dim % tk == 0, (name, a.shape, b.shape)
    nk = k_dim // tk
    n_ex, n_out = len(extras), len(out_dtypes)
    dims = (((0 if ta else 1,), (1 if tb else 0,)), ((), ()))

    def body(*refs):
        a_ref, b_ref = refs[0], refs[1]
        ex_refs = refs[2:2 + n_ex]
        o_refs = refs[2 + n_ex:2 + n_ex + n_out]
        part = lax.dot_general(a_ref[...].astype(BF16), b_ref[...].astype(BF16), dims,
                               preferred_element_type=F32)

        def finish(acc):
            outs = epilogue(acc, *[e[...] for e in ex_refs]) if epilogue is not None else (acc,)
            for o_ref, o in zip(o_refs, outs):
                o_ref[...] = o.astype(o_ref.dtype)

        if nk == 1:
            finish(part)
        else:
            acc_ref = refs[-1]
            k = pl.program_id(2)

            @pl.when(k == 0)
            def _():
                acc_ref[...] = part

            @pl.when(k > 0)
            def _():
                acc_ref[...] += part

            @pl.when(k == nk - 1)
            def _():
                finish(acc_ref[...])

    a_spec = (pl.BlockSpec((tk, tm), lambda i, j, k: (k, i)) if ta
              else pl.BlockSpec((tm, tk), lambda i, j, k: (i, k)))
    if b_spec is None:
        b_spec = (pl.BlockSpec((tn, tk), lambda i, j, k: (j, k)) if tb
                  else pl.BlockSpec((tk, tn), lambda i, j, k: (k, j)))
    io_spec = pl.BlockSpec((tm, tn), lambda i, j, k: (i, j))
    if out_shapes is None:
        out_shapes = [jax.ShapeDtypeStruct((m_dim, n_dim), dt) for dt in out_dtypes]
    if out_specs is None:
        out_specs = [io_spec] * n_out
    return _call(
        body,
        grid=(m_dim // tm, n_dim // tn, nk),
        in_specs=[a_spec, b_spec] + [io_spec] * n_ex,
        out_specs=out_specs,
        out_shape=out_shapes,
        scratch_shapes=[pltpu.VMEM((tm, tn), F32)] if nk > 1 else [],
        sem=("parallel", "parallel", "arbitrary"),
        operands=(a, b, *extras), name=name, comm=comm)


def _add_epilogue(acc, r):
    return (acc + r,)


def _rms_fwd(x, g, *, name, tm=512, comm=None):
    s_len, d = x.shape

    def body(x_ref, g_ref, o_ref):
        xv = x_ref[...]
        r = lax.rsqrt(jnp.mean(xv * xv, axis=-1, keepdims=True) + EPS)
        o_ref[...] = (xv * r * g_ref[...]).astype(o_ref.dtype)

    return _call(
        body, grid=(s_len // tm,),
        in_specs=[pl.BlockSpec((tm, d), lambda i: (i, 0)), pl.BlockSpec((1, d), lambda i: (0, 0))],
        out_specs=[pl.BlockSpec((tm, d), lambda i: (i, 0))],
        out_shape=[jax.ShapeDtypeStruct((s_len, d), BF16)],
        sem=("parallel",), operands=(x, g), name=name, comm=comm)


def _rms_bwd(dn, x, g, res, *, name, tm=512, comm=None):
    s_len, d = x.shape
    n = s_len // tm

    def body(dn_ref, x_ref, g_ref, res_ref, dx_ref, dg_ref, acc_ref):
        i = pl.program_id(0)
        xv = x_ref[...]
        dnv = dn_ref[...]
        r = lax.rsqrt(jnp.mean(xv * xv, axis=-1, keepdims=True) + EPS)
        xh = xv * r

        @pl.when(i == 0)
        def _():
            acc_ref[...] = jnp.zeros_like(acc_ref)

        acc_ref[...] += _rowsum8(dnv * xh)
        dxh = dnv * g_ref[...]
        dx_ref[...] = r * (dxh - xh * jnp.mean(dxh * xh, axis=-1, keepdims=True)) + res_ref[...]

        @pl.when(i == n - 1)
        def _():
            dg_ref[...] = jnp.sum(acc_ref[...], axis=0, keepdims=True)

    row = pl.BlockSpec((tm, d), lambda i: (i, 0))
    vec = pl.BlockSpec((1, d), lambda i: (0, 0))
    return _call(
        body, grid=(n,),
        in_specs=[row, row, vec, row],
        out_specs=[row, vec],
        out_shape=[jax.ShapeDtypeStruct((s_len, d), F32), jax.ShapeDtypeStruct((1, d), F32)],
        scratch_shapes=[pltpu.VMEM((SUBLANES, d), F32)],
        operands=(dn, x, g, res), name=name, comm=comm)


def _loss_head(x3, g, target, *, tm=512):
    s_len, d = x3.shape
    n = s_len // tm

    def body(x_ref, g_ref, t_ref, dx_ref, dxb_ref, dg_ref, loss_ref, accg_ref, accl_ref):
        i = pl.program_id(0)
        xv = x_ref[...]
        gv = g_ref[...]
        r = lax.rsqrt(jnp.mean(xv * xv, axis=-1, keepdims=True) + EPS)
        xh = xv * r
        err = xh * gv - t_ref[...]

        @pl.when(i == 0)
        def _():
            accg_ref[...] = jnp.zeros_like(accg_ref)
            accl_ref[...] = jnp.zeros_like(accl_ref)

        accl_ref[...] += _rowsum8(err * err)
        dn = err * (1.0 / d)
        accg_ref[...] += _rowsum8(dn * xh)
        dxh = dn * gv
        dx = r * (dxh - xh * jnp.mean(dxh * xh, axis=-1, keepdims=True))
        dx_ref[...] = dx
        dxb_ref[...] = dx.astype(BF16)

        @pl.when(i == n - 1)
        def _():
            dg_ref[...] = jnp.sum(accg_ref[...], axis=0, keepdims=True)
            tot = jnp.sum(jnp.sum(accl_ref[...], axis=0, keepdims=True), axis=1, keepdims=True)
            loss_ref[...] = jnp.broadcast_to(tot * (0.5 / d), (1, LANES))

    row = pl.BlockSpec((tm, d), lambda i: (i, 0))
    vec = pl.BlockSpec((1, d), lambda i: (0, 0))
    return pl.pallas_call(
        body, grid=(n,),
        in_specs=[row, vec, row],
        out_specs=[row, row, vec, pl.BlockSpec((1, LANES), lambda i: (0, 0))],
        out_shape=[jax.ShapeDtypeStruct((s_len, d), F32), jax.ShapeDtypeStruct((s_len, d), BF16),
                   jax.ShapeDtypeStruct((1, d), F32), jax.ShapeDtypeStruct((1, LANES), F32)],
        scratch_shapes=[pltpu.VMEM((SUBLANES, d), F32), pltpu.VMEM((SUBLANES, d), F32)],
        compiler_params=_params("arbitrary"), name="loss_head",
    )(x3, g, target)


def _shift_down(v, k, rows_before, row):
    out = pltpu.roll(v, k, axis=0)
    for j in range(k):
        out = jnp.where(row == j, rows_before[j], out)
    return out


def _shift_up(v, k, rows_after, row):
    t = v.shape[0]
    out = pltpu.roll(v, t - k, axis=0)
    for j in range(k):
        out = jnp.where(row == t - k + j, rows_after[j], out)
    return out


def _conv_fwd(z, w_t, gain, *, ts=512):
    s_len = z.shape[0]
    n_grp = D_CONV // GROUP

    def body(cb_ref, cc_ref, ch_ref, w_ref, g_ref, y_ref, carry_ref):
        i = pl.program_id(0)

        @pl.when(i == 0)
        def _():
            carry_ref[...] = jnp.zeros_like(carry_ref)

        row = lax.broadcasted_iota(jnp.int32, (ts, GROUP), 0)
        for g in range(n_grp):
            sl = slice(g * GROUP, (g + 1) * GROUP)
            uu = cc_ref[:, sl] * ch_ref[:, sl]
            p2 = carry_ref[6:7, sl]
            p1 = carry_ref[7:8, sl]
            u1 = _shift_down(uu, 1, [p1], row)
            u2 = _shift_down(uu, 2, [p2, p1], row)
            conv = w_ref[0:1, sl] * u2 + w_ref[1:2, sl] * u1 + w_ref[2:3, sl] * uu
            y = cb_ref[:, sl] * conv
            carry_ref[:, sl] = uu[ts - SUBLANES:ts, :]
            rg = lax.rsqrt(jnp.mean(y * y, axis=-1, keepdims=True) + EPS)
            y_ref[:, sl] = (y * rg * g_ref[:, sl]).astype(BF16)

    def col(j):
        return pl.BlockSpec((ts, D_CONV), lambda i, j=j: (i, j))

    small = lambda r: pl.BlockSpec((r, D_CONV), lambda i: (0, 0))
    return pl.pallas_call(
        body, grid=(s_len // ts,),
        in_specs=[col(0), col(1), col(2), small(3), small(1)],
        out_specs=col(0),
        out_shape=jax.ShapeDtypeStruct((s_len, 2 * D_CONV), BF16),
        scratch_shapes=[pltpu.VMEM((SUBLANES, D_CONV), F32)],
        compiler_params=_params("arbitrary"), name="conv_fwd",
    )(z, z, z, w_t, gain)


def _conv_bwd(dy, z, w_t, gain, *, ts=512):
    s_len = z.shape[0]
    n = s_len // ts
    n_grp = D_CONV // GROUP
    halo_blocks = ts // SUBLANES

    def body(dy_ref, cb_ref, cc_ref, ch_ref, hcc_ref, hch_ref, w_ref, g_ref,
             dz_ref, dw_ref, dg_ref, carry_ref, accw_ref, accg_ref):
        i = pl.program_id(0)
        first_tile = (n - 1 - i) == 0

        @pl.when(i == 0)
        def _():
            carry_ref[...] = jnp.zeros_like(carry_ref)
            accw_ref[...] = jnp.zeros_like(accw_ref)
            accg_ref[...] = jnp.zeros_like(accg_ref)

        row = lax.broadcasted_iota(jnp.int32, (ts, GROUP), 0)
        keep = jnp.where(first_tile, 0.0, 1.0)
        for g in range(n_grp):
            sl = slice(g * GROUP, (g + 1) * GROUP)
            cc = cc_ref[:, sl]
            ch = ch_ref[:, sl]
            cb = cb_ref[:, sl]
            uu = cc * ch
            p2 = hcc_ref[6:7, sl] * hch_ref[6:7, sl] * keep
            p1 = hcc_ref[7:8, sl] * hch_ref[7:8, sl] * keep
            u1 = _shift_down(uu, 1, [p1], row)
            u2 = _shift_down(uu, 2, [p2, p1], row)
            w0, w1, w2 = w_ref[0:1, sl], w_ref[1:2, sl], w_ref[2:3, sl]
            conv = w0 * u2 + w1 * u1 + w2 * uu
            y = cb * conv
            rg = lax.rsqrt(jnp.mean(y * y, axis=-1, keepdims=True) + EPS)
            yh = y * rg
            dyv = dy_ref[:, sl]
            accg_ref[:, sl] += _rowsum8(dyv * yh)
            dyn = dyv * g_ref[:, sl]
            dpre = rg * (dyn - yh * jnp.mean(dyn * yh, axis=-1, keepdims=True))
            dz_ref[:, sl] = dpre * conv
            dconv = dpre * cb
            accw_ref[0:8, sl] += _rowsum8(dconv * u2)
            accw_ref[8:16, sl] += _rowsum8(dconv * u1)
            accw_ref[16:24, sl] += _rowsum8(dconv * uu)
            n0 = carry_ref[0:1, sl]
            n1 = carry_ref[1:2, sl]
            d1 = _shift_up(dconv, 1, [n0], row)
            d2 = _shift_up(dconv, 2, [n0, n1], row)
            duu = w2 * dconv + w1 * d1 + w0 * d2
            carry_ref[:, sl] = dconv[0:SUBLANES, :]
            dz_ref[:, D_CONV + g * GROUP:D_CONV + (g + 1) * GROUP] = duu * ch
            dz_ref[:, 2 * D_CONV + g * GROUP:2 * D_CONV + (g + 1) * GROUP] = duu * cc

        @pl.when(i == n - 1)
        def _():
            for k in range(3):
                dw_ref[k:k + 1, :] = jnp.sum(accw_ref[8 * k:8 * k + 8, :], axis=0, keepdims=True)
            dg_ref[...] = jnp.sum(accg_ref[...], axis=0, keepdims=True)

    def col(j):
        return pl.BlockSpec((ts, D_CONV), lambda i, j=j: (n - 1 - i, j))

    def halo(j):
        return pl.BlockSpec((SUBLANES, D_CONV),
                            lambda i, j=j: (jnp.maximum((n - 1 - i) * halo_blocks - 1, 0), j))

    small = lambda r: pl.BlockSpec((r, D_CONV), lambda i: (0, 0))
    return pl.pallas_call(
        body, grid=(n,),
        in_specs=[col(0), col(0), col(1), col(2), halo(1), halo(2), small(3), small(1)],
        out_specs=[pl.BlockSpec((ts, 3 * D_CONV), lambda i: (n - 1 - i, 0)), small(3), small(1)],
        out_shape=[jax.ShapeDtypeStruct((s_len, 6 * D_CONV), F32),
                   jax.ShapeDtypeStruct((3, D_CONV), F32), jax.ShapeDtypeStruct((1, D_CONV), F32)],
        scratch_shapes=[pltpu.VMEM((SUBLANES, D_CONV), F32), pltpu.VMEM((24, D_CONV), F32),
                        pltpu.VMEM((SUBLANES, D_CONV), F32)],
        compiler_params=_params("arbitrary"), name="conv_bwd",
    )(dy, z, z, z, z, z, w_t, gain)


def _split3(v):
    hi = v.astype(BF16)
    r1 = v - hi.astype(F32)
    mid = r1.astype(BF16)
    lo = (r1 - mid.astype(F32)).astype(BF16)
    return jnp.concatenate([hi, mid, lo], axis=1)


def _tri_sum(tri, v):
    w = v.shape[1]
    dd = jnp.dot(tri, _split3(v), preferred_element_type=F32)
    return dd[:, :w] + dd[:, w:2 * w] + dd[:, 2 * w:]


def _strict_tri(upper):
    r = lax.broadcasted_iota(jnp.int32, (CHUNK, CHUNK), 0)
    c = lax.broadcasted_iota(jnp.int32, (CHUNK, CHUNK), 1)
    return jnp.where((c > r) if upper else (c < r), 1.0, 0.0).astype(BF16)


def _sigmoid(v):
    return 1.0 / (1.0 + jnp.exp(-v))


def _gla_fwd(z, alow, wgu, bg, gg, y_in, *, ts=512):
    s_len = z.shape[0]
    nch = ts // CHUNK
    scale = DK ** -0.5

    def body(q_ref, k_ref, v_ref, og_ref, al_ref, wgu_ref, bg_ref, gg_ref, yin_ref,
             y_ref, o_ref, la_ref, st_ref, state_ref):
        del yin_ref
        i = pl.program_id(0)

        @pl.when(i == 0)
        def _():
            state_ref[...] = jnp.zeros_like(state_ref)

        pre = jnp.dot(al_ref[...].astype(BF16), wgu_ref[...], preferred_element_type=F32) + bg_ref[...]
        la_ref[...] = (jnp.minimum(pre, 0.0) - jnp.log(1.0 + jnp.exp(-jnp.abs(pre)))) * (1.0 / 16.0)
        upper = _strict_tri(True)
        ggv = gg_ref[...]

        def chunk(cl, carry):
            rows = pl.ds(pl.multiple_of(cl * CHUNK, CHUNK), CHUNK)
            la_c = la_ref[rows, :]
            e_dec = jnp.exp(_tri_sum(upper, la_c))
            dec = jnp.exp(jnp.sum(la_c, axis=0, keepdims=True))
            kd = k_ref[rows, :] * e_dec
            qs = q_ref[rows, :] * scale
            for h in range(HEADS):
                ks = slice(h * DK, (h + 1) * DK)
                vs = slice(h * DV, (h + 1) * DV)
                kv_t = lax.dot_general(v_ref[rows, vs].astype(BF16), kd[:, ks].astype(BF16),
                                       (((0,), (0,)), ((), ())), preferred_element_type=F32)
                st = state_ref[h] * dec[:, ks] + kv_t
                state_ref[h] = st
                st_ref[cl, h] = st
                o_h = lax.dot_general(qs[:, ks].astype(BF16), st.astype(BF16),
                                      (((1,), (1,)), ((), ())), preferred_element_type=F32)
                o_ref[rows, vs] = o_h
                ro = lax.rsqrt(jnp.mean(o_h * o_h, axis=-1, keepdims=True) + EPS)
                og_h = og_ref[rows, vs]
                y_ref[rows, vs] = (o_h * ro * ggv * (og_h * _sigmoid(og_h))).astype(BF16)
            return carry

        lax.fori_loop(0, nch, chunk, 0)

    def zcol(width, j):
        return pl.BlockSpec((ts, width), lambda i, j=j: (i, j))

    full = lambda shape: pl.BlockSpec(shape, lambda i: tuple(0 for _ in shape))
    return pl.pallas_call(
        body, grid=(s_len // ts,),
        in_specs=[zcol(512, 6), zcol(512, 7), zcol(1024, 4), zcol(1024, 5), zcol(LANES, 0),
                  full((LANES, 512)), full((1, 512)), full((1, DV)),
                  pl.BlockSpec(memory_space=pl.ANY)],
        out_specs=[zcol(1024, 1), zcol(1024, 0), zcol(512, 0),
                   pl.BlockSpec((nch, HEADS, DV, DK), lambda i: (i, 0, 0, 0))],
        out_shape=[jax.ShapeDtypeStruct((s_len, 2048), BF16), jax.ShapeDtypeStruct((s_len, 1024), F32),
                   jax.ShapeDtypeStruct((s_len, 512), F32),
                   jax.ShapeDtypeStruct((s_len // CHUNK, HEADS, DV, DK), F32)],
        scratch_shapes=[pltpu.VMEM((HEADS, DV, DK), F32)],
        input_output_aliases={8: 0},
        compiler_params=_params("arbitrary"), name="gla_fwd",
    )(z, z, z, z, alow, wgu, bg, gg, y_in)


def _gla_bwd(dy, z, o, la, st, alow, wgu, gg, dz_in, *, ts=512, comm=None):
    s_len = z.shape[0]
    n = s_len // ts
    nch = ts // CHUNK
    scale = DK ** -0.5

    def body(dy_ref, q_ref, k_ref, v_ref, og_ref, o_ref, la_ref, st_ref, stp_ref, al_ref, wgu_ref,
             gg_ref, dzin_ref, dz_ref, dal_ref, dwgu_ref, dbg_ref, dgg_ref,
             gt_ref, decn_ref, accw_ref, accb_ref, accg_ref, dla_ref, tt_ref, dbe_ref):
        del dzin_ref
        i = pl.program_id(0)
        first_tile = (n - 1 - i) == 0

        @pl.when(i == 0)
        def _():
            gt_ref[...] = jnp.zeros_like(gt_ref)
            decn_ref[...] = jnp.ones_like(decn_ref)
            accw_ref[...] = jnp.zeros_like(accw_ref)
            accb_ref[...] = jnp.zeros_like(accb_ref)
            accg_ref[...] = jnp.zeros_like(accg_ref)

        upper = _strict_tri(True)
        lower = _strict_tri(False)
        ggv = gg_ref[...]
        keep = jnp.where(first_tile, 0.0, 1.0)

        def chunk(jrev, carry):
            cl = nch - 1 - jrev
            rows = pl.ds(pl.multiple_of(cl * CHUNK, CHUNK), CHUNK)
            la_c = la_ref[rows, :]
            e_dec = jnp.exp(_tri_sum(upper, la_c))
            dec = jnp.exp(jnp.sum(la_c, axis=0, keepdims=True))
            kd = k_ref[rows, :] * e_dec
            qs = q_ref[rows, :] * scale
            decn = decn_ref[0:1, :]
            has_prev = jnp.where(cl > 0, 1.0, 0.0)
            prev_idx = jnp.maximum(cl - 1, 0)
            for h in range(HEADS):
                ks = slice(h * DK, (h + 1) * DK)
                vs = slice(h * DV, (h + 1) * DV)
                o_h = o_ref[rows, vs]
                og_h = og_ref[rows, vs]
                dy_h = dy_ref[rows, vs]
                ro = lax.rsqrt(jnp.mean(o_h * o_h, axis=-1, keepdims=True) + EPS)
                oh = o_h * ro
                sig = _sigmoid(og_h)
                sil = og_h * sig
                accg_ref[...] += _rowsum8(dy_h * oh * sil)
                dz_ref[rows, 2048 + h * DV:2048 + (h + 1) * DV] = (
                    dy_h * oh * ggv * sig * (1.0 + og_h * (1.0 - sig)))
                don = dy_h * ggv * sil
                do = ro * (don - oh * jnp.mean(don * oh, axis=-1, keepdims=True))
                dob = do.astype(BF16)
                s_c = st_ref[cl, h]
                dqs = jnp.dot(dob, s_c.astype(BF16), preferred_element_type=F32)
                dz_ref[rows, ks] = dqs * scale
                gt = gt_ref[h] * decn[:, ks] + lax.dot_general(
                    dob, qs[:, ks].astype(BF16), (((0,), (0,)), ((), ())), preferred_element_type=F32)
                gt_ref[h] = gt
                gb = gt.astype(BF16)
                kd_h = kd[:, ks]
                dkd = jnp.dot(v_ref[rows, vs].astype(BF16), gb, preferred_element_type=F32)
                dz_ref[rows, 1024 + h * DV:1024 + (h + 1) * DV] = lax.dot_general(
                    kd_h.astype(BF16), gb, (((1,), (1,)), ((), ())), preferred_element_type=F32)
                s_prev = has_prev * st_ref[prev_idx, h] + (1.0 - has_prev) * keep * stp_ref[0, h]
                ddec = jnp.sum(gt * s_prev, axis=0, keepdims=True)
                dz_ref[rows, 512 + h * DK:512 + (h + 1) * DK] = dkd * e_dec[:, ks]
                tt_ref[:, ks] = dkd * kd_h
                dbe_ref[0:1, ks] = ddec * dec[:, ks]
            dla_ref[rows, :] = _tri_sum(lower, tt_ref[...]) + dbe_ref[0:1, :]
            decn_ref[0:1, :] = dec
            return carry

        lax.fori_loop(0, nch, chunk, 0)

        dpre = dla_ref[...] * (1.0 / 16.0) * (1.0 - jnp.exp(16.0 * la_ref[...]))
        accb_ref[...] += _rowsum8(dpre)
        dpb = dpre.astype(BF16)
        accw_ref[...] += lax.dot_general(al_ref[...].astype(BF16), dpb, (((0,), (0,)), ((), ())),
                                         preferred_element_type=F32)
        dal_ref[...] = lax.dot_general(dpb, wgu_ref[...], (((1,), (1,)), ((), ())),
                                       preferred_element_type=F32)

        @pl.when(i == n - 1)
        def _():
            dwgu_ref[...] = accw_ref[...]
            dbg_ref[...] = jnp.sum(accb_ref[...], axis=0, keepdims=True)
            dgg_ref[...] = jnp.sum(accg_ref[...], axis=0, keepdims=True)

    def zcol(width, j):
        return pl.BlockSpec((ts, width), lambda i, j=j: (n - 1 - i, j))

    full = lambda shape: pl.BlockSpec(shape, lambda i: tuple(0 for _ in shape))
    return _call(
        body, grid=(n,),
        in_specs=[zcol(1024, 1), zcol(512, 6), zcol(512, 7), zcol(1024, 4), zcol(1024, 5),
                  zcol(1024, 0), zcol(512, 0),
                  pl.BlockSpec((nch, HEADS, DV, DK), lambda i: (n - 1 - i, 0, 0, 0)),
                  pl.BlockSpec((1, HEADS, DV, DK),
                               lambda i: (jnp.maximum((n - 1 - i) * nch - 1, 0), 0, 0, 0)),
                  zcol(LANES, 0), full((LANES, 512)), full((1, DV)),
                  pl.BlockSpec(memory_space=pl.ANY)],
        out_specs=[zcol(3072, 1), zcol(LANES, 0), full((LANES, 512)), full((1, 512)), full((1, DV))],
        out_shape=[jax.ShapeDtypeStruct((s_len, 6144), F32), jax.ShapeDtypeStruct((s_len, LANES), F32),
                   jax.ShapeDtypeStruct((LANES, 512), F32), jax.ShapeDtypeStruct((1, 512), F32),
                   jax.ShapeDtypeStruct((1, DV), F32)],
        scratch_shapes=[pltpu.VMEM((HEADS, DV, DK), F32), pltpu.VMEM((SUBLANES, 512), F32),
                        pltpu.VMEM((LANES, 512), F32), pltpu.VMEM((SUBLANES, 512), F32),
                        pltpu.VMEM((SUBLANES, DV), F32), pltpu.VMEM((ts, 512), F32),
                        pltpu.VMEM((CHUNK, 512), F32), pltpu.VMEM((SUBLANES, 512), F32)],
        aliases={12: 0},
        operands=(dy, z, z, z, z, o, la, st, st, alow, wgu, gg, dz_in), name="gla_bwd", comm=comm)


def _adamw(w, g, m, v, *, name):
    rows, cols = w.shape
    tr = rows
    for cand in (256, 128, 64, 32, 16, 8):
        if rows % cand == 0 and rows > cand:
            tr = cand
            break

    def body(w_ref, g_ref, m_ref, v_ref, go_ref, d_ref, nm_ref, nv_ref):
        gv = g_ref[...]
        go_ref[...] = gv
        m2 = ADAM_B1 * m_ref[...] + (1.0 - ADAM_B1) * gv
        v2 = ADAM_B2 * v_ref[...] + (1.0 - ADAM_B2) * jnp.square(gv)
        m_hat = m2 / (1.0 - ADAM_B1 ** ADAM_STEP)
        v_hat = v2 / (1.0 - ADAM_B2 ** ADAM_STEP)
        d_ref[...] = -ADAM_LR * (m_hat / (jnp.sqrt(v_hat) + ADAM_EPS) + ADAM_WD * w_ref[...])
        nm_ref[...] = m2
        nv_ref[...] = v2

    blk = pl.BlockSpec((tr, cols), lambda i: (i, 0))
    shp = jax.ShapeDtypeStruct((rows, cols), F32)
    return pl.pallas_call(
        body, grid=(rows // tr,), in_specs=[blk] * 4, out_specs=[blk] * 4, out_shape=[shp] * 4,
        compiler_params=_params("parallel"), name=name,
    )(w, g, m, v)


def _my_place():
    return lax.axis_index("x"), lax.axis_index("y"), lax.axis_index("c")


def _flip(v, bit):
    return 1 - v if bit else v


def _allgather_small(buf, *, reduce, name):
    rows = buf.shape[0]

    def body(in_ref, out_ref, gat_ref, send_sems, recv_sems):
        x, y, c = _my_place()
        me = 4 * x + 2 * y + c
        gat_ref[me] = in_ref[...]
        copies = []
        for m in range(1, N_DEV):
            peer = (_flip(x, m & 4), _flip(y, m & 2), _flip(c, m & 1))
            cp = pltpu.make_async_remote_copy(
                src_ref=in_ref, dst_ref=gat_ref.at[me],
                send_sem=send_sems.at[m - 1], recv_sem=recv_sems.at[m - 1],
                device_id=peer, device_id_type=MESH)
            cp.start()
            copies.append(cp)
        for m in range(1, N_DEV):
            px, py, pc = _flip(x, m & 4), _flip(y, m & 2), _flip(c, m & 1)
            src_slot = gat_ref.at[4 * px + 2 * py + pc]
            pltpu.make_async_remote_copy(
                src_ref=src_slot, dst_ref=src_slot,
                send_sem=send_sems.at[m - 1], recv_sem=recv_sems.at[m - 1],
                device_id=(px, py, pc), device_id_type=MESH).wait_recv()
        for cp in copies:
            cp.wait_send()
        if reduce:
            tot = gat_ref[0]
            for d in range(1, N_DEV):
                tot = tot + gat_ref[d]
            out_ref[...] = tot
        else:
            out_ref[...] = gat_ref[...]

    out_shape = (rows, LANES) if reduce else (N_DEV, rows, LANES)
    return pl.pallas_call(
        body,
        in_specs=[pl.BlockSpec(memory_space=pltpu.VMEM)],
        out_specs=pl.BlockSpec(memory_space=pltpu.VMEM),
        out_shape=jax.ShapeDtypeStruct(out_shape, F32),
        scratch_shapes=[pltpu.VMEM((N_DEV, rows, LANES), F32),
                        pltpu.SemaphoreType.DMA((N_DEV - 1,)), pltpu.SemaphoreType.DMA((N_DEV - 1,))],
        compiler_params=pltpu.CompilerParams(has_side_effects=True),
        name=name,
    )(buf)


def _cast_into(shard, chip_core, *, name):
    _, rows, cols = shard.shape
    tr = 256

    def body(cc_ref, s_ref, o_ref):
        del cc_ref
        o_ref[...] = s_ref[...].astype(BF16)

    grid_spec = pltpu.PrefetchScalarGridSpec(
        num_scalar_prefetch=1, grid=(2, rows // tr),
        in_specs=[pl.BlockSpec((None, tr, cols), lambda h, r, cc: (h, r, 0))],
        out_specs=pl.BlockSpec((None, None, tr, cols), lambda h, r, cc: (cc[0], h, r, 0)))
    return pl.pallas_call(
        body, grid_spec=grid_spec, out_shape=jax.ShapeDtypeStruct((N_CHIPS, 2, rows, cols), BF16),
        compiler_params=_params("arbitrary", "arbitrary"), name=name,
    )(chip_core, shard)


def _remote(src, dst, send_sems, recv_sems, k, device):
    return pltpu.make_async_remote_copy(src_ref=src, dst_ref=dst, send_sem=send_sems.at[k],
                                        recv_sem=recv_sems.at[k], device_id=device, device_id_type=MESH)


def _gather_comm(bufs):
    n_w, n_m = len(bufs), len(CHIP_MASKS)

    def first(c_ins, c_outs, ss, rs):
        x, y, c = _my_place()
        chip = 2 * x + y
        for w in range(n_w):
            mine = c_outs[w].at[chip, c]
            for mi, (mx, my) in enumerate(CHIP_MASKS):
                _remote(mine, mine, ss, rs, w * n_m + mi, (_flip(x, mx), _flip(y, my), c)).start()

    def last(c_ins, c_outs, ss, rs):
        x, y, c = _my_place()
        chip = 2 * x + y
        for w in range(n_w):
            for mi, (mx, my) in enumerate(CHIP_MASKS):
                k = w * n_m + mi
                px, py = _flip(x, mx), _flip(y, my)
                landed = c_outs[w].at[2 * px + py, c]
                _remote(landed, landed, ss, rs, k, (px, py, c)).wait_recv()
                _remote(landed, landed, ss, rs, n_w * n_m + k, (x, y, 1 - c)).start()
        for w in range(n_w):
            mine = c_outs[w].at[chip, c]
            for mi, (mx, my) in enumerate(CHIP_MASKS):
                k = w * n_m + mi
                px, py = _flip(x, mx), _flip(y, my)
                theirs = c_outs[w].at[2 * px + py, 1 - c]
                _remote(theirs, theirs, ss, rs, n_w * n_m + k, (x, y, 1 - c)).wait_recv()
                _remote(mine, mine, ss, rs, k, (px, py, c)).wait_send()
                _remote(mine, mine, ss, rs, n_w * n_m + k, (x, y, 1 - c)).wait_send()

    return _Comm(ins=bufs, outs=[jax.ShapeDtypeStruct(b.shape, b.dtype) for b in bufs],
                 aliases={w: w for w in range(n_w)}, n_sems=2 * n_w * n_m, first=first, last=last)


def _swap_comm(grads):
    n_w = len(grads)

    def copy(c_ins, c_outs, ss, rs, w):
        x, y, c = _my_place()
        return _remote(c_ins[w].at[:, 1 - c], c_outs[w], ss, rs, w, (x, y, 1 - c))

    def first(c_ins, c_outs, ss, rs):
        for w in range(n_w):
            copy(c_ins, c_outs, ss, rs, w).start()

    def last(c_ins, c_outs, ss, rs):
        for w in range(n_w):
            copy(c_ins, c_outs, ss, rs, w).wait()

    return _Comm(ins=grads,
                 outs=[jax.ShapeDtypeStruct((g.shape[0],) + g.shape[2:], g.dtype) for g in grads],
                 aliases={}, n_sems=n_w, first=first, last=last)


def _add_own_half(g, other, core, *, name):
    n_chip, _, rows, cols = g.shape
    tr = 256

    def body(core_ref, g_ref, o_ref, out_ref):
        del core_ref
        out_ref[...] = (g_ref[...].astype(F32) + o_ref[...].astype(F32)).astype(BF16)

    grid_spec = pltpu.PrefetchScalarGridSpec(
        num_scalar_prefetch=1, grid=(n_chip, rows // tr),
        in_specs=[pl.BlockSpec((None, None, tr, cols), lambda j, r, core_ref: (j, core_ref[1], r, 0)),
                  pl.BlockSpec((None, tr, cols), lambda j, r, core_ref: (j, r, 0))],
        out_specs=pl.BlockSpec((None, tr, cols), lambda j, r, core_ref: (j, r, 0)))
    return pl.pallas_call(
        body, grid_spec=grid_spec, out_shape=jax.ShapeDtypeStruct((n_chip, rows, cols), BF16),
        compiler_params=_params("parallel", "parallel"), name=name,
    )(core, g, other)


def _exchange_comm(pieces):
    n_w, n_m = len(pieces), len(CHIP_MASKS)

    def copies(c_ins, c_outs, ss, rs):
        x, y, c = _my_place()
        chip = 2 * x + y
        for w in range(n_w):
            for mi, (mx, my) in enumerate(CHIP_MASKS):
                px, py = _flip(x, mx), _flip(y, my)
                send = _remote(c_ins[w].at[2 * px + py], c_outs[w].at[chip], ss, rs, w * n_m + mi, (px, py, c))
                landed = c_outs[w].at[2 * px + py]
                yield send, _remote(landed, landed, ss, rs, w * n_m + mi, (px, py, c))

    def first(c_ins, c_outs, ss, rs):
        for send, _ in copies(c_ins, c_outs, ss, rs):
            send.start()

    def last(c_ins, c_outs, ss, rs):
        for send, arrival in copies(c_ins, c_outs, ss, rs):
            arrival.wait_recv()
            send.wait_send()

    return _Comm(ins=pieces, outs=[jax.ShapeDtypeStruct(p.shape, p.dtype) for p in pieces],
                 aliases={}, n_sems=n_w * n_m, first=first, last=last)


def _sum_chips(own, landed, chip_core, *, name):
    n_chip, rows, cols = own.shape
    tr = 256

    def body(cc_ref, o_ref, l1_ref, l2_ref, l3_ref, out_ref):
        del cc_ref
        out_ref[...] = ((o_ref[...].astype(F32) + l1_ref[...].astype(F32))
                        + l2_ref[...].astype(F32)) + l3_ref[...].astype(F32)

    def slot(k):
        return pl.BlockSpec((None, tr, cols), lambda r, cc, k=k: ((cc[0] + k) % n_chip, r, 0))

    grid_spec = pltpu.PrefetchScalarGridSpec(
        num_scalar_prefetch=1, grid=(rows // tr,),
        in_specs=[slot(0), slot(1), slot(2), slot(3)],
        out_specs=pl.BlockSpec((None, tr, cols), lambda r, cc: (cc[1], r, 0)))
    return pl.pallas_call(
        body, grid_spec=grid_spec, out_shape=jax.ShapeDtypeStruct((2, rows, cols), F32),
        compiler_params=_params("arbitrary"), name=name,
    )(chip_core, own, landed, landed, landed)


def _join_comm(halves):
    n_w = len(halves)

    def first(c_ins, c_outs, ss, rs):
        x, y, c = _my_place()
        for w in range(n_w):
            mine = c_outs[w].at[c]
            _remote(mine, mine, ss, rs, w, (x, y, 1 - c)).start()

    def last(c_ins, c_outs, ss, rs):
        x, y, c = _my_place()
        for w in range(n_w):
            theirs = c_outs[w].at[1 - c]
            _remote(theirs, theirs, ss, rs, w, (x, y, 1 - c)).wait()

    return _Comm(ins=halves, outs=[jax.ShapeDtypeStruct(h.shape, h.dtype) for h in halves],
                 aliases={w: w for w in range(n_w)}, n_sems=n_w, first=first, last=last)


def _pack(pieces):
    flat, spans, off = [], [], 0
    for p in pieces:
        v = p.reshape(-1).astype(F32)
        pad = (-v.shape[0]) % LANES
        if pad:
            v = jnp.concatenate([v, jnp.zeros((pad,), F32)])
        spans.append((off, p.size))
        off += v.shape[0]
        flat.append(v)
    tail = (-off) % (SUBLANES * LANES)
    if tail:
        flat.append(jnp.zeros((tail,), F32))
    return jnp.concatenate(flat).reshape(-1, LANES), spans


def _unpack(buf, span, shape):
    off, size = span
    return buf.reshape(-1)[off:off + size].reshape(shape)


def kernel(x, norm1_g, w_in, w_gate_up, b_gate, conv_w, conv_norm_g, gla_norm_g, w_out, norm2_g, w_ff1, w_ff2, norm_f_g, loss_target, m_norm1_g, m_w_in, m_w_gate_up, m_b_gate, m_conv_w, m_conv_norm_g, m_gla_norm_g, m_w_out, m_norm2_g, m_w_ff1, m_w_ff2, m_norm_f_g, v_norm1_g, v_w_in, v_w_gate_up, v_b_gate, v_conv_w, v_conv_norm_g, v_gla_norm_g, v_w_out, v_norm2_g, v_w_ff1, v_w_ff2, v_norm_f_g):
    xs = x[0]
    target = loss_target[0]
    s_len, d = xs.shape
    d_in = w_in.shape[2] * N_CHIPS
    d_main = d_in - GATE_RANK
    d_ff = w_ff1.shape[2] * N_CHIPS
    cx, cy, cc = _my_place()
    chip = 2 * cx + cy
    chip_core = jnp.stack([chip, cc]).astype(jnp.int32)
    n_ff = d_ff // N_CHIPS
    norm_f = norm_f_g.reshape(1, d)

    def shard_buf(w2d, name):
        return _cast_into(w2d.reshape(2, w2d.shape[0] // 2, w2d.shape[1]), chip_core, name=name)

    wi_buf = shard_buf(w_in[0], "cast_w_in")
    wo_buf = shard_buf(w_out[0], "cast_w_out")
    w1_buf = shard_buf(w_ff1[0], "cast_w_ff1")
    w2_buf = shard_buf(w_ff2[0], "cast_w_ff2")

    small_w, spans_w = _pack([w_gate_up[0], conv_w[0]])
    small_all = _allgather_small(small_w, reduce=False, name="gather_small_weights")
    chips_first = [small_all[2 * j] for j in range(N_CHIPS)]
    wgu_full = jnp.concatenate(
        [_unpack(b, spans_w[0], w_gate_up.shape[1:]) for b in chips_first], axis=1)
    convw_full = jnp.concatenate(
        [_unpack(b, spans_w[1], conv_w.shape[1:]) for b in chips_first], axis=0)
    wgu_pad = jnp.concatenate(
        [wgu_full, jnp.zeros((LANES - GATE_RANK, wgu_full.shape[1]), F32)], axis=0).astype(BF16)
    convw_t = convw_full.T

    u, wi_buf = _rms_fwd(xs, norm1_g, name="norm1_fwd", comm=_gather_comm([wi_buf]))
    wi_all = wi_buf.reshape(N_CHIPS, d, d_in // N_CHIPS)
    wi_full = jnp.concatenate([wi_all[j] for j in range(N_CHIPS)], axis=1)
    w_main = wi_full[:, :d_main]
    w_alow = jnp.concatenate([wi_full[:, d_main:], jnp.zeros((d, LANES - GATE_RANK), BF16)], axis=1)
    z, wo_buf, w1_buf = _matmul(u, w_main, tm=1024, tn=1024, tk=d, out_dtypes=[F32], name="in_proj",
                                comm=_gather_comm([wo_buf, w1_buf]))
    wo_full = wo_buf.reshape(d, d)
    w1_cm = w1_buf.reshape(N_CHIPS, d, n_ff)
    (alow,) = _matmul(u, w_alow, tm=1024, tn=LANES, tk=d, out_dtypes=[F32], name="in_proj_gate")
    y0 = _conv_fwd(z, convw_t, conv_norm_g)
    y, o, la, st = _gla_fwd(z, alow, wgu_pad, b_gate, gla_norm_g, y0)
    (x2,) = _matmul(y, wo_full, tm=1024, tn=1024, tk=d, out_dtypes=[F32], extras=(xs,),
                    epilogue=_add_epilogue, name="out_proj")
    (h,) = _rms_fwd(x2, norm2_g, name="norm2_fwd")
    a, p, w2_buf = _matmul(
        h, w1_cm, tm=1024, tn=1024, tk=d, out_dtypes=[F32, BF16], n_dim=d_ff,
        b_spec=pl.BlockSpec((None, d, 1024), lambda i, j, k: (j // 2, 0, j % 2)),
        epilogue=lambda acc: (acc, jnp.square(jnp.maximum(acc, 0.0))), name="ff1",
        comm=_gather_comm([w2_buf]))
    w2_full = w2_buf.reshape(d_ff, d)
    (x3,) = _matmul(p, w2_full, tm=1024, tn=1024, tk=2048, out_dtypes=[F32], extras=(x2,),
                    epilogue=_add_epilogue, name="ff2")
    dx3, dx3b, g_normf, loss_part = _loss_head(x3, norm_f, target)

    (da,) = _matmul(dx3b, w2_full, tb=True, tm=1024, tn=1024, tk=d, out_dtypes=[BF16], extras=(a,),
                    epilogue=lambda acc, av: (acc * (2.0 * jnp.maximum(av, 0.0)),), name="ff2_dx")
    (g_w2,) = _matmul(p, dx3b, ta=True, tm=2048, tn=2048, tk=512, out_dtypes=[BF16], name="ff2_dw")
    (g_w1,) = _matmul(
        h, da, ta=True, tm=d, tn=n_ff, tk=512, out_dtypes=[BF16], name="ff1_dw",
        out_shapes=[jax.ShapeDtypeStruct((N_CHIPS, d, n_ff), BF16)],
        out_specs=[pl.BlockSpec((None, d, n_ff), lambda i, j, k: (j, 0, 0))])
    g_w1 = g_w1.reshape(N_CHIPS, 2, d // 2, n_ff)
    g_w2 = g_w2.reshape(N_CHIPS, 2, d_ff // (2 * N_CHIPS), d)
    dh, t_w1, t_w2 = _matmul(
        da, w1_cm, tb=True, tm=1024, tn=1024, tk=2048, out_dtypes=[F32], n_dim=d,
        b_spec=pl.BlockSpec((None, 1024, 2048), lambda i, j, k: (k, j, 0)), name="ff1_dx",
        comm=_swap_comm([g_w1, g_w2]))
    p_w1 = _add_own_half(g_w1, t_w1, chip_core, name="pre_reduce_w_ff1")
    p_w2 = _add_own_half(g_w2, t_w2, chip_core, name="pre_reduce_w_ff2")
    dx2, g_norm2 = _rms_bwd(dh, x2, norm2_g, dx3, name="norm2_bwd")
    (dy,) = _matmul(dx2, wo_full, tb=True, tm=1024, tn=1024, tk=d, out_dtypes=[F32], name="out_proj_dx")
    (g_wo,) = _matmul(y, dx2, ta=True, tm=d, tn=d, tk=512, out_dtypes=[BF16], name="out_proj_dw")
    dz0, g_convw_t, g_convg = _conv_bwd(dy, z, convw_t, conv_norm_g)
    dz, dalow, g_wgu_pad, g_bg, g_gg, l_w2 = _gla_bwd(dy, z, o, la, st, alow, wgu_pad, gla_norm_g, dz0,
                                                      comm=_exchange_comm([p_w2]))
    m_w2 = _sum_chips(p_w2, l_w2, chip_core, name="reduce_w_ff2")
    (g_wi_gate,) = _matmul(u, dalow, ta=True, tm=d, tn=LANES, tk=512, out_dtypes=[BF16],
                           name="in_proj_gate_dw")
    g_wi_main, l_w1 = _matmul(u, dz, ta=True, tm=d, tn=1024, tk=512, out_dtypes=[BF16], name="in_proj_dw",
                              comm=_exchange_comm([p_w1]))
    m_w1 = _sum_chips(p_w1, l_w1, chip_core, name="reduce_w_ff1")
    n_sh = d_in // N_CHIPS
    g_wi = jnp.concatenate([g_wi_main, g_wi_gate[:, :GATE_RANK]], axis=1)
    g_wi = g_wi.reshape(d, N_CHIPS, n_sh).transpose(1, 0, 2)
    g_wi = g_wi.reshape(N_CHIPS, 2, d // 2, n_sh)
    g_wo = g_wo.reshape(N_CHIPS, 2, d // (2 * N_CHIPS), d)
    du_gate, t_wi, t_wo = _matmul(dalow, w_alow, tb=True, tm=1024, tn=1024, tk=LANES, out_dtypes=[F32],
                                  name="in_proj_gate_dx", comm=_swap_comm([g_wi, g_wo]))
    p_wi = _add_own_half(g_wi, t_wi, chip_core, name="pre_reduce_w_in")
    p_wo = _add_own_half(g_wo, t_wo, chip_core, name="pre_reduce_w_out")
    du, l_wi, l_wo = _matmul(dz, w_main, tb=True, tm=1024, tn=1024, tk=2048, out_dtypes=[F32],
                             extras=(du_gate,), epilogue=_add_epilogue, name="in_proj_dx",
                             comm=_exchange_comm([p_wi, p_wo]))
    m_wi = _sum_chips(p_wi, l_wi, chip_core, name="reduce_w_in")
    m_wo = _sum_chips(p_wo, l_wo, chip_core, name="reduce_w_out")
    grad_x, g_norm1, *joined = _rms_bwd(du, xs, norm1_g, dx2, name="norm1_bwd",
                                        comm=_join_comm([m_wi, m_wo, m_w1, m_w2]))
    g_big = [j.reshape(2 * j.shape[1], j.shape[2]) for j in joined]

    small_g, spans_g = _pack([g_norm1, g_wgu_pad[:GATE_RANK], g_bg, g_convw_t, g_convg, g_gg, g_norm2,
                              g_normf, loss_part[:, :1]])
    tot = _allgather_small(small_g, reduce=True, name="reduce_small_grads")
    t_norm1 = _unpack(tot, spans_g[0], (1, d))
    t_wgu = _unpack(tot, spans_g[1], (GATE_RANK, HEADS * DK))
    t_bg = _unpack(tot, spans_g[2], (1, HEADS * DK))
    t_convw = _unpack(tot, spans_g[3], (3, D_CONV)).T
    t_convg = _unpack(tot, spans_g[4], (1, D_CONV))
    t_gg = _unpack(tot, spans_g[5], (1, DV))
    t_norm2 = _unpack(tot, spans_g[6], (1, d))
    t_normf = _unpack(tot, spans_g[7], (1, d))
    loss = _unpack(tot, spans_g[8], ())
    n_gu = w_gate_up.shape[2]
    n_cw = conv_w.shape[1]
    t_wgu = lax.dynamic_slice(t_wgu, (0, chip * n_gu), (GATE_RANK, n_gu))
    t_convw = lax.dynamic_slice(t_convw, (chip * n_cw, 0), (n_cw, 3))

    order = ["norm1_g", "w_in", "w_gate_up", "b_gate", "conv_w", "conv_norm_g", "gla_norm_g", "w_out",
             "norm2_g", "w_ff1", "w_ff2", "norm_f_g"]
    weights = dict(norm1_g=norm1_g, w_in=w_in, w_gate_up=w_gate_up, b_gate=b_gate, conv_w=conv_w,
                   conv_norm_g=conv_norm_g, gla_norm_g=gla_norm_g, w_out=w_out, norm2_g=norm2_g,
                   w_ff1=w_ff1, w_ff2=w_ff2, norm_f_g=norm_f_g)
    moms = dict(norm1_g=m_norm1_g, w_in=m_w_in, w_gate_up=m_w_gate_up, b_gate=m_b_gate, conv_w=m_conv_w,
                conv_norm_g=m_conv_norm_g, gla_norm_g=m_gla_norm_g, w_out=m_w_out, norm2_g=m_norm2_g,
                w_ff1=m_w_ff1, w_ff2=m_w_ff2, norm_f_g=m_norm_f_g)
    vels = dict(norm1_g=v_norm1_g, w_in=v_w_in, w_gate_up=v_w_gate_up, b_gate=v_b_gate, conv_w=v_conv_w,
                conv_norm_g=v_conv_norm_g, gla_norm_g=v_gla_norm_g, w_out=v_w_out, norm2_g=v_norm2_g,
                w_ff1=v_w_ff1, w_ff2=v_w_ff2, norm_f_g=v_norm_f_g)
    grads2d = dict(norm1_g=t_norm1, w_in=g_big[0], w_gate_up=t_wgu, b_gate=t_bg, conv_w=t_convw,
                   conv_norm_g=t_convg, gla_norm_g=t_gg, w_out=g_big[1], norm2_g=t_norm2,
                   w_ff1=g_big[2], w_ff2=g_big[3], norm_f_g=t_normf)
    out_g, out_d, out_m, out_v = [], [], [], []
    for nm in order:
        w = weights[nm]
        g2 = grads2d[nm]
        g_out, dlt, nm_, nv_ = _adamw(w.reshape(g2.shape), g2, moms[nm].reshape(g2.shape),
                                      vels[nm].reshape(g2.shape), name="adamw_" + nm)
        out_g.append(g_out.reshape(w.shape))
        out_d.append(dlt.reshape(w.shape))
        out_m.append(nm_.reshape(w.shape))
        out_v.append(nv_.reshape(w.shape))
    return (loss, grad_x.reshape(x.shape), *out_g, *out_d, *out_m, *out_v)
```

```python
import functools

import jax
import jax.numpy as jnp
from jax import lax
from jax.experimental import pallas as pl
from jax.experimental.pallas import tpu as pltpu

F32 = jnp.float32
BF16 = jnp.bfloat16
MESH = pl.DeviceIdType.MESH

EPS = 1e-6
CHUNK = 64
HEADS = 4
DK = 128
DV = 256
D_CONV = 1024
GROUP = 128
GATE_RANK = 16
LANES = 128
SUBLANES = 8
N_CHIPS = 4
N_DEV = 8
CHIP_MASKS = ((1, 0), (0, 1), (1, 1))

ADAM_LR = 0.001
ADAM_B1 = 0.9
ADAM_B2 = 0.999
ADAM_EPS = 1e-08
ADAM_WD = 0.01
ADAM_STEP = 10

VMEM_LIMIT = 56 * 1024 * 1024


def _params(*sem):
    return pltpu.CompilerParams(dimension_semantics=tuple(sem), vmem_limit_bytes=VMEM_LIMIT)


def _rowsum8(v):
    r, c = v.shape
    return jnp.sum(v.reshape(r // SUBLANES, SUBLANES, c), axis=0)


def _tile(rows, cols):
    for cand in (256, 128, 64, 32, 16, 8):
        if rows % cand == 0 and rows > cand:
            return cand, cols
    if rows * cols * 4 > (2 << 20) and cols % 256 == 0:
        return rows, 256
    return rows, cols


class _Comm:
    def __init__(self, ins, outs, aliases, n_sems, first, last):
        self.ins = list(ins)
        self.outs = list(outs)
        self.aliases = dict(aliases)
        self.n_sems = n_sems
        self.first = first
        self.last = last


def _call(body, *, grid, in_specs, out_specs, out_shape, operands, name, scratch_shapes=(), sem=None,
          aliases=None, comm=None):
    aliases = dict(aliases or {})
    if comm is None:
        return pl.pallas_call(
            body, grid=grid, in_specs=list(in_specs), out_specs=list(out_specs), out_shape=list(out_shape),
            scratch_shapes=list(scratch_shapes), input_output_aliases=aliases,
            compiler_params=_params(*(sem or ("arbitrary",) * len(grid))), name=name)(*operands)
    comms = list(comm) if isinstance(comm, (list, tuple)) else [comm]
    n_in, n_out, n_scr = len(in_specs), len(out_specs), len(scratch_shapes)
    c_ins_all = [a for cm in comms for a in cm.ins]
    c_outs_all = [o for cm in comms for o in cm.outs]
    n_ci, n_co = len(c_ins_all), len(c_outs_all)

    def full_body(*refs):
        ins = refs[:n_in]
        o0 = n_in + n_ci
        outs = refs[o0:o0 + n_out]
        s0 = o0 + n_out + n_co
        scr = refs[s0:s0 + n_scr]
        sems = refs[s0 + n_scr:]
        parts, i_at, o_at = [], n_in, o0 + n_out
        for q, cm in enumerate(comms):
            parts.append((cm, refs[i_at:i_at + len(cm.ins)], refs[o_at:o_at + len(cm.outs)],
                          sems[2 * q], sems[2 * q + 1]))
            i_at += len(cm.ins)
            o_at += len(cm.outs)
        ids = [pl.program_id(ax) for ax in range(len(grid))]
        is_first = functools.reduce(jnp.logical_and, [i == 0 for i in ids])
        is_last = functools.reduce(jnp.logical_and, [i == g - 1 for i, g in zip(ids, grid)])

        @pl.when(is_first)
        def _():
            for cm, c_ins, c_outs, ss, rs in parts:
                cm.first(c_ins, c_outs, ss, rs)

        body(*ins, *outs, *scr)

        @pl.when(is_last)
        def _():
            for cm, c_ins, c_outs, ss, rs in parts:
                cm.last(c_ins, c_outs, ss, rs)

    any_spec = pl.BlockSpec(memory_space=pl.ANY)
    i_at, o_at, sem_shapes = n_in, n_out, []
    for cm in comms:
        for i_in, i_out in cm.aliases.items():
            aliases[i_at + i_in] = o_at + i_out
        i_at += len(cm.ins)
        o_at += len(cm.outs)
        sem_shapes += [pltpu.SemaphoreType.DMA((cm.n_sems,)), pltpu.SemaphoreType.DMA((cm.n_sems,))]
    return pl.pallas_call(
        full_body, grid=grid,
        in_specs=list(in_specs) + [any_spec] * n_ci,
        out_specs=list(out_specs) + [any_spec] * n_co,
        out_shape=list(out_shape) + c_outs_all,
        scratch_shapes=list(scratch_shapes) + sem_shapes,
        input_output_aliases=aliases,
        compiler_params=pltpu.CompilerParams(dimension_semantics=("arbitrary",) * len(grid),
                                             vmem_limit_bytes=VMEM_LIMIT, has_side_effects=True),
        name=name)(*operands, *c_ins_all)


def _matmul(a, b, *, ta=False, tb=False, tm, tn, tk, out_dtypes, name, extras=(), epilogue=None,
            out_shapes=None, out_specs=None, b_spec=None, n_dim=None, into=None, comm=None):
    n_into = 0 if into is None else 1
    if ta:
        k_dim, m_dim = a.shape
    else:
        m_dim, k_dim = a.shape
    if n_dim is None:
        n_dim = b.shape[0] if tb else b.shape[1]
        assert (b.shape[1] if tb else b.shape[0]) == k_dim
    assert m_dim % tm == 0 and n_dim % tn == 0 and k_dim % tk == 0, (name, a.shape, b.shape)
    nk = k_dim // tk
    n_ex, n_out = len(extras), len(out_dtypes)
    dims = (((0 if ta else 1,), (1 if tb else 0,)), ((), ()))

    def body(*refs):
        a_ref, b_ref = refs[0], refs[1]
        ex_refs = refs[2:2 + n_ex]
        o_refs = refs[2 + n_ex + n_into:2 + n_ex + n_into + n_out]

        def dot():
            return lax.dot_general(a_ref[...].astype(BF16), b_ref[...].astype(BF16), dims,
                                   preferred_element_type=F32)

        def finish(acc):
            outs = epilogue(acc, *[e[...] for e in ex_refs]) if epilogue is not None else (acc,)
            for o_ref, o in zip(o_refs, outs):
                o_ref[...] = o.astype(o_ref.dtype)

        if nk == 1:
            finish(dot())
        else:
            acc_ref = refs[-1]
            k = pl.program_id(2)

            @pl.when(k == 0)
            def _():
                acc_ref[...] = dot()

            @pl.when(jnp.logical_and(k > 0, k < nk - 1))
            def _():
                acc_ref[...] += dot()

            @pl.when(k == nk - 1)
            def _():
                finish(acc_ref[...] + dot())

    a_spec = (pl.BlockSpec((tk, tm), lambda i, j, k: (k, i)) if ta
              else pl.BlockSpec((tm, tk), lambda i, j, k: (i, k)))
    if b_spec is None:
        b_spec = (pl.BlockSpec((tn, tk), lambda i, j, k: (j, k)) if tb
                  else pl.BlockSpec((tk, tn), lambda i, j, k: (k, j)))
    io_spec = pl.BlockSpec((tm, tn), lambda i, j, k: (i, j))
    if out_shapes is None:
        out_shapes = [jax.ShapeDtypeStruct((m_dim, n_dim), dt) for dt in out_dtypes]
    if out_specs is None:
        out_specs = [io_spec] * n_out
    return _call(
        body,
        grid=(m_dim // tm, n_dim // tn, nk),
        in_specs=[a_spec, b_spec] + [io_spec] * n_ex + [pl.BlockSpec(memory_space=pl.ANY)] * n_into,
        out_specs=out_specs,
        out_shape=out_shapes,
        scratch_shapes=[pltpu.VMEM((tm, tn), F32)] if nk > 1 else [],
        sem=("parallel", "parallel", "arbitrary"),
        aliases={2 + n_ex: 0} if n_into else None,
        operands=(a, b, *extras) + ((into,) if n_into else ()), name=name, comm=comm)


def _add_epilogue(acc, r):
    return (acc + r,)


def _rms_fwd(x, g, *, name, tm=512, comm=None):
    s_len, d = x.shape

    def body(x_ref, g_ref, o_ref):
        xv = x_ref[...]
        r = lax.rsqrt(jnp.mean(xv * xv, axis=-1, keepdims=True) + EPS)
        o_ref[...] = (xv * r * g_ref[...]).astype(o_ref.dtype)

    return _call(
        body, grid=(s_len // tm,),
        in_specs=[pl.BlockSpec((tm, d), lambda i: (i, 0)), pl.BlockSpec((1, d), lambda i: (0, 0))],
        out_specs=[pl.BlockSpec((tm, d), lambda i: (i, 0))],
        out_shape=[jax.ShapeDtypeStruct((s_len, d), BF16)],
        sem=("parallel",), operands=(x, g), name=name, comm=comm)


def _rms_bwd(dn, x, g, res, *, name, tm=512, comm=None):
    s_len, d = x.shape
    n = s_len // tm

    def body(dn_ref, x_ref, g_ref, res_ref, dx_ref, dg_ref, acc_ref):
        i = pl.program_id(0)
        xv = x_ref[...]
        dnv = dn_ref[...]
        r = lax.rsqrt(jnp.mean(xv * xv, axis=-1, keepdims=True) + EPS)
        xh = xv * r

        @pl.when(i == 0)
        def _():
            acc_ref[...] = jnp.zeros_like(acc_ref)

        acc_ref[...] += _rowsum8(dnv * xh)
        dxh = dnv * g_ref[...]
        dx_ref[...] = r * (dxh - xh * jnp.mean(dxh * xh, axis=-1, keepdims=True)) + res_ref[...]

        @pl.when(i == n - 1)
        def _():
            dg_ref[...] = jnp.sum(acc_ref[...], axis=0, keepdims=True)

    row = pl.BlockSpec((tm, d), lambda i: (i, 0))
    vec = pl.BlockSpec((1, d), lambda i: (0, 0))
    return _call(
        body, grid=(n,),
        in_specs=[row, row, vec, row],
        out_specs=[row, vec],
        out_shape=[jax.ShapeDtypeStruct((s_len, d), F32), jax.ShapeDtypeStruct((1, d), F32)],
        scratch_shapes=[pltpu.VMEM((SUBLANES, d), F32)],
        operands=(dn, x, g, res), name=name, comm=comm)


def _loss_head(x3, g, target, *, tm=512):
    s_len, d = x3.shape
    n = s_len // tm

    def body(x_ref, g_ref, t_ref, dx_ref, dxb_ref, dg_ref, loss_ref, accg_ref, accl_ref):
        i = pl.program_id(0)
        xv = x_ref[...]
        gv = g_ref[...]
        r = lax.rsqrt(jnp.mean(xv * xv, axis=-1, keepdims=True) + EPS)
        xh = xv * r
        err = xh * gv - t_ref[...]

        @pl.when(i == 0)
        def _():
            accg_ref[...] = jnp.zeros_like(accg_ref)
            accl_ref[...] = jnp.zeros_like(accl_ref)

        accl_ref[...] += _rowsum8(err * err)
        dn = err * (1.0 / d)
        accg_ref[...] += _rowsum8(dn * xh)
        dxh = dn * gv
        dx = r * (dxh - xh * jnp.mean(dxh * xh, axis=-1, keepdims=True))
        dx_ref[...] = dx
        dxb_ref[...] = dx.astype(BF16)

        @pl.when(i == n - 1)
        def _():
            dg_ref[...] = jnp.sum(accg_ref[...], axis=0, keepdims=True)
            tot = jnp.sum(jnp.sum(accl_ref[...], axis=0, keepdims=True), axis=1, keepdims=True)
            loss_ref[...] = jnp.broadcast_to(tot * (0.5 / d), (1, LANES))

    row = pl.BlockSpec((tm, d), lambda i: (i, 0))
    vec = pl.BlockSpec((1, d), lambda i: (0, 0))
    return pl.pallas_call(
        body, grid=(n,),
        in_specs=[row, vec, row],
        out_specs=[row, row, vec, pl.BlockSpec((1, LANES), lambda i: (0, 0))],
        out_shape=[jax.ShapeDtypeStruct((s_len, d), F32), jax.ShapeDtypeStruct((s_len, d), BF16),
                   jax.ShapeDtypeStruct((1, d), F32), jax.ShapeDtypeStruct((1, LANES), F32)],
        scratch_shapes=[pltpu.VMEM((SUBLANES, d), F32), pltpu.VMEM((SUBLANES, d), F32)],
        compiler_params=_params("arbitrary"), name="loss_head",
    )(x3, g, target)


def _shift_down(v, k, rows_before, row):
    out = pltpu.roll(v, k, axis=0)
    for j in range(k):
        out = jnp.where(row == j, rows_before[j], out)
    return out


def _shift_up(v, k, rows_after, row):
    t = v.shape[0]
    out = pltpu.roll(v, t - k, axis=0)
    for j in range(k):
        out = jnp.where(row == t - k + j, rows_after[j], out)
    return out


def _conv_fwd(z, w_t, gain, *, ts=512):
    s_len = z.shape[0]
    n_grp = D_CONV // GROUP

    def body(cb_ref, cc_ref, ch_ref, w_ref, g_ref, y_ref, carry_ref):
        i = pl.program_id(0)

        @pl.when(i == 0)
        def _():
            carry_ref[...] = jnp.zeros_like(carry_ref)

        row = lax.broadcasted_iota(jnp.int32, (ts, GROUP), 0)
        for g in range(n_grp):
            sl = slice(g * GROUP, (g + 1) * GROUP)
            uu = cc_ref[:, sl] * ch_ref[:, sl]
            p2 = carry_ref[6:7, sl]
            p1 = carry_ref[7:8, sl]
            u1 = _shift_down(uu, 1, [p1], row)
            u2 = _shift_down(uu, 2, [p2, p1], row)
            conv = w_ref[0:1, sl] * u2 + w_ref[1:2, sl] * u1 + w_ref[2:3, sl] * uu
            y = cb_ref[:, sl] * conv
            carry_ref[:, sl] = uu[ts - SUBLANES:ts, :]
            rg = lax.rsqrt(jnp.mean(y * y, axis=-1, keepdims=True) + EPS)
            y_ref[:, sl] = (y * rg * g_ref[:, sl]).astype(BF16)

    def col(j):
        return pl.BlockSpec((ts, D_CONV), lambda i, j=j: (i, j))

    small = lambda r: pl.BlockSpec((r, D_CONV), lambda i: (0, 0))
    return pl.pallas_call(
        body, grid=(s_len // ts,),
        in_specs=[col(0), col(1), col(2), small(3), small(1)],
        out_specs=col(0),
        out_shape=jax.ShapeDtypeStruct((s_len, 2 * D_CONV), BF16),
        scratch_shapes=[pltpu.VMEM((SUBLANES, D_CONV), F32)],
        compiler_params=_params("arbitrary"), name="conv_fwd",
    )(z, z, z, w_t, gain)


def _conv_bwd(dy, z, w_t, gain, *, ts=512):
    s_len = z.shape[0]
    n = s_len // ts
    n_grp = D_CONV // GROUP
    halo_blocks = ts // SUBLANES

    def body(dy_ref, cb_ref, cc_ref, ch_ref, hcc_ref, hch_ref, w_ref, g_ref,
             dz_ref, dw_ref, dg_ref, carry_ref, accw_ref, accg_ref):
        i = pl.program_id(0)
        first_tile = (n - 1 - i) == 0

        @pl.when(i == 0)
        def _():
            carry_ref[...] = jnp.zeros_like(carry_ref)
            accw_ref[...] = jnp.zeros_like(accw_ref)
            accg_ref[...] = jnp.zeros_like(accg_ref)

        row = lax.broadcasted_iota(jnp.int32, (ts, GROUP), 0)
        keep = jnp.where(first_tile, 0.0, 1.0)
        for g in range(n_grp):
            sl = slice(g * GROUP, (g + 1) * GROUP)
            cc = cc_ref[:, sl]
            ch = ch_ref[:, sl]
            cb = cb_ref[:, sl]
            uu = cc * ch
            p2 = hcc_ref[6:7, sl] * hch_ref[6:7, sl] * keep
            p1 = hcc_ref[7:8, sl] * hch_ref[7:8, sl] * keep
            u1 = _shift_down(uu, 1, [p1], row)
            u2 = _shift_down(uu, 2, [p2, p1], row)
            w0, w1, w2 = w_ref[0:1, sl], w_ref[1:2, sl], w_ref[2:3, sl]
            conv = w0 * u2 + w1 * u1 + w2 * uu
            y = cb * conv
            rg = lax.rsqrt(jnp.mean(y * y, axis=-1, keepdims=True) + EPS)
            yh = y * rg
            dyv = dy_ref[:, sl]
            accg_ref[:, sl] += _rowsum8(dyv * yh)
            dyn = dyv * g_ref[:, sl]
            dpre = rg * (dyn - yh * jnp.mean(dyn * yh, axis=-1, keepdims=True))
            dz_ref[:, sl] = dpre * conv
            dconv = dpre * cb
            accw_ref[0:8, sl] += _rowsum8(dconv * u2)
            accw_ref[8:16, sl] += _rowsum8(dconv * u1)
            accw_ref[16:24, sl] += _rowsum8(dconv * uu)
            n0 = carry_ref[0:1, sl]
            n1 = carry_ref[1:2, sl]
            d1 = _shift_up(dconv, 1, [n0], row)
            d2 = _shift_up(dconv, 2, [n0, n1], row)
            duu = w2 * dconv + w1 * d1 + w0 * d2
            carry_ref[:, sl] = dconv[0:SUBLANES, :]
            dz_ref[:, D_CONV + g * GROUP:D_CONV + (g + 1) * GROUP] = duu * ch
            dz_ref[:, 2 * D_CONV + g * GROUP:2 * D_CONV + (g + 1) * GROUP] = duu * cc

        @pl.when(i == n - 1)
        def _():
            for k in range(3):
                dw_ref[k:k + 1, :] = jnp.sum(accw_ref[8 * k:8 * k + 8, :], axis=0, keepdims=True)
            dg_ref[...] = jnp.sum(accg_ref[...], axis=0, keepdims=True)

    def col(j):
        return pl.BlockSpec((ts, D_CONV), lambda i, j=j: (n - 1 - i, j))

    def halo(j):
        return pl.BlockSpec((SUBLANES, D_CONV),
                            lambda i, j=j: (jnp.maximum((n - 1 - i) * halo_blocks - 1, 0), j))

    small = lambda r: pl.BlockSpec((r, D_CONV), lambda i: (0, 0))
    return pl.pallas_call(
        body, grid=(n,),
        in_specs=[col(0), col(0), col(1), col(2), halo(1), halo(2), small(3), small(1)],
        out_specs=[pl.BlockSpec((ts, 3 * D_CONV), lambda i: (n - 1 - i, 0)), small(3), small(1)],
        out_shape=[jax.ShapeDtypeStruct((s_len, 6 * D_CONV), F32),
                   jax.ShapeDtypeStruct((3, D_CONV), F32), jax.ShapeDtypeStruct((1, D_CONV), F32)],
        scratch_shapes=[pltpu.VMEM((SUBLANES, D_CONV), F32), pltpu.VMEM((24, D_CONV), F32),
                        pltpu.VMEM((SUBLANES, D_CONV), F32)],
        compiler_params=_params("arbitrary"), name="conv_bwd",
    )(dy, z, z, z, z, z, w_t, gain)


def _split3(v):
    hi = v.astype(BF16)
    r1 = v - hi.astype(F32)
    mid = r1.astype(BF16)
    lo = (r1 - mid.astype(F32)).astype(BF16)
    return jnp.concatenate([hi, mid, lo], axis=1)


def _tri_sum(tri, v):
    w = v.shape[1]
    dd = jnp.dot(tri, _split3(v), preferred_element_type=F32)
    return dd[:, :w] + dd[:, w:2 * w] + dd[:, 2 * w:]


def _strict_tri(upper):
    r = lax.broadcasted_iota(jnp.int32, (CHUNK, CHUNK), 0)
    c = lax.broadcasted_iota(jnp.int32, (CHUNK, CHUNK), 1)
    return jnp.where((c > r) if upper else (c < r), 1.0, 0.0).astype(BF16)


def _sigmoid(v):
    return 1.0 / (1.0 + jnp.exp(-v))


def _gla_fwd(z, alow, wgu, bg, gg, y_in, *, ts=512):
    s_len = z.shape[0]
    nch = ts // CHUNK
    scale = DK ** -0.5

    def body(q_ref, k_ref, v_ref, og_ref, al_ref, wgu_ref, bg_ref, gg_ref, yin_ref,
             y_ref, o_ref, la_ref, st_ref, state_ref):
        del yin_ref
        i = pl.program_id(0)

        @pl.when(i == 0)
        def _():
            state_ref[...] = jnp.zeros_like(state_ref)

        pre = jnp.dot(al_ref[...].astype(BF16), wgu_ref[...], preferred_element_type=F32) + bg_ref[...]
        la_ref[...] = (jnp.minimum(pre, 0.0) - jnp.log(1.0 + jnp.exp(-jnp.abs(pre)))) * (1.0 / 16.0)
        upper = _strict_tri(True)
        ggv = gg_ref[...]

        def chunk(cl, carry):
            rows = pl.ds(pl.multiple_of(cl * CHUNK, CHUNK), CHUNK)
            la_c = la_ref[rows, :]
            e_dec = jnp.exp(_tri_sum(upper, la_c))
            dec = jnp.exp(jnp.sum(la_c, axis=0, keepdims=True))
            kd = k_ref[rows, :] * e_dec
            qs = q_ref[rows, :] * scale
            for h in range(HEADS):
                ks = slice(h * DK, (h + 1) * DK)
                vs = slice(h * DV, (h + 1) * DV)
                kv_t = lax.dot_general(v_ref[rows, vs].astype(BF16), kd[:, ks].astype(BF16),
                                       (((0,), (0,)), ((), ())), preferred_element_type=F32)
                st = state_ref[h] * dec[:, ks] + kv_t
                state_ref[h] = st
                st_ref[cl, h] = st
                o_h = lax.dot_general(qs[:, ks].astype(BF16), st.astype(BF16),
                                      (((1,), (1,)), ((), ())), preferred_element_type=F32)
                o_ref[rows, vs] = o_h
                ro = lax.rsqrt(jnp.mean(o_h * o_h, axis=-1, keepdims=True) + EPS)
                og_h = og_ref[rows, vs]
                y_ref[rows, vs] = (o_h * ro * ggv * (og_h * _sigmoid(og_h))).astype(BF16)
            return carry

        lax.fori_loop(0, nch, chunk, 0)

    def zcol(width, j):
        return pl.BlockSpec((ts, width), lambda i, j=j: (i, j))

    full = lambda shape: pl.BlockSpec(shape, lambda i: tuple(0 for _ in shape))
    return pl.pallas_call(
        body, grid=(s_len // ts,),
        in_specs=[zcol(512, 6), zcol(512, 7), zcol(1024, 4), zcol(1024, 5), zcol(LANES, 0),
                  full((LANES, 512)), full((1, 512)), full((1, DV)),
                  pl.BlockSpec(memory_space=pl.ANY)],
        out_specs=[zcol(1024, 1), zcol(1024, 0), zcol(512, 0),
                   pl.BlockSpec((nch, HEADS, DV, DK), lambda i: (i, 0, 0, 0))],
        out_shape=[jax.ShapeDtypeStruct((s_len, 2048), BF16), jax.ShapeDtypeStruct((s_len, 1024), F32),
                   jax.ShapeDtypeStruct((s_len, 512), F32),
                   jax.ShapeDtypeStruct((s_len // CHUNK, HEADS, DV, DK), F32)],
        scratch_shapes=[pltpu.VMEM((HEADS, DV, DK), F32)],
        input_output_aliases={8: 0},
        compiler_params=_params("arbitrary"), name="gla_fwd",
    )(z, z, z, z, alow, wgu, bg, gg, y_in)


def _gla_bwd(dy, z, o, la, st, alow, wgu, gg, dz_in, *, ts=512, comm=None):
    s_len = z.shape[0]
    n = s_len // ts
    nch = ts // CHUNK
    scale = DK ** -0.5

    def body(dy_ref, q_ref, k_ref, v_ref, og_ref, o_ref, la_ref, st_ref, stp_ref, al_ref, wgu_ref,
             gg_ref, dzin_ref, dz_ref, dal_ref, dwgu_ref, dbg_ref, dgg_ref,
             gt_ref, decn_ref, accw_ref, accb_ref, accg_ref, dla_ref, tt_ref, dbe_ref):
        del dzin_ref
        i = pl.program_id(0)
        first_tile = (n - 1 - i) == 0

        @pl.when(i == 0)
        def _():
            gt_ref[...] = jnp.zeros_like(gt_ref)
            decn_ref[...] = jnp.ones_like(decn_ref)
            accw_ref[...] = jnp.zeros_like(accw_ref)
            accb_ref[...] = jnp.zeros_like(accb_ref)
            accg_ref[...] = jnp.zeros_like(accg_ref)

        upper = _strict_tri(True)
        lower = _strict_tri(False)
        ggv = gg_ref[...]
        keep = jnp.where(first_tile, 0.0, 1.0)

        def chunk(jrev, carry):
            cl = nch - 1 - jrev
            rows = pl.ds(pl.multiple_of(cl * CHUNK, CHUNK), CHUNK)
            la_c = la_ref[rows, :]
            e_dec = jnp.exp(_tri_sum(upper, la_c))
            dec = jnp.exp(jnp.sum(la_c, axis=0, keepdims=True))
            kd = k_ref[rows, :] * e_dec
            qs = q_ref[rows, :] * scale
            decn = decn_ref[0:1, :]
            has_prev = jnp.where(cl > 0, 1.0, 0.0)
            prev_idx = jnp.maximum(cl - 1, 0)
            for h in range(HEADS):
                ks = slice(h * DK, (h + 1) * DK)
                vs = slice(h * DV, (h + 1) * DV)
                o_h = o_ref[rows, vs]
                og_h = og_ref[rows, vs]
                dy_h = dy_ref[rows, vs]
                ro = lax.rsqrt(jnp.mean(o_h * o_h, axis=-1, keepdims=True) + EPS)
                oh = o_h * ro
                sig = _sigmoid(og_h)
                sil = og_h * sig
                accg_ref[...] += _rowsum8(dy_h * oh * sil)
                dz_ref[rows, 2048 + h * DV:2048 + (h + 1) * DV] = (
                    dy_h * oh * ggv * sig * (1.0 + og_h * (1.0 - sig)))
                don = dy_h * ggv * sil
                do = ro * (don - oh * jnp.mean(don * oh, axis=-1, keepdims=True))
                dob = do.astype(BF16)
                s_c = st_ref[cl, h]
                dqs = jnp.dot(dob, s_c.astype(BF16), preferred_element_type=F32)
                dz_ref[rows, ks] = dqs * scale
                gt = gt_ref[h] * decn[:, ks] + lax.dot_general(
                    dob, qs[:, ks].astype(BF16), (((0,), (0,)), ((), ())), preferred_element_type=F32)
                gt_ref[h] = gt
                gb = gt.astype(BF16)
                kd_h = kd[:, ks]
                dkd = jnp.dot(v_ref[rows, vs].astype(BF16), gb, preferred_element_type=F32)
                dz_ref[rows, 1024 + h * DV:1024 + (h + 1) * DV] = lax.dot_general(
                    kd_h.astype(BF16), gb, (((1,), (1,)), ((), ())), preferred_element_type=F32)
                s_prev = has_prev * st_ref[prev_idx, h] + (1.0 - has_prev) * keep * stp_ref[0, h]
                ddec = jnp.sum(gt * s_prev, axis=0, keepdims=True)
                dz_ref[rows, 512 + h * DK:512 + (h + 1) * DK] = dkd * e_dec[:, ks]
                tt_ref[:, ks] = dkd * kd_h
                dbe_ref[0:1, ks] = ddec * dec[:, ks]
            dla_ref[rows, :] = _tri_sum(lower, tt_ref[...]) + dbe_ref[0:1, :]
            decn_ref[0:1, :] = dec
            return carry

        lax.fori_loop(0, nch, chunk, 0)

        dpre = dla_ref[...] * (1.0 / 16.0) * (1.0 - jnp.exp(16.0 * la_ref[...]))
        accb_ref[...] += _rowsum8(dpre)
        dpb = dpre.astype(BF16)
        accw_ref[...] += lax.dot_general(al_ref[...].astype(BF16), dpb, (((0,), (0,)), ((), ())),
                                         preferred_element_type=F32)
        dal_ref[...] = lax.dot_general(dpb, wgu_ref[...], (((1,), (1,)), ((), ())),
                                       preferred_element_type=F32)

        @pl.when(i == n - 1)
        def _():
            dwgu_ref[...] = accw_ref[...]
            dbg_ref[...] = jnp.sum(accb_ref[...], axis=0, keepdims=True)
            dgg_ref[...] = jnp.sum(accg_ref[...], axis=0, keepdims=True)

    def zcol(width, j):
        return pl.BlockSpec((ts, width), lambda i, j=j: (n - 1 - i, j))

    full = lambda shape: pl.BlockSpec(shape, lambda i: tuple(0 for _ in shape))
    return _call(
        body, grid=(n,),
        in_specs=[zcol(1024, 1), zcol(512, 6), zcol(512, 7), zcol(1024, 4), zcol(1024, 5),
                  zcol(1024, 0), zcol(512, 0),
                  pl.BlockSpec((nch, HEADS, DV, DK), lambda i: (n - 1 - i, 0, 0, 0)),
                  pl.BlockSpec((1, HEADS, DV, DK),
                               lambda i: (jnp.maximum((n - 1 - i) * nch - 1, 0), 0, 0, 0)),
                  zcol(LANES, 0), full((LANES, 512)), full((1, DV)),
                  pl.BlockSpec(memory_space=pl.ANY)],
        out_specs=[zcol(3072, 1), zcol(LANES, 0), full((LANES, 512)), full((1, 512)), full((1, DV))],
        out_shape=[jax.ShapeDtypeStruct((s_len, 6144), F32), jax.ShapeDtypeStruct((s_len, LANES), F32),
                   jax.ShapeDtypeStruct((LANES, 512), F32), jax.ShapeDtypeStruct((1, 512), F32),
                   jax.ShapeDtypeStruct((1, DV), F32)],
        scratch_shapes=[pltpu.VMEM((HEADS, DV, DK), F32), pltpu.VMEM((SUBLANES, 512), F32),
                        pltpu.VMEM((LANES, 512), F32), pltpu.VMEM((SUBLANES, 512), F32),
                        pltpu.VMEM((SUBLANES, DV), F32), pltpu.VMEM((ts, 512), F32),
                        pltpu.VMEM((CHUNK, 512), F32), pltpu.VMEM((SUBLANES, 512), F32)],
        aliases={12: 0},
        operands=(dy, z, z, z, z, o, la, st, st, alow, wgu, gg, dz_in), name="gla_bwd", comm=comm)


def _adamw(w, g, m, v, *, name):
    rows, cols = w.shape
    tr, tc = _tile(rows, cols)

    def body(w_ref, g_ref, m_ref, v_ref, go_ref, d_ref, nm_ref, nv_ref):
        gv = g_ref[...]
        go_ref[...] = gv
        m2 = ADAM_B1 * m_ref[...] + (1.0 - ADAM_B1) * gv
        v2 = ADAM_B2 * v_ref[...] + (1.0 - ADAM_B2) * jnp.square(gv)
        m_hat = m2 / (1.0 - ADAM_B1 ** ADAM_STEP)
        v_hat = v2 / (1.0 - ADAM_B2 ** ADAM_STEP)
        d_ref[...] = -ADAM_LR * (m_hat / (jnp.sqrt(v_hat) + ADAM_EPS) + ADAM_WD * w_ref[...])
        nm_ref[...] = m2
        nv_ref[...] = v2

    blk = pl.BlockSpec((tr, tc), lambda i, j: (i, j))
    shp = jax.ShapeDtypeStruct((rows, cols), F32)
    return pl.pallas_call(
        body, grid=(rows // tr, cols // tc), in_specs=[blk] * 4, out_specs=[blk] * 4, out_shape=[shp] * 4,
        compiler_params=_params("parallel", "parallel"), name=name,
    )(w, g, m, v)


def _my_place():
    return lax.axis_index("x"), lax.axis_index("y"), lax.axis_index("c")


def _flip(v, bit):
    return 1 - v if bit else v


def _allgather_small(buf, *, reduce, name):
    rows = buf.shape[0]

    def body(in_ref, out_ref, gat_ref, send_sems, recv_sems):
        x, y, c = _my_place()
        me = 4 * x + 2 * y + c
        gat_ref[me] = in_ref[...]
        copies = []
        for m in range(1, N_DEV):
            peer = (_flip(x, m & 4), _flip(y, m & 2), _flip(c, m & 1))
            cp = pltpu.make_async_remote_copy(
                src_ref=in_ref, dst_ref=gat_ref.at[me],
                send_sem=send_sems.at[m - 1], recv_sem=recv_sems.at[m - 1],
                device_id=peer, device_id_type=MESH)
            cp.start()
            copies.append(cp)
        for m in range(1, N_DEV):
            px, py, pc = _flip(x, m & 4), _flip(y, m & 2), _flip(c, m & 1)
            src_slot = gat_ref.at[4 * px + 2 * py + pc]
            pltpu.make_async_remote_copy(
                src_ref=src_slot, dst_ref=src_slot,
                send_sem=send_sems.at[m - 1], recv_sem=recv_sems.at[m - 1],
                device_id=(px, py, pc), device_id_type=MESH).wait_recv()
        for cp in copies:
            cp.wait_send()
        if reduce:
            tot = gat_ref[0]
            for d in range(1, N_DEV):
                tot = tot + gat_ref[d]
            out_ref[...] = tot
        else:
            out_ref[...] = gat_ref[...]

    out_shape = (rows, LANES) if reduce else (N_DEV, rows, LANES)
    return pl.pallas_call(
        body,
        in_specs=[pl.BlockSpec(memory_space=pltpu.VMEM)],
        out_specs=pl.BlockSpec(memory_space=pltpu.VMEM),
        out_shape=jax.ShapeDtypeStruct(out_shape, F32),
        scratch_shapes=[pltpu.VMEM((N_DEV, rows, LANES), F32),
                        pltpu.SemaphoreType.DMA((N_DEV - 1,)), pltpu.SemaphoreType.DMA((N_DEV - 1,))],
        compiler_params=pltpu.CompilerParams(has_side_effects=True),
        name=name,
    )(buf)


def _cast_into(shard, chip_core, *, name):
    rows, cols = shard.shape
    tr, tc = _tile(rows, cols)

    def body(cc_ref, s_ref, o_ref):
        del cc_ref
        o_ref[...] = s_ref[...].astype(BF16)

    grid_spec = pltpu.PrefetchScalarGridSpec(
        num_scalar_prefetch=1, grid=(rows // tr, cols // tc),
        in_specs=[pl.BlockSpec((tr, tc), lambda r, q, cc: (r, q))],
        out_specs=pl.BlockSpec((None, tr, tc), lambda r, q, cc: (cc[0], r, q)))
    return pl.pallas_call(
        body, grid_spec=grid_spec, out_shape=jax.ShapeDtypeStruct((N_CHIPS, rows, cols), BF16),
        compiler_params=_params("arbitrary", "arbitrary"), name=name,
    )(chip_core, shard)


def _remote(src, dst, send_sems, recv_sems, k, device):
    return pltpu.make_async_remote_copy(src_ref=src, dst_ref=dst, send_sem=send_sems.at[k],
                                        recv_sem=recv_sems.at[k], device_id=device, device_id_type=MESH)


def _col_half(ref, h, *lead):
    hc = ref.shape[-1] // 2
    mid = (slice(None),) * (len(ref.shape) - 1 - len(lead))
    return ref.at[tuple(lead) + mid + (pl.ds(h * hc, hc),)]


def _gather_comm(bufs):
    n_w, n_m = len(bufs), len(CHIP_MASKS)

    def first(c_ins, c_outs, ss, rs):
        x, y, c = _my_place()
        chip = 2 * x + y
        for w in range(n_w):
            mine = _col_half(c_outs[w], c, chip)
            for mi, (mx, my) in enumerate(CHIP_MASKS):
                _remote(mine, mine, ss, rs, w * n_m + mi, (_flip(x, mx), _flip(y, my), c)).start()

    def last(c_ins, c_outs, ss, rs):
        x, y, c = _my_place()
        chip = 2 * x + y
        for w in range(n_w):
            for mi, (mx, my) in enumerate(CHIP_MASKS):
                k = w * n_m + mi
                px, py = _flip(x, mx), _flip(y, my)
                landed = _col_half(c_outs[w], c, 2 * px + py)
                _remote(landed, landed, ss, rs, k, (px, py, c)).wait_recv()
                _remote(landed, landed, ss, rs, n_w * n_m + k, (x, y, 1 - c)).start()
        for w in range(n_w):
            mine = _col_half(c_outs[w], c, chip)
            for mi, (mx, my) in enumerate(CHIP_MASKS):
                k = w * n_m + mi
                px, py = _flip(x, mx), _flip(y, my)
                theirs = _col_half(c_outs[w], 1 - c, 2 * px + py)
                _remote(theirs, theirs, ss, rs, n_w * n_m + k, (x, y, 1 - c)).wait_recv()
                _remote(mine, mine, ss, rs, k, (px, py, c)).wait_send()
                _remote(mine, mine, ss, rs, n_w * n_m + k, (x, y, 1 - c)).wait_send()

    return _Comm(ins=bufs, outs=[jax.ShapeDtypeStruct(b.shape, b.dtype) for b in bufs],
                 aliases={w: w for w in range(n_w)}, n_sems=2 * n_w * n_m, first=first, last=last)


def _swap_comm(grads):
    n_w = len(grads)

    def copy(c_ins, c_outs, ss, rs, w):
        x, y, c = _my_place()
        return _remote(_col_half(c_ins[w], 1 - c), c_outs[w], ss, rs, w, (x, y, 1 - c))

    def first(c_ins, c_outs, ss, rs):
        for w in range(n_w):
            copy(c_ins, c_outs, ss, rs, w).start()

    def last(c_ins, c_outs, ss, rs):
        for w in range(n_w):
            copy(c_ins, c_outs, ss, rs, w).wait()

    return _Comm(ins=grads,
                 outs=[jax.ShapeDtypeStruct(g.shape[:2] + (g.shape[2] // 2,), g.dtype) for g in grads],
                 aliases={}, n_sems=n_w, first=first, last=last)


def _add_own_half(g, other, chip_core, *, name):
    n_chip, rows, hc = other.shape
    tr, tc = _tile(rows, hc)
    per_half = hc // tc

    def body(cc_ref, g_ref, o_ref, out_ref):
        del cc_ref
        out_ref[...] = (g_ref[...].astype(F32) + o_ref[...].astype(F32)).astype(BF16)

    grid_spec = pltpu.PrefetchScalarGridSpec(
        num_scalar_prefetch=1, grid=(n_chip, rows // tr, per_half),
        in_specs=[pl.BlockSpec((None, tr, tc), lambda j, r, q, cc: (j, r, cc[1] * per_half + q)),
                  pl.BlockSpec((None, tr, tc), lambda j, r, q, cc: (j, r, q))],
        out_specs=pl.BlockSpec((None, tr, tc), lambda j, r, q, cc: (j, r, q)))
    return pl.pallas_call(
        body, grid_spec=grid_spec, out_shape=jax.ShapeDtypeStruct((n_chip, rows, hc), BF16),
        compiler_params=_params("parallel", "parallel", "parallel"), name=name,
    )(chip_core, g, other)


def _exchange_comm(pieces):
    n_w, n_m = len(pieces), len(CHIP_MASKS)

    def copies(c_ins, c_outs, ss, rs):
        x, y, c = _my_place()
        chip = 2 * x + y
        for w in range(n_w):
            for mi, (mx, my) in enumerate(CHIP_MASKS):
                px, py = _flip(x, mx), _flip(y, my)
                send = _remote(c_ins[w].at[2 * px + py], c_outs[w].at[chip], ss, rs, w * n_m + mi, (px, py, c))
                landed = c_outs[w].at[2 * px + py]
                yield send, _remote(landed, landed, ss, rs, w * n_m + mi, (px, py, c))

    def first(c_ins, c_outs, ss, rs):
        for send, _ in copies(c_ins, c_outs, ss, rs):
            send.start()

    def last(c_ins, c_outs, ss, rs):
        for send, arrival in copies(c_ins, c_outs, ss, rs):
            arrival.wait_recv()
            send.wait_send()

    return _Comm(ins=pieces, outs=[jax.ShapeDtypeStruct(p.shape, p.dtype) for p in pieces],
                 aliases={}, n_sems=n_w * n_m, first=first, last=last)


def _sum_chips(own, landed, chip_core, *, name):
    n_chip, rows, hc = own.shape
    tr, tc = _tile(rows, hc)
    per_half = hc // tc

    def body(cc_ref, o_ref, l1_ref, l2_ref, l3_ref, out_ref):
        del cc_ref
        out_ref[...] = ((o_ref[...].astype(F32) + l1_ref[...].astype(F32))
                        + l2_ref[...].astype(F32)) + l3_ref[...].astype(F32)

    def slot(k):
        return pl.BlockSpec((None, tr, tc), lambda r, q, cc, k=k: ((cc[0] + k) % n_chip, r, q))

    grid_spec = pltpu.PrefetchScalarGridSpec(
        num_scalar_prefetch=1, grid=(rows // tr, per_half),
        in_specs=[slot(0), slot(1), slot(2), slot(3)],
        out_specs=pl.BlockSpec((tr, tc), lambda r, q, cc: (r, cc[1] * per_half + q)))
    return pl.pallas_call(
        body, grid_spec=grid_spec, out_shape=jax.ShapeDtypeStruct((rows, 2 * hc), F32),
        compiler_params=_params("arbitrary", "arbitrary"), name=name,
    )(chip_core, own, landed, landed, landed)


def _join_comm(halves):
    n_w = len(halves)

    def first(c_ins, c_outs, ss, rs):
        x, y, c = _my_place()
        for w in range(n_w):
            mine = _col_half(c_outs[w], c)
            _remote(mine, mine, ss, rs, w, (x, y, 1 - c)).start()

    def last(c_ins, c_outs, ss, rs):
        x, y, c = _my_place()
        for w in range(n_w):
            theirs = _col_half(c_outs[w], 1 - c)
            _remote(theirs, theirs, ss, rs, w, (x, y, 1 - c)).wait()

    return _Comm(ins=halves, outs=[jax.ShapeDtypeStruct(h.shape, h.dtype) for h in halves],
                 aliases={w: w for w in range(n_w)}, n_sems=n_w, first=first, last=last)


def _standalone(comm, *, name):
    def body(o_ref):
        o_ref[...] = jnp.zeros_like(o_ref)

    return _call(body, grid=(1,), in_specs=[],
                 out_specs=[pl.BlockSpec((SUBLANES, LANES), lambda i: (0, 0))],
                 out_shape=[jax.ShapeDtypeStruct((SUBLANES, LANES), F32)], operands=(), name=name,
                 comm=comm)[1:]


def _pack(pieces):
    flat, spans, off = [], [], 0
    for p in pieces:
        v = p.reshape(-1).astype(F32)
        pad = (-v.shape[0]) % LANES
        if pad:
            v = jnp.concatenate([v, jnp.zeros((pad,), F32)])
        spans.append((off, p.size))
        off += v.shape[0]
        flat.append(v)
    tail = (-off) % (SUBLANES * LANES)
    if tail:
        flat.append(jnp.zeros((tail,), F32))
    return jnp.concatenate(flat).reshape(-1, LANES), spans


def _unpack(buf, span, shape):
    off, size = span
    return buf.reshape(-1)[off:off + size].reshape(shape)


def kernel(x, norm1_g, w_in, w_gate_up, b_gate, conv_w, conv_norm_g, gla_norm_g, w_out, norm2_g, w_ff1, w_ff2, norm_f_g, loss_target, m_norm1_g, m_w_in, m_w_gate_up, m_b_gate, m_conv_w, m_conv_norm_g, m_gla_norm_g, m_w_out, m_norm2_g, m_w_ff1, m_w_ff2, m_norm_f_g, v_norm1_g, v_w_in, v_w_gate_up, v_b_gate, v_conv_w, v_conv_norm_g, v_gla_norm_g, v_w_out, v_norm2_g, v_w_ff1, v_w_ff2, v_norm_f_g):
    xs = x[0]
    target = loss_target[0]
    s_len, d = xs.shape
    d_in = w_in.shape[2] * N_CHIPS
    d_main = d_in - GATE_RANK
    d_ff = w_ff1.shape[2] * N_CHIPS
    cx, cy, cc = _my_place()
    chip = 2 * cx + cy
    chip_core = jnp.stack([chip, cc]).astype(jnp.int32)
    n_ff = d_ff // N_CHIPS
    norm_f = norm_f_g.reshape(1, d)

    n_sh = d_in // N_CHIPS
    wi_buf = _cast_into(w_in[0].T, chip_core, name="cast_w_in")
    wo_buf = _cast_into(w_out[0], chip_core, name="cast_w_out")
    w1_buf = _cast_into(w_ff1[0], chip_core, name="cast_w_ff1")
    w2_buf = _cast_into(w_ff2[0], chip_core, name="cast_w_ff2")

    small_w, spans_w = _pack([w_gate_up[0], conv_w[0]])
    small_all = _allgather_small(small_w, reduce=False, name="gather_small_weights")
    chips_first = [small_all[2 * j] for j in range(N_CHIPS)]
    wgu_full = jnp.concatenate(
        [_unpack(b, spans_w[0], w_gate_up.shape[1:]) for b in chips_first], axis=1)
    convw_full = jnp.concatenate(
        [_unpack(b, spans_w[1], conv_w.shape[1:]) for b in chips_first], axis=0)
    wgu_pad = jnp.concatenate(
        [wgu_full, jnp.zeros((LANES - GATE_RANK, wgu_full.shape[1]), F32)], axis=0).astype(BF16)
    convw_t = convw_full.T

    u, wi_buf = _rms_fwd(xs, norm1_g, name="norm1_fwd", comm=_gather_comm([wi_buf]))
    wi_t = wi_buf.reshape(d_in, d)
    wg_t = jnp.concatenate([wi_t[d_main:], jnp.zeros((LANES - GATE_RANK, d), BF16)], axis=0)
    z, wo_buf, w1_buf = _matmul(u, wi_t, tb=True, tm=1024, tn=1024, tk=d, out_dtypes=[F32], n_dim=d_main,
                                name="in_proj", comm=_gather_comm([wo_buf, w1_buf]))
    wo_full = wo_buf.reshape(d, d)
    w1_cm = w1_buf
    (alow,) = _matmul(u, wg_t, tb=True, tm=1024, tn=LANES, tk=d, out_dtypes=[F32], name="in_proj_gate")
    y0 = _conv_fwd(z, convw_t, conv_norm_g)
    y, o, la, st = _gla_fwd(z, alow, wgu_pad, b_gate, gla_norm_g, y0)
    (x2,) = _matmul(y, wo_full, tm=1024, tn=1024, tk=d, out_dtypes=[F32], extras=(xs,),
                    epilogue=_add_epilogue, name="out_proj")
    (h,) = _rms_fwd(x2, norm2_g, name="norm2_fwd")
    a, p, w2_buf = _matmul(
        h, w1_cm, tm=1024, tn=1024, tk=d, out_dtypes=[F32, BF16], n_dim=d_ff,
        b_spec=pl.BlockSpec((None, d, 1024), lambda i, j, k: (j // 2, 0, j % 2)),
        epilogue=lambda acc: (acc, jnp.square(jnp.maximum(acc, 0.0))), name="ff1",
        comm=_gather_comm([w2_buf]))
    w2_full = w2_buf.reshape(d_ff, d)
    (x3,) = _matmul(p, w2_full, tm=1024, tn=1024, tk=2048, out_dtypes=[F32], extras=(x2,),
                    epilogue=_add_epilogue, name="ff2")
    dx3, dx3b, g_normf, loss_part = _loss_head(x3, norm_f, target)

    (da,) = _matmul(dx3b, w2_full, tb=True, tm=1024, tn=1024, tk=d, out_dtypes=[BF16], extras=(a,),
                    epilogue=lambda acc, av: (acc * (2.0 * jnp.maximum(av, 0.0)),), name="ff2_dx")
    (g_w2,) = _matmul(p, dx3b, ta=True, tm=1024, tn=d, tk=1024, out_dtypes=[BF16], name="ff2_dw")
    (g_w1,) = _matmul(
        h, da, ta=True, tm=1024, tn=n_ff, tk=1024, out_dtypes=[BF16], name="ff1_dw",
        out_shapes=[jax.ShapeDtypeStruct((N_CHIPS, d, n_ff), BF16)],
        out_specs=[pl.BlockSpec((None, 1024, n_ff), lambda i, j, k: (j, i, 0))])
    g_w2 = g_w2.reshape(N_CHIPS, n_ff, d)
    dh, t_w1, t_w2 = _matmul(
        da, w1_cm, tb=True, tm=1024, tn=1024, tk=2048, out_dtypes=[F32], n_dim=d,
        b_spec=pl.BlockSpec((None, 1024, 2048), lambda i, j, k: (k, j, 0)), name="ff1_dx",
        comm=_swap_comm([g_w1, g_w2]))
    p_w1 = _add_own_half(g_w1, t_w1, chip_core, name="pre_reduce_w_ff1")
    p_w2 = _add_own_half(g_w2, t_w2, chip_core, name="pre_reduce_w_ff2")
    dx2, g_norm2 = _rms_bwd(dh, x2, norm2_g, dx3, name="norm2_bwd")
    (dy,) = _matmul(dx2, wo_full, tb=True, tm=1024, tn=1024, tk=d, out_dtypes=[F32], name="out_proj_dx")
    (g_wo,) = _matmul(y, dx2, ta=True, tm=1024, tn=d, tk=1024, out_dtypes=[BF16], name="out_proj_dw")
    g_wo = g_wo.reshape(N_CHIPS, d // N_CHIPS, d)
    dz0, g_convw_t, g_convg = _conv_bwd(dy, z, convw_t, conv_norm_g)
    dz, dalow, g_wgu_pad, g_bg, g_gg, l_w2 = _gla_bwd(dy, z, o, la, st, alow, wgu_pad, gla_norm_g, dz0,
                                                      comm=_exchange_comm([p_w2]))
    m_w2 = _sum_chips(p_w2, l_w2, chip_core, name="reduce_w_ff2")
    g_wi_t, l_w1 = _matmul(dz, u, ta=True, tm=1024, tn=d, tk=1024, out_dtypes=[BF16], name="in_proj_dw",
                           out_shapes=[jax.ShapeDtypeStruct((d_in, d), BF16)],
                           comm=_exchange_comm([p_w1]))
    (g_wi_t,) = _matmul(dalow, u, ta=True, tm=LANES, tn=d, tk=1024, out_dtypes=[BF16],
                        epilogue=lambda acc: (acc[:GATE_RANK],), into=g_wi_t, name="in_proj_gate_dw",
                        out_shapes=[jax.ShapeDtypeStruct((d_in, d), BF16)],
                        out_specs=[pl.BlockSpec((GATE_RANK, d), lambda i, j, k: (d_main // GATE_RANK, 0))])
    m_w1 = _sum_chips(p_w1, l_w1, chip_core, name="reduce_w_ff1")
    g_wi = g_wi_t.reshape(N_CHIPS, n_sh, d)
    du_gate, t_wi, t_wo = _matmul(dalow, wg_t, tm=1024, tn=1024, tk=LANES, out_dtypes=[F32],
                                  name="in_proj_gate_dx", comm=_swap_comm([g_wi, g_wo]))
    p_wi = _add_own_half(g_wi, t_wi, chip_core, name="pre_reduce_w_in")
    p_wo = _add_own_half(g_wo, t_wo, chip_core, name="pre_reduce_w_out")
    du, l_wi, l_wo, m_w1, m_w2 = _matmul(
        dz, wi_t, tm=1024, tn=1024, tk=2048, out_dtypes=[F32], n_dim=d, extras=(du_gate,),
        epilogue=_add_epilogue, name="in_proj_dx",
        comm=[_exchange_comm([p_wi, p_wo]), _join_comm([m_w1, m_w2])])
    m_wi = _sum_chips(p_wi, l_wi, chip_core, name="reduce_w_in")
    m_wo = _sum_chips(p_wo, l_wo, chip_core, name="reduce_w_out")
    grad_x, g_norm1 = _rms_bwd(du, xs, norm1_g, dx2, name="norm1_bwd")
    m_wi, m_wo = _standalone(_join_comm([m_wi, m_wo]), name="join_w_in_w_out")
    g_big = [m_wi, m_wo, m_w1, m_w2]

    small_g, spans_g = _pack([g_norm1, g_wgu_pad[:GATE_RANK], g_bg, g_convw_t, g_convg, g_gg, g_norm2,
                              g_normf, loss_part[:, :1]])
    tot = _allgather_small(small_g, reduce=True, name="reduce_small_grads")
    t_norm1 = _unpack(tot, spans_g[0], (1, d))
    t_wgu = _unpack(tot, spans_g[1], (GATE_RANK, HEADS * DK))
    t_bg = _unpack(tot, spans_g[2], (1, HEADS * DK))
    t_convw = _unpack(tot, spans_g[3], (3, D_CONV)).T
    t_convg = _unpack(tot, spans_g[4], (1, D_CONV))
    t_gg = _unpack(tot, spans_g[5], (1, DV))
    t_norm2 = _unpack(tot, spans_g[6], (1, d))
    t_normf = _unpack(tot, spans_g[7], (1, d))
    loss = _unpack(tot, spans_g[8], ())
    n_gu = w_gate_up.shape[2]
    n_cw = conv_w.shape[1]
    t_wgu = lax.dynamic_slice(t_wgu, (0, chip * n_gu), (GATE_RANK, n_gu))
    t_convw = lax.dynamic_slice(t_convw, (chip * n_cw, 0), (n_cw, 3))

    order = ["norm1_g", "w_in", "w_gate_up", "b_gate", "conv_w", "conv_norm_g", "gla_norm_g", "w_out",
             "norm2_g", "w_ff1", "w_ff2", "norm_f_g"]
    weights = dict(norm1_g=norm1_g, w_in=w_in, w_gate_up=w_gate_up, b_gate=b_gate, conv_w=conv_w,
                   conv_norm_g=conv_norm_g, gla_norm_g=gla_norm_g, w_out=w_out, norm2_g=norm2_g,
                   w_ff1=w_ff1, w_ff2=w_ff2, norm_f_g=norm_f_g)
    moms = dict(norm1_g=m_norm1_g, w_in=m_w_in, w_gate_up=m_w_gate_up, b_gate=m_b_gate, conv_w=m_conv_w,
                conv_norm_g=m_conv_norm_g, gla_norm_g=m_gla_norm_g, w_out=m_w_out, norm2_g=m_norm2_g,
                w_ff1=m_w_ff1, w_ff2=m_w_ff2, norm_f_g=m_norm_f_g)
    vels = dict(norm1_g=v_norm1_g, w_in=v_w_in, w_gate_up=v_w_gate_up, b_gate=v_b_gate, conv_w=v_conv_w,
                conv_norm_g=v_conv_norm_g, gla_norm_g=v_gla_norm_g, w_out=v_w_out, norm2_g=v_norm2_g,
                w_ff1=v_w_ff1, w_ff2=v_w_ff2, norm_f_g=v_norm_f_g)
    grads2d = dict(norm1_g=t_norm1, w_in=g_big[0], w_gate_up=t_wgu, b_gate=t_bg, conv_w=t_convw,
                   conv_norm_g=t_convg, gla_norm_g=t_gg, w_out=g_big[1], norm2_g=t_norm2,
                   w_ff1=g_big[2], w_ff2=g_big[3], norm_f_g=t_normf)
    out_g, out_d, out_m, out_v = [], [], [], []
    for nm in order:
        w = weights[nm]
        g2 = grads2d[nm]
        if nm == "w_in":
            to2d, back = (lambda t: t[0].T), (lambda t: t.T.reshape(w.shape))
        else:
            to2d, back = (lambda t: t.reshape(g2.shape)), (lambda t: t.reshape(w.shape))
        res = _adamw(to2d(w), g2, to2d(moms[nm]), to2d(vels[nm]), name="adamw_" + nm)
        for lst, r in zip((out_g, out_d, out_m, out_v), res):
            lst.append(back(r))
    return (loss, grad_x.reshape(x.shape), *out_g, *out_d, *out_m, *out_v)
```

```python
import functools

import jax
import jax.numpy as jnp
from jax import lax
from jax.experimental import pallas as pl
from jax.experimental.pallas import tpu as pltpu

F32 = jnp.float32
BF16 = jnp.bfloat16
MESH = pl.DeviceIdType.MESH

EPS = 1e-6
CHUNK = 64
HEADS = 4
DK = 128
DV = 256
D_CONV = 1024
GROUP = 128
GATE_RANK = 16
LANES = 128
SUBLANES = 8
N_CHIPS = 4
N_DEV = 8
CHIP_MASKS = ((1, 0), (0, 1), (1, 1))

ADAM_LR = 0.001
ADAM_B1 = 0.9
ADAM_B2 = 0.999
ADAM_EPS = 1e-08
ADAM_WD = 0.01
ADAM_STEP = 10

VMEM_LIMIT = 56 * 1024 * 1024


def _params(*sem):
    return pltpu.CompilerParams(dimension_semantics=tuple(sem), vmem_limit_bytes=VMEM_LIMIT)


def _rowsum8(v):
    r, c = v.shape
    return jnp.sum(v.reshape(r // SUBLANES, SUBLANES, c), axis=0)


def _tile(rows, cols):
    for cand in (256, 128, 64, 32, 16, 8):
        if rows % cand == 0 and rows > cand:
            return cand, cols
    if rows * cols * 4 > (2 << 20) and cols % 256 == 0:
        return rows, 256
    return rows, cols


class _Comm:
    def __init__(self, ins, outs, aliases, n_sems, first, last):
        self.ins = list(ins)
        self.outs = list(outs)
        self.aliases = dict(aliases)
        self.n_sems = n_sems
        self.first = first
        self.last = last


def _call(body, *, grid, in_specs, out_specs, out_shape, operands, name, scratch_shapes=(), sem=None,
          aliases=None, comm=None):
    aliases = dict(aliases or {})
    if comm is None:
        return pl.pallas_call(
            body, grid=grid, in_specs=list(in_specs), out_specs=list(out_specs), out_shape=list(out_shape),
            scratch_shapes=list(scratch_shapes), input_output_aliases=aliases,
            compiler_params=_params(*(sem or ("arbitrary",) * len(grid))), name=name)(*operands)
    comms = list(comm) if isinstance(comm, (list, tuple)) else [comm]
    n_in, n_out, n_scr = len(in_specs), len(out_specs), len(scratch_shapes)
    c_ins_all = [a for cm in comms for a in cm.ins]
    c_outs_all = [o for cm in comms for o in cm.outs]
    n_ci, n_co = len(c_ins_all), len(c_outs_all)

    def full_body(*refs):
        ins = refs[:n_in]
        o0 = n_in + n_ci
        outs = refs[o0:o0 + n_out]
        s0 = o0 + n_out + n_co
        scr = refs[s0:s0 + n_scr]
        sems = refs[s0 + n_scr:]
        parts, i_at, o_at = [], n_in, o0 + n_out
        for q, cm in enumerate(comms):
            parts.append((cm, refs[i_at:i_at + len(cm.ins)], refs[o_at:o_at + len(cm.outs)],
                          sems[2 * q], sems[2 * q + 1]))
            i_at += len(cm.ins)
            o_at += len(cm.outs)
        ids = [pl.program_id(ax) for ax in range(len(grid))]
        is_first = functools.reduce(jnp.logical_and, [i == 0 for i in ids])
        is_last = functools.reduce(jnp.logical_and, [i == g - 1 for i, g in zip(ids, grid)])

        @pl.when(is_first)
        def _():
            for cm, c_ins, c_outs, ss, rs in parts:
                cm.first(c_ins, c_outs, ss, rs)

        body(*ins, *outs, *scr)

        @pl.when(is_last)
        def _():
            for cm, c_ins, c_outs, ss, rs in parts:
                cm.last(c_ins, c_outs, ss, rs)

    any_spec = pl.BlockSpec(memory_space=pl.ANY)
    i_at, o_at, sem_shapes = n_in, n_out, []
    for cm in comms:
        for i_in, i_out in cm.aliases.items():
            aliases[i_at + i_in] = o_at + i_out
        i_at += len(cm.ins)
        o_at += len(cm.outs)
        sem_shapes += [pltpu.SemaphoreType.DMA((cm.n_sems,)), pltpu.SemaphoreType.DMA((cm.n_sems,))]
    return pl.pallas_call(
        full_body, grid=grid,
        in_specs=list(in_specs) + [any_spec] * n_ci,
        out_specs=list(out_specs) + [any_spec] * n_co,
        out_shape=list(out_shape) + c_outs_all,
        scratch_shapes=list(scratch_shapes) + sem_shapes,
        input_output_aliases=aliases,
        compiler_params=pltpu.CompilerParams(dimension_semantics=("arbitrary",) * len(grid),
                                             vmem_limit_bytes=VMEM_LIMIT, has_side_effects=True),
        name=name)(*operands, *c_ins_all)


def _matmul(a, b, *, ta=False, tb=False, tm, tn, tk, out_dtypes, name, extras=(), epilogue=None,
            out_shapes=None, out_specs=None, b_spec=None, n_dim=None, into=None, comm=None):
    n_into = 0 if into is None else 1
    if ta:
        k_dim, m_dim = a.shape
    else:
        m_dim, k_dim = a.shape
    if n_dim is None:
        n_dim = b.shape[0] if tb else b.shape[1]
        assert (b.shape[1] if tb else b.shape[0]) == k_dim
    assert m_dim % tm == 0 and n_dim % tn == 0 and k_dim % tk == 0, (name, a.shape, b.shape)
    nk = k_dim // tk
    n_ex, n_out = len(extras), len(out_dtypes)
    dims = (((0 if ta else 1,), (1 if tb else 0,)), ((), ()))

    def body(*refs):
        a_ref, b_ref = refs[0], refs[1]
        ex_refs = refs[2:2 + n_ex]
        o_refs = refs[2 + n_ex + n_into:2 + n_ex + n_into + n_out]

        def dot():
            return lax.dot_general(a_ref[...].astype(BF16), b_ref[...].astype(BF16), dims,
                                   preferred_element_type=F32)

        def finish(acc):
            outs = epilogue(acc, *[e[...] for e in ex_refs]) if epilogue is not None else (acc,)
            for o_ref, o in zip(o_refs, outs):
                o_ref[...] = o.astype(o_ref.dtype)

        if nk == 1:
            finish(dot())
        else:
            acc_ref = refs[-1]
            k = pl.program_id(2)

            @pl.when(k == 0)
            def _():
                acc_ref[...] = dot()

            @pl.when(jnp.logical_and(k > 0, k < nk - 1))
            def _():
                acc_ref[...] += dot()

            @pl.when(k == nk - 1)
            def _():
                finish(acc_ref[...] + dot())

    a_spec = (pl.BlockSpec((tk, tm), lambda i, j, k: (k, i)) if ta
              else pl.BlockSpec((tm, tk), lambda i, j, k: (i, k)))
    if b_spec is None:
        b_spec = (pl.BlockSpec((tn, tk), lambda i, j, k: (j, k)) if tb
                  else pl.BlockSpec((tk, tn), lambda i, j, k: (k, j)))
    io_spec = pl.BlockSpec((tm, tn), lambda i, j, k: (i, j))
    if out_shapes is None:
        out_shapes = [jax.ShapeDtypeStruct((m_dim, n_dim), dt) for dt in out_dtypes]
    if out_specs is None:
        out_specs = [io_spec] * n_out
    return _call(
        body,
        grid=(m_dim // tm, n_dim // tn, nk),
        in_specs=[a_spec, b_spec] + [io_spec] * n_ex + [pl.BlockSpec(memory_space=pl.ANY)] * n_into,
        out_specs=out_specs,
        out_shape=out_shapes,
        scratch_shapes=[pltpu.VMEM((tm, tn), F32)] if nk > 1 else [],
        sem=("parallel", "parallel", "arbitrary"),
        aliases={2 + n_ex: 0} if n_into else None,
        operands=(a, b, *extras) + ((into,) if n_into else ()), name=name, comm=comm)


def _add_epilogue(acc, r):
    return (acc + r,)


def _rms_fwd(x, g, *, name, tm=512, comm=None):
    s_len, d = x.shape

    def body(x_ref, g_ref, o_ref):
        xv = x_ref[...]
        r = lax.rsqrt(jnp.mean(xv * xv, axis=-1, keepdims=True) + EPS)
        o_ref[...] = (xv * r * g_ref[...]).astype(o_ref.dtype)

    return _call(
        body, grid=(s_len // tm,),
        in_specs=[pl.BlockSpec((tm, d), lambda i: (i, 0)), pl.BlockSpec((1, d), lambda i: (0, 0))],
        out_specs=[pl.BlockSpec((tm, d), lambda i: (i, 0))],
        out_shape=[jax.ShapeDtypeStruct((s_len, d), BF16)],
        sem=("parallel",), operands=(x, g), name=name, comm=comm)


def _rms_bwd(dn, x, g, res, *, name, tm=512, comm=None):
    s_len, d = x.shape
    n = s_len // tm

    def body(dn_ref, x_ref, g_ref, res_ref, dx_ref, dg_ref, acc_ref):
        i = pl.program_id(0)
        xv = x_ref[...]
        dnv = dn_ref[...]
        r = lax.rsqrt(jnp.mean(xv * xv, axis=-1, keepdims=True) + EPS)
        xh = xv * r

        @pl.when(i == 0)
        def _():
            acc_ref[...] = jnp.zeros_like(acc_ref)

        acc_ref[...] += _rowsum8(dnv * xh)
        dxh = dnv * g_ref[...]
        dx_ref[...] = r * (dxh - xh * jnp.mean(dxh * xh, axis=-1, keepdims=True)) + res_ref[...]

        @pl.when(i == n - 1)
        def _():
            dg_ref[...] = jnp.sum(acc_ref[...], axis=0, keepdims=True)

    row = pl.BlockSpec((tm, d), lambda i: (i, 0))
    vec = pl.BlockSpec((1, d), lambda i: (0, 0))
    return _call(
        body, grid=(n,),
        in_specs=[row, row, vec, row],
        out_specs=[row, vec],
        out_shape=[jax.ShapeDtypeStruct((s_len, d), F32), jax.ShapeDtypeStruct((1, d), F32)],
        scratch_shapes=[pltpu.VMEM((SUBLANES, d), F32)],
        operands=(dn, x, g, res), name=name, comm=comm)


def _loss_head(x3, g, target, *, tm=512):
    s_len, d = x3.shape
    n = s_len // tm

    def body(x_ref, g_ref, t_ref, dx_ref, dxb_ref, dg_ref, loss_ref, accg_ref, accl_ref):
        i = pl.program_id(0)
        xv = x_ref[...]
        gv = g_ref[...]
        r = lax.rsqrt(jnp.mean(xv * xv, axis=-1, keepdims=True) + EPS)
        xh = xv * r
        err = xh * gv - t_ref[...]

        @pl.when(i == 0)
        def _():
            accg_ref[...] = jnp.zeros_like(accg_ref)
            accl_ref[...] = jnp.zeros_like(accl_ref)

        accl_ref[...] += _rowsum8(err * err)
        dn = err * (1.0 / d)
        accg_ref[...] += _rowsum8(dn * xh)
        dxh = dn * gv
        dx = r * (dxh - xh * jnp.mean(dxh * xh, axis=-1, keepdims=True))
        dx_ref[...] = dx
        dxb_ref[...] = dx.astype(BF16)

        @pl.when(i == n - 1)
        def _():
            dg_ref[...] = jnp.sum(accg_ref[...], axis=0, keepdims=True)
            tot = jnp.sum(jnp.sum(accl_ref[...], axis=0, keepdims=True), axis=1, keepdims=True)
            loss_ref[...] = jnp.broadcast_to(tot * (0.5 / d), (1, LANES))

    row = pl.BlockSpec((tm, d), lambda i: (i, 0))
    vec = pl.BlockSpec((1, d), lambda i: (0, 0))
    return pl.pallas_call(
        body, grid=(n,),
        in_specs=[row, vec, row],
        out_specs=[row, row, vec, pl.BlockSpec((1, LANES), lambda i: (0, 0))],
        out_shape=[jax.ShapeDtypeStruct((s_len, d), F32), jax.ShapeDtypeStruct((s_len, d), BF16),
                   jax.ShapeDtypeStruct((1, d), F32), jax.ShapeDtypeStruct((1, LANES), F32)],
        scratch_shapes=[pltpu.VMEM((SUBLANES, d), F32), pltpu.VMEM((SUBLANES, d), F32)],
        compiler_params=_params("arbitrary"), name="loss_head",
    )(x3, g, target)


def _shift_down(v, k, rows_before, row):
    out = pltpu.roll(v, k, axis=0)
    for j in range(k):
        out = jnp.where(row == j, rows_before[j], out)
    return out


def _shift_up(v, k, rows_after, row):
    t = v.shape[0]
    out = pltpu.roll(v, t - k, axis=0)
    for j in range(k):
        out = jnp.where(row == t - k + j, rows_after[j], out)
    return out


def _conv_fwd(z, w_t, gain, *, ts=512):
    s_len = z.shape[0]
    n_grp = D_CONV // GROUP

    def body(cb_ref, cc_ref, ch_ref, w_ref, g_ref, y_ref, carry_ref):
        i = pl.program_id(0)

        @pl.when(i == 0)
        def _():
            carry_ref[...] = jnp.zeros_like(carry_ref)

        row = lax.broadcasted_iota(jnp.int32, (ts, GROUP), 0)
        for g in range(n_grp):
            sl = slice(g * GROUP, (g + 1) * GROUP)
            uu = cc_ref[:, sl] * ch_ref[:, sl]
            p2 = carry_ref[6:7, sl]
            p1 = carry_ref[7:8, sl]
            u1 = _shift_down(uu, 1, [p1], row)
            u2 = _shift_down(uu, 2, [p2, p1], row)
            conv = w_ref[0:1, sl] * u2 + w_ref[1:2, sl] * u1 + w_ref[2:3, sl] * uu
            y = cb_ref[:, sl] * conv
            carry_ref[:, sl] = uu[ts - SUBLANES:ts, :]
            rg = lax.rsqrt(jnp.mean(y * y, axis=-1, keepdims=True) + EPS)
            y_ref[:, sl] = (y * rg * g_ref[:, sl]).astype(BF16)

    def col(j):
        return pl.BlockSpec((ts, D_CONV), lambda i, j=j: (i, j))

    small = lambda r: pl.BlockSpec((r, D_CONV), lambda i: (0, 0))
    return pl.pallas_call(
        body, grid=(s_len // ts,),
        in_specs=[col(0), col(1), col(2), small(3), small(1)],
        out_specs=col(0),
        out_shape=jax.ShapeDtypeStruct((s_len, 2 * D_CONV), BF16),
        scratch_shapes=[pltpu.VMEM((SUBLANES, D_CONV), F32)],
        compiler_params=_params("arbitrary"), name="conv_fwd",
    )(z, z, z, w_t, gain)


def _conv_bwd(dy, z, w_t, gain, *, ts=512):
    s_len = z.shape[0]
    n = s_len // ts
    n_grp = D_CONV // GROUP
    halo_blocks = ts // SUBLANES

    def body(dy_ref, cb_ref, cc_ref, ch_ref, hcc_ref, hch_ref, w_ref, g_ref,
             dz_ref, dw_ref, dg_ref, carry_ref, accw_ref, accg_ref):
        i = pl.program_id(0)
        first_tile = (n - 1 - i) == 0

        @pl.when(i == 0)
        def _():
            carry_ref[...] = jnp.zeros_like(carry_ref)
            accw_ref[...] = jnp.zeros_like(accw_ref)
            accg_ref[...] = jnp.zeros_like(accg_ref)

        row = lax.broadcasted_iota(jnp.int32, (ts, GROUP), 0)
        keep = jnp.where(first_tile, 0.0, 1.0)
        for g in range(n_grp):
            sl = slice(g * GROUP, (g + 1) * GROUP)
            cc = cc_ref[:, sl]
            ch = ch_ref[:, sl]
            cb = cb_ref[:, sl]
            uu = cc * ch
            p2 = hcc_ref[6:7, sl] * hch_ref[6:7, sl] * keep
            p1 = hcc_ref[7:8, sl] * hch_ref[7:8, sl] * keep
            u1 = _shift_down(uu, 1, [p1], row)
            u2 = _shift_down(uu, 2, [p2, p1], row)
            w0, w1, w2 = w_ref[0:1, sl], w_ref[1:2, sl], w_ref[2:3, sl]
            conv = w0 * u2 + w1 * u1 + w2 * uu
            y = cb * conv
            rg = lax.rsqrt(jnp.mean(y * y, axis=-1, keepdims=True) + EPS)
            yh = y * rg
            dyv = dy_ref[:, sl]
            accg_ref[:, sl] += _rowsum8(dyv * yh)
            dyn = dyv * g_ref[:, sl]
            dpre = rg * (dyn - yh * jnp.mean(dyn * yh, axis=-1, keepdims=True))
            dz_ref[:, sl] = (dpre * conv).astype(BF16)
            dconv = dpre * cb
            accw_ref[0:8, sl] += _rowsum8(dconv * u2)
            accw_ref[8:16, sl] += _rowsum8(dconv * u1)
            accw_ref[16:24, sl] += _rowsum8(dconv * uu)
            n0 = carry_ref[0:1, sl]
            n1 = carry_ref[1:2, sl]
            d1 = _shift_up(dconv, 1, [n0], row)
            d2 = _shift_up(dconv, 2, [n0, n1], row)
            duu = w2 * dconv + w1 * d1 + w0 * d2
            carry_ref[:, sl] = dconv[0:SUBLANES, :]
            dz_ref[:, D_CONV + g * GROUP:D_CONV + (g + 1) * GROUP] = (duu * ch).astype(BF16)
            dz_ref[:, 2 * D_CONV + g * GROUP:2 * D_CONV + (g + 1) * GROUP] = (duu * cc).astype(BF16)

        @pl.when(i == n - 1)
        def _():
            for k in range(3):
                dw_ref[k:k + 1, :] = jnp.sum(accw_ref[8 * k:8 * k + 8, :], axis=0, keepdims=True)
            dg_ref[...] = jnp.sum(accg_ref[...], axis=0, keepdims=True)

    def col(j):
        return pl.BlockSpec((ts, D_CONV), lambda i, j=j: (n - 1 - i, j))

    def halo(j):
        return pl.BlockSpec((SUBLANES, D_CONV),
                            lambda i, j=j: (jnp.maximum((n - 1 - i) * halo_blocks - 1, 0), j))

    small = lambda r: pl.BlockSpec((r, D_CONV), lambda i: (0, 0))
    return pl.pallas_call(
        body, grid=(n,),
        in_specs=[col(0), col(0), col(1), col(2), halo(1), halo(2), small(3), small(1)],
        out_specs=[pl.BlockSpec((ts, 3 * D_CONV), lambda i: (n - 1 - i, 0)), small(3), small(1)],
        out_shape=[jax.ShapeDtypeStruct((s_len, 6 * D_CONV), BF16),
                   jax.ShapeDtypeStruct((3, D_CONV), F32), jax.ShapeDtypeStruct((1, D_CONV), F32)],
        scratch_shapes=[pltpu.VMEM((SUBLANES, D_CONV), F32), pltpu.VMEM((24, D_CONV), F32),
                        pltpu.VMEM((SUBLANES, D_CONV), F32)],
        compiler_params=_params("arbitrary"), name="conv_bwd",
    )(dy, z, z, z, z, z, w_t, gain)


def _split3(v):
    hi = v.astype(BF16)
    r1 = v - hi.astype(F32)
    mid = r1.astype(BF16)
    lo = (r1 - mid.astype(F32)).astype(BF16)
    return jnp.concatenate([hi, mid, lo], axis=1)


def _tri_sum(tri, v):
    w = v.shape[1]
    dd = jnp.dot(tri, _split3(v), preferred_element_type=F32)
    return dd[:, :w] + dd[:, w:2 * w] + dd[:, 2 * w:]


def _strict_tri(upper):
    r = lax.broadcasted_iota(jnp.int32, (CHUNK, CHUNK), 0)
    c = lax.broadcasted_iota(jnp.int32, (CHUNK, CHUNK), 1)
    return jnp.where((c > r) if upper else (c < r), 1.0, 0.0).astype(BF16)


def _sigmoid(v):
    return 1.0 / (1.0 + jnp.exp(-v))


def _gla_fwd(z, alow, wgu, bg, gg, y_in, *, ts=512, comm=None):
    s_len = z.shape[0]
    nch = ts // CHUNK
    scale = DK ** -0.5

    def body(q_ref, k_ref, v_ref, og_ref, al_ref, wgu_ref, bg_ref, gg_ref, yin_ref,
             y_ref, o_ref, la_ref, st_ref, state_ref):
        del yin_ref
        i = pl.program_id(0)

        @pl.when(i == 0)
        def _():
            state_ref[...] = jnp.zeros_like(state_ref)

        pre = jnp.dot(al_ref[...].astype(BF16), wgu_ref[...], preferred_element_type=F32) + bg_ref[...]
        la_ref[...] = (jnp.minimum(pre, 0.0) - jnp.log(1.0 + jnp.exp(-jnp.abs(pre)))) * (1.0 / 16.0)
        upper = _strict_tri(True)
        ggv = gg_ref[...]

        def chunk(cl, carry):
            rows = pl.ds(pl.multiple_of(cl * CHUNK, CHUNK), CHUNK)
            la_c = la_ref[rows, :]
            e_dec = jnp.exp(_tri_sum(upper, la_c))
            dec = jnp.exp(jnp.sum(la_c, axis=0, keepdims=True))
            kd = k_ref[rows, :] * e_dec
            qs = q_ref[rows, :] * scale
            for h in range(HEADS):
                ks = slice(h * DK, (h + 1) * DK)
                vs = slice(h * DV, (h + 1) * DV)
                kv_t = lax.dot_general(v_ref[rows, vs].astype(BF16), kd[:, ks].astype(BF16),
                                       (((0,), (0,)), ((), ())), preferred_element_type=F32)
                st = state_ref[h] * dec[:, ks] + kv_t
                state_ref[h] = st
                st_ref[cl, h] = st
                o_h = lax.dot_general(qs[:, ks].astype(BF16), st.astype(BF16),
                                      (((1,), (1,)), ((), ())), preferred_element_type=F32)
                o_ref[rows, vs] = o_h
                ro = lax.rsqrt(jnp.mean(o_h * o_h, axis=-1, keepdims=True) + EPS)
                og_h = og_ref[rows, vs]
                y_ref[rows, vs] = (o_h * ro * ggv * (og_h * _sigmoid(og_h))).astype(BF16)
            return carry

        lax.fori_loop(0, nch, chunk, 0, unroll=2)

    def zcol(width, j):
        return pl.BlockSpec((ts, width), lambda i, j=j: (i, j))

    full = lambda shape: pl.BlockSpec(shape, lambda i: tuple(0 for _ in shape))
    return _call(
        body, grid=(s_len // ts,),
        in_specs=[zcol(512, 6), zcol(512, 7), zcol(1024, 4), zcol(1024, 5), zcol(LANES, 0),
                  full((LANES, 512)), full((1, 512)), full((1, DV)),
                  pl.BlockSpec(memory_space=pl.ANY)],
        out_specs=[zcol(1024, 1), zcol(1024, 0), zcol(512, 0),
                   pl.BlockSpec((nch, HEADS, DV, DK), lambda i: (i, 0, 0, 0))],
        out_shape=[jax.ShapeDtypeStruct((s_len, 2048), BF16), jax.ShapeDtypeStruct((s_len, 1024), F32),
                   jax.ShapeDtypeStruct((s_len, 512), F32),
                   jax.ShapeDtypeStruct((s_len // CHUNK, HEADS, DV, DK), F32)],
        scratch_shapes=[pltpu.VMEM((HEADS, DV, DK), F32)],
        aliases={8: 0},
        operands=(z, z, z, z, alow, wgu, bg, gg, y_in), name="gla_fwd", comm=comm)


def _gla_bwd(dy, z, o, la, st, alow, wgu, gg, dz_in, *, ts=512, comm=None):
    s_len = z.shape[0]
    n = s_len // ts
    nch = ts // CHUNK
    scale = DK ** -0.5

    def body(dy_ref, q_ref, k_ref, v_ref, og_ref, o_ref, la_ref, st_ref, stp_ref, al_ref, wgu_ref,
             gg_ref, dzin_ref, dz_ref, dal_ref, dwgu_ref, dbg_ref, dgg_ref,
             gt_ref, decn_ref, accw_ref, accb_ref, accg_ref, dla_ref):
        del dzin_ref
        i = pl.program_id(0)
        first_tile = (n - 1 - i) == 0

        @pl.when(i == 0)
        def _():
            gt_ref[...] = jnp.zeros_like(gt_ref)
            decn_ref[...] = jnp.ones_like(decn_ref)
            accw_ref[...] = jnp.zeros_like(accw_ref)
            accb_ref[...] = jnp.zeros_like(accb_ref)
            accg_ref[...] = jnp.zeros_like(accg_ref)

        upper = _strict_tri(True)
        lower = _strict_tri(False)
        ggv = gg_ref[...]
        keep = jnp.where(first_tile, 0.0, 1.0)

        def chunk(jrev, decn):
            cl = nch - 1 - jrev
            rows = pl.ds(pl.multiple_of(cl * CHUNK, CHUNK), CHUNK)
            la_c = la_ref[rows, :]
            e_dec = jnp.exp(_tri_sum(upper, la_c))
            dec = jnp.exp(jnp.sum(la_c, axis=0, keepdims=True))
            kd = k_ref[rows, :] * e_dec
            qs = q_ref[rows, :] * scale
            has_prev = jnp.where(cl > 0, 1.0, 0.0)
            prev_idx = jnp.maximum(cl - 1, 0)
            tt, dbe = [], []
            for h in range(HEADS):
                ks = slice(h * DK, (h + 1) * DK)
                vs = slice(h * DV, (h + 1) * DV)
                o_h = o_ref[rows, vs]
                og_h = og_ref[rows, vs]
                dy_h = dy_ref[rows, vs]
                ro = lax.rsqrt(jnp.mean(o_h * o_h, axis=-1, keepdims=True) + EPS)
                oh = o_h * ro
                sig = _sigmoid(og_h)
                sil = og_h * sig
                accg_ref[...] += _rowsum8(dy_h * oh * sil)
                dz_ref[rows, 2048 + h * DV:2048 + (h + 1) * DV] = (
                    dy_h * oh * ggv * sig * (1.0 + og_h * (1.0 - sig))).astype(BF16)
                don = dy_h * ggv * sil
                do = ro * (don - oh * jnp.mean(don * oh, axis=-1, keepdims=True))
                dob = do.astype(BF16)
                s_c = st_ref[cl, h]
                dqs = jnp.dot(dob, s_c.astype(BF16), preferred_element_type=F32)
                dz_ref[rows, ks] = (dqs * scale).astype(BF16)
                gt = gt_ref[h] * decn[:, ks] + lax.dot_general(
                    dob, qs[:, ks].astype(BF16), (((0,), (0,)), ((), ())), preferred_element_type=F32)
                gt_ref[h] = gt
                gb = gt.astype(BF16)
                kd_h = kd[:, ks]
                dkd = jnp.dot(v_ref[rows, vs].astype(BF16), gb, preferred_element_type=F32)
                dz_ref[rows, 1024 + h * DV:1024 + (h + 1) * DV] = lax.dot_general(
                    kd_h.astype(BF16), gb, (((1,), (1,)), ((), ())),
                    preferred_element_type=F32).astype(BF16)
                s_prev = has_prev * st_ref[prev_idx, h] + (1.0 - has_prev) * keep * stp_ref[0, h]
                ddec = jnp.sum(gt * s_prev, axis=0, keepdims=True)
                dz_ref[rows, 512 + h * DK:512 + (h + 1) * DK] = (dkd * e_dec[:, ks]).astype(BF16)
                tt.append(dkd * kd_h)
                dbe.append(ddec * dec[:, ks])
            dla_ref[rows, :] = (_tri_sum(lower, jnp.concatenate(tt, axis=1))
                                + jnp.concatenate(dbe, axis=1))
            return dec

        decn_ref[0:1, :] = lax.fori_loop(0, nch, chunk, decn_ref[0:1, :], unroll=2)

        dpre = dla_ref[...] * (1.0 / 16.0) * (1.0 - jnp.exp(16.0 * la_ref[...]))
        accb_ref[...] += _rowsum8(dpre)
        dpb = dpre.astype(BF16)
        accw_ref[...] += lax.dot_general(al_ref[...].astype(BF16), dpb, (((0,), (0,)), ((), ())),
                                         preferred_element_type=F32)
        dal_ref[...] = lax.dot_general(dpb, wgu_ref[...], (((1,), (1,)), ((), ())),
                                       preferred_element_type=F32)

        @pl.when(i == n - 1)
        def _():
            dwgu_ref[...] = accw_ref[...]
            dbg_ref[...] = jnp.sum(accb_ref[...], axis=0, keepdims=True)
            dgg_ref[...] = jnp.sum(accg_ref[...], axis=0, keepdims=True)

    def zcol(width, j):
        return pl.BlockSpec((ts, width), lambda i, j=j: (n - 1 - i, j))

    full = lambda shape: pl.BlockSpec(shape, lambda i: tuple(0 for _ in shape))
    return _call(
        body, grid=(n,),
        in_specs=[zcol(1024, 1), zcol(512, 6), zcol(512, 7), zcol(1024, 4), zcol(1024, 5),
                  zcol(1024, 0), zcol(512, 0),
                  pl.BlockSpec((nch, HEADS, DV, DK), lambda i: (n - 1 - i, 0, 0, 0)),
                  pl.BlockSpec((1, HEADS, DV, DK),
                               lambda i: (jnp.maximum((n - 1 - i) * nch - 1, 0), 0, 0, 0)),
                  zcol(LANES, 0), full((LANES, 512)), full((1, DV)),
                  pl.BlockSpec(memory_space=pl.ANY)],
        out_specs=[zcol(3072, 1), zcol(LANES, 0), full((LANES, 512)), full((1, 512)), full((1, DV))],
        out_shape=[jax.ShapeDtypeStruct((s_len, 6144), BF16), jax.ShapeDtypeStruct((s_len, LANES), F32),
                   jax.ShapeDtypeStruct((LANES, 512), F32), jax.ShapeDtypeStruct((1, 512), F32),
                   jax.ShapeDtypeStruct((1, DV), F32)],
        scratch_shapes=[pltpu.VMEM((HEADS, DV, DK), F32), pltpu.VMEM((SUBLANES, 512), F32),
                        pltpu.VMEM((LANES, 512), F32), pltpu.VMEM((SUBLANES, 512), F32),
                        pltpu.VMEM((SUBLANES, DV), F32), pltpu.VMEM((ts, 512), F32)],
        aliases={12: 0},
        operands=(dy, z, z, z, z, o, la, st, st, alow, wgu, gg, dz_in), name="gla_bwd", comm=comm)


def _adamw(w, g, m, v, *, name):
    rows, cols = w.shape
    tr, tc = _tile(rows, cols)

    def body(w_ref, g_ref, m_ref, v_ref, go_ref, d_ref, nm_ref, nv_ref):
        gv = g_ref[...]
        go_ref[...] = gv
        m2 = ADAM_B1 * m_ref[...] + (1.0 - ADAM_B1) * gv
        v2 = ADAM_B2 * v_ref[...] + (1.0 - ADAM_B2) * jnp.square(gv)
        m_hat = m2 / (1.0 - ADAM_B1 ** ADAM_STEP)
        v_hat = v2 / (1.0 - ADAM_B2 ** ADAM_STEP)
        d_ref[...] = -ADAM_LR * (m_hat / (jnp.sqrt(v_hat) + ADAM_EPS) + ADAM_WD * w_ref[...])
        nm_ref[...] = m2
        nv_ref[...] = v2

    blk = pl.BlockSpec((tr, tc), lambda i, j: (i, j))
    shp = jax.ShapeDtypeStruct((rows, cols), F32)
    return pl.pallas_call(
        body, grid=(rows // tr, cols // tc), in_specs=[blk] * 4, out_specs=[blk] * 4, out_shape=[shp] * 4,
        compiler_params=_params("parallel", "parallel"), name=name,
    )(w, g, m, v)


def _my_place():
    return lax.axis_index("x"), lax.axis_index("y"), lax.axis_index("c")


def _flip(v, bit):
    return 1 - v if bit else v


def _allgather_small(buf, *, reduce, name):
    rows = buf.shape[0]

    def body(in_ref, out_ref, gat_ref, send_sems, recv_sems):
        x, y, c = _my_place()
        me = 4 * x + 2 * y + c
        gat_ref[me] = in_ref[...]
        copies = []
        for m in range(1, N_DEV):
            peer = (_flip(x, m & 4), _flip(y, m & 2), _flip(c, m & 1))
            cp = pltpu.make_async_remote_copy(
                src_ref=in_ref, dst_ref=gat_ref.at[me],
                send_sem=send_sems.at[m - 1], recv_sem=recv_sems.at[m - 1],
                device_id=peer, device_id_type=MESH)
            cp.start()
            copies.append(cp)
        for m in range(1, N_DEV):
            px, py, pc = _flip(x, m & 4), _flip(y, m & 2), _flip(c, m & 1)
            src_slot = gat_ref.at[4 * px + 2 * py + pc]
            pltpu.make_async_remote_copy(
                src_ref=src_slot, dst_ref=src_slot,
                send_sem=send_sems.at[m - 1], recv_sem=recv_sems.at[m - 1],
                device_id=(px, py, pc), device_id_type=MESH).wait_recv()
        for cp in copies:
            cp.wait_send()
        if reduce:
            tot = gat_ref[0]
            for d in range(1, N_DEV):
                tot = tot + gat_ref[d]
            out_ref[...] = tot
        else:
            out_ref[...] = gat_ref[...]

    out_shape = (rows, LANES) if reduce else (N_DEV, rows, LANES)
    return pl.pallas_call(
        body,
        in_specs=[pl.BlockSpec(memory_space=pltpu.VMEM)],
        out_specs=pl.BlockSpec(memory_space=pltpu.VMEM),
        out_shape=jax.ShapeDtypeStruct(out_shape, F32),
        scratch_shapes=[pltpu.VMEM((N_DEV, rows, LANES), F32),
                        pltpu.SemaphoreType.DMA((N_DEV - 1,)), pltpu.SemaphoreType.DMA((N_DEV - 1,))],
        compiler_params=pltpu.CompilerParams(has_side_effects=True),
        name=name,
    )(buf)


def _cast_into(shard, chip_core, *, name):
    rows, cols = shard.shape
    tr, tc = _tile(rows, cols)

    def body(cc_ref, s_ref, o_ref):
        del cc_ref
        o_ref[...] = s_ref[...].astype(BF16)

    grid_spec = pltpu.PrefetchScalarGridSpec(
        num_scalar_prefetch=1, grid=(rows // tr, cols // tc),
        in_specs=[pl.BlockSpec((tr, tc), lambda r, q, cc: (r, q))],
        out_specs=pl.BlockSpec((None, tr, tc), lambda r, q, cc: (cc[0], r, q)))
    return pl.pallas_call(
        body, grid_spec=grid_spec, out_shape=jax.ShapeDtypeStruct((N_CHIPS, rows, cols), BF16),
        compiler_params=_params("arbitrary", "arbitrary"), name=name,
    )(chip_core, shard)


def _remote(src, dst, send_sems, recv_sems, k, device):
    return pltpu.make_async_remote_copy(src_ref=src, dst_ref=dst, send_sem=send_sems.at[k],
                                        recv_sem=recv_sems.at[k], device_id=device, device_id_type=MESH)


def _col_half(ref, h, *lead, rows=None):
    hc = ref.shape[-1] // 2
    mid = (slice(None),) * (len(ref.shape) - 2 - len(lead))
    row_sel = slice(None) if rows is None else pl.ds(rows[0], rows[1])
    return ref.at[tuple(lead) + mid + (row_sel, pl.ds(h * hc, hc))]


def _gather_comm(bufs, rows=None):
    n_w, n_m = len(bufs), len(CHIP_MASKS)
    rows = rows or [None] * n_w

    def first(c_ins, c_outs, ss, rs):
        x, y, c = _my_place()
        chip = 2 * x + y
        for w in range(n_w):
            mine = _col_half(c_outs[w], c, chip, rows=rows[w])
            for mi, (mx, my) in enumerate(CHIP_MASKS):
                _remote(mine, mine, ss, rs, w * n_m + mi, (_flip(x, mx), _flip(y, my), c)).start()

    def last(c_ins, c_outs, ss, rs):
        x, y, c = _my_place()
        chip = 2 * x + y
        for w in range(n_w):
            for mi, (mx, my) in enumerate(CHIP_MASKS):
                k = w * n_m + mi
                px, py = _flip(x, mx), _flip(y, my)
                landed = _col_half(c_outs[w], c, 2 * px + py, rows=rows[w])
                _remote(landed, landed, ss, rs, k, (px, py, c)).wait_recv()
                _remote(landed, landed, ss, rs, n_w * n_m + k, (x, y, 1 - c)).start()
        for w in range(n_w):
            mine = _col_half(c_outs[w], c, chip, rows=rows[w])
            for mi, (mx, my) in enumerate(CHIP_MASKS):
                k = w * n_m + mi
                px, py = _flip(x, mx), _flip(y, my)
                theirs = _col_half(c_outs[w], 1 - c, 2 * px + py, rows=rows[w])
                _remote(theirs, theirs, ss, rs, n_w * n_m + k, (x, y, 1 - c)).wait_recv()
                _remote(mine, mine, ss, rs, k, (px, py, c)).wait_send()
                _remote(mine, mine, ss, rs, n_w * n_m + k, (x, y, 1 - c)).wait_send()

    return _Comm(ins=bufs, outs=[jax.ShapeDtypeStruct(b.shape, b.dtype) for b in bufs],
                 aliases={w: w for w in range(n_w)}, n_sems=2 * n_w * n_m, first=first, last=last)


def _swap_comm(grads):
    n_w = len(grads)

    def copy(c_ins, c_outs, ss, rs, w):
        x, y, c = _my_place()
        return _remote(_col_half(c_ins[w], 1 - c), c_outs[w], ss, rs, w, (x, y, 1 - c))

    def first(c_ins, c_outs, ss, rs):
        for w in range(n_w):
            copy(c_ins, c_outs, ss, rs, w).start()

    def last(c_ins, c_outs, ss, rs):
        for w in range(n_w):
            copy(c_ins, c_outs, ss, rs, w).wait()

    return _Comm(ins=grads,
                 outs=[jax.ShapeDtypeStruct(g.shape[:2] + (g.shape[2] // 2,), g.dtype) for g in grads],
                 aliases={}, n_sems=n_w, first=first, last=last)


def _add_own_half(g, other, chip_core, *, name):
    n_chip, rows, hc = other.shape
    tr, tc = _tile(rows, hc)
    per_half = hc // tc

    def body(cc_ref, g_ref, o_ref, out_ref):
        del cc_ref
        out_ref[...] = (g_ref[...].astype(F32) + o_ref[...].astype(F32)).astype(BF16)

    grid_spec = pltpu.PrefetchScalarGridSpec(
        num_scalar_prefetch=1, grid=(n_chip, rows // tr, per_half),
        in_specs=[pl.BlockSpec((None, tr, tc), lambda j, r, q, cc: (j, r, cc[1] * per_half + q)),
                  pl.BlockSpec((None, tr, tc), lambda j, r, q, cc: (j, r, q))],
        out_specs=pl.BlockSpec((None, tr, tc), lambda j, r, q, cc: (j, r, q)))
    return pl.pallas_call(
        body, grid_spec=grid_spec, out_shape=jax.ShapeDtypeStruct((n_chip, rows, hc), BF16),
        compiler_params=_params("parallel", "parallel", "parallel"), name=name,
    )(chip_core, g, other)


def _exchange_comm(pieces):
    n_w, n_m = len(pieces), len(CHIP_MASKS)

    def copies(c_ins, c_outs, ss, rs):
        x, y, c = _my_place()
        chip = 2 * x + y
        for w in range(n_w):
            for mi, (mx, my) in enumerate(CHIP_MASKS):
                px, py = _flip(x, mx), _flip(y, my)
                send = _remote(c_ins[w].at[2 * px + py], c_outs[w].at[chip], ss, rs, w * n_m + mi, (px, py, c))
                landed = c_outs[w].at[2 * px + py]
                yield send, _remote(landed, landed, ss, rs, w * n_m + mi, (px, py, c))

    def first(c_ins, c_outs, ss, rs):
        for send, _ in copies(c_ins, c_outs, ss, rs):
            send.start()

    def last(c_ins, c_outs, ss, rs):
        for send, arrival in copies(c_ins, c_outs, ss, rs):
            arrival.wait_recv()
            send.wait_send()

    return _Comm(ins=pieces, outs=[jax.ShapeDtypeStruct(p.shape, p.dtype) for p in pieces],
                 aliases={}, n_sems=n_w * n_m, first=first, last=last)


def _sum_chips(own, landed, chip_core, *, name):
    n_chip, rows, hc = own.shape
    tr, tc = _tile(rows, hc)
    per_half = hc // tc

    def body(cc_ref, o_ref, l1_ref, l2_ref, l3_ref, out_ref):
        del cc_ref
        out_ref[...] = ((o_ref[...].astype(F32) + l1_ref[...].astype(F32))
                        + l2_ref[...].astype(F32)) + l3_ref[...].astype(F32)

    def slot(k):
        return pl.BlockSpec((None, tr, tc), lambda r, q, cc, k=k: ((cc[0] + k) % n_chip, r, q))

    grid_spec = pltpu.PrefetchScalarGridSpec(
        num_scalar_prefetch=1, grid=(rows // tr, per_half),
        in_specs=[slot(0), slot(1), slot(2), slot(3)],
        out_specs=pl.BlockSpec((tr, tc), lambda r, q, cc: (r, cc[1] * per_half + q)))
    return pl.pallas_call(
        body, grid_spec=grid_spec, out_shape=jax.ShapeDtypeStruct((rows, 2 * hc), F32),
        compiler_params=_params("arbitrary", "arbitrary"), name=name,
    )(chip_core, own, landed, landed, landed)


def _join_comm(halves):
    n_w = len(halves)

    def first(c_ins, c_outs, ss, rs):
        x, y, c = _my_place()
        for w in range(n_w):
            mine = _col_half(c_outs[w], c)
            _remote(mine, mine, ss, rs, w, (x, y, 1 - c)).start()

    def last(c_ins, c_outs, ss, rs):
        x, y, c = _my_place()
        for w in range(n_w):
            theirs = _col_half(c_outs[w], 1 - c)
            _remote(theirs, theirs, ss, rs, w, (x, y, 1 - c)).wait()

    return _Comm(ins=halves, outs=[jax.ShapeDtypeStruct(h.shape, h.dtype) for h in halves],
                 aliases={w: w for w in range(n_w)}, n_sems=n_w, first=first, last=last)


def _standalone(comm, *, name):
    def body(o_ref):
        o_ref[...] = jnp.zeros_like(o_ref)

    return _call(body, grid=(1,), in_specs=[],
                 out_specs=[pl.BlockSpec((SUBLANES, LANES), lambda i: (0, 0))],
                 out_shape=[jax.ShapeDtypeStruct((SUBLANES, LANES), F32)], operands=(), name=name,
                 comm=comm)[1:]


def _pack(pieces):
    flat, spans, off = [], [], 0
    for p in pieces:
        v = p.reshape(-1).astype(F32)
        pad = (-v.shape[0]) % LANES
        if pad:
            v = jnp.concatenate([v, jnp.zeros((pad,), F32)])
        spans.append((off, p.size))
        off += v.shape[0]
        flat.append(v)
    tail = (-off) % (SUBLANES * LANES)
    if tail:
        flat.append(jnp.zeros((tail,), F32))
    return jnp.concatenate(flat).reshape(-1, LANES), spans


def _unpack(buf, span, shape):
    off, size = span
    return buf.reshape(-1)[off:off + size].reshape(shape)


def kernel(x, norm1_g, w_in, w_gate_up, b_gate, conv_w, conv_norm_g, gla_norm_g, w_out, norm2_g, w_ff1, w_ff2, norm_f_g, loss_target, m_norm1_g, m_w_in, m_w_gate_up, m_b_gate, m_conv_w, m_conv_norm_g, m_gla_norm_g, m_w_out, m_norm2_g, m_w_ff1, m_w_ff2, m_norm_f_g, v_norm1_g, v_w_in, v_w_gate_up, v_b_gate, v_conv_w, v_conv_norm_g, v_gla_norm_g, v_w_out, v_norm2_g, v_w_ff1, v_w_ff2, v_norm_f_g):
    xs = x[0]
    target = loss_target[0]
    s_len, d = xs.shape
    d_in = w_in.shape[2] * N_CHIPS
    d_main = d_in - GATE_RANK
    d_ff = w_ff1.shape[2] * N_CHIPS
    cx, cy, cc = _my_place()
    chip = 2 * cx + cy
    chip_core = jnp.stack([chip, cc]).astype(jnp.int32)
    n_ff = d_ff // N_CHIPS
    norm_f = norm_f_g.reshape(1, d)

    n_sh = d_in // N_CHIPS
    wi_buf = _cast_into(w_in[0].T, chip_core, name="cast_w_in")
    wo_buf = _cast_into(w_out[0], chip_core, name="cast_w_out")
    w1_buf = _cast_into(w_ff1[0], chip_core, name="cast_w_ff1")
    w2_buf = _cast_into(w_ff2[0], chip_core, name="cast_w_ff2")

    small_w, spans_w = _pack([w_gate_up[0], conv_w[0]])
    small_all = _allgather_small(small_w, reduce=False, name="gather_small_weights")
    chips_first = [small_all[2 * j] for j in range(N_CHIPS)]
    wgu_full = jnp.concatenate(
        [_unpack(b, spans_w[0], w_gate_up.shape[1:]) for b in chips_first], axis=1)
    convw_full = jnp.concatenate(
        [_unpack(b, spans_w[1], conv_w.shape[1:]) for b in chips_first], axis=0)
    wgu_pad = jnp.concatenate(
        [wgu_full, jnp.zeros((LANES - GATE_RANK, wgu_full.shape[1]), F32)], axis=0).astype(BF16)
    convw_t = convw_full.T

    u, wi_buf = _rms_fwd(xs, norm1_g, name="norm1_fwd", comm=_gather_comm([wi_buf]))
    wi_t = wi_buf.reshape(d_in, d)
    wg_t = jnp.concatenate([wi_t[d_main:], jnp.zeros((LANES - GATE_RANK, d), BF16)], axis=0)
    z, wo_buf, w1_buf = _matmul(u, wi_t, tb=True, tm=1024, tn=1024, tk=d, out_dtypes=[F32], n_dim=d_main,
                                name="in_proj",
                                comm=_gather_comm([wo_buf, w1_buf], rows=[None, (0, d // 2)]))
    wo_full = wo_buf.reshape(d, d)
    (alow,) = _matmul(u, wg_t, tb=True, tm=1024, tn=LANES, tk=d, out_dtypes=[F32], name="in_proj_gate")
    y0 = _conv_fwd(z, convw_t, conv_norm_g)
    y, o, la, st, w1_cm = _gla_fwd(z, alow, wgu_pad, b_gate, gla_norm_g, y0,
                                   comm=_gather_comm([w1_buf], rows=[(d // 2, d // 2)]))
    (x2,) = _matmul(y, wo_full, tm=1024, tn=1024, tk=d, out_dtypes=[F32], extras=(xs,),
                    epilogue=_add_epilogue, name="out_proj")
    (h,) = _rms_fwd(x2, norm2_g, name="norm2_fwd")
    a, p, w2_buf = _matmul(
        h, w1_cm, tm=1024, tn=1024, tk=d, out_dtypes=[F32, BF16], n_dim=d_ff,
        b_spec=pl.BlockSpec((None, d, 1024), lambda i, j, k: (j // 2, 0, j % 2)),
        epilogue=lambda acc: (acc, jnp.square(jnp.maximum(acc, 0.0))), name="ff1",
        comm=_gather_comm([w2_buf]))
    w2_full = w2_buf.reshape(d_ff, d)
    (x3,) = _matmul(p, w2_full, tm=1024, tn=1024, tk=2048, out_dtypes=[F32], extras=(x2,),
                    epilogue=_add_epilogue, name="ff2")
    dx3, dx3b, g_normf, loss_part = _loss_head(x3, norm_f, target)

    (da,) = _matmul(dx3b, w2_full, tb=True, tm=1024, tn=1024, tk=d, out_dtypes=[BF16], extras=(a,),
                    epilogue=lambda acc, av: (acc * (2.0 * jnp.maximum(av, 0.0)),), name="ff2_dx")
    (g_w2,) = _matmul(p, dx3b, ta=True, tm=1024, tn=d, tk=1024, out_dtypes=[BF16], name="ff2_dw")
    (g_w1,) = _matmul(
        h, da, ta=True, tm=1024, tn=n_ff, tk=1024, out_dtypes=[BF16], name="ff1_dw",
        out_shapes=[jax.ShapeDtypeStruct((N_CHIPS, d, n_ff), BF16)],
        out_specs=[pl.BlockSpec((None, 1024, n_ff), lambda i, j, k: (j, i, 0))])
    g_w2 = g_w2.reshape(N_CHIPS, n_ff, d)
    dh, t_w1, t_w2 = _matmul(
        da, w1_cm, tb=True, tm=1024, tn=1024, tk=2048, out_dtypes=[F32], n_dim=d,
        b_spec=pl.BlockSpec((None, 1024, 2048), lambda i, j, k: (k, j, 0)), name="ff1_dx",
        comm=_swap_comm([g_w1, g_w2]))
    p_w1 = _add_own_half(g_w1, t_w1, chip_core, name="pre_reduce_w_ff1")
    p_w2 = _add_own_half(g_w2, t_w2, chip_core, name="pre_reduce_w_ff2")
    dx2, g_norm2 = _rms_bwd(dh, x2, norm2_g, dx3, name="norm2_bwd")
    (dy,) = _matmul(dx2, wo_full, tb=True, tm=1024, tn=1024, tk=d, out_dtypes=[F32], name="out_proj_dx")
    (g_wo,) = _matmul(y, dx2, ta=True, tm=1024, tn=d, tk=1024, out_dtypes=[BF16], name="out_proj_dw")
    g_wo = g_wo.reshape(N_CHIPS, d // N_CHIPS, d)
    dz0, g_convw_t, g_convg = _conv_bwd(dy, z, convw_t, conv_norm_g)
    dz, dalow, g_wgu_pad, g_bg, g_gg, l_w2 = _gla_bwd(dy, z, o, la, st, alow, wgu_pad, gla_norm_g, dz0,
                                                      comm=_exchange_comm([p_w2]))
    m_w2 = _sum_chips(p_w2, l_w2, chip_core, name="reduce_w_ff2")
    g_wi_t, l_w1 = _matmul(dz, u, ta=True, tm=1024, tn=d, tk=1024, out_dtypes=[BF16], name="in_proj_dw",
                           out_shapes=[jax.ShapeDtypeStruct((d_in, d), BF16)],
                           comm=_exchange_comm([p_w1]))
    (g_wi_t,) = _matmul(dalow, u, ta=True, tm=LANES, tn=d, tk=1024, out_dtypes=[BF16],
                        epilogue=lambda acc: (acc[:GATE_RANK],), into=g_wi_t, name="in_proj_gate_dw",
                        out_shapes=[jax.ShapeDtypeStruct((d_in, d), BF16)],
                        out_specs=[pl.BlockSpec((GATE_RANK, d), lambda i, j, k: (d_main // GATE_RANK, 0))])
    m_w1 = _sum_chips(p_w1, l_w1, chip_core, name="reduce_w_ff1")
    g_wi = g_wi_t.reshape(N_CHIPS, n_sh, d)
    du_gate, t_wi, t_wo = _matmul(dalow, wg_t, tm=1024, tn=1024, tk=LANES, out_dtypes=[F32],
                                  name="in_proj_gate_dx", comm=_swap_comm([g_wi, g_wo]))
    p_wi = _add_own_half(g_wi, t_wi, chip_core, name="pre_reduce_w_in")
    p_wo = _add_own_half(g_wo, t_wo, chip_core, name="pre_reduce_w_out")
    du, l_wi, l_wo, m_w1, m_w2 = _matmul(
        dz, wi_t, tm=1024, tn=1024, tk=2048, out_dtypes=[F32], n_dim=d, extras=(du_gate,),
        epilogue=_add_epilogue, name="in_proj_dx",
        comm=[_exchange_comm([p_wi, p_wo]), _join_comm([m_w1, m_w2])])
    m_wi = _sum_chips(p_wi, l_wi, chip_core, name="reduce_w_in")
    m_wo = _sum_chips(p_wo, l_wo, chip_core, name="reduce_w_out")
    grad_x, g_norm1 = _rms_bwd(du, xs, norm1_g, dx2, name="norm1_bwd")
    m_wi, m_wo = _standalone(_join_comm([m_wi, m_wo]), name="join_w_in_w_out")
    g_big = [m_wi, m_wo, m_w1, m_w2]

    small_g, spans_g = _pack([g_norm1, g_wgu_pad[:GATE_RANK], g_bg, g_convw_t, g_convg, g_gg, g_norm2,
                              g_normf, loss_part[:, :1]])
    tot = _allgather_small(small_g, reduce=True, name="reduce_small_grads")
    t_norm1 = _unpack(tot, spans_g[0], (1, d))
    t_wgu = _unpack(tot, spans_g[1], (GATE_RANK, HEADS * DK))
    t_bg = _unpack(tot, spans_g[2], (1, HEADS * DK))
    t_convw = _unpack(tot, spans_g[3], (3, D_CONV)).T
    t_convg = _unpack(tot, spans_g[4], (1, D_CONV))
    t_gg = _unpack(tot, spans_g[5], (1, DV))
    t_norm2 = _unpack(tot, spans_g[6], (1, d))
    t_normf = _unpack(tot, spans_g[7], (1, d))
    loss = _unpack(tot, spans_g[8], ())
    n_gu = w_gate_up.shape[2]
    n_cw = conv_w.shape[1]
    t_wgu = lax.dynamic_slice(t_wgu, (0, chip * n_gu), (GATE_RANK, n_gu))
    t_convw = lax.dynamic_slice(t_convw, (chip * n_cw, 0), (n_cw, 3))

    order = ["norm1_g", "w_in", "w_gate_up", "b_gate", "conv_w", "conv_norm_g", "gla_norm_g", "w_out",
             "norm2_g", "w_ff1", "w_ff2", "norm_f_g"]
    weights = dict(norm1_g=norm1_g, w_in=w_in, w_gate_up=w_gate_up, b_gate=b_gate, conv_w=conv_w,
                   conv_norm_g=conv_norm_g, gla_norm_g=gla_norm_g, w_out=w_out, norm2_g=norm2_g,
                   w_ff1=w_ff1, w_ff2=w_ff2, norm_f_g=norm_f_g)
    moms = dict(norm1_g=m_norm1_g, w_in=m_w_in, w_gate_up=m_w_gate_up, b_gate=m_b_gate, conv_w=m_conv_w,
                conv_norm_g=m_conv_norm_g, gla_norm_g=m_gla_norm_g, w_out=m_w_out, norm2_g=m_norm2_g,
                w_ff1=m_w_ff1, w_ff2=m_w_ff2, norm_f_g=m_norm_f_g)
    vels = dict(norm1_g=v_norm1_g, w_in=v_w_in, w_gate_up=v_w_gate_up, b_gate=v_b_gate, conv_w=v_conv_w,
                conv_norm_g=v_conv_norm_g, gla_norm_g=v_gla_norm_g, w_out=v_w_out, norm2_g=v_norm2_g,
                w_ff1=v_w_ff1, w_ff2=v_w_ff2, norm_f_g=v_norm_f_g)
    grads2d = dict(norm1_g=t_norm1, w_in=g_big[0], w_gate_up=t_wgu, b_gate=t_bg, conv_w=t_convw,
                   conv_norm_g=t_convg, gla_norm_g=t_gg, w_out=g_big[1], norm2_g=t_norm2,
                   w_ff1=g_big[2], w_ff2=g_big[3], norm_f_g=t_normf)
    out_g, out_d, out_m, out_v = [], [], [], []
    for nm in order:
        w = weights[nm]
        g2 = grads2d[nm]
        if nm == "w_in":
            to2d, back = (lambda t: t[0].T), (lambda t: t.T.reshape(w.shape))
        else:
            to2d, back = (lambda t: t.reshape(g2.shape)), (lambda t: t.reshape(w.shape))
        res = _adamw(to2d(w), g2, to2d(moms[nm]), to2d(vels[nm]), name="adamw_" + nm)
        for lst, r in zip((out_g, out_d, out_m, out_v), res):
            lst.append(back(r))
    return (loss, grad_x.reshape(x.shape), *out_g, *out_d, *out_m, *out_v)
```

```python
import functools

import jax
import jax.numpy as jnp
from jax import lax
from jax.experimental import pallas as pl
from jax.experimental.pallas import tpu as pltpu

F32 = jnp.float32
BF16 = jnp.bfloat16
MESH = pl.DeviceIdType.MESH

EPS = 1e-6
CHUNK = 64
HEADS = 4
DK = 128
DV = 256
D_CONV = 1024
GROUP = 128
GATE_RANK = 16
LANES = 128
SUBLANES = 8
N_CHIPS = 4
N_DEV = 8
CHIP_MASKS = ((1, 0), (0, 1), (1, 1))

ADAM_LR = 0.001
ADAM_B1 = 0.9
ADAM_B2 = 0.999
ADAM_EPS = 1e-08
ADAM_WD = 0.01
ADAM_STEP = 10

VMEM_LIMIT = 56 * 1024 * 1024


def _params(*sem):
    return pltpu.CompilerParams(dimension_semantics=tuple(sem), vmem_limit_bytes=VMEM_LIMIT)


def _rowsum8(v):
    r, c = v.shape
    return jnp.sum(v.reshape(r // SUBLANES, SUBLANES, c), axis=0)


def _tile(rows, cols):
    for cand in (256, 128, 64, 32, 16, 8):
        if rows % cand == 0 and rows > cand:
            return cand, cols
    if rows * cols * 4 > (2 << 20) and cols % 256 == 0:
        return rows, 256
    return rows, cols


class _Comm:
    def __init__(self, ins, outs, aliases, n_sems, first, last, mid=None):
        self.ins = list(ins)
        self.outs = list(outs)
        self.aliases = dict(aliases)
        self.n_sems = n_sems
        self.first = first
        self.mid = mid
        self.last = last


def _call(body, *, grid, in_specs, out_specs, out_shape, operands, name, scratch_shapes=(), sem=None,
          aliases=None, comm=None):
    aliases = dict(aliases or {})
    if comm is None:
        return pl.pallas_call(
            body, grid=grid, in_specs=list(in_specs), out_specs=list(out_specs), out_shape=list(out_shape),
            scratch_shapes=list(scratch_shapes), input_output_aliases=aliases,
            compiler_params=_params(*(sem or ("arbitrary",) * len(grid))), name=name)(*operands)
    comms = list(comm) if isinstance(comm, (list, tuple)) else [comm]
    n_in, n_out, n_scr = len(in_specs), len(out_specs), len(scratch_shapes)
    c_ins_all = [a for cm in comms for a in cm.ins]
    c_outs_all = [o for cm in comms for o in cm.outs]
    n_ci, n_co = len(c_ins_all), len(c_outs_all)

    def full_body(*refs):
        ins = refs[:n_in]
        o0 = n_in + n_ci
        outs = refs[o0:o0 + n_out]
        s0 = o0 + n_out + n_co
        scr = refs[s0:s0 + n_scr]
        sems = refs[s0 + n_scr:]
        parts, i_at, o_at = [], n_in, o0 + n_out
        for q, cm in enumerate(comms):
            parts.append((cm, refs[i_at:i_at + len(cm.ins)], refs[o_at:o_at + len(cm.outs)],
                          sems[2 * q], sems[2 * q + 1]))
            i_at += len(cm.ins)
            o_at += len(cm.outs)
        step = functools.reduce(lambda acc, ig: acc * ig[1] + pl.program_id(ig[0]), enumerate(grid), 0)
        n_steps = functools.reduce(lambda acc, g: acc * g, grid, 1)
        mid_step = (3 * n_steps) // 4

        @pl.when(step == 0)
        def _():
            for cm, c_ins, c_outs, ss, rs in parts:
                cm.first(c_ins, c_outs, ss, rs)

        if 0 < mid_step < n_steps - 1:
            @pl.when(step == mid_step)
            def _():
                for cm, c_ins, c_outs, ss, rs in parts:
                    if cm.mid is not None:
                        cm.mid(c_ins, c_outs, ss, rs)

        body(*ins, *outs, *scr)

        @pl.when(step == n_steps - 1)
        def _():
            for cm, c_ins, c_outs, ss, rs in parts:
                if cm.mid is not None and not 0 < mid_step < n_steps - 1:
                    cm.mid(c_ins, c_outs, ss, rs)
                cm.last(c_ins, c_outs, ss, rs)

    any_spec = pl.BlockSpec(memory_space=pl.ANY)
    i_at, o_at, sem_shapes = n_in, n_out, []
    for cm in comms:
        for i_in, i_out in cm.aliases.items():
            aliases[i_at + i_in] = o_at + i_out
        i_at += len(cm.ins)
        o_at += len(cm.outs)
        sem_shapes += [pltpu.SemaphoreType.DMA((cm.n_sems,)), pltpu.SemaphoreType.DMA((cm.n_sems,))]
    return pl.pallas_call(
        full_body, grid=grid,
        in_specs=list(in_specs) + [any_spec] * n_ci,
        out_specs=list(out_specs) + [any_spec] * n_co,
        out_shape=list(out_shape) + c_outs_all,
        scratch_shapes=list(scratch_shapes) + sem_shapes,
        input_output_aliases=aliases,
        compiler_params=pltpu.CompilerParams(dimension_semantics=("arbitrary",) * len(grid),
                                             vmem_limit_bytes=VMEM_LIMIT, has_side_effects=True),
        name=name)(*operands, *c_ins_all)


def _matmul(a, b, *, ta=False, tb=False, tm, tn, tk, out_dtypes, name, extras=(), epilogue=None,
            out_shapes=None, out_specs=None, b_spec=None, n_dim=None, into=None, comm=None):
    n_into = 0 if into is None else 1
    if ta:
        k_dim, m_dim = a.shape
    else:
        m_dim, k_dim = a.shape
    if n_dim is None:
        n_dim = b.shape[0] if tb else b.shape[1]
        assert (b.shape[1] if tb else b.shape[0]) == k_dim
    assert m_dim % tm == 0 and n_dim % tn == 0 and k_dim % tk == 0, (name, a.shape, b.shape)
    nk = k_dim // tk
    n_ex, n_out = len(extras), len(out_dtypes)
    dims = (((0 if ta else 1,), (1 if tb else 0,)), ((), ()))

    def body(*refs):
        a_ref, b_ref = refs[0], refs[1]
        ex_refs = refs[2:2 + n_ex]
        o_refs = refs[2 + n_ex + n_into:2 + n_ex + n_into + n_out]

        def dot():
            return lax.dot_general(a_ref[...].astype(BF16), b_ref[...].astype(BF16), dims,
                                   preferred_element_type=F32)

        def finish(acc):
            outs = epilogue(acc, *[e[...] for e in ex_refs]) if epilogue is not None else (acc,)
            for o_ref, o in zip(o_refs, outs):
                o_ref[...] = o.astype(o_ref.dtype)

        if nk == 1:
            finish(dot())
        else:
            acc_ref = refs[-1]
            k = pl.program_id(2)

            @pl.when(k == 0)
            def _():
                acc_ref[...] = dot()

            @pl.when(jnp.logical_and(k > 0, k < nk - 1))
            def _():
                acc_ref[...] += dot()

            @pl.when(k == nk - 1)
            def _():
                finish(acc_ref[...] + dot())

    a_spec = (pl.BlockSpec((tk, tm), lambda i, j, k: (k, i)) if ta
              else pl.BlockSpec((tm, tk), lambda i, j, k: (i, k)))
    if b_spec is None:
        b_spec = (pl.BlockSpec((tn, tk), lambda i, j, k: (j, k)) if tb
                  else pl.BlockSpec((tk, tn), lambda i, j, k: (k, j)))
    io_spec = pl.BlockSpec((tm, tn), lambda i, j, k: (i, j))
    if out_shapes is None:
        out_shapes = [jax.ShapeDtypeStruct((m_dim, n_dim), dt) for dt in out_dtypes]
    if out_specs is None:
        out_specs = [io_spec] * n_out
    return _call(
        body,
        grid=(m_dim // tm, n_dim // tn, nk),
        in_specs=[a_spec, b_spec] + [io_spec] * n_ex + [pl.BlockSpec(memory_space=pl.ANY)] * n_into,
        out_specs=out_specs,
        out_shape=out_shapes,
        scratch_shapes=[pltpu.VMEM((tm, tn), F32)] if nk > 1 else [],
        sem=("parallel", "parallel", "arbitrary"),
        aliases={2 + n_ex: 0} if n_into else None,
        operands=(a, b, *extras) + ((into,) if n_into else ()), name=name, comm=comm)


def _add_epilogue(acc, r):
    return (acc + r,)


def _rms_fwd(x, g, *, name, tm=512, comm=None):
    s_len, d = x.shape

    def body(x_ref, g_ref, o_ref):
        xv = x_ref[...]
        r = lax.rsqrt(jnp.mean(xv * xv, axis=-1, keepdims=True) + EPS)
        o_ref[...] = (xv * r * g_ref[...]).astype(o_ref.dtype)

    return _call(
        body, grid=(s_len // tm,),
        in_specs=[pl.BlockSpec((tm, d), lambda i: (i, 0)), pl.BlockSpec((1, d), lambda i: (0, 0))],
        out_specs=[pl.BlockSpec((tm, d), lambda i: (i, 0))],
        out_shape=[jax.ShapeDtypeStruct((s_len, d), BF16)],
        sem=("parallel",), operands=(x, g), name=name, comm=comm)


def _rms_bwd(dn, x, g, res, *, name, tm=512, comm=None):
    s_len, d = x.shape
    n = s_len // tm

    def body(dn_ref, x_ref, g_ref, res_ref, dx_ref, dg_ref, acc_ref):
        i = pl.program_id(0)
        xv = x_ref[...]
        dnv = dn_ref[...].astype(F32)
        r = lax.rsqrt(jnp.mean(xv * xv, axis=-1, keepdims=True) + EPS)
        xh = xv * r

        @pl.when(i == 0)
        def _():
            acc_ref[...] = jnp.zeros_like(acc_ref)

        acc_ref[...] += _rowsum8(dnv * xh)
        dxh = dnv * g_ref[...]
        dx_ref[...] = (r * (dxh - xh * jnp.mean(dxh * xh, axis=-1, keepdims=True))
                       + res_ref[...].astype(F32))

        @pl.when(i == n - 1)
        def _():
            dg_ref[...] = jnp.sum(acc_ref[...], axis=0, keepdims=True)

    row = pl.BlockSpec((tm, d), lambda i: (i, 0))
    vec = pl.BlockSpec((1, d), lambda i: (0, 0))
    return _call(
        body, grid=(n,),
        in_specs=[row, row, vec, row],
        out_specs=[row, vec],
        out_shape=[jax.ShapeDtypeStruct((s_len, d), F32), jax.ShapeDtypeStruct((1, d), F32)],
        scratch_shapes=[pltpu.VMEM((SUBLANES, d), F32)],
        operands=(dn, x, g, res), name=name, comm=comm)


def _loss_head(x3, g, target, *, tm=512):
    s_len, d = x3.shape
    n = s_len // tm

    def body(x_ref, g_ref, t_ref, dxb_ref, dg_ref, loss_ref, accg_ref, accl_ref):
        i = pl.program_id(0)
        xv = x_ref[...]
        gv = g_ref[...]
        r = lax.rsqrt(jnp.mean(xv * xv, axis=-1, keepdims=True) + EPS)
        xh = xv * r
        err = xh * gv - t_ref[...]

        @pl.when(i == 0)
        def _():
            accg_ref[...] = jnp.zeros_like(accg_ref)
            accl_ref[...] = jnp.zeros_like(accl_ref)

        accl_ref[...] += _rowsum8(err * err)
        dn = err * (1.0 / d)
        accg_ref[...] += _rowsum8(dn * xh)
        dxh = dn * gv
        dx = r * (dxh - xh * jnp.mean(dxh * xh, axis=-1, keepdims=True))
        dxb_ref[...] = dx.astype(BF16)

        @pl.when(i == n - 1)
        def _():
            dg_ref[...] = jnp.sum(accg_ref[...], axis=0, keepdims=True)
            tot = jnp.sum(jnp.sum(accl_ref[...], axis=0, keepdims=True), axis=1, keepdims=True)
            loss_ref[...] = jnp.broadcast_to(tot * (0.5 / d), (1, LANES))

    row = pl.BlockSpec((tm, d), lambda i: (i, 0))
    vec = pl.BlockSpec((1, d), lambda i: (0, 0))
    return pl.pallas_call(
        body, grid=(n,),
        in_specs=[row, vec, row],
        out_specs=[row, vec, pl.BlockSpec((1, LANES), lambda i: (0, 0))],
        out_shape=[jax.ShapeDtypeStruct((s_len, d), BF16),
                   jax.ShapeDtypeStruct((1, d), F32), jax.ShapeDtypeStruct((1, LANES), F32)],
        scratch_shapes=[pltpu.VMEM((SUBLANES, d), F32), pltpu.VMEM((SUBLANES, d), F32)],
        compiler_params=_params("arbitrary"), name="loss_head",
    )(x3, g, target)


def _shift_down(v, k, rows_before, row):
    out = pltpu.roll(v, k, axis=0)
    for j in range(k):
        out = jnp.where(row == j, rows_before[j], out)
    return out


def _shift_up(v, k, rows_after, row):
    t = v.shape[0]
    out = pltpu.roll(v, t - k, axis=0)
    for j in range(k):
        out = jnp.where(row == t - k + j, rows_after[j], out)
    return out


def _conv_fwd(z, w_t, gain, *, ts=512):
    s_len = z.shape[0]
    n_grp = D_CONV // GROUP

    def body(cb_ref, cc_ref, ch_ref, w_ref, g_ref, y_ref, carry_ref):
        i = pl.program_id(0)

        @pl.when(i == 0)
        def _():
            carry_ref[...] = jnp.zeros_like(carry_ref)

        row = lax.broadcasted_iota(jnp.int32, (ts, GROUP), 0)
        for g in range(n_grp):
            sl = slice(g * GROUP, (g + 1) * GROUP)
            uu = cc_ref[:, sl] * ch_ref[:, sl]
            p2 = carry_ref[6:7, sl]
            p1 = carry_ref[7:8, sl]
            u1 = _shift_down(uu, 1, [p1], row)
            u2 = _shift_down(uu, 2, [p2, p1], row)
            conv = w_ref[0:1, sl] * u2 + w_ref[1:2, sl] * u1 + w_ref[2:3, sl] * uu
            y = cb_ref[:, sl] * conv
            carry_ref[:, sl] = uu[ts - SUBLANES:ts, :]
            rg = lax.rsqrt(jnp.mean(y * y, axis=-1, keepdims=True) + EPS)
            y_ref[:, sl] = (y * rg * g_ref[:, sl]).astype(BF16)

    def col(j):
        return pl.BlockSpec((ts, D_CONV), lambda i, j=j: (i, j))

    small = lambda r: pl.BlockSpec((r, D_CONV), lambda i: (0, 0))
    return pl.pallas_call(
        body, grid=(s_len // ts,),
        in_specs=[col(0), col(1), col(2), small(3), small(1)],
        out_specs=col(0),
        out_shape=jax.ShapeDtypeStruct((s_len, 2 * D_CONV), BF16),
        scratch_shapes=[pltpu.VMEM((SUBLANES, D_CONV), F32)],
        compiler_params=_params("arbitrary"), name="conv_fwd",
    )(z, z, z, w_t, gain)


def _conv_bwd(dy, z, w_t, gain, *, ts=512):
    s_len = z.shape[0]
    n = s_len // ts
    n_grp = D_CONV // GROUP
    halo_blocks = ts // SUBLANES

    def body(dy_ref, cb_ref, cc_ref, ch_ref, hcc_ref, hch_ref, w_ref, g_ref,
             dz_ref, dw_ref, dg_ref, carry_ref, accw_ref, accg_ref):
        i = pl.program_id(0)
        first_tile = (n - 1 - i) == 0

        @pl.when(i == 0)
        def _():
            carry_ref[...] = jnp.zeros_like(carry_ref)
            accw_ref[...] = jnp.zeros_like(accw_ref)
            accg_ref[...] = jnp.zeros_like(accg_ref)

        row = lax.broadcasted_iota(jnp.int32, (ts, GROUP), 0)
        keep = jnp.where(first_tile, 0.0, 1.0)
        for g in range(n_grp):
            sl = slice(g * GROUP, (g + 1) * GROUP)
            cc = cc_ref[:, sl]
            ch = ch_ref[:, sl]
            cb = cb_ref[:, sl]
            uu = cc * ch
            p2 = hcc_ref[6:7, sl] * hch_ref[6:7, sl] * keep
            p1 = hcc_ref[7:8, sl] * hch_ref[7:8, sl] * keep
            u1 = _shift_down(uu, 1, [p1], row)
            u2 = _shift_down(uu, 2, [p2, p1], row)
            w0, w1, w2 = w_ref[0:1, sl], w_ref[1:2, sl], w_ref[2:3, sl]
            conv = w0 * u2 + w1 * u1 + w2 * uu
            y = cb * conv
            rg = lax.rsqrt(jnp.mean(y * y, axis=-1, keepdims=True) + EPS)
            yh = y * rg
            dyv = dy_ref[:, sl].astype(F32)
            accg_ref[:, sl] += _rowsum8(dyv * yh)
            dyn = dyv * g_ref[:, sl]
            dpre = rg * (dyn - yh * jnp.mean(dyn * yh, axis=-1, keepdims=True))
            dz_ref[:, sl] = (dpre * conv).astype(BF16)
            dconv = dpre * cb
            accw_ref[0:8, sl] += _rowsum8(dconv * u2)
            accw_ref[8:16, sl] += _rowsum8(dconv * u1)
            accw_ref[16:24, sl] += _rowsum8(dconv * uu)
            n0 = carry_ref[0:1, sl]
            n1 = carry_ref[1:2, sl]
            d1 = _shift_up(dconv, 1, [n0], row)
            d2 = _shift_up(dconv, 2, [n0, n1], row)
            duu = w2 * dconv + w1 * d1 + w0 * d2
            carry_ref[:, sl] = dconv[0:SUBLANES, :]
            dz_ref[:, D_CONV + g * GROUP:D_CONV + (g + 1) * GROUP] = (duu * ch).astype(BF16)
            dz_ref[:, 2 * D_CONV + g * GROUP:2 * D_CONV + (g + 1) * GROUP] = (duu * cc).astype(BF16)

        @pl.when(i == n - 1)
        def _():
            for k in range(3):
                dw_ref[k:k + 1, :] = jnp.sum(accw_ref[8 * k:8 * k + 8, :], axis=0, keepdims=True)
            dg_ref[...] = jnp.sum(accg_ref[...], axis=0, keepdims=True)

    def col(j):
        return pl.BlockSpec((ts, D_CONV), lambda i, j=j: (n - 1 - i, j))

    def halo(j):
        return pl.BlockSpec((SUBLANES, D_CONV),
                            lambda i, j=j: (jnp.maximum((n - 1 - i) * halo_blocks - 1, 0), j))

    small = lambda r: pl.BlockSpec((r, D_CONV), lambda i: (0, 0))
    return pl.pallas_call(
        body, grid=(n,),
        in_specs=[col(0), col(0), col(1), col(2), halo(1), halo(2), small(3), small(1)],
        out_specs=[pl.BlockSpec((ts, 3 * D_CONV), lambda i: (n - 1 - i, 0)), small(3), small(1)],
        out_shape=[jax.ShapeDtypeStruct((s_len, 6 * D_CONV), BF16),
                   jax.ShapeDtypeStruct((3, D_CONV), F32), jax.ShapeDtypeStruct((1, D_CONV), F32)],
        scratch_shapes=[pltpu.VMEM((SUBLANES, D_CONV), F32), pltpu.VMEM((24, D_CONV), F32),
                        pltpu.VMEM((SUBLANES, D_CONV), F32)],
        compiler_params=_params("arbitrary"), name="conv_bwd",
    )(dy, z, z, z, z, z, w_t, gain)


def _split3(v):
    hi = v.astype(BF16)
    r1 = v - hi.astype(F32)
    mid = r1.astype(BF16)
    lo = (r1 - mid.astype(F32)).astype(BF16)
    return jnp.concatenate([hi, mid, lo], axis=1)


def _tri_sum(tri, v):
    w = v.shape[1]
    dd = jnp.dot(tri, _split3(v), preferred_element_type=F32)
    return dd[:, :w] + dd[:, w:2 * w] + dd[:, 2 * w:]


def _chunk_masks(ts):
    r = jnp.arange(ts)
    same = (r[:, None] // CHUNK) == (r[None, :] // CHUNK)
    later = jnp.logical_and(same, r[None, :] > r[:, None]).astype(BF16)
    earlier = jnp.logical_and(same, r[None, :] < r[:, None]).astype(BF16)
    chunk_of_row = jnp.arange(ts // CHUNK * SUBLANES)[:, None] // SUBLANES
    member = (chunk_of_row == (r[None, :] // CHUNK)).astype(BF16)
    return later, earlier, member


def _sigmoid(v):
    return 1.0 / (1.0 + jnp.exp(-v))


def _gla_fwd(z, alow, wgu, bg, gg, y_in, *, ts=512, comm=None):
    s_len = z.shape[0]
    nch = ts // CHUNK
    scale = DK ** -0.5

    tri_u, _, ind8 = _chunk_masks(ts)

    def body(q_ref, k_ref, v_ref, og_ref, al_ref, wgu_ref, bg_ref, gg_ref, tu_ref, ind_ref, yin_ref,
             y_ref, o_ref, la_ref, st_ref, state_ref, kd_ref, qs_ref, dec_ref):
        del yin_ref
        i = pl.program_id(0)

        @pl.when(i == 0)
        def _():
            state_ref[...] = jnp.zeros_like(state_ref)

        pre = jnp.dot(al_ref[...].astype(BF16), wgu_ref[...], preferred_element_type=F32) + bg_ref[...]
        la = (jnp.minimum(pre, 0.0) - jnp.log(1.0 + jnp.exp(-jnp.abs(pre)))) * (1.0 / 16.0)
        la_ref[...] = la
        kd_ref[...] = (k_ref[...] * jnp.exp(_tri_sum(tu_ref[...], la))).astype(BF16)
        qs_ref[...] = (q_ref[...] * scale).astype(BF16)
        dec_ref[...] = jnp.exp(_tri_sum(ind_ref[...], la))

        def chunk(cl, carry):
            rows = pl.ds(pl.multiple_of(cl * CHUNK, CHUNK), CHUNK)
            dec = dec_ref[pl.ds(pl.multiple_of(cl * SUBLANES, SUBLANES), 1), :]
            for h in range(HEADS):
                ks = slice(h * DK, (h + 1) * DK)
                vs = slice(h * DV, (h + 1) * DV)
                kv_t = lax.dot_general(v_ref[rows, vs].astype(BF16), kd_ref[rows, ks],
                                       (((0,), (0,)), ((), ())), preferred_element_type=F32)
                st = state_ref[h] * dec[:, ks] + kv_t
                state_ref[h] = st
                st_ref[cl, h] = st
                o_ref[rows, vs] = lax.dot_general(qs_ref[rows, ks], st.astype(BF16),
                                                  (((1,), (1,)), ((), ())), preferred_element_type=F32)
            return carry

        lax.fori_loop(0, nch, chunk, 0, unroll=2)

        ggv = gg_ref[...]
        for h in range(HEADS):
            vs = slice(h * DV, (h + 1) * DV)
            o_h = o_ref[:, vs]
            og_h = og_ref[:, vs]
            ro = lax.rsqrt(jnp.mean(o_h * o_h, axis=-1, keepdims=True) + EPS)
            y_ref[:, vs] = (o_h * ro * ggv * (og_h * _sigmoid(og_h))).astype(BF16)

    def zcol(width, j):
        return pl.BlockSpec((ts, width), lambda i, j=j: (i, j))

    full = lambda shape: pl.BlockSpec(shape, lambda i: tuple(0 for _ in shape))
    return _call(
        body, grid=(s_len // ts,),
        in_specs=[zcol(512, 6), zcol(512, 7), zcol(1024, 4), zcol(1024, 5), zcol(LANES, 0),
                  full((LANES, 512)), full((1, 512)), full((1, DV)), full(tri_u.shape), full(ind8.shape),
                  pl.BlockSpec(memory_space=pl.ANY)],
        out_specs=[zcol(1024, 1), zcol(1024, 0), zcol(512, 0),
                   pl.BlockSpec((nch, HEADS, DV, DK), lambda i: (i, 0, 0, 0))],
        out_shape=[jax.ShapeDtypeStruct((s_len, 2048), BF16), jax.ShapeDtypeStruct((s_len, 1024), F32),
                   jax.ShapeDtypeStruct((s_len, 512), F32),
                   jax.ShapeDtypeStruct((s_len // CHUNK, HEADS, DV, DK), F32)],
        scratch_shapes=[pltpu.VMEM((HEADS, DV, DK), F32), pltpu.VMEM((ts, 512), BF16),
                        pltpu.VMEM((ts, 512), BF16), pltpu.VMEM((nch * SUBLANES, 512), F32)],
        aliases={10: 0},
        operands=(z, z, z, z, alow, wgu, bg, gg, tri_u, ind8, y_in), name="gla_fwd", comm=comm)


def _gla_bwd(dy, z, o, la, st, alow, wgu, gg, dz_in, *, ts=512, comm=None):
    s_len = z.shape[0]
    n = s_len // ts
    nch = ts // CHUNK
    scale = DK ** -0.5

    tri_u, tri_l, ind8 = _chunk_masks(ts)

    def body(dy_ref, q_ref, k_ref, v_ref, og_ref, o_ref, la_ref, st_ref, stp_ref, al_ref, wgu_ref,
             gg_ref, tu_ref, tl_ref, ind_ref, dzin_ref, dz_ref, dal_ref, dwgu_ref, dbg_ref, dgg_ref,
             gt_ref, decn_ref, accw_ref, accb_ref, accg_ref, dla_ref,
             e_ref, kd_ref, kdb_ref, qs_ref, do_ref, dkd_ref, dec_ref, dbe_ref):
        del dzin_ref
        i = pl.program_id(0)
        first_tile = (n - 1 - i) == 0

        @pl.when(i == 0)
        def _():
            gt_ref[...] = jnp.zeros_like(gt_ref)
            decn_ref[...] = jnp.ones_like(decn_ref)
            accw_ref[...] = jnp.zeros_like(accw_ref)
            accb_ref[...] = jnp.zeros_like(accb_ref)
            accg_ref[...] = jnp.zeros_like(accg_ref)

        la = la_ref[...]
        e_dec = jnp.exp(_tri_sum(tu_ref[...], la))
        e_ref[...] = e_dec
        kd = k_ref[...] * e_dec
        kd_ref[...] = kd
        kdb_ref[...] = kd.astype(BF16)
        qs_ref[...] = (q_ref[...] * scale).astype(BF16)
        dec_ref[...] = jnp.exp(_tri_sum(ind_ref[...], la))
        ggv = gg_ref[...]
        for h in range(HEADS):
            vs = slice(h * DV, (h + 1) * DV)
            o_h = o_ref[:, vs]
            og_h = og_ref[:, vs]
            dy_h = dy_ref[:, vs].astype(F32)
            ro = lax.rsqrt(jnp.mean(o_h * o_h, axis=-1, keepdims=True) + EPS)
            oh = o_h * ro
            sig = _sigmoid(og_h)
            sil = og_h * sig
            accg_ref[...] += _rowsum8(dy_h * oh * sil)
            dz_ref[:, 2048 + h * DV:2048 + (h + 1) * DV] = (
                dy_h * oh * ggv * sig * (1.0 + og_h * (1.0 - sig))).astype(BF16)
            don = dy_h * ggv * sil
            do_ref[:, vs] = (ro * (don - oh * jnp.mean(don * oh, axis=-1, keepdims=True))).astype(BF16)
        keep = jnp.where(first_tile, 0.0, 1.0)

        def chunk(jrev, decn):
            cl = nch - 1 - jrev
            rows = pl.ds(pl.multiple_of(cl * CHUNK, CHUNK), CHUNK)
            one_row = pl.ds(pl.multiple_of(cl * SUBLANES, SUBLANES), 1)
            dec = dec_ref[one_row, :]
            has_prev = jnp.where(cl > 0, 1.0, 0.0)
            prev_idx = jnp.maximum(cl - 1, 0)
            for h in range(HEADS):
                ks = slice(h * DK, (h + 1) * DK)
                vs = slice(h * DV, (h + 1) * DV)
                dob = do_ref[rows, vs]
                s_c = st_ref[cl, h]
                dqs = jnp.dot(dob, s_c.astype(BF16), preferred_element_type=F32)
                dz_ref[rows, ks] = (dqs * scale).astype(BF16)
                gt = gt_ref[h] * decn[:, ks] + lax.dot_general(
                    dob, qs_ref[rows, ks], (((0,), (0,)), ((), ())), preferred_element_type=F32)
                gt_ref[h] = gt
                gb = gt.astype(BF16)
                dkd_ref[rows, ks] = jnp.dot(v_ref[rows, vs].astype(BF16), gb, preferred_element_type=F32)
                dz_ref[rows, 1024 + h * DV:1024 + (h + 1) * DV] = lax.dot_general(
                    kdb_ref[rows, ks], gb, (((1,), (1,)), ((), ())),
                    preferred_element_type=F32).astype(BF16)
                s_prev = has_prev * st_ref[prev_idx, h] + (1.0 - has_prev) * keep * stp_ref[0, h]
                dbe_ref[one_row, ks] = jnp.sum(gt * s_prev, axis=0, keepdims=True) * dec[:, ks]
            return dec

        decn_ref[0:1, :] = lax.fori_loop(0, nch, chunk, decn_ref[0:1, :], unroll=2)

        dkd = dkd_ref[...]
        dz_ref[:, 512:1024] = (dkd * e_ref[...]).astype(BF16)
        dla_ref[...] = _tri_sum(tl_ref[...], dkd * kd_ref[...])
        for c in range(nch):
            dla_ref[c * CHUNK:(c + 1) * CHUNK, :] += dbe_ref[c * SUBLANES:c * SUBLANES + 1, :]
        dpre = dla_ref[...] * (1.0 / 16.0) * (1.0 - jnp.exp(16.0 * la))
        accb_ref[...] += _rowsum8(dpre)
        dpb = dpre.astype(BF16)
        accw_ref[...] += lax.dot_general(al_ref[...].astype(BF16), dpb, (((0,), (0,)), ((), ())),
                                         preferred_element_type=F32)
        dal_ref[...] = lax.dot_general(dpb, wgu_ref[...], (((1,), (1,)), ((), ())),
                                       preferred_element_type=F32)

        @pl.when(i == n - 1)
        def _():
            dwgu_ref[...] = accw_ref[...]
            dbg_ref[...] = jnp.sum(accb_ref[...], axis=0, keepdims=True)
            dgg_ref[...] = jnp.sum(accg_ref[...], axis=0, keepdims=True)

    def zcol(width, j):
        return pl.BlockSpec((ts, width), lambda i, j=j: (n - 1 - i, j))

    full = lambda shape: pl.BlockSpec(shape, lambda i: tuple(0 for _ in shape))
    return _call(
        body, grid=(n,),
        in_specs=[zcol(1024, 1), zcol(512, 6), zcol(512, 7), zcol(1024, 4), zcol(1024, 5),
                  zcol(1024, 0), zcol(512, 0),
                  pl.BlockSpec((nch, HEADS, DV, DK), lambda i: (n - 1 - i, 0, 0, 0)),
                  pl.BlockSpec((1, HEADS, DV, DK),
                               lambda i: (jnp.maximum((n - 1 - i) * nch - 1, 0), 0, 0, 0)),
                  zcol(LANES, 0), full((LANES, 512)), full((1, DV)),
                  full(tri_u.shape), full(tri_l.shape), full(ind8.shape),
                  pl.BlockSpec(memory_space=pl.ANY)],
        out_specs=[zcol(3072, 1), zcol(LANES, 0), full((LANES, 512)), full((1, 512)), full((1, DV))],
        out_shape=[jax.ShapeDtypeStruct((s_len, 6144), BF16), jax.ShapeDtypeStruct((s_len, LANES), F32),
                   jax.ShapeDtypeStruct((LANES, 512), F32), jax.ShapeDtypeStruct((1, 512), F32),
                   jax.ShapeDtypeStruct((1, DV), F32)],
        scratch_shapes=[pltpu.VMEM((HEADS, DV, DK), F32), pltpu.VMEM((SUBLANES, 512), F32),
                        pltpu.VMEM((LANES, 512), F32), pltpu.VMEM((SUBLANES, 512), F32),
                        pltpu.VMEM((SUBLANES, DV), F32), pltpu.VMEM((ts, 512), F32),
                        pltpu.VMEM((ts, 512), F32), pltpu.VMEM((ts, 512), F32), pltpu.VMEM((ts, 512), BF16),
                        pltpu.VMEM((ts, 512), BF16), pltpu.VMEM((ts, 1024), BF16),
                        pltpu.VMEM((ts, 512), F32), pltpu.VMEM((nch * SUBLANES, 512), F32),
                        pltpu.VMEM((nch * SUBLANES, 512), F32)],
        aliases={15: 0},
        operands=(dy, z, z, z, z, o, la, st, st, alow, wgu, gg, tri_u, tri_l, ind8, dz_in),
        name="gla_bwd", comm=comm)


def _adamw(w, g, m, v, *, name):
    rows, cols = w.shape
    tr, tc = _tile(rows, cols)

    def body(w_ref, g_ref, m_ref, v_ref, go_ref, d_ref, nm_ref, nv_ref):
        gv = g_ref[...]
        go_ref[...] = gv
        m2 = ADAM_B1 * m_ref[...] + (1.0 - ADAM_B1) * gv
        v2 = ADAM_B2 * v_ref[...] + (1.0 - ADAM_B2) * jnp.square(gv)
        m_hat = m2 / (1.0 - ADAM_B1 ** ADAM_STEP)
        v_hat = v2 / (1.0 - ADAM_B2 ** ADAM_STEP)
        d_ref[...] = -ADAM_LR * (m_hat / (jnp.sqrt(v_hat) + ADAM_EPS) + ADAM_WD * w_ref[...])
        nm_ref[...] = m2
        nv_ref[...] = v2

    blk = pl.BlockSpec((tr, tc), lambda i, j: (i, j))
    shp = jax.ShapeDtypeStruct((rows, cols), F32)
    return pl.pallas_call(
        body, grid=(rows // tr, cols // tc), in_specs=[blk] * 4, out_specs=[blk] * 4, out_shape=[shp] * 4,
        compiler_params=_params("parallel", "parallel"), name=name,
    )(w, g, m, v)


def _my_place():
    return lax.axis_index("x"), lax.axis_index("y"), lax.axis_index("c")


def _flip(v, bit):
    return 1 - v if bit else v


def _allgather_small(buf, *, reduce, name):
    rows = buf.shape[0]

    def body(in_ref, out_ref, gat_ref, send_sems, recv_sems):
        x, y, c = _my_place()
        me = 4 * x + 2 * y + c
        gat_ref[me] = in_ref[...]
        copies = []
        for m in range(1, N_DEV):
            peer = (_flip(x, m & 4), _flip(y, m & 2), _flip(c, m & 1))
            cp = pltpu.make_async_remote_copy(
                src_ref=in_ref, dst_ref=gat_ref.at[me],
                send_sem=send_sems.at[m - 1], recv_sem=recv_sems.at[m - 1],
                device_id=peer, device_id_type=MESH)
            cp.start()
            copies.append(cp)
        for m in range(1, N_DEV):
            px, py, pc = _flip(x, m & 4), _flip(y, m & 2), _flip(c, m & 1)
            src_slot = gat_ref.at[4 * px + 2 * py + pc]
            pltpu.make_async_remote_copy(
                src_ref=src_slot, dst_ref=src_slot,
                send_sem=send_sems.at[m - 1], recv_sem=recv_sems.at[m - 1],
                device_id=(px, py, pc), device_id_type=MESH).wait_recv()
        for cp in copies:
            cp.wait_send()
        if reduce:
            tot = gat_ref[0]
            for d in range(1, N_DEV):
                tot = tot + gat_ref[d]
            out_ref[...] = tot
        else:
            out_ref[...] = gat_ref[...]

    out_shape = (rows, LANES) if reduce else (N_DEV, rows, LANES)
    return pl.pallas_call(
        body,
        in_specs=[pl.BlockSpec(memory_space=pltpu.VMEM)],
        out_specs=pl.BlockSpec(memory_space=pltpu.VMEM),
        out_shape=jax.ShapeDtypeStruct(out_shape, F32),
        scratch_shapes=[pltpu.VMEM((N_DEV, rows, LANES), F32),
                        pltpu.SemaphoreType.DMA((N_DEV - 1,)), pltpu.SemaphoreType.DMA((N_DEV - 1,))],
        compiler_params=pltpu.CompilerParams(has_side_effects=True),
        name=name,
    )(buf)


def _cast_into(shard, chip_core, *, name):
    rows, cols = shard.shape
    tr, tc = _tile(rows, cols)

    def body(cc_ref, s_ref, o_ref):
        del cc_ref
        o_ref[...] = s_ref[...].astype(BF16)

    grid_spec = pltpu.PrefetchScalarGridSpec(
        num_scalar_prefetch=1, grid=(rows // tr, cols // tc),
        in_specs=[pl.BlockSpec((tr, tc), lambda r, q, cc: (r, q))],
        out_specs=pl.BlockSpec((None, tr, tc), lambda r, q, cc: (cc[0], r, q)))
    return pl.pallas_call(
        body, grid_spec=grid_spec, out_shape=jax.ShapeDtypeStruct((N_CHIPS, rows, cols), BF16),
        compiler_params=_params("arbitrary", "arbitrary"), name=name,
    )(chip_core, shard)


def _remote(src, dst, send_sems, recv_sems, k, device):
    return pltpu.make_async_remote_copy(src_ref=src, dst_ref=dst, send_sem=send_sems.at[k],
                                        recv_sem=recv_sems.at[k], device_id=device, device_id_type=MESH)


def _col_half(ref, h, *lead, rows=None):
    hc = ref.shape[-1] // 2
    mid = (slice(None),) * (len(ref.shape) - 2 - len(lead))
    row_sel = slice(None) if rows is None else pl.ds(rows[0], rows[1])
    return ref.at[tuple(lead) + mid + (row_sel, pl.ds(h * hc, hc))]


def _gather_comm(bufs, rows=None):
    n_w, n_m = len(bufs), len(CHIP_MASKS)
    rows = rows or [None] * n_w

    def first(c_ins, c_outs, ss, rs):
        x, y, c = _my_place()
        chip = 2 * x + y
        for w in range(n_w):
            mine = _col_half(c_outs[w], c, chip, rows=rows[w])
            for mi, (mx, my) in enumerate(CHIP_MASKS):
                _remote(mine, mine, ss, rs, w * n_m + mi, (_flip(x, mx), _flip(y, my), c)).start()

    def mid(c_ins, c_outs, ss, rs):
        x, y, c = _my_place()
        for w in range(n_w):
            for mi, (mx, my) in enumerate(CHIP_MASKS):
                k = w * n_m + mi
                px, py = _flip(x, mx), _flip(y, my)
                landed = _col_half(c_outs[w], c, 2 * px + py, rows=rows[w])
                _remote(landed, landed, ss, rs, k, (px, py, c)).wait_recv()
                _remote(landed, landed, ss, rs, n_w * n_m + k, (x, y, 1 - c)).start()

    def last(c_ins, c_outs, ss, rs):
        x, y, c = _my_place()
        chip = 2 * x + y
        for w in range(n_w):
            mine = _col_half(c_outs[w], c, chip, rows=rows[w])
            for mi, (mx, my) in enumerate(CHIP_MASKS):
                k = w * n_m + mi
                px, py = _flip(x, mx), _flip(y, my)
                theirs = _col_half(c_outs[w], 1 - c, 2 * px + py, rows=rows[w])
                _remote(theirs, theirs, ss, rs, n_w * n_m + k, (x, y, 1 - c)).wait_recv()
                _remote(mine, mine, ss, rs, k, (px, py, c)).wait_send()
                _remote(mine, mine, ss, rs, n_w * n_m + k, (x, y, 1 - c)).wait_send()

    return _Comm(ins=bufs, outs=[jax.ShapeDtypeStruct(b.shape, b.dtype) for b in bufs],
                 aliases={w: w for w in range(n_w)}, n_sems=2 * n_w * n_m, first=first, mid=mid, last=last)


def _swap_comm(grads):
    n_w = len(grads)

    def copy(c_ins, c_outs, ss, rs, w):
        x, y, c = _my_place()
        return _remote(_col_half(c_ins[w], 1 - c), c_outs[w], ss, rs, w, (x, y, 1 - c))

    def first(c_ins, c_outs, ss, rs):
        for w in range(n_w):
            copy(c_ins, c_outs, ss, rs, w).start()

    def last(c_ins, c_outs, ss, rs):
        for w in range(n_w):
            copy(c_ins, c_outs, ss, rs, w).wait()

    return _Comm(ins=grads,
                 outs=[jax.ShapeDtypeStruct(g.shape[:2] + (g.shape[2] // 2,), g.dtype) for g in grads],
                 aliases={}, n_sems=n_w, first=first, last=last)


def _add_own_half(g, other, chip_core, *, name):
    n_chip, rows, hc = other.shape
    tr, tc = _tile(rows, hc)
    per_half = hc // tc

    def body(cc_ref, g_ref, o_ref, out_ref):
        del cc_ref
        out_ref[...] = (g_ref[...].astype(F32) + o_ref[...].astype(F32)).astype(BF16)

    grid_spec = pltpu.PrefetchScalarGridSpec(
        num_scalar_prefetch=1, grid=(n_chip, rows // tr, per_half),
        in_specs=[pl.BlockSpec((None, tr, tc), lambda j, r, q, cc: (j, r, cc[1] * per_half + q)),
                  pl.BlockSpec((None, tr, tc), lambda j, r, q, cc: (j, r, q))],
        out_specs=pl.BlockSpec((None, tr, tc), lambda j, r, q, cc: (j, r, q)))
    return pl.pallas_call(
        body, grid_spec=grid_spec, out_shape=jax.ShapeDtypeStruct((n_chip, rows, hc), BF16),
        compiler_params=_params("parallel", "parallel", "parallel"), name=name,
    )(chip_core, g, other)


def _exchange_comm(pieces):
    n_w, n_m = len(pieces), len(CHIP_MASKS)

    def copies(c_ins, c_outs, ss, rs):
        x, y, c = _my_place()
        chip = 2 * x + y
        for w in range(n_w):
            for mi, (mx, my) in enumerate(CHIP_MASKS):
                px, py = _flip(x, mx), _flip(y, my)
                send = _remote(c_ins[w].at[2 * px + py], c_outs[w].at[chip], ss, rs, w * n_m + mi, (px, py, c))
                landed = c_outs[w].at[2 * px + py]
                yield send, _remote(landed, landed, ss, rs, w * n_m + mi, (px, py, c))

    def first(c_ins, c_outs, ss, rs):
        for send, _ in copies(c_ins, c_outs, ss, rs):
            send.start()

    def last(c_ins, c_outs, ss, rs):
        for send, arrival in copies(c_ins, c_outs, ss, rs):
            arrival.wait_recv()
            send.wait_send()

    return _Comm(ins=pieces, outs=[jax.ShapeDtypeStruct(p.shape, p.dtype) for p in pieces],
                 aliases={}, n_sems=n_w * n_m, first=first, last=last)


def _sum_chips(own, landed, chip_core, *, name):
    n_chip, rows, hc = own.shape
    tr, tc = _tile(rows, hc)
    per_half = hc // tc

    def body(cc_ref, o_ref, l1_ref, l2_ref, l3_ref, out_ref):
        del cc_ref
        out_ref[...] = ((o_ref[...].astype(F32) + l1_ref[...].astype(F32))
                        + l2_ref[...].astype(F32)) + l3_ref[...].astype(F32)

    def slot(k):
        return pl.BlockSpec((None, tr, tc), lambda r, q, cc, k=k: ((cc[0] + k) % n_chip, r, q))

    grid_spec = pltpu.PrefetchScalarGridSpec(
        num_scalar_prefetch=1, grid=(rows // tr, per_half),
        in_specs=[slot(0), slot(1), slot(2), slot(3)],
        out_specs=pl.BlockSpec((tr, tc), lambda r, q, cc: (r, cc[1] * per_half + q)))
    return pl.pallas_call(
        body, grid_spec=grid_spec, out_shape=jax.ShapeDtypeStruct((rows, 2 * hc), F32),
        compiler_params=_params("arbitrary", "arbitrary"), name=name,
    )(chip_core, own, landed, landed, landed)


def _join_comm(halves):
    n_w = len(halves)

    def first(c_ins, c_outs, ss, rs):
        x, y, c = _my_place()
        for w in range(n_w):
            mine = _col_half(c_outs[w], c)
            _remote(mine, mine, ss, rs, w, (x, y, 1 - c)).start()

    def last(c_ins, c_outs, ss, rs):
        x, y, c = _my_place()
        for w in range(n_w):
            theirs = _col_half(c_outs[w], 1 - c)
            _remote(theirs, theirs, ss, rs, w, (x, y, 1 - c)).wait()

    return _Comm(ins=halves, outs=[jax.ShapeDtypeStruct(h.shape, h.dtype) for h in halves],
                 aliases={w: w for w in range(n_w)}, n_sems=n_w, first=first, last=last)


def _standalone(comm, *, name):
    def body(o_ref):
        o_ref[...] = jnp.zeros_like(o_ref)

    return _call(body, grid=(1,), in_specs=[],
                 out_specs=[pl.BlockSpec((SUBLANES, LANES), lambda i: (0, 0))],
                 out_shape=[jax.ShapeDtypeStruct((SUBLANES, LANES), F32)], operands=(), name=name,
                 comm=comm)[1:]


def _pack(pieces):
    flat, spans, off = [], [], 0
    for p in pieces:
        v = p.reshape(-1).astype(F32)
        pad = (-v.shape[0]) % LANES
        if pad:
            v = jnp.concatenate([v, jnp.zeros((pad,), F32)])
        spans.append((off, p.size))
        off += v.shape[0]
        flat.append(v)
    tail = (-off) % (SUBLANES * LANES)
    if tail:
        flat.append(jnp.zeros((tail,), F32))
    return jnp.concatenate(flat).reshape(-1, LANES), spans


def _unpack(buf, span, shape):
    off, size = span
    return buf.reshape(-1)[off:off + size].reshape(shape)


def kernel(x, norm1_g, w_in, w_gate_up, b_gate, conv_w, conv_norm_g, gla_norm_g, w_out, norm2_g, w_ff1, w_ff2, norm_f_g, loss_target, m_norm1_g, m_w_in, m_w_gate_up, m_b_gate, m_conv_w, m_conv_norm_g, m_gla_norm_g, m_w_out, m_norm2_g, m_w_ff1, m_w_ff2, m_norm_f_g, v_norm1_g, v_w_in, v_w_gate_up, v_b_gate, v_conv_w, v_conv_norm_g, v_gla_norm_g, v_w_out, v_norm2_g, v_w_ff1, v_w_ff2, v_norm_f_g):
    xs = x[0]
    target = loss_target[0]
    s_len, d = xs.shape
    d_in = w_in.shape[2] * N_CHIPS
    d_main = d_in - GATE_RANK
    d_ff = w_ff1.shape[2] * N_CHIPS
    cx, cy, cc = _my_place()
    chip = 2 * cx + cy
    chip_core = jnp.stack([chip, cc]).astype(jnp.int32)
    n_ff = d_ff // N_CHIPS
    norm_f = norm_f_g.reshape(1, d)

    n_sh = d_in // N_CHIPS
    wi_buf = _cast_into(w_in[0].T, chip_core, name="cast_w_in")
    wo_buf = _cast_into(w_out[0], chip_core, name="cast_w_out")
    w1_buf = _cast_into(w_ff1[0], chip_core, name="cast_w_ff1")
    w2_buf = _cast_into(w_ff2[0], chip_core, name="cast_w_ff2")

    small_w, spans_w = _pack([w_gate_up[0], conv_w[0]])
    small_all = _allgather_small(small_w, reduce=False, name="gather_small_weights")
    chips_first = [small_all[2 * j] for j in range(N_CHIPS)]
    wgu_full = jnp.concatenate(
        [_unpack(b, spans_w[0], w_gate_up.shape[1:]) for b in chips_first], axis=1)
    convw_full = jnp.concatenate(
        [_unpack(b, spans_w[1], conv_w.shape[1:]) for b in chips_first], axis=0)
    wgu_pad = jnp.concatenate(
        [wgu_full, jnp.zeros((LANES - GATE_RANK, wgu_full.shape[1]), F32)], axis=0).astype(BF16)
    convw_t = convw_full.T

    u, wi_buf = _rms_fwd(xs, norm1_g, name="norm1_fwd", comm=_gather_comm([wi_buf]))
    wi_t = wi_buf.reshape(d_in, d)
    wg_t = jnp.concatenate([wi_t[d_main:], jnp.zeros((LANES - GATE_RANK, d), BF16)], axis=0)
    z, wo_buf, w1_buf = _matmul(u, wi_t, tb=True, tm=1024, tn=1024, tk=d, out_dtypes=[F32], n_dim=d_main,
                                name="in_proj",
                                comm=_gather_comm([wo_buf, w1_buf], rows=[None, (0, d // 2)]))
    wo_full = wo_buf.reshape(d, d)
    (alow,) = _matmul(u, wg_t, tb=True, tm=1024, tn=LANES, tk=d, out_dtypes=[F32], name="in_proj_gate")
    y0 = _conv_fwd(z, convw_t, conv_norm_g)
    y, o, la, st, w1_cm = _gla_fwd(z, alow, wgu_pad, b_gate, gla_norm_g, y0,
                                   comm=_gather_comm([w1_buf], rows=[(d // 2, d // 2)]))
    (x2,) = _matmul(y, wo_full, tm=1024, tn=1024, tk=d, out_dtypes=[F32], extras=(xs,),
                    epilogue=_add_epilogue, name="out_proj")
    (h,) = _rms_fwd(x2, norm2_g, name="norm2_fwd")
    a, p, w2_buf = _matmul(
        h, w1_cm, tm=1024, tn=1024, tk=d, out_dtypes=[F32, BF16], n_dim=d_ff,
        b_spec=pl.BlockSpec((None, d, 1024), lambda i, j, k: (j // 2, 0, j % 2)),
        epilogue=lambda acc: (acc, jnp.square(jnp.maximum(acc, 0.0))), name="ff1",
        comm=_gather_comm([w2_buf]))
    w2_full = w2_buf.reshape(d_ff, d)
    (x3,) = _matmul(p, w2_full, tm=1024, tn=1024, tk=2048, out_dtypes=[F32], extras=(x2,),
                    epilogue=_add_epilogue, name="ff2")
    dx3b, g_normf, loss_part = _loss_head(x3, norm_f, target)

    (da,) = _matmul(dx3b, w2_full, tb=True, tm=1024, tn=1024, tk=d, out_dtypes=[BF16], extras=(a,),
                    epilogue=lambda acc, av: (acc * (2.0 * jnp.maximum(av, 0.0)),), name="ff2_dx")
    (g_w2,) = _matmul(p, dx3b, ta=True, tm=1024, tn=d, tk=1024, out_dtypes=[BF16], name="ff2_dw")
    (g_w1,) = _matmul(
        h, da, ta=True, tm=1024, tn=n_ff, tk=1024, out_dtypes=[BF16], name="ff1_dw",
        out_shapes=[jax.ShapeDtypeStruct((N_CHIPS, d, n_ff), BF16)],
        out_specs=[pl.BlockSpec((None, 1024, n_ff), lambda i, j, k: (j, i, 0))])
    g_w2 = g_w2.reshape(N_CHIPS, n_ff, d)
    dh, t_w1, t_w2 = _matmul(
        da, w1_cm, tb=True, tm=1024, tn=1024, tk=2048, out_dtypes=[BF16], n_dim=d,
        b_spec=pl.BlockSpec((None, 1024, 2048), lambda i, j, k: (k, j, 0)), name="ff1_dx",
        comm=_swap_comm([g_w1, g_w2]))
    p_w1 = _add_own_half(g_w1, t_w1, chip_core, name="pre_reduce_w_ff1")
    p_w2 = _add_own_half(g_w2, t_w2, chip_core, name="pre_reduce_w_ff2")
    dx2, g_norm2 = _rms_bwd(dh, x2, norm2_g, dx3b, name="norm2_bwd")
    (dy,) = _matmul(dx2, wo_full, tb=True, tm=1024, tn=1024, tk=d, out_dtypes=[BF16], name="out_proj_dx")
    (g_wo,) = _matmul(y, dx2, ta=True, tm=1024, tn=d, tk=1024, out_dtypes=[BF16], name="out_proj_dw")
    g_wo = g_wo.reshape(N_CHIPS, d // N_CHIPS, d)
    dz0, g_convw_t, g_convg = _conv_bwd(dy, z, convw_t, conv_norm_g)
    dz, dalow, g_wgu_pad, g_bg, g_gg, l_w2 = _gla_bwd(dy, z, o, la, st, alow, wgu_pad, gla_norm_g, dz0,
                                                      comm=_exchange_comm([p_w2]))
    m_w2 = _sum_chips(p_w2, l_w2, chip_core, name="reduce_w_ff2")
    g_wi_t, l_w1 = _matmul(dz, u, ta=True, tm=1024, tn=d, tk=1024, out_dtypes=[BF16], name="in_proj_dw",
                           out_shapes=[jax.ShapeDtypeStruct((d_in, d), BF16)],
                           comm=_exchange_comm([p_w1]))
    (g_wi_t,) = _matmul(dalow, u, ta=True, tm=LANES, tn=d, tk=1024, out_dtypes=[BF16],
                        epilogue=lambda acc: (acc[:GATE_RANK],), into=g_wi_t, name="in_proj_gate_dw",
                        out_shapes=[jax.ShapeDtypeStruct((d_in, d), BF16)],
                        out_specs=[pl.BlockSpec((GATE_RANK, d), lambda i, j, k: (d_main // GATE_RANK, 0))])
    m_w1 = _sum_chips(p_w1, l_w1, chip_core, name="reduce_w_ff1")
    g_wi = g_wi_t.reshape(N_CHIPS, n_sh, d)
    du_gate, t_wi, t_wo = _matmul(dalow, wg_t, tm=1024, tn=1024, tk=LANES, out_dtypes=[F32],
                                  name="in_proj_gate_dx", comm=_swap_comm([g_wi, g_wo]))
    p_wi = _add_own_half(g_wi, t_wi, chip_core, name="pre_reduce_w_in")
    p_wo = _add_own_half(g_wo, t_wo, chip_core, name="pre_reduce_w_out")
    du, l_wi, l_wo, m_w1, m_w2 = _matmul(
        dz, wi_t, tm=1024, tn=1024, tk=2048, out_dtypes=[BF16], n_dim=d, extras=(du_gate,),
        epilogue=_add_epilogue, name="in_proj_dx",
        comm=[_exchange_comm([p_wi, p_wo]), _join_comm([m_w1, m_w2])])
    m_wi = _sum_chips(p_wi, l_wi, chip_core, name="reduce_w_in")
    m_wo = _sum_chips(p_wo, l_wo, chip_core, name="reduce_w_out")
    grad_x, g_norm1 = _rms_bwd(du, xs, norm1_g, dx2, name="norm1_bwd")
    m_wi, m_wo = _standalone(_join_comm([m_wi, m_wo]), name="join_w_in_w_out")
    g_big = [m_wi, m_wo, m_w1, m_w2]

    small_g, spans_g = _pack([g_norm1, g_wgu_pad[:GATE_RANK], g_bg, g_convw_t, g_convg, g_gg, g_norm2,
                              g_normf, loss_part[:, :1]])
    tot = _allgather_small(small_g, reduce=True, name="reduce_small_grads")
    t_norm1 = _unpack(tot, spans_g[0], (1, d))
    t_wgu = _unpack(tot, spans_g[1], (GATE_RANK, HEADS * DK))
    t_bg = _unpack(tot, spans_g[2], (1, HEADS * DK))
    t_convw = _unpack(tot, spans_g[3], (3, D_CONV)).T
    t_convg = _unpack(tot, spans_g[4], (1, D_CONV))
    t_gg = _unpack(tot, spans_g[5], (1, DV))
    t_norm2 = _unpack(tot, spans_g[6], (1, d))
    t_normf = _unpack(tot, spans_g[7], (1, d))
    loss = _unpack(tot, spans_g[8], ())
    n_gu = w_gate_up.shape[2]
    n_cw = conv_w.shape[1]
    t_wgu = lax.dynamic_slice(t_wgu, (0, chip * n_gu), (GATE_RANK, n_gu))
    t_convw = lax.dynamic_slice(t_convw, (chip * n_cw, 0), (n_cw, 3))

    order = ["norm1_g", "w_in", "w_gate_up", "b_gate", "conv_w", "conv_norm_g", "gla_norm_g", "w_out",
             "norm2_g", "w_ff1", "w_ff2", "norm_f_g"]
    weights = dict(norm1_g=norm1_g, w_in=w_in, w_gate_up=w_gate_up, b_gate=b_gate, conv_w=conv_w,
                   conv_norm_g=conv_norm_g, gla_norm_g=gla_norm_g, w_out=w_out, norm2_g=norm2_g,
                   w_ff1=w_ff1, w_ff2=w_ff2, norm_f_g=norm_f_g)
    moms = dict(norm1_g=m_norm1_g, w_in=m_w_in, w_gate_up=m_w_gate_up, b_gate=m_b_gate, conv_w=m_conv_w,
                conv_norm_g=m_conv_norm_g, gla_norm_g=m_gla_norm_g, w_out=m_w_out, norm2_g=m_norm2_g,
                w_ff1=m_w_ff1, w_ff2=m_w_ff2, norm_f_g=m_norm_f_g)
    vels = dict(norm1_g=v_norm1_g, w_in=v_w_in, w_gate_up=v_w_gate_up, b_gate=v_b_gate, conv_w=v_conv_w,
                conv_norm_g=v_conv_norm_g, gla_norm_g=v_gla_norm_g, w_out=v_w_out, norm2_g=v_norm2_g,
                w_ff1=v_w_ff1, w_ff2=v_w_ff2, norm_f_g=v_norm_f_g)
    grads2d = dict(norm1_g=t_norm1, w_in=g_big[0], w_gate_up=t_wgu, b_gate=t_bg, conv_w=t_convw,
                   conv_norm_g=t_convg, gla_norm_g=t_gg, w_out=g_big[1], norm2_g=t_norm2,
                   w_ff1=g_big[2], w_ff2=g_big[3], norm_f_g=t_normf)
    out_g, out_d, out_m, out_v = [], [], [], []
    for nm in order:
        w = weights[nm]
        g2 = grads2d[nm]
        if nm == "w_in":
            to2d, back = (lambda t: t[0].T), (lambda t: t.T.reshape(w.shape))
        else:
            to2d, back = (lambda t: t.reshape(g2.shape)), (lambda t: t.reshape(w.shape))
        res = _adamw(to2d(w), g2, to2d(moms[nm]), to2d(vels[nm]), name="adamw_" + nm)
        for lst, r in zip((out_g, out_d, out_m, out_v), res):
            lst.append(back(r))
    return (loss, grad_x.reshape(x.shape), *out_g, *out_d, *out_m, *out_v)
```

```python
import functools

import jax
import jax.numpy as jnp
from jax import lax
from jax.experimental import pallas as pl
from jax.experimental.pallas import tpu as pltpu

F32 = jnp.float32
BF16 = jnp.bfloat16
MESH = pl.DeviceIdType.MESH

EPS = 1e-6
CHUNK = 64
HEADS = 4
DK = 128
DV = 256
D_CONV = 1024
GROUP = 128
GATE_RANK = 16
LANES = 128
SUBLANES = 8
N_CHIPS = 4
N_DEV = 8
CHIP_MASKS = ((1, 0), (0, 1), (1, 1))

ADAM_LR = 0.001
ADAM_B1 = 0.9
ADAM_B2 = 0.999
ADAM_EPS = 1e-08
ADAM_WD = 0.01
ADAM_STEP = 10

VMEM_LIMIT = 56 * 1024 * 1024


def _params(*sem):
    return pltpu.CompilerParams(dimension_semantics=tuple(sem), vmem_limit_bytes=VMEM_LIMIT)


def _rowsum8(v):
    r, c = v.shape
    return jnp.sum(v.reshape(r // SUBLANES, SUBLANES, c), axis=0)


def _tile(rows, cols):
    for cand in (256, 128, 64, 32, 16, 8):
        if rows % cand == 0 and rows > cand:
            return cand, cols
    if rows * cols * 4 > (2 << 20) and cols % 256 == 0:
        return rows, 256
    return rows, cols


class _Comm:
    def __init__(self, ins, outs, aliases, n_sems, first, last, mid=None):
        self.ins = list(ins)
        self.outs = list(outs)
        self.aliases = dict(aliases)
        self.n_sems = n_sems
        self.first = first
        self.mid = mid
        self.last = last


def _call(body, *, grid, in_specs, out_specs, out_shape, operands, name, scratch_shapes=(), sem=None,
          aliases=None, comm=None):
    aliases = dict(aliases or {})
    if comm is None:
        return pl.pallas_call(
            body, grid=grid, in_specs=list(in_specs), out_specs=list(out_specs), out_shape=list(out_shape),
            scratch_shapes=list(scratch_shapes), input_output_aliases=aliases,
            compiler_params=_params(*(sem or ("arbitrary",) * len(grid))), name=name)(*operands)
    comms = list(comm) if isinstance(comm, (list, tuple)) else [comm]
    n_in, n_out, n_scr = len(in_specs), len(out_specs), len(scratch_shapes)
    c_ins_all = [a for cm in comms for a in cm.ins]
    c_outs_all = [o for cm in comms for o in cm.outs]
    n_ci, n_co = len(c_ins_all), len(c_outs_all)

    def full_body(*refs):
        ins = refs[:n_in]
        o0 = n_in + n_ci
        outs = refs[o0:o0 + n_out]
        s0 = o0 + n_out + n_co
        scr = refs[s0:s0 + n_scr]
        sems = refs[s0 + n_scr:]
        parts, i_at, o_at = [], n_in, o0 + n_out
        for q, cm in enumerate(comms):
            parts.append((cm, refs[i_at:i_at + len(cm.ins)], refs[o_at:o_at + len(cm.outs)],
                          sems[2 * q], sems[2 * q + 1]))
            i_at += len(cm.ins)
            o_at += len(cm.outs)
        step = functools.reduce(lambda acc, ig: acc * ig[1] + pl.program_id(ig[0]), enumerate(grid), 0)
        n_steps = functools.reduce(lambda acc, g: acc * g, grid, 1)
        mid_step = (3 * n_steps) // 4

        @pl.when(step == 0)
        def _():
            for cm, c_ins, c_outs, ss, rs in parts:
                cm.first(c_ins, c_outs, ss, rs)

        if 0 < mid_step < n_steps - 1:
            @pl.when(step == mid_step)
            def _():
                for cm, c_ins, c_outs, ss, rs in parts:
                    if cm.mid is not None:
                        cm.mid(c_ins, c_outs, ss, rs)

        body(*ins, *outs, *scr)

        @pl.when(step == n_steps - 1)
        def _():
            for cm, c_ins, c_outs, ss, rs in parts:
                if cm.mid is not None and not 0 < mid_step < n_steps - 1:
                    cm.mid(c_ins, c_outs, ss, rs)
                cm.last(c_ins, c_outs, ss, rs)

    any_spec = pl.BlockSpec(memory_space=pl.ANY)
    i_at, o_at, sem_shapes = n_in, n_out, []
    for cm in comms:
        for i_in, i_out in cm.aliases.items():
            aliases[i_at + i_in] = o_at + i_out
        i_at += len(cm.ins)
        o_at += len(cm.outs)
        sem_shapes += [pltpu.SemaphoreType.DMA((cm.n_sems,)), pltpu.SemaphoreType.DMA((cm.n_sems,))]
    return pl.pallas_call(
        full_body, grid=grid,
        in_specs=list(in_specs) + [any_spec] * n_ci,
        out_specs=list(out_specs) + [any_spec] * n_co,
        out_shape=list(out_shape) + c_outs_all,
        scratch_shapes=list(scratch_shapes) + sem_shapes,
        input_output_aliases=aliases,
        compiler_params=pltpu.CompilerParams(dimension_semantics=("arbitrary",) * len(grid),
                                             vmem_limit_bytes=VMEM_LIMIT, has_side_effects=True),
        name=name)(*operands, *c_ins_all)


def _matmul(a, b, *, ta=False, tb=False, tm, tn, tk, out_dtypes, name, extras=(), epilogue=None,
            out_shapes=None, out_specs=None, b_spec=None, n_dim=None, into=None, side=None, comm=None):
    n_into = 0 if into is None else 1
    if ta:
        k_dim, m_dim = a.shape
    else:
        m_dim, k_dim = a.shape
    if n_dim is None:
        n_dim = b.shape[0] if tb else b.shape[1]
        assert (b.shape[1] if tb else b.shape[0]) == k_dim
    assert m_dim % tm == 0 and n_dim % tn == 0 and k_dim % tk == 0, (name, a.shape, b.shape)
    nk = k_dim // tk
    n_ex, n_out = len(extras), len(out_dtypes)
    dims = (((0 if ta else 1,), (1 if tb else 0,)), ((), ()))
    grid = (m_dim // tm, n_dim // tn, nk)
    n_steps = grid[0] * grid[1] * grid[2]
    s_ins, s_outs, s_scr = (len(side.ins), len(side.outs), len(side.scratch)) if side else (0, 0, 0)

    def body(*refs):
        a_ref, b_ref = refs[0], refs[1]
        ex_refs = refs[2:2 + n_ex]
        i0 = 2 + n_ex + n_into
        side_in = refs[i0:i0 + s_ins]
        o_refs = refs[i0 + s_ins:i0 + s_ins + n_out]
        side_out = refs[i0 + s_ins + n_out:i0 + s_ins + n_out + s_outs]
        side_scr = refs[len(refs) - s_scr:] if s_scr else ()
        if side is not None:
            step = (pl.program_id(0) * grid[1] + pl.program_id(1)) * grid[2] + pl.program_id(2)

            @pl.when(step == 0)
            def _():
                side.init(side_scr)

        def dot():
            if side is not None:
                side.body(step, side_in, side_out, side_scr)
            return lax.dot_general(a_ref[...].astype(BF16), b_ref[...].astype(BF16), dims,
                                   preferred_element_type=F32)

        def finish(acc):
            outs = epilogue(acc, *[e[...] for e in ex_refs]) if epilogue is not None else (acc,)
            for o_ref, o in zip(o_refs, outs):
                o_ref[...] = o.astype(o_ref.dtype)

        if nk == 1:
            finish(dot())
        else:
            acc_ref = refs[len(refs) - s_scr - 1]
            k = pl.program_id(2)

            @pl.when(k == 0)
            def _():
                acc_ref[...] = dot()

            @pl.when(jnp.logical_and(k > 0, k < nk - 1))
            def _():
                acc_ref[...] += dot()

            @pl.when(k == nk - 1)
            def _():
                finish(acc_ref[...] + dot())

    a_spec = (pl.BlockSpec((tk, tm), lambda i, j, k: (k, i)) if ta
              else pl.BlockSpec((tm, tk), lambda i, j, k: (i, k)))
    if b_spec is None:
        b_spec = (pl.BlockSpec((tn, tk), lambda i, j, k: (j, k)) if tb
                  else pl.BlockSpec((tk, tn), lambda i, j, k: (k, j)))
    io_spec = pl.BlockSpec((tm, tn), lambda i, j, k: (i, j))
    if out_shapes is None:
        out_shapes = [jax.ShapeDtypeStruct((m_dim, n_dim), dt) for dt in out_dtypes]
    if out_specs is None:
        out_specs = [io_spec] * n_out
    return _call(
        body,
        grid=grid,
        in_specs=([a_spec, b_spec] + [io_spec] * n_ex + [pl.BlockSpec(memory_space=pl.ANY)] * n_into
                  + (list(side.in_specs) if side else [])),
        out_specs=list(out_specs) + (list(side.out_specs) if side else []),
        out_shape=list(out_shapes) + (list(side.outs) if side else []),
        scratch_shapes=([pltpu.VMEM((tm, tn), F32)] if nk > 1 else []) + (list(side.scratch) if side else []),
        sem=("parallel", "parallel", "arbitrary") if side is None else None,
        aliases={2 + n_ex: 0} if n_into else None,
        operands=(a, b, *extras) + ((into,) if n_into else ()) + (tuple(side.ins) if side else ()),
        name=name, comm=comm)


def _add_epilogue(acc, r):
    return (acc + r,)


def _rms_fwd(x, g, *, name, tm=512, comm=None):
    s_len, d = x.shape

    def body(x_ref, g_ref, o_ref):
        xv = x_ref[...]
        r = lax.rsqrt(jnp.mean(xv * xv, axis=-1, keepdims=True) + EPS)
        o_ref[...] = (xv * r * g_ref[...]).astype(o_ref.dtype)

    return _call(
        body, grid=(s_len // tm,),
        in_specs=[pl.BlockSpec((tm, d), lambda i: (i, 0)), pl.BlockSpec((1, d), lambda i: (0, 0))],
        out_specs=[pl.BlockSpec((tm, d), lambda i: (i, 0))],
        out_shape=[jax.ShapeDtypeStruct((s_len, d), BF16)],
        sem=("parallel",), operands=(x, g), name=name, comm=comm)


def _zero_refs(refs):
    for r in refs:
        r[...] = jnp.zeros_like(r)


def _rms_bwd_block(dn_ref, x_ref, g_ref, res_ref, dx_ref, dg_ref, acc_ref):
    xv = x_ref[...]
    dnv = dn_ref[...].astype(F32)
    r = lax.rsqrt(jnp.mean(xv * xv, axis=-1, keepdims=True) + EPS)
    xh = xv * r
    acc = acc_ref[...] + _rowsum8(dnv * xh)
    acc_ref[...] = acc
    dg_ref[...] = jnp.sum(acc, axis=0, keepdims=True)
    dxh = dnv * g_ref[...]
    dx_ref[...] = (r * (dxh - xh * jnp.mean(dxh * xh, axis=-1, keepdims=True))
                   + res_ref[...].astype(F32))


class _Side:
    def __init__(self, ins, in_specs, outs, out_specs, scratch, init, body):
        self.ins, self.in_specs, self.outs, self.out_specs = ins, in_specs, outs, out_specs
        self.scratch = scratch
        self.init = init
        self.body = body


def _rms_bwd_side(dn, x, g, res, *, block_of_step, n_steps):
    s_len, d = x.shape
    row = pl.BlockSpec((s_len // n_steps, d), lambda i, j, k: (block_of_step(i, j, k), 0))
    vec = pl.BlockSpec((1, d), lambda i, j, k: (0, 0))
    return _Side(
        ins=[dn, x, g, res], in_specs=[row, row, vec, row],
        outs=[jax.ShapeDtypeStruct((s_len, d), F32), jax.ShapeDtypeStruct((1, d), F32)],
        out_specs=[row, vec], scratch=[pltpu.VMEM((SUBLANES, d), F32)],
        init=_zero_refs, body=lambda step, ins, outs, scr: _rms_bwd_block(*ins, *outs, *scr))


def _rms_bwd(dn, x, g, res, *, name, tm=512, comm=None):
    s_len, d = x.shape
    n = s_len // tm

    def body(*refs):
        @pl.when(pl.program_id(0) == 0)
        def _():
            _zero_refs(refs[-1:])

        _rms_bwd_block(*refs)

    row = pl.BlockSpec((tm, d), lambda i: (i, 0))
    vec = pl.BlockSpec((1, d), lambda i: (0, 0))
    return _call(
        body, grid=(n,),
        in_specs=[row, row, vec, row],
        out_specs=[row, vec],
        out_shape=[jax.ShapeDtypeStruct((s_len, d), F32), jax.ShapeDtypeStruct((1, d), F32)],
        scratch_shapes=[pltpu.VMEM((SUBLANES, d), F32)],
        operands=(dn, x, g, res), name=name, comm=comm)


def _loss_head(x3, g, target, *, tm=512):
    s_len, d = x3.shape
    n = s_len // tm

    def body(x_ref, g_ref, t_ref, dxb_ref, dg_ref, loss_ref, accg_ref, accl_ref):
        i = pl.program_id(0)
        xv = x_ref[...]
        gv = g_ref[...]
        r = lax.rsqrt(jnp.mean(xv * xv, axis=-1, keepdims=True) + EPS)
        xh = xv * r
        err = xh * gv - t_ref[...]

        @pl.when(i == 0)
        def _():
            accg_ref[...] = jnp.zeros_like(accg_ref)
            accl_ref[...] = jnp.zeros_like(accl_ref)

        accl_ref[...] += _rowsum8(err * err)
        dn = err * (1.0 / d)
        accg_ref[...] += _rowsum8(dn * xh)
        dxh = dn * gv
        dx = r * (dxh - xh * jnp.mean(dxh * xh, axis=-1, keepdims=True))
        dxb_ref[...] = dx.astype(BF16)

        @pl.when(i == n - 1)
        def _():
            dg_ref[...] = jnp.sum(accg_ref[...], axis=0, keepdims=True)
            tot = jnp.sum(jnp.sum(accl_ref[...], axis=0, keepdims=True), axis=1, keepdims=True)
            loss_ref[...] = jnp.broadcast_to(tot * (0.5 / d), (1, LANES))

    row = pl.BlockSpec((tm, d), lambda i: (i, 0))
    vec = pl.BlockSpec((1, d), lambda i: (0, 0))
    return pl.pallas_call(
        body, grid=(n,),
        in_specs=[row, vec, row],
        out_specs=[row, vec, pl.BlockSpec((1, LANES), lambda i: (0, 0))],
        out_shape=[jax.ShapeDtypeStruct((s_len, d), BF16),
                   jax.ShapeDtypeStruct((1, d), F32), jax.ShapeDtypeStruct((1, LANES), F32)],
        scratch_shapes=[pltpu.VMEM((SUBLANES, d), F32), pltpu.VMEM((SUBLANES, d), F32)],
        compiler_params=_params("arbitrary"), name="loss_head",
    )(x3, g, target)


def _shift_down(v, k, rows_before, row):
    out = pltpu.roll(v, k, axis=0)
    for j in range(k):
        out = jnp.where(row == j, rows_before[j], out)
    return out


def _shift_up(v, k, rows_after, row):
    t = v.shape[0]
    out = pltpu.roll(v, t - k, axis=0)
    for j in range(k):
        out = jnp.where(row == t - k + j, rows_after[j], out)
    return out


def _conv_fwd(z, w_t, gain, *, ts=512):
    s_len = z.shape[0]
    n_grp = D_CONV // GROUP

    def body(cb_ref, cc_ref, ch_ref, w_ref, g_ref, y_ref, carry_ref):
        i = pl.program_id(0)

        @pl.when(i == 0)
        def _():
            carry_ref[...] = jnp.zeros_like(carry_ref)

        row = lax.broadcasted_iota(jnp.int32, (ts, GROUP), 0)
        for g in range(n_grp):
            sl = slice(g * GROUP, (g + 1) * GROUP)
            uu = cc_ref[:, sl] * ch_ref[:, sl]
            p2 = carry_ref[6:7, sl]
            p1 = carry_ref[7:8, sl]
            u1 = _shift_down(uu, 1, [p1], row)
            u2 = _shift_down(uu, 2, [p2, p1], row)
            conv = w_ref[0:1, sl] * u2 + w_ref[1:2, sl] * u1 + w_ref[2:3, sl] * uu
            y = cb_ref[:, sl] * conv
            carry_ref[:, sl] = uu[ts - SUBLANES:ts, :]
            rg = lax.rsqrt(jnp.mean(y * y, axis=-1, keepdims=True) + EPS)
            y_ref[:, sl] = (y * rg * g_ref[:, sl]).astype(BF16)

    def col(j):
        return pl.BlockSpec((ts, D_CONV), lambda i, j=j: (i, j))

    small = lambda r: pl.BlockSpec((r, D_CONV), lambda i: (0, 0))
    return pl.pallas_call(
        body, grid=(s_len // ts,),
        in_specs=[col(0), col(1), col(2), small(3), small(1)],
        out_specs=col(0),
        out_shape=jax.ShapeDtypeStruct((s_len, 2 * D_CONV), BF16),
        scratch_shapes=[pltpu.VMEM((SUBLANES, D_CONV), F32)],
        compiler_params=_params("arbitrary"), name="conv_fwd",
    )(z, z, z, w_t, gain)


def _conv_bwd_side(dy, z, w_t, gain, *, n_steps, step_of):
    s_len = z.shape[0]
    n = n_steps
    ts = s_len // n
    n_grp = D_CONV // GROUP
    halo_blocks = ts // SUBLANES

    def body(step, ins, outs, scr):
        dy_ref, cb_ref, cc_ref, ch_ref, hcc_ref, hch_ref, w_ref, g_ref = ins
        dz_ref, dw_ref, dg_ref = outs
        carry_ref, accw_ref, accg_ref = scr
        first_tile = (n - 1 - step) == 0
        row = lax.broadcasted_iota(jnp.int32, (ts, GROUP), 0)
        keep = jnp.where(first_tile, 0.0, 1.0)
        for g in range(n_grp):
            sl = slice(g * GROUP, (g + 1) * GROUP)
            cc = cc_ref[:, sl]
            ch = ch_ref[:, sl]
            cb = cb_ref[:, sl]
            uu = cc * ch
            p2 = hcc_ref[6:7, sl] * hch_ref[6:7, sl] * keep
            p1 = hcc_ref[7:8, sl] * hch_ref[7:8, sl] * keep
            u1 = _shift_down(uu, 1, [p1], row)
            u2 = _shift_down(uu, 2, [p2, p1], row)
            w0, w1, w2 = w_ref[0:1, sl], w_ref[1:2, sl], w_ref[2:3, sl]
            conv = w0 * u2 + w1 * u1 + w2 * uu
            y = cb * conv
            rg = lax.rsqrt(jnp.mean(y * y, axis=-1, keepdims=True) + EPS)
            yh = y * rg
            dyv = dy_ref[:, sl].astype(F32)
            accg_ref[:, sl] += _rowsum8(dyv * yh)
            dyn = dyv * g_ref[:, sl]
            dpre = rg * (dyn - yh * jnp.mean(dyn * yh, axis=-1, keepdims=True))
            dz_ref[:, sl] = (dpre * conv).astype(BF16)
            dconv = dpre * cb
            accw_ref[0:8, sl] += _rowsum8(dconv * u2)
            accw_ref[8:16, sl] += _rowsum8(dconv * u1)
            accw_ref[16:24, sl] += _rowsum8(dconv * uu)
            n0 = carry_ref[0:1, sl]
            n1 = carry_ref[1:2, sl]
            d1 = _shift_up(dconv, 1, [n0], row)
            d2 = _shift_up(dconv, 2, [n0, n1], row)
            duu = w2 * dconv + w1 * d1 + w0 * d2
            carry_ref[:, sl] = dconv[0:SUBLANES, :]
            dz_ref[:, D_CONV + g * GROUP:D_CONV + (g + 1) * GROUP] = (duu * ch).astype(BF16)
            dz_ref[:, 2 * D_CONV + g * GROUP:2 * D_CONV + (g + 1) * GROUP] = (duu * cc).astype(BF16)

        for k in range(3):
            dw_ref[k:k + 1, :] = jnp.sum(accw_ref[8 * k:8 * k + 8, :], axis=0, keepdims=True)
        dg_ref[...] = jnp.sum(accg_ref[...], axis=0, keepdims=True)

    def tile(i, j, k):
        return n - 1 - step_of(i, j, k)

    def col(c):
        return pl.BlockSpec((ts, D_CONV), lambda i, j, k, c=c: (tile(i, j, k), c))

    def halo(c):
        return pl.BlockSpec((SUBLANES, D_CONV),
                            lambda i, j, k, c=c: (jnp.maximum(tile(i, j, k) * halo_blocks - 1, 0), c))

    small = lambda r: pl.BlockSpec((r, D_CONV), lambda i, j, k: (0, 0))
    return _Side(
        ins=[dy, z, z, z, z, z, w_t, gain],
        in_specs=[col(0), col(0), col(1), col(2), halo(1), halo(2), small(3), small(1)],
        outs=[jax.ShapeDtypeStruct((s_len, 6 * D_CONV), BF16),
              jax.ShapeDtypeStruct((3, D_CONV), F32), jax.ShapeDtypeStruct((1, D_CONV), F32)],
        out_specs=[pl.BlockSpec((ts, 3 * D_CONV), lambda i, j, k: (tile(i, j, k), 0)), small(3), small(1)],
        scratch=[pltpu.VMEM((SUBLANES, D_CONV), F32), pltpu.VMEM((24, D_CONV), F32),
                 pltpu.VMEM((SUBLANES, D_CONV), F32)],
        init=_zero_refs, body=body)


def _split3(v):
    hi = v.astype(BF16)
    r1 = v - hi.astype(F32)
    mid = r1.astype(BF16)
    lo = (r1 - mid.astype(F32)).astype(BF16)
    return jnp.concatenate([hi, mid, lo], axis=1)


def _tri_sum(tri, v):
    w = v.shape[1]
    dd = jnp.dot(tri, _split3(v), preferred_element_type=F32)
    return dd[:, :w] + dd[:, w:2 * w] + dd[:, 2 * w:]


def _chunk_masks(ts):
    r = jnp.arange(ts)
    same = (r[:, None] // CHUNK) == (r[None, :] // CHUNK)
    later = jnp.logical_and(same, r[None, :] > r[:, None]).astype(BF16)
    earlier = jnp.logical_and(same, r[None, :] < r[:, None]).astype(BF16)
    chunk_of_row = jnp.arange(ts // CHUNK * SUBLANES)[:, None] // SUBLANES
    member = (chunk_of_row == (r[None, :] // CHUNK)).astype(BF16)
    return later, earlier, member


def _sigmoid(v):
    return 1.0 / (1.0 + jnp.exp(-v))


def _gla_fwd(z, alow, wgu, bg, gg, y_in, *, ts=512, comm=None):
    s_len = z.shape[0]
    nch = ts // CHUNK
    scale = DK ** -0.5

    tri_u, _, ind8 = _chunk_masks(ts)

    def body(q_ref, k_ref, v_ref, og_ref, al_ref, wgu_ref, bg_ref, gg_ref, tu_ref, ind_ref, yin_ref,
             y_ref, o_ref, la_ref, st_ref, state_ref, kd_ref, qs_ref, dec_ref):
        del yin_ref
        i = pl.program_id(0)

        @pl.when(i == 0)
        def _():
            state_ref[...] = jnp.zeros_like(state_ref)

        pre = jnp.dot(al_ref[...].astype(BF16), wgu_ref[...], preferred_element_type=F32) + bg_ref[...]
        la = (jnp.minimum(pre, 0.0) - jnp.log(1.0 + jnp.exp(-jnp.abs(pre)))) * (1.0 / 16.0)
        la_ref[...] = la
        kd_ref[...] = (k_ref[...] * jnp.exp(_tri_sum(tu_ref[...], la))).astype(BF16)
        qs_ref[...] = (q_ref[...] * scale).astype(BF16)
        dec_ref[...] = jnp.exp(_tri_sum(ind_ref[...], la))

        def chunk(cl, carry):
            rows = pl.ds(pl.multiple_of(cl * CHUNK, CHUNK), CHUNK)
            dec = dec_ref[pl.ds(pl.multiple_of(cl * SUBLANES, SUBLANES), 1), :]
            for h in range(HEADS):
                ks = slice(h * DK, (h + 1) * DK)
                vs = slice(h * DV, (h + 1) * DV)
                kv_t = lax.dot_general(v_ref[rows, vs].astype(BF16), kd_ref[rows, ks],
                                       (((0,), (0,)), ((), ())), preferred_element_type=F32)
                st = state_ref[h] * dec[:, ks] + kv_t
                state_ref[h] = st
                st_ref[cl, h] = st
                o_ref[rows, vs] = lax.dot_general(qs_ref[rows, ks], st.astype(BF16),
                                                  (((1,), (1,)), ((), ())), preferred_element_type=F32)
            return carry

        lax.fori_loop(0, nch, chunk, 0, unroll=2)

        ggv = gg_ref[...]
        for h in range(HEADS):
            vs = slice(h * DV, (h + 1) * DV)
            o_h = o_ref[:, vs]
            og_h = og_ref[:, vs]
            ro = lax.rsqrt(jnp.mean(o_h * o_h, axis=-1, keepdims=True) + EPS)
            y_ref[:, vs] = (o_h * ro * ggv * (og_h * _sigmoid(og_h))).astype(BF16)

    def zcol(width, j):
        return pl.BlockSpec((ts, width), lambda i, j=j: (i, j))

    full = lambda shape: pl.BlockSpec(shape, lambda i: tuple(0 for _ in shape))
    return _call(
        body, grid=(s_len // ts,),
        in_specs=[zcol(512, 6), zcol(512, 7), zcol(1024, 4), zcol(1024, 5), zcol(LANES, 0),
                  full((LANES, 512)), full((1, 512)), full((1, DV)), full(tri_u.shape), full(ind8.shape),
                  pl.BlockSpec(memory_space=pl.ANY)],
        out_specs=[zcol(1024, 1), zcol(1024, 0), zcol(512, 0),
                   pl.BlockSpec((nch, HEADS, DV, DK), lambda i: (i, 0, 0, 0))],
        out_shape=[jax.ShapeDtypeStruct((s_len, 2048), BF16), jax.ShapeDtypeStruct((s_len, 1024), F32),
                   jax.ShapeDtypeStruct((s_len, 512), F32),
                   jax.ShapeDtypeStruct((s_len // CHUNK, HEADS, DV, DK), F32)],
        scratch_shapes=[pltpu.VMEM((HEADS, DV, DK), F32), pltpu.VMEM((ts, 512), BF16),
                        pltpu.VMEM((ts, 512), BF16), pltpu.VMEM((nch * SUBLANES, 512), F32)],
        aliases={10: 0},
        operands=(z, z, z, z, alow, wgu, bg, gg, tri_u, ind8, y_in), name="gla_fwd", comm=comm)


def _gla_bwd(dy, z, o, la, st, alow, wgu, gg, dz_in, *, ts=512, comm=None):
    s_len = z.shape[0]
    n = s_len // ts
    nch = ts // CHUNK
    scale = DK ** -0.5

    tri_u, tri_l, ind8 = _chunk_masks(ts)

    def body(dy_ref, q_ref, k_ref, v_ref, og_ref, o_ref, la_ref, st_ref, stp_ref, al_ref, wgu_ref,
             gg_ref, tu_ref, tl_ref, ind_ref, dzin_ref, dz_ref, dal_ref, dwgu_ref, dbg_ref, dgg_ref,
             gt_ref, decn_ref, accw_ref, accb_ref, accg_ref, dla_ref,
             e_ref, kd_ref, kdb_ref, qs_ref, do_ref, dkd_ref, dec_ref, dbe_ref):
        del dzin_ref
        i = pl.program_id(0)
        first_tile = (n - 1 - i) == 0

        @pl.when(i == 0)
        def _():
            gt_ref[...] = jnp.zeros_like(gt_ref)
            decn_ref[...] = jnp.ones_like(decn_ref)
            accw_ref[...] = jnp.zeros_like(accw_ref)
            accb_ref[...] = jnp.zeros_like(accb_ref)
            accg_ref[...] = jnp.zeros_like(accg_ref)

        la = la_ref[...]
        e_dec = jnp.exp(_tri_sum(tu_ref[...], la))
        e_ref[...] = e_dec
        kd = k_ref[...] * e_dec
        kd_ref[...] = kd
        kdb_ref[...] = kd.astype(BF16)
        qs_ref[...] = (q_ref[...] * scale).astype(BF16)
        dec_ref[...] = jnp.exp(_tri_sum(ind_ref[...], la))
        ggv = gg_ref[...]
        for h in range(HEADS):
            vs = slice(h * DV, (h + 1) * DV)
            o_h = o_ref[:, vs]
            og_h = og_ref[:, vs]
            dy_h = dy_ref[:, vs].astype(F32)
            ro = lax.rsqrt(jnp.mean(o_h * o_h, axis=-1, keepdims=True) + EPS)
            oh = o_h * ro
            sig = _sigmoid(og_h)
            sil = og_h * sig
            accg_ref[...] += _rowsum8(dy_h * oh * sil)
            dz_ref[:, 2048 + h * DV:2048 + (h + 1) * DV] = (
                dy_h * oh * ggv * sig * (1.0 + og_h * (1.0 - sig))).astype(BF16)
            don = dy_h * ggv * sil
            do_ref[:, vs] = (ro * (don - oh * jnp.mean(don * oh, axis=-1, keepdims=True))).astype(BF16)
        keep = jnp.where(first_tile, 0.0, 1.0)

        def chunk(jrev, decn):
            cl = nch - 1 - jrev
            rows = pl.ds(pl.multiple_of(cl * CHUNK, CHUNK), CHUNK)
            one_row = pl.ds(pl.multiple_of(cl * SUBLANES, SUBLANES), 1)
            dec = dec_ref[one_row, :]
            has_prev = jnp.where(cl > 0, 1.0, 0.0)
            prev_idx = jnp.maximum(cl - 1, 0)
            for h in range(HEADS):
                ks = slice(h * DK, (h + 1) * DK)
                vs = slice(h * DV, (h + 1) * DV)
                dob = do_ref[rows, vs]
                s_c = st_ref[cl, h]
                dqs = jnp.dot(dob, s_c.astype(BF16), preferred_element_type=F32)
                dz_ref[rows, ks] = (dqs * scale).astype(BF16)
                gt = gt_ref[h] * decn[:, ks] + lax.dot_general(
                    dob, qs_ref[rows, ks], (((0,), (0,)), ((), ())), preferred_element_type=F32)
                gt_ref[h] = gt
                gb = gt.astype(BF16)
                dkd_ref[rows, ks] = jnp.dot(v_ref[rows, vs].astype(BF16), gb, preferred_element_type=F32)
                dz_ref[rows, 1024 + h * DV:1024 + (h + 1) * DV] = lax.dot_general(
                    kdb_ref[rows, ks], gb, (((1,), (1,)), ((), ())),
                    preferred_element_type=F32).astype(BF16)
                s_prev = has_prev * st_ref[prev_idx, h] + (1.0 - has_prev) * keep * stp_ref[0, h]
                dbe_ref[one_row, ks] = jnp.sum(gt * s_prev, axis=0, keepdims=True) * dec[:, ks]
            return dec

        decn_ref[0:1, :] = lax.fori_loop(0, nch, chunk, decn_ref[0:1, :], unroll=2)

        dkd = dkd_ref[...]
        dz_ref[:, 512:1024] = (dkd * e_ref[...]).astype(BF16)
        dla_ref[...] = _tri_sum(tl_ref[...], dkd * kd_ref[...])
        for c in range(nch):
            dla_ref[c * CHUNK:(c + 1) * CHUNK, :] += dbe_ref[c * SUBLANES:c * SUBLANES + 1, :]
        dpre = dla_ref[...] * (1.0 / 16.0) * (1.0 - jnp.exp(16.0 * la))
        accb_ref[...] += _rowsum8(dpre)
        dpb = dpre.astype(BF16)
        accw_ref[...] += lax.dot_general(al_ref[...].astype(BF16), dpb, (((0,), (0,)), ((), ())),
                                         preferred_element_type=F32)
        dal_ref[...] = lax.dot_general(dpb, wgu_ref[...], (((1,), (1,)), ((), ())),
                                       preferred_element_type=F32)

        @pl.when(i == n - 1)
        def _():
            dwgu_ref[...] = accw_ref[...]
            dbg_ref[...] = jnp.sum(accb_ref[...], axis=0, keepdims=True)
            dgg_ref[...] = jnp.sum(accg_ref[...], axis=0, keepdims=True)

    def zcol(width, j):
        return pl.BlockSpec((ts, width), lambda i, j=j: (n - 1 - i, j))

    full = lambda shape: pl.BlockSpec(shape, lambda i: tuple(0 for _ in shape))
    return _call(
        body, grid=(n,),
        in_specs=[zcol(1024, 1), zcol(512, 6), zcol(512, 7), zcol(1024, 4), zcol(1024, 5),
                  zcol(1024, 0), zcol(512, 0),
                  pl.BlockSpec((nch, HEADS, DV, DK), lambda i: (n - 1 - i, 0, 0, 0)),
                  pl.BlockSpec((1, HEADS, DV, DK),
                               lambda i: (jnp.maximum((n - 1 - i) * nch - 1, 0), 0, 0, 0)),
                  zcol(LANES, 0), full((LANES, 512)), full((1, DV)),
                  full(tri_u.shape), full(tri_l.shape), full(ind8.shape),
                  pl.BlockSpec(memory_space=pl.ANY)],
        out_specs=[zcol(3072, 1), zcol(LANES, 0), full((LANES, 512)), full((1, 512)), full((1, DV))],
        out_shape=[jax.ShapeDtypeStruct((s_len, 6144), BF16), jax.ShapeDtypeStruct((s_len, LANES), F32),
                   jax.ShapeDtypeStruct((LANES, 512), F32), jax.ShapeDtypeStruct((1, 512), F32),
                   jax.ShapeDtypeStruct((1, DV), F32)],
        scratch_shapes=[pltpu.VMEM((HEADS, DV, DK), F32), pltpu.VMEM((SUBLANES, 512), F32),
                        pltpu.VMEM((LANES, 512), F32), pltpu.VMEM((SUBLANES, 512), F32),
                        pltpu.VMEM((SUBLANES, DV), F32), pltpu.VMEM((ts, 512), F32),
                        pltpu.VMEM((ts, 512), F32), pltpu.VMEM((ts, 512), F32), pltpu.VMEM((ts, 512), BF16),
                        pltpu.VMEM((ts, 512), BF16), pltpu.VMEM((ts, 1024), BF16),
                        pltpu.VMEM((ts, 512), F32), pltpu.VMEM((nch * SUBLANES, 512), F32),
                        pltpu.VMEM((nch * SUBLANES, 512), F32)],
        aliases={15: 0},
        operands=(dy, z, z, z, z, o, la, st, st, alow, wgu, gg, tri_u, tri_l, ind8, dz_in),
        name="gla_bwd", comm=comm)


def _adamw(w, g, m, v, *, name):
    rows, cols = w.shape
    tr, tc = _tile(rows, cols)

    def body(w_ref, g_ref, m_ref, v_ref, go_ref, d_ref, nm_ref, nv_ref):
        gv = g_ref[...]
        go_ref[...] = gv
        m2 = ADAM_B1 * m_ref[...] + (1.0 - ADAM_B1) * gv
        v2 = ADAM_B2 * v_ref[...] + (1.0 - ADAM_B2) * jnp.square(gv)
        m_hat = m2 / (1.0 - ADAM_B1 ** ADAM_STEP)
        v_hat = v2 / (1.0 - ADAM_B2 ** ADAM_STEP)
        d_ref[...] = -ADAM_LR * (m_hat / (jnp.sqrt(v_hat) + ADAM_EPS) + ADAM_WD * w_ref[...])
        nm_ref[...] = m2
        nv_ref[...] = v2

    blk = pl.BlockSpec((tr, tc), lambda i, j: (i, j))
    shp = jax.ShapeDtypeStruct((rows, cols), F32)
    return pl.pallas_call(
        body, grid=(rows // tr, cols // tc), in_specs=[blk] * 4, out_specs=[blk] * 4, out_shape=[shp] * 4,
        compiler_params=_params("parallel", "parallel"), name=name,
    )(w, g, m, v)


def _my_place():
    return lax.axis_index("x"), lax.axis_index("y"), lax.axis_index("c")


def _flip(v, bit):
    return 1 - v if bit else v


def _allgather_small(buf, *, reduce, name):
    rows = buf.shape[0]

    def body(in_ref, out_ref, gat_ref, send_sems, recv_sems):
        x, y, c = _my_place()
        me = 4 * x + 2 * y + c
        gat_ref[me] = in_ref[...]
        copies = []
        for m in range(1, N_DEV):
            peer = (_flip(x, m & 4), _flip(y, m & 2), _flip(c, m & 1))
            cp = pltpu.make_async_remote_copy(
                src_ref=in_ref, dst_ref=gat_ref.at[me],
                send_sem=send_sems.at[m - 1], recv_sem=recv_sems.at[m - 1],
                device_id=peer, device_id_type=MESH)
            cp.start()
            copies.append(cp)
        for m in range(1, N_DEV):
            px, py, pc = _flip(x, m & 4), _flip(y, m & 2), _flip(c, m & 1)
            src_slot = gat_ref.at[4 * px + 2 * py + pc]
            pltpu.make_async_remote_copy(
                src_ref=src_slot, dst_ref=src_slot,
                send_sem=send_sems.at[m - 1], recv_sem=recv_sems.at[m - 1],
                device_id=(px, py, pc), device_id_type=MESH).wait_recv()
        for cp in copies:
            cp.wait_send()
        if reduce:
            tot = gat_ref[0]
            for d in range(1, N_DEV):
                tot = tot + gat_ref[d]
            out_ref[...] = tot
        else:
            out_ref[...] = gat_ref[...]

    out_shape = (rows, LANES) if reduce else (N_DEV, rows, LANES)
    return pl.pallas_call(
        body,
        in_specs=[pl.BlockSpec(memory_space=pltpu.VMEM)],
        out_specs=pl.BlockSpec(memory_space=pltpu.VMEM),
        out_shape=jax.ShapeDtypeStruct(out_shape, F32),
        scratch_shapes=[pltpu.VMEM((N_DEV, rows, LANES), F32),
                        pltpu.SemaphoreType.DMA((N_DEV - 1,)), pltpu.SemaphoreType.DMA((N_DEV - 1,))],
        compiler_params=pltpu.CompilerParams(has_side_effects=True),
        name=name,
    )(buf)


def _cast_into(shard, chip_core, *, name):
    rows, cols = shard.shape
    tr, tc = _tile(rows, cols)

    def body(cc_ref, s_ref, o_ref):
        del cc_ref
        o_ref[...] = s_ref[...].astype(BF16)

    grid_spec = pltpu.PrefetchScalarGridSpec(
        num_scalar_prefetch=1, grid=(rows // tr, cols // tc),
        in_specs=[pl.BlockSpec((tr, tc), lambda r, q, cc: (r, q))],
        out_specs=pl.BlockSpec((None, tr, tc), lambda r, q, cc: (cc[0], r, q)))
    return pl.pallas_call(
        body, grid_spec=grid_spec, out_shape=jax.ShapeDtypeStruct((N_CHIPS, rows, cols), BF16),
        compiler_params=_params("arbitrary", "arbitrary"), name=name,
    )(chip_core, shard)


def _remote(src, dst, send_sems, recv_sems, k, device):
    return pltpu.make_async_remote_copy(src_ref=src, dst_ref=dst, send_sem=send_sems.at[k],
                                        recv_sem=recv_sems.at[k], device_id=device, device_id_type=MESH)


def _col_half(ref, h, *lead, rows=None):
    hc = ref.shape[-1] // 2
    mid = (slice(None),) * (len(ref.shape) - 2 - len(lead))
    row_sel = slice(None) if rows is None else pl.ds(rows[0], rows[1])
    return ref.at[tuple(lead) + mid + (row_sel, pl.ds(h * hc, hc))]


def _gather_comm(bufs, rows=None):
    n_w, n_m = len(bufs), len(CHIP_MASKS)
    rows = rows or [None] * n_w

    def first(c_ins, c_outs, ss, rs):
        x, y, c = _my_place()
        chip = 2 * x + y
        for w in range(n_w):
            mine = _col_half(c_outs[w], c, chip, rows=rows[w])
            for mi, (mx, my) in enumerate(CHIP_MASKS):
                _remote(mine, mine, ss, rs, w * n_m + mi, (_flip(x, mx), _flip(y, my), c)).start()

    def mid(c_ins, c_outs, ss, rs):
        x, y, c = _my_place()
        for w in range(n_w):
            for mi, (mx, my) in enumerate(CHIP_MASKS):
                k = w * n_m + mi
                px, py = _flip(x, mx), _flip(y, my)
                landed = _col_half(c_outs[w], c, 2 * px + py, rows=rows[w])
                _remote(landed, landed, ss, rs, k, (px, py, c)).wait_recv()
                _remote(landed, landed, ss, rs, n_w * n_m + k, (x, y, 1 - c)).start()

    def last(c_ins, c_outs, ss, rs):
        x, y, c = _my_place()
        chip = 2 * x + y
        for w in range(n_w):
            mine = _col_half(c_outs[w], c, chip, rows=rows[w])
            for mi, (mx, my) in enumerate(CHIP_MASKS):
                k = w * n_m + mi
                px, py = _flip(x, mx), _flip(y, my)
                theirs = _col_half(c_outs[w], 1 - c, 2 * px + py, rows=rows[w])
                _remote(theirs, theirs, ss, rs, n_w * n_m + k, (x, y, 1 - c)).wait_recv()
                _remote(mine, mine, ss, rs, k, (px, py, c)).wait_send()
                _remote(mine, mine, ss, rs, n_w * n_m + k, (x, y, 1 - c)).wait_send()

    return _Comm(ins=bufs, outs=[jax.ShapeDtypeStruct(b.shape, b.dtype) for b in bufs],
                 aliases={w: w for w in range(n_w)}, n_sems=2 * n_w * n_m, first=first, mid=mid, last=last)


def _swap_comm(grads):
    n_w = len(grads)

    def copy(c_ins, c_outs, ss, rs, w):
        x, y, c = _my_place()
        return _remote(_col_half(c_ins[w], 1 - c), c_outs[w], ss, rs, w, (x, y, 1 - c))

    def first(c_ins, c_outs, ss, rs):
        for w in range(n_w):
            copy(c_ins, c_outs, ss, rs, w).start()

    def last(c_ins, c_outs, ss, rs):
        for w in range(n_w):
            copy(c_ins, c_outs, ss, rs, w).wait()

    return _Comm(ins=grads,
                 outs=[jax.ShapeDtypeStruct(g.shape[:2] + (g.shape[2] // 2,), g.dtype) for g in grads],
                 aliases={}, n_sems=n_w, first=first, last=last)


def _add_own_half(g, other, chip_core, *, name):
    n_chip, rows, hc = other.shape
    tr, tc = _tile(rows, hc)
    per_half = hc // tc

    def body(cc_ref, g_ref, o_ref, out_ref):
        del cc_ref
        out_ref[...] = (g_ref[...].astype(F32) + o_ref[...].astype(F32)).astype(BF16)

    grid_spec = pltpu.PrefetchScalarGridSpec(
        num_scalar_prefetch=1, grid=(n_chip, rows // tr, per_half),
        in_specs=[pl.BlockSpec((None, tr, tc), lambda j, r, q, cc: (j, r, cc[1] * per_half + q)),
                  pl.BlockSpec((None, tr, tc), lambda j, r, q, cc: (j, r, q))],
        out_specs=pl.BlockSpec((None, tr, tc), lambda j, r, q, cc: (j, r, q)))
    return pl.pallas_call(
        body, grid_spec=grid_spec, out_shape=jax.ShapeDtypeStruct((n_chip, rows, hc), BF16),
        compiler_params=_params("parallel", "parallel", "parallel"), name=name,
    )(chip_core, g, other)


def _exchange_comm(pieces):
    n_w, n_m = len(pieces), len(CHIP_MASKS)

    def copies(c_ins, c_outs, ss, rs):
        x, y, c = _my_place()
        chip = 2 * x + y
        for w in range(n_w):
            for mi, (mx, my) in enumerate(CHIP_MASKS):
                px, py = _flip(x, mx), _flip(y, my)
                send = _remote(c_ins[w].at[2 * px + py], c_outs[w].at[chip], ss, rs, w * n_m + mi, (px, py, c))
                landed = c_outs[w].at[2 * px + py]
                yield send, _remote(landed, landed, ss, rs, w * n_m + mi, (px, py, c))

    def first(c_ins, c_outs, ss, rs):
        for send, _ in copies(c_ins, c_outs, ss, rs):
            send.start()

    def last(c_ins, c_outs, ss, rs):
        for send, arrival in copies(c_ins, c_outs, ss, rs):
            arrival.wait_recv()
            send.wait_send()

    return _Comm(ins=pieces, outs=[jax.ShapeDtypeStruct(p.shape, p.dtype) for p in pieces],
                 aliases={}, n_sems=n_w * n_m, first=first, last=last)


def _sum_chips(own, landed, chip_core, *, name):
    n_chip, rows, hc = own.shape
    tr, tc = _tile(rows, hc)
    per_half = hc // tc

    def body(cc_ref, o_ref, l1_ref, l2_ref, l3_ref, out_ref):
        del cc_ref
        out_ref[...] = ((o_ref[...].astype(F32) + l1_ref[...].astype(F32))
                        + l2_ref[...].astype(F32)) + l3_ref[...].astype(F32)

    def slot(k):
        return pl.BlockSpec((None, tr, tc), lambda r, q, cc, k=k: ((cc[0] + k) % n_chip, r, q))

    grid_spec = pltpu.PrefetchScalarGridSpec(
        num_scalar_prefetch=1, grid=(rows // tr, per_half),
        in_specs=[slot(0), slot(1), slot(2), slot(3)],
        out_specs=pl.BlockSpec((tr, tc), lambda r, q, cc: (r, cc[1] * per_half + q)))
    return pl.pallas_call(
        body, grid_spec=grid_spec, out_shape=jax.ShapeDtypeStruct((rows, 2 * hc), F32),
        compiler_params=_params("arbitrary", "arbitrary"), name=name,
    )(chip_core, own, landed, landed, landed)


def _join_comm(halves):
    n_w = len(halves)

    def first(c_ins, c_outs, ss, rs):
        x, y, c = _my_place()
        for w in range(n_w):
            mine = _col_half(c_outs[w], c)
            _remote(mine, mine, ss, rs, w, (x, y, 1 - c)).start()

    def last(c_ins, c_outs, ss, rs):
        x, y, c = _my_place()
        for w in range(n_w):
            theirs = _col_half(c_outs[w], 1 - c)
            _remote(theirs, theirs, ss, rs, w, (x, y, 1 - c)).wait()

    return _Comm(ins=halves, outs=[jax.ShapeDtypeStruct(h.shape, h.dtype) for h in halves],
                 aliases={w: w for w in range(n_w)}, n_sems=n_w, first=first, last=last)


def _standalone(comm, *, name):
    def body(o_ref):
        o_ref[...] = jnp.zeros_like(o_ref)

    return _call(body, grid=(1,), in_specs=[],
                 out_specs=[pl.BlockSpec((SUBLANES, LANES), lambda i: (0, 0))],
                 out_shape=[jax.ShapeDtypeStruct((SUBLANES, LANES), F32)], operands=(), name=name,
                 comm=comm)[1:]


def _pack(pieces):
    flat, spans, off = [], [], 0
    for p in pieces:
        v = p.reshape(-1).astype(F32)
        pad = (-v.shape[0]) % LANES
        if pad:
            v = jnp.concatenate([v, jnp.zeros((pad,), F32)])
        spans.append((off, p.size))
        off += v.shape[0]
        flat.append(v)
    tail = (-off) % (SUBLANES * LANES)
    if tail:
        flat.append(jnp.zeros((tail,), F32))
    return jnp.concatenate(flat).reshape(-1, LANES), spans


def _unpack(buf, span, shape):
    off, size = span
    return buf.reshape(-1)[off:off + size].reshape(shape)


def kernel(x, norm1_g, w_in, w_gate_up, b_gate, conv_w, conv_norm_g, gla_norm_g, w_out, norm2_g, w_ff1, w_ff2, norm_f_g, loss_target, m_norm1_g, m_w_in, m_w_gate_up, m_b_gate, m_conv_w, m_conv_norm_g, m_gla_norm_g, m_w_out, m_norm2_g, m_w_ff1, m_w_ff2, m_norm_f_g, v_norm1_g, v_w_in, v_w_gate_up, v_b_gate, v_conv_w, v_conv_norm_g, v_gla_norm_g, v_w_out, v_norm2_g, v_w_ff1, v_w_ff2, v_norm_f_g):
    xs = x[0]
    target = loss_target[0]
    s_len, d = xs.shape
    d_in = w_in.shape[2] * N_CHIPS
    d_main = d_in - GATE_RANK
    d_ff = w_ff1.shape[2] * N_CHIPS
    cx, cy, cc = _my_place()
    chip = 2 * cx + cy
    chip_core = jnp.stack([chip, cc]).astype(jnp.int32)
    n_ff = d_ff // N_CHIPS
    norm_f = norm_f_g.reshape(1, d)

    n_sh = d_in // N_CHIPS
    wi_buf = _cast_into(w_in[0].T, chip_core, name="cast_w_in")
    wo_buf = _cast_into(w_out[0], chip_core, name="cast_w_out")
    w1_buf = _cast_into(w_ff1[0], chip_core, name="cast_w_ff1")
    w2_buf = _cast_into(w_ff2[0], chip_core, name="cast_w_ff2")

    small_w, spans_w = _pack([w_gate_up[0], conv_w[0]])
    small_all = _allgather_small(small_w, reduce=False, name="gather_small_weights")
    chips_first = [small_all[2 * j] for j in range(N_CHIPS)]
    wgu_full = jnp.concatenate(
        [_unpack(b, spans_w[0], w_gate_up.shape[1:]) for b in chips_first], axis=1)
    convw_full = jnp.concatenate(
        [_unpack(b, spans_w[1], conv_w.shape[1:]) for b in chips_first], axis=0)
    wgu_pad = jnp.concatenate(
        [wgu_full, jnp.zeros((LANES - GATE_RANK, wgu_full.shape[1]), F32)], axis=0).astype(BF16)
    convw_t = convw_full.T

    u, wi_buf = _rms_fwd(xs, norm1_g, name="norm1_fwd", comm=_gather_comm([wi_buf]))
    wi_t = wi_buf.reshape(d_in, d)
    wg_t = jnp.concatenate([wi_t[d_main:], jnp.zeros((LANES - GATE_RANK, d), BF16)], axis=0)
    z, wo_buf, w1_buf = _matmul(u, wi_t, tb=True, tm=1024, tn=1024, tk=d, out_dtypes=[F32], n_dim=d_main,
                                name="in_proj",
                                comm=_gather_comm([wo_buf, w1_buf], rows=[None, (0, d // 2)]))
    wo_full = wo_buf.reshape(d, d)
    (alow,) = _matmul(u, wg_t, tb=True, tm=1024, tn=LANES, tk=d, out_dtypes=[F32], name="in_proj_gate")
    y0 = _conv_fwd(z, convw_t, conv_norm_g)
    y, o, la, st, w1_cm = _gla_fwd(z, alow, wgu_pad, b_gate, gla_norm_g, y0,
                                   comm=_gather_comm([w1_buf], rows=[(d // 2, d // 2)]))
    (x2,) = _matmul(y, wo_full, tm=1024, tn=1024, tk=d, out_dtypes=[F32], extras=(xs,),
                    epilogue=_add_epilogue, name="out_proj")
    (h,) = _rms_fwd(x2, norm2_g, name="norm2_fwd")
    a, p, w2_buf = _matmul(
        h, w1_cm, tm=1024, tn=1024, tk=d, out_dtypes=[F32, BF16], n_dim=d_ff,
        b_spec=pl.BlockSpec((None, d, 1024), lambda i, j, k: (j // 2, 0, j % 2)),
        epilogue=lambda acc: (acc, jnp.square(jnp.maximum(acc, 0.0))), name="ff1",
        comm=_gather_comm([w2_buf]))
    w2_full = w2_buf.reshape(d_ff, d)
    (x3,) = _matmul(p, w2_full, tm=1024, tn=1024, tk=2048, out_dtypes=[F32], extras=(x2,),
                    epilogue=_add_epilogue, name="ff2")
    dx3b, g_normf, loss_part = _loss_head(x3, norm_f, target)

    (da,) = _matmul(dx3b, w2_full, tb=True, tm=1024, tn=1024, tk=d, out_dtypes=[BF16], extras=(a,),
                    epilogue=lambda acc, av: (acc * (2.0 * jnp.maximum(av, 0.0)),), name="ff2_dx")
    (dh,) = _matmul(
        da, w1_cm, tb=True, tm=1024, tn=1024, tk=2048, out_dtypes=[BF16], n_dim=d,
        b_spec=pl.BlockSpec((None, 1024, 2048), lambda i, j, k: (k, j, 0)), name="ff1_dx")
    dw_steps = (d_ff // 1024) * (s_len // 1024)
    g_w2, dx2, g_norm2 = _matmul(
        p, dx3b, ta=True, tm=1024, tn=d, tk=1024, out_dtypes=[BF16], name="ff2_dw",
        side=_rms_bwd_side(dh, x2, norm2_g, dx3b, n_steps=dw_steps,
                           block_of_step=lambda i, j, k: i * (s_len // 1024) + k))
    (g_w1,) = _matmul(
        h, da, ta=True, tm=1024, tn=n_ff, tk=1024, out_dtypes=[BF16], name="ff1_dw",
        out_shapes=[jax.ShapeDtypeStruct((N_CHIPS, d, n_ff), BF16)],
        out_specs=[pl.BlockSpec((None, 1024, n_ff), lambda i, j, k: (j, i, 0))])
    g_w2 = g_w2.reshape(N_CHIPS, n_ff, d)
    dy, t_w1, t_w2 = _matmul(dx2, wo_full, tb=True, tm=1024, tn=1024, tk=d, out_dtypes=[BF16],
                             name="out_proj_dx", comm=_swap_comm([g_w1, g_w2]))
    p_w1 = _add_own_half(g_w1, t_w1, chip_core, name="pre_reduce_w_ff1")
    p_w2 = _add_own_half(g_w2, t_w2, chip_core, name="pre_reduce_w_ff2")
    ow_k = s_len // 512
    g_wo, dz0, g_convw_t, g_convg = _matmul(
        y, dx2, ta=True, tm=1024, tn=d, tk=512, out_dtypes=[BF16], name="out_proj_dw",
        side=_conv_bwd_side(dy, z, convw_t, conv_norm_g, n_steps=(d // 1024) * ow_k,
                            step_of=lambda i, j, k: i * ow_k + k))
    g_wo = g_wo.reshape(N_CHIPS, d // N_CHIPS, d)
    dz, dalow, g_wgu_pad, g_bg, g_gg, l_w2 = _gla_bwd(dy, z, o, la, st, alow, wgu_pad, gla_norm_g, dz0,
                                                      comm=_exchange_comm([p_w2]))
    m_w2 = _sum_chips(p_w2, l_w2, chip_core, name="reduce_w_ff2")
    g_wi_t, l_w1 = _matmul(dz, u, ta=True, tm=1024, tn=d, tk=1024, out_dtypes=[BF16], name="in_proj_dw",
                           out_shapes=[jax.ShapeDtypeStruct((d_in, d), BF16)],
                           comm=_exchange_comm([p_w1]))
    (g_wi_t,) = _matmul(dalow, u, ta=True, tm=LANES, tn=d, tk=1024, out_dtypes=[BF16],
                        epilogue=lambda acc: (acc[:GATE_RANK],), into=g_wi_t, name="in_proj_gate_dw",
                        out_shapes=[jax.ShapeDtypeStruct((d_in, d), BF16)],
                        out_specs=[pl.BlockSpec((GATE_RANK, d), lambda i, j, k: (d_main // GATE_RANK, 0))])
    m_w1 = _sum_chips(p_w1, l_w1, chip_core, name="reduce_w_ff1")
    g_wi = g_wi_t.reshape(N_CHIPS, n_sh, d)
    du_gate, t_wi, t_wo = _matmul(dalow, wg_t, tm=1024, tn=1024, tk=LANES, out_dtypes=[F32],
                                  name="in_proj_gate_dx", comm=_swap_comm([g_wi, g_wo]))
    p_wi = _add_own_half(g_wi, t_wi, chip_core, name="pre_reduce_w_in")
    p_wo = _add_own_half(g_wo, t_wo, chip_core, name="pre_reduce_w_out")
    du, l_wi, l_wo, m_w1, m_w2 = _matmul(
        dz, wi_t, tm=1024, tn=1024, tk=2048, out_dtypes=[BF16], n_dim=d, extras=(du_gate,),
        epilogue=_add_epilogue, name="in_proj_dx",
        comm=[_exchange_comm([p_wi, p_wo]), _join_comm([m_w1, m_w2])])
    m_wi = _sum_chips(p_wi, l_wi, chip_core, name="reduce_w_in")
    m_wo = _sum_chips(p_wo, l_wo, chip_core, name="reduce_w_out")
    grad_x, g_norm1 = _rms_bwd(du, xs, norm1_g, dx2, name="norm1_bwd")
    m_wi, m_wo = _standalone(_join_comm([m_wi, m_wo]), name="join_w_in_w_out")
    g_big = [m_wi, m_wo, m_w1, m_w2]

    small_g, spans_g = _pack([g_norm1, g_wgu_pad[:GATE_RANK], g_bg, g_convw_t, g_convg, g_gg, g_norm2,
                              g_normf, loss_part[:, :1]])
    tot = _allgather_small(small_g, reduce=True, name="reduce_small_grads")
    t_norm1 = _unpack(tot, spans_g[0], (1, d))
    t_wgu = _unpack(tot, spans_g[1], (GATE_RANK, HEADS * DK))
    t_bg = _unpack(tot, spans_g[2], (1, HEADS * DK))
    t_convw = _unpack(tot, spans_g[3], (3, D_CONV)).T
    t_convg = _unpack(tot, spans_g[4], (1, D_CONV))
    t_gg = _unpack(tot, spans_g[5], (1, DV))
    t_norm2 = _unpack(tot, spans_g[6], (1, d))
    t_normf = _unpack(tot, spans_g[7], (1, d))
    loss = _unpack(tot, spans_g[8], ())
    n_gu = w_gate_up.shape[2]
    n_cw = conv_w.shape[1]
    t_wgu = lax.dynamic_slice(t_wgu, (0, chip * n_gu), (GATE_RANK, n_gu))
    t_convw = lax.dynamic_slice(t_convw, (chip * n_cw, 0), (n_cw, 3))

    order = ["norm1_g", "w_in", "w_gate_up", "b_gate", "conv_w", "conv_norm_g", "gla_norm_g", "w_out",
             "norm2_g", "w_ff1", "w_ff2", "norm_f_g"]
    weights = dict(norm1_g=norm1_g, w_in=w_in, w_gate_up=w_gate_up, b_gate=b_gate, conv_w=conv_w,
                   conv_norm_g=conv_norm_g, gla_norm_g=gla_norm_g, w_out=w_out, norm2_g=norm2_g,
                   w_ff1=w_ff1, w_ff2=w_ff2, norm_f_g=norm_f_g)
    moms = dict(norm1_g=m_norm1_g, w_in=m_w_in, w_gate_up=m_w_gate_up, b_gate=m_b_gate, conv_w=m_conv_w,
                conv_norm_g=m_conv_norm_g, gla_norm_g=m_gla_norm_g, w_out=m_w_out, norm2_g=m_norm2_g,
                w_ff1=m_w_ff1, w_ff2=m_w_ff2, norm_f_g=m_norm_f_g)
    vels = dict(norm1_g=v_norm1_g, w_in=v_w_in, w_gate_up=v_w_gate_up, b_gate=v_b_gate, conv_w=v_conv_w,
                conv_norm_g=v_conv_norm_g, gla_norm_g=v_gla_norm_g, w_out=v_w_out, norm2_g=v_norm2_g,
                w_ff1=v_w_ff1, w_ff2=v_w_ff2, norm_f_g=v_norm_f_g)
    grads2d = dict(norm1_g=t_norm1, w_in=g_big[0], w_gate_up=t_wgu, b_gate=t_bg, conv_w=t_convw,
                   conv_norm_g=t_convg, gla_norm_g=t_gg, w_out=g_big[1], norm2_g=t_norm2,
                   w_ff1=g_big[2], w_ff2=g_big[3], norm_f_g=t_normf)
    out_g, out_d, out_m, out_v = [], [], [], []
    for nm in order:
        w = weights[nm]
        g2 = grads2d[nm]
        if nm == "w_in":
            to2d, back = (lambda t: t[0].T), (lambda t: t.T.reshape(w.shape))
        else:
            to2d, back = (lambda t: t.reshape(g2.shape)), (lambda t: t.reshape(w.shape))
        res = _adamw(to2d(w), g2, to2d(moms[nm]), to2d(vels[nm]), name="adamw_" + nm)
        for lst, r in zip((out_g, out_d, out_m, out_v), res):
            lst.append(back(r))
    return (loss, grad_x.reshape(x.shape), *out_g, *out_d, *out_m, *out_v)
```

```python
import functools

import jax
import jax.numpy as jnp
from jax import lax
from jax.experimental import pallas as pl
from jax.experimental.pallas import tpu as pltpu

F32 = jnp.float32
BF16 = jnp.bfloat16
MESH = pl.DeviceIdType.MESH

EPS = 1e-6
CHUNK = 64
HEADS = 4
DK = 128
DV = 256
D_CONV = 1024
GROUP = 128
GATE_RANK = 16
LANES = 128
SUBLANES = 8
N_CHIPS = 4
N_DEV = 8
CHIP_MASKS = ((1, 0), (0, 1), (1, 1))

ADAM_LR = 0.001
ADAM_B1 = 0.9
ADAM_B2 = 0.999
ADAM_EPS = 1e-08
ADAM_WD = 0.01
ADAM_STEP = 10

VMEM_LIMIT = 56 * 1024 * 1024


def _params(*sem):
    return pltpu.CompilerParams(dimension_semantics=tuple(sem), vmem_limit_bytes=VMEM_LIMIT)


def _rowsum8(v):
    r, c = v.shape
    return jnp.sum(v.reshape(r // SUBLANES, SUBLANES, c), axis=0)


def _tile(rows, cols):
    for cand in (256, 128, 64, 32, 16, 8):
        if rows % cand == 0 and rows > cand:
            return cand, cols
    if rows * cols * 4 > (2 << 20) and cols % 256 == 0:
        return rows, 256
    return rows, cols


class _Comm:
    def __init__(self, ins, outs, aliases, n_sems, first, last, mid=None, mid_at=0.75):
        self.ins = list(ins)
        self.outs = list(outs)
        self.aliases = dict(aliases)
        self.n_sems = n_sems
        self.first = first
        self.mid = mid
        self.mid_at = mid_at
        self.last = last


def _call(body, *, grid, in_specs, out_specs, out_shape, operands, name, scratch_shapes=(), sem=None,
          aliases=None, comm=None):
    aliases = dict(aliases or {})
    if comm is None:
        return pl.pallas_call(
            body, grid=grid, in_specs=list(in_specs), out_specs=list(out_specs), out_shape=list(out_shape),
            scratch_shapes=list(scratch_shapes), input_output_aliases=aliases,
            compiler_params=_params(*(sem or ("arbitrary",) * len(grid))), name=name)(*operands)
    comms = list(comm) if isinstance(comm, (list, tuple)) else [comm]
    n_in, n_out, n_scr = len(in_specs), len(out_specs), len(scratch_shapes)
    c_ins_all = [a for cm in comms for a in cm.ins]
    c_outs_all = [o for cm in comms for o in cm.outs]
    n_ci, n_co = len(c_ins_all), len(c_outs_all)

    def full_body(*refs):
        ins = refs[:n_in]
        o0 = n_in + n_ci
        outs = refs[o0:o0 + n_out]
        s0 = o0 + n_out + n_co
        scr = refs[s0:s0 + n_scr]
        sems = refs[s0 + n_scr:]
        parts, i_at, o_at = [], n_in, o0 + n_out
        for q, cm in enumerate(comms):
            parts.append((cm, refs[i_at:i_at + len(cm.ins)], refs[o_at:o_at + len(cm.outs)],
                          sems[2 * q], sems[2 * q + 1]))
            i_at += len(cm.ins)
            o_at += len(cm.outs)
        step = functools.reduce(lambda acc, ig: acc * ig[1] + pl.program_id(ig[0]), enumerate(grid), 0)
        n_steps = functools.reduce(lambda acc, g: acc * g, grid, 1)
        @pl.when(step == 0)
        def _():
            for cm, c_ins, c_outs, ss, rs in parts:
                cm.first(c_ins, c_outs, ss, rs)

        def mid_step(cm):
            ms = int(cm.mid_at * n_steps)
            return ms if 0 < ms < n_steps - 1 else None

        for cm, c_ins, c_outs, ss, rs in parts:
            if cm.mid is not None and mid_step(cm) is not None:
                @pl.when(step == mid_step(cm))
                def _(cm=cm, c_ins=c_ins, c_outs=c_outs, ss=ss, rs=rs):
                    cm.mid(c_ins, c_outs, ss, rs)

        body(*ins, *outs, *scr)

        @pl.when(step == n_steps - 1)
        def _():
            for cm, c_ins, c_outs, ss, rs in parts:
                if cm.mid is not None and mid_step(cm) is None:
                    cm.mid(c_ins, c_outs, ss, rs)
                cm.last(c_ins, c_outs, ss, rs)

    any_spec = pl.BlockSpec(memory_space=pl.ANY)
    i_at, o_at, sem_shapes = n_in, n_out, []
    for cm in comms:
        for i_in, i_out in cm.aliases.items():
            aliases[i_at + i_in] = o_at + i_out
        i_at += len(cm.ins)
        o_at += len(cm.outs)
        sem_shapes += [pltpu.SemaphoreType.DMA((cm.n_sems,)), pltpu.SemaphoreType.DMA((cm.n_sems,))]
    return pl.pallas_call(
        full_body, grid=grid,
        in_specs=list(in_specs) + [any_spec] * n_ci,
        out_specs=list(out_specs) + [any_spec] * n_co,
        out_shape=list(out_shape) + c_outs_all,
        scratch_shapes=list(scratch_shapes) + sem_shapes,
        input_output_aliases=aliases,
        compiler_params=pltpu.CompilerParams(dimension_semantics=("arbitrary",) * len(grid),
                                             vmem_limit_bytes=VMEM_LIMIT, has_side_effects=True),
        name=name)(*operands, *c_ins_all)


def _matmul(a, b, *, ta=False, tb=False, tm, tn, tk, out_dtypes, name, extras=(), epilogue=None,
            out_shapes=None, out_specs=None, b_spec=None, n_dim=None, into=None, side=None, comm=None):
    n_into = 0 if into is None else 1
    if ta:
        k_dim, m_dim = a.shape
    else:
        m_dim, k_dim = a.shape
    if n_dim is None:
        n_dim = b.shape[0] if tb else b.shape[1]
        assert (b.shape[1] if tb else b.shape[0]) == k_dim
    assert m_dim % tm == 0 and n_dim % tn == 0 and k_dim % tk == 0, (name, a.shape, b.shape)
    nk = k_dim // tk
    n_ex, n_out = len(extras), len(out_dtypes)
    dims = (((0 if ta else 1,), (1 if tb else 0,)), ((), ()))
    grid = (m_dim // tm, n_dim // tn, nk)
    n_steps = grid[0] * grid[1] * grid[2]
    s_ins, s_outs, s_scr = (len(side.ins), len(side.outs), len(side.scratch)) if side else (0, 0, 0)

    def body(*refs):
        a_ref, b_ref = refs[0], refs[1]
        ex_refs = refs[2:2 + n_ex]
        i0 = 2 + n_ex + n_into
        side_in = refs[i0:i0 + s_ins]
        o_refs = refs[i0 + s_ins:i0 + s_ins + n_out]
        side_out = refs[i0 + s_ins + n_out:i0 + s_ins + n_out + s_outs]
        side_scr = refs[len(refs) - s_scr:] if s_scr else ()
        if side is not None:
            step = (pl.program_id(0) * grid[1] + pl.program_id(1)) * grid[2] + pl.program_id(2)

            @pl.when(step == 0)
            def _():
                side.init(side_scr)

        def dot():
            if side is not None:
                side.body(step, side_in, side_out, side_scr)
            return lax.dot_general(a_ref[...].astype(BF16), b_ref[...].astype(BF16), dims,
                                   preferred_element_type=F32)

        def finish(acc):
            outs = epilogue(acc, *[e[...] for e in ex_refs]) if epilogue is not None else (acc,)
            for o_ref, o in zip(o_refs, outs):
                o_ref[...] = o.astype(o_ref.dtype)

        if nk == 1:
            finish(dot())
        else:
            acc_ref = refs[len(refs) - s_scr - 1]
            k = pl.program_id(2)

            @pl.when(k == 0)
            def _():
                acc_ref[...] = dot()

            @pl.when(jnp.logical_and(k > 0, k < nk - 1))
            def _():
                acc_ref[...] += dot()

            @pl.when(k == nk - 1)
            def _():
                finish(acc_ref[...] + dot())

    a_spec = (pl.BlockSpec((tk, tm), lambda i, j, k: (k, i)) if ta
              else pl.BlockSpec((tm, tk), lambda i, j, k: (i, k)))
    if b_spec is None:
        b_spec = (pl.BlockSpec((tn, tk), lambda i, j, k: (j, k)) if tb
                  else pl.BlockSpec((tk, tn), lambda i, j, k: (k, j)))
    io_spec = pl.BlockSpec((tm, tn), lambda i, j, k: (i, j))
    if out_shapes is None:
        out_shapes = [jax.ShapeDtypeStruct((m_dim, n_dim), dt) for dt in out_dtypes]
    if out_specs is None:
        out_specs = [io_spec] * n_out
    return _call(
        body,
        grid=grid,
        in_specs=([a_spec, b_spec] + [io_spec] * n_ex + [pl.BlockSpec(memory_space=pl.ANY)] * n_into
                  + (list(side.in_specs) if side else [])),
        out_specs=list(out_specs) + (list(side.out_specs) if side else []),
        out_shape=list(out_shapes) + (list(side.outs) if side else []),
        scratch_shapes=([pltpu.VMEM((tm, tn), F32)] if nk > 1 else []) + (list(side.scratch) if side else []),
        sem=("parallel", "parallel", "arbitrary") if side is None else None,
        aliases={2 + n_ex: 0} if n_into else None,
        operands=(a, b, *extras) + ((into,) if n_into else ()) + (tuple(side.ins) if side else ()),
        name=name, comm=comm)


def _add_epilogue(acc, r):
    return (acc + r,)


def _rms_fwd(x, g, *, name, tm=512, comm=None):
    s_len, d = x.shape

    def body(x_ref, g_ref, o_ref):
        xv = x_ref[...]
        r = lax.rsqrt(jnp.mean(xv * xv, axis=-1, keepdims=True) + EPS)
        o_ref[...] = (xv * r * g_ref[...]).astype(o_ref.dtype)

    return _call(
        body, grid=(s_len // tm,),
        in_specs=[pl.BlockSpec((tm, d), lambda i: (i, 0)), pl.BlockSpec((1, d), lambda i: (0, 0))],
        out_specs=[pl.BlockSpec((tm, d), lambda i: (i, 0))],
        out_shape=[jax.ShapeDtypeStruct((s_len, d), BF16)],
        sem=("parallel",), operands=(x, g), name=name, comm=comm)


def _zero_refs(refs):
    for r in refs:
        r[...] = jnp.zeros_like(r)


def _rms_bwd_block(dn_ref, x_ref, g_ref, res_ref, dx_ref, dg_ref, acc_ref):
    xv = x_ref[...]
    dnv = dn_ref[...].astype(F32)
    r = lax.rsqrt(jnp.mean(xv * xv, axis=-1, keepdims=True) + EPS)
    xh = xv * r
    acc = acc_ref[...] + _rowsum8(dnv * xh)
    acc_ref[...] = acc
    dg_ref[...] = jnp.sum(acc, axis=0, keepdims=True)
    dxh = dnv * g_ref[...]
    dx_ref[...] = (r * (dxh - xh * jnp.mean(dxh * xh, axis=-1, keepdims=True))
                   + res_ref[...].astype(F32))


class _Side:
    def __init__(self, ins, in_specs, outs, out_specs, scratch, init, body):
        self.ins, self.in_specs, self.outs, self.out_specs = ins, in_specs, outs, out_specs
        self.scratch = scratch
        self.init = init
        self.body = body


def _rms_bwd_side(dn, x, g, res, *, block_of_step, n_steps):
    s_len, d = x.shape
    row = pl.BlockSpec((s_len // n_steps, d), lambda i, j, k: (block_of_step(i, j, k), 0))
    vec = pl.BlockSpec((1, d), lambda i, j, k: (0, 0))
    return _Side(
        ins=[dn, x, g, res], in_specs=[row, row, vec, row],
        outs=[jax.ShapeDtypeStruct((s_len, d), F32), jax.ShapeDtypeStruct((1, d), F32)],
        out_specs=[row, vec], scratch=[pltpu.VMEM((SUBLANES, d), F32)],
        init=_zero_refs, body=lambda step, ins, outs, scr: _rms_bwd_block(*ins, *outs, *scr))


def _rms_bwd(dn, x, g, res, *, name, tm=512, comm=None):
    s_len, d = x.shape
    n = s_len // tm

    def body(*refs):
        @pl.when(pl.program_id(0) == 0)
        def _():
            _zero_refs(refs[-1:])

        _rms_bwd_block(*refs)

    row = pl.BlockSpec((tm, d), lambda i: (i, 0))
    vec = pl.BlockSpec((1, d), lambda i: (0, 0))
    return _call(
        body, grid=(n,),
        in_specs=[row, row, vec, row],
        out_specs=[row, vec],
        out_shape=[jax.ShapeDtypeStruct((s_len, d), F32), jax.ShapeDtypeStruct((1, d), F32)],
        scratch_shapes=[pltpu.VMEM((SUBLANES, d), F32)],
        operands=(dn, x, g, res), name=name, comm=comm)


def _loss_head(x3, g, target, *, tm=512):
    s_len, d = x3.shape
    n = s_len // tm

    def body(x_ref, g_ref, t_ref, dxb_ref, dg_ref, loss_ref, accg_ref, accl_ref):
        i = pl.program_id(0)
        xv = x_ref[...]
        gv = g_ref[...]
        r = lax.rsqrt(jnp.mean(xv * xv, axis=-1, keepdims=True) + EPS)
        xh = xv * r
        err = xh * gv - t_ref[...]

        @pl.when(i == 0)
        def _():
            accg_ref[...] = jnp.zeros_like(accg_ref)
            accl_ref[...] = jnp.zeros_like(accl_ref)

        accl_ref[...] += _rowsum8(err * err)
        dn = err * (1.0 / d)
        accg_ref[...] += _rowsum8(dn * xh)
        dxh = dn * gv
        dx = r * (dxh - xh * jnp.mean(dxh * xh, axis=-1, keepdims=True))
        dxb_ref[...] = dx.astype(BF16)

        @pl.when(i == n - 1)
        def _():
            dg_ref[...] = jnp.sum(accg_ref[...], axis=0, keepdims=True)
            tot = jnp.sum(jnp.sum(accl_ref[...], axis=0, keepdims=True), axis=1, keepdims=True)
            loss_ref[...] = jnp.broadcast_to(tot * (0.5 / d), (1, LANES))

    row = pl.BlockSpec((tm, d), lambda i: (i, 0))
    vec = pl.BlockSpec((1, d), lambda i: (0, 0))
    return pl.pallas_call(
        body, grid=(n,),
        in_specs=[row, vec, row],
        out_specs=[row, vec, pl.BlockSpec((1, LANES), lambda i: (0, 0))],
        out_shape=[jax.ShapeDtypeStruct((s_len, d), BF16),
                   jax.ShapeDtypeStruct((1, d), F32), jax.ShapeDtypeStruct((1, LANES), F32)],
        scratch_shapes=[pltpu.VMEM((SUBLANES, d), F32), pltpu.VMEM((SUBLANES, d), F32)],
        compiler_params=_params("arbitrary"), name="loss_head",
    )(x3, g, target)


def _shift_down(v, k, rows_before, row):
    out = pltpu.roll(v, k, axis=0)
    for j in range(k):
        out = jnp.where(row == j, rows_before[j], out)
    return out


def _shift_up(v, k, rows_after, row):
    t = v.shape[0]
    out = pltpu.roll(v, t - k, axis=0)
    for j in range(k):
        out = jnp.where(row == t - k + j, rows_after[j], out)
    return out


def _conv_fwd(z, w_t, gain, *, ts=512):
    s_len = z.shape[0]
    n_grp = D_CONV // GROUP

    def body(cb_ref, cc_ref, ch_ref, w_ref, g_ref, y_ref, carry_ref):
        i = pl.program_id(0)

        @pl.when(i == 0)
        def _():
            carry_ref[...] = jnp.zeros_like(carry_ref)

        row = lax.broadcasted_iota(jnp.int32, (ts, GROUP), 0)
        for g in range(n_grp):
            sl = slice(g * GROUP, (g + 1) * GROUP)
            uu = cc_ref[:, sl] * ch_ref[:, sl]
            p2 = carry_ref[6:7, sl]
            p1 = carry_ref[7:8, sl]
            u1 = _shift_down(uu, 1, [p1], row)
            u2 = _shift_down(uu, 2, [p2, p1], row)
            conv = w_ref[0:1, sl] * u2 + w_ref[1:2, sl] * u1 + w_ref[2:3, sl] * uu
            y = cb_ref[:, sl] * conv
            carry_ref[:, sl] = uu[ts - SUBLANES:ts, :]
            rg = lax.rsqrt(jnp.mean(y * y, axis=-1, keepdims=True) + EPS)
            y_ref[:, sl] = (y * rg * g_ref[:, sl]).astype(BF16)

    def col(j):
        return pl.BlockSpec((ts, D_CONV), lambda i, j=j: (i, j))

    small = lambda r: pl.BlockSpec((r, D_CONV), lambda i: (0, 0))
    return pl.pallas_call(
        body, grid=(s_len // ts,),
        in_specs=[col(0), col(1), col(2), small(3), small(1)],
        out_specs=col(0),
        out_shape=jax.ShapeDtypeStruct((s_len, 2 * D_CONV), BF16),
        scratch_shapes=[pltpu.VMEM((SUBLANES, D_CONV), F32)],
        compiler_params=_params("arbitrary"), name="conv_fwd",
    )(z, z, z, w_t, gain)


def _conv_bwd_side(dy, z, w_t, gain, *, n_steps, step_of):
    s_len = z.shape[0]
    n = n_steps
    ts = s_len // n
    n_grp = D_CONV // GROUP
    halo_blocks = ts // SUBLANES

    def body(step, ins, outs, scr):
        dy_ref, cb_ref, cc_ref, ch_ref, hcc_ref, hch_ref, w_ref, g_ref = ins
        dz_ref, dw_ref, dg_ref = outs
        carry_ref, accw_ref, accg_ref = scr
        first_tile = (n - 1 - step) == 0
        row = lax.broadcasted_iota(jnp.int32, (ts, GROUP), 0)
        keep = jnp.where(first_tile, 0.0, 1.0)
        for g in range(n_grp):
            sl = slice(g * GROUP, (g + 1) * GROUP)
            cc = cc_ref[:, sl]
            ch = ch_ref[:, sl]
            cb = cb_ref[:, sl]
            uu = cc * ch
            p2 = hcc_ref[6:7, sl] * hch_ref[6:7, sl] * keep
            p1 = hcc_ref[7:8, sl] * hch_ref[7:8, sl] * keep
            u1 = _shift_down(uu, 1, [p1], row)
            u2 = _shift_down(uu, 2, [p2, p1], row)
            w0, w1, w2 = w_ref[0:1, sl], w_ref[1:2, sl], w_ref[2:3, sl]
            conv = w0 * u2 + w1 * u1 + w2 * uu
            y = cb * conv
            rg = lax.rsqrt(jnp.mean(y * y, axis=-1, keepdims=True) + EPS)
            yh = y * rg
            dyv = dy_ref[:, sl].astype(F32)
            accg_ref[:, sl] += _rowsum8(dyv * yh)
            dyn = dyv * g_ref[:, sl]
            dpre = rg * (dyn - yh * jnp.mean(dyn * yh, axis=-1, keepdims=True))
            dz_ref[:, sl] = (dpre * conv).astype(BF16)
            dconv = dpre * cb
            accw_ref[0:8, sl] += _rowsum8(dconv * u2)
            accw_ref[8:16, sl] += _rowsum8(dconv * u1)
            accw_ref[16:24, sl] += _rowsum8(dconv * uu)
            n0 = carry_ref[0:1, sl]
            n1 = carry_ref[1:2, sl]
            d1 = _shift_up(dconv, 1, [n0], row)
            d2 = _shift_up(dconv, 2, [n0, n1], row)
            duu = w2 * dconv + w1 * d1 + w0 * d2
            carry_ref[:, sl] = dconv[0:SUBLANES, :]
            dz_ref[:, D_CONV + g * GROUP:D_CONV + (g + 1) * GROUP] = (duu * ch).astype(BF16)
            dz_ref[:, 2 * D_CONV + g * GROUP:2 * D_CONV + (g + 1) * GROUP] = (duu * cc).astype(BF16)

        for k in range(3):
            dw_ref[k:k + 1, :] = jnp.sum(accw_ref[8 * k:8 * k + 8, :], axis=0, keepdims=True)
        dg_ref[...] = jnp.sum(accg_ref[...], axis=0, keepdims=True)

    def tile(i, j, k):
        return n - 1 - step_of(i, j, k)

    def col(c):
        return pl.BlockSpec((ts, D_CONV), lambda i, j, k, c=c: (tile(i, j, k), c))

    def halo(c):
        return pl.BlockSpec((SUBLANES, D_CONV),
                            lambda i, j, k, c=c: (jnp.maximum(tile(i, j, k) * halo_blocks - 1, 0), c))

    small = lambda r: pl.BlockSpec((r, D_CONV), lambda i, j, k: (0, 0))
    return _Side(
        ins=[dy, z, z, z, z, z, w_t, gain],
        in_specs=[col(0), col(0), col(1), col(2), halo(1), halo(2), small(3), small(1)],
        outs=[jax.ShapeDtypeStruct((s_len, 6 * D_CONV), BF16),
              jax.ShapeDtypeStruct((3, D_CONV), F32), jax.ShapeDtypeStruct((1, D_CONV), F32)],
        out_specs=[pl.BlockSpec((ts, 3 * D_CONV), lambda i, j, k: (tile(i, j, k), 0)), small(3), small(1)],
        scratch=[pltpu.VMEM((SUBLANES, D_CONV), F32), pltpu.VMEM((24, D_CONV), F32),
                 pltpu.VMEM((SUBLANES, D_CONV), F32)],
        init=_zero_refs, body=body)


def _split3(v):
    hi = v.astype(BF16)
    r1 = v - hi.astype(F32)
    mid = r1.astype(BF16)
    lo = (r1 - mid.astype(F32)).astype(BF16)
    return jnp.concatenate([hi, mid, lo], axis=1)


def _tri_sum(tri, v):
    w = v.shape[1]
    dd = jnp.dot(tri, _split3(v), preferred_element_type=F32)
    return dd[:, :w] + dd[:, w:2 * w] + dd[:, 2 * w:]


def _chunk_masks(ts):
    r = jnp.arange(ts)
    same = (r[:, None] // CHUNK) == (r[None, :] // CHUNK)
    later = jnp.logical_and(same, r[None, :] > r[:, None]).astype(BF16)
    earlier = jnp.logical_and(same, r[None, :] < r[:, None]).astype(BF16)
    chunk_of_row = jnp.arange(ts // CHUNK * SUBLANES)[:, None] // SUBLANES
    member = (chunk_of_row == (r[None, :] // CHUNK)).astype(BF16)
    return later, earlier, member


def _sigmoid(v):
    return 1.0 / (1.0 + jnp.exp(-v))


def _gla_fwd(z, alow, wgu, bg, gg, y_in, *, ts=512, comm=None):
    s_len = z.shape[0]
    nch = ts // CHUNK
    scale = DK ** -0.5

    tri_u, _, ind8 = _chunk_masks(ts)

    def body(q_ref, k_ref, v_ref, og_ref, al_ref, wgu_ref, bg_ref, gg_ref, tu_ref, ind_ref, yin_ref,
             y_ref, o_ref, la_ref, st_ref, state_ref, kd_ref, qs_ref, dec_ref):
        del yin_ref
        i = pl.program_id(0)

        @pl.when(i == 0)
        def _():
            state_ref[...] = jnp.zeros_like(state_ref)

        pre = jnp.dot(al_ref[...].astype(BF16), wgu_ref[...], preferred_element_type=F32) + bg_ref[...]
        la = (jnp.minimum(pre, 0.0) - jnp.log(1.0 + jnp.exp(-jnp.abs(pre)))) * (1.0 / 16.0)
        la_ref[...] = la
        kd_ref[...] = (k_ref[...] * jnp.exp(_tri_sum(tu_ref[...], la))).astype(BF16)
        qs_ref[...] = (q_ref[...] * scale).astype(BF16)
        dec_ref[...] = jnp.exp(_tri_sum(ind_ref[...], la))

        def chunk(cl, carry):
            rows = pl.ds(pl.multiple_of(cl * CHUNK, CHUNK), CHUNK)
            dec = dec_ref[pl.ds(pl.multiple_of(cl * SUBLANES, SUBLANES), 1), :]
            for h in range(HEADS):
                ks = slice(h * DK, (h + 1) * DK)
                vs = slice(h * DV, (h + 1) * DV)
                kv_t = lax.dot_general(v_ref[rows, vs].astype(BF16), kd_ref[rows, ks],
                                       (((0,), (0,)), ((), ())), preferred_element_type=F32)
                st = state_ref[h] * dec[:, ks] + kv_t
                state_ref[h] = st
                st_ref[cl, h] = st
                o_ref[rows, vs] = lax.dot_general(qs_ref[rows, ks], st.astype(BF16),
                                                  (((1,), (1,)), ((), ())), preferred_element_type=F32)
            return carry

        lax.fori_loop(0, nch, chunk, 0, unroll=2)

        ggv = gg_ref[...]
        for h in range(HEADS):
            vs = slice(h * DV, (h + 1) * DV)
            o_h = o_ref[:, vs]
            og_h = og_ref[:, vs]
            ro = lax.rsqrt(jnp.mean(o_h * o_h, axis=-1, keepdims=True) + EPS)
            y_ref[:, vs] = (o_h * ro * ggv * (og_h * _sigmoid(og_h))).astype(BF16)

    def zcol(width, j):
        return pl.BlockSpec((ts, width), lambda i, j=j: (i, j))

    full = lambda shape: pl.BlockSpec(shape, lambda i: tuple(0 for _ in shape))
    return _call(
        body, grid=(s_len // ts,),
        in_specs=[zcol(512, 6), zcol(512, 7), zcol(1024, 4), zcol(1024, 5), zcol(LANES, 0),
                  full((LANES, 512)), full((1, 512)), full((1, DV)), full(tri_u.shape), full(ind8.shape),
                  pl.BlockSpec(memory_space=pl.ANY)],
        out_specs=[zcol(1024, 1), zcol(1024, 0), zcol(512, 0),
                   pl.BlockSpec((nch, HEADS, DV, DK), lambda i: (i, 0, 0, 0))],
        out_shape=[jax.ShapeDtypeStruct((s_len, 2048), BF16), jax.ShapeDtypeStruct((s_len, 1024), F32),
                   jax.ShapeDtypeStruct((s_len, 512), F32),
                   jax.ShapeDtypeStruct((s_len // CHUNK, HEADS, DV, DK), F32)],
        scratch_shapes=[pltpu.VMEM((HEADS, DV, DK), F32), pltpu.VMEM((ts, 512), BF16),
                        pltpu.VMEM((ts, 512), BF16), pltpu.VMEM((nch * SUBLANES, 512), F32)],
        aliases={10: 0},
        operands=(z, z, z, z, alow, wgu, bg, gg, tri_u, ind8, y_in), name="gla_fwd", comm=comm)


def _gla_bwd(dy, z, o, la, st, alow, wgu, gg, dz_in, *, ts=512, comm=None):
    s_len = z.shape[0]
    n = s_len // ts
    nch = ts // CHUNK
    scale = DK ** -0.5

    tri_u, tri_l, ind8 = _chunk_masks(ts)

    def body(dy_ref, q_ref, k_ref, v_ref, og_ref, o_ref, la_ref, st_ref, stp_ref, al_ref, wgu_ref,
             gg_ref, tu_ref, tl_ref, ind_ref, dzin_ref, dz_ref, dal_ref, dwgu_ref, dbg_ref, dgg_ref,
             gt_ref, decn_ref, accw_ref, accb_ref, accg_ref, dla_ref,
             e_ref, kd_ref, kdb_ref, qs_ref, do_ref, dkd_ref, dec_ref, dbe_ref):
        del dzin_ref
        i = pl.program_id(0)
        first_tile = (n - 1 - i) == 0

        @pl.when(i == 0)
        def _():
            gt_ref[...] = jnp.zeros_like(gt_ref)
            decn_ref[...] = jnp.ones_like(decn_ref)
            accw_ref[...] = jnp.zeros_like(accw_ref)
            accb_ref[...] = jnp.zeros_like(accb_ref)
            accg_ref[...] = jnp.zeros_like(accg_ref)

        la = la_ref[...]
        e_dec = jnp.exp(_tri_sum(tu_ref[...], la))
        e_ref[...] = e_dec
        kd = k_ref[...] * e_dec
        kd_ref[...] = kd
        kdb_ref[...] = kd.astype(BF16)
        qs_ref[...] = (q_ref[...] * scale).astype(BF16)
        dec_ref[...] = jnp.exp(_tri_sum(ind_ref[...], la))
        ggv = gg_ref[...]
        for h in range(HEADS):
            vs = slice(h * DV, (h + 1) * DV)
            o_h = o_ref[:, vs]
            og_h = og_ref[:, vs]
            dy_h = dy_ref[:, vs].astype(F32)
            ro = lax.rsqrt(jnp.mean(o_h * o_h, axis=-1, keepdims=True) + EPS)
            oh = o_h * ro
            sig = _sigmoid(og_h)
            sil = og_h * sig
            accg_ref[...] += _rowsum8(dy_h * oh * sil)
            dz_ref[:, 2048 + h * DV:2048 + (h + 1) * DV] = (
                dy_h * oh * ggv * sig * (1.0 + og_h * (1.0 - sig))).astype(BF16)
            don = dy_h * ggv * sil
            do_ref[:, vs] = (ro * (don - oh * jnp.mean(don * oh, axis=-1, keepdims=True))).astype(BF16)
        keep = jnp.where(first_tile, 0.0, 1.0)

        def chunk(jrev, decn):
            cl = nch - 1 - jrev
            rows = pl.ds(pl.multiple_of(cl * CHUNK, CHUNK), CHUNK)
            one_row = pl.ds(pl.multiple_of(cl * SUBLANES, SUBLANES), 1)
            dec = dec_ref[one_row, :]
            has_prev = jnp.where(cl > 0, 1.0, 0.0)
            prev_idx = jnp.maximum(cl - 1, 0)
            for h in range(HEADS):
                ks = slice(h * DK, (h + 1) * DK)
                vs = slice(h * DV, (h + 1) * DV)
                dob = do_ref[rows, vs]
                s_c = st_ref[cl, h]
                dqs = jnp.dot(dob, s_c.astype(BF16), preferred_element_type=F32)
                dz_ref[rows, ks] = (dqs * scale).astype(BF16)
                gt = gt_ref[h] * decn[:, ks] + lax.dot_general(
                    dob, qs_ref[rows, ks], (((0,), (0,)), ((), ())), preferred_element_type=F32)
                gt_ref[h] = gt
                gb = gt.astype(BF16)
                dkd_ref[rows, ks] = jnp.dot(v_ref[rows, vs].astype(BF16), gb, preferred_element_type=F32)
                dz_ref[rows, 1024 + h * DV:1024 + (h + 1) * DV] = lax.dot_general(
                    kdb_ref[rows, ks], gb, (((1,), (1,)), ((), ())),
                    preferred_element_type=F32).astype(BF16)
                s_prev = has_prev * st_ref[prev_idx, h] + (1.0 - has_prev) * keep * stp_ref[0, h]
                dbe_ref[one_row, ks] = jnp.sum(gt * s_prev, axis=0, keepdims=True) * dec[:, ks]
            return dec

        decn_ref[0:1, :] = lax.fori_loop(0, nch, chunk, decn_ref[0:1, :], unroll=2)

        dkd = dkd_ref[...]
        dz_ref[:, 512:1024] = (dkd * e_ref[...]).astype(BF16)
        dla_ref[...] = _tri_sum(tl_ref[...], dkd * kd_ref[...])
        for c in range(nch):
            dla_ref[c * CHUNK:(c + 1) * CHUNK, :] += dbe_ref[c * SUBLANES:c * SUBLANES + 1, :]
        dpre = dla_ref[...] * (1.0 / 16.0) * (1.0 - jnp.exp(16.0 * la))
        accb_ref[...] += _rowsum8(dpre)
        dpb = dpre.astype(BF16)
        accw_ref[...] += lax.dot_general(al_ref[...].astype(BF16), dpb, (((0,), (0,)), ((), ())),
                                         preferred_element_type=F32)
        dal_ref[...] = lax.dot_general(dpb, wgu_ref[...], (((1,), (1,)), ((), ())),
                                       preferred_element_type=F32)

        @pl.when(i == n - 1)
        def _():
            dwgu_ref[...] = accw_ref[...]
            dbg_ref[...] = jnp.sum(accb_ref[...], axis=0, keepdims=True)
            dgg_ref[...] = jnp.sum(accg_ref[...], axis=0, keepdims=True)

    def zcol(width, j):
        return pl.BlockSpec((ts, width), lambda i, j=j: (n - 1 - i, j))

    full = lambda shape: pl.BlockSpec(shape, lambda i: tuple(0 for _ in shape))
    return _call(
        body, grid=(n,),
        in_specs=[zcol(1024, 1), zcol(512, 6), zcol(512, 7), zcol(1024, 4), zcol(1024, 5),
                  zcol(1024, 0), zcol(512, 0),
                  pl.BlockSpec((nch, HEADS, DV, DK), lambda i: (n - 1 - i, 0, 0, 0)),
                  pl.BlockSpec((1, HEADS, DV, DK),
                               lambda i: (jnp.maximum((n - 1 - i) * nch - 1, 0), 0, 0, 0)),
                  zcol(LANES, 0), full((LANES, 512)), full((1, DV)),
                  full(tri_u.shape), full(tri_l.shape), full(ind8.shape),
                  pl.BlockSpec(memory_space=pl.ANY)],
        out_specs=[zcol(3072, 1), zcol(LANES, 0), full((LANES, 512)), full((1, 512)), full((1, DV))],
        out_shape=[jax.ShapeDtypeStruct((s_len, 6144), BF16), jax.ShapeDtypeStruct((s_len, LANES), F32),
                   jax.ShapeDtypeStruct((LANES, 512), F32), jax.ShapeDtypeStruct((1, 512), F32),
                   jax.ShapeDtypeStruct((1, DV), F32)],
        scratch_shapes=[pltpu.VMEM((HEADS, DV, DK), F32), pltpu.VMEM((SUBLANES, 512), F32),
                        pltpu.VMEM((LANES, 512), F32), pltpu.VMEM((SUBLANES, 512), F32),
                        pltpu.VMEM((SUBLANES, DV), F32), pltpu.VMEM((ts, 512), F32),
                        pltpu.VMEM((ts, 512), F32), pltpu.VMEM((ts, 512), F32), pltpu.VMEM((ts, 512), BF16),
                        pltpu.VMEM((ts, 512), BF16), pltpu.VMEM((ts, 1024), BF16),
                        pltpu.VMEM((ts, 512), F32), pltpu.VMEM((nch * SUBLANES, 512), F32),
                        pltpu.VMEM((nch * SUBLANES, 512), F32)],
        aliases={15: 0},
        operands=(dy, z, z, z, z, o, la, st, st, alow, wgu, gg, tri_u, tri_l, ind8, dz_in),
        name="gla_bwd", comm=comm)


def _adamw(w, g, m, v, *, name):
    rows, cols = w.shape
    tr, tc = _tile(rows, cols)

    def body(w_ref, g_ref, m_ref, v_ref, go_ref, d_ref, nm_ref, nv_ref):
        gv = g_ref[...]
        go_ref[...] = gv
        m2 = ADAM_B1 * m_ref[...] + (1.0 - ADAM_B1) * gv
        v2 = ADAM_B2 * v_ref[...] + (1.0 - ADAM_B2) * jnp.square(gv)
        m_hat = m2 / (1.0 - ADAM_B1 ** ADAM_STEP)
        v_hat = v2 / (1.0 - ADAM_B2 ** ADAM_STEP)
        d_ref[...] = -ADAM_LR * (m_hat / (jnp.sqrt(v_hat) + ADAM_EPS) + ADAM_WD * w_ref[...])
        nm_ref[...] = m2
        nv_ref[...] = v2

    blk = pl.BlockSpec((tr, tc), lambda i, j: (i, j))
    shp = jax.ShapeDtypeStruct((rows, cols), F32)
    return pl.pallas_call(
        body, grid=(rows // tr, cols // tc), in_specs=[blk] * 4, out_specs=[blk] * 4, out_shape=[shp] * 4,
        compiler_params=_params("parallel", "parallel"), name=name,
    )(w, g, m, v)


def _my_place():
    return lax.axis_index("x"), lax.axis_index("y"), lax.axis_index("c")


def _flip(v, bit):
    return 1 - v if bit else v


def _allgather_small(buf, *, reduce, name):
    rows = buf.shape[0]

    def body(in_ref, out_ref, gat_ref, send_sems, recv_sems):
        x, y, c = _my_place()
        me = 4 * x + 2 * y + c
        gat_ref[me] = in_ref[...]
        copies = []
        for m in range(1, N_DEV):
            peer = (_flip(x, m & 4), _flip(y, m & 2), _flip(c, m & 1))
            cp = pltpu.make_async_remote_copy(
                src_ref=in_ref, dst_ref=gat_ref.at[me],
                send_sem=send_sems.at[m - 1], recv_sem=recv_sems.at[m - 1],
                device_id=peer, device_id_type=MESH)
            cp.start()
            copies.append(cp)
        for m in range(1, N_DEV):
            px, py, pc = _flip(x, m & 4), _flip(y, m & 2), _flip(c, m & 1)
            src_slot = gat_ref.at[4 * px + 2 * py + pc]
            pltpu.make_async_remote_copy(
                src_ref=src_slot, dst_ref=src_slot,
                send_sem=send_sems.at[m - 1], recv_sem=recv_sems.at[m - 1],
                device_id=(px, py, pc), device_id_type=MESH).wait_recv()
        for cp in copies:
            cp.wait_send()
        if reduce:
            tot = gat_ref[0]
            for d in range(1, N_DEV):
                tot = tot + gat_ref[d]
            out_ref[...] = tot
        else:
            out_ref[...] = gat_ref[...]

    out_shape = (rows, LANES) if reduce else (N_DEV, rows, LANES)
    return pl.pallas_call(
        body,
        in_specs=[pl.BlockSpec(memory_space=pltpu.VMEM)],
        out_specs=pl.BlockSpec(memory_space=pltpu.VMEM),
        out_shape=jax.ShapeDtypeStruct(out_shape, F32),
        scratch_shapes=[pltpu.VMEM((N_DEV, rows, LANES), F32),
                        pltpu.SemaphoreType.DMA((N_DEV - 1,)), pltpu.SemaphoreType.DMA((N_DEV - 1,))],
        compiler_params=pltpu.CompilerParams(has_side_effects=True),
        name=name,
    )(buf)


def _cast_into(shard, chip_core, *, name):
    rows, cols = shard.shape
    tr, tc = _tile(rows, cols)

    def body(cc_ref, s_ref, o_ref):
        del cc_ref
        o_ref[...] = s_ref[...].astype(BF16)

    grid_spec = pltpu.PrefetchScalarGridSpec(
        num_scalar_prefetch=1, grid=(rows // tr, cols // tc),
        in_specs=[pl.BlockSpec((tr, tc), lambda r, q, cc: (r, q))],
        out_specs=pl.BlockSpec((None, tr, tc), lambda r, q, cc: (cc[0], r, q)))
    return pl.pallas_call(
        body, grid_spec=grid_spec, out_shape=jax.ShapeDtypeStruct((N_CHIPS, rows, cols), BF16),
        compiler_params=_params("arbitrary", "arbitrary"), name=name,
    )(chip_core, shard)


def _remote(src, dst, send_sems, recv_sems, k, device):
    return pltpu.make_async_remote_copy(src_ref=src, dst_ref=dst, send_sem=send_sems.at[k],
                                        recv_sem=recv_sems.at[k], device_id=device, device_id_type=MESH)


def _col_half(ref, h, *lead, rows=None):
    hc = ref.shape[-1] // 2
    mid = (slice(None),) * (len(ref.shape) - 2 - len(lead))
    row_sel = slice(None) if rows is None else pl.ds(rows[0], rows[1])
    return ref.at[tuple(lead) + mid + (row_sel, pl.ds(h * hc, hc))]


def _gather_comm(bufs, rows=None, mid_at=0.75):
    n_w, n_m = len(bufs), len(CHIP_MASKS)
    rows = rows or [None] * n_w

    def first(c_ins, c_outs, ss, rs):
        x, y, c = _my_place()
        chip = 2 * x + y
        for w in range(n_w):
            mine = _col_half(c_outs[w], c, chip, rows=rows[w])
            for mi, (mx, my) in enumerate(CHIP_MASKS):
                _remote(mine, mine, ss, rs, w * n_m + mi, (_flip(x, mx), _flip(y, my), c)).start()

    def mid(c_ins, c_outs, ss, rs):
        x, y, c = _my_place()
        for w in range(n_w):
            for mi, (mx, my) in enumerate(CHIP_MASKS):
                k = w * n_m + mi
                px, py = _flip(x, mx), _flip(y, my)
                landed = _col_half(c_outs[w], c, 2 * px + py, rows=rows[w])
                _remote(landed, landed, ss, rs, k, (px, py, c)).wait_recv()
                _remote(landed, landed, ss, rs, n_w * n_m + k, (x, y, 1 - c)).start()

    def last(c_ins, c_outs, ss, rs):
        x, y, c = _my_place()
        chip = 2 * x + y
        for w in range(n_w):
            mine = _col_half(c_outs[w], c, chip, rows=rows[w])
            for mi, (mx, my) in enumerate(CHIP_MASKS):
                k = w * n_m + mi
                px, py = _flip(x, mx), _flip(y, my)
                theirs = _col_half(c_outs[w], 1 - c, 2 * px + py, rows=rows[w])
                _remote(theirs, theirs, ss, rs, n_w * n_m + k, (x, y, 1 - c)).wait_recv()
                _remote(mine, mine, ss, rs, k, (px, py, c)).wait_send()
                _remote(mine, mine, ss, rs, n_w * n_m + k, (x, y, 1 - c)).wait_send()

    return _Comm(ins=bufs, outs=[jax.ShapeDtypeStruct(b.shape, b.dtype) for b in bufs],
                 aliases={w: w for w in range(n_w)}, n_sems=2 * n_w * n_m, first=first, mid=mid, last=last,
                 mid_at=mid_at)


def _swap_comm(grads):
    n_w = len(grads)

    def copy(c_ins, c_outs, ss, rs, w):
        x, y, c = _my_place()
        return _remote(_col_half(c_ins[w], 1 - c), c_outs[w], ss, rs, w, (x, y, 1 - c))

    def first(c_ins, c_outs, ss, rs):
        for w in range(n_w):
            copy(c_ins, c_outs, ss, rs, w).start()

    def last(c_ins, c_outs, ss, rs):
        for w in range(n_w):
            copy(c_ins, c_outs, ss, rs, w).wait()

    return _Comm(ins=grads,
                 outs=[jax.ShapeDtypeStruct(g.shape[:2] + (g.shape[2] // 2,), g.dtype) for g in grads],
                 aliases={}, n_sems=n_w, first=first, last=last)


def _add_own_half(g, other, chip_core, *, name):
    n_chip, rows, hc = other.shape
    tr, tc = _tile(rows, hc)
    per_half = hc // tc

    def body(cc_ref, g_ref, o_ref, out_ref):
        del cc_ref
        out_ref[...] = (g_ref[...].astype(F32) + o_ref[...].astype(F32)).astype(BF16)

    grid_spec = pltpu.PrefetchScalarGridSpec(
        num_scalar_prefetch=1, grid=(n_chip, rows // tr, per_half),
        in_specs=[pl.BlockSpec((None, tr, tc), lambda j, r, q, cc: (j, r, cc[1] * per_half + q)),
                  pl.BlockSpec((None, tr, tc), lambda j, r, q, cc: (j, r, q))],
        out_specs=pl.BlockSpec((None, tr, tc), lambda j, r, q, cc: (j, r, q)))
    return pl.pallas_call(
        body, grid_spec=grid_spec, out_shape=jax.ShapeDtypeStruct((n_chip, rows, hc), BF16),
        compiler_params=_params("parallel", "parallel", "parallel"), name=name,
    )(chip_core, g, other)


def _exchange_comm(pieces):
    n_w, n_m = len(pieces), len(CHIP_MASKS)

    def copies(c_ins, c_outs, ss, rs):
        x, y, c = _my_place()
        chip = 2 * x + y
        for w in range(n_w):
            for mi, (mx, my) in enumerate(CHIP_MASKS):
                px, py = _flip(x, mx), _flip(y, my)
                send = _remote(c_ins[w].at[2 * px + py], c_outs[w].at[chip], ss, rs, w * n_m + mi, (px, py, c))
                landed = c_outs[w].at[2 * px + py]
                yield send, _remote(landed, landed, ss, rs, w * n_m + mi, (px, py, c))

    def first(c_ins, c_outs, ss, rs):
        for send, _ in copies(c_ins, c_outs, ss, rs):
            send.start()

    def last(c_ins, c_outs, ss, rs):
        for send, arrival in copies(c_ins, c_outs, ss, rs):
            arrival.wait_recv()
            send.wait_send()

    return _Comm(ins=pieces, outs=[jax.ShapeDtypeStruct(p.shape, p.dtype) for p in pieces],
                 aliases={}, n_sems=n_w * n_m, first=first, last=last)


def _sum_chips(own, landed, chip_core, *, name):
    n_chip, rows, hc = own.shape
    tr, tc = _tile(rows, hc)
    per_half = hc // tc

    def body(cc_ref, o_ref, l1_ref, l2_ref, l3_ref, out_ref):
        del cc_ref
        out_ref[...] = ((o_ref[...].astype(F32) + l1_ref[...].astype(F32))
                        + l2_ref[...].astype(F32)) + l3_ref[...].astype(F32)

    def slot(k):
        return pl.BlockSpec((None, tr, tc), lambda r, q, cc, k=k: ((cc[0] + k) % n_chip, r, q))

    grid_spec = pltpu.PrefetchScalarGridSpec(
        num_scalar_prefetch=1, grid=(rows // tr, per_half),
        in_specs=[slot(0), slot(1), slot(2), slot(3)],
        out_specs=pl.BlockSpec((tr, tc), lambda r, q, cc: (r, cc[1] * per_half + q)))
    return pl.pallas_call(
        body, grid_spec=grid_spec, out_shape=jax.ShapeDtypeStruct((rows, 2 * hc), F32),
        compiler_params=_params("arbitrary", "arbitrary"), name=name,
    )(chip_core, own, landed, landed, landed)


def _join_comm(halves):
    n_w = len(halves)

    def first(c_ins, c_outs, ss, rs):
        x, y, c = _my_place()
        for w in range(n_w):
            mine = _col_half(c_outs[w], c)
            _remote(mine, mine, ss, rs, w, (x, y, 1 - c)).start()

    def last(c_ins, c_outs, ss, rs):
        x, y, c = _my_place()
        for w in range(n_w):
            theirs = _col_half(c_outs[w], 1 - c)
            _remote(theirs, theirs, ss, rs, w, (x, y, 1 - c)).wait()

    return _Comm(ins=halves, outs=[jax.ShapeDtypeStruct(h.shape, h.dtype) for h in halves],
                 aliases={w: w for w in range(n_w)}, n_sems=n_w, first=first, last=last)


def _standalone(comm, *, name):
    def body(o_ref):
        o_ref[...] = jnp.zeros_like(o_ref)

    return _call(body, grid=(1,), in_specs=[],
                 out_specs=[pl.BlockSpec((SUBLANES, LANES), lambda i: (0, 0))],
                 out_shape=[jax.ShapeDtypeStruct((SUBLANES, LANES), F32)], operands=(), name=name,
                 comm=comm)[1:]


def _pack(pieces):
    flat, spans, off = [], [], 0
    for p in pieces:
        v = p.reshape(-1).astype(F32)
        pad = (-v.shape[0]) % LANES
        if pad:
            v = jnp.concatenate([v, jnp.zeros((pad,), F32)])
        spans.append((off, p.size))
        off += v.shape[0]
        flat.append(v)
    tail = (-off) % (SUBLANES * LANES)
    if tail:
        flat.append(jnp.zeros((tail,), F32))
    return jnp.concatenate(flat).reshape(-1, LANES), spans


def _unpack(buf, span, shape):
    off, size = span
    return buf.reshape(-1)[off:off + size].reshape(shape)


def kernel(x, norm1_g, w_in, w_gate_up, b_gate, conv_w, conv_norm_g, gla_norm_g, w_out, norm2_g, w_ff1, w_ff2, norm_f_g, loss_target, m_norm1_g, m_w_in, m_w_gate_up, m_b_gate, m_conv_w, m_conv_norm_g, m_gla_norm_g, m_w_out, m_norm2_g, m_w_ff1, m_w_ff2, m_norm_f_g, v_norm1_g, v_w_in, v_w_gate_up, v_b_gate, v_conv_w, v_conv_norm_g, v_gla_norm_g, v_w_out, v_norm2_g, v_w_ff1, v_w_ff2, v_norm_f_g):
    xs = x[0]
    target = loss_target[0]
    s_len, d = xs.shape
    d_in = w_in.shape[2] * N_CHIPS
    d_main = d_in - GATE_RANK
    d_ff = w_ff1.shape[2] * N_CHIPS
    cx, cy, cc = _my_place()
    chip = 2 * cx + cy
    chip_core = jnp.stack([chip, cc]).astype(jnp.int32)
    n_ff = d_ff // N_CHIPS
    norm_f = norm_f_g.reshape(1, d)

    n_sh = d_in // N_CHIPS
    wi_buf = _cast_into(w_in[0].T, chip_core, name="cast_w_in")
    wo_buf = _cast_into(w_out[0], chip_core, name="cast_w_out")
    w1_buf = _cast_into(w_ff1[0], chip_core, name="cast_w_ff1")
    w2_buf = _cast_into(w_ff2[0], chip_core, name="cast_w_ff2")

    small_w, spans_w = _pack([w_gate_up[0], conv_w[0]])
    small_all = _allgather_small(small_w, reduce=False, name="gather_small_weights")
    chips_first = [small_all[2 * j] for j in range(N_CHIPS)]
    wgu_full = jnp.concatenate(
        [_unpack(b, spans_w[0], w_gate_up.shape[1:]) for b in chips_first], axis=1)
    convw_full = jnp.concatenate(
        [_unpack(b, spans_w[1], conv_w.shape[1:]) for b in chips_first], axis=0)
    wgu_pad = jnp.concatenate(
        [wgu_full, jnp.zeros((LANES - GATE_RANK, wgu_full.shape[1]), F32)], axis=0).astype(BF16)
    convw_t = convw_full.T

    u, wi_buf = _rms_fwd(xs, norm1_g, name="norm1_fwd", comm=_gather_comm([wi_buf]))
    wi_t = wi_buf.reshape(d_in, d)
    wg_t = jnp.concatenate([wi_t[d_main:], jnp.zeros((LANES - GATE_RANK, d), BF16)], axis=0)
    z, wo_buf, w1_buf = _matmul(u, wi_t, tb=True, tm=2048, tn=1024, tk=d, out_dtypes=[F32], n_dim=d_main,
                                name="in_proj",
                                comm=_gather_comm([wo_buf, w1_buf], rows=[None, (0, d // 2)]))
    wo_full = wo_buf.reshape(d, d)
    (alow,) = _matmul(u, wg_t, tb=True, tm=1024, tn=LANES, tk=d, out_dtypes=[F32], name="in_proj_gate")
    y0 = _conv_fwd(z, convw_t, conv_norm_g)
    y, o, la, st, w1_cm = _gla_fwd(z, alow, wgu_pad, b_gate, gla_norm_g, y0,
                                   comm=_gather_comm([w1_buf], rows=[(d // 2, d // 2)], mid_at=0.9))
    (x2,) = _matmul(y, wo_full, tm=1024, tn=1024, tk=d, out_dtypes=[F32], extras=(xs,),
                    epilogue=_add_epilogue, name="out_proj")
    (h,) = _rms_fwd(x2, norm2_g, name="norm2_fwd")
    a, p, w2_buf = _matmul(
        h, w1_cm, tm=1024, tn=1024, tk=d, out_dtypes=[F32, BF16], n_dim=d_ff,
        b_spec=pl.BlockSpec((None, d, 1024), lambda i, j, k: (j // 2, 0, j % 2)),
        epilogue=lambda acc: (acc, jnp.square(jnp.maximum(acc, 0.0))), name="ff1",
        comm=_gather_comm([w2_buf]))
    w2_full = w2_buf.reshape(d_ff, d)
    (x3,) = _matmul(p, w2_full, tm=1024, tn=1024, tk=2048, out_dtypes=[F32], extras=(x2,),
                    epilogue=_add_epilogue, name="ff2")
    dx3b, g_normf, loss_part = _loss_head(x3, norm_f, target)

    (da,) = _matmul(dx3b, w2_full, tb=True, tm=1024, tn=1024, tk=d, out_dtypes=[BF16], extras=(a,),
                    epilogue=lambda acc, av: (acc * (2.0 * jnp.maximum(av, 0.0)),), name="ff2_dx")
    (dh,) = _matmul(
        da, w1_cm, tb=True, tm=2048, tn=1024, tk=2048, out_dtypes=[BF16], n_dim=d,
        b_spec=pl.BlockSpec((None, 1024, 2048), lambda i, j, k: (k, j, 0)), name="ff1_dx")
    dw_k = s_len // 1024
    g_w2, dx2, g_norm2 = _matmul(
        p, dx3b, ta=True, tm=1024, tn=d, tk=1024, out_dtypes=[BF16], name="ff2_dw",
        side=_rms_bwd_side(dh, x2, norm2_g, dx3b, n_steps=(d_ff // 1024) * dw_k,
                           block_of_step=lambda i, j, k: i * dw_k + k))
    (g_w1,) = _matmul(
        h, da, ta=True, tm=1024, tn=n_ff, tk=2048, out_dtypes=[BF16], name="ff1_dw",
        out_shapes=[jax.ShapeDtypeStruct((N_CHIPS, d, n_ff), BF16)],
        out_specs=[pl.BlockSpec((None, 1024, n_ff), lambda i, j, k: (j, i, 0))])
    g_w2 = g_w2.reshape(N_CHIPS, n_ff, d)
    dy, t_w1, t_w2 = _matmul(dx2, wo_full, tb=True, tm=1024, tn=1024, tk=d, out_dtypes=[BF16],
                             name="out_proj_dx", comm=_swap_comm([g_w1, g_w2]))
    p_w1 = _add_own_half(g_w1, t_w1, chip_core, name="pre_reduce_w_ff1")
    p_w2 = _add_own_half(g_w2, t_w2, chip_core, name="pre_reduce_w_ff2")
    ow_k = s_len // 512
    g_wo, dz0, g_convw_t, g_convg = _matmul(
        y, dx2, ta=True, tm=1024, tn=d, tk=512, out_dtypes=[BF16], name="out_proj_dw",
        side=_conv_bwd_side(dy, z, convw_t, conv_norm_g, n_steps=(d // 1024) * ow_k,
                            step_of=lambda i, j, k: i * ow_k + k))
    g_wo = g_wo.reshape(N_CHIPS, d // N_CHIPS, d)
    dz, dalow, g_wgu_pad, g_bg, g_gg, l_w2 = _gla_bwd(dy, z, o, la, st, alow, wgu_pad, gla_norm_g, dz0,
                                                      comm=_exchange_comm([p_w2]))
    m_w2 = _sum_chips(p_w2, l_w2, chip_core, name="reduce_w_ff2")
    g_wi_t, l_w1 = _matmul(dz, u, ta=True, tm=1024, tn=d, tk=2048, out_dtypes=[BF16], name="in_proj_dw",
                           out_shapes=[jax.ShapeDtypeStruct((d_in, d), BF16)],
                           comm=_exchange_comm([p_w1]))
    (g_wi_t,) = _matmul(dalow, u, ta=True, tm=LANES, tn=d, tk=1024, out_dtypes=[BF16],
                        epilogue=lambda acc: (acc[:GATE_RANK],), into=g_wi_t, name="in_proj_gate_dw",
                        out_shapes=[jax.ShapeDtypeStruct((d_in, d), BF16)],
                        out_specs=[pl.BlockSpec((GATE_RANK, d), lambda i, j, k: (d_main // GATE_RANK, 0))])
    m_w1 = _sum_chips(p_w1, l_w1, chip_core, name="reduce_w_ff1")
    g_wi = g_wi_t.reshape(N_CHIPS, n_sh, d)
    du_gate, t_wi, t_wo = _matmul(dalow, wg_t, tm=1024, tn=1024, tk=LANES, out_dtypes=[F32],
                                  name="in_proj_gate_dx", comm=_swap_comm([g_wi, g_wo]))
    p_wi = _add_own_half(g_wi, t_wi, chip_core, name="pre_reduce_w_in")
    p_wo = _add_own_half(g_wo, t_wo, chip_core, name="pre_reduce_w_out")
    du, l_wi, l_wo, m_w1, m_w2 = _matmul(
        dz, wi_t, tm=1024, tn=1024, tk=2048, out_dtypes=[BF16], n_dim=d, extras=(du_gate,),
        epilogue=_add_epilogue, name="in_proj_dx",
        comm=[_exchange_comm([p_wi, p_wo]), _join_comm([m_w1, m_w2])])
    m_wi = _sum_chips(p_wi, l_wi, chip_core, name="reduce_w_in")
    m_wo = _sum_chips(p_wo, l_wo, chip_core, name="reduce_w_out")
    grad_x, g_norm1 = _rms_bwd(du, xs, norm1_g, dx2, name="norm1_bwd")
    m_wi, m_wo = _standalone(_join_comm([m_wi, m_wo]), name="join_w_in_w_out")
    g_big = [m_wi, m_wo, m_w1, m_w2]

    small_g, spans_g = _pack([g_norm1, g_wgu_pad[:GATE_RANK], g_bg, g_convw_t, g_convg, g_gg, g_norm2,
                              g_normf, loss_part[:, :1]])
    tot = _allgather_small(small_g, reduce=True, name="reduce_small_grads")
    t_norm1 = _unpack(tot, spans_g[0], (1, d))
    t_wgu = _unpack(tot, spans_g[1], (GATE_RANK, HEADS * DK))
    t_bg = _unpack(tot, spans_g[2], (1, HEADS * DK))
    t_convw = _unpack(tot, spans_g[3], (3, D_CONV)).T
    t_convg = _unpack(tot, spans_g[4], (1, D_CONV))
    t_gg = _unpack(tot, spans_g[5], (1, DV))
    t_norm2 = _unpack(tot, spans_g[6], (1, d))
    t_normf = _unpack(tot, spans_g[7], (1, d))
    loss = _unpack(tot, spans_g[8], ())
    n_gu = w_gate_up.shape[2]
    n_cw = conv_w.shape[1]
    t_wgu = lax.dynamic_slice(t_wgu, (0, chip * n_gu), (GATE_RANK, n_gu))
    t_convw = lax.dynamic_slice(t_convw, (chip * n_cw, 0), (n_cw, 3))

    order = ["norm1_g", "w_in", "w_gate_up", "b_gate", "conv_w", "conv_norm_g", "gla_norm_g", "w_out",
             "norm2_g", "w_ff1", "w_ff2", "norm_f_g"]
    weights = dict(norm1_g=norm1_g, w_in=w_in, w_gate_up=w_gate_up, b_gate=b_gate, conv_w=conv_w,
                   conv_norm_g=conv_norm_g, gla_norm_g=gla_norm_g, w_out=w_out, norm2_g=norm2_g,
                   w_ff1=w_ff1, w_ff2=w_ff2, norm_f_g=norm_f_g)
    moms = dict(norm1_g=m_norm1_g, w_in=m_w_in, w_gate_up=m_w_gate_up, b_gate=m_b_gate, conv_w=m_conv_w,
                conv_norm_g=m_conv_norm_g, gla_norm_g=m_gla_norm_g, w_out=m_w_out, norm2_g=m_norm2_g,
                w_ff1=m_w_ff1, w_ff2=m_w_ff2, norm_f_g=m_norm_f_g)
    vels = dict(norm1_g=v_norm1_g, w_in=v_w_in, w_gate_up=v_w_gate_up, b_gate=v_b_gate, conv_w=v_conv_w,
                conv_norm_g=v_conv_norm_g, gla_norm_g=v_gla_norm_g, w_out=v_w_out, norm2_g=v_norm2_g,
                w_ff1=v_w_ff1, w_ff2=v_w_ff2, norm_f_g=v_norm_f_g)
    grads2d = dict(norm1_g=t_norm1, w_in=g_big[0], w_gate_up=t_wgu, b_gate=t_bg, conv_w=t_convw,
                   conv_norm_g=t_convg, gla_norm_g=t_gg, w_out=g_big[1], norm2_g=t_norm2,
                   w_ff1=g_big[2], w_ff2=g_big[3], norm_f_g=t_normf)
    out_g, out_d, out_m, out_v = [], [], [], []
    for nm in order:
        w = weights[nm]
        g2 = grads2d[nm]
        if nm == "w_in":
            to2d, back = (lambda t: t[0].T), (lambda t: t.T.reshape(w.shape))
        else:
            to2d, back = (lambda t: t.reshape(g2.shape)), (lambda t: t.reshape(w.shape))
        res = _adamw(to2d(w), g2, to2d(moms[nm]), to2d(vels[nm]), name="adamw_" + nm)
        for lst, r in zip((out_g, out_d, out_m, out_v), res):
            lst.append(back(r))
    return (loss, grad_x.reshape(x.shape), *out_g, *out_d, *out_m, *out_v)
```

```python
import functools

import jax
import jax.numpy as jnp
from jax import lax
from jax.experimental import pallas as pl
from jax.experimental.pallas import tpu as pltpu

F32 = jnp.float32
BF16 = jnp.bfloat16
MESH = pl.DeviceIdType.MESH

EPS = 1e-6
CHUNK = 64
HEADS = 4
DK = 128
DV = 256
D_CONV = 1024
GROUP = 128
GATE_RANK = 16
LANES = 128
SUBLANES = 8
N_CHIPS = 4
N_DEV = 8
CHIP_MASKS = ((1, 0), (0, 1), (1, 1))

ADAM_LR = 0.001
ADAM_B1 = 0.9
ADAM_B2 = 0.999
ADAM_EPS = 1e-08
ADAM_WD = 0.01
ADAM_STEP = 10

VMEM_LIMIT = 56 * 1024 * 1024


def _params(*sem):
    return pltpu.CompilerParams(dimension_semantics=tuple(sem), vmem_limit_bytes=VMEM_LIMIT)


def _rowsum8(v):
    r, c = v.shape
    return jnp.sum(v.reshape(r // SUBLANES, SUBLANES, c), axis=0)


def _tile(rows, cols):
    for cand in (256, 128, 64, 32, 16, 8):
        if rows % cand == 0 and rows > cand:
            return cand, cols
    if rows * cols * 4 > (2 << 20) and cols % 256 == 0:
        return rows, 256
    return rows, cols


class _Comm:
    def __init__(self, ins, outs, aliases, n_sems, first, last, mid=None, mid_at=0.75):
        self.ins = list(ins)
        self.outs = list(outs)
        self.aliases = dict(aliases)
        self.n_sems = n_sems
        self.first = first
        self.mid = mid
        self.mid_at = mid_at
        self.last = last


def _call(body, *, grid, in_specs, out_specs, out_shape, operands, name, scratch_shapes=(), sem=None,
          aliases=None, comm=None):
    aliases = dict(aliases or {})
    if comm is None:
        return pl.pallas_call(
            body, grid=grid, in_specs=list(in_specs), out_specs=list(out_specs), out_shape=list(out_shape),
            scratch_shapes=list(scratch_shapes), input_output_aliases=aliases,
            compiler_params=_params(*(sem or ("arbitrary",) * len(grid))), name=name)(*operands)
    comms = list(comm) if isinstance(comm, (list, tuple)) else [comm]
    n_in, n_out, n_scr = len(in_specs), len(out_specs), len(scratch_shapes)
    c_ins_all = [a for cm in comms for a in cm.ins]
    c_outs_all = [o for cm in comms for o in cm.outs]
    n_ci, n_co = len(c_ins_all), len(c_outs_all)

    def full_body(*refs):
        ins = refs[:n_in]
        o0 = n_in + n_ci
        outs = refs[o0:o0 + n_out]
        s0 = o0 + n_out + n_co
        scr = refs[s0:s0 + n_scr]
        sems = refs[s0 + n_scr:]
        parts, i_at, o_at = [], n_in, o0 + n_out
        for q, cm in enumerate(comms):
            parts.append((cm, refs[i_at:i_at + len(cm.ins)], refs[o_at:o_at + len(cm.outs)],
                          sems[2 * q], sems[2 * q + 1]))
            i_at += len(cm.ins)
            o_at += len(cm.outs)
        step = functools.reduce(lambda acc, ig: acc * ig[1] + pl.program_id(ig[0]), enumerate(grid), 0)
        n_steps = functools.reduce(lambda acc, g: acc * g, grid, 1)
        @pl.when(step == 0)
        def _():
            for cm, c_ins, c_outs, ss, rs in parts:
                cm.first(c_ins, c_outs, ss, rs)

        def mid_step(cm):
            ms = int(cm.mid_at * n_steps)
            return ms if 0 < ms < n_steps - 1 else None

        for cm, c_ins, c_outs, ss, rs in parts:
            if cm.mid is not None and mid_step(cm) is not None:
                @pl.when(step == mid_step(cm))
                def _(cm=cm, c_ins=c_ins, c_outs=c_outs, ss=ss, rs=rs):
                    cm.mid(c_ins, c_outs, ss, rs)

        body(*ins, *outs, *scr)

        @pl.when(step == n_steps - 1)
        def _():
            for cm, c_ins, c_outs, ss, rs in parts:
                if cm.mid is not None and mid_step(cm) is None:
                    cm.mid(c_ins, c_outs, ss, rs)
                cm.last(c_ins, c_outs, ss, rs)

    any_spec = pl.BlockSpec(memory_space=pl.ANY)
    i_at, o_at, sem_shapes = n_in, n_out, []
    for cm in comms:
        for i_in, i_out in cm.aliases.items():
            aliases[i_at + i_in] = o_at + i_out
        i_at += len(cm.ins)
        o_at += len(cm.outs)
        sem_shapes += [pltpu.SemaphoreType.DMA((cm.n_sems,)), pltpu.SemaphoreType.DMA((cm.n_sems,))]
    return pl.pallas_call(
        full_body, grid=grid,
        in_specs=list(in_specs) + [any_spec] * n_ci,
        out_specs=list(out_specs) + [any_spec] * n_co,
        out_shape=list(out_shape) + c_outs_all,
        scratch_shapes=list(scratch_shapes) + sem_shapes,
        input_output_aliases=aliases,
        compiler_params=pltpu.CompilerParams(dimension_semantics=("arbitrary",) * len(grid),
                                             vmem_limit_bytes=VMEM_LIMIT, has_side_effects=True),
        name=name)(*operands, *c_ins_all)


def _matmul(a, b, *, ta=False, tb=False, tm, tn, tk, out_dtypes, name, extras=(), epilogue=None,
            out_shapes=None, out_specs=None, b_spec=None, n_dim=None, into=None, side=None, comm=None):
    n_into = 0 if into is None else 1
    if ta:
        k_dim, m_dim = a.shape
    else:
        m_dim, k_dim = a.shape
    if n_dim is None:
        n_dim = b.shape[0] if tb else b.shape[1]
        assert (b.shape[1] if tb else b.shape[0]) == k_dim
    assert m_dim % tm == 0 and n_dim % tn == 0 and k_dim % tk == 0, (name, a.shape, b.shape)
    nk = k_dim // tk
    n_ex, n_out = len(extras), len(out_dtypes)
    dims = (((0 if ta else 1,), (1 if tb else 0,)), ((), ()))
    grid = (m_dim // tm, n_dim // tn, nk)
    n_steps = grid[0] * grid[1] * grid[2]
    s_ins, s_outs, s_scr = (len(side.ins), len(side.outs), len(side.scratch)) if side else (0, 0, 0)

    def body(*refs):
        a_ref, b_ref = refs[0], refs[1]
        ex_refs = refs[2:2 + n_ex]
        i0 = 2 + n_ex + n_into
        side_in = refs[i0:i0 + s_ins]
        o_refs = refs[i0 + s_ins:i0 + s_ins + n_out]
        side_out = refs[i0 + s_ins + n_out:i0 + s_ins + n_out + s_outs]
        side_scr = refs[len(refs) - s_scr:] if s_scr else ()
        if side is not None:
            step = (pl.program_id(0) * grid[1] + pl.program_id(1)) * grid[2] + pl.program_id(2)

            @pl.when(step == 0)
            def _():
                side.init(side_scr)

        def dot():
            if side is not None:
                side.body(step, side_in, side_out, side_scr)
            return lax.dot_general(a_ref[...].astype(BF16), b_ref[...].astype(BF16), dims,
                                   preferred_element_type=F32)

        def finish(acc):
            outs = epilogue(acc, *[e[...] for e in ex_refs]) if epilogue is not None else (acc,)
            for o_ref, o in zip(o_refs, outs):
                o_ref[...] = o.astype(o_ref.dtype)

        if nk == 1:
            finish(dot())
        else:
            acc_ref = refs[len(refs) - s_scr - 1]
            k = pl.program_id(2)

            @pl.when(k == 0)
            def _():
                acc_ref[...] = dot()

            @pl.when(jnp.logical_and(k > 0, k < nk - 1))
            def _():
                acc_ref[...] += dot()

            @pl.when(k == nk - 1)
            def _():
                finish(acc_ref[...] + dot())

    a_spec = (pl.BlockSpec((tk, tm), lambda i, j, k: (k, i)) if ta
              else pl.BlockSpec((tm, tk), lambda i, j, k: (i, k)))
    if b_spec is None:
        b_spec = (pl.BlockSpec((tn, tk), lambda i, j, k: (j, k)) if tb
                  else pl.BlockSpec((tk, tn), lambda i, j, k: (k, j)))
    io_spec = pl.BlockSpec((tm, tn), lambda i, j, k: (i, j))
    if out_shapes is None:
        out_shapes = [jax.ShapeDtypeStruct((m_dim, n_dim), dt) for dt in out_dtypes]
    if out_specs is None:
        out_specs = [io_spec] * n_out
    return _call(
        body,
        grid=grid,
        in_specs=([a_spec, b_spec] + [io_spec] * n_ex + [pl.BlockSpec(memory_space=pl.ANY)] * n_into
                  + (list(side.in_specs) if side else [])),
        out_specs=list(out_specs) + (list(side.out_specs) if side else []),
        out_shape=list(out_shapes) + (list(side.outs) if side else []),
        scratch_shapes=([pltpu.VMEM((tm, tn), F32)] if nk > 1 else []) + (list(side.scratch) if side else []),
        sem=("parallel", "parallel", "arbitrary") if side is None else None,
        aliases={2 + n_ex: 0} if n_into else None,
        operands=(a, b, *extras) + ((into,) if n_into else ()) + (tuple(side.ins) if side else ()),
        name=name, comm=comm)


def _add_epilogue(acc, r):
    return (acc + r,)


def _rms_fwd(x, g, *, name, tm=512, comm=None):
    s_len, d = x.shape

    def body(x_ref, g_ref, o_ref):
        xv = x_ref[...]
        r = lax.rsqrt(jnp.mean(xv * xv, axis=-1, keepdims=True) + EPS)
        o_ref[...] = (xv * r * g_ref[...]).astype(o_ref.dtype)

    return _call(
        body, grid=(s_len // tm,),
        in_specs=[pl.BlockSpec((tm, d), lambda i: (i, 0)), pl.BlockSpec((1, d), lambda i: (0, 0))],
        out_specs=[pl.BlockSpec((tm, d), lambda i: (i, 0))],
        out_shape=[jax.ShapeDtypeStruct((s_len, d), BF16)],
        sem=("parallel",), operands=(x, g), name=name, comm=comm)


def _zero_refs(refs):
    for r in refs:
        r[...] = jnp.zeros_like(r)


def _rms_bwd_block(dn_ref, x_ref, g_ref, res_ref, dx_ref, dg_ref, acc_ref):
    xv = x_ref[...]
    dnv = dn_ref[...].astype(F32)
    r = lax.rsqrt(jnp.mean(xv * xv, axis=-1, keepdims=True) + EPS)
    xh = xv * r
    acc = acc_ref[...] + _rowsum8(dnv * xh)
    acc_ref[...] = acc
    dg_ref[...] = jnp.sum(acc, axis=0, keepdims=True)
    dxh = dnv * g_ref[...]
    dx_ref[...] = (r * (dxh - xh * jnp.mean(dxh * xh, axis=-1, keepdims=True))
                   + res_ref[...].astype(F32))


class _Side:
    def __init__(self, ins, in_specs, outs, out_specs, scratch, init, body):
        self.ins, self.in_specs, self.outs, self.out_specs = ins, in_specs, outs, out_specs
        self.scratch = scratch
        self.init = init
        self.body = body


def _rms_bwd_side(dn, x, g, res, *, block_of_step, n_steps):
    s_len, d = x.shape
    row = pl.BlockSpec((s_len // n_steps, d), lambda i, j, k: (block_of_step(i, j, k), 0))
    vec = pl.BlockSpec((1, d), lambda i, j, k: (0, 0))
    return _Side(
        ins=[dn, x, g, res], in_specs=[row, row, vec, row],
        outs=[jax.ShapeDtypeStruct((s_len, d), F32), jax.ShapeDtypeStruct((1, d), F32)],
        out_specs=[row, vec], scratch=[pltpu.VMEM((SUBLANES, d), F32)],
        init=_zero_refs, body=lambda step, ins, outs, scr: _rms_bwd_block(*ins, *outs, *scr))


def _rms_bwd(dn, x, g, res, *, name, tm=512, comm=None):
    s_len, d = x.shape
    n = s_len // tm

    def body(*refs):
        @pl.when(pl.program_id(0) == 0)
        def _():
            _zero_refs(refs[-1:])

        _rms_bwd_block(*refs)

    row = pl.BlockSpec((tm, d), lambda i: (i, 0))
    vec = pl.BlockSpec((1, d), lambda i: (0, 0))
    return _call(
        body, grid=(n,),
        in_specs=[row, row, vec, row],
        out_specs=[row, vec],
        out_shape=[jax.ShapeDtypeStruct((s_len, d), F32), jax.ShapeDtypeStruct((1, d), F32)],
        scratch_shapes=[pltpu.VMEM((SUBLANES, d), F32)],
        operands=(dn, x, g, res), name=name, comm=comm)


def _loss_head(x3, g, target, *, tm=512):
    s_len, d = x3.shape
    n = s_len // tm

    def body(x_ref, g_ref, t_ref, dxb_ref, dg_ref, loss_ref, accg_ref, accl_ref):
        i = pl.program_id(0)
        xv = x_ref[...]
        gv = g_ref[...]
        r = lax.rsqrt(jnp.mean(xv * xv, axis=-1, keepdims=True) + EPS)
        xh = xv * r
        err = xh * gv - t_ref[...]

        @pl.when(i == 0)
        def _():
            accg_ref[...] = jnp.zeros_like(accg_ref)
            accl_ref[...] = jnp.zeros_like(accl_ref)

        accl_ref[...] += _rowsum8(err * err)
        dn = err * (1.0 / d)
        accg_ref[...] += _rowsum8(dn * xh)
        dxh = dn * gv
        dx = r * (dxh - xh * jnp.mean(dxh * xh, axis=-1, keepdims=True))
        dxb_ref[...] = dx.astype(BF16)

        @pl.when(i == n - 1)
        def _():
            dg_ref[...] = jnp.sum(accg_ref[...], axis=0, keepdims=True)
            tot = jnp.sum(jnp.sum(accl_ref[...], axis=0, keepdims=True), axis=1, keepdims=True)
            loss_ref[...] = jnp.broadcast_to(tot * (0.5 / d), (1, LANES))

    row = pl.BlockSpec((tm, d), lambda i: (i, 0))
    vec = pl.BlockSpec((1, d), lambda i: (0, 0))
    return pl.pallas_call(
        body, grid=(n,),
        in_specs=[row, vec, row],
        out_specs=[row, vec, pl.BlockSpec((1, LANES), lambda i: (0, 0))],
        out_shape=[jax.ShapeDtypeStruct((s_len, d), BF16),
                   jax.ShapeDtypeStruct((1, d), F32), jax.ShapeDtypeStruct((1, LANES), F32)],
        scratch_shapes=[pltpu.VMEM((SUBLANES, d), F32), pltpu.VMEM((SUBLANES, d), F32)],
        compiler_params=_params("arbitrary"), name="loss_head",
    )(x3, g, target)


def _shift_down(v, k, rows_before, row):
    out = pltpu.roll(v, k, axis=0)
    for j in range(k):
        out = jnp.where(row == j, rows_before[j], out)
    return out


def _shift_up(v, k, rows_after, row):
    t = v.shape[0]
    out = pltpu.roll(v, t - k, axis=0)
    for j in range(k):
        out = jnp.where(row == t - k + j, rows_after[j], out)
    return out


def _conv_fwd(z, w_t, gain, *, ts=512):
    s_len = z.shape[0]
    n_grp = D_CONV // GROUP

    def body(cb_ref, cc_ref, ch_ref, w_ref, g_ref, y_ref, carry_ref):
        i = pl.program_id(0)

        @pl.when(i == 0)
        def _():
            carry_ref[...] = jnp.zeros_like(carry_ref)

        row = lax.broadcasted_iota(jnp.int32, (ts, GROUP), 0)
        for g in range(n_grp):
            sl = slice(g * GROUP, (g + 1) * GROUP)
            uu = cc_ref[:, sl] * ch_ref[:, sl]
            p2 = carry_ref[6:7, sl]
            p1 = carry_ref[7:8, sl]
            u1 = _shift_down(uu, 1, [p1], row)
            u2 = _shift_down(uu, 2, [p2, p1], row)
            conv = w_ref[0:1, sl] * u2 + w_ref[1:2, sl] * u1 + w_ref[2:3, sl] * uu
            y = cb_ref[:, sl] * conv
            carry_ref[:, sl] = uu[ts - SUBLANES:ts, :]
            rg = lax.rsqrt(jnp.mean(y * y, axis=-1, keepdims=True) + EPS)
            y_ref[:, sl] = (y * rg * g_ref[:, sl]).astype(BF16)

    def col(j):
        return pl.BlockSpec((ts, D_CONV), lambda i, j=j: (i, j))

    small = lambda r: pl.BlockSpec((r, D_CONV), lambda i: (0, 0))
    return pl.pallas_call(
        body, grid=(s_len // ts,),
        in_specs=[col(0), col(1), col(2), small(3), small(1)],
        out_specs=col(0),
        out_shape=jax.ShapeDtypeStruct((s_len, 2 * D_CONV), BF16),
        scratch_shapes=[pltpu.VMEM((SUBLANES, D_CONV), F32)],
        compiler_params=_params("arbitrary"), name="conv_fwd",
    )(z, z, z, w_t, gain)


def _conv_bwd_side(dy, z, w_t, gain, *, n_steps, step_of):
    s_len = z.shape[0]
    n = n_steps
    ts = s_len // n
    n_grp = D_CONV // GROUP
    halo_blocks = ts // SUBLANES

    def body(step, ins, outs, scr):
        dy_ref, cb_ref, cc_ref, ch_ref, hcc_ref, hch_ref, w_ref, g_ref = ins
        dz_ref, dw_ref, dg_ref = outs
        carry_ref, accw_ref, accg_ref = scr
        first_tile = (n - 1 - step) == 0
        row = lax.broadcasted_iota(jnp.int32, (ts, GROUP), 0)
        keep = jnp.where(first_tile, 0.0, 1.0)
        for g in range(n_grp):
            sl = slice(g * GROUP, (g + 1) * GROUP)
            cc = cc_ref[:, sl]
            ch = ch_ref[:, sl]
            cb = cb_ref[:, sl]
            uu = cc * ch
            p2 = hcc_ref[6:7, sl] * hch_ref[6:7, sl] * keep
            p1 = hcc_ref[7:8, sl] * hch_ref[7:8, sl] * keep
            u1 = _shift_down(uu, 1, [p1], row)
            u2 = _shift_down(uu, 2, [p2, p1], row)
            w0, w1, w2 = w_ref[0:1, sl], w_ref[1:2, sl], w_ref[2:3, sl]
            conv = w0 * u2 + w1 * u1 + w2 * uu
            y = cb * conv
            rg = lax.rsqrt(jnp.mean(y * y, axis=-1, keepdims=True) + EPS)
            yh = y * rg
            dyv = dy_ref[:, sl].astype(F32)
            accg_ref[:, sl] += _rowsum8(dyv * yh)
            dyn = dyv * g_ref[:, sl]
            dpre = rg * (dyn - yh * jnp.mean(dyn * yh, axis=-1, keepdims=True))
            dz_ref[:, sl] = (dpre * conv).astype(BF16)
            dconv = dpre * cb
            accw_ref[0:8, sl] += _rowsum8(dconv * u2)
            accw_ref[8:16, sl] += _rowsum8(dconv * u1)
            accw_ref[16:24, sl] += _rowsum8(dconv * uu)
            n0 = carry_ref[0:1, sl]
            n1 = carry_ref[1:2, sl]
            d1 = _shift_up(dconv, 1, [n0], row)
            d2 = _shift_up(dconv, 2, [n0, n1], row)
            duu = w2 * dconv + w1 * d1 + w0 * d2
            carry_ref[:, sl] = dconv[0:SUBLANES, :]
            dz_ref[:, D_CONV + g * GROUP:D_CONV + (g + 1) * GROUP] = (duu * ch).astype(BF16)
            dz_ref[:, 2 * D_CONV + g * GROUP:2 * D_CONV + (g + 1) * GROUP] = (duu * cc).astype(BF16)

        for k in range(3):
            dw_ref[k:k + 1, :] = jnp.sum(accw_ref[8 * k:8 * k + 8, :], axis=0, keepdims=True)
        dg_ref[...] = jnp.sum(accg_ref[...], axis=0, keepdims=True)

    def tile(i, j, k):
        return n - 1 - step_of(i, j, k)

    def col(c):
        return pl.BlockSpec((ts, D_CONV), lambda i, j, k, c=c: (tile(i, j, k), c))

    def halo(c):
        return pl.BlockSpec((SUBLANES, D_CONV),
                            lambda i, j, k, c=c: (jnp.maximum(tile(i, j, k) * halo_blocks - 1, 0), c))

    small = lambda r: pl.BlockSpec((r, D_CONV), lambda i, j, k: (0, 0))
    return _Side(
        ins=[dy, z, z, z, z, z, w_t, gain],
        in_specs=[col(0), col(0), col(1), col(2), halo(1), halo(2), small(3), small(1)],
        outs=[jax.ShapeDtypeStruct((s_len, 6 * D_CONV), BF16),
              jax.ShapeDtypeStruct((3, D_CONV), F32), jax.ShapeDtypeStruct((1, D_CONV), F32)],
        out_specs=[pl.BlockSpec((ts, 3 * D_CONV), lambda i, j, k: (tile(i, j, k), 0)), small(3), small(1)],
        scratch=[pltpu.VMEM((SUBLANES, D_CONV), F32), pltpu.VMEM((24, D_CONV), F32),
                 pltpu.VMEM((SUBLANES, D_CONV), F32)],
        init=_zero_refs, body=body)


def _split3(v):
    hi = v.astype(BF16)
    r1 = v - hi.astype(F32)
    mid = r1.astype(BF16)
    lo = (r1 - mid.astype(F32)).astype(BF16)
    return jnp.concatenate([hi, mid, lo], axis=1)


def _tri_sum(tri, v):
    w = v.shape[1]
    dd = jnp.dot(tri, _split3(v), preferred_element_type=F32)
    return dd[:, :w] + dd[:, w:2 * w] + dd[:, 2 * w:]


def _chunk_masks(ts):
    r = jnp.arange(ts)
    same = (r[:, None] // CHUNK) == (r[None, :] // CHUNK)
    later = jnp.logical_and(same, r[None, :] > r[:, None]).astype(BF16)
    earlier = jnp.logical_and(same, r[None, :] < r[:, None]).astype(BF16)
    chunk_of_row = jnp.arange(ts // CHUNK * SUBLANES)[:, None] // SUBLANES
    member = (chunk_of_row == (r[None, :] // CHUNK)).astype(BF16)
    return later, earlier, member


def _sigmoid(v):
    return 0.5 * jnp.tanh(0.5 * v) + 0.5


def _gla_fwd(z, alow, wgu, bg, gg, y_in, *, ts=512, comm=None):
    s_len = z.shape[0]
    nch = ts // CHUNK
    scale = DK ** -0.5

    tri_u, _, ind8 = _chunk_masks(ts)

    def body(q_ref, k_ref, v_ref, og_ref, al_ref, wgu_ref, bg_ref, gg_ref, tu_ref, ind_ref, yin_ref,
             y_ref, o_ref, la_ref, st_ref, state_ref, kd_ref, qs_ref, dec_ref):
        del yin_ref
        i = pl.program_id(0)

        @pl.when(i == 0)
        def _():
            state_ref[...] = jnp.zeros_like(state_ref)

        pre = jnp.dot(al_ref[...].astype(BF16), wgu_ref[...], preferred_element_type=F32) + bg_ref[...]
        la = (jnp.minimum(pre, 0.0) - jnp.log(1.0 + jnp.exp(-jnp.abs(pre)))) * (1.0 / 16.0)
        la_ref[...] = la
        kd_ref[...] = (k_ref[...] * jnp.exp(_tri_sum(tu_ref[...], la))).astype(BF16)
        qs_ref[...] = (q_ref[...] * scale).astype(BF16)
        dec_ref[...] = jnp.exp(_tri_sum(ind_ref[...], la))

        def chunk(cl, carry):
            rows = pl.ds(pl.multiple_of(cl * CHUNK, CHUNK), CHUNK)
            dec = dec_ref[pl.ds(pl.multiple_of(cl * SUBLANES, SUBLANES), 1), :]
            for h in range(HEADS):
                ks = slice(h * DK, (h + 1) * DK)
                vs = slice(h * DV, (h + 1) * DV)
                kv_t = lax.dot_general(v_ref[rows, vs].astype(BF16), kd_ref[rows, ks],
                                       (((0,), (0,)), ((), ())), preferred_element_type=F32)
                st = state_ref[h] * dec[:, ks] + kv_t
                state_ref[h] = st
                st_ref[cl, h] = st
                o_ref[rows, vs] = lax.dot_general(qs_ref[rows, ks], st.astype(BF16),
                                                  (((1,), (1,)), ((), ())), preferred_element_type=F32)
            return carry

        lax.fori_loop(0, nch, chunk, 0, unroll=2)

        ggv = gg_ref[...]
        for h in range(HEADS):
            vs = slice(h * DV, (h + 1) * DV)
            o_h = o_ref[:, vs]
            og_h = og_ref[:, vs]
            ro = lax.rsqrt(jnp.mean(o_h * o_h, axis=-1, keepdims=True) + EPS)
            y_ref[:, vs] = (o_h * ro * ggv * (og_h * _sigmoid(og_h))).astype(BF16)

    def zcol(width, j):
        return pl.BlockSpec((ts, width), lambda i, j=j: (i, j))

    full = lambda shape: pl.BlockSpec(shape, lambda i: tuple(0 for _ in shape))
    return _call(
        body, grid=(s_len // ts,),
        in_specs=[zcol(512, 6), zcol(512, 7), zcol(1024, 4), zcol(1024, 5), zcol(LANES, 0),
                  full((LANES, 512)), full((1, 512)), full((1, DV)), full(tri_u.shape), full(ind8.shape),
                  pl.BlockSpec(memory_space=pl.ANY)],
        out_specs=[zcol(1024, 1), zcol(1024, 0), zcol(512, 0),
                   pl.BlockSpec((nch, HEADS, DV, DK), lambda i: (i, 0, 0, 0))],
        out_shape=[jax.ShapeDtypeStruct((s_len, 2048), BF16), jax.ShapeDtypeStruct((s_len, 1024), F32),
                   jax.ShapeDtypeStruct((s_len, 512), F32),
                   jax.ShapeDtypeStruct((s_len // CHUNK, HEADS, DV, DK), F32)],
        scratch_shapes=[pltpu.VMEM((HEADS, DV, DK), F32), pltpu.VMEM((ts, 512), BF16),
                        pltpu.VMEM((ts, 512), BF16), pltpu.VMEM((nch * SUBLANES, 512), F32)],
        aliases={10: 0},
        operands=(z, z, z, z, alow, wgu, bg, gg, tri_u, ind8, y_in), name="gla_fwd", comm=comm)


def _gla_bwd(dy, z, o, la, st, alow, wgu, gg, dz_in, *, ts=512, comm=None):
    s_len = z.shape[0]
    n = s_len // ts
    nch = ts // CHUNK
    scale = DK ** -0.5

    tri_u, tri_l, ind8 = _chunk_masks(ts)

    def body(dy_ref, q_ref, k_ref, v_ref, og_ref, o_ref, la_ref, st_ref, stp_ref, al_ref, wgu_ref,
             gg_ref, tu_ref, tl_ref, ind_ref, dzin_ref, dz_ref, dal_ref, dwgu_ref, dbg_ref, dgg_ref,
             gt_ref, decn_ref, accw_ref, accb_ref, accg_ref, dla_ref,
             e_ref, kd_ref, kdb_ref, qs_ref, do_ref, dkd_ref, dec_ref, dbe_ref):
        del dzin_ref
        i = pl.program_id(0)
        first_tile = (n - 1 - i) == 0

        @pl.when(i == 0)
        def _():
            gt_ref[...] = jnp.zeros_like(gt_ref)
            decn_ref[...] = jnp.ones_like(decn_ref)
            accw_ref[...] = jnp.zeros_like(accw_ref)
            accb_ref[...] = jnp.zeros_like(accb_ref)
            accg_ref[...] = jnp.zeros_like(accg_ref)

        la = la_ref[...]
        e_dec = jnp.exp(_tri_sum(tu_ref[...], la))
        e_ref[...] = e_dec
        kd = k_ref[...] * e_dec
        kd_ref[...] = kd
        kdb_ref[...] = kd.astype(BF16)
        qs_ref[...] = (q_ref[...] * scale).astype(BF16)
        dec_ref[...] = jnp.exp(_tri_sum(ind_ref[...], la))
        ggv = gg_ref[...]
        for h in range(HEADS):
            vs = slice(h * DV, (h + 1) * DV)
            o_h = o_ref[:, vs]
            og_h = og_ref[:, vs]
            dy_h = dy_ref[:, vs].astype(F32)
            ro = lax.rsqrt(jnp.mean(o_h * o_h, axis=-1, keepdims=True) + EPS)
            oh = o_h * ro
            sig = _sigmoid(og_h)
            sil = og_h * sig
            accg_ref[...] += _rowsum8(dy_h * oh * sil)
            dz_ref[:, 2048 + h * DV:2048 + (h + 1) * DV] = (
                dy_h * oh * ggv * sig * (1.0 + og_h * (1.0 - sig))).astype(BF16)
            don = dy_h * ggv * sil
            do_ref[:, vs] = (ro * (don - oh * jnp.mean(don * oh, axis=-1, keepdims=True))).astype(BF16)
        keep = jnp.where(first_tile, 0.0, 1.0)

        def chunk(jrev, decn):
            cl = nch - 1 - jrev
            rows = pl.ds(pl.multiple_of(cl * CHUNK, CHUNK), CHUNK)
            one_row = pl.ds(pl.multiple_of(cl * SUBLANES, SUBLANES), 1)
            dec = dec_ref[one_row, :]
            has_prev = jnp.where(cl > 0, 1.0, 0.0)
            prev_idx = jnp.maximum(cl - 1, 0)
            for h in range(HEADS):
                ks = slice(h * DK, (h + 1) * DK)
                vs = slice(h * DV, (h + 1) * DV)
                dob = do_ref[rows, vs]
                s_c = st_ref[cl, h]
                dqs = jnp.dot(dob, s_c.astype(BF16), preferred_element_type=F32)
                dz_ref[rows, ks] = (dqs * scale).astype(BF16)
                gt = gt_ref[h] * decn[:, ks] + lax.dot_general(
                    dob, qs_ref[rows, ks], (((0,), (0,)), ((), ())), preferred_element_type=F32)
                gt_ref[h] = gt
                gb = gt.astype(BF16)
                dkd_ref[rows, ks] = jnp.dot(v_ref[rows, vs].astype(BF16), gb, preferred_element_type=F32)
                dz_ref[rows, 1024 + h * DV:1024 + (h + 1) * DV] = lax.dot_general(
                    kdb_ref[rows, ks], gb, (((1,), (1,)), ((), ())),
                    preferred_element_type=F32).astype(BF16)
                s_prev = has_prev * st_ref[prev_idx, h] + (1.0 - has_prev) * keep * stp_ref[0, h]
                dbe_ref[one_row, ks] = jnp.sum(gt * s_prev, axis=0, keepdims=True) * dec[:, ks]
            return dec

        decn_ref[0:1, :] = lax.fori_loop(0, nch, chunk, decn_ref[0:1, :], unroll=2)

        dkd = dkd_ref[...]
        dz_ref[:, 512:1024] = (dkd * e_ref[...]).astype(BF16)
        dla_ref[...] = _tri_sum(tl_ref[...], dkd * kd_ref[...])
        for c in range(nch):
            dla_ref[c * CHUNK:(c + 1) * CHUNK, :] += dbe_ref[c * SUBLANES:c * SUBLANES + 1, :]
        dpre = dla_ref[...] * (1.0 / 16.0) * (1.0 - jnp.exp(16.0 * la))
        accb_ref[...] += _rowsum8(dpre)
        dpb = dpre.astype(BF16)
        accw_ref[...] += lax.dot_general(al_ref[...].astype(BF16), dpb, (((0,), (0,)), ((), ())),
                                         preferred_element_type=F32)
        dal_ref[...] = lax.dot_general(dpb, wgu_ref[...], (((1,), (1,)), ((), ())),
                                       preferred_element_type=F32)

        @pl.when(i == n - 1)
        def _():
            dwgu_ref[...] = accw_ref[...]
            dbg_ref[...] = jnp.sum(accb_ref[...], axis=0, keepdims=True)
            dgg_ref[...] = jnp.sum(accg_ref[...], axis=0, keepdims=True)

    def zcol(width, j):
        return pl.BlockSpec((ts, width), lambda i, j=j: (n - 1 - i, j))

    full = lambda shape: pl.BlockSpec(shape, lambda i: tuple(0 for _ in shape))
    return _call(
        body, grid=(n,),
        in_specs=[zcol(1024, 1), zcol(512, 6), zcol(512, 7), zcol(1024, 4), zcol(1024, 5),
                  zcol(1024, 0), zcol(512, 0),
                  pl.BlockSpec((nch, HEADS, DV, DK), lambda i: (n - 1 - i, 0, 0, 0)),
                  pl.BlockSpec((1, HEADS, DV, DK),
                               lambda i: (jnp.maximum((n - 1 - i) * nch - 1, 0), 0, 0, 0)),
                  zcol(LANES, 0), full((LANES, 512)), full((1, DV)),
                  full(tri_u.shape), full(tri_l.shape), full(ind8.shape),
                  pl.BlockSpec(memory_space=pl.ANY)],
        out_specs=[zcol(3072, 1), zcol(LANES, 0), full((LANES, 512)), full((1, 512)), full((1, DV))],
        out_shape=[jax.ShapeDtypeStruct((s_len, 6144), BF16), jax.ShapeDtypeStruct((s_len, LANES), F32),
                   jax.ShapeDtypeStruct((LANES, 512), F32), jax.ShapeDtypeStruct((1, 512), F32),
                   jax.ShapeDtypeStruct((1, DV), F32)],
        scratch_shapes=[pltpu.VMEM((HEADS, DV, DK), F32), pltpu.VMEM((SUBLANES, 512), F32),
                        pltpu.VMEM((LANES, 512), F32), pltpu.VMEM((SUBLANES, 512), F32),
                        pltpu.VMEM((SUBLANES, DV), F32), pltpu.VMEM((ts, 512), F32),
                        pltpu.VMEM((ts, 512), F32), pltpu.VMEM((ts, 512), F32), pltpu.VMEM((ts, 512), BF16),
                        pltpu.VMEM((ts, 512), BF16), pltpu.VMEM((ts, 1024), BF16),
                        pltpu.VMEM((ts, 512), F32), pltpu.VMEM((nch * SUBLANES, 512), F32),
                        pltpu.VMEM((nch * SUBLANES, 512), F32)],
        aliases={15: 0},
        operands=(dy, z, z, z, z, o, la, st, st, alow, wgu, gg, tri_u, tri_l, ind8, dz_in),
        name="gla_bwd", comm=comm)


def _adamw(w, g, m, v, *, name):
    rows, cols = w.shape
    tr, tc = _tile(rows, cols)

    def body(w_ref, g_ref, m_ref, v_ref, go_ref, d_ref, nm_ref, nv_ref):
        gv = g_ref[...]
        go_ref[...] = gv
        m2 = ADAM_B1 * m_ref[...] + (1.0 - ADAM_B1) * gv
        v2 = ADAM_B2 * v_ref[...] + (1.0 - ADAM_B2) * jnp.square(gv)
        m_hat = m2 / (1.0 - ADAM_B1 ** ADAM_STEP)
        v_hat = v2 / (1.0 - ADAM_B2 ** ADAM_STEP)
        d_ref[...] = -ADAM_LR * (m_hat / (jnp.sqrt(v_hat) + ADAM_EPS) + ADAM_WD * w_ref[...])
        nm_ref[...] = m2
        nv_ref[...] = v2

    blk = pl.BlockSpec((tr, tc), lambda i, j: (i, j))
    shp = jax.ShapeDtypeStruct((rows, cols), F32)
    return pl.pallas_call(
        body, grid=(rows // tr, cols // tc), in_specs=[blk] * 4, out_specs=[blk] * 4, out_shape=[shp] * 4,
        compiler_params=_params("parallel", "parallel"), name=name,
    )(w, g, m, v)


def _my_place():
    return lax.axis_index("x"), lax.axis_index("y"), lax.axis_index("c")


def _flip(v, bit):
    return 1 - v if bit else v


def _allgather_small(buf, *, reduce, name):
    rows = buf.shape[0]

    def body(in_ref, out_ref, gat_ref, send_sems, recv_sems):
        x, y, c = _my_place()
        me = 4 * x + 2 * y + c
        gat_ref[me] = in_ref[...]
        copies = []
        for m in range(1, N_DEV):
            peer = (_flip(x, m & 4), _flip(y, m & 2), _flip(c, m & 1))
            cp = pltpu.make_async_remote_copy(
                src_ref=in_ref, dst_ref=gat_ref.at[me],
                send_sem=send_sems.at[m - 1], recv_sem=recv_sems.at[m - 1],
                device_id=peer, device_id_type=MESH)
            cp.start()
            copies.append(cp)
        for m in range(1, N_DEV):
            px, py, pc = _flip(x, m & 4), _flip(y, m & 2), _flip(c, m & 1)
            src_slot = gat_ref.at[4 * px + 2 * py + pc]
            pltpu.make_async_remote_copy(
                src_ref=src_slot, dst_ref=src_slot,
                send_sem=send_sems.at[m - 1], recv_sem=recv_sems.at[m - 1],
                device_id=(px, py, pc), device_id_type=MESH).wait_recv()
        for cp in copies:
            cp.wait_send()
        if reduce:
            tot = gat_ref[0]
            for d in range(1, N_DEV):
                tot = tot + gat_ref[d]
            out_ref[...] = tot
        else:
            out_ref[...] = gat_ref[...]

    out_shape = (rows, LANES) if reduce else (N_DEV, rows, LANES)
    return pl.pallas_call(
        body,
        in_specs=[pl.BlockSpec(memory_space=pltpu.VMEM)],
        out_specs=pl.BlockSpec(memory_space=pltpu.VMEM),
        out_shape=jax.ShapeDtypeStruct(out_shape, F32),
        scratch_shapes=[pltpu.VMEM((N_DEV, rows, LANES), F32),
                        pltpu.SemaphoreType.DMA((N_DEV - 1,)), pltpu.SemaphoreType.DMA((N_DEV - 1,))],
        compiler_params=pltpu.CompilerParams(has_side_effects=True),
        name=name,
    )(buf)


def _cast_into(shard, chip_core, *, name):
    rows, cols = shard.shape
    tr, tc = _tile(rows, cols)

    def body(cc_ref, s_ref, o_ref):
        del cc_ref
        o_ref[...] = s_ref[...].astype(BF16)

    grid_spec = pltpu.PrefetchScalarGridSpec(
        num_scalar_prefetch=1, grid=(rows // tr, cols // tc),
        in_specs=[pl.BlockSpec((tr, tc), lambda r, q, cc: (r, q))],
        out_specs=pl.BlockSpec((None, tr, tc), lambda r, q, cc: (cc[0], r, q)))
    return pl.pallas_call(
        body, grid_spec=grid_spec, out_shape=jax.ShapeDtypeStruct((N_CHIPS, rows, cols), BF16),
        compiler_params=_params("arbitrary", "arbitrary"), name=name,
    )(chip_core, shard)


def _remote(src, dst, send_sems, recv_sems, k, device):
    return pltpu.make_async_remote_copy(src_ref=src, dst_ref=dst, send_sem=send_sems.at[k],
                                        recv_sem=recv_sems.at[k], device_id=device, device_id_type=MESH)


def _col_half(ref, h, *lead, rows=None):
    hc = ref.shape[-1] // 2
    mid = (slice(None),) * (len(ref.shape) - 2 - len(lead))
    row_sel = slice(None) if rows is None else pl.ds(rows[0], rows[1])
    return ref.at[tuple(lead) + mid + (row_sel, pl.ds(h * hc, hc))]


def _gather_comm(bufs, rows=None, mid_at=0.75):
    n_w, n_m = len(bufs), len(CHIP_MASKS)
    rows = rows or [None] * n_w

    def first(c_ins, c_outs, ss, rs):
        x, y, c = _my_place()
        chip = 2 * x + y
        for w in range(n_w):
            mine = _col_half(c_outs[w], c, chip, rows=rows[w])
            for mi, (mx, my) in enumerate(CHIP_MASKS):
                _remote(mine, mine, ss, rs, w * n_m + mi, (_flip(x, mx), _flip(y, my), c)).start()

    def mid(c_ins, c_outs, ss, rs):
        x, y, c = _my_place()
        for w in range(n_w):
            for mi, (mx, my) in enumerate(CHIP_MASKS):
                k = w * n_m + mi
                px, py = _flip(x, mx), _flip(y, my)
                landed = _col_half(c_outs[w], c, 2 * px + py, rows=rows[w])
                _remote(landed, landed, ss, rs, k, (px, py, c)).wait_recv()
                _remote(landed, landed, ss, rs, n_w * n_m + k, (x, y, 1 - c)).start()

    def last(c_ins, c_outs, ss, rs):
        x, y, c = _my_place()
        chip = 2 * x + y
        for w in range(n_w):
            mine = _col_half(c_outs[w], c, chip, rows=rows[w])
            for mi, (mx, my) in enumerate(CHIP_MASKS):
                k = w * n_m + mi
                px, py = _flip(x, mx), _flip(y, my)
                theirs = _col_half(c_outs[w], 1 - c, 2 * px + py, rows=rows[w])
                _remote(theirs, theirs, ss, rs, n_w * n_m + k, (x, y, 1 - c)).wait_recv()
                _remote(mine, mine, ss, rs, k, (px, py, c)).wait_send()
                _remote(mine, mine, ss, rs, n_w * n_m + k, (x, y, 1 - c)).wait_send()

    return _Comm(ins=bufs, outs=[jax.ShapeDtypeStruct(b.shape, b.dtype) for b in bufs],
                 aliases={w: w for w in range(n_w)}, n_sems=2 * n_w * n_m, first=first, mid=mid, last=last,
                 mid_at=mid_at)


def _swap_comm(grads):
    n_w = len(grads)

    def copy(c_ins, c_outs, ss, rs, w):
        x, y, c = _my_place()
        return _remote(_col_half(c_ins[w], 1 - c), c_outs[w], ss, rs, w, (x, y, 1 - c))

    def first(c_ins, c_outs, ss, rs):
        for w in range(n_w):
            copy(c_ins, c_outs, ss, rs, w).start()

    def last(c_ins, c_outs, ss, rs):
        for w in range(n_w):
            copy(c_ins, c_outs, ss, rs, w).wait()

    return _Comm(ins=grads,
                 outs=[jax.ShapeDtypeStruct(g.shape[:2] + (g.shape[2] // 2,), g.dtype) for g in grads],
                 aliases={}, n_sems=n_w, first=first, last=last)


def _add_own_half(g, other, chip_core, *, name):
    n_chip, rows, hc = other.shape
    tr, tc = _tile(rows, hc)
    per_half = hc // tc

    def body(cc_ref, g_ref, o_ref, out_ref):
        del cc_ref
        out_ref[...] = (g_ref[...].astype(F32) + o_ref[...].astype(F32)).astype(BF16)

    grid_spec = pltpu.PrefetchScalarGridSpec(
        num_scalar_prefetch=1, grid=(n_chip, rows // tr, per_half),
        in_specs=[pl.BlockSpec((None, tr, tc),
                               lambda j, r, q, cc: (j, r, lax.axis_index("c") * per_half + q)),
                  pl.BlockSpec((None, tr, tc), lambda j, r, q, cc: (j, r, q))],
        out_specs=pl.BlockSpec((None, tr, tc), lambda j, r, q, cc: (j, r, q)))
    return pl.pallas_call(
        body, grid_spec=grid_spec, out_shape=jax.ShapeDtypeStruct((n_chip, rows, hc), BF16),
        compiler_params=_params("parallel", "parallel", "parallel"), name=name,
    )(chip_core, g, other)


def _add_own_half_side(gs, others, *, n_steps, step_of):
    n_chip, rows, hc = others[0].shape
    per_chip = n_steps // n_chip
    tr = rows // per_chip
    assert all(o.shape == others[0].shape for o in others) and tr * per_chip == rows and tr % 16 == 0

    def body(step, ins, outs, scr):
        del step, scr
        for q, out_ref in enumerate(outs):
            out_ref[...] = (ins[2 * q][...].astype(F32) + ins[2 * q + 1][...].astype(F32)).astype(BF16)

    def blk(own_half):
        def index(i, j, k):
            s = step_of(i, j, k)
            return (s // per_chip, s % per_chip, lax.axis_index("c") if own_half else 0)
        return pl.BlockSpec((None, tr, hc), index)

    return _Side(
        ins=[a for pair in zip(gs, others) for a in pair], in_specs=[blk(True), blk(False)] * len(gs),
        outs=[jax.ShapeDtypeStruct(o.shape, BF16) for o in others], out_specs=[blk(False)] * len(gs),
        scratch=[], init=_zero_refs, body=body)


def _run_side(side, *, n_steps, name):
    n_in, n_out = len(side.ins), len(side.outs)

    def body(*refs):
        step = pl.program_id(0)

        @pl.when(step == 0)
        def _():
            side.init(refs[n_in + n_out:])

        side.body(step, refs[:n_in], refs[n_in:n_in + n_out], refs[n_in + n_out:])

    return _call(body, grid=(n_steps, 1, 1), in_specs=side.in_specs, out_specs=side.out_specs,
                 out_shape=side.outs, scratch_shapes=side.scratch, operands=tuple(side.ins), name=name)


def _exchange_comm(pieces):
    n_w, n_m = len(pieces), len(CHIP_MASKS)

    def copies(c_ins, c_outs, ss, rs):
        x, y, c = _my_place()
        chip = 2 * x + y
        for w in range(n_w):
            for mi, (mx, my) in enumerate(CHIP_MASKS):
                px, py = _flip(x, mx), _flip(y, my)
                send = _remote(c_ins[w].at[2 * px + py], c_outs[w].at[chip], ss, rs, w * n_m + mi, (px, py, c))
                landed = c_outs[w].at[2 * px + py]
                yield send, _remote(landed, landed, ss, rs, w * n_m + mi, (px, py, c))

    def first(c_ins, c_outs, ss, rs):
        for send, _ in copies(c_ins, c_outs, ss, rs):
            send.start()

    def last(c_ins, c_outs, ss, rs):
        for send, arrival in copies(c_ins, c_outs, ss, rs):
            arrival.wait_recv()
            send.wait_send()

    return _Comm(ins=pieces, outs=[jax.ShapeDtypeStruct(p.shape, p.dtype) for p in pieces],
                 aliases={}, n_sems=n_w * n_m, first=first, last=last)


def _sum_chips(own, landed, chip_core, *, name):
    n_chip, rows, hc = own.shape
    tr, tc = _tile(rows, hc)
    per_half = hc // tc

    def body(cc_ref, o_ref, l1_ref, l2_ref, l3_ref, out_ref):
        del cc_ref
        out_ref[...] = ((o_ref[...].astype(F32) + l1_ref[...].astype(F32))
                        + l2_ref[...].astype(F32)) + l3_ref[...].astype(F32)

    def slot(k):
        return pl.BlockSpec((None, tr, tc), lambda r, q, cc, k=k: ((cc[0] + k) % n_chip, r, q))

    grid_spec = pltpu.PrefetchScalarGridSpec(
        num_scalar_prefetch=1, grid=(rows // tr, per_half),
        in_specs=[slot(0), slot(1), slot(2), slot(3)],
        out_specs=pl.BlockSpec((tr, tc), lambda r, q, cc: (r, cc[1] * per_half + q)))
    return pl.pallas_call(
        body, grid_spec=grid_spec, out_shape=jax.ShapeDtypeStruct((rows, 2 * hc), F32),
        compiler_params=_params("arbitrary", "arbitrary"), name=name,
    )(chip_core, own, landed, landed, landed)


def _join_comm(halves):
    n_w = len(halves)

    def first(c_ins, c_outs, ss, rs):
        x, y, c = _my_place()
        for w in range(n_w):
            mine = _col_half(c_outs[w], c)
            _remote(mine, mine, ss, rs, w, (x, y, 1 - c)).start()

    def last(c_ins, c_outs, ss, rs):
        x, y, c = _my_place()
        for w in range(n_w):
            theirs = _col_half(c_outs[w], 1 - c)
            _remote(theirs, theirs, ss, rs, w, (x, y, 1 - c)).wait()

    return _Comm(ins=halves, outs=[jax.ShapeDtypeStruct(h.shape, h.dtype) for h in halves],
                 aliases={w: w for w in range(n_w)}, n_sems=n_w, first=first, last=last)


def _standalone(comm, *, name):
    def body(o_ref):
        o_ref[...] = jnp.zeros_like(o_ref)

    return _call(body, grid=(1,), in_specs=[],
                 out_specs=[pl.BlockSpec((SUBLANES, LANES), lambda i: (0, 0))],
                 out_shape=[jax.ShapeDtypeStruct((SUBLANES, LANES), F32)], operands=(), name=name,
                 comm=comm)[1:]


def _pack(pieces):
    flat, spans, off = [], [], 0
    for p in pieces:
        v = p.reshape(-1).astype(F32)
        pad = (-v.shape[0]) % LANES
        if pad:
            v = jnp.concatenate([v, jnp.zeros((pad,), F32)])
        spans.append((off, p.size))
        off += v.shape[0]
        flat.append(v)
    tail = (-off) % (SUBLANES * LANES)
    if tail:
        flat.append(jnp.zeros((tail,), F32))
    return jnp.concatenate(flat).reshape(-1, LANES), spans


def _unpack(buf, span, shape):
    off, size = span
    return buf.reshape(-1)[off:off + size].reshape(shape)


def kernel(x, norm1_g, w_in, w_gate_up, b_gate, conv_w, conv_norm_g, gla_norm_g, w_out, norm2_g, w_ff1, w_ff2, norm_f_g, loss_target, m_norm1_g, m_w_in, m_w_gate_up, m_b_gate, m_conv_w, m_conv_norm_g, m_gla_norm_g, m_w_out, m_norm2_g, m_w_ff1, m_w_ff2, m_norm_f_g, v_norm1_g, v_w_in, v_w_gate_up, v_b_gate, v_conv_w, v_conv_norm_g, v_gla_norm_g, v_w_out, v_norm2_g, v_w_ff1, v_w_ff2, v_norm_f_g):
    xs = x[0]
    target = loss_target[0]
    s_len, d = xs.shape
    d_in = w_in.shape[2] * N_CHIPS
    d_main = d_in - GATE_RANK
    d_ff = w_ff1.shape[2] * N_CHIPS
    cx, cy, cc = _my_place()
    chip = 2 * cx + cy
    chip_core = jnp.stack([chip, cc]).astype(jnp.int32)
    n_ff = d_ff // N_CHIPS
    norm_f = norm_f_g.reshape(1, d)

    n_sh = d_in // N_CHIPS
    wi_buf = _cast_into(w_in[0].T, chip_core, name="cast_w_in")
    wo_buf = _cast_into(w_out[0], chip_core, name="cast_w_out")
    w1_buf = _cast_into(w_ff1[0], chip_core, name="cast_w_ff1")
    w2_buf = _cast_into(w_ff2[0], chip_core, name="cast_w_ff2")

    small_w, spans_w = _pack([w_gate_up[0], conv_w[0]])
    small_all = _allgather_small(small_w, reduce=False, name="gather_small_weights")
    chips_first = [small_all[2 * j] for j in range(N_CHIPS)]
    wgu_full = jnp.concatenate(
        [_unpack(b, spans_w[0], w_gate_up.shape[1:]) for b in chips_first], axis=1)
    convw_full = jnp.concatenate(
        [_unpack(b, spans_w[1], conv_w.shape[1:]) for b in chips_first], axis=0)
    wgu_pad = jnp.concatenate(
        [wgu_full, jnp.zeros((LANES - GATE_RANK, wgu_full.shape[1]), F32)], axis=0).astype(BF16)
    convw_t = convw_full.T

    u, wi_buf = _rms_fwd(xs, norm1_g, name="norm1_fwd", comm=_gather_comm([wi_buf]))
    wi_t = wi_buf.reshape(d_in, d)
    wg_t = jnp.concatenate([wi_t[d_main:], jnp.zeros((LANES - GATE_RANK, d), BF16)], axis=0)
    z, wo_buf, w1_buf = _matmul(u, wi_t, tb=True, tm=2048, tn=1024, tk=d, out_dtypes=[F32], n_dim=d_main,
                                name="in_proj",
                                comm=_gather_comm([wo_buf, w1_buf], rows=[None, (0, d // 2)]))
    wo_full = wo_buf.reshape(d, d)
    (alow,) = _matmul(u, wg_t, tb=True, tm=1024, tn=LANES, tk=d, out_dtypes=[F32], name="in_proj_gate")
    y0 = _conv_fwd(z, convw_t, conv_norm_g)
    y, o, la, st, w1_cm = _gla_fwd(z, alow, wgu_pad, b_gate, gla_norm_g, y0,
                                   comm=_gather_comm([w1_buf], rows=[(d // 2, d // 2)], mid_at=0.9))
    (x2,) = _matmul(y, wo_full, tm=1024, tn=1024, tk=d, out_dtypes=[F32], extras=(xs,),
                    epilogue=_add_epilogue, name="out_proj")
    (h,) = _rms_fwd(x2, norm2_g, name="norm2_fwd")
    a, p, w2_buf = _matmul(
        h, w1_cm, tm=1024, tn=1024, tk=d, out_dtypes=[F32, BF16], n_dim=d_ff,
        b_spec=pl.BlockSpec((None, d, 1024), lambda i, j, k: (j // 2, 0, j % 2)),
        epilogue=lambda acc: (acc, jnp.square(jnp.maximum(acc, 0.0))), name="ff1",
        comm=_gather_comm([w2_buf]))
    w2_full = w2_buf.reshape(d_ff, d)
    (x3,) = _matmul(p, w2_full, tm=1024, tn=1024, tk=2048, out_dtypes=[F32], extras=(x2,),
                    epilogue=_add_epilogue, name="ff2")
    dx3b, g_normf, loss_part = _loss_head(x3, norm_f, target)

    (da,) = _matmul(dx3b, w2_full, tb=True, tm=1024, tn=1024, tk=d, out_dtypes=[BF16], extras=(a,),
                    epilogue=lambda acc, av: (acc * (2.0 * jnp.maximum(av, 0.0)),), name="ff2_dx")
    (dh,) = _matmul(
        da, w1_cm, tb=True, tm=2048, tn=1024, tk=2048, out_dtypes=[BF16], n_dim=d,
        b_spec=pl.BlockSpec((None, 1024, 2048), lambda i, j, k: (k, j, 0)), name="ff1_dx")
    dw_k = s_len // 1024
    g_w2, dx2, g_norm2 = _matmul(
        p, dx3b, ta=True, tm=1024, tn=d, tk=1024, out_dtypes=[BF16], name="ff2_dw",
        side=_rms_bwd_side(dh, x2, norm2_g, dx3b, n_steps=(d_ff // 1024) * dw_k,
                           block_of_step=lambda i, j, k: i * dw_k + k))
    (g_w1,) = _matmul(
        h, da, ta=True, tm=1024, tn=n_ff, tk=2048, out_dtypes=[BF16], name="ff1_dw",
        out_shapes=[jax.ShapeDtypeStruct((N_CHIPS, d, n_ff), BF16)],
        out_specs=[pl.BlockSpec((None, 1024, n_ff), lambda i, j, k: (j, i, 0))])
    g_w2 = g_w2.reshape(N_CHIPS, n_ff, d)
    dy, t_w1, t_w2 = _matmul(dx2, wo_full, tb=True, tm=1024, tn=1024, tk=d, out_dtypes=[BF16],
                             name="out_proj_dx", comm=_swap_comm([g_w1, g_w2]))
    ow_k = s_len // 512
    g_wo, p_w1, p_w2 = _matmul(
        y, dx2, ta=True, tm=1024, tn=d, tk=512, out_dtypes=[BF16], name="out_proj_dw",
        side=_add_own_half_side([g_w1, g_w2], [t_w1, t_w2], n_steps=(d // 1024) * ow_k,
                                step_of=lambda i, j, k: i * ow_k + k))
    g_wo = g_wo.reshape(N_CHIPS, d // N_CHIPS, d)
    conv_tiles = s_len // 512
    dz0, g_convw_t, g_convg = _run_side(
        _conv_bwd_side(dy, z, convw_t, conv_norm_g, n_steps=conv_tiles, step_of=lambda i, j, k: i),
        n_steps=conv_tiles, name="conv_bwd")
    dz, dalow, g_wgu_pad, g_bg, g_gg, l_w2 = _gla_bwd(dy, z, o, la, st, alow, wgu_pad, gla_norm_g, dz0,
                                                      comm=_exchange_comm([p_w2]))
    m_w2 = _sum_chips(p_w2, l_w2, chip_core, name="reduce_w_ff2")
    g_wi_t, l_w1 = _matmul(dz, u, ta=True, tm=1024, tn=d, tk=2048, out_dtypes=[BF16], name="in_proj_dw",
                           out_shapes=[jax.ShapeDtypeStruct((d_in, d), BF16)],
                           comm=_exchange_comm([p_w1]))
    (g_wi_t,) = _matmul(dalow, u, ta=True, tm=LANES, tn=d, tk=1024, out_dtypes=[BF16],
                        epilogue=lambda acc: (acc[:GATE_RANK],), into=g_wi_t, name="in_proj_gate_dw",
                        out_shapes=[jax.ShapeDtypeStruct((d_in, d), BF16)],
                        out_specs=[pl.BlockSpec((GATE_RANK, d), lambda i, j, k: (d_main // GATE_RANK, 0))])
    m_w1 = _sum_chips(p_w1, l_w1, chip_core, name="reduce_w_ff1")
    g_wi = g_wi_t.reshape(N_CHIPS, n_sh, d)
    du_gate, t_wi, t_wo = _matmul(dalow, wg_t, tm=1024, tn=1024, tk=LANES, out_dtypes=[F32],
                                  name="in_proj_gate_dx", comm=_swap_comm([g_wi, g_wo]))
    p_wi = _add_own_half(g_wi, t_wi, chip_core, name="pre_reduce_w_in")
    p_wo = _add_own_half(g_wo, t_wo, chip_core, name="pre_reduce_w_out")
    du, l_wi, l_wo, m_w1, m_w2 = _matmul(
        dz, wi_t, tm=1024, tn=1024, tk=2048, out_dtypes=[BF16], n_dim=d, extras=(du_gate,),
        epilogue=_add_epilogue, name="in_proj_dx",
        comm=[_exchange_comm([p_wi, p_wo]), _join_comm([m_w1, m_w2])])
    m_wi = _sum_chips(p_wi, l_wi, chip_core, name="reduce_w_in")
    m_wo = _sum_chips(p_wo, l_wo, chip_core, name="reduce_w_out")
    grad_x, g_norm1 = _rms_bwd(du, xs, norm1_g, dx2, name="norm1_bwd")
    m_wi, m_wo = _standalone(_join_comm([m_wi, m_wo]), name="join_w_in_w_out")
    g_big = [m_wi, m_wo, m_w1, m_w2]

    small_g, spans_g = _pack([g_norm1, g_wgu_pad[:GATE_RANK], g_bg, g_convw_t, g_convg, g_gg, g_norm2,
                              g_normf, loss_part[:, :1]])
    tot = _allgather_small(small_g, reduce=True, name="reduce_small_grads")
    t_norm1 = _unpack(tot, spans_g[0], (1, d))
    t_wgu = _unpack(tot, spans_g[1], (GATE_RANK, HEADS * DK))
    t_bg = _unpack(tot, spans_g[2], (1, HEADS * DK))
    t_convw = _unpack(tot, spans_g[3], (3, D_CONV)).T
    t_convg = _unpack(tot, spans_g[4], (1, D_CONV))
    t_gg = _unpack(tot, spans_g[5], (1, DV))
    t_norm2 = _unpack(tot, spans_g[6], (1, d))
    t_normf = _unpack(tot, spans_g[7], (1, d))
    loss = _unpack(tot, spans_g[8], ())
    n_gu = w_gate_up.shape[2]
    n_cw = conv_w.shape[1]
    t_wgu = lax.dynamic_slice(t_wgu, (0, chip * n_gu), (GATE_RANK, n_gu))
    t_convw = lax.dynamic_slice(t_convw, (chip * n_cw, 0), (n_cw, 3))

    order = ["norm1_g", "w_in", "w_gate_up", "b_gate", "conv_w", "conv_norm_g", "gla_norm_g", "w_out",
             "norm2_g", "w_ff1", "w_ff2", "norm_f_g"]
    weights = dict(norm1_g=norm1_g, w_in=w_in, w_gate_up=w_gate_up, b_gate=b_gate, conv_w=conv_w,
                   conv_norm_g=conv_norm_g, gla_norm_g=gla_norm_g, w_out=w_out, norm2_g=norm2_g,
                   w_ff1=w_ff1, w_ff2=w_ff2, norm_f_g=norm_f_g)
    moms = dict(norm1_g=m_norm1_g, w_in=m_w_in, w_gate_up=m_w_gate_up, b_gate=m_b_gate, conv_w=m_conv_w,
                conv_norm_g=m_conv_norm_g, gla_norm_g=m_gla_norm_g, w_out=m_w_out, norm2_g=m_norm2_g,
                w_ff1=m_w_ff1, w_ff2=m_w_ff2, norm_f_g=m_norm_f_g)
    vels = dict(norm1_g=v_norm1_g, w_in=v_w_in, w_gate_up=v_w_gate_up, b_gate=v_b_gate, conv_w=v_conv_w,
                conv_norm_g=v_conv_norm_g, gla_norm_g=v_gla_norm_g, w_out=v_w_out, norm2_g=v_norm2_g,
                w_ff1=v_w_ff1, w_ff2=v_w_ff2, norm_f_g=v_norm_f_g)
    grads2d = dict(norm1_g=t_norm1, w_in=g_big[0], w_gate_up=t_wgu, b_gate=t_bg, conv_w=t_convw,
                   conv_norm_g=t_convg, gla_norm_g=t_gg, w_out=g_big[1], norm2_g=t_norm2,
                   w_ff1=g_big[2], w_ff2=g_big[3], norm_f_g=t_normf)
    out_g, out_d, out_m, out_v = [], [], [], []
    for nm in order:
        w = weights[nm]
        g2 = grads2d[nm]
        if nm == "w_in":
            to2d, back = (lambda t: t[0].T), (lambda t: t.T.reshape(w.shape))
        else:
            to2d, back = (lambda t: t.reshape(g2.shape)), (lambda t: t.reshape(w.shape))
        res = _adamw(to2d(w), g2, to2d(moms[nm]), to2d(vels[nm]), name="adamw_" + nm)
        for lst, r in zip((out_g, out_d, out_m, out_v), res):
            lst.append(back(r))
    return (loss, grad_x.reshape(x.shape), *out_g, *out_d, *out_m, *out_v)
```

```python
import functools

import jax
import jax.numpy as jnp
from jax import lax
from jax.experimental import pallas as pl
from jax.experimental.pallas import tpu as pltpu

F32 = jnp.float32
BF16 = jnp.bfloat16
MESH = pl.DeviceIdType.MESH

EPS = 1e-6
CHUNK = 64
HEADS = 4
DK = 128
DV = 256
D_CONV = 1024
GROUP = 128
GATE_RANK = 16
LANES = 128
SUBLANES = 8
N_CHIPS = 4
N_DEV = 8
CHIP_MASKS = ((1, 0), (0, 1), (1, 1))

ADAM_LR = 0.001
ADAM_B1 = 0.9
ADAM_B2 = 0.999
ADAM_EPS = 1e-08
ADAM_WD = 0.01
ADAM_STEP = 10

VMEM_LIMIT = 56 * 1024 * 1024


def _params(*sem):
    return pltpu.CompilerParams(dimension_semantics=tuple(sem), vmem_limit_bytes=VMEM_LIMIT)


def _rowsum8(v):
    r, c = v.shape
    return jnp.sum(v.reshape(r // SUBLANES, SUBLANES, c), axis=0)


def _tile(rows, cols):
    for cand in (256, 128, 64, 32, 16, 8):
        if rows % cand == 0 and rows > cand:
            return cand, cols
    if rows * cols * 4 > (2 << 20) and cols % 256 == 0:
        return rows, 256
    return rows, cols


class _Comm:
    def __init__(self, ins, outs, aliases, n_sems, first, last, mid=None, mid_at=0.75):
        self.ins = list(ins)
        self.outs = list(outs)
        self.aliases = dict(aliases)
        self.n_sems = n_sems
        self.first = first
        self.mid = mid
        self.mid_at = mid_at
        self.last = last


def _call(body, *, grid, in_specs, out_specs, out_shape, operands, name, scratch_shapes=(), sem=None,
          aliases=None, comm=None):
    aliases = dict(aliases or {})
    if comm is None:
        return pl.pallas_call(
            body, grid=grid, in_specs=list(in_specs), out_specs=list(out_specs), out_shape=list(out_shape),
            scratch_shapes=list(scratch_shapes), input_output_aliases=aliases,
            compiler_params=_params(*(sem or ("arbitrary",) * len(grid))), name=name)(*operands)
    comms = list(comm) if isinstance(comm, (list, tuple)) else [comm]
    n_in, n_out, n_scr = len(in_specs), len(out_specs), len(scratch_shapes)
    c_ins_all = [a for cm in comms for a in cm.ins]
    c_outs_all = [o for cm in comms for o in cm.outs]
    n_ci, n_co = len(c_ins_all), len(c_outs_all)

    def full_body(*refs):
        ins = refs[:n_in]
        o0 = n_in + n_ci
        outs = refs[o0:o0 + n_out]
        s0 = o0 + n_out + n_co
        scr = refs[s0:s0 + n_scr]
        sems = refs[s0 + n_scr:]
        parts, i_at, o_at = [], n_in, o0 + n_out
        for q, cm in enumerate(comms):
            parts.append((cm, refs[i_at:i_at + len(cm.ins)], refs[o_at:o_at + len(cm.outs)],
                          sems[2 * q], sems[2 * q + 1]))
            i_at += len(cm.ins)
            o_at += len(cm.outs)
        step = functools.reduce(lambda acc, ig: acc * ig[1] + pl.program_id(ig[0]), enumerate(grid), 0)
        n_steps = functools.reduce(lambda acc, g: acc * g, grid, 1)
        @pl.when(step == 0)
        def _():
            for cm, c_ins, c_outs, ss, rs in parts:
                cm.first(c_ins, c_outs, ss, rs)

        def mid_step(cm):
            ms = int(cm.mid_at * n_steps)
            return ms if 0 < ms < n_steps - 1 else None

        for cm, c_ins, c_outs, ss, rs in parts:
            if cm.mid is not None and mid_step(cm) is not None:
                @pl.when(step == mid_step(cm))
                def _(cm=cm, c_ins=c_ins, c_outs=c_outs, ss=ss, rs=rs):
                    cm.mid(c_ins, c_outs, ss, rs)

        body(*ins, *outs, *scr)

        @pl.when(step == n_steps - 1)
        def _():
            for cm, c_ins, c_outs, ss, rs in parts:
                if cm.mid is not None and mid_step(cm) is None:
                    cm.mid(c_ins, c_outs, ss, rs)
                cm.last(c_ins, c_outs, ss, rs)

    any_spec = pl.BlockSpec(memory_space=pl.ANY)
    i_at, o_at, sem_shapes = n_in, n_out, []
    for cm in comms:
        for i_in, i_out in cm.aliases.items():
            aliases[i_at + i_in] = o_at + i_out
        i_at += len(cm.ins)
        o_at += len(cm.outs)
        sem_shapes += [pltpu.SemaphoreType.DMA((cm.n_sems,)), pltpu.SemaphoreType.DMA((cm.n_sems,))]
    return pl.pallas_call(
        full_body, grid=grid,
        in_specs=list(in_specs) + [any_spec] * n_ci,
        out_specs=list(out_specs) + [any_spec] * n_co,
        out_shape=list(out_shape) + c_outs_all,
        scratch_shapes=list(scratch_shapes) + sem_shapes,
        input_output_aliases=aliases,
        compiler_params=pltpu.CompilerParams(dimension_semantics=("arbitrary",) * len(grid),
                                             vmem_limit_bytes=VMEM_LIMIT, has_side_effects=True),
        name=name)(*operands, *c_ins_all)


def _matmul(a, b, *, ta=False, tb=False, tm, tn, tk, out_dtypes, name, extras=(), epilogue=None,
            out_shapes=None, out_specs=None, b_spec=None, n_dim=None, into=None, side=None, comm=None):
    n_into = 0 if into is None else 1
    if ta:
        k_dim, m_dim = a.shape
    else:
        m_dim, k_dim = a.shape
    if n_dim is None:
        n_dim = b.shape[0] if tb else b.shape[1]
        assert (b.shape[1] if tb else b.shape[0]) == k_dim
    assert m_dim % tm == 0 and n_dim % tn == 0 and k_dim % tk == 0, (name, a.shape, b.shape)
    nk = k_dim // tk
    n_ex, n_out = len(extras), len(out_dtypes)
    dims = (((0 if ta else 1,), (1 if tb else 0,)), ((), ()))
    grid = (m_dim // tm, n_dim // tn, nk)
    s_ins, s_outs, s_scr = (len(side.ins), len(side.outs), len(side.scratch)) if side else (0, 0, 0)

    def body(*refs):
        a_ref, b_ref = refs[0], refs[1]
        ex_refs = refs[2:2 + n_ex]
        i0 = 2 + n_ex + n_into
        side_in = refs[i0:i0 + s_ins]
        o_refs = refs[i0 + s_ins:i0 + s_ins + n_out]
        side_out = refs[i0 + s_ins + n_out:i0 + s_ins + n_out + s_outs]
        side_scr = refs[len(refs) - s_scr:] if s_scr else ()
        if side is not None:
            step = (pl.program_id(0) * grid[1] + pl.program_id(1)) * grid[2] + pl.program_id(2)

            @pl.when(step == 0)
            def _():
                side.init(side_scr)

        def dot():
            if side is not None:
                side.body(step, side_in, side_out, side_scr)
            return lax.dot_general(a_ref[...].astype(BF16), b_ref[...].astype(BF16), dims,
                                   preferred_element_type=F32)

        def finish(acc):
            outs = epilogue(acc, *[e[...] for e in ex_refs]) if epilogue is not None else (acc,)
            for o_ref, o in zip(o_refs, outs):
                o_ref[...] = o.astype(o_ref.dtype)

        if nk == 1:
            finish(dot())
        else:
            acc_ref = refs[len(refs) - s_scr - 1]
            k = pl.program_id(2)

            @pl.when(k == 0)
            def _():
                acc_ref[...] = dot()

            @pl.when(jnp.logical_and(k > 0, k < nk - 1))
            def _():
                acc_ref[...] += dot()

            @pl.when(k == nk - 1)
            def _():
                finish(acc_ref[...] + dot())

    a_spec = (pl.BlockSpec((tk, tm), lambda i, j, k: (k, i)) if ta
              else pl.BlockSpec((tm, tk), lambda i, j, k: (i, k)))
    if b_spec is None:
        b_spec = (pl.BlockSpec((tn, tk), lambda i, j, k: (j, k)) if tb
                  else pl.BlockSpec((tk, tn), lambda i, j, k: (k, j)))
    io_spec = pl.BlockSpec((tm, tn), lambda i, j, k: (i, j))
    vec_spec = pl.BlockSpec((1, tn), lambda i, j, k: (0, j))
    if out_shapes is None:
        out_shapes = [jax.ShapeDtypeStruct((m_dim, n_dim), dt) for dt in out_dtypes]
    if out_specs is None:
        out_specs = [io_spec] * n_out
    return _call(
        body,
        grid=grid,
        in_specs=([a_spec, b_spec] + [vec_spec if e.shape[0] == 1 else io_spec for e in extras]
                  + [pl.BlockSpec(memory_space=pl.ANY)] * n_into
                  + (list(side.in_specs) if side else [])),
        out_specs=list(out_specs) + (list(side.out_specs) if side else []),
        out_shape=list(out_shapes) + (list(side.outs) if side else []),
        scratch_shapes=([pltpu.VMEM((tm, tn), F32)] if nk > 1 else []) + (list(side.scratch) if side else []),
        sem=("parallel", "parallel", "arbitrary") if side is None else None,
        aliases={2 + n_ex: 0} if n_into else None,
        operands=(a, b, *extras) + ((into,) if n_into else ()) + (tuple(side.ins) if side else ()),
        name=name, comm=comm)


def _add_epilogue(acc, r):
    return (acc + r,)


def _residual_norm_epilogue(acc, x, g):
    x2 = acc + x
    r = lax.rsqrt(jnp.mean(x2 * x2, axis=-1, keepdims=True) + EPS)
    return x2, x2 * r * g


def _rms_fwd(x, g, *, name, tm=512, comm=None):
    s_len, d = x.shape

    def body(x_ref, g_ref, o_ref):
        xv = x_ref[...]
        r = lax.rsqrt(jnp.mean(xv * xv, axis=-1, keepdims=True) + EPS)
        o_ref[...] = (xv * r * g_ref[...]).astype(o_ref.dtype)

    return _call(
        body, grid=(s_len // tm,),
        in_specs=[pl.BlockSpec((tm, d), lambda i: (i, 0)), pl.BlockSpec((1, d), lambda i: (0, 0))],
        out_specs=[pl.BlockSpec((tm, d), lambda i: (i, 0))],
        out_shape=[jax.ShapeDtypeStruct((s_len, d), BF16)],
        sem=("parallel",), operands=(x, g), name=name, comm=comm)


def _zero_refs(refs):
    for r in refs:
        r[...] = jnp.zeros_like(r)


def _rms_bwd_block(dn_ref, x_ref, g_ref, res_ref, dx_ref, dg_ref, acc_ref):
    xv = x_ref[...]
    dnv = dn_ref[...].astype(F32)
    r = lax.rsqrt(jnp.mean(xv * xv, axis=-1, keepdims=True) + EPS)
    xh = xv * r
    acc = acc_ref[...] + _rowsum8(dnv * xh)
    acc_ref[...] = acc
    dg_ref[...] = jnp.sum(acc, axis=0, keepdims=True)
    dxh = dnv * g_ref[...]
    dx_ref[...] = (r * (dxh - xh * jnp.mean(dxh * xh, axis=-1, keepdims=True))
                   + res_ref[...].astype(F32))


class _Side:
    def __init__(self, ins, in_specs, outs, out_specs, scratch, init, body):
        self.ins, self.in_specs, self.outs, self.out_specs = ins, in_specs, outs, out_specs
        self.scratch = scratch
        self.init = init
        self.body = body


def _rms_bwd_side(dn, x, g, res, *, block_of_step, n_steps):
    s_len, d = x.shape
    row = pl.BlockSpec((s_len // n_steps, d), lambda i, j, k: (block_of_step(i, j, k), 0))
    vec = pl.BlockSpec((1, d), lambda i, j, k: (0, 0))
    return _Side(
        ins=[dn, x, g, res], in_specs=[row, row, vec, row],
        outs=[jax.ShapeDtypeStruct((s_len, d), F32), jax.ShapeDtypeStruct((1, d), F32)],
        out_specs=[row, vec], scratch=[pltpu.VMEM((SUBLANES, d), F32)],
        init=_zero_refs, body=lambda step, ins, outs, scr: _rms_bwd_block(*ins, *outs, *scr))


def _rms_bwd(dn, x, g, res, *, name, tm=512, comm=None):
    s_len, d = x.shape
    n = s_len // tm

    def body(*refs):
        @pl.when(pl.program_id(0) == 0)
        def _():
            _zero_refs(refs[-1:])

        _rms_bwd_block(*refs)

    row = pl.BlockSpec((tm, d), lambda i: (i, 0))
    vec = pl.BlockSpec((1, d), lambda i: (0, 0))
    return _call(
        body, grid=(n,),
        in_specs=[row, row, vec, row],
        out_specs=[row, vec],
        out_shape=[jax.ShapeDtypeStruct((s_len, d), F32), jax.ShapeDtypeStruct((1, d), F32)],
        scratch_shapes=[pltpu.VMEM((SUBLANES, d), F32)],
        operands=(dn, x, g, res), name=name, comm=comm)


def _loss_head(x3, g, target, *, tm=512):
    s_len, d = x3.shape
    n = s_len // tm

    def body(x_ref, g_ref, t_ref, dxb_ref, dg_ref, loss_ref, accg_ref, accl_ref):
        i = pl.program_id(0)
        xv = x_ref[...]
        gv = g_ref[...]
        r = lax.rsqrt(jnp.mean(xv * xv, axis=-1, keepdims=True) + EPS)
        xh = xv * r
        err = xh * gv - t_ref[...]

        @pl.when(i == 0)
        def _():
            accg_ref[...] = jnp.zeros_like(accg_ref)
            accl_ref[...] = jnp.zeros_like(accl_ref)

        accl_ref[...] += _rowsum8(err * err)
        dn = err * (1.0 / d)
        accg_ref[...] += _rowsum8(dn * xh)
        dxh = dn * gv
        dx = r * (dxh - xh * jnp.mean(dxh * xh, axis=-1, keepdims=True))
        dxb_ref[...] = dx.astype(BF16)

        @pl.when(i == n - 1)
        def _():
            dg_ref[...] = jnp.sum(accg_ref[...], axis=0, keepdims=True)
            tot = jnp.sum(jnp.sum(accl_ref[...], axis=0, keepdims=True), axis=1, keepdims=True)
            loss_ref[...] = jnp.broadcast_to(tot * (0.5 / d), (1, LANES))

    row = pl.BlockSpec((tm, d), lambda i: (i, 0))
    vec = pl.BlockSpec((1, d), lambda i: (0, 0))
    return pl.pallas_call(
        body, grid=(n,),
        in_specs=[row, vec, row],
        out_specs=[row, vec, pl.BlockSpec((1, LANES), lambda i: (0, 0))],
        out_shape=[jax.ShapeDtypeStruct((s_len, d), BF16),
                   jax.ShapeDtypeStruct((1, d), F32), jax.ShapeDtypeStruct((1, LANES), F32)],
        scratch_shapes=[pltpu.VMEM((SUBLANES, d), F32), pltpu.VMEM((SUBLANES, d), F32)],
        compiler_params=_params("arbitrary"), name="loss_head",
    )(x3, g, target)


def _shift_down(v, k, rows_before, row):
    out = pltpu.roll(v, k, axis=0)
    for j in range(k):
        out = jnp.where(row == j, rows_before[j], out)
    return out


def _shift_up(v, k, rows_after, row):
    t = v.shape[0]
    out = pltpu.roll(v, t - k, axis=0)
    for j in range(k):
        out = jnp.where(row == t - k + j, rows_after[j], out)
    return out


def _conv_fwd(z, w_t, gain, *, ts=512):
    s_len = z.shape[0]
    n_grp = D_CONV // GROUP

    def body(cb_ref, cc_ref, ch_ref, w_ref, g_ref, y_ref, carry_ref):
        i = pl.program_id(0)

        @pl.when(i == 0)
        def _():
            carry_ref[...] = jnp.zeros_like(carry_ref)

        row = lax.broadcasted_iota(jnp.int32, (ts, GROUP), 0)
        for g in range(n_grp):
            sl = slice(g * GROUP, (g + 1) * GROUP)
            uu = cc_ref[:, sl] * ch_ref[:, sl]
            p2 = carry_ref[6:7, sl]
            p1 = carry_ref[7:8, sl]
            u1 = _shift_down(uu, 1, [p1], row)
            u2 = _shift_down(uu, 2, [p2, p1], row)
            conv = w_ref[0:1, sl] * u2 + w_ref[1:2, sl] * u1 + w_ref[2:3, sl] * uu
            y = cb_ref[:, sl] * conv
            carry_ref[:, sl] = uu[ts - SUBLANES:ts, :]
            rg = lax.rsqrt(jnp.mean(y * y, axis=-1, keepdims=True) + EPS)
            y_ref[:, sl] = (y * rg * g_ref[:, sl]).astype(BF16)

    def col(j):
        return pl.BlockSpec((ts, D_CONV), lambda i, j=j: (i, j))

    small = lambda r: pl.BlockSpec((r, D_CONV), lambda i: (0, 0))
    return pl.pallas_call(
        body, grid=(s_len // ts,),
        in_specs=[col(0), col(1), col(2), small(3), small(1)],
        out_specs=col(0),
        out_shape=jax.ShapeDtypeStruct((s_len, 2 * D_CONV), BF16),
        scratch_shapes=[pltpu.VMEM((SUBLANES, D_CONV), F32)],
        compiler_params=_params("arbitrary"), name="conv_fwd",
    )(z, z, z, w_t, gain)


def _conv_bwd_side(dy, z, w_t, gain, *, n_steps, step_of):
    s_len = z.shape[0]
    n = n_steps
    ts = s_len // n
    n_grp = D_CONV // GROUP
    halo_blocks = ts // SUBLANES

    def body(step, ins, outs, scr):
        dy_ref, cb_ref, cc_ref, ch_ref, hcc_ref, hch_ref, w_ref, g_ref = ins
        dz_ref, dw_ref, dg_ref = outs
        carry_ref, accw_ref, accg_ref = scr
        first_tile = (n - 1 - step) == 0
        row = lax.broadcasted_iota(jnp.int32, (ts, GROUP), 0)
        keep = jnp.where(first_tile, 0.0, 1.0)
        for g in range(n_grp):
            sl = slice(g * GROUP, (g + 1) * GROUP)
            cc = cc_ref[:, sl]
            ch = ch_ref[:, sl]
            cb = cb_ref[:, sl]
            uu = cc * ch
            p2 = hcc_ref[6:7, sl] * hch_ref[6:7, sl] * keep
            p1 = hcc_ref[7:8, sl] * hch_ref[7:8, sl] * keep
            u1 = _shift_down(uu, 1, [p1], row)
            u2 = _shift_down(uu, 2, [p2, p1], row)
            w0, w1, w2 = w_ref[0:1, sl], w_ref[1:2, sl], w_ref[2:3, sl]
            conv = w0 * u2 + w1 * u1 + w2 * uu
            y = cb * conv
            rg = lax.rsqrt(jnp.mean(y * y, axis=-1, keepdims=True) + EPS)
            yh = y * rg
            dyv = dy_ref[:, sl].astype(F32)
            accg_ref[:, sl] += _rowsum8(dyv * yh)
            dyn = dyv * g_ref[:, sl]
            dpre = rg * (dyn - yh * jnp.mean(dyn * yh, axis=-1, keepdims=True))
            dz_ref[:, sl] = (dpre * conv).astype(BF16)
            dconv = dpre * cb
            accw_ref[0:8, sl] += _rowsum8(dconv * u2)
            accw_ref[8:16, sl] += _rowsum8(dconv * u1)
            accw_ref[16:24, sl] += _rowsum8(dconv * uu)
            n0 = carry_ref[0:1, sl]
            n1 = carry_ref[1:2, sl]
            d1 = _shift_up(dconv, 1, [n0], row)
            d2 = _shift_up(dconv, 2, [n0, n1], row)
            duu = w2 * dconv + w1 * d1 + w0 * d2
            carry_ref[:, sl] = dconv[0:SUBLANES, :]
            dz_ref[:, D_CONV + g * GROUP:D_CONV + (g + 1) * GROUP] = (duu * ch).astype(BF16)
            dz_ref[:, 2 * D_CONV + g * GROUP:2 * D_CONV + (g + 1) * GROUP] = (duu * cc).astype(BF16)

        for k in range(3):
            dw_ref[k:k + 1, :] = jnp.sum(accw_ref[8 * k:8 * k + 8, :], axis=0, keepdims=True)
        dg_ref[...] = jnp.sum(accg_ref[...], axis=0, keepdims=True)

    def tile(i, j, k):
        return n - 1 - step_of(i, j, k)

    def col(c):
        return pl.BlockSpec((ts, D_CONV), lambda i, j, k, c=c: (tile(i, j, k), c))

    def halo(c):
        return pl.BlockSpec((SUBLANES, D_CONV),
                            lambda i, j, k, c=c: (jnp.maximum(tile(i, j, k) * halo_blocks - 1, 0), c))

    small = lambda r: pl.BlockSpec((r, D_CONV), lambda i, j, k: (0, 0))
    return _Side(
        ins=[dy, z, z, z, z, z, w_t, gain],
        in_specs=[col(0), col(0), col(1), col(2), halo(1), halo(2), small(3), small(1)],
        outs=[jax.ShapeDtypeStruct((s_len, 6 * D_CONV), BF16),
              jax.ShapeDtypeStruct((3, D_CONV), F32), jax.ShapeDtypeStruct((1, D_CONV), F32)],
        out_specs=[pl.BlockSpec((ts, 3 * D_CONV), lambda i, j, k: (tile(i, j, k), 0)), small(3), small(1)],
        scratch=[pltpu.VMEM((SUBLANES, D_CONV), F32), pltpu.VMEM((24, D_CONV), F32),
                 pltpu.VMEM((SUBLANES, D_CONV), F32)],
        init=_zero_refs, body=body)


def _split3(v):
    hi = v.astype(BF16)
    r1 = v - hi.astype(F32)
    mid = r1.astype(BF16)
    lo = (r1 - mid.astype(F32)).astype(BF16)
    return jnp.concatenate([hi, mid, lo], axis=1)


def _tri_sum(tri, v):
    w = v.shape[1]
    dd = jnp.dot(tri, _split3(v), preferred_element_type=F32)
    return dd[:, :w] + dd[:, w:2 * w] + dd[:, 2 * w:]


def _chunk_masks(ts):
    r = jnp.arange(ts)
    same = (r[:, None] // CHUNK) == (r[None, :] // CHUNK)
    later = jnp.logical_and(same, r[None, :] > r[:, None]).astype(BF16)
    earlier = jnp.logical_and(same, r[None, :] < r[:, None]).astype(BF16)
    chunk_of_row = jnp.arange(ts // CHUNK * SUBLANES)[:, None] // SUBLANES
    member = (chunk_of_row == (r[None, :] // CHUNK)).astype(BF16)
    return later, earlier, member


def _sigmoid(v):
    return 0.5 * jnp.tanh(0.5 * v) + 0.5


def _gla_fwd(z, alow, wgu, bg, gg, y_in, *, ts=512, comm=None):
    s_len = z.shape[0]
    nch = ts // CHUNK
    scale = DK ** -0.5

    tri_u, _, ind8 = _chunk_masks(ts)

    def body(q_ref, k_ref, v_ref, og_ref, al_ref, wgu_ref, bg_ref, gg_ref, tu_ref, ind_ref, yin_ref,
             y_ref, o_ref, la_ref, st_ref, state_ref, kd_ref, qs_ref, dec_ref):
        del yin_ref
        i = pl.program_id(0)

        @pl.when(i == 0)
        def _():
            state_ref[...] = jnp.zeros_like(state_ref)

        pre = jnp.dot(al_ref[...].astype(BF16), wgu_ref[...], preferred_element_type=F32) + bg_ref[...]
        la = (jnp.minimum(pre, 0.0) - jnp.log(1.0 + jnp.exp(-jnp.abs(pre)))) * (1.0 / 16.0)
        la_ref[...] = la
        kd_ref[...] = (k_ref[...] * jnp.exp(_tri_sum(tu_ref[...], la))).astype(BF16)
        qs_ref[...] = (q_ref[...] * scale).astype(BF16)
        dec_ref[...] = jnp.exp(_tri_sum(ind_ref[...], la))

        def chunk(cl, carry):
            rows = pl.ds(pl.multiple_of(cl * CHUNK, CHUNK), CHUNK)
            dec = dec_ref[pl.ds(pl.multiple_of(cl * SUBLANES, SUBLANES), 1), :]
            for h in range(HEADS):
                ks = slice(h * DK, (h + 1) * DK)
                vs = slice(h * DV, (h + 1) * DV)
                kv_t = lax.dot_general(v_ref[rows, vs].astype(BF16), kd_ref[rows, ks],
                                       (((0,), (0,)), ((), ())), preferred_element_type=F32)
                st = state_ref[h] * dec[:, ks] + kv_t
                state_ref[h] = st
                st_ref[cl, h] = st
                o_ref[rows, vs] = lax.dot_general(qs_ref[rows, ks], st.astype(BF16),
                                                  (((1,), (1,)), ((), ())), preferred_element_type=F32)
            return carry

        lax.fori_loop(0, nch, chunk, 0, unroll=2)

        ggv = gg_ref[...]
        for h in range(HEADS):
            vs = slice(h * DV, (h + 1) * DV)
            o_h = o_ref[:, vs]
            og_h = og_ref[:, vs]
            ro = lax.rsqrt(jnp.mean(o_h * o_h, axis=-1, keepdims=True) + EPS)
            y_ref[:, vs] = (o_h * ro * ggv * (og_h * _sigmoid(og_h))).astype(BF16)

    def zcol(width, j):
        return pl.BlockSpec((ts, width), lambda i, j=j: (i, j))

    full = lambda shape: pl.BlockSpec(shape, lambda i: tuple(0 for _ in shape))
    return _call(
        body, grid=(s_len // ts,),
        in_specs=[zcol(512, 6), zcol(512, 7), zcol(1024, 4), zcol(1024, 5), zcol(LANES, 0),
                  full((LANES, 512)), full((1, 512)), full((1, DV)), full(tri_u.shape), full(ind8.shape),
                  pl.BlockSpec(memory_space=pl.ANY)],
        out_specs=[zcol(1024, 1), zcol(1024, 0), zcol(512, 0),
                   pl.BlockSpec((nch, HEADS, DV, DK), lambda i: (i, 0, 0, 0))],
        out_shape=[jax.ShapeDtypeStruct((s_len, 2048), BF16), jax.ShapeDtypeStruct((s_len, 1024), F32),
                   jax.ShapeDtypeStruct((s_len, 512), F32),
                   jax.ShapeDtypeStruct((s_len // CHUNK, HEADS, DV, DK), F32)],
        scratch_shapes=[pltpu.VMEM((HEADS, DV, DK), F32), pltpu.VMEM((ts, 512), BF16),
                        pltpu.VMEM((ts, 512), BF16), pltpu.VMEM((nch * SUBLANES, 512), F32)],
        aliases={10: 0},
        operands=(z, z, z, z, alow, wgu, bg, gg, tri_u, ind8, y_in), name="gla_fwd", comm=comm)


def _gla_bwd(dy, z, o, la, st, alow, wgu, gg, dz_in, *, ts=512, comm=None):
    s_len = z.shape[0]
    n = s_len // ts
    nch = ts // CHUNK
    scale = DK ** -0.5

    tri_u, tri_l, ind8 = _chunk_masks(ts)

    def body(dy_ref, q_ref, k_ref, v_ref, og_ref, o_ref, la_ref, st_ref, stp_ref, al_ref, wgu_ref,
             gg_ref, tu_ref, tl_ref, ind_ref, dzin_ref, dz_ref, dal_ref, dwgu_ref, dbg_ref, dgg_ref,
             gt_ref, decn_ref, accw_ref, accb_ref, accg_ref, dla_ref,
             e_ref, kd_ref, kdb_ref, qs_ref, do_ref, dkd_ref, dec_ref, dbe_ref):
        del dzin_ref
        i = pl.program_id(0)
        first_tile = (n - 1 - i) == 0

        @pl.when(i == 0)
        def _():
            gt_ref[...] = jnp.zeros_like(gt_ref)
            decn_ref[...] = jnp.ones_like(decn_ref)
            accw_ref[...] = jnp.zeros_like(accw_ref)
            accb_ref[...] = jnp.zeros_like(accb_ref)
            accg_ref[...] = jnp.zeros_like(accg_ref)

        la = la_ref[...]
        e_dec = jnp.exp(_tri_sum(tu_ref[...], la))
        e_ref[...] = e_dec
        kd = k_ref[...] * e_dec
        kd_ref[...] = kd
        kdb_ref[...] = kd.astype(BF16)
        qs_ref[...] = (q_ref[...] * scale).astype(BF16)
        dec_ref[...] = jnp.exp(_tri_sum(ind_ref[...], la))
        ggv = gg_ref[...]
        for h in range(HEADS):
            vs = slice(h * DV, (h + 1) * DV)
            o_h = o_ref[:, vs]
            og_h = og_ref[:, vs]
            dy_h = dy_ref[:, vs].astype(F32)
            ro = lax.rsqrt(jnp.mean(o_h * o_h, axis=-1, keepdims=True) + EPS)
            oh = o_h * ro
            sig = _sigmoid(og_h)
            sil = og_h * sig
            accg_ref[...] += _rowsum8(dy_h * oh * sil)
            dz_ref[:, 2048 + h * DV:2048 + (h + 1) * DV] = (
                dy_h * oh * ggv * sig * (1.0 + og_h * (1.0 - sig))).astype(BF16)
            don = dy_h * ggv * sil
            do_ref[:, vs] = (ro * (don - oh * jnp.mean(don * oh, axis=-1, keepdims=True))).astype(BF16)
        keep = jnp.where(first_tile, 0.0, 1.0)

        def chunk(jrev, decn):
            cl = nch - 1 - jrev
            rows = pl.ds(pl.multiple_of(cl * CHUNK, CHUNK), CHUNK)
            one_row = pl.ds(pl.multiple_of(cl * SUBLANES, SUBLANES), 1)
            dec = dec_ref[one_row, :]
            has_prev = jnp.where(cl > 0, 1.0, 0.0)
            prev_idx = jnp.maximum(cl - 1, 0)
            for h in range(HEADS):
                ks = slice(h * DK, (h + 1) * DK)
                vs = slice(h * DV, (h + 1) * DV)
                dob = do_ref[rows, vs]
                s_c = st_ref[cl, h]
                dqs = jnp.dot(dob, s_c.astype(BF16), preferred_element_type=F32)
                dz_ref[rows, ks] = (dqs * scale).astype(BF16)
                gt = gt_ref[h] * decn[:, ks] + lax.dot_general(
                    dob, qs_ref[rows, ks], (((0,), (0,)), ((), ())), preferred_element_type=F32)
                gt_ref[h] = gt
                gb = gt.astype(BF16)
                dkd_ref[rows, ks] = jnp.dot(v_ref[rows, vs].astype(BF16), gb, preferred_element_type=F32)
                dz_ref[rows, 1024 + h * DV:1024 + (h + 1) * DV] = lax.dot_general(
                    kdb_ref[rows, ks], gb, (((1,), (1,)), ((), ())),
                    preferred_element_type=F32).astype(BF16)
                s_prev = has_prev * st_ref[prev_idx, h] + (1.0 - has_prev) * keep * stp_ref[0, h]
                dbe_ref[one_row, ks] = jnp.sum(gt * s_prev, axis=0, keepdims=True) * dec[:, ks]
            return dec

        decn_ref[0:1, :] = lax.fori_loop(0, nch, chunk, decn_ref[0:1, :], unroll=2)

        dkd = dkd_ref[...]
        dz_ref[:, 512:1024] = (dkd * e_ref[...]).astype(BF16)
        dla_ref[...] = _tri_sum(tl_ref[...], dkd * kd_ref[...])
        for c in range(nch):
            dla_ref[c * CHUNK:(c + 1) * CHUNK, :] += dbe_ref[c * SUBLANES:c * SUBLANES + 1, :]
        dpre = dla_ref[...] * (1.0 / 16.0) * (1.0 - jnp.exp(16.0 * la))
        accb_ref[...] += _rowsum8(dpre)
        dpb = dpre.astype(BF16)
        accw_ref[...] += lax.dot_general(al_ref[...].astype(BF16), dpb, (((0,), (0,)), ((), ())),
                                         preferred_element_type=F32)
        dal_ref[...] = lax.dot_general(dpb, wgu_ref[...], (((1,), (1,)), ((), ())),
                                       preferred_element_type=F32)

        @pl.when(i == n - 1)
        def _():
            dwgu_ref[...] = accw_ref[...]
            dbg_ref[...] = jnp.sum(accb_ref[...], axis=0, keepdims=True)
            dgg_ref[...] = jnp.sum(accg_ref[...], axis=0, keepdims=True)

    def zcol(width, j):
        return pl.BlockSpec((ts, width), lambda i, j=j: (n - 1 - i, j))

    full = lambda shape: pl.BlockSpec(shape, lambda i: tuple(0 for _ in shape))
    return _call(
        body, grid=(n,),
        in_specs=[zcol(1024, 1), zcol(512, 6), zcol(512, 7), zcol(1024, 4), zcol(1024, 5),
                  zcol(1024, 0), zcol(512, 0),
                  pl.BlockSpec((nch, HEADS, DV, DK), lambda i: (n - 1 - i, 0, 0, 0)),
                  pl.BlockSpec((1, HEADS, DV, DK),
                               lambda i: (jnp.maximum((n - 1 - i) * nch - 1, 0), 0, 0, 0)),
                  zcol(LANES, 0), full((LANES, 512)), full((1, DV)),
                  full(tri_u.shape), full(tri_l.shape), full(ind8.shape),
                  pl.BlockSpec(memory_space=pl.ANY)],
        out_specs=[zcol(3072, 1), zcol(LANES, 0), full((LANES, 512)), full((1, 512)), full((1, DV))],
        out_shape=[jax.ShapeDtypeStruct((s_len, 6144), BF16), jax.ShapeDtypeStruct((s_len, LANES), F32),
                   jax.ShapeDtypeStruct((LANES, 512), F32), jax.ShapeDtypeStruct((1, 512), F32),
                   jax.ShapeDtypeStruct((1, DV), F32)],
        scratch_shapes=[pltpu.VMEM((HEADS, DV, DK), F32), pltpu.VMEM((SUBLANES, 512), F32),
                        pltpu.VMEM((LANES, 512), F32), pltpu.VMEM((SUBLANES, 512), F32),
                        pltpu.VMEM((SUBLANES, DV), F32), pltpu.VMEM((ts, 512), F32),
                        pltpu.VMEM((ts, 512), F32), pltpu.VMEM((ts, 512), F32), pltpu.VMEM((ts, 512), BF16),
                        pltpu.VMEM((ts, 512), BF16), pltpu.VMEM((ts, 1024), BF16),
                        pltpu.VMEM((ts, 512), F32), pltpu.VMEM((nch * SUBLANES, 512), F32),
                        pltpu.VMEM((nch * SUBLANES, 512), F32)],
        aliases={15: 0},
        operands=(dy, z, z, z, z, o, la, st, st, alow, wgu, gg, tri_u, tri_l, ind8, dz_in),
        name="gla_bwd", comm=comm)


def _adamw(w, g, m, v, *, name):
    rows, cols = w.shape
    tr, tc = _tile(rows, cols)

    def body(w_ref, g_ref, m_ref, v_ref, go_ref, d_ref, nm_ref, nv_ref):
        gv = g_ref[...]
        go_ref[...] = gv
        m2 = ADAM_B1 * m_ref[...] + (1.0 - ADAM_B1) * gv
        v2 = ADAM_B2 * v_ref[...] + (1.0 - ADAM_B2) * jnp.square(gv)
        m_hat = m2 / (1.0 - ADAM_B1 ** ADAM_STEP)
        v_hat = v2 / (1.0 - ADAM_B2 ** ADAM_STEP)
        d_ref[...] = -ADAM_LR * (m_hat / (jnp.sqrt(v_hat) + ADAM_EPS) + ADAM_WD * w_ref[...])
        nm_ref[...] = m2
        nv_ref[...] = v2

    blk = pl.BlockSpec((tr, tc), lambda i, j: (i, j))
    shp = jax.ShapeDtypeStruct((rows, cols), F32)
    return pl.pallas_call(
        body, grid=(rows // tr, cols // tc), in_specs=[blk] * 4, out_specs=[blk] * 4, out_shape=[shp] * 4,
        compiler_params=_params("parallel", "parallel"), name=name,
    )(w, g, m, v)


def _my_place():
    return lax.axis_index("x"), lax.axis_index("y"), lax.axis_index("c")


def _flip(v, bit):
    return 1 - v if bit else v


def _allgather_small(buf, *, reduce, name):
    rows = buf.shape[0]

    def body(in_ref, out_ref, gat_ref, send_sems, recv_sems):
        x, y, c = _my_place()
        me = 4 * x + 2 * y + c
        gat_ref[me] = in_ref[...]
        copies = []
        for m in range(1, N_DEV):
            peer = (_flip(x, m & 4), _flip(y, m & 2), _flip(c, m & 1))
            cp = pltpu.make_async_remote_copy(
                src_ref=in_ref, dst_ref=gat_ref.at[me],
                send_sem=send_sems.at[m - 1], recv_sem=recv_sems.at[m - 1],
                device_id=peer, device_id_type=MESH)
            cp.start()
            copies.append(cp)
        for m in range(1, N_DEV):
            px, py, pc = _flip(x, m & 4), _flip(y, m & 2), _flip(c, m & 1)
            src_slot = gat_ref.at[4 * px + 2 * py + pc]
            pltpu.make_async_remote_copy(
                src_ref=src_slot, dst_ref=src_slot,
                send_sem=send_sems.at[m - 1], recv_sem=recv_sems.at[m - 1],
                device_id=(px, py, pc), device_id_type=MESH).wait_recv()
        for cp in copies:
            cp.wait_send()
        if reduce:
            tot = gat_ref[0]
            for d in range(1, N_DEV):
                tot = tot + gat_ref[d]
            out_ref[...] = tot
        else:
            out_ref[...] = gat_ref[...]

    out_shape = (rows, LANES) if reduce else (N_DEV, rows, LANES)
    return pl.pallas_call(
        body,
        in_specs=[pl.BlockSpec(memory_space=pltpu.VMEM)],
        out_specs=pl.BlockSpec(memory_space=pltpu.VMEM),
        out_shape=jax.ShapeDtypeStruct(out_shape, F32),
        scratch_shapes=[pltpu.VMEM((N_DEV, rows, LANES), F32),
                        pltpu.SemaphoreType.DMA((N_DEV - 1,)), pltpu.SemaphoreType.DMA((N_DEV - 1,))],
        compiler_params=pltpu.CompilerParams(has_side_effects=True),
        name=name,
    )(buf)


def _cast_into(shard, chip_core, *, name):
    rows, cols = shard.shape
    tr, tc = _tile(rows, cols)

    def body(cc_ref, s_ref, o_ref):
        del cc_ref
        o_ref[...] = s_ref[...].astype(BF16)

    grid_spec = pltpu.PrefetchScalarGridSpec(
        num_scalar_prefetch=1, grid=(rows // tr, cols // tc),
        in_specs=[pl.BlockSpec((tr, tc), lambda r, q, cc: (r, q))],
        out_specs=pl.BlockSpec((None, tr, tc), lambda r, q, cc: (cc[0], r, q)))
    return pl.pallas_call(
        body, grid_spec=grid_spec, out_shape=jax.ShapeDtypeStruct((N_CHIPS, rows, cols), BF16),
        compiler_params=_params("arbitrary", "arbitrary"), name=name,
    )(chip_core, shard)


def _remote(src, dst, send_sems, recv_sems, k, device):
    return pltpu.make_async_remote_copy(src_ref=src, dst_ref=dst, send_sem=send_sems.at[k],
                                        recv_sem=recv_sems.at[k], device_id=device, device_id_type=MESH)


def _col_half(ref, h, *lead, rows=None):
    hc = ref.shape[-1] // 2
    mid = (slice(None),) * (len(ref.shape) - 2 - len(lead))
    row_sel = slice(None) if rows is None else pl.ds(rows[0], rows[1])
    return ref.at[tuple(lead) + mid + (row_sel, pl.ds(h * hc, hc))]


def _gather_comm(bufs, rows=None, mid_at=0.75):
    n_w, n_m = len(bufs), len(CHIP_MASKS)
    rows = rows or [None] * n_w

    def first(c_ins, c_outs, ss, rs):
        x, y, c = _my_place()
        chip = 2 * x + y
        for w in range(n_w):
            mine = _col_half(c_outs[w], c, chip, rows=rows[w])
            for mi, (mx, my) in enumerate(CHIP_MASKS):
                _remote(mine, mine, ss, rs, w * n_m + mi, (_flip(x, mx), _flip(y, my), c)).start()

    def mid(c_ins, c_outs, ss, rs):
        x, y, c = _my_place()
        for w in range(n_w):
            for mi, (mx, my) in enumerate(CHIP_MASKS):
                k = w * n_m + mi
                px, py = _flip(x, mx), _flip(y, my)
                landed = _col_half(c_outs[w], c, 2 * px + py, rows=rows[w])
                _remote(landed, landed, ss, rs, k, (px, py, c)).wait_recv()
                _remote(landed, landed, ss, rs, n_w * n_m + k, (x, y, 1 - c)).start()

    def last(c_ins, c_outs, ss, rs):
        x, y, c = _my_place()
        chip = 2 * x + y
        for w in range(n_w):
            mine = _col_half(c_outs[w], c, chip, rows=rows[w])
            for mi, (mx, my) in enumerate(CHIP_MASKS):
                k = w * n_m + mi
                px, py = _flip(x, mx), _flip(y, my)
                theirs = _col_half(c_outs[w], 1 - c, 2 * px + py, rows=rows[w])
                _remote(theirs, theirs, ss, rs, n_w * n_m + k, (x, y, 1 - c)).wait_recv()
                _remote(mine, mine, ss, rs, k, (px, py, c)).wait_send()
                _remote(mine, mine, ss, rs, n_w * n_m + k, (x, y, 1 - c)).wait_send()

    return _Comm(ins=bufs, outs=[jax.ShapeDtypeStruct(b.shape, b.dtype) for b in bufs],
                 aliases={w: w for w in range(n_w)}, n_sems=2 * n_w * n_m, first=first, mid=mid, last=last,
                 mid_at=mid_at)


def _swap_comm(grads):
    n_w = len(grads)

    def copy(c_ins, c_outs, ss, rs, w):
        x, y, c = _my_place()
        return _remote(_col_half(c_ins[w], 1 - c), c_outs[w], ss, rs, w, (x, y, 1 - c))

    def first(c_ins, c_outs, ss, rs):
        for w in range(n_w):
            copy(c_ins, c_outs, ss, rs, w).start()

    def last(c_ins, c_outs, ss, rs):
        for w in range(n_w):
            copy(c_ins, c_outs, ss, rs, w).wait()

    return _Comm(ins=grads,
                 outs=[jax.ShapeDtypeStruct(g.shape[:2] + (g.shape[2] // 2,), g.dtype) for g in grads],
                 aliases={}, n_sems=n_w, first=first, last=last)


def _add_own_half(g, other, chip_core, *, name):
    n_chip, rows, hc = other.shape
    tr, tc = _tile(rows, hc)
    per_half = hc // tc

    def body(cc_ref, g_ref, o_ref, out_ref):
        del cc_ref
        out_ref[...] = (g_ref[...].astype(F32) + o_ref[...].astype(F32)).astype(BF16)

    grid_spec = pltpu.PrefetchScalarGridSpec(
        num_scalar_prefetch=1, grid=(n_chip, rows // tr, per_half),
        in_specs=[pl.BlockSpec((None, tr, tc), lambda j, r, q, cc: (j, r, cc[1] * per_half + q)),
                  pl.BlockSpec((None, tr, tc), lambda j, r, q, cc: (j, r, q))],
        out_specs=pl.BlockSpec((None, tr, tc), lambda j, r, q, cc: (j, r, q)))
    return pl.pallas_call(
        body, grid_spec=grid_spec, out_shape=jax.ShapeDtypeStruct((n_chip, rows, hc), BF16),
        compiler_params=_params("parallel", "parallel", "parallel"), name=name,
    )(chip_core, g, other)


def _add_own_half_side(gs, others, *, n_steps, step_of):
    n_chip, rows, hc = others[0].shape
    per_chip = n_steps // n_chip
    tr = rows // per_chip
    assert all(o.shape == others[0].shape for o in others) and tr * per_chip == rows and tr % 16 == 0

    def body(step, ins, outs, scr):
        del step, scr
        for q, out_ref in enumerate(outs):
            out_ref[...] = (ins[2 * q][...].astype(F32) + ins[2 * q + 1][...].astype(F32)).astype(BF16)

    def blk(own_half):
        def index(i, j, k):
            s = step_of(i, j, k)
            return (s // per_chip, s % per_chip, lax.axis_index("c") if own_half else 0)
        return pl.BlockSpec((None, tr, hc), index)

    return _Side(
        ins=[a for pair in zip(gs, others) for a in pair], in_specs=[blk(True), blk(False)] * len(gs),
        outs=[jax.ShapeDtypeStruct(o.shape, BF16) for o in others], out_specs=[blk(False)] * len(gs),
        scratch=[], init=_zero_refs, body=body)


def _run_side(side, *, n_steps, name):
    n_in, n_out = len(side.ins), len(side.outs)

    def body(*refs):
        step = pl.program_id(0)

        @pl.when(step == 0)
        def _():
            side.init(refs[n_in + n_out:])

        side.body(step, refs[:n_in], refs[n_in:n_in + n_out], refs[n_in + n_out:])

    return _call(body, grid=(n_steps, 1, 1), in_specs=side.in_specs, out_specs=side.out_specs,
                 out_shape=side.outs, scratch_shapes=side.scratch, operands=tuple(side.ins), name=name)


def _exchange_comm(pieces):
    n_w, n_m = len(pieces), len(CHIP_MASKS)

    def copies(c_ins, c_outs, ss, rs):
        x, y, c = _my_place()
        chip = 2 * x + y
        for w in range(n_w):
            for mi, (mx, my) in enumerate(CHIP_MASKS):
                px, py = _flip(x, mx), _flip(y, my)
                send = _remote(c_ins[w].at[2 * px + py], c_outs[w].at[chip], ss, rs, w * n_m + mi, (px, py, c))
                landed = c_outs[w].at[2 * px + py]
                yield send, _remote(landed, landed, ss, rs, w * n_m + mi, (px, py, c))

    def first(c_ins, c_outs, ss, rs):
        for send, _ in copies(c_ins, c_outs, ss, rs):
            send.start()

    def last(c_ins, c_outs, ss, rs):
        for send, arrival in copies(c_ins, c_outs, ss, rs):
            arrival.wait_recv()
            send.wait_send()

    return _Comm(ins=pieces, outs=[jax.ShapeDtypeStruct(p.shape, p.dtype) for p in pieces],
                 aliases={}, n_sems=n_w * n_m, first=first, last=last)


def _sum_chips(own, landed, chip_core, *, name):
    n_chip, rows, hc = own.shape
    tr, tc = _tile(rows, hc)
    per_half = hc // tc

    def body(cc_ref, o_ref, l1_ref, l2_ref, l3_ref, out_ref):
        del cc_ref
        out_ref[...] = ((o_ref[...].astype(F32) + l1_ref[...].astype(F32))
                        + l2_ref[...].astype(F32)) + l3_ref[...].astype(F32)

    def slot(k):
        return pl.BlockSpec((None, tr, tc), lambda r, q, cc, k=k: ((cc[0] + k) % n_chip, r, q))

    grid_spec = pltpu.PrefetchScalarGridSpec(
        num_scalar_prefetch=1, grid=(rows // tr, per_half),
        in_specs=[slot(0), slot(1), slot(2), slot(3)],
        out_specs=pl.BlockSpec((tr, tc), lambda r, q, cc: (r, cc[1] * per_half + q)))
    return pl.pallas_call(
        body, grid_spec=grid_spec, out_shape=jax.ShapeDtypeStruct((rows, 2 * hc), F32),
        compiler_params=_params("arbitrary", "arbitrary"), name=name,
    )(chip_core, own, landed, landed, landed)


def _join_comm(halves):
    n_w = len(halves)

    def first(c_ins, c_outs, ss, rs):
        x, y, c = _my_place()
        for w in range(n_w):
            mine = _col_half(c_outs[w], c)
            _remote(mine, mine, ss, rs, w, (x, y, 1 - c)).start()

    def last(c_ins, c_outs, ss, rs):
        x, y, c = _my_place()
        for w in range(n_w):
            theirs = _col_half(c_outs[w], 1 - c)
            _remote(theirs, theirs, ss, rs, w, (x, y, 1 - c)).wait()

    return _Comm(ins=halves, outs=[jax.ShapeDtypeStruct(h.shape, h.dtype) for h in halves],
                 aliases={w: w for w in range(n_w)}, n_sems=n_w, first=first, last=last)


def _standalone(comm, *, name):
    def body(o_ref):
        o_ref[...] = jnp.zeros_like(o_ref)

    return _call(body, grid=(1,), in_specs=[],
                 out_specs=[pl.BlockSpec((SUBLANES, LANES), lambda i: (0, 0))],
                 out_shape=[jax.ShapeDtypeStruct((SUBLANES, LANES), F32)], operands=(), name=name,
                 comm=comm)[1:]


def _pack(pieces):
    flat, spans, off = [], [], 0
    for p in pieces:
        v = p.reshape(-1).astype(F32)
        pad = (-v.shape[0]) % LANES
        if pad:
            v = jnp.concatenate([v, jnp.zeros((pad,), F32)])
        spans.append((off, p.size))
        off += v.shape[0]
        flat.append(v)
    tail = (-off) % (SUBLANES * LANES)
    if tail:
        flat.append(jnp.zeros((tail,), F32))
    return jnp.concatenate(flat).reshape(-1, LANES), spans


def _unpack(buf, span, shape):
    off, size = span
    return buf.reshape(-1)[off:off + size].reshape(shape)


def kernel(x, norm1_g, w_in, w_gate_up, b_gate, conv_w, conv_norm_g, gla_norm_g, w_out, norm2_g, w_ff1, w_ff2, norm_f_g, loss_target, m_norm1_g, m_w_in, m_w_gate_up, m_b_gate, m_conv_w, m_conv_norm_g, m_gla_norm_g, m_w_out, m_norm2_g, m_w_ff1, m_w_ff2, m_norm_f_g, v_norm1_g, v_w_in, v_w_gate_up, v_b_gate, v_conv_w, v_conv_norm_g, v_gla_norm_g, v_w_out, v_norm2_g, v_w_ff1, v_w_ff2, v_norm_f_g):
    xs = x[0]
    target = loss_target[0]
    s_len, d = xs.shape
    d_in = w_in.shape[2] * N_CHIPS
    d_main = d_in - GATE_RANK
    d_ff = w_ff1.shape[2] * N_CHIPS
    cx, cy, cc = _my_place()
    chip = 2 * cx + cy
    chip_core = jnp.stack([chip, cc]).astype(jnp.int32)
    n_ff = d_ff // N_CHIPS
    norm_f = norm_f_g.reshape(1, d)

    n_sh = d_in // N_CHIPS
    wi_buf = _cast_into(w_in[0].T, chip_core, name="cast_w_in")
    wo_buf = _cast_into(w_out[0], chip_core, name="cast_w_out")
    w1_buf = _cast_into(w_ff1[0], chip_core, name="cast_w_ff1")
    w2_buf = _cast_into(w_ff2[0], chip_core, name="cast_w_ff2")

    small_w, spans_w = _pack([w_gate_up[0], conv_w[0]])
    small_all = _allgather_small(small_w, reduce=False, name="gather_small_weights")
    chips_first = [small_all[2 * j] for j in range(N_CHIPS)]
    wgu_full = jnp.concatenate(
        [_unpack(b, spans_w[0], w_gate_up.shape[1:]) for b in chips_first], axis=1)
    convw_full = jnp.concatenate(
        [_unpack(b, spans_w[1], conv_w.shape[1:]) for b in chips_first], axis=0)
    wgu_pad = jnp.concatenate(
        [wgu_full, jnp.zeros((LANES - GATE_RANK, wgu_full.shape[1]), F32)], axis=0).astype(BF16)
    convw_t = convw_full.T

    u, wi_buf = _rms_fwd(xs, norm1_g, name="norm1_fwd", comm=_gather_comm([wi_buf]))
    wi_t = wi_buf.reshape(d_in, d)
    wg_t = jnp.concatenate([wi_t[d_main:], jnp.zeros((LANES - GATE_RANK, d), BF16)], axis=0)
    z, wo_buf, w1_buf = _matmul(u, wi_t, tb=True, tm=2048, tn=1024, tk=d, out_dtypes=[F32], n_dim=d_main,
                                name="in_proj",
                                comm=_gather_comm([wo_buf, w1_buf], rows=[None, (0, d // 2)]))
    wo_full = wo_buf.reshape(d, d)
    (alow,) = _matmul(u, wg_t, tb=True, tm=1024, tn=LANES, tk=d, out_dtypes=[F32], name="in_proj_gate")
    y0 = _conv_fwd(z, convw_t, conv_norm_g)
    y, o, la, st, w1_cm = _gla_fwd(z, alow, wgu_pad, b_gate, gla_norm_g, y0,
                                   comm=_gather_comm([w1_buf], rows=[(d // 2, d // 2)], mid_at=0.9))
    x2, h = _matmul(y, wo_full, tm=512, tn=d, tk=d, out_dtypes=[F32, BF16], extras=(xs, norm2_g),
                    epilogue=_residual_norm_epilogue, name="out_proj")
    a, p, w2_buf = _matmul(
        h, w1_cm, tm=1024, tn=1024, tk=d, out_dtypes=[BF16, BF16], n_dim=d_ff,
        b_spec=pl.BlockSpec((None, d, 1024), lambda i, j, k: (j // 2, 0, j % 2)),
        epilogue=lambda acc: (acc, jnp.square(jnp.maximum(acc, 0.0))), name="ff1",
        comm=_gather_comm([w2_buf]))
    w2_full = w2_buf.reshape(d_ff, d)
    (x3,) = _matmul(p, w2_full, tm=1024, tn=1024, tk=2048, out_dtypes=[F32], extras=(x2,),
                    epilogue=_add_epilogue, name="ff2")
    dx3b, g_normf, loss_part = _loss_head(x3, norm_f, target)

    (da,) = _matmul(dx3b, w2_full, tb=True, tm=1024, tn=1024, tk=d, out_dtypes=[BF16], extras=(a,),
                    epilogue=lambda acc, av: (acc * (2.0 * jnp.maximum(av, 0.0)),), name="ff2_dx")
    (dh,) = _matmul(
        da, w1_cm, tb=True, tm=2048, tn=1024, tk=2048, out_dtypes=[BF16], n_dim=d,
        b_spec=pl.BlockSpec((None, 1024, 2048), lambda i, j, k: (k, j, 0)), name="ff1_dx")
    dw_k = s_len // 1024
    g_w2, dx2, g_norm2 = _matmul(
        p, dx3b, ta=True, tm=1024, tn=d, tk=1024, out_dtypes=[BF16], name="ff2_dw",
        side=_rms_bwd_side(dh, x2, norm2_g, dx3b, n_steps=(d_ff // 1024) * dw_k,
                           block_of_step=lambda i, j, k: i * dw_k + k))
    (g_w1,) = _matmul(
        h, da, ta=True, tm=1024, tn=n_ff, tk=2048, out_dtypes=[BF16], name="ff1_dw",
        out_shapes=[jax.ShapeDtypeStruct((N_CHIPS, d, n_ff), BF16)],
        out_specs=[pl.BlockSpec((None, 1024, n_ff), lambda i, j, k: (j, i, 0))])
    g_w2 = g_w2.reshape(N_CHIPS, n_ff, d)
    dy, t_w1, t_w2 = _matmul(dx2, wo_full, tb=True, tm=1024, tn=1024, tk=d, out_dtypes=[BF16],
                             name="out_proj_dx", comm=_swap_comm([g_w1, g_w2]))
    ow_k = s_len // 512
    g_wo, p_w1, p_w2 = _matmul(
        y, dx2, ta=True, tm=1024, tn=d, tk=512, out_dtypes=[BF16], name="out_proj_dw",
        side=_add_own_half_side([g_w1, g_w2], [t_w1, t_w2], n_steps=(d // 1024) * ow_k,
                                step_of=lambda i, j, k: i * ow_k + k))
    g_wo = g_wo.reshape(N_CHIPS, d // N_CHIPS, d)
    conv_tiles = s_len // 512
    dz0, g_convw_t, g_convg = _run_side(
        _conv_bwd_side(dy, z, convw_t, conv_norm_g, n_steps=conv_tiles, step_of=lambda i, j, k: i),
        n_steps=conv_tiles, name="conv_bwd")
    dz, dalow, g_wgu_pad, g_bg, g_gg, l_w2 = _gla_bwd(dy, z, o, la, st, alow, wgu_pad, gla_norm_g, dz0,
                                                      comm=_exchange_comm([p_w2]))
    m_w2 = _sum_chips(p_w2, l_w2, chip_core, name="reduce_w_ff2")
    g_wi_t, l_w1 = _matmul(dz, u, ta=True, tm=1024, tn=d, tk=2048, out_dtypes=[BF16], name="in_proj_dw",
                           out_shapes=[jax.ShapeDtypeStruct((d_in, d), BF16)],
                           comm=_exchange_comm([p_w1]))
    (g_wi_t,) = _matmul(dalow, u, ta=True, tm=LANES, tn=d, tk=1024, out_dtypes=[BF16],
                        epilogue=lambda acc: (acc[:GATE_RANK],), into=g_wi_t, name="in_proj_gate_dw",
                        out_shapes=[jax.ShapeDtypeStruct((d_in, d), BF16)],
                        out_specs=[pl.BlockSpec((GATE_RANK, d), lambda i, j, k: (d_main // GATE_RANK, 0))])
    m_w1 = _sum_chips(p_w1, l_w1, chip_core, name="reduce_w_ff1")
    g_wi = g_wi_t.reshape(N_CHIPS, n_sh, d)
    du_gate, t_wi, t_wo = _matmul(dalow, wg_t, tm=1024, tn=1024, tk=LANES, out_dtypes=[F32],
                                  name="in_proj_gate_dx", comm=_swap_comm([g_wi, g_wo]))
    p_wi = _add_own_half(g_wi, t_wi, chip_core, name="pre_reduce_w_in")
    p_wo = _add_own_half(g_wo, t_wo, chip_core, name="pre_reduce_w_out")
    du, l_wi, l_wo, m_w1, m_w2 = _matmul(
        dz, wi_t, tm=1024, tn=1024, tk=2048, out_dtypes=[BF16], n_dim=d, extras=(du_gate,),
        epilogue=_add_epilogue, name="in_proj_dx",
        comm=[_exchange_comm([p_wi, p_wo]), _join_comm([m_w1, m_w2])])
    m_wi = _sum_chips(p_wi, l_wi, chip_core, name="reduce_w_in")
    m_wo = _sum_chips(p_wo, l_wo, chip_core, name="reduce_w_out")
    grad_x, g_norm1 = _rms_bwd(du, xs, norm1_g, dx2, name="norm1_bwd")
    m_wi, m_wo = _standalone(_join_comm([m_wi, m_wo]), name="join_w_in_w_out")
    g_big = [m_wi, m_wo, m_w1, m_w2]

    small_g, spans_g = _pack([g_norm1, g_wgu_pad[:GATE_RANK], g_bg, g_convw_t, g_convg, g_gg, g_norm2,
                              g_normf, loss_part[:, :1]])
    tot = _allgather_small(small_g, reduce=True, name="reduce_small_grads")
    t_norm1 = _unpack(tot, spans_g[0], (1, d))
    t_wgu = _unpack(tot, spans_g[1], (GATE_RANK, HEADS * DK))
    t_bg = _unpack(tot, spans_g[2], (1, HEADS * DK))
    t_convw = _unpack(tot, spans_g[3], (3, D_CONV)).T
    t_convg = _unpack(tot, spans_g[4], (1, D_CONV))
    t_gg = _unpack(tot, spans_g[5], (1, DV))
    t_norm2 = _unpack(tot, spans_g[6], (1, d))
    t_normf = _unpack(tot, spans_g[7], (1, d))
    loss = _unpack(tot, spans_g[8], ())
    n_gu = w_gate_up.shape[2]
    n_cw = conv_w.shape[1]
    t_wgu = lax.dynamic_slice(t_wgu, (0, chip * n_gu), (GATE_RANK, n_gu))
    t_convw = lax.dynamic_slice(t_convw, (chip * n_cw, 0), (n_cw, 3))

    order = ["norm1_g", "w_in", "w_gate_up", "b_gate", "conv_w", "conv_norm_g", "gla_norm_g", "w_out",
             "norm2_g", "w_ff1", "w_ff2", "norm_f_g"]
    weights = dict(norm1_g=norm1_g, w_in=w_in, w_gate_up=w_gate_up, b_gate=b_gate, conv_w=conv_w,
                   conv_norm_g=conv_norm_g, gla_norm_g=gla_norm_g, w_out=w_out, norm2_g=norm2_g,
                   w_ff1=w_ff1, w_ff2=w_ff2, norm_f_g=norm_f_g)
    moms = dict(norm1_g=m_norm1_g, w_in=m_w_in, w_gate_up=m_w_gate_up, b_gate=m_b_gate, conv_w=m_conv_w,
                conv_norm_g=m_conv_norm_g, gla_norm_g=m_gla_norm_g, w_out=m_w_out, norm2_g=m_norm2_g,
                w_ff1=m_w_ff1, w_ff2=m_w_ff2, norm_f_g=m_norm_f_g)
    vels = dict(norm1_g=v_norm1_g, w_in=v_w_in, w_gate_up=v_w_gate_up, b_gate=v_b_gate, conv_w=v_conv_w,
                conv_norm_g=v_conv_norm_g, gla_norm_g=v_gla_norm_g, w_out=v_w_out, norm2_g=v_norm2_g,
                w_ff1=v_w_ff1, w_ff2=v_w_ff2, norm_f_g=v_norm_f_g)
    grads2d = dict(norm1_g=t_norm1, w_in=g_big[0], w_gate_up=t_wgu, b_gate=t_bg, conv_w=t_convw,
                   conv_norm_g=t_convg, gla_norm_g=t_gg, w_out=g_big[1], norm2_g=t_norm2,
                   w_ff1=g_big[2], w_ff2=g_big[3], norm_f_g=t_normf)
    out_g, out_d, out_m, out_v = [], [], [], []
    for nm in order:
        w = weights[nm]
        g2 = grads2d[nm]
        if nm == "w_in":
            to2d, back = (lambda t: t[0].T), (lambda t: t.T.reshape(w.shape))
        else:
            to2d, back = (lambda t: t.reshape(g2.shape)), (lambda t: t.reshape(w.shape))
        res = _adamw(to2d(w), g2, to2d(moms[nm]), to2d(vels[nm]), name="adamw_" + nm)
        for lst, r in zip((out_g, out_d, out_m, out_v), res):
            lst.append(back(r))
    return (loss, grad_x.reshape(x.shape), *out_g, *out_d, *out_m, *out_v)
```

```python
import functools

import jax
import jax.numpy as jnp
from jax import lax
from jax.experimental import pallas as pl
from jax.experimental.pallas import tpu as pltpu

F32 = jnp.float32
BF16 = jnp.bfloat16
MESH = pl.DeviceIdType.MESH

EPS = 1e-6
CHUNK = 64
HEADS = 4
DK = 128
DV = 256
D_CONV = 1024
GROUP = 128
GATE_RANK = 16
LANES = 128
SUBLANES = 8
N_CHIPS = 4
N_DEV = 8
CHIP_MASKS = ((1, 0), (0, 1), (1, 1))

ADAM_LR = 0.001
ADAM_B1 = 0.9
ADAM_B2 = 0.999
ADAM_EPS = 1e-08
ADAM_WD = 0.01
ADAM_STEP = 10

VMEM_LIMIT = 56 * 1024 * 1024


def _params(*sem):
    return pltpu.CompilerParams(dimension_semantics=tuple(sem), vmem_limit_bytes=VMEM_LIMIT)


def _rowsum8(v):
    r, c = v.shape
    return jnp.sum(v.reshape(r // SUBLANES, SUBLANES, c), axis=0)


def _tile(rows, cols):
    for cand in (256, 128, 64, 32, 16, 8):
        if rows % cand == 0 and rows > cand:
            return cand, cols
    if rows * cols * 4 > (2 << 20) and cols % 256 == 0:
        return rows, 256
    return rows, cols


class _Comm:
    def __init__(self, ins, outs, aliases, n_sems, first, last, mid=None, mid_at=0.75):
        self.ins = list(ins)
        self.outs = list(outs)
        self.aliases = dict(aliases)
        self.n_sems = n_sems
        self.first = first
        self.mid = mid
        self.mid_at = mid_at
        self.last = last


class _PlaceSpec:
    def __init__(self, block_shape, index_map):
        self.block_shape, self.index_map = block_shape, index_map


def _call(body, *, grid, in_specs, out_specs, out_shape, operands, name, scratch_shapes=(), sem=None,
          aliases=None, comm=None, place=None):
    aliases = dict(aliases or {})
    comms = [] if comm is None else (list(comm) if isinstance(comm, (list, tuple)) else [comm])
    n_in, n_out, n_scr = len(in_specs), len(out_specs), len(scratch_shapes)
    c_ins_all = [a for cm in comms for a in cm.ins]
    c_outs_all = [o for cm in comms for o in cm.outs]
    n_ci, n_co = len(c_ins_all), len(c_outs_all)
    n_place = 0 if place is None else 1

    def adapt(spec):
        if isinstance(spec, _PlaceSpec):
            return pl.BlockSpec(spec.block_shape, spec.index_map)
        if place is None or spec.index_map is None:
            return spec
        return pl.BlockSpec(spec.block_shape, lambda *a, f=spec.index_map: f(*a[:-1]))

    def full_body(*refs):
        refs = refs[n_place:]
        ins = refs[:n_in]
        o0 = n_in + n_ci
        outs = refs[o0:o0 + n_out]
        s0 = o0 + n_out + n_co
        scr = refs[s0:s0 + n_scr]
        sems = refs[s0 + n_scr:]
        parts, i_at, o_at = [], n_in, o0 + n_out
        for q, cm in enumerate(comms):
            parts.append((cm, refs[i_at:i_at + len(cm.ins)], refs[o_at:o_at + len(cm.outs)],
                          sems[2 * q], sems[2 * q + 1]))
            i_at += len(cm.ins)
            o_at += len(cm.outs)
        step = functools.reduce(lambda acc, ig: acc * ig[1] + pl.program_id(ig[0]), enumerate(grid), 0)
        n_steps = functools.reduce(lambda acc, g: acc * g, grid, 1)
        @pl.when(step == 0)
        def _():
            for cm, c_ins, c_outs, ss, rs in parts:
                cm.first(c_ins, c_outs, ss, rs)

        def mid_step(cm):
            ms = int(cm.mid_at * n_steps)
            return ms if 0 < ms < n_steps - 1 else None

        for cm, c_ins, c_outs, ss, rs in parts:
            if cm.mid is not None and mid_step(cm) is not None:
                @pl.when(step == mid_step(cm))
                def _(cm=cm, c_ins=c_ins, c_outs=c_outs, ss=ss, rs=rs):
                    cm.mid(c_ins, c_outs, ss, rs)

        body(*ins, *outs, *scr)

        @pl.when(step == n_steps - 1)
        def _():
            for cm, c_ins, c_outs, ss, rs in parts:
                if cm.mid is not None and mid_step(cm) is None:
                    cm.mid(c_ins, c_outs, ss, rs)
                cm.last(c_ins, c_outs, ss, rs)

    any_spec = pl.BlockSpec(memory_space=pl.ANY)
    i_at, o_at, sem_shapes = n_in, n_out, []
    for cm in comms:
        for i_in, i_out in cm.aliases.items():
            aliases[i_at + i_in] = o_at + i_out
        i_at += len(cm.ins)
        o_at += len(cm.outs)
        sem_shapes += [pltpu.SemaphoreType.DMA((cm.n_sems,)), pltpu.SemaphoreType.DMA((cm.n_sems,))]
    specs = dict(grid=grid, in_specs=[adapt(s) for s in in_specs] + [any_spec] * n_ci,
                 out_specs=[adapt(s) for s in out_specs] + [any_spec] * n_co,
                 scratch_shapes=list(scratch_shapes) + sem_shapes)
    if place is not None:
        specs = dict(grid_spec=pltpu.PrefetchScalarGridSpec(num_scalar_prefetch=1, **specs))
    if comms:
        params = pltpu.CompilerParams(dimension_semantics=("arbitrary",) * len(grid),
                                      vmem_limit_bytes=VMEM_LIMIT, has_side_effects=True)
    else:
        params = _params(*(sem or ("arbitrary",) * len(grid)))
    return pl.pallas_call(
        full_body, out_shape=list(out_shape) + c_outs_all,
        input_output_aliases={k + n_place: v for k, v in aliases.items()},
        compiler_params=params, name=name, **specs,
    )(*(() if place is None else (place,)), *operands, *c_ins_all)


def _matmul(a, b, *, ta=False, tb=False, tm, tn, tk, out_dtypes, name, extras=(), epilogue=None,
            out_shapes=None, out_specs=None, b_spec=None, n_dim=None, into=None, side=None, comm=None,
            place=None):
    n_into = 0 if into is None else 1
    if ta:
        k_dim, m_dim = a.shape
    else:
        m_dim, k_dim = a.shape
    if n_dim is None:
        n_dim = b.shape[0] if tb else b.shape[1]
        assert (b.shape[1] if tb else b.shape[0]) == k_dim
    assert m_dim % tm == 0 and n_dim % tn == 0 and k_dim % tk == 0, (name, a.shape, b.shape)
    nk = k_dim // tk
    n_ex, n_out = len(extras), len(out_dtypes)
    dims = (((0 if ta else 1,), (1 if tb else 0,)), ((), ()))
    grid = (m_dim // tm, n_dim // tn, nk)
    s_ins, s_outs, s_scr = (len(side.ins), len(side.outs), len(side.scratch)) if side else (0, 0, 0)

    def body(*refs):
        a_ref, b_ref = refs[0], refs[1]
        ex_refs = refs[2:2 + n_ex]
        i0 = 2 + n_ex + n_into
        side_in = refs[i0:i0 + s_ins]
        o_refs = refs[i0 + s_ins:i0 + s_ins + n_out]
        side_out = refs[i0 + s_ins + n_out:i0 + s_ins + n_out + s_outs]
        side_scr = refs[len(refs) - s_scr:] if s_scr else ()
        if side is not None:
            step = (pl.program_id(0) * grid[1] + pl.program_id(1)) * grid[2] + pl.program_id(2)

            @pl.when(step == 0)
            def _():
                side.init(side_scr)

        def dot():
            if side is not None:
                side.body(step, side_in, side_out, side_scr)
            return lax.dot_general(a_ref[...].astype(BF16), b_ref[...].astype(BF16), dims,
                                   preferred_element_type=F32)

        def finish(acc):
            outs = epilogue(acc, *[e[...] for e in ex_refs]) if epilogue is not None else (acc,)
            for o_ref, o in zip(o_refs, outs):
                o_ref[...] = o.astype(o_ref.dtype)

        if nk == 1:
            finish(dot())
        else:
            acc_ref = refs[len(refs) - s_scr - 1]
            k = pl.program_id(2)

            @pl.when(k == 0)
            def _():
                acc_ref[...] = dot()

            @pl.when(jnp.logical_and(k > 0, k < nk - 1))
            def _():
                acc_ref[...] += dot()

            @pl.when(k == nk - 1)
            def _():
                finish(acc_ref[...] + dot())

    a_spec = (pl.BlockSpec((tk, tm), lambda i, j, k: (k, i)) if ta
              else pl.BlockSpec((tm, tk), lambda i, j, k: (i, k)))
    if b_spec is None:
        b_spec = (pl.BlockSpec((tn, tk), lambda i, j, k: (j, k)) if tb
                  else pl.BlockSpec((tk, tn), lambda i, j, k: (k, j)))
    io_spec = pl.BlockSpec((tm, tn), lambda i, j, k: (i, j))
    vec_spec = pl.BlockSpec((1, tn), lambda i, j, k: (0, j))
    if out_shapes is None:
        out_shapes = [jax.ShapeDtypeStruct((m_dim, n_dim), dt) for dt in out_dtypes]
    if out_specs is None:
        out_specs = [io_spec] * n_out
    return _call(
        body,
        grid=grid,
        in_specs=([a_spec, b_spec] + [vec_spec if e.shape[0] == 1 else io_spec for e in extras]
                  + [pl.BlockSpec(memory_space=pl.ANY)] * n_into
                  + (list(side.in_specs) if side else [])),
        out_specs=list(out_specs) + (list(side.out_specs) if side else []),
        out_shape=list(out_shapes) + (list(side.outs) if side else []),
        scratch_shapes=([pltpu.VMEM((tm, tn), F32)] if nk > 1 else []) + (list(side.scratch) if side else []),
        sem=("parallel", "parallel", "arbitrary") if side is None else None,
        aliases={2 + n_ex: 0} if n_into else None,
        operands=(a, b, *extras) + ((into,) if n_into else ()) + (tuple(side.ins) if side else ()),
        name=name, comm=comm, place=place)


def _add_epilogue(acc, r):
    return (acc + r,)


def _residual_norm_epilogue(acc, x, g):
    x2 = acc + x
    r = lax.rsqrt(jnp.mean(x2 * x2, axis=-1, keepdims=True) + EPS)
    return x2, x2 * r * g


def _rms_fwd(x, g, *, name, tm=512, comm=None, side=None, place=None):
    s_len, d = x.shape
    s_ins, s_outs = (len(side.ins), len(side.outs)) if side else (0, 0)

    def body(*refs):
        x_ref, g_ref = refs[0], refs[1]
        o_ref = refs[2 + s_ins]
        xv = x_ref[...]
        r = lax.rsqrt(jnp.mean(xv * xv, axis=-1, keepdims=True) + EPS)
        o_ref[...] = (xv * r * g_ref[...]).astype(o_ref.dtype)
        if side is not None:
            scr = refs[3 + s_ins + s_outs:]

            @pl.when(pl.program_id(0) == 0)
            def _():
                side.init(scr)

            side.body(pl.program_id(0), refs[2:2 + s_ins], refs[3 + s_ins:3 + s_ins + s_outs], scr)

    row = pl.BlockSpec((tm, d), lambda i, j, k: (i, 0))
    return _call(
        body, grid=(s_len // tm, 1, 1),
        in_specs=[row, pl.BlockSpec((1, d), lambda i, j, k: (0, 0))] + (list(side.in_specs) if side else []),
        out_specs=[row] + (list(side.out_specs) if side else []),
        out_shape=[jax.ShapeDtypeStruct((s_len, d), BF16)] + (list(side.outs) if side else []),
        scratch_shapes=list(side.scratch) if side else [],
        operands=(x, g) + (tuple(side.ins) if side else ()), name=name, comm=comm, place=place)


NORM_ROWS = 16


def _zero_refs(refs):
    for r in refs:
        r[...] = jnp.zeros_like(r)


def _rms_bwd_block(dn_ref, x_ref, g_ref, res_ref, dx_ref, dg_ref, acc_ref):
    gv = g_ref[...]
    acc = acc_ref[...]
    for s in range(x_ref.shape[0] // NORM_ROWS):
        sl = slice(s * NORM_ROWS, (s + 1) * NORM_ROWS)
        xv = x_ref[sl, :]
        dnv = dn_ref[sl, :].astype(F32)
        r = lax.rsqrt(jnp.mean(xv * xv, axis=-1, keepdims=True) + EPS)
        xh = xv * r
        acc = acc + _rowsum8(dnv * xh)
        dxh = dnv * gv
        dx_ref[sl, :] = (r * (dxh - xh * jnp.mean(dxh * xh, axis=-1, keepdims=True))
                         + res_ref[sl, :].astype(F32))
    acc_ref[...] = acc
    dg_ref[...] = jnp.sum(acc, axis=0, keepdims=True)


class _Side:
    def __init__(self, ins, in_specs, outs, out_specs, scratch, init, body):
        self.ins, self.in_specs, self.outs, self.out_specs = ins, in_specs, outs, out_specs
        self.scratch = scratch
        self.init = init
        self.body = body


def _rms_bwd_side(dn, x, g, res, *, block_of_step, n_steps):
    s_len, d = x.shape
    row = pl.BlockSpec((s_len // n_steps, d), lambda i, j, k: (block_of_step(i, j, k), 0))
    vec = pl.BlockSpec((1, d), lambda i, j, k: (0, 0))
    return _Side(
        ins=[dn, x, g, res], in_specs=[row, row, vec, row],
        outs=[jax.ShapeDtypeStruct((s_len, d), F32), jax.ShapeDtypeStruct((1, d), F32)],
        out_specs=[row, vec], scratch=[pltpu.VMEM((SUBLANES, d), F32)],
        init=_zero_refs, body=lambda step, ins, outs, scr: _rms_bwd_block(*ins, *outs, *scr))


def _rms_bwd(dn, x, g, res, *, name, tm=512, comm=None):
    s_len, d = x.shape
    n = s_len // tm

    def body(*refs):
        @pl.when(pl.program_id(0) == 0)
        def _():
            _zero_refs(refs[-1:])

        _rms_bwd_block(*refs)

    row = pl.BlockSpec((tm, d), lambda i: (i, 0))
    vec = pl.BlockSpec((1, d), lambda i: (0, 0))
    return _call(
        body, grid=(n,),
        in_specs=[row, row, vec, row],
        out_specs=[row, vec],
        out_shape=[jax.ShapeDtypeStruct((s_len, d), F32), jax.ShapeDtypeStruct((1, d), F32)],
        scratch_shapes=[pltpu.VMEM((SUBLANES, d), F32)],
        operands=(dn, x, g, res), name=name, comm=comm)


def _loss_head(x3, g, target, *, tm=512):
    s_len, d = x3.shape
    n = s_len // tm

    def body(x_ref, g_ref, t_ref, dxb_ref, dg_ref, loss_ref, accg_ref, accl_ref):
        i = pl.program_id(0)
        xv = x_ref[...]
        gv = g_ref[...]
        r = lax.rsqrt(jnp.mean(xv * xv, axis=-1, keepdims=True) + EPS)
        xh = xv * r
        err = xh * gv - t_ref[...]

        @pl.when(i == 0)
        def _():
            accg_ref[...] = jnp.zeros_like(accg_ref)
            accl_ref[...] = jnp.zeros_like(accl_ref)

        accl_ref[...] += _rowsum8(err * err)
        dn = err * (1.0 / d)
        accg_ref[...] += _rowsum8(dn * xh)
        dxh = dn * gv
        dx = r * (dxh - xh * jnp.mean(dxh * xh, axis=-1, keepdims=True))
        dxb_ref[...] = dx.astype(BF16)

        @pl.when(i == n - 1)
        def _():
            dg_ref[...] = jnp.sum(accg_ref[...], axis=0, keepdims=True)
            tot = jnp.sum(jnp.sum(accl_ref[...], axis=0, keepdims=True), axis=1, keepdims=True)
            loss_ref[...] = jnp.broadcast_to(tot * (0.5 / d), (1, LANES))

    row = pl.BlockSpec((tm, d), lambda i: (i, 0))
    vec = pl.BlockSpec((1, d), lambda i: (0, 0))
    return pl.pallas_call(
        body, grid=(n,),
        in_specs=[row, vec, row],
        out_specs=[row, vec, pl.BlockSpec((1, LANES), lambda i: (0, 0))],
        out_shape=[jax.ShapeDtypeStruct((s_len, d), BF16),
                   jax.ShapeDtypeStruct((1, d), F32), jax.ShapeDtypeStruct((1, LANES), F32)],
        scratch_shapes=[pltpu.VMEM((SUBLANES, d), F32), pltpu.VMEM((SUBLANES, d), F32)],
        compiler_params=_params("arbitrary"), name="loss_head",
    )(x3, g, target)


def _shift_down(v, k, rows_before, row):
    out = pltpu.roll(v, k, axis=0)
    for j in range(k):
        out = jnp.where(row == j, rows_before[j], out)
    return out


def _shift_up(v, k, rows_after, row):
    t = v.shape[0]
    out = pltpu.roll(v, t - k, axis=0)
    for j in range(k):
        out = jnp.where(row == t - k + j, rows_after[j], out)
    return out


def _conv_fwd(z, w_t, gain, *, ts=512):
    s_len = z.shape[0]
    n_grp = D_CONV // GROUP

    def body(cb_ref, cc_ref, ch_ref, w_ref, g_ref, y_ref, carry_ref):
        i = pl.program_id(0)

        @pl.when(i == 0)
        def _():
            carry_ref[...] = jnp.zeros_like(carry_ref)

        row = lax.broadcasted_iota(jnp.int32, (ts, GROUP), 0)
        for g in range(n_grp):
            sl = slice(g * GROUP, (g + 1) * GROUP)
            uu = cc_ref[:, sl] * ch_ref[:, sl]
            p2 = carry_ref[6:7, sl]
            p1 = carry_ref[7:8, sl]
            u1 = _shift_down(uu, 1, [p1], row)
            u2 = _shift_down(uu, 2, [p2, p1], row)
            conv = w_ref[0:1, sl] * u2 + w_ref[1:2, sl] * u1 + w_ref[2:3, sl] * uu
            y = cb_ref[:, sl] * conv
            carry_ref[:, sl] = uu[ts - SUBLANES:ts, :]
            rg = lax.rsqrt(jnp.mean(y * y, axis=-1, keepdims=True) + EPS)
            y_ref[:, sl] = (y * rg * g_ref[:, sl]).astype(BF16)

    def col(j):
        return pl.BlockSpec((ts, D_CONV), lambda i, j=j: (i, j))

    small = lambda r: pl.BlockSpec((r, D_CONV), lambda i: (0, 0))
    return pl.pallas_call(
        body, grid=(s_len // ts,),
        in_specs=[col(0), col(1), col(2), small(3), small(1)],
        out_specs=col(0),
        out_shape=jax.ShapeDtypeStruct((s_len, 2 * D_CONV), BF16),
        scratch_shapes=[pltpu.VMEM((SUBLANES, D_CONV), F32)],
        compiler_params=_params("arbitrary"), name="conv_fwd",
    )(z, z, z, w_t, gain)


def _conv_bwd_side(dy, z, w_t, gain, *, n_steps, step_of):
    s_len = z.shape[0]
    n = n_steps
    ts = s_len // n
    n_grp = D_CONV // GROUP
    halo_blocks = ts // SUBLANES

    def body(step, ins, outs, scr):
        dy_ref, cb_ref, cc_ref, ch_ref, hcc_ref, hch_ref, w_ref, g_ref = ins
        dz_ref, dw_ref, dg_ref = outs
        carry_ref, accw_ref, accg_ref = scr
        first_tile = (n - 1 - step) == 0
        row = lax.broadcasted_iota(jnp.int32, (ts, GROUP), 0)
        keep = jnp.where(first_tile, 0.0, 1.0)
        for g in range(n_grp):
            sl = slice(g * GROUP, (g + 1) * GROUP)
            cc = cc_ref[:, sl]
            ch = ch_ref[:, sl]
            cb = cb_ref[:, sl]
            uu = cc * ch
            p2 = hcc_ref[6:7, sl] * hch_ref[6:7, sl] * keep
            p1 = hcc_ref[7:8, sl] * hch_ref[7:8, sl] * keep
            u1 = _shift_down(uu, 1, [p1], row)
            u2 = _shift_down(uu, 2, [p2, p1], row)
            w0, w1, w2 = w_ref[0:1, sl], w_ref[1:2, sl], w_ref[2:3, sl]
            conv = w0 * u2 + w1 * u1 + w2 * uu
            y = cb * conv
            rg = lax.rsqrt(jnp.mean(y * y, axis=-1, keepdims=True) + EPS)
            yh = y * rg
            dyv = dy_ref[:, sl].astype(F32)
            accg_ref[:, sl] += _rowsum8(dyv * yh)
            dyn = dyv * g_ref[:, sl]
            dpre = rg * (dyn - yh * jnp.mean(dyn * yh, axis=-1, keepdims=True))
            dz_ref[:, sl] = (dpre * conv).astype(BF16)
            dconv = dpre * cb
            accw_ref[0:8, sl] += _rowsum8(dconv * u2)
            accw_ref[8:16, sl] += _rowsum8(dconv * u1)
            accw_ref[16:24, sl] += _rowsum8(dconv * uu)
            n0 = carry_ref[0:1, sl]
            n1 = carry_ref[1:2, sl]
            d1 = _shift_up(dconv, 1, [n0], row)
            d2 = _shift_up(dconv, 2, [n0, n1], row)
            duu = w2 * dconv + w1 * d1 + w0 * d2
            carry_ref[:, sl] = dconv[0:SUBLANES, :]
            dz_ref[:, D_CONV + g * GROUP:D_CONV + (g + 1) * GROUP] = (duu * ch).astype(BF16)
            dz_ref[:, 2 * D_CONV + g * GROUP:2 * D_CONV + (g + 1) * GROUP] = (duu * cc).astype(BF16)

        for k in range(3):
            dw_ref[k:k + 1, :] = jnp.sum(accw_ref[8 * k:8 * k + 8, :], axis=0, keepdims=True)
        dg_ref[...] = jnp.sum(accg_ref[...], axis=0, keepdims=True)

    def tile(i, j, k):
        return n - 1 - step_of(i, j, k)

    def col(c):
        return pl.BlockSpec((ts, D_CONV), lambda i, j, k, c=c: (tile(i, j, k), c))

    def halo(c):
        return pl.BlockSpec((SUBLANES, D_CONV),
                            lambda i, j, k, c=c: (jnp.maximum(tile(i, j, k) * halo_blocks - 1, 0), c))

    small = lambda r: pl.BlockSpec((r, D_CONV), lambda i, j, k: (0, 0))
    return _Side(
        ins=[dy, z, z, z, z, z, w_t, gain],
        in_specs=[col(0), col(0), col(1), col(2), halo(1), halo(2), small(3), small(1)],
        outs=[jax.ShapeDtypeStruct((s_len, 6 * D_CONV), BF16),
              jax.ShapeDtypeStruct((3, D_CONV), F32), jax.ShapeDtypeStruct((1, D_CONV), F32)],
        out_specs=[pl.BlockSpec((ts, 3 * D_CONV), lambda i, j, k: (tile(i, j, k), 0)), small(3), small(1)],
        scratch=[pltpu.VMEM((SUBLANES, D_CONV), F32), pltpu.VMEM((24, D_CONV), F32),
                 pltpu.VMEM((SUBLANES, D_CONV), F32)],
        init=_zero_refs, body=body)


def _split3(v):
    hi = v.astype(BF16)
    r1 = v - hi.astype(F32)
    mid = r1.astype(BF16)
    lo = (r1 - mid.astype(F32)).astype(BF16)
    return jnp.concatenate([hi, mid, lo], axis=1)


def _tri_sum(tri, v):
    w = v.shape[1]
    dd = jnp.dot(tri, _split3(v), preferred_element_type=F32)
    return dd[:, :w] + dd[:, w:2 * w] + dd[:, 2 * w:]


def _chunk_masks(ts):
    r = jnp.arange(ts)
    same = (r[:, None] // CHUNK) == (r[None, :] // CHUNK)
    later = jnp.logical_and(same, r[None, :] > r[:, None]).astype(BF16)
    earlier = jnp.logical_and(same, r[None, :] < r[:, None]).astype(BF16)
    chunk_of_row = jnp.arange(ts // CHUNK * SUBLANES)[:, None] // SUBLANES
    member = (chunk_of_row == (r[None, :] // CHUNK)).astype(BF16)
    return later, earlier, member


def _sigmoid(v):
    return 0.5 * jnp.tanh(0.5 * v) + 0.5


def _gla_fwd(z, alow, wgu, bg, gg, y_in, *, ts=512, comm=None):
    s_len = z.shape[0]
    nch = ts // CHUNK
    scale = DK ** -0.5

    tri_u, _, ind8 = _chunk_masks(ts)

    def body(q_ref, k_ref, v_ref, og_ref, al_ref, wgu_ref, bg_ref, gg_ref, tu_ref, ind_ref, yin_ref,
             y_ref, o_ref, la_ref, st_ref, state_ref, kd_ref, qs_ref, dec_ref):
        del yin_ref
        i = pl.program_id(0)

        @pl.when(i == 0)
        def _():
            state_ref[...] = jnp.zeros_like(state_ref)

        pre = jnp.dot(al_ref[...].astype(BF16), wgu_ref[...], preferred_element_type=F32) + bg_ref[...]
        la = (jnp.minimum(pre, 0.0) - jnp.log(1.0 + jnp.exp(-jnp.abs(pre)))) * (1.0 / 16.0)
        la_ref[...] = la
        kd_ref[...] = (k_ref[...] * jnp.exp(_tri_sum(tu_ref[...], la))).astype(BF16)
        qs_ref[...] = (q_ref[...] * scale).astype(BF16)
        dec_ref[...] = jnp.exp(_tri_sum(ind_ref[...], la))

        def chunk(cl, carry):
            rows = pl.ds(pl.multiple_of(cl * CHUNK, CHUNK), CHUNK)
            dec = dec_ref[pl.ds(pl.multiple_of(cl * SUBLANES, SUBLANES), 1), :]
            for h in range(HEADS):
                ks = slice(h * DK, (h + 1) * DK)
                vs = slice(h * DV, (h + 1) * DV)
                kv_t = lax.dot_general(v_ref[rows, vs].astype(BF16), kd_ref[rows, ks],
                                       (((0,), (0,)), ((), ())), preferred_element_type=F32)
                st = state_ref[h] * dec[:, ks] + kv_t
                state_ref[h] = st
                st_ref[cl, h] = st
                o_ref[rows, vs] = lax.dot_general(qs_ref[rows, ks], st.astype(BF16),
                                                  (((1,), (1,)), ((), ())), preferred_element_type=F32)
            return carry

        lax.fori_loop(0, nch, chunk, 0, unroll=2)

        ggv = gg_ref[...]
        for h in range(HEADS):
            vs = slice(h * DV, (h + 1) * DV)
            o_h = o_ref[:, vs]
            og_h = og_ref[:, vs]
            ro = lax.rsqrt(jnp.mean(o_h * o_h, axis=-1, keepdims=True) + EPS)
            y_ref[:, vs] = (o_h * ro * ggv * (og_h * _sigmoid(og_h))).astype(BF16)

    def zcol(width, j):
        return pl.BlockSpec((ts, width), lambda i, j=j: (i, j))

    full = lambda shape: pl.BlockSpec(shape, lambda i: tuple(0 for _ in shape))
    return _call(
        body, grid=(s_len // ts,),
        in_specs=[zcol(512, 6), zcol(512, 7), zcol(1024, 4), zcol(1024, 5), zcol(LANES, 0),
                  full((LANES, 512)), full((1, 512)), full((1, DV)), full(tri_u.shape), full(ind8.shape),
                  pl.BlockSpec(memory_space=pl.ANY)],
        out_specs=[zcol(1024, 1), zcol(1024, 0), zcol(512, 0),
                   pl.BlockSpec((nch, HEADS, DV, DK), lambda i: (i, 0, 0, 0))],
        out_shape=[jax.ShapeDtypeStruct((s_len, 2048), BF16), jax.ShapeDtypeStruct((s_len, 1024), F32),
                   jax.ShapeDtypeStruct((s_len, 512), F32),
                   jax.ShapeDtypeStruct((s_len // CHUNK, HEADS, DV, DK), F32)],
        scratch_shapes=[pltpu.VMEM((HEADS, DV, DK), F32), pltpu.VMEM((ts, 512), BF16),
                        pltpu.VMEM((ts, 512), BF16), pltpu.VMEM((nch * SUBLANES, 512), F32)],
        aliases={10: 0},
        operands=(z, z, z, z, alow, wgu, bg, gg, tri_u, ind8, y_in), name="gla_fwd", comm=comm)


def _gla_bwd(dy, z, o, la, st, alow, wgu, gg, dz_in, *, ts=512, comm=None):
    s_len = z.shape[0]
    n = s_len // ts
    nch = ts // CHUNK
    scale = DK ** -0.5

    tri_u, tri_l, ind8 = _chunk_masks(ts)

    def body(dy_ref, q_ref, k_ref, v_ref, og_ref, o_ref, la_ref, st_ref, stp_ref, al_ref, wgu_ref,
             gg_ref, tu_ref, tl_ref, ind_ref, dzin_ref, dz_ref, dal_ref, dwgu_ref, dbg_ref, dgg_ref,
             gt_ref, decn_ref, accw_ref, accb_ref, accg_ref, dla_ref,
             e_ref, kd_ref, kdb_ref, qs_ref, do_ref, dkd_ref, dec_ref, dbe_ref):
        del dzin_ref
        i = pl.program_id(0)
        first_tile = (n - 1 - i) == 0

        @pl.when(i == 0)
        def _():
            gt_ref[...] = jnp.zeros_like(gt_ref)
            decn_ref[...] = jnp.ones_like(decn_ref)
            accw_ref[...] = jnp.zeros_like(accw_ref)
            accb_ref[...] = jnp.zeros_like(accb_ref)
            accg_ref[...] = jnp.zeros_like(accg_ref)

        la = la_ref[...]
        e_dec = jnp.exp(_tri_sum(tu_ref[...], la))
        e_ref[...] = e_dec
        kd = k_ref[...] * e_dec
        kd_ref[...] = kd
        kdb_ref[...] = kd.astype(BF16)
        qs_ref[...] = (q_ref[...] * scale).astype(BF16)
        dec_ref[...] = jnp.exp(_tri_sum(ind_ref[...], la))
        ggv = gg_ref[...]
        for h in range(HEADS):
            vs = slice(h * DV, (h + 1) * DV)
            o_h = o_ref[:, vs]
            og_h = og_ref[:, vs]
            dy_h = dy_ref[:, vs].astype(F32)
            ro = lax.rsqrt(jnp.mean(o_h * o_h, axis=-1, keepdims=True) + EPS)
            oh = o_h * ro
            sig = _sigmoid(og_h)
            sil = og_h * sig
            accg_ref[...] += _rowsum8(dy_h * oh * sil)
            dz_ref[:, 2048 + h * DV:2048 + (h + 1) * DV] = (
                dy_h * oh * ggv * sig * (1.0 + og_h * (1.0 - sig))).astype(BF16)
            don = dy_h * ggv * sil
            do_ref[:, vs] = (ro * (don - oh * jnp.mean(don * oh, axis=-1, keepdims=True))).astype(BF16)
        keep = jnp.where(first_tile, 0.0, 1.0)

        def chunk(jrev, decn):
            cl = nch - 1 - jrev
            rows = pl.ds(pl.multiple_of(cl * CHUNK, CHUNK), CHUNK)
            one_row = pl.ds(pl.multiple_of(cl * SUBLANES, SUBLANES), 1)
            dec = dec_ref[one_row, :]
            has_prev = jnp.where(cl > 0, 1.0, 0.0)
            prev_idx = jnp.maximum(cl - 1, 0)
            for h in range(HEADS):
                ks = slice(h * DK, (h + 1) * DK)
                vs = slice(h * DV, (h + 1) * DV)
                dob = do_ref[rows, vs]
                s_c = st_ref[cl, h]
                dqs = jnp.dot(dob, s_c.astype(BF16), preferred_element_type=F32)
                dz_ref[rows, ks] = (dqs * scale).astype(BF16)
                gt = gt_ref[h] * decn[:, ks] + lax.dot_general(
                    dob, qs_ref[rows, ks], (((0,), (0,)), ((), ())), preferred_element_type=F32)
                gt_ref[h] = gt
                gb = gt.astype(BF16)
                dkd_ref[rows, ks] = jnp.dot(v_ref[rows, vs].astype(BF16), gb, preferred_element_type=F32)
                dz_ref[rows, 1024 + h * DV:1024 + (h + 1) * DV] = lax.dot_general(
                    kdb_ref[rows, ks], gb, (((1,), (1,)), ((), ())),
                    preferred_element_type=F32).astype(BF16)
                s_prev = has_prev * st_ref[prev_idx, h] + (1.0 - has_prev) * keep * stp_ref[0, h]
                dbe_ref[one_row, ks] = jnp.sum(gt * s_prev, axis=0, keepdims=True) * dec[:, ks]
            return dec

        decn_ref[0:1, :] = lax.fori_loop(0, nch, chunk, decn_ref[0:1, :], unroll=2)

        dkd = dkd_ref[...]
        dz_ref[:, 512:1024] = (dkd * e_ref[...]).astype(BF16)
        dla_ref[...] = _tri_sum(tl_ref[...], dkd * kd_ref[...])
        for c in range(nch):
            dla_ref[c * CHUNK:(c + 1) * CHUNK, :] += dbe_ref[c * SUBLANES:c * SUBLANES + 1, :]
        dpre = dla_ref[...] * (1.0 / 16.0) * (1.0 - jnp.exp(16.0 * la))
        accb_ref[...] += _rowsum8(dpre)
        dpb = dpre.astype(BF16)
        accw_ref[...] += lax.dot_general(al_ref[...].astype(BF16), dpb, (((0,), (0,)), ((), ())),
                                         preferred_element_type=F32)
        dal_ref[...] = lax.dot_general(dpb, wgu_ref[...], (((1,), (1,)), ((), ())),
                                       preferred_element_type=F32)

        @pl.when(i == n - 1)
        def _():
            dwgu_ref[...] = accw_ref[...]
            dbg_ref[...] = jnp.sum(accb_ref[...], axis=0, keepdims=True)
            dgg_ref[...] = jnp.sum(accg_ref[...], axis=0, keepdims=True)

    def zcol(width, j):
        return pl.BlockSpec((ts, width), lambda i, j=j: (n - 1 - i, j))

    full = lambda shape: pl.BlockSpec(shape, lambda i: tuple(0 for _ in shape))
    return _call(
        body, grid=(n,),
        in_specs=[zcol(1024, 1), zcol(512, 6), zcol(512, 7), zcol(1024, 4), zcol(1024, 5),
                  zcol(1024, 0), zcol(512, 0),
                  pl.BlockSpec((nch, HEADS, DV, DK), lambda i: (n - 1 - i, 0, 0, 0)),
                  pl.BlockSpec((1, HEADS, DV, DK),
                               lambda i: (jnp.maximum((n - 1 - i) * nch - 1, 0), 0, 0, 0)),
                  zcol(LANES, 0), full((LANES, 512)), full((1, DV)),
                  full(tri_u.shape), full(tri_l.shape), full(ind8.shape),
                  pl.BlockSpec(memory_space=pl.ANY)],
        out_specs=[zcol(3072, 1), zcol(LANES, 0), full((LANES, 512)), full((1, 512)), full((1, DV))],
        out_shape=[jax.ShapeDtypeStruct((s_len, 6144), BF16), jax.ShapeDtypeStruct((s_len, LANES), F32),
                   jax.ShapeDtypeStruct((LANES, 512), F32), jax.ShapeDtypeStruct((1, 512), F32),
                   jax.ShapeDtypeStruct((1, DV), F32)],
        scratch_shapes=[pltpu.VMEM((HEADS, DV, DK), F32), pltpu.VMEM((SUBLANES, 512), F32),
                        pltpu.VMEM((LANES, 512), F32), pltpu.VMEM((SUBLANES, 512), F32),
                        pltpu.VMEM((SUBLANES, DV), F32), pltpu.VMEM((ts, 512), F32),
                        pltpu.VMEM((ts, 512), F32), pltpu.VMEM((ts, 512), F32), pltpu.VMEM((ts, 512), BF16),
                        pltpu.VMEM((ts, 512), BF16), pltpu.VMEM((ts, 1024), BF16),
                        pltpu.VMEM((ts, 512), F32), pltpu.VMEM((nch * SUBLANES, 512), F32),
                        pltpu.VMEM((nch * SUBLANES, 512), F32)],
        aliases={15: 0},
        operands=(dy, z, z, z, z, o, la, st, st, alow, wgu, gg, tri_u, tri_l, ind8, dz_in),
        name="gla_bwd", comm=comm)


def _adamw(w, g, m, v, *, name):
    rows, cols = w.shape
    tr, tc = _tile(rows, cols)

    def body(w_ref, g_ref, m_ref, v_ref, go_ref, d_ref, nm_ref, nv_ref):
        gv = g_ref[...]
        go_ref[...] = gv
        m2 = ADAM_B1 * m_ref[...] + (1.0 - ADAM_B1) * gv
        v2 = ADAM_B2 * v_ref[...] + (1.0 - ADAM_B2) * jnp.square(gv)
        m_hat = m2 / (1.0 - ADAM_B1 ** ADAM_STEP)
        v_hat = v2 / (1.0 - ADAM_B2 ** ADAM_STEP)
        d_ref[...] = -ADAM_LR * (m_hat / (jnp.sqrt(v_hat) + ADAM_EPS) + ADAM_WD * w_ref[...])
        nm_ref[...] = m2
        nv_ref[...] = v2

    blk = pl.BlockSpec((tr, tc), lambda i, j: (i, j))
    shp = jax.ShapeDtypeStruct((rows, cols), F32)
    return pl.pallas_call(
        body, grid=(rows // tr, cols // tc), in_specs=[blk] * 4, out_specs=[blk] * 4, out_shape=[shp] * 4,
        compiler_params=_params("parallel", "parallel"), name=name,
    )(w, g, m, v)


def _my_place():
    return lax.axis_index("x"), lax.axis_index("y"), lax.axis_index("c")


def _flip(v, bit):
    return 1 - v if bit else v


def _allgather_small(buf, *, reduce, name):
    rows = buf.shape[0]

    def body(in_ref, out_ref, gat_ref, send_sems, recv_sems):
        x, y, c = _my_place()
        me = 4 * x + 2 * y + c
        gat_ref[me] = in_ref[...]
        copies = []
        for m in range(1, N_DEV):
            peer = (_flip(x, m & 4), _flip(y, m & 2), _flip(c, m & 1))
            cp = pltpu.make_async_remote_copy(
                src_ref=in_ref, dst_ref=gat_ref.at[me],
                send_sem=send_sems.at[m - 1], recv_sem=recv_sems.at[m - 1],
                device_id=peer, device_id_type=MESH)
            cp.start()
            copies.append(cp)
        for m in range(1, N_DEV):
            px, py, pc = _flip(x, m & 4), _flip(y, m & 2), _flip(c, m & 1)
            src_slot = gat_ref.at[4 * px + 2 * py + pc]
            pltpu.make_async_remote_copy(
                src_ref=src_slot, dst_ref=src_slot,
                send_sem=send_sems.at[m - 1], recv_sem=recv_sems.at[m - 1],
                device_id=(px, py, pc), device_id_type=MESH).wait_recv()
        for cp in copies:
            cp.wait_send()
        if reduce:
            tot = gat_ref[0]
            for d in range(1, N_DEV):
                tot = tot + gat_ref[d]
            out_ref[...] = tot
        else:
            out_ref[...] = gat_ref[...]

    out_shape = (rows, LANES) if reduce else (N_DEV, rows, LANES)
    return pl.pallas_call(
        body,
        in_specs=[pl.BlockSpec(memory_space=pltpu.VMEM)],
        out_specs=pl.BlockSpec(memory_space=pltpu.VMEM),
        out_shape=jax.ShapeDtypeStruct(out_shape, F32),
        scratch_shapes=[pltpu.VMEM((N_DEV, rows, LANES), F32),
                        pltpu.SemaphoreType.DMA((N_DEV - 1,)), pltpu.SemaphoreType.DMA((N_DEV - 1,))],
        compiler_params=pltpu.CompilerParams(has_side_effects=True),
        name=name,
    )(buf)


def _cast_into(shard, chip_core, *, name):
    rows, cols = shard.shape
    tr, tc = _tile(rows, cols)

    def body(cc_ref, s_ref, o_ref):
        del cc_ref
        o_ref[...] = s_ref[...].astype(BF16)

    grid_spec = pltpu.PrefetchScalarGridSpec(
        num_scalar_prefetch=1, grid=(rows // tr, cols // tc),
        in_specs=[pl.BlockSpec((tr, tc), lambda r, q, cc: (r, q))],
        out_specs=pl.BlockSpec((None, tr, tc), lambda r, q, cc: (cc[0], r, q)))
    return pl.pallas_call(
        body, grid_spec=grid_spec, out_shape=jax.ShapeDtypeStruct((N_CHIPS, rows, cols), BF16),
        compiler_params=_params("arbitrary", "arbitrary"), name=name,
    )(chip_core, shard)


def _remote(src, dst, send_sems, recv_sems, k, device):
    return pltpu.make_async_remote_copy(src_ref=src, dst_ref=dst, send_sem=send_sems.at[k],
                                        recv_sem=recv_sems.at[k], device_id=device, device_id_type=MESH)


def _col_half(ref, h, *lead, rows=None):
    hc = ref.shape[-1] // 2
    mid = (slice(None),) * (len(ref.shape) - 2 - len(lead))
    row_sel = slice(None) if rows is None else pl.ds(rows[0], rows[1])
    return ref.at[tuple(lead) + mid + (row_sel, pl.ds(h * hc, hc))]


def _gather_comm(bufs, rows=None, mid_at=0.75):
    n_w, n_m = len(bufs), len(CHIP_MASKS)
    rows = rows or [None] * n_w

    def first(c_ins, c_outs, ss, rs):
        x, y, c = _my_place()
        chip = 2 * x + y
        for w in range(n_w):
            mine = _col_half(c_outs[w], c, chip, rows=rows[w])
            for mi, (mx, my) in enumerate(CHIP_MASKS):
                _remote(mine, mine, ss, rs, w * n_m + mi, (_flip(x, mx), _flip(y, my), c)).start()

    def mid(c_ins, c_outs, ss, rs):
        x, y, c = _my_place()
        for w in range(n_w):
            for mi, (mx, my) in enumerate(CHIP_MASKS):
                k = w * n_m + mi
                px, py = _flip(x, mx), _flip(y, my)
                landed = _col_half(c_outs[w], c, 2 * px + py, rows=rows[w])
                _remote(landed, landed, ss, rs, k, (px, py, c)).wait_recv()
                _remote(landed, landed, ss, rs, n_w * n_m + k, (x, y, 1 - c)).start()

    def last(c_ins, c_outs, ss, rs):
        x, y, c = _my_place()
        chip = 2 * x + y
        for w in range(n_w):
            mine = _col_half(c_outs[w], c, chip, rows=rows[w])
            for mi, (mx, my) in enumerate(CHIP_MASKS):
                k = w * n_m + mi
                px, py = _flip(x, mx), _flip(y, my)
                theirs = _col_half(c_outs[w], 1 - c, 2 * px + py, rows=rows[w])
                _remote(theirs, theirs, ss, rs, n_w * n_m + k, (x, y, 1 - c)).wait_recv()
                _remote(mine, mine, ss, rs, k, (px, py, c)).wait_send()
                _remote(mine, mine, ss, rs, n_w * n_m + k, (x, y, 1 - c)).wait_send()

    return _Comm(ins=bufs, outs=[jax.ShapeDtypeStruct(b.shape, b.dtype) for b in bufs],
                 aliases={w: w for w in range(n_w)}, n_sems=2 * n_w * n_m, first=first, mid=mid, last=last,
                 mid_at=mid_at)


def _swap_comm(grads):
    n_w = len(grads)

    def copy(c_ins, c_outs, ss, rs, w):
        x, y, c = _my_place()
        return _remote(_col_half(c_ins[w], 1 - c), c_outs[w], ss, rs, w, (x, y, 1 - c))

    def first(c_ins, c_outs, ss, rs):
        for w in range(n_w):
            copy(c_ins, c_outs, ss, rs, w).start()

    def last(c_ins, c_outs, ss, rs):
        for w in range(n_w):
            copy(c_ins, c_outs, ss, rs, w).wait()

    return _Comm(ins=grads,
                 outs=[jax.ShapeDtypeStruct(g.shape[:2] + (g.shape[2] // 2,), g.dtype) for g in grads],
                 aliases={}, n_sems=n_w, first=first, last=last)


def _add_own_half(g, other, chip_core, *, name):
    n_chip, rows, hc = other.shape
    tr, tc = _tile(rows, hc)
    per_half = hc // tc

    def body(cc_ref, g_ref, o_ref, out_ref):
        del cc_ref
        out_ref[...] = (g_ref[...].astype(F32) + o_ref[...].astype(F32)).astype(BF16)

    grid_spec = pltpu.PrefetchScalarGridSpec(
        num_scalar_prefetch=1, grid=(n_chip, rows // tr, per_half),
        in_specs=[pl.BlockSpec((None, tr, tc), lambda j, r, q, cc: (j, r, cc[1] * per_half + q)),
                  pl.BlockSpec((None, tr, tc), lambda j, r, q, cc: (j, r, q))],
        out_specs=pl.BlockSpec((None, tr, tc), lambda j, r, q, cc: (j, r, q)))
    return pl.pallas_call(
        body, grid_spec=grid_spec, out_shape=jax.ShapeDtypeStruct((n_chip, rows, hc), BF16),
        compiler_params=_params("parallel", "parallel", "parallel"), name=name,
    )(chip_core, g, other)


def _add_own_half_side(gs, others, *, n_steps, step_of):
    n_chip, rows, hc = others[0].shape
    per_chip = n_steps // n_chip
    tr = rows // per_chip
    assert all(o.shape == others[0].shape for o in others) and tr * per_chip == rows and tr % 16 == 0

    def body(step, ins, outs, scr):
        del step, scr
        for q, out_ref in enumerate(outs):
            out_ref[...] = (ins[2 * q][...].astype(F32) + ins[2 * q + 1][...].astype(F32)).astype(BF16)

    def blk(own_half):
        def index(i, j, k, place):
            s = step_of(i, j, k)
            return (s // per_chip, s % per_chip, place[1] if own_half else 0)
        return _PlaceSpec((None, tr, hc), index)

    return _Side(
        ins=[a for pair in zip(gs, others) for a in pair], in_specs=[blk(True), blk(False)] * len(gs),
        outs=[jax.ShapeDtypeStruct(o.shape, BF16) for o in others], out_specs=[blk(False)] * len(gs),
        scratch=[], init=_zero_refs, body=body)


def _cast_side(shards, *, n_steps, step_of):
    def body(step, ins, outs, scr):
        del step, scr
        for in_ref, out_ref in zip(ins, outs):
            out_ref[...] = in_ref[...].astype(BF16)

    def rows(s):
        assert s.shape[0] % (16 * n_steps) == 0, s.shape
        return s.shape[0] // n_steps

    return _Side(
        ins=list(shards),
        in_specs=[pl.BlockSpec((rows(s), s.shape[1]), lambda i, j, k: (step_of(i, j, k), 0)) for s in shards],
        outs=[jax.ShapeDtypeStruct((N_CHIPS,) + s.shape, BF16) for s in shards],
        out_specs=[_PlaceSpec((None, rows(s), s.shape[1]),
                              lambda i, j, k, place: (place[0], step_of(i, j, k), 0)) for s in shards],
        scratch=[], init=_zero_refs, body=body)


def _pow2_below(n_steps, most=16):
    return min(most, 1 << (n_steps.bit_length() - 1))


def _sum_chips_side(own, landed, *, n_blocks, step_of):
    n_chip, rows, hc = own.shape
    tr = rows // n_blocks
    assert tr * n_blocks == rows and tr % 16 == 0

    def body(step, ins, outs, scr):
        del step, scr
        o_ref, l1_ref, l2_ref, l3_ref = ins
        outs[0][...] = ((o_ref[...].astype(F32) + l1_ref[...].astype(F32))
                        + l2_ref[...].astype(F32)) + l3_ref[...].astype(F32)

    def block(i, j, k):
        return jnp.minimum(step_of(i, j, k), n_blocks - 1)

    def slot(q):
        return _PlaceSpec((None, tr, hc),
                          lambda i, j, k, place, q=q: ((place[0] + q) % n_chip, block(i, j, k), 0))

    return _Side(
        ins=[own, landed, landed, landed], in_specs=[slot(0), slot(1), slot(2), slot(3)],
        outs=[jax.ShapeDtypeStruct((rows, 2 * hc), F32)],
        out_specs=[_PlaceSpec((tr, hc), lambda i, j, k, place: (block(i, j, k), place[1]))],
        scratch=[], init=_zero_refs, body=body)


def _run_side(side, *, n_steps, name):
    n_in, n_out = len(side.ins), len(side.outs)

    def body(*refs):
        step = pl.program_id(0)

        @pl.when(step == 0)
        def _():
            side.init(refs[n_in + n_out:])

        side.body(step, refs[:n_in], refs[n_in:n_in + n_out], refs[n_in + n_out:])

    return _call(body, grid=(n_steps, 1, 1), in_specs=side.in_specs, out_specs=side.out_specs,
                 out_shape=side.outs, scratch_shapes=side.scratch, operands=tuple(side.ins), name=name)


def _exchange_comm(pieces):
    n_w, n_m = len(pieces), len(CHIP_MASKS)

    def copies(c_ins, c_outs, ss, rs):
        x, y, c = _my_place()
        chip = 2 * x + y
        for w in range(n_w):
            for mi, (mx, my) in enumerate(CHIP_MASKS):
                px, py = _flip(x, mx), _flip(y, my)
                send = _remote(c_ins[w].at[2 * px + py], c_outs[w].at[chip], ss, rs, w * n_m + mi, (px, py, c))
                landed = c_outs[w].at[2 * px + py]
                yield send, _remote(landed, landed, ss, rs, w * n_m + mi, (px, py, c))

    def first(c_ins, c_outs, ss, rs):
        for send, _ in copies(c_ins, c_outs, ss, rs):
            send.start()

    def last(c_ins, c_outs, ss, rs):
        for send, arrival in copies(c_ins, c_outs, ss, rs):
            arrival.wait_recv()
            send.wait_send()

    return _Comm(ins=pieces, outs=[jax.ShapeDtypeStruct(p.shape, p.dtype) for p in pieces],
                 aliases={}, n_sems=n_w * n_m, first=first, last=last)


def _sum_chips(own, landed, chip_core, *, name):
    n_chip, rows, hc = own.shape
    tr, tc = _tile(rows, hc)
    per_half = hc // tc

    def body(cc_ref, o_ref, l1_ref, l2_ref, l3_ref, out_ref):
        del cc_ref
        out_ref[...] = ((o_ref[...].astype(F32) + l1_ref[...].astype(F32))
                        + l2_ref[...].astype(F32)) + l3_ref[...].astype(F32)

    def slot(k):
        return pl.BlockSpec((None, tr, tc), lambda r, q, cc, k=k: ((cc[0] + k) % n_chip, r, q))

    grid_spec = pltpu.PrefetchScalarGridSpec(
        num_scalar_prefetch=1, grid=(rows // tr, per_half),
        in_specs=[slot(0), slot(1), slot(2), slot(3)],
        out_specs=pl.BlockSpec((tr, tc), lambda r, q, cc: (r, cc[1] * per_half + q)))
    return pl.pallas_call(
        body, grid_spec=grid_spec, out_shape=jax.ShapeDtypeStruct((rows, 2 * hc), F32),
        compiler_params=_params("arbitrary", "arbitrary"), name=name,
    )(chip_core, own, landed, landed, landed)


def _join_comm(halves):
    n_w = len(halves)

    def first(c_ins, c_outs, ss, rs):
        x, y, c = _my_place()
        for w in range(n_w):
            mine = _col_half(c_outs[w], c)
            _remote(mine, mine, ss, rs, w, (x, y, 1 - c)).start()

    def last(c_ins, c_outs, ss, rs):
        x, y, c = _my_place()
        for w in range(n_w):
            theirs = _col_half(c_outs[w], 1 - c)
            _remote(theirs, theirs, ss, rs, w, (x, y, 1 - c)).wait()

    return _Comm(ins=halves, outs=[jax.ShapeDtypeStruct(h.shape, h.dtype) for h in halves],
                 aliases={w: w for w in range(n_w)}, n_sems=n_w, first=first, last=last)


def _standalone(comm, *, name):
    def body(o_ref):
        o_ref[...] = jnp.zeros_like(o_ref)

    return _call(body, grid=(1,), in_specs=[],
                 out_specs=[pl.BlockSpec((SUBLANES, LANES), lambda i: (0, 0))],
                 out_shape=[jax.ShapeDtypeStruct((SUBLANES, LANES), F32)], operands=(), name=name,
                 comm=comm)[1:]


def _pack(pieces):
    flat, spans, off = [], [], 0
    for p in pieces:
        v = p.reshape(-1).astype(F32)
        pad = (-v.shape[0]) % LANES
        if pad:
            v = jnp.concatenate([v, jnp.zeros((pad,), F32)])
        spans.append((off, p.size))
        off += v.shape[0]
        flat.append(v)
    tail = (-off) % (SUBLANES * LANES)
    if tail:
        flat.append(jnp.zeros((tail,), F32))
    return jnp.concatenate(flat).reshape(-1, LANES), spans


def _unpack(buf, span, shape):
    off, size = span
    return buf.reshape(-1)[off:off + size].reshape(shape)


def kernel(x, norm1_g, w_in, w_gate_up, b_gate, conv_w, conv_norm_g, gla_norm_g, w_out, norm2_g, w_ff1, w_ff2, norm_f_g, loss_target, m_norm1_g, m_w_in, m_w_gate_up, m_b_gate, m_conv_w, m_conv_norm_g, m_gla_norm_g, m_w_out, m_norm2_g, m_w_ff1, m_w_ff2, m_norm_f_g, v_norm1_g, v_w_in, v_w_gate_up, v_b_gate, v_conv_w, v_conv_norm_g, v_gla_norm_g, v_w_out, v_norm2_g, v_w_ff1, v_w_ff2, v_norm_f_g):
    xs = x[0]
    target = loss_target[0]
    s_len, d = xs.shape
    d_in = w_in.shape[2] * N_CHIPS
    d_main = d_in - GATE_RANK
    d_ff = w_ff1.shape[2] * N_CHIPS
    cx, cy, cc = _my_place()
    chip = 2 * cx + cy
    chip_core = jnp.stack([chip, cc]).astype(jnp.int32)
    n_ff = d_ff // N_CHIPS
    norm_f = norm_f_g.reshape(1, d)

    n_sh = d_in // N_CHIPS
    wi_buf = _cast_into(w_in[0].T, chip_core, name="cast_w_in")

    small_w, spans_w = _pack([w_gate_up[0], conv_w[0]])
    small_all = _allgather_small(small_w, reduce=False, name="gather_small_weights")
    chips_first = [small_all[2 * j] for j in range(N_CHIPS)]
    wgu_full = jnp.concatenate(
        [_unpack(b, spans_w[0], w_gate_up.shape[1:]) for b in chips_first], axis=1)
    convw_full = jnp.concatenate(
        [_unpack(b, spans_w[1], conv_w.shape[1:]) for b in chips_first], axis=0)
    wgu_pad = jnp.concatenate(
        [wgu_full, jnp.zeros((LANES - GATE_RANK, wgu_full.shape[1]), F32)], axis=0).astype(BF16)
    convw_t = convw_full.T

    u, wo_buf, w1_buf, w2_buf, wi_buf = _rms_fwd(
        xs, norm1_g, name="norm1_fwd", comm=_gather_comm([wi_buf]), place=chip_core,
        side=_cast_side([w_out[0], w_ff1[0], w_ff2[0]], n_steps=s_len // 512, step_of=lambda i, j, k: i))
    wi_t = wi_buf.reshape(d_in, d)
    wg_t = jnp.concatenate([wi_t[d_main:], jnp.zeros((LANES - GATE_RANK, d), BF16)], axis=0)
    z, wo_buf, w1_buf = _matmul(u, wi_t, tb=True, tm=2048, tn=1024, tk=d, out_dtypes=[F32], n_dim=d_main,
                                name="in_proj",
                                comm=_gather_comm([wo_buf, w1_buf], rows=[None, (0, d // 2)]))
    wo_full = wo_buf.reshape(d, d)
    (alow,) = _matmul(u, wg_t, tb=True, tm=1024, tn=LANES, tk=d, out_dtypes=[F32], name="in_proj_gate")
    y0 = _conv_fwd(z, convw_t, conv_norm_g)
    y, o, la, st, w1_cm = _gla_fwd(z, alow, wgu_pad, b_gate, gla_norm_g, y0,
                                   comm=_gather_comm([w1_buf], rows=[(d // 2, d // 2)], mid_at=0.9))
    x2, h = _matmul(y, wo_full, tm=512, tn=d, tk=d, out_dtypes=[F32, BF16], extras=(xs, norm2_g),
                    epilogue=_residual_norm_epilogue, name="out_proj")
    a, p, w2_buf = _matmul(
        h, w1_cm, tm=1024, tn=1024, tk=d, out_dtypes=[BF16, BF16], n_dim=d_ff,
        b_spec=pl.BlockSpec((None, d, 1024), lambda i, j, k: (j // 2, 0, j % 2)),
        epilogue=lambda acc: (acc, jnp.square(jnp.maximum(acc, 0.0))), name="ff1",
        comm=_gather_comm([w2_buf]))
    w2_full = w2_buf.reshape(d_ff, d)
    (x3,) = _matmul(p, w2_full, tm=1024, tn=1024, tk=2048, out_dtypes=[F32], extras=(x2,),
                    epilogue=_add_epilogue, name="ff2")
    dx3b, g_normf, loss_part = _loss_head(x3, norm_f, target)

    (da,) = _matmul(dx3b, w2_full, tb=True, tm=1024, tn=1024, tk=d, out_dtypes=[BF16], extras=(a,),
                    epilogue=lambda acc, av: (acc * (2.0 * jnp.maximum(av, 0.0)),), name="ff2_dx")
    (dh,) = _matmul(
        da, w1_cm, tb=True, tm=2048, tn=1024, tk=2048, out_dtypes=[BF16], n_dim=d,
        b_spec=pl.BlockSpec((None, 1024, 2048), lambda i, j, k: (k, j, 0)), name="ff1_dx")
    dw_k = s_len // 1024
    g_w2, dx2, g_norm2 = _matmul(
        p, dx3b, ta=True, tm=1024, tn=d, tk=1024, out_dtypes=[BF16], name="ff2_dw",
        side=_rms_bwd_side(dh, x2, norm2_g, dx3b, n_steps=(d_ff // 1024) * dw_k,
                           block_of_step=lambda i, j, k: i * dw_k + k))
    (g_w1,) = _matmul(
        h, da, ta=True, tm=1024, tn=n_ff, tk=2048, out_dtypes=[BF16], name="ff1_dw",
        out_shapes=[jax.ShapeDtypeStruct((N_CHIPS, d, n_ff), BF16)],
        out_specs=[pl.BlockSpec((None, 1024, n_ff), lambda i, j, k: (j, i, 0))])
    g_w2 = g_w2.reshape(N_CHIPS, n_ff, d)
    dy, t_w1, t_w2 = _matmul(dx2, wo_full, tb=True, tm=1024, tn=1024, tk=d, out_dtypes=[BF16],
                             name="out_proj_dx", comm=_swap_comm([g_w1, g_w2]))
    ow_k = s_len // 512
    g_wo, p_w1, p_w2 = _matmul(
        y, dx2, ta=True, tm=1024, tn=d, tk=512, out_dtypes=[BF16], name="out_proj_dw",
        side=_add_own_half_side([g_w1, g_w2], [t_w1, t_w2], n_steps=(d // 1024) * ow_k,
                                step_of=lambda i, j, k: i * ow_k + k), place=chip_core)
    g_wo = g_wo.reshape(N_CHIPS, d // N_CHIPS, d)
    conv_tiles = s_len // 512
    dz0, g_convw_t, g_convg = _run_side(
        _conv_bwd_side(dy, z, convw_t, conv_norm_g, n_steps=conv_tiles, step_of=lambda i, j, k: i),
        n_steps=conv_tiles, name="conv_bwd")
    dz, dalow, g_wgu_pad, g_bg, g_gg, l_w2 = _gla_bwd(dy, z, o, la, st, alow, wgu_pad, gla_norm_g, dz0,
                                                      comm=_exchange_comm([p_w2]))
    dwi_k = s_len // 2048
    g_wi_t, m_w2, l_w1 = _matmul(
        dz, u, ta=True, tm=1024, tn=d, tk=2048, out_dtypes=[BF16], name="in_proj_dw",
        out_shapes=[jax.ShapeDtypeStruct((d_in, d), BF16)], comm=_exchange_comm([p_w1]),
        side=_sum_chips_side(p_w2, l_w2, n_blocks=_pow2_below(d_main // 1024 * dwi_k),
                             step_of=lambda i, j, k: i * dwi_k + k),
        place=chip_core)
    (g_wi_t,) = _matmul(dalow, u, ta=True, tm=LANES, tn=d, tk=1024, out_dtypes=[BF16],
                        epilogue=lambda acc: (acc[:GATE_RANK],), into=g_wi_t, name="in_proj_gate_dw",
                        out_shapes=[jax.ShapeDtypeStruct((d_in, d), BF16)],
                        out_specs=[pl.BlockSpec((GATE_RANK, d), lambda i, j, k: (d_main // GATE_RANK, 0))])
    g_wi = g_wi_t.reshape(N_CHIPS, n_sh, d)
    du_gate, m_w1, t_wi, t_wo = _matmul(
        dalow, wg_t, tm=1024, tn=1024, tk=LANES, out_dtypes=[F32], name="in_proj_gate_dx",
        comm=_swap_comm([g_wi, g_wo]),
        side=_sum_chips_side(p_w1, l_w1, n_blocks=_pow2_below(s_len // 1024 * (d // 1024)),
                             step_of=lambda i, j, k: i * (d // 1024) + j),
        place=chip_core)
    p_wi = _add_own_half(g_wi, t_wi, chip_core, name="pre_reduce_w_in")
    p_wo = _add_own_half(g_wo, t_wo, chip_core, name="pre_reduce_w_out")
    du, l_wi, l_wo, m_w1, m_w2 = _matmul(
        dz, wi_t, tm=1024, tn=1024, tk=2048, out_dtypes=[BF16], n_dim=d, extras=(du_gate,),
        epilogue=_add_epilogue, name="in_proj_dx",
        comm=[_exchange_comm([p_wi, p_wo]), _join_comm([m_w1, m_w2])])
    m_wi = _sum_chips(p_wi, l_wi, chip_core, name="reduce_w_in")
    m_wo = _sum_chips(p_wo, l_wo, chip_core, name="reduce_w_out")
    grad_x, g_norm1 = _rms_bwd(du, xs, norm1_g, dx2, name="norm1_bwd")
    m_wi, m_wo = _standalone(_join_comm([m_wi, m_wo]), name="join_w_in_w_out")
    g_big = [m_wi, m_wo, m_w1, m_w2]

    small_g, spans_g = _pack([g_norm1, g_wgu_pad[:GATE_RANK], g_bg, g_convw_t, g_convg, g_gg, g_norm2,
                              g_normf, loss_part[:, :1]])
    tot = _allgather_small(small_g, reduce=True, name="reduce_small_grads")
    t_norm1 = _unpack(tot, spans_g[0], (1, d))
    t_wgu = _unpack(tot, spans_g[1], (GATE_RANK, HEADS * DK))
    t_bg = _unpack(tot, spans_g[2], (1, HEADS * DK))
    t_convw = _unpack(tot, spans_g[3], (3, D_CONV)).T
    t_convg = _unpack(tot, spans_g[4], (1, D_CONV))
    t_gg = _unpack(tot, spans_g[5], (1, DV))
    t_norm2 = _unpack(tot, spans_g[6], (1, d))
    t_normf = _unpack(tot, spans_g[7], (1, d))
    loss = _unpack(tot, spans_g[8], ())
    n_gu = w_gate_up.shape[2]
    n_cw = conv_w.shape[1]
    t_wgu = lax.dynamic_slice(t_wgu, (0, chip * n_gu), (GATE_RANK, n_gu))
    t_convw = lax.dynamic_slice(t_convw, (chip * n_cw, 0), (n_cw, 3))

    order = ["norm1_g", "w_in", "w_gate_up", "b_gate", "conv_w", "conv_norm_g", "gla_norm_g", "w_out",
             "norm2_g", "w_ff1", "w_ff2", "norm_f_g"]
    weights = dict(norm1_g=norm1_g, w_in=w_in, w_gate_up=w_gate_up, b_gate=b_gate, conv_w=conv_w,
                   conv_norm_g=conv_norm_g, gla_norm_g=gla_norm_g, w_out=w_out, norm2_g=norm2_g,
                   w_ff1=w_ff1, w_ff2=w_ff2, norm_f_g=norm_f_g)
    moms = dict(norm1_g=m_norm1_g, w_in=m_w_in, w_gate_up=m_w_gate_up, b_gate=m_b_gate, conv_w=m_conv_w,
                conv_norm_g=m_conv_norm_g, gla_norm_g=m_gla_norm_g, w_out=m_w_out, norm2_g=m_norm2_g,
                w_ff1=m_w_ff1, w_ff2=m_w_ff2, norm_f_g=m_norm_f_g)
    vels = dict(norm1_g=v_norm1_g, w_in=v_w_in, w_gate_up=v_w_gate_up, b_gate=v_b_gate, conv_w=v_conv_w,
                conv_norm_g=v_conv_norm_g, gla_norm_g=v_gla_norm_g, w_out=v_w_out, norm2_g=v_norm2_g,
                w_ff1=v_w_ff1, w_ff2=v_w_ff2, norm_f_g=v_norm_f_g)
    grads2d = dict(norm1_g=t_norm1, w_in=g_big[0], w_gate_up=t_wgu, b_gate=t_bg, conv_w=t_convw,
                   conv_norm_g=t_convg, gla_norm_g=t_gg, w_out=g_big[1], norm2_g=t_norm2,
                   w_ff1=g_big[2], w_ff2=g_big[3], norm_f_g=t_normf)
    out_g, out_d, out_m, out_v = [], [], [], []
    for nm in order:
        w = weights[nm]
        g2 = grads2d[nm]
        if nm == "w_in":
            to2d, back = (lambda t: t[0].T), (lambda t: t.T.reshape(w.shape))
        else:
            to2d, back = (lambda t: t.reshape(g2.shape)), (lambda t: t.reshape(w.shape))
        res = _adamw(to2d(w), g2, to2d(moms[nm]), to2d(vels[nm]), name="adamw_" + nm)
        for lst, r in zip((out_g, out_d, out_m, out_v), res):
            lst.append(back(r))
    return (loss, grad_x.reshape(x.shape), *out_g, *out_d, *out_m, *out_v)
```

```python
import functools

import jax
import jax.numpy as jnp
from jax import lax
from jax.experimental import pallas as pl
from jax.experimental.pallas import tpu as pltpu

F32 = jnp.float32
BF16 = jnp.bfloat16
MESH = pl.DeviceIdType.MESH

EPS = 1e-6
CHUNK = 64
HEADS = 4
DK = 128
DV = 256
D_CONV = 1024
GROUP = 128
GATE_RANK = 16
LANES = 128
SUBLANES = 8
N_CHIPS = 4
N_DEV = 8
CHIP_MASKS = ((1, 0), (0, 1), (1, 1))

ADAM_LR = 0.001
ADAM_B1 = 0.9
ADAM_B2 = 0.999
ADAM_EPS = 1e-08
ADAM_WD = 0.01
ADAM_STEP = 10

VMEM_LIMIT = 56 * 1024 * 1024


def _params(*sem):
    return pltpu.CompilerParams(dimension_semantics=tuple(sem), vmem_limit_bytes=VMEM_LIMIT)


def _rowsum8(v):
    r, c = v.shape
    return jnp.sum(v.reshape(r // SUBLANES, SUBLANES, c), axis=0)


def _tile(rows, cols):
    for cand in (256, 128, 64, 32, 16, 8):
        if rows % cand == 0 and rows > cand:
            return cand, cols
    if rows * cols * 4 > (2 << 20) and cols % 256 == 0:
        return rows, 256
    return rows, cols


class _Comm:
    def __init__(self, ins, outs, aliases, n_sems, first, last, mid=None, mid_at=0.75):
        self.ins = list(ins)
        self.outs = list(outs)
        self.aliases = dict(aliases)
        self.n_sems = n_sems
        self.first = first
        self.mid = mid
        self.mid_at = mid_at
        self.last = last


class _PlaceSpec:
    def __init__(self, block_shape, index_map):
        self.block_shape, self.index_map = block_shape, index_map


def _call(body, *, grid, in_specs, out_specs, out_shape, operands, name, scratch_shapes=(), sem=None,
          aliases=None, comm=None, place=None):
    aliases = dict(aliases or {})
    comms = [] if comm is None else (list(comm) if isinstance(comm, (list, tuple)) else [comm])
    n_in, n_out, n_scr = len(in_specs), len(out_specs), len(scratch_shapes)
    c_ins_all = [a for cm in comms for a in cm.ins]
    c_outs_all = [o for cm in comms for o in cm.outs]
    n_ci, n_co = len(c_ins_all), len(c_outs_all)
    n_place = 0 if place is None else 1

    def adapt(spec):
        if isinstance(spec, _PlaceSpec):
            return pl.BlockSpec(spec.block_shape, spec.index_map)
        if place is None or spec.index_map is None:
            return spec
        return pl.BlockSpec(spec.block_shape, lambda *a, f=spec.index_map: f(*a[:-1]))

    def full_body(*refs):
        refs = refs[n_place:]
        ins = refs[:n_in]
        o0 = n_in + n_ci
        outs = refs[o0:o0 + n_out]
        s0 = o0 + n_out + n_co
        scr = refs[s0:s0 + n_scr]
        sems = refs[s0 + n_scr:]
        parts, i_at, o_at = [], n_in, o0 + n_out
        for q, cm in enumerate(comms):
            parts.append((cm, refs[i_at:i_at + len(cm.ins)], refs[o_at:o_at + len(cm.outs)],
                          sems[2 * q], sems[2 * q + 1]))
            i_at += len(cm.ins)
            o_at += len(cm.outs)
        step = functools.reduce(lambda acc, ig: acc * ig[1] + pl.program_id(ig[0]), enumerate(grid), 0)
        n_steps = functools.reduce(lambda acc, g: acc * g, grid, 1)
        @pl.when(step == 0)
        def _():
            for cm, c_ins, c_outs, ss, rs in parts:
                cm.first(c_ins, c_outs, ss, rs)

        def mid_step(cm):
            ms = int(cm.mid_at * n_steps)
            return ms if 0 < ms < n_steps - 1 else None

        for cm, c_ins, c_outs, ss, rs in parts:
            if cm.mid is not None and mid_step(cm) is not None:
                @pl.when(step == mid_step(cm))
                def _(cm=cm, c_ins=c_ins, c_outs=c_outs, ss=ss, rs=rs):
                    cm.mid(c_ins, c_outs, ss, rs)

        body(*ins, *outs, *scr)

        @pl.when(step == n_steps - 1)
        def _():
            for cm, c_ins, c_outs, ss, rs in parts:
                if cm.mid is not None and mid_step(cm) is None:
                    cm.mid(c_ins, c_outs, ss, rs)
                cm.last(c_ins, c_outs, ss, rs)

    any_spec = pl.BlockSpec(memory_space=pl.ANY)
    i_at, o_at, sem_shapes = n_in, n_out, []
    for cm in comms:
        for i_in, i_out in cm.aliases.items():
            aliases[i_at + i_in] = o_at + i_out
        i_at += len(cm.ins)
        o_at += len(cm.outs)
        sem_shapes += [pltpu.SemaphoreType.DMA((cm.n_sems,)), pltpu.SemaphoreType.DMA((cm.n_sems,))]
    specs = dict(grid=grid, in_specs=[adapt(s) for s in in_specs] + [any_spec] * n_ci,
                 out_specs=[adapt(s) for s in out_specs] + [any_spec] * n_co,
                 scratch_shapes=list(scratch_shapes) + sem_shapes)
    if place is not None:
        specs = dict(grid_spec=pltpu.PrefetchScalarGridSpec(num_scalar_prefetch=1, **specs))
    if comms:
        params = pltpu.CompilerParams(dimension_semantics=("arbitrary",) * len(grid),
                                      vmem_limit_bytes=VMEM_LIMIT, has_side_effects=True)
    else:
        params = _params(*(sem or ("arbitrary",) * len(grid)))
    return pl.pallas_call(
        full_body, out_shape=list(out_shape) + c_outs_all,
        input_output_aliases={k + n_place: v for k, v in aliases.items()},
        compiler_params=params, name=name, **specs,
    )(*(() if place is None else (place,)), *operands, *c_ins_all)


def _matmul(a, b, *, ta=False, tb=False, tm, tn, tk, out_dtypes, name, extras=(), epilogue=None,
            out_shapes=None, out_specs=None, b_spec=None, n_dim=None, into=None, side=None, comm=None,
            place=None):
    n_into = 0 if into is None else 1
    if ta:
        k_dim, m_dim = a.shape
    else:
        m_dim, k_dim = a.shape
    if n_dim is None:
        n_dim = b.shape[0] if tb else b.shape[1]
        assert (b.shape[1] if tb else b.shape[0]) == k_dim
    assert m_dim % tm == 0 and n_dim % tn == 0 and k_dim % tk == 0, (name, a.shape, b.shape)
    nk = k_dim // tk
    n_ex, n_out = len(extras), len(out_dtypes)
    dims = (((0 if ta else 1,), (1 if tb else 0,)), ((), ()))
    grid = (m_dim // tm, n_dim // tn, nk)
    s_ins, s_outs, s_scr = (len(side.ins), len(side.outs), len(side.scratch)) if side else (0, 0, 0)

    def body(*refs):
        a_ref, b_ref = refs[0], refs[1]
        ex_refs = refs[2:2 + n_ex]
        i0 = 2 + n_ex + n_into
        side_in = refs[i0:i0 + s_ins]
        o_refs = refs[i0 + s_ins:i0 + s_ins + n_out]
        side_out = refs[i0 + s_ins + n_out:i0 + s_ins + n_out + s_outs]
        side_scr = refs[len(refs) - s_scr:] if s_scr else ()
        if side is not None:
            step = (pl.program_id(0) * grid[1] + pl.program_id(1)) * grid[2] + pl.program_id(2)

            @pl.when(step == 0)
            def _():
                side.init(side_scr)

        def dot():
            if side is not None:
                side.body(step, side_in, side_out, side_scr)
            return lax.dot_general(a_ref[...].astype(BF16), b_ref[...].astype(BF16), dims,
                                   preferred_element_type=F32)

        def finish(acc):
            outs = epilogue(acc, *[e[...] for e in ex_refs]) if epilogue is not None else (acc,)
            for o_ref, o in zip(o_refs, outs):
                o_ref[...] = o.astype(o_ref.dtype)

        if nk == 1:
            finish(dot())
        else:
            acc_ref = refs[len(refs) - s_scr - 1]
            k = pl.program_id(2)

            @pl.when(k == 0)
            def _():
                acc_ref[...] = dot()

            @pl.when(jnp.logical_and(k > 0, k < nk - 1))
            def _():
                acc_ref[...] += dot()

            @pl.when(k == nk - 1)
            def _():
                finish(acc_ref[...] + dot())

    a_spec = (pl.BlockSpec((tk, tm), lambda i, j, k: (k, i)) if ta
              else pl.BlockSpec((tm, tk), lambda i, j, k: (i, k)))
    if b_spec is None:
        b_spec = (pl.BlockSpec((tn, tk), lambda i, j, k: (j, k)) if tb
                  else pl.BlockSpec((tk, tn), lambda i, j, k: (k, j)))
    io_spec = pl.BlockSpec((tm, tn), lambda i, j, k: (i, j))

    def extra_spec(e):
        if e.shape == (m_dim, n_dim):
            return io_spec
        if e.shape[1] == n_dim:
            return pl.BlockSpec((e.shape[0], tn), lambda i, j, k: (0, j))
        assert e.shape[0] == m_dim, (name, e.shape)
        return pl.BlockSpec((tm, e.shape[1]), lambda i, j, k: (i, 0))

    if out_shapes is None:
        out_shapes = [jax.ShapeDtypeStruct((m_dim, n_dim), dt) for dt in out_dtypes]
    if out_specs is None:
        out_specs = [io_spec] * n_out
    return _call(
        body,
        grid=grid,
        in_specs=([a_spec, b_spec] + [extra_spec(e) for e in extras]
                  + [pl.BlockSpec(memory_space=pl.ANY)] * n_into
                  + (list(side.in_specs) if side else [])),
        out_specs=list(out_specs) + (list(side.out_specs) if side else []),
        out_shape=list(out_shapes) + (list(side.outs) if side else []),
        scratch_shapes=([pltpu.VMEM((tm, tn), F32)] if nk > 1 else []) + (list(side.scratch) if side else []),
        sem=("parallel", "parallel", "arbitrary") if side is None else None,
        aliases={2 + n_ex: 0} if n_into else None,
        operands=(a, b, *extras) + ((into,) if n_into else ()) + (tuple(side.ins) if side else ()),
        name=name, comm=comm, place=place)


def _add_epilogue(acc, r):
    return (acc + r,)


def _residual_norm_epilogue(acc, x, g):
    x2 = acc + x
    r = lax.rsqrt(jnp.mean(x2 * x2, axis=-1, keepdims=True) + EPS)
    return x2, x2 * r * g


def _rms_fwd(x, g, *, name, tm=512, comm=None, side=None, place=None):
    s_len, d = x.shape
    s_ins, s_outs = (len(side.ins), len(side.outs)) if side else (0, 0)

    def body(*refs):
        x_ref, g_ref = refs[0], refs[1]
        o_ref = refs[2 + s_ins]
        xv = x_ref[...]
        r = lax.rsqrt(jnp.mean(xv * xv, axis=-1, keepdims=True) + EPS)
        o_ref[...] = (xv * r * g_ref[...]).astype(o_ref.dtype)
        if side is not None:
            scr = refs[3 + s_ins + s_outs:]

            @pl.when(pl.program_id(0) == 0)
            def _():
                side.init(scr)

            side.body(pl.program_id(0), refs[2:2 + s_ins], refs[3 + s_ins:3 + s_ins + s_outs], scr)

    row = pl.BlockSpec((tm, d), lambda i, j, k: (i, 0))
    return _call(
        body, grid=(s_len // tm, 1, 1),
        in_specs=[row, pl.BlockSpec((1, d), lambda i, j, k: (0, 0))] + (list(side.in_specs) if side else []),
        out_specs=[row] + (list(side.out_specs) if side else []),
        out_shape=[jax.ShapeDtypeStruct((s_len, d), BF16)] + (list(side.outs) if side else []),
        scratch_shapes=list(side.scratch) if side else [],
        operands=(x, g) + (tuple(side.ins) if side else ()), name=name, comm=comm, place=place)


NORM_ROWS = 16


def _zero_refs(refs):
    for r in refs:
        r[...] = jnp.zeros_like(r)


def _rms_bwd_block(dn_ref, x_ref, g_ref, res_ref, dx_ref, dg_ref, acc_ref):
    gv = g_ref[...]
    acc = acc_ref[...]
    for s in range(x_ref.shape[0] // NORM_ROWS):
        sl = slice(s * NORM_ROWS, (s + 1) * NORM_ROWS)
        xv = x_ref[sl, :]
        dnv = dn_ref[sl, :].astype(F32)
        r = lax.rsqrt(jnp.mean(xv * xv, axis=-1, keepdims=True) + EPS)
        xh = xv * r
        acc = acc + _rowsum8(dnv * xh)
        dxh = dnv * gv
        dx_ref[sl, :] = (r * (dxh - xh * jnp.mean(dxh * xh, axis=-1, keepdims=True))
                         + res_ref[sl, :].astype(F32))
    acc_ref[...] = acc
    dg_ref[...] = jnp.sum(acc, axis=0, keepdims=True)


class _Side:
    def __init__(self, ins, in_specs, outs, out_specs, scratch, init, body):
        self.ins, self.in_specs, self.outs, self.out_specs = ins, in_specs, outs, out_specs
        self.scratch = scratch
        self.init = init
        self.body = body


def _rms_bwd_side(dn, x, g, res, *, block_of_step, n_steps):
    s_len, d = x.shape
    row = pl.BlockSpec((s_len // n_steps, d), lambda i, j, k: (block_of_step(i, j, k), 0))
    vec = pl.BlockSpec((1, d), lambda i, j, k: (0, 0))
    return _Side(
        ins=[dn, x, g, res], in_specs=[row, row, vec, row],
        outs=[jax.ShapeDtypeStruct((s_len, d), F32), jax.ShapeDtypeStruct((1, d), F32)],
        out_specs=[row, vec], scratch=[pltpu.VMEM((SUBLANES, d), F32)],
        init=_zero_refs, body=lambda step, ins, outs, scr: _rms_bwd_block(*ins, *outs, *scr))


def _rms_bwd(dn, x, g, res, *, name, tm=512, comm=None):
    s_len, d = x.shape
    n = s_len // tm

    def body(*refs):
        @pl.when(pl.program_id(0) == 0)
        def _():
            _zero_refs(refs[-1:])

        _rms_bwd_block(*refs)

    row = pl.BlockSpec((tm, d), lambda i: (i, 0))
    vec = pl.BlockSpec((1, d), lambda i: (0, 0))
    return _call(
        body, grid=(n,),
        in_specs=[row, row, vec, row],
        out_specs=[row, vec],
        out_shape=[jax.ShapeDtypeStruct((s_len, d), F32), jax.ShapeDtypeStruct((1, d), F32)],
        scratch_shapes=[pltpu.VMEM((SUBLANES, d), F32)],
        operands=(dn, x, g, res), name=name, comm=comm)


def _loss_head(x3, g, target, *, tm=512):
    s_len, d = x3.shape
    n = s_len // tm

    def body(x_ref, g_ref, t_ref, dxb_ref, dg_ref, loss_ref, accg_ref, accl_ref):
        i = pl.program_id(0)
        xv = x_ref[...]
        gv = g_ref[...]
        r = lax.rsqrt(jnp.mean(xv * xv, axis=-1, keepdims=True) + EPS)
        xh = xv * r
        err = xh * gv - t_ref[...]

        @pl.when(i == 0)
        def _():
            accg_ref[...] = jnp.zeros_like(accg_ref)
            accl_ref[...] = jnp.zeros_like(accl_ref)

        accl_ref[...] += _rowsum8(err * err)
        dn = err * (1.0 / d)
        accg_ref[...] += _rowsum8(dn * xh)
        dxh = dn * gv
        dx = r * (dxh - xh * jnp.mean(dxh * xh, axis=-1, keepdims=True))
        dxb_ref[...] = dx.astype(BF16)

        @pl.when(i == n - 1)
        def _():
            dg_ref[...] = jnp.sum(accg_ref[...], axis=0, keepdims=True)
            tot = jnp.sum(jnp.sum(accl_ref[...], axis=0, keepdims=True), axis=1, keepdims=True)
            loss_ref[...] = jnp.broadcast_to(tot * (0.5 / d), (1, LANES))

    row = pl.BlockSpec((tm, d), lambda i: (i, 0))
    vec = pl.BlockSpec((1, d), lambda i: (0, 0))
    return pl.pallas_call(
        body, grid=(n,),
        in_specs=[row, vec, row],
        out_specs=[row, vec, pl.BlockSpec((1, LANES), lambda i: (0, 0))],
        out_shape=[jax.ShapeDtypeStruct((s_len, d), BF16),
                   jax.ShapeDtypeStruct((1, d), F32), jax.ShapeDtypeStruct((1, LANES), F32)],
        scratch_shapes=[pltpu.VMEM((SUBLANES, d), F32), pltpu.VMEM((SUBLANES, d), F32)],
        compiler_params=_params("arbitrary"), name="loss_head",
    )(x3, g, target)


def _shift_down(v, k, rows_before, row):
    out = pltpu.roll(v, k, axis=0)
    for j in range(k):
        out = jnp.where(row == j, rows_before[j], out)
    return out


def _shift_up(v, k, rows_after, row):
    t = v.shape[0]
    out = pltpu.roll(v, t - k, axis=0)
    for j in range(k):
        out = jnp.where(row == t - k + j, rows_after[j], out)
    return out


def _conv_fwd(z, w_t, gain, *, ts=512):
    s_len = z.shape[0]
    n_grp = D_CONV // GROUP

    def body(cb_ref, cc_ref, ch_ref, w_ref, g_ref, y_ref, carry_ref):
        i = pl.program_id(0)

        @pl.when(i == 0)
        def _():
            carry_ref[...] = jnp.zeros_like(carry_ref)

        row = lax.broadcasted_iota(jnp.int32, (ts, GROUP), 0)
        for g in range(n_grp):
            sl = slice(g * GROUP, (g + 1) * GROUP)
            uu = cc_ref[:, sl] * ch_ref[:, sl]
            p2 = carry_ref[6:7, sl]
            p1 = carry_ref[7:8, sl]
            u1 = _shift_down(uu, 1, [p1], row)
            u2 = _shift_down(uu, 2, [p2, p1], row)
            conv = w_ref[0:1, sl] * u2 + w_ref[1:2, sl] * u1 + w_ref[2:3, sl] * uu
            y = cb_ref[:, sl] * conv
            carry_ref[:, sl] = uu[ts - SUBLANES:ts, :]
            rg = lax.rsqrt(jnp.mean(y * y, axis=-1, keepdims=True) + EPS)
            y_ref[:, sl] = (y * rg * g_ref[:, sl]).astype(BF16)

    def col(j):
        return pl.BlockSpec((ts, D_CONV), lambda i, j=j: (i, j))

    small = lambda r: pl.BlockSpec((r, D_CONV), lambda i: (0, 0))
    return pl.pallas_call(
        body, grid=(s_len // ts,),
        in_specs=[col(0), col(1), col(2), small(3), small(1)],
        out_specs=col(0),
        out_shape=jax.ShapeDtypeStruct((s_len, 2 * D_CONV), BF16),
        scratch_shapes=[pltpu.VMEM((SUBLANES, D_CONV), F32)],
        compiler_params=_params("arbitrary"), name="conv_fwd",
    )(z, z, z, w_t, gain)


def _conv_bwd_side(dy, z, w_t, gain, *, n_steps, step_of):
    s_len = z.shape[0]
    n = n_steps
    ts = s_len // n
    n_grp = D_CONV // GROUP
    halo_blocks = ts // SUBLANES

    def body(step, ins, outs, scr):
        dy_ref, cb_ref, cc_ref, ch_ref, hcc_ref, hch_ref, w_ref, g_ref = ins
        dz_ref, dw_ref, dg_ref = outs
        carry_ref, accw_ref, accg_ref = scr
        first_tile = (n - 1 - step) == 0
        row = lax.broadcasted_iota(jnp.int32, (ts, GROUP), 0)
        keep = jnp.where(first_tile, 0.0, 1.0)
        for g in range(n_grp):
            sl = slice(g * GROUP, (g + 1) * GROUP)
            cc = cc_ref[:, sl]
            ch = ch_ref[:, sl]
            cb = cb_ref[:, sl]
            uu = cc * ch
            p2 = hcc_ref[6:7, sl] * hch_ref[6:7, sl] * keep
            p1 = hcc_ref[7:8, sl] * hch_ref[7:8, sl] * keep
            u1 = _shift_down(uu, 1, [p1], row)
            u2 = _shift_down(uu, 2, [p2, p1], row)
            w0, w1, w2 = w_ref[0:1, sl], w_ref[1:2, sl], w_ref[2:3, sl]
            conv = w0 * u2 + w1 * u1 + w2 * uu
            y = cb * conv
            rg = lax.rsqrt(jnp.mean(y * y, axis=-1, keepdims=True) + EPS)
            yh = y * rg
            dyv = dy_ref[:, sl].astype(F32)
            accg_ref[:, sl] += _rowsum8(dyv * yh)
            dyn = dyv * g_ref[:, sl]
            dpre = rg * (dyn - yh * jnp.mean(dyn * yh, axis=-1, keepdims=True))
            dz_ref[:, sl] = (dpre * conv).astype(BF16)
            dconv = dpre * cb
            accw_ref[0:8, sl] += _rowsum8(dconv * u2)
            accw_ref[8:16, sl] += _rowsum8(dconv * u1)
            accw_ref[16:24, sl] += _rowsum8(dconv * uu)
            n0 = carry_ref[0:1, sl]
            n1 = carry_ref[1:2, sl]
            d1 = _shift_up(dconv, 1, [n0], row)
            d2 = _shift_up(dconv, 2, [n0, n1], row)
            duu = w2 * dconv + w1 * d1 + w0 * d2
            carry_ref[:, sl] = dconv[0:SUBLANES, :]
            dz_ref[:, D_CONV + g * GROUP:D_CONV + (g + 1) * GROUP] = (duu * ch).astype(BF16)
            dz_ref[:, 2 * D_CONV + g * GROUP:2 * D_CONV + (g + 1) * GROUP] = (duu * cc).astype(BF16)

        for k in range(3):
            dw_ref[k:k + 1, :] = jnp.sum(accw_ref[8 * k:8 * k + 8, :], axis=0, keepdims=True)
        dg_ref[...] = jnp.sum(accg_ref[...], axis=0, keepdims=True)

    def tile(i, j, k):
        return n - 1 - step_of(i, j, k)

    def col(c):
        return pl.BlockSpec((ts, D_CONV), lambda i, j, k, c=c: (tile(i, j, k), c))

    def halo(c):
        return pl.BlockSpec((SUBLANES, D_CONV),
                            lambda i, j, k, c=c: (jnp.maximum(tile(i, j, k) * halo_blocks - 1, 0), c))

    small = lambda r: pl.BlockSpec((r, D_CONV), lambda i, j, k: (0, 0))
    return _Side(
        ins=[dy, z, z, z, z, z, w_t, gain],
        in_specs=[col(0), col(0), col(1), col(2), halo(1), halo(2), small(3), small(1)],
        outs=[jax.ShapeDtypeStruct((s_len, 6 * D_CONV), BF16),
              jax.ShapeDtypeStruct((3, D_CONV), F32), jax.ShapeDtypeStruct((1, D_CONV), F32)],
        out_specs=[pl.BlockSpec((ts, 3 * D_CONV), lambda i, j, k: (tile(i, j, k), 0)), small(3), small(1)],
        scratch=[pltpu.VMEM((SUBLANES, D_CONV), F32), pltpu.VMEM((24, D_CONV), F32),
                 pltpu.VMEM((SUBLANES, D_CONV), F32)],
        init=_zero_refs, body=body)


def _split3(v):
    hi = v.astype(BF16)
    r1 = v - hi.astype(F32)
    mid = r1.astype(BF16)
    lo = (r1 - mid.astype(F32)).astype(BF16)
    return jnp.concatenate([hi, mid, lo], axis=1)


def _tri_sum(tri, v):
    w = v.shape[1]
    dd = jnp.dot(tri, _split3(v), preferred_element_type=F32)
    return dd[:, :w] + dd[:, w:2 * w] + dd[:, 2 * w:]


def _chunk_masks(ts):
    r = jnp.arange(ts)
    same = (r[:, None] // CHUNK) == (r[None, :] // CHUNK)
    later = jnp.logical_and(same, r[None, :] > r[:, None]).astype(BF16)
    earlier = jnp.logical_and(same, r[None, :] < r[:, None]).astype(BF16)
    chunk_of_row = jnp.arange(ts // CHUNK * SUBLANES)[:, None] // SUBLANES
    member = (chunk_of_row == (r[None, :] // CHUNK)).astype(BF16)
    return later, earlier, member


def _sigmoid(v):
    return 0.5 * jnp.tanh(0.5 * v) + 0.5


def _gla_fwd(z, alow, wgu, bg, gg, y_in, *, ts=512, comm=None):
    s_len = z.shape[0]
    nch = ts // CHUNK
    scale = DK ** -0.5

    tri_u, _, ind8 = _chunk_masks(ts)

    def body(q_ref, k_ref, v_ref, og_ref, al_ref, wgu_ref, bg_ref, gg_ref, tu_ref, ind_ref, yin_ref,
             y_ref, o_ref, la_ref, st_ref, state_ref, kd_ref, qs_ref, dec_ref):
        del yin_ref
        i = pl.program_id(0)

        @pl.when(i == 0)
        def _():
            state_ref[...] = jnp.zeros_like(state_ref)

        pre = jnp.dot(al_ref[...].astype(BF16), wgu_ref[...], preferred_element_type=F32) + bg_ref[...]
        la = (jnp.minimum(pre, 0.0) - jnp.log(1.0 + jnp.exp(-jnp.abs(pre)))) * (1.0 / 16.0)
        la_ref[...] = la
        kd_ref[...] = (k_ref[...] * jnp.exp(_tri_sum(tu_ref[...], la))).astype(BF16)
        qs_ref[...] = (q_ref[...] * scale).astype(BF16)
        dec_ref[...] = jnp.exp(_tri_sum(ind_ref[...], la))

        def chunk(cl, carry):
            rows = pl.ds(pl.multiple_of(cl * CHUNK, CHUNK), CHUNK)
            dec = dec_ref[pl.ds(pl.multiple_of(cl * SUBLANES, SUBLANES), 1), :]
            for h in range(HEADS):
                ks = slice(h * DK, (h + 1) * DK)
                vs = slice(h * DV, (h + 1) * DV)
                kv_t = lax.dot_general(v_ref[rows, vs].astype(BF16), kd_ref[rows, ks],
                                       (((0,), (0,)), ((), ())), preferred_element_type=F32)
                st = state_ref[h] * dec[:, ks] + kv_t
                state_ref[h] = st
                st_ref[cl, h] = st
                o_ref[rows, vs] = lax.dot_general(qs_ref[rows, ks], st.astype(BF16),
                                                  (((1,), (1,)), ((), ())), preferred_element_type=F32)
            return carry

        lax.fori_loop(0, nch, chunk, 0, unroll=2)

        ggv = gg_ref[...]
        for h in range(HEADS):
            vs = slice(h * DV, (h + 1) * DV)
            o_h = o_ref[:, vs]
            og_h = og_ref[:, vs]
            ro = lax.rsqrt(jnp.mean(o_h * o_h, axis=-1, keepdims=True) + EPS)
            y_ref[:, vs] = (o_h * ro * ggv * (og_h * _sigmoid(og_h))).astype(BF16)

    def zcol(width, j):
        return pl.BlockSpec((ts, width), lambda i, j=j: (i, j))

    full = lambda shape: pl.BlockSpec(shape, lambda i: tuple(0 for _ in shape))
    return _call(
        body, grid=(s_len // ts,),
        in_specs=[zcol(512, 6), zcol(512, 7), zcol(1024, 4), zcol(1024, 5), zcol(LANES, 0),
                  full((LANES, 512)), full((1, 512)), full((1, DV)), full(tri_u.shape), full(ind8.shape),
                  pl.BlockSpec(memory_space=pl.ANY)],
        out_specs=[zcol(1024, 1), zcol(1024, 0), zcol(512, 0),
                   pl.BlockSpec((nch, HEADS, DV, DK), lambda i: (i, 0, 0, 0))],
        out_shape=[jax.ShapeDtypeStruct((s_len, 2048), BF16), jax.ShapeDtypeStruct((s_len, 1024), F32),
                   jax.ShapeDtypeStruct((s_len, 512), F32),
                   jax.ShapeDtypeStruct((s_len // CHUNK, HEADS, DV, DK), F32)],
        scratch_shapes=[pltpu.VMEM((HEADS, DV, DK), F32), pltpu.VMEM((ts, 512), BF16),
                        pltpu.VMEM((ts, 512), BF16), pltpu.VMEM((nch * SUBLANES, 512), F32)],
        aliases={10: 0},
        operands=(z, z, z, z, alow, wgu, bg, gg, tri_u, ind8, y_in), name="gla_fwd", comm=comm)


def _gla_bwd(dy, z, o, la, st, alow, wgu, gg, dz_in, *, ts=512, comm=None):
    s_len = z.shape[0]
    n = s_len // ts
    nch = ts // CHUNK
    scale = DK ** -0.5

    tri_u, tri_l, ind8 = _chunk_masks(ts)

    def body(dy_ref, q_ref, k_ref, v_ref, og_ref, o_ref, la_ref, st_ref, stp_ref, al_ref, wgu_ref,
             gg_ref, tu_ref, tl_ref, ind_ref, dzin_ref, dz_ref, dal_ref, dwgu_ref, dbg_ref, dgg_ref,
             gt_ref, decn_ref, accw_ref, accb_ref, accg_ref, dla_ref,
             e_ref, kd_ref, kdb_ref, qs_ref, do_ref, dkd_ref, dec_ref, dbe_ref):
        del dzin_ref
        i = pl.program_id(0)
        first_tile = (n - 1 - i) == 0

        @pl.when(i == 0)
        def _():
            gt_ref[...] = jnp.zeros_like(gt_ref)
            decn_ref[...] = jnp.ones_like(decn_ref)
            accw_ref[...] = jnp.zeros_like(accw_ref)
            accb_ref[...] = jnp.zeros_like(accb_ref)
            accg_ref[...] = jnp.zeros_like(accg_ref)

        la = la_ref[...]
        e_dec = jnp.exp(_tri_sum(tu_ref[...], la))
        e_ref[...] = e_dec
        kd = k_ref[...] * e_dec
        kd_ref[...] = kd
        kdb_ref[...] = kd.astype(BF16)
        qs_ref[...] = (q_ref[...] * scale).astype(BF16)
        dec_ref[...] = jnp.exp(_tri_sum(ind_ref[...], la))
        ggv = gg_ref[...]
        for h in range(HEADS):
            vs = slice(h * DV, (h + 1) * DV)
            o_h = o_ref[:, vs]
            og_h = og_ref[:, vs]
            dy_h = dy_ref[:, vs].astype(F32)
            ro = lax.rsqrt(jnp.mean(o_h * o_h, axis=-1, keepdims=True) + EPS)
            oh = o_h * ro
            sig = _sigmoid(og_h)
            sil = og_h * sig
            accg_ref[...] += _rowsum8(dy_h * oh * sil)
            dz_ref[:, 2048 + h * DV:2048 + (h + 1) * DV] = (
                dy_h * oh * ggv * sig * (1.0 + og_h * (1.0 - sig))).astype(BF16)
            don = dy_h * ggv * sil
            do_ref[:, vs] = (ro * (don - oh * jnp.mean(don * oh, axis=-1, keepdims=True))).astype(BF16)
        keep = jnp.where(first_tile, 0.0, 1.0)

        def chunk(jrev, decn):
            cl = nch - 1 - jrev
            rows = pl.ds(pl.multiple_of(cl * CHUNK, CHUNK), CHUNK)
            one_row = pl.ds(pl.multiple_of(cl * SUBLANES, SUBLANES), 1)
            dec = dec_ref[one_row, :]
            has_prev = jnp.where(cl > 0, 1.0, 0.0)
            prev_idx = jnp.maximum(cl - 1, 0)
            for h in range(HEADS):
                ks = slice(h * DK, (h + 1) * DK)
                vs = slice(h * DV, (h + 1) * DV)
                dob = do_ref[rows, vs]
                s_c = st_ref[cl, h]
                dqs = jnp.dot(dob, s_c.astype(BF16), preferred_element_type=F32)
                dz_ref[rows, ks] = (dqs * scale).astype(BF16)
                gt = gt_ref[h] * decn[:, ks] + lax.dot_general(
                    dob, qs_ref[rows, ks], (((0,), (0,)), ((), ())), preferred_element_type=F32)
                gt_ref[h] = gt
                gb = gt.astype(BF16)
                dkd_ref[rows, ks] = jnp.dot(v_ref[rows, vs].astype(BF16), gb, preferred_element_type=F32)
                dz_ref[rows, 1024 + h * DV:1024 + (h + 1) * DV] = lax.dot_general(
                    kdb_ref[rows, ks], gb, (((1,), (1,)), ((), ())),
                    preferred_element_type=F32).astype(BF16)
                s_prev = has_prev * st_ref[prev_idx, h] + (1.0 - has_prev) * keep * stp_ref[0, h]
                dbe_ref[one_row, ks] = jnp.sum(gt * s_prev, axis=0, keepdims=True) * dec[:, ks]
            return dec

        decn_ref[0:1, :] = lax.fori_loop(0, nch, chunk, decn_ref[0:1, :], unroll=2)

        dkd = dkd_ref[...]
        dz_ref[:, 512:1024] = (dkd * e_ref[...]).astype(BF16)
        dla_ref[...] = _tri_sum(tl_ref[...], dkd * kd_ref[...])
        for c in range(nch):
            dla_ref[c * CHUNK:(c + 1) * CHUNK, :] += dbe_ref[c * SUBLANES:c * SUBLANES + 1, :]
        dpre = dla_ref[...] * (1.0 / 16.0) * (1.0 - jnp.exp(16.0 * la))
        accb_ref[...] += _rowsum8(dpre)
        dpb = dpre.astype(BF16)
        accw_ref[...] += lax.dot_general(al_ref[...].astype(BF16), dpb, (((0,), (0,)), ((), ())),
                                         preferred_element_type=F32)
        dal_ref[...] = lax.dot_general(dpb, wgu_ref[...], (((1,), (1,)), ((), ())),
                                       preferred_element_type=F32)

        @pl.when(i == n - 1)
        def _():
            dwgu_ref[...] = accw_ref[...]
            dbg_ref[...] = jnp.sum(accb_ref[...], axis=0, keepdims=True)
            dgg_ref[...] = jnp.sum(accg_ref[...], axis=0, keepdims=True)

    def zcol(width, j):
        return pl.BlockSpec((ts, width), lambda i, j=j: (n - 1 - i, j))

    full = lambda shape: pl.BlockSpec(shape, lambda i: tuple(0 for _ in shape))
    return _call(
        body, grid=(n,),
        in_specs=[zcol(1024, 1), zcol(512, 6), zcol(512, 7), zcol(1024, 4), zcol(1024, 5),
                  zcol(1024, 0), zcol(512, 0),
                  pl.BlockSpec((nch, HEADS, DV, DK), lambda i: (n - 1 - i, 0, 0, 0)),
                  pl.BlockSpec((1, HEADS, DV, DK),
                               lambda i: (jnp.maximum((n - 1 - i) * nch - 1, 0), 0, 0, 0)),
                  zcol(LANES, 0), full((LANES, 512)), full((1, DV)),
                  full(tri_u.shape), full(tri_l.shape), full(ind8.shape),
                  pl.BlockSpec(memory_space=pl.ANY)],
        out_specs=[zcol(3072, 1), zcol(LANES, 0), full((LANES, 512)), full((1, 512)), full((1, DV))],
        out_shape=[jax.ShapeDtypeStruct((s_len, 6144), BF16), jax.ShapeDtypeStruct((s_len, LANES), F32),
                   jax.ShapeDtypeStruct((LANES, 512), F32), jax.ShapeDtypeStruct((1, 512), F32),
                   jax.ShapeDtypeStruct((1, DV), F32)],
        scratch_shapes=[pltpu.VMEM((HEADS, DV, DK), F32), pltpu.VMEM((SUBLANES, 512), F32),
                        pltpu.VMEM((LANES, 512), F32), pltpu.VMEM((SUBLANES, 512), F32),
                        pltpu.VMEM((SUBLANES, DV), F32), pltpu.VMEM((ts, 512), F32),
                        pltpu.VMEM((ts, 512), F32), pltpu.VMEM((ts, 512), F32), pltpu.VMEM((ts, 512), BF16),
                        pltpu.VMEM((ts, 512), BF16), pltpu.VMEM((ts, 1024), BF16),
                        pltpu.VMEM((ts, 512), F32), pltpu.VMEM((nch * SUBLANES, 512), F32),
                        pltpu.VMEM((nch * SUBLANES, 512), F32)],
        aliases={15: 0},
        operands=(dy, z, z, z, z, o, la, st, st, alow, wgu, gg, tri_u, tri_l, ind8, dz_in),
        name="gla_bwd", comm=comm)


def _adamw(w, g, m, v, *, name):
    rows, cols = w.shape
    tr, tc = _tile(rows, cols)

    def body(w_ref, g_ref, m_ref, v_ref, go_ref, d_ref, nm_ref, nv_ref):
        gv = g_ref[...]
        go_ref[...] = gv
        m2 = ADAM_B1 * m_ref[...] + (1.0 - ADAM_B1) * gv
        v2 = ADAM_B2 * v_ref[...] + (1.0 - ADAM_B2) * jnp.square(gv)
        m_hat = m2 / (1.0 - ADAM_B1 ** ADAM_STEP)
        v_hat = v2 / (1.0 - ADAM_B2 ** ADAM_STEP)
        d_ref[...] = -ADAM_LR * (m_hat / (jnp.sqrt(v_hat) + ADAM_EPS) + ADAM_WD * w_ref[...])
        nm_ref[...] = m2
        nv_ref[...] = v2

    blk = pl.BlockSpec((tr, tc), lambda i, j: (i, j))
    shp = jax.ShapeDtypeStruct((rows, cols), F32)
    return pl.pallas_call(
        body, grid=(rows // tr, cols // tc), in_specs=[blk] * 4, out_specs=[blk] * 4, out_shape=[shp] * 4,
        compiler_params=_params("parallel", "parallel"), name=name,
    )(w, g, m, v)


def _my_place():
    return lax.axis_index("x"), lax.axis_index("y"), lax.axis_index("c")


def _flip(v, bit):
    return 1 - v if bit else v


def _allgather_small(buf, *, reduce, name):
    rows = buf.shape[0]

    def body(in_ref, out_ref, gat_ref, send_sems, recv_sems):
        x, y, c = _my_place()
        me = 4 * x + 2 * y + c
        gat_ref[me] = in_ref[...]
        copies = []
        for m in range(1, N_DEV):
            peer = (_flip(x, m & 4), _flip(y, m & 2), _flip(c, m & 1))
            cp = pltpu.make_async_remote_copy(
                src_ref=in_ref, dst_ref=gat_ref.at[me],
                send_sem=send_sems.at[m - 1], recv_sem=recv_sems.at[m - 1],
                device_id=peer, device_id_type=MESH)
            cp.start()
            copies.append(cp)
        for m in range(1, N_DEV):
            px, py, pc = _flip(x, m & 4), _flip(y, m & 2), _flip(c, m & 1)
            src_slot = gat_ref.at[4 * px + 2 * py + pc]
            pltpu.make_async_remote_copy(
                src_ref=src_slot, dst_ref=src_slot,
                send_sem=send_sems.at[m - 1], recv_sem=recv_sems.at[m - 1],
                device_id=(px, py, pc), device_id_type=MESH).wait_recv()
        for cp in copies:
            cp.wait_send()
        if reduce:
            tot = gat_ref[0]
            for d in range(1, N_DEV):
                tot = tot + gat_ref[d]
            out_ref[...] = tot
        else:
            out_ref[...] = gat_ref[...]

    out_shape = (rows, LANES) if reduce else (N_DEV, rows, LANES)
    return pl.pallas_call(
        body,
        in_specs=[pl.BlockSpec(memory_space=pltpu.VMEM)],
        out_specs=pl.BlockSpec(memory_space=pltpu.VMEM),
        out_shape=jax.ShapeDtypeStruct(out_shape, F32),
        scratch_shapes=[pltpu.VMEM((N_DEV, rows, LANES), F32),
                        pltpu.SemaphoreType.DMA((N_DEV - 1,)), pltpu.SemaphoreType.DMA((N_DEV - 1,))],
        compiler_params=pltpu.CompilerParams(has_side_effects=True),
        name=name,
    )(buf)


def _cast_into(shard, chip_core, *, name):
    rows, cols = shard.shape
    tr, tc = _tile(rows, cols)

    def body(cc_ref, s_ref, o_ref):
        del cc_ref
        o_ref[...] = s_ref[...].astype(BF16)

    grid_spec = pltpu.PrefetchScalarGridSpec(
        num_scalar_prefetch=1, grid=(rows // tr, cols // tc),
        in_specs=[pl.BlockSpec((tr, tc), lambda r, q, cc: (r, q))],
        out_specs=pl.BlockSpec((None, tr, tc), lambda r, q, cc: (cc[0], r, q)))
    return pl.pallas_call(
        body, grid_spec=grid_spec, out_shape=jax.ShapeDtypeStruct((N_CHIPS, rows, cols), BF16),
        compiler_params=_params("arbitrary", "arbitrary"), name=name,
    )(chip_core, shard)


def _remote(src, dst, send_sems, recv_sems, k, device):
    return pltpu.make_async_remote_copy(src_ref=src, dst_ref=dst, send_sem=send_sems.at[k],
                                        recv_sem=recv_sems.at[k], device_id=device, device_id_type=MESH)


def _col_half(ref, h, *lead, rows=None):
    hc = ref.shape[-1] // 2
    mid = (slice(None),) * (len(ref.shape) - 2 - len(lead))
    row_sel = slice(None) if rows is None else pl.ds(rows[0], rows[1])
    return ref.at[tuple(lead) + mid + (row_sel, pl.ds(h * hc, hc))]


def _gather_comm(bufs, rows=None, mid_at=0.75):
    n_w, n_m = len(bufs), len(CHIP_MASKS)
    rows = rows or [None] * n_w

    def first(c_ins, c_outs, ss, rs):
        x, y, c = _my_place()
        chip = 2 * x + y
        for w in range(n_w):
            mine = _col_half(c_outs[w], c, chip, rows=rows[w])
            for mi, (mx, my) in enumerate(CHIP_MASKS):
                _remote(mine, mine, ss, rs, w * n_m + mi, (_flip(x, mx), _flip(y, my), c)).start()

    def mid(c_ins, c_outs, ss, rs):
        x, y, c = _my_place()
        for w in range(n_w):
            for mi, (mx, my) in enumerate(CHIP_MASKS):
                k = w * n_m + mi
                px, py = _flip(x, mx), _flip(y, my)
                landed = _col_half(c_outs[w], c, 2 * px + py, rows=rows[w])
                _remote(landed, landed, ss, rs, k, (px, py, c)).wait_recv()
                _remote(landed, landed, ss, rs, n_w * n_m + k, (x, y, 1 - c)).start()

    def last(c_ins, c_outs, ss, rs):
        x, y, c = _my_place()
        chip = 2 * x + y
        for w in range(n_w):
            mine = _col_half(c_outs[w], c, chip, rows=rows[w])
            for mi, (mx, my) in enumerate(CHIP_MASKS):
                k = w * n_m + mi
                px, py = _flip(x, mx), _flip(y, my)
                theirs = _col_half(c_outs[w], 1 - c, 2 * px + py, rows=rows[w])
                _remote(theirs, theirs, ss, rs, n_w * n_m + k, (x, y, 1 - c)).wait_recv()
                _remote(mine, mine, ss, rs, k, (px, py, c)).wait_send()
                _remote(mine, mine, ss, rs, n_w * n_m + k, (x, y, 1 - c)).wait_send()

    return _Comm(ins=bufs, outs=[jax.ShapeDtypeStruct(b.shape, b.dtype) for b in bufs],
                 aliases={w: w for w in range(n_w)}, n_sems=2 * n_w * n_m, first=first, mid=mid, last=last,
                 mid_at=mid_at)


def _swap_comm(grads):
    n_w = len(grads)

    def copy(c_ins, c_outs, ss, rs, w):
        x, y, c = _my_place()
        return _remote(_col_half(c_ins[w], 1 - c), c_outs[w], ss, rs, w, (x, y, 1 - c))

    def first(c_ins, c_outs, ss, rs):
        for w in range(n_w):
            copy(c_ins, c_outs, ss, rs, w).start()

    def last(c_ins, c_outs, ss, rs):
        for w in range(n_w):
            copy(c_ins, c_outs, ss, rs, w).wait()

    return _Comm(ins=grads,
                 outs=[jax.ShapeDtypeStruct(g.shape[:2] + (g.shape[2] // 2,), g.dtype) for g in grads],
                 aliases={}, n_sems=n_w, first=first, last=last)


def _add_own_half(g, other, chip_core, *, name):
    n_chip, rows, hc = other.shape
    tr, tc = _tile(rows, hc)
    per_half = hc // tc

    def body(cc_ref, g_ref, o_ref, out_ref):
        del cc_ref
        out_ref[...] = (g_ref[...].astype(F32) + o_ref[...].astype(F32)).astype(BF16)

    grid_spec = pltpu.PrefetchScalarGridSpec(
        num_scalar_prefetch=1, grid=(n_chip, rows // tr, per_half),
        in_specs=[pl.BlockSpec((None, tr, tc), lambda j, r, q, cc: (j, r, cc[1] * per_half + q)),
                  pl.BlockSpec((None, tr, tc), lambda j, r, q, cc: (j, r, q))],
        out_specs=pl.BlockSpec((None, tr, tc), lambda j, r, q, cc: (j, r, q)))
    return pl.pallas_call(
        body, grid_spec=grid_spec, out_shape=jax.ShapeDtypeStruct((n_chip, rows, hc), BF16),
        compiler_params=_params("parallel", "parallel", "parallel"), name=name,
    )(chip_core, g, other)


def _add_own_half_side(gs, others, *, n_steps, step_of):
    n_chip, rows, hc = others[0].shape
    per_chip = n_steps // n_chip
    tr = rows // per_chip
    assert all(o.shape == others[0].shape for o in others) and tr * per_chip == rows and tr % 16 == 0

    def body(step, ins, outs, scr):
        del step, scr
        for q, out_ref in enumerate(outs):
            out_ref[...] = (ins[2 * q][...].astype(F32) + ins[2 * q + 1][...].astype(F32)).astype(BF16)

    def blk(own_half):
        def index(i, j, k, place):
            s = step_of(i, j, k)
            return (s // per_chip, s % per_chip, place[1] if own_half else 0)
        return _PlaceSpec((None, tr, hc), index)

    return _Side(
        ins=[a for pair in zip(gs, others) for a in pair], in_specs=[blk(True), blk(False)] * len(gs),
        outs=[jax.ShapeDtypeStruct(o.shape, BF16) for o in others], out_specs=[blk(False)] * len(gs),
        scratch=[], init=_zero_refs, body=body)


def _cast_side(shards, *, n_steps, step_of):
    def body(step, ins, outs, scr):
        del step, scr
        for in_ref, out_ref in zip(ins, outs):
            out_ref[...] = in_ref[...].astype(BF16)

    def rows(s):
        assert s.shape[0] % (16 * n_steps) == 0, s.shape
        return s.shape[0] // n_steps

    return _Side(
        ins=list(shards),
        in_specs=[pl.BlockSpec((rows(s), s.shape[1]), lambda i, j, k: (step_of(i, j, k), 0)) for s in shards],
        outs=[jax.ShapeDtypeStruct((N_CHIPS,) + s.shape, BF16) for s in shards],
        out_specs=[_PlaceSpec((None, rows(s), s.shape[1]),
                              lambda i, j, k, place: (place[0], step_of(i, j, k), 0)) for s in shards],
        scratch=[], init=_zero_refs, body=body)


def _pow2_below(n_steps, most=16):
    return min(most, 1 << (n_steps.bit_length() - 1))


def _sum_chips_side(own, landed, *, n_blocks, step_of):
    n_chip, rows, hc = own.shape
    tr = rows // n_blocks
    assert tr * n_blocks == rows and tr % 16 == 0

    def body(step, ins, outs, scr):
        del step, scr
        o_ref, l1_ref, l2_ref, l3_ref = ins
        outs[0][...] = ((o_ref[...].astype(F32) + l1_ref[...].astype(F32))
                        + l2_ref[...].astype(F32)) + l3_ref[...].astype(F32)

    def block(i, j, k):
        return jnp.minimum(step_of(i, j, k), n_blocks - 1)

    def slot(q):
        return _PlaceSpec((None, tr, hc),
                          lambda i, j, k, place, q=q: ((place[0] + q) % n_chip, block(i, j, k), 0))

    return _Side(
        ins=[own, landed, landed, landed], in_specs=[slot(0), slot(1), slot(2), slot(3)],
        outs=[jax.ShapeDtypeStruct((rows, 2 * hc), F32)],
        out_specs=[_PlaceSpec((tr, hc), lambda i, j, k, place: (block(i, j, k), place[1]))],
        scratch=[], init=_zero_refs, body=body)


def _run_side(side, *, n_steps, name):
    n_in, n_out = len(side.ins), len(side.outs)

    def body(*refs):
        step = pl.program_id(0)

        @pl.when(step == 0)
        def _():
            side.init(refs[n_in + n_out:])

        side.body(step, refs[:n_in], refs[n_in:n_in + n_out], refs[n_in + n_out:])

    return _call(body, grid=(n_steps, 1, 1), in_specs=side.in_specs, out_specs=side.out_specs,
                 out_shape=side.outs, scratch_shapes=side.scratch, operands=tuple(side.ins), name=name)


def _exchange_comm(pieces):
    n_w, n_m = len(pieces), len(CHIP_MASKS)

    def copies(c_ins, c_outs, ss, rs):
        x, y, c = _my_place()
        chip = 2 * x + y
        for w in range(n_w):
            for mi, (mx, my) in enumerate(CHIP_MASKS):
                px, py = _flip(x, mx), _flip(y, my)
                send = _remote(c_ins[w].at[2 * px + py], c_outs[w].at[chip], ss, rs, w * n_m + mi, (px, py, c))
                landed = c_outs[w].at[2 * px + py]
                yield send, _remote(landed, landed, ss, rs, w * n_m + mi, (px, py, c))

    def first(c_ins, c_outs, ss, rs):
        for send, _ in copies(c_ins, c_outs, ss, rs):
            send.start()

    def last(c_ins, c_outs, ss, rs):
        for send, arrival in copies(c_ins, c_outs, ss, rs):
            arrival.wait_recv()
            send.wait_send()

    return _Comm(ins=pieces, outs=[jax.ShapeDtypeStruct(p.shape, p.dtype) for p in pieces],
                 aliases={}, n_sems=n_w * n_m, first=first, last=last)


def _sum_chips(own, landed, chip_core, *, name):
    n_chip, rows, hc = own.shape
    tr, tc = _tile(rows, hc)
    per_half = hc // tc

    def body(cc_ref, o_ref, l1_ref, l2_ref, l3_ref, out_ref):
        del cc_ref
        out_ref[...] = ((o_ref[...].astype(F32) + l1_ref[...].astype(F32))
                        + l2_ref[...].astype(F32)) + l3_ref[...].astype(F32)

    def slot(k):
        return pl.BlockSpec((None, tr, tc), lambda r, q, cc, k=k: ((cc[0] + k) % n_chip, r, q))

    grid_spec = pltpu.PrefetchScalarGridSpec(
        num_scalar_prefetch=1, grid=(rows // tr, per_half),
        in_specs=[slot(0), slot(1), slot(2), slot(3)],
        out_specs=pl.BlockSpec((tr, tc), lambda r, q, cc: (r, cc[1] * per_half + q)))
    return pl.pallas_call(
        body, grid_spec=grid_spec, out_shape=jax.ShapeDtypeStruct((rows, 2 * hc), F32),
        compiler_params=_params("arbitrary", "arbitrary"), name=name,
    )(chip_core, own, landed, landed, landed)


def _join_comm(halves):
    n_w = len(halves)

    def first(c_ins, c_outs, ss, rs):
        x, y, c = _my_place()
        for w in range(n_w):
            mine = _col_half(c_outs[w], c)
            _remote(mine, mine, ss, rs, w, (x, y, 1 - c)).start()

    def last(c_ins, c_outs, ss, rs):
        x, y, c = _my_place()
        for w in range(n_w):
            theirs = _col_half(c_outs[w], 1 - c)
            _remote(theirs, theirs, ss, rs, w, (x, y, 1 - c)).wait()

    return _Comm(ins=halves, outs=[jax.ShapeDtypeStruct(h.shape, h.dtype) for h in halves],
                 aliases={w: w for w in range(n_w)}, n_sems=n_w, first=first, last=last)


def _standalone(comm, *, name):
    def body(o_ref):
        o_ref[...] = jnp.zeros_like(o_ref)

    return _call(body, grid=(1,), in_specs=[],
                 out_specs=[pl.BlockSpec((SUBLANES, LANES), lambda i: (0, 0))],
                 out_shape=[jax.ShapeDtypeStruct((SUBLANES, LANES), F32)], operands=(), name=name,
                 comm=comm)[1:]


def _pack(pieces):
    flat, spans, off = [], [], 0
    for p in pieces:
        v = p.reshape(-1).astype(F32)
        pad = (-v.shape[0]) % LANES
        if pad:
            v = jnp.concatenate([v, jnp.zeros((pad,), F32)])
        spans.append((off, p.size))
        off += v.shape[0]
        flat.append(v)
    tail = (-off) % (SUBLANES * LANES)
    if tail:
        flat.append(jnp.zeros((tail,), F32))
    return jnp.concatenate(flat).reshape(-1, LANES), spans


def _unpack(buf, span, shape):
    off, size = span
    return buf.reshape(-1)[off:off + size].reshape(shape)


def kernel(x, norm1_g, w_in, w_gate_up, b_gate, conv_w, conv_norm_g, gla_norm_g, w_out, norm2_g, w_ff1, w_ff2, norm_f_g, loss_target, m_norm1_g, m_w_in, m_w_gate_up, m_b_gate, m_conv_w, m_conv_norm_g, m_gla_norm_g, m_w_out, m_norm2_g, m_w_ff1, m_w_ff2, m_norm_f_g, v_norm1_g, v_w_in, v_w_gate_up, v_b_gate, v_conv_w, v_conv_norm_g, v_gla_norm_g, v_w_out, v_norm2_g, v_w_ff1, v_w_ff2, v_norm_f_g):
    xs = x[0]
    target = loss_target[0]
    s_len, d = xs.shape
    d_in = w_in.shape[2] * N_CHIPS
    d_main = d_in - GATE_RANK
    d_ff = w_ff1.shape[2] * N_CHIPS
    cx, cy, cc = _my_place()
    chip = 2 * cx + cy
    chip_core = jnp.stack([chip, cc]).astype(jnp.int32)
    n_ff = d_ff // N_CHIPS
    norm_f = norm_f_g.reshape(1, d)

    n_sh = d_in // N_CHIPS
    wi_buf = _cast_into(w_in[0].T, chip_core, name="cast_w_in")

    small_w, spans_w = _pack([w_gate_up[0], conv_w[0]])
    small_all = _allgather_small(small_w, reduce=False, name="gather_small_weights")
    chips_first = [small_all[2 * j] for j in range(N_CHIPS)]
    wgu_full = jnp.concatenate(
        [_unpack(b, spans_w[0], w_gate_up.shape[1:]) for b in chips_first], axis=1)
    convw_full = jnp.concatenate(
        [_unpack(b, spans_w[1], conv_w.shape[1:]) for b in chips_first], axis=0)
    wgu_pad = jnp.concatenate(
        [wgu_full, jnp.zeros((LANES - GATE_RANK, wgu_full.shape[1]), F32)], axis=0).astype(BF16)
    convw_t = convw_full.T

    u, wo_buf, w1_buf, w2_buf, wi_buf = _rms_fwd(
        xs, norm1_g, name="norm1_fwd", comm=_gather_comm([wi_buf]), place=chip_core,
        side=_cast_side([w_out[0], w_ff1[0], w_ff2[0]], n_steps=s_len // 512, step_of=lambda i, j, k: i))
    wi_t = wi_buf.reshape(d_in, d)
    wg_t = jnp.concatenate([wi_t[d_main:], jnp.zeros((LANES - GATE_RANK, d), BF16)], axis=0)
    z, wo_buf, w1_buf = _matmul(u, wi_t, tb=True, tm=2048, tn=1024, tk=d, out_dtypes=[F32], n_dim=d_main,
                                name="in_proj",
                                comm=_gather_comm([wo_buf, w1_buf], rows=[None, (0, d // 2)]))
    wo_full = wo_buf.reshape(d, d)
    (alow,) = _matmul(u, wg_t, tb=True, tm=1024, tn=LANES, tk=d, out_dtypes=[F32], name="in_proj_gate")
    y0 = _conv_fwd(z, convw_t, conv_norm_g)
    y, o, la, st, w1_cm = _gla_fwd(z, alow, wgu_pad, b_gate, gla_norm_g, y0,
                                   comm=_gather_comm([w1_buf], rows=[(d // 2, d // 2)], mid_at=0.9))
    x2, h = _matmul(y, wo_full, tm=512, tn=d, tk=d, out_dtypes=[F32, BF16], extras=(xs, norm2_g),
                    epilogue=_residual_norm_epilogue, name="out_proj")
    a, p, w2_buf = _matmul(
        h, w1_cm, tm=1024, tn=1024, tk=d, out_dtypes=[BF16, BF16], n_dim=d_ff,
        b_spec=pl.BlockSpec((None, d, 1024), lambda i, j, k: (j // 2, 0, j % 2)),
        epilogue=lambda acc: (acc, jnp.square(jnp.maximum(acc, 0.0))), name="ff1",
        comm=_gather_comm([w2_buf]))
    w2_full = w2_buf.reshape(d_ff, d)
    (x3,) = _matmul(p, w2_full, tm=1024, tn=1024, tk=2048, out_dtypes=[F32], extras=(x2,),
                    epilogue=_add_epilogue, name="ff2")
    dx3b, g_normf, loss_part = _loss_head(x3, norm_f, target)

    (da,) = _matmul(dx3b, w2_full, tb=True, tm=1024, tn=1024, tk=d, out_dtypes=[BF16], extras=(a,),
                    epilogue=lambda acc, av: (acc * (2.0 * jnp.maximum(av, 0.0)),), name="ff2_dx")
    (dh,) = _matmul(
        da, w1_cm, tb=True, tm=2048, tn=1024, tk=2048, out_dtypes=[BF16], n_dim=d,
        b_spec=pl.BlockSpec((None, 1024, 2048), lambda i, j, k: (k, j, 0)), name="ff1_dx")
    dw_k = s_len // 1024
    g_w2, dx2, g_norm2 = _matmul(
        p, dx3b, ta=True, tm=1024, tn=d, tk=1024, out_dtypes=[BF16], name="ff2_dw",
        side=_rms_bwd_side(dh, x2, norm2_g, dx3b, n_steps=(d_ff // 1024) * dw_k,
                           block_of_step=lambda i, j, k: i * dw_k + k))
    (g_w1,) = _matmul(
        h, da, ta=True, tm=1024, tn=n_ff, tk=2048, out_dtypes=[BF16], name="ff1_dw",
        out_shapes=[jax.ShapeDtypeStruct((N_CHIPS, d, n_ff), BF16)],
        out_specs=[pl.BlockSpec((None, 1024, n_ff), lambda i, j, k: (j, i, 0))])
    g_w2 = g_w2.reshape(N_CHIPS, n_ff, d)
    dy, t_w1, t_w2 = _matmul(dx2, wo_full, tb=True, tm=1024, tn=1024, tk=d, out_dtypes=[BF16],
                             name="out_proj_dx", comm=_swap_comm([g_w1, g_w2]))
    ow_k = s_len // 512
    g_wo, p_w1, p_w2 = _matmul(
        y, dx2, ta=True, tm=1024, tn=d, tk=512, out_dtypes=[BF16], name="out_proj_dw",
        side=_add_own_half_side([g_w1, g_w2], [t_w1, t_w2], n_steps=(d // 1024) * ow_k,
                                step_of=lambda i, j, k: i * ow_k + k), place=chip_core)
    g_wo = g_wo.reshape(N_CHIPS, d // N_CHIPS, d)
    conv_tiles = s_len // 512
    dz0, g_convw_t, g_convg = _run_side(
        _conv_bwd_side(dy, z, convw_t, conv_norm_g, n_steps=conv_tiles, step_of=lambda i, j, k: i),
        n_steps=conv_tiles, name="conv_bwd")
    dz, dalow, g_wgu_pad, g_bg, g_gg, l_w2 = _gla_bwd(dy, z, o, la, st, alow, wgu_pad, gla_norm_g, dz0,
                                                      comm=_exchange_comm([p_w2]))
    dwi_k = s_len // 2048
    g_wi_t, m_w2, l_w1 = _matmul(
        dz, u, ta=True, tm=1024, tn=d, tk=2048, out_dtypes=[BF16], name="in_proj_dw",
        out_shapes=[jax.ShapeDtypeStruct((d_in, d), BF16)], comm=_exchange_comm([p_w1]),
        side=_sum_chips_side(p_w2, l_w2, n_blocks=_pow2_below(d_main // 1024 * dwi_k),
                             step_of=lambda i, j, k: i * dwi_k + k),
        place=chip_core)
    g_wi_t, m_w1 = _matmul(
        dalow, u, ta=True, tm=LANES, tn=d, tk=1024, out_dtypes=[BF16],
        epilogue=lambda acc: (acc[:GATE_RANK],), into=g_wi_t, name="in_proj_gate_dw",
        out_shapes=[jax.ShapeDtypeStruct((d_in, d), BF16)],
        out_specs=[pl.BlockSpec((GATE_RANK, d), lambda i, j, k: (d_main // GATE_RANK, 0))],
        side=_sum_chips_side(p_w1, l_w1, n_blocks=_pow2_below(s_len // 1024), step_of=lambda i, j, k: k),
        place=chip_core)
    g_wi = g_wi_t.reshape(N_CHIPS, n_sh, d)
    t_wi, t_wo = _standalone(_swap_comm([g_wi, g_wo]), name="swap_w_in_w_out")
    p_wi = _add_own_half(g_wi, t_wi, chip_core, name="pre_reduce_w_in")
    p_wo = _add_own_half(g_wo, t_wo, chip_core, name="pre_reduce_w_out")
    du, l_wi, l_wo, m_w1, m_w2 = _matmul(
        dz, wi_t, tm=1024, tn=1024, tk=2048, out_dtypes=[BF16], n_dim=d, extras=(dalow, wg_t),
        epilogue=lambda acc, dal, wg: (acc + jnp.dot(dal.astype(BF16), wg, preferred_element_type=F32),),
        name="in_proj_dx", comm=[_exchange_comm([p_wi, p_wo]), _join_comm([m_w1, m_w2])])
    m_wi = _sum_chips(p_wi, l_wi, chip_core, name="reduce_w_in")
    m_wo = _sum_chips(p_wo, l_wo, chip_core, name="reduce_w_out")
    grad_x, g_norm1 = _rms_bwd(du, xs, norm1_g, dx2, name="norm1_bwd")
    m_wi, m_wo = _standalone(_join_comm([m_wi, m_wo]), name="join_w_in_w_out")
    g_big = [m_wi, m_wo, m_w1, m_w2]

    small_g, spans_g = _pack([g_norm1, g_wgu_pad[:GATE_RANK], g_bg, g_convw_t, g_convg, g_gg, g_norm2,
                              g_normf, loss_part[:, :1]])
    tot = _allgather_small(small_g, reduce=True, name="reduce_small_grads")
    t_norm1 = _unpack(tot, spans_g[0], (1, d))
    t_wgu = _unpack(tot, spans_g[1], (GATE_RANK, HEADS * DK))
    t_bg = _unpack(tot, spans_g[2], (1, HEADS * DK))
    t_convw = _unpack(tot, spans_g[3], (3, D_CONV)).T
    t_convg = _unpack(tot, spans_g[4], (1, D_CONV))
    t_gg = _unpack(tot, spans_g[5], (1, DV))
    t_norm2 = _unpack(tot, spans_g[6], (1, d))
    t_normf = _unpack(tot, spans_g[7], (1, d))
    loss = _unpack(tot, spans_g[8], ())
    n_gu = w_gate_up.shape[2]
    n_cw = conv_w.shape[1]
    t_wgu = lax.dynamic_slice(t_wgu, (0, chip * n_gu), (GATE_RANK, n_gu))
    t_convw = lax.dynamic_slice(t_convw, (chip * n_cw, 0), (n_cw, 3))

    order = ["norm1_g", "w_in", "w_gate_up", "b_gate", "conv_w", "conv_norm_g", "gla_norm_g", "w_out",
             "norm2_g", "w_ff1", "w_ff2", "norm_f_g"]
    weights = dict(norm1_g=norm1_g, w_in=w_in, w_gate_up=w_gate_up, b_gate=b_gate, conv_w=conv_w,
                   conv_norm_g=conv_norm_g, gla_norm_g=gla_norm_g, w_out=w_out, norm2_g=norm2_g,
                   w_ff1=w_ff1, w_ff2=w_ff2, norm_f_g=norm_f_g)
    moms = dict(norm1_g=m_norm1_g, w_in=m_w_in, w_gate_up=m_w_gate_up, b_gate=m_b_gate, conv_w=m_conv_w,
                conv_norm_g=m_conv_norm_g, gla_norm_g=m_gla_norm_g, w_out=m_w_out, norm2_g=m_norm2_g,
                w_ff1=m_w_ff1, w_ff2=m_w_ff2, norm_f_g=m_norm_f_g)
    vels = dict(norm1_g=v_norm1_g, w_in=v_w_in, w_gate_up=v_w_gate_up, b_gate=v_b_gate, conv_w=v_conv_w,
                conv_norm_g=v_conv_norm_g, gla_norm_g=v_gla_norm_g, w_out=v_w_out, norm2_g=v_norm2_g,
                w_ff1=v_w_ff1, w_ff2=v_w_ff2, norm_f_g=v_norm_f_g)
    grads2d = dict(norm1_g=t_norm1, w_in=g_big[0], w_gate_up=t_wgu, b_gate=t_bg, conv_w=t_convw,
                   conv_norm_g=t_convg, gla_norm_g=t_gg, w_out=g_big[1], norm2_g=t_norm2,
                   w_ff1=g_big[2], w_ff2=g_big[3], norm_f_g=t_normf)
    out_g, out_d, out_m, out_v = [], [], [], []
    for nm in order:
        w = weights[nm]
        g2 = grads2d[nm]
        if nm == "w_in":
            to2d, back = (lambda t: t[0].T), (lambda t: t.T.reshape(w.shape))
        else:
            to2d, back = (lambda t: t.reshape(g2.shape)), (lambda t: t.reshape(w.shape))
        res = _adamw(to2d(w), g2, to2d(moms[nm]), to2d(vels[nm]), name="adamw_" + nm)
        for lst, r in zip((out_g, out_d, out_m, out_v), res):
            lst.append(back(r))
    return (loss, grad_x.reshape(x.shape), *out_g, *out_d, *out_m, *out_v)
```

```python
import functools

import jax
import jax.numpy as jnp
from jax import lax
from jax.experimental import pallas as pl
from jax.experimental.pallas import tpu as pltpu

F32 = jnp.float32
BF16 = jnp.bfloat16
MESH = pl.DeviceIdType.MESH

EPS = 1e-6
CHUNK = 64
HEADS = 4
DK = 128
DV = 256
D_CONV = 1024
GROUP = 128
GATE_RANK = 16
LANES = 128
SUBLANES = 8
N_CHIPS = 4
N_DEV = 8
CHIP_MASKS = ((1, 0), (0, 1), (1, 1))

ADAM_LR = 0.001
ADAM_B1 = 0.9
ADAM_B2 = 0.999
ADAM_EPS = 1e-08
ADAM_WD = 0.01
ADAM_STEP = 10

VMEM_LIMIT = 56 * 1024 * 1024


def _params(*sem):
    return pltpu.CompilerParams(dimension_semantics=tuple(sem), vmem_limit_bytes=VMEM_LIMIT)


def _rowsum8(v):
    r, c = v.shape
    return jnp.sum(v.reshape(r // SUBLANES, SUBLANES, c), axis=0)


def _tile(rows, cols):
    for cand in (256, 128, 64, 32, 16, 8):
        if rows % cand == 0 and rows > cand:
            return cand, cols
    if rows * cols * 4 > (2 << 20) and cols % 256 == 0:
        return rows, 256
    return rows, cols


class _Comm:
    def __init__(self, ins, outs, aliases, n_sems, first, last, mid=None, mid_at=0.75):
        self.ins = list(ins)
        self.outs = list(outs)
        self.aliases = dict(aliases)
        self.n_sems = n_sems
        self.first = first
        self.mid = mid
        self.mid_at = mid_at
        self.last = last


class _PlaceSpec:
    def __init__(self, block_shape, index_map):
        self.block_shape, self.index_map = block_shape, index_map


def _call(body, *, grid, in_specs, out_specs, out_shape, operands, name, scratch_shapes=(), sem=None,
          aliases=None, comm=None, place=None):
    aliases = dict(aliases or {})
    comms = [] if comm is None else (list(comm) if isinstance(comm, (list, tuple)) else [comm])
    n_in, n_out, n_scr = len(in_specs), len(out_specs), len(scratch_shapes)
    c_ins_all = [a for cm in comms for a in cm.ins]
    c_outs_all = [o for cm in comms for o in cm.outs]
    n_ci, n_co = len(c_ins_all), len(c_outs_all)
    n_place = 0 if place is None else 1

    def adapt(spec):
        if isinstance(spec, _PlaceSpec):
            return pl.BlockSpec(spec.block_shape, spec.index_map)
        if place is None or spec.index_map is None:
            return spec
        return pl.BlockSpec(spec.block_shape, lambda *a, f=spec.index_map: f(*a[:-1]))

    def full_body(*refs):
        refs = refs[n_place:]
        ins = refs[:n_in]
        o0 = n_in + n_ci
        outs = refs[o0:o0 + n_out]
        s0 = o0 + n_out + n_co
        scr = refs[s0:s0 + n_scr]
        sems = refs[s0 + n_scr:]
        parts, i_at, o_at = [], n_in, o0 + n_out
        for q, cm in enumerate(comms):
            parts.append((cm, refs[i_at:i_at + len(cm.ins)], refs[o_at:o_at + len(cm.outs)],
                          sems[2 * q], sems[2 * q + 1]))
            i_at += len(cm.ins)
            o_at += len(cm.outs)
        step = functools.reduce(lambda acc, ig: acc * ig[1] + pl.program_id(ig[0]), enumerate(grid), 0)
        n_steps = functools.reduce(lambda acc, g: acc * g, grid, 1)
        @pl.when(step == 0)
        def _():
            for cm, c_ins, c_outs, ss, rs in parts:
                cm.first(c_ins, c_outs, ss, rs)

        def mid_step(cm):
            ms = int(cm.mid_at * n_steps)
            return ms if 0 < ms < n_steps - 1 else None

        for cm, c_ins, c_outs, ss, rs in parts:
            if cm.mid is not None and mid_step(cm) is not None:
                @pl.when(step == mid_step(cm))
                def _(cm=cm, c_ins=c_ins, c_outs=c_outs, ss=ss, rs=rs):
                    cm.mid(c_ins, c_outs, ss, rs)

        body(*ins, *outs, *scr)

        @pl.when(step == n_steps - 1)
        def _():
            for cm, c_ins, c_outs, ss, rs in parts:
                if cm.mid is not None and mid_step(cm) is None:
                    cm.mid(c_ins, c_outs, ss, rs)
                cm.last(c_ins, c_outs, ss, rs)

    any_spec = pl.BlockSpec(memory_space=pl.ANY)
    i_at, o_at, sem_shapes = n_in, n_out, []
    for cm in comms:
        for i_in, i_out in cm.aliases.items():
            aliases[i_at + i_in] = o_at + i_out
        i_at += len(cm.ins)
        o_at += len(cm.outs)
        sem_shapes += [pltpu.SemaphoreType.DMA((cm.n_sems,)), pltpu.SemaphoreType.DMA((cm.n_sems,))]
    specs = dict(grid=grid, in_specs=[adapt(s) for s in in_specs] + [any_spec] * n_ci,
                 out_specs=[adapt(s) for s in out_specs] + [any_spec] * n_co,
                 scratch_shapes=list(scratch_shapes) + sem_shapes)
    if place is not None:
        specs = dict(grid_spec=pltpu.PrefetchScalarGridSpec(num_scalar_prefetch=1, **specs))
    if comms:
        params = pltpu.CompilerParams(dimension_semantics=("arbitrary",) * len(grid),
                                      vmem_limit_bytes=VMEM_LIMIT, has_side_effects=True)
    else:
        params = _params(*(sem or ("arbitrary",) * len(grid)))
    return pl.pallas_call(
        full_body, out_shape=list(out_shape) + c_outs_all,
        input_output_aliases={k + n_place: v for k, v in aliases.items()},
        compiler_params=params, name=name, **specs,
    )(*(() if place is None else (place,)), *operands, *c_ins_all)


def _matmul(a, b, *, ta=False, tb=False, tm, tn, tk, out_dtypes, name, extras=(), epilogue=None,
            out_shapes=None, out_specs=None, b_spec=None, n_dim=None, into=None, side=None, comm=None,
            place=None):
    n_into = 0 if into is None else 1
    if ta:
        k_dim, m_dim = a.shape
    else:
        m_dim, k_dim = a.shape
    if n_dim is None:
        n_dim = b.shape[0] if tb else b.shape[1]
        assert (b.shape[1] if tb else b.shape[0]) == k_dim
    assert m_dim % tm == 0 and n_dim % tn == 0 and k_dim % tk == 0, (name, a.shape, b.shape)
    nk = k_dim // tk
    n_ex, n_out = len(extras), len(out_dtypes)
    dims = (((0 if ta else 1,), (1 if tb else 0,)), ((), ()))
    grid = (m_dim // tm, n_dim // tn, nk)
    s_ins, s_outs, s_scr = (len(side.ins), len(side.outs), len(side.scratch)) if side else (0, 0, 0)

    def body(*refs):
        a_ref, b_ref = refs[0], refs[1]
        ex_refs = refs[2:2 + n_ex]
        i0 = 2 + n_ex + n_into
        side_in = refs[i0:i0 + s_ins]
        o_refs = refs[i0 + s_ins:i0 + s_ins + n_out]
        side_out = refs[i0 + s_ins + n_out:i0 + s_ins + n_out + s_outs]
        side_scr = refs[len(refs) - s_scr:] if s_scr else ()
        if side is not None:
            step = (pl.program_id(0) * grid[1] + pl.program_id(1)) * grid[2] + pl.program_id(2)

            @pl.when(step == 0)
            def _():
                side.init(side_scr)

        def dot():
            if side is not None:
                side.body(step, side_in, side_out, side_scr)
            return lax.dot_general(a_ref[...].astype(BF16), b_ref[...].astype(BF16), dims,
                                   preferred_element_type=F32)

        def finish(acc):
            outs = epilogue(acc, *[e[...] for e in ex_refs]) if epilogue is not None else (acc,)
            for o_ref, o in zip(o_refs, outs):
                o_ref[...] = o.astype(o_ref.dtype)

        if nk == 1:
            finish(dot())
        else:
            acc_ref = refs[len(refs) - s_scr - 1]
            k = pl.program_id(2)

            @pl.when(k == 0)
            def _():
                acc_ref[...] = dot()

            @pl.when(jnp.logical_and(k > 0, k < nk - 1))
            def _():
                acc_ref[...] += dot()

            @pl.when(k == nk - 1)
            def _():
                finish(acc_ref[...] + dot())

    a_spec = (pl.BlockSpec((tk, tm), lambda i, j, k: (k, i)) if ta
              else pl.BlockSpec((tm, tk), lambda i, j, k: (i, k)))
    if b_spec is None:
        b_spec = (pl.BlockSpec((tn, tk), lambda i, j, k: (j, k)) if tb
                  else pl.BlockSpec((tk, tn), lambda i, j, k: (k, j)))
    io_spec = pl.BlockSpec((tm, tn), lambda i, j, k: (i, j))

    def extra_spec(e):
        if e.shape == (m_dim, n_dim):
            return io_spec
        if e.shape[1] == n_dim:
            return pl.BlockSpec((e.shape[0], tn), lambda i, j, k: (0, j))
        assert e.shape[0] == m_dim, (name, e.shape)
        return pl.BlockSpec((tm, e.shape[1]), lambda i, j, k: (i, 0))

    if out_shapes is None:
        out_shapes = [jax.ShapeDtypeStruct((m_dim, n_dim), dt) for dt in out_dtypes]
    if out_specs is None:
        out_specs = [io_spec] * n_out
    return _call(
        body,
        grid=grid,
        in_specs=([a_spec, b_spec] + [extra_spec(e) for e in extras]
                  + [pl.BlockSpec(memory_space=pl.ANY)] * n_into
                  + (list(side.in_specs) if side else [])),
        out_specs=list(out_specs) + (list(side.out_specs) if side else []),
        out_shape=list(out_shapes) + (list(side.outs) if side else []),
        scratch_shapes=([pltpu.VMEM((tm, tn), F32)] if nk > 1 else []) + (list(side.scratch) if side else []),
        sem=("parallel", "parallel", "arbitrary") if side is None else None,
        aliases={2 + n_ex: 0} if n_into else None,
        operands=(a, b, *extras) + ((into,) if n_into else ()) + (tuple(side.ins) if side else ()),
        name=name, comm=comm, place=place)


def _add_epilogue(acc, r):
    return (acc + r,)


def _residual_norm_epilogue(acc, x, g):
    x2 = acc + x
    r = lax.rsqrt(jnp.mean(x2 * x2, axis=-1, keepdims=True) + EPS)
    return x2, x2 * r * g


def _rms_fwd(x, g, *, name, tm=512, comm=None, side=None, place=None):
    s_len, d = x.shape
    s_ins, s_outs = (len(side.ins), len(side.outs)) if side else (0, 0)

    def body(*refs):
        x_ref, g_ref = refs[0], refs[1]
        o_ref = refs[2 + s_ins]
        xv = x_ref[...]
        r = lax.rsqrt(jnp.mean(xv * xv, axis=-1, keepdims=True) + EPS)
        o_ref[...] = (xv * r * g_ref[...]).astype(o_ref.dtype)
        if side is not None:
            scr = refs[3 + s_ins + s_outs:]

            @pl.when(pl.program_id(0) == 0)
            def _():
                side.init(scr)

            side.body(pl.program_id(0), refs[2:2 + s_ins], refs[3 + s_ins:3 + s_ins + s_outs], scr)

    row = pl.BlockSpec((tm, d), lambda i, j, k: (i, 0))
    return _call(
        body, grid=(s_len // tm, 1, 1),
        in_specs=[row, pl.BlockSpec((1, d), lambda i, j, k: (0, 0))] + (list(side.in_specs) if side else []),
        out_specs=[row] + (list(side.out_specs) if side else []),
        out_shape=[jax.ShapeDtypeStruct((s_len, d), BF16)] + (list(side.outs) if side else []),
        scratch_shapes=list(side.scratch) if side else [],
        operands=(x, g) + (tuple(side.ins) if side else ()), name=name, comm=comm, place=place)


NORM_ROWS = 16


def _zero_refs(refs):
    for r in refs:
        r[...] = jnp.zeros_like(r)


def _rms_bwd_block(dn_ref, x_ref, g_ref, res_ref, dx_ref, dg_ref, acc_ref):
    gv = g_ref[...]
    acc = acc_ref[...]
    for s in range(x_ref.shape[0] // NORM_ROWS):
        sl = slice(s * NORM_ROWS, (s + 1) * NORM_ROWS)
        xv = x_ref[sl, :]
        dnv = dn_ref[sl, :].astype(F32)
        r = lax.rsqrt(jnp.mean(xv * xv, axis=-1, keepdims=True) + EPS)
        xh = xv * r
        acc = acc + _rowsum8(dnv * xh)
        dxh = dnv * gv
        dx_ref[sl, :] = (r * (dxh - xh * jnp.mean(dxh * xh, axis=-1, keepdims=True))
                         + res_ref[sl, :].astype(F32))
    acc_ref[...] = acc
    dg_ref[...] = jnp.sum(acc, axis=0, keepdims=True)


class _Side:
    def __init__(self, ins, in_specs, outs, out_specs, scratch, init, body):
        self.ins, self.in_specs, self.outs, self.out_specs = ins, in_specs, outs, out_specs
        self.scratch = scratch
        self.init = init
        self.body = body


def _rms_bwd_side(dn, x, g, res, *, block_of_step, n_steps):
    s_len, d = x.shape
    row = pl.BlockSpec((s_len // n_steps, d), lambda i, j, k: (block_of_step(i, j, k), 0))
    vec = pl.BlockSpec((1, d), lambda i, j, k: (0, 0))
    return _Side(
        ins=[dn, x, g, res], in_specs=[row, row, vec, row],
        outs=[jax.ShapeDtypeStruct((s_len, d), F32), jax.ShapeDtypeStruct((1, d), F32)],
        out_specs=[row, vec], scratch=[pltpu.VMEM((SUBLANES, d), F32)],
        init=_zero_refs, body=lambda step, ins, outs, scr: _rms_bwd_block(*ins, *outs, *scr))


def _rms_bwd(dn, x, g, res, *, name, tm=512, comm=None):
    s_len, d = x.shape
    n = s_len // tm

    def body(*refs):
        @pl.when(pl.program_id(0) == 0)
        def _():
            _zero_refs(refs[-1:])

        _rms_bwd_block(*refs)

    row = pl.BlockSpec((tm, d), lambda i: (i, 0))
    vec = pl.BlockSpec((1, d), lambda i: (0, 0))
    return _call(
        body, grid=(n,),
        in_specs=[row, row, vec, row],
        out_specs=[row, vec],
        out_shape=[jax.ShapeDtypeStruct((s_len, d), F32), jax.ShapeDtypeStruct((1, d), F32)],
        scratch_shapes=[pltpu.VMEM((SUBLANES, d), F32)],
        operands=(dn, x, g, res), name=name, comm=comm)


def _loss_head(x3, g, target, *, tm=512):
    s_len, d = x3.shape
    n = s_len // tm

    def body(x_ref, g_ref, t_ref, dxb_ref, dg_ref, loss_ref, accg_ref, accl_ref):
        i = pl.program_id(0)
        xv = x_ref[...]
        gv = g_ref[...]
        r = lax.rsqrt(jnp.mean(xv * xv, axis=-1, keepdims=True) + EPS)
        xh = xv * r
        err = xh * gv - t_ref[...]

        @pl.when(i == 0)
        def _():
            accg_ref[...] = jnp.zeros_like(accg_ref)
            accl_ref[...] = jnp.zeros_like(accl_ref)

        accl_ref[...] += _rowsum8(err * err)
        dn = err * (1.0 / d)
        accg_ref[...] += _rowsum8(dn * xh)
        dxh = dn * gv
        dx = r * (dxh - xh * jnp.mean(dxh * xh, axis=-1, keepdims=True))
        dxb_ref[...] = dx.astype(BF16)

        @pl.when(i == n - 1)
        def _():
            dg_ref[...] = jnp.sum(accg_ref[...], axis=0, keepdims=True)
            tot = jnp.sum(jnp.sum(accl_ref[...], axis=0, keepdims=True), axis=1, keepdims=True)
            loss_ref[...] = jnp.broadcast_to(tot * (0.5 / d), (1, LANES))

    row = pl.BlockSpec((tm, d), lambda i: (i, 0))
    vec = pl.BlockSpec((1, d), lambda i: (0, 0))
    return pl.pallas_call(
        body, grid=(n,),
        in_specs=[row, vec, row],
        out_specs=[row, vec, pl.BlockSpec((1, LANES), lambda i: (0, 0))],
        out_shape=[jax.ShapeDtypeStruct((s_len, d), BF16),
                   jax.ShapeDtypeStruct((1, d), F32), jax.ShapeDtypeStruct((1, LANES), F32)],
        scratch_shapes=[pltpu.VMEM((SUBLANES, d), F32), pltpu.VMEM((SUBLANES, d), F32)],
        compiler_params=_params("arbitrary"), name="loss_head",
    )(x3, g, target)


def _shift_down(v, k, rows_before, row):
    out = pltpu.roll(v, k, axis=0)
    for j in range(k):
        out = jnp.where(row == j, rows_before[j], out)
    return out


def _shift_up(v, k, rows_after, row):
    t = v.shape[0]
    out = pltpu.roll(v, t - k, axis=0)
    for j in range(k):
        out = jnp.where(row == t - k + j, rows_after[j], out)
    return out


def _conv_fwd(z, w_t, gain, *, ts=512):
    s_len = z.shape[0]
    n_grp = D_CONV // GROUP

    def body(cb_ref, cc_ref, ch_ref, w_ref, g_ref, y_ref, carry_ref):
        i = pl.program_id(0)

        @pl.when(i == 0)
        def _():
            carry_ref[...] = jnp.zeros_like(carry_ref)

        row = lax.broadcasted_iota(jnp.int32, (ts, GROUP), 0)
        for g in range(n_grp):
            sl = slice(g * GROUP, (g + 1) * GROUP)
            uu = cc_ref[:, sl] * ch_ref[:, sl]
            p2 = carry_ref[6:7, sl]
            p1 = carry_ref[7:8, sl]
            u1 = _shift_down(uu, 1, [p1], row)
            u2 = _shift_down(uu, 2, [p2, p1], row)
            conv = w_ref[0:1, sl] * u2 + w_ref[1:2, sl] * u1 + w_ref[2:3, sl] * uu
            y = cb_ref[:, sl] * conv
            carry_ref[:, sl] = uu[ts - SUBLANES:ts, :]
            rg = lax.rsqrt(jnp.mean(y * y, axis=-1, keepdims=True) + EPS)
            y_ref[:, sl] = (y * rg * g_ref[:, sl]).astype(BF16)

    def col(j):
        return pl.BlockSpec((ts, D_CONV), lambda i, j=j: (i, j))

    small = lambda r: pl.BlockSpec((r, D_CONV), lambda i: (0, 0))
    return pl.pallas_call(
        body, grid=(s_len // ts,),
        in_specs=[col(0), col(1), col(2), small(3), small(1)],
        out_specs=col(0),
        out_shape=jax.ShapeDtypeStruct((s_len, 2 * D_CONV), BF16),
        scratch_shapes=[pltpu.VMEM((SUBLANES, D_CONV), F32)],
        compiler_params=_params("arbitrary"), name="conv_fwd",
    )(z, z, z, w_t, gain)


def _conv_bwd_side(dy, z, w_t, gain, *, n_steps, step_of):
    s_len = z.shape[0]
    n = n_steps
    ts = s_len // n
    n_grp = D_CONV // GROUP
    halo_blocks = ts // SUBLANES

    def body(step, ins, outs, scr):
        dy_ref, cb_ref, cc_ref, ch_ref, hcc_ref, hch_ref, w_ref, g_ref = ins
        dz_ref, dw_ref, dg_ref = outs
        carry_ref, accw_ref, accg_ref = scr
        first_tile = (n - 1 - step) == 0
        row = lax.broadcasted_iota(jnp.int32, (ts, GROUP), 0)
        keep = jnp.where(first_tile, 0.0, 1.0)
        for g in range(n_grp):
            sl = slice(g * GROUP, (g + 1) * GROUP)
            cc = cc_ref[:, sl]
            ch = ch_ref[:, sl]
            cb = cb_ref[:, sl]
            uu = cc * ch
            p2 = hcc_ref[6:7, sl] * hch_ref[6:7, sl] * keep
            p1 = hcc_ref[7:8, sl] * hch_ref[7:8, sl] * keep
            u1 = _shift_down(uu, 1, [p1], row)
            u2 = _shift_down(uu, 2, [p2, p1], row)
            w0, w1, w2 = w_ref[0:1, sl], w_ref[1:2, sl], w_ref[2:3, sl]
            conv = w0 * u2 + w1 * u1 + w2 * uu
            y = cb * conv
            rg = lax.rsqrt(jnp.mean(y * y, axis=-1, keepdims=True) + EPS)
            yh = y * rg
            dyv = dy_ref[:, sl].astype(F32)
            accg_ref[:, sl] += _rowsum8(dyv * yh)
            dyn = dyv * g_ref[:, sl]
            dpre = rg * (dyn - yh * jnp.mean(dyn * yh, axis=-1, keepdims=True))
            dz_ref[:, sl] = (dpre * conv).astype(BF16)
            dconv = dpre * cb
            accw_ref[0:8, sl] += _rowsum8(dconv * u2)
            accw_ref[8:16, sl] += _rowsum8(dconv * u1)
            accw_ref[16:24, sl] += _rowsum8(dconv * uu)
            n0 = carry_ref[0:1, sl]
            n1 = carry_ref[1:2, sl]
            d1 = _shift_up(dconv, 1, [n0], row)
            d2 = _shift_up(dconv, 2, [n0, n1], row)
            duu = w2 * dconv + w1 * d1 + w0 * d2
            carry_ref[:, sl] = dconv[0:SUBLANES, :]
            dz_ref[:, D_CONV + g * GROUP:D_CONV + (g + 1) * GROUP] = (duu * ch).astype(BF16)
            dz_ref[:, 2 * D_CONV + g * GROUP:2 * D_CONV + (g + 1) * GROUP] = (duu * cc).astype(BF16)

        for k in range(3):
            dw_ref[k:k + 1, :] = jnp.sum(accw_ref[8 * k:8 * k + 8, :], axis=0, keepdims=True)
        dg_ref[...] = jnp.sum(accg_ref[...], axis=0, keepdims=True)

    def tile(i, j, k):
        return n - 1 - step_of(i, j, k)

    def col(c):
        return pl.BlockSpec((ts, D_CONV), lambda i, j, k, c=c: (tile(i, j, k), c))

    def halo(c):
        return pl.BlockSpec((SUBLANES, D_CONV),
                            lambda i, j, k, c=c: (jnp.maximum(tile(i, j, k) * halo_blocks - 1, 0), c))

    small = lambda r: pl.BlockSpec((r, D_CONV), lambda i, j, k: (0, 0))
    return _Side(
        ins=[dy, z, z, z, z, z, w_t, gain],
        in_specs=[col(0), col(0), col(1), col(2), halo(1), halo(2), small(3), small(1)],
        outs=[jax.ShapeDtypeStruct((s_len, 6 * D_CONV), BF16),
              jax.ShapeDtypeStruct((3, D_CONV), F32), jax.ShapeDtypeStruct((1, D_CONV), F32)],
        out_specs=[pl.BlockSpec((ts, 3 * D_CONV), lambda i, j, k: (tile(i, j, k), 0)), small(3), small(1)],
        scratch=[pltpu.VMEM((SUBLANES, D_CONV), F32), pltpu.VMEM((24, D_CONV), F32),
                 pltpu.VMEM((SUBLANES, D_CONV), F32)],
        init=_zero_refs, body=body)


def _split3(v):
    hi = v.astype(BF16)
    r1 = v - hi.astype(F32)
    mid = r1.astype(BF16)
    lo = (r1 - mid.astype(F32)).astype(BF16)
    return jnp.concatenate([hi, mid, lo], axis=1)


def _tri_sum(tri, v):
    w = v.shape[1]
    dd = jnp.dot(tri, _split3(v), preferred_element_type=F32)
    return dd[:, :w] + dd[:, w:2 * w] + dd[:, 2 * w:]


def _chunk_masks(ts):
    r = jnp.arange(ts)
    same = (r[:, None] // CHUNK) == (r[None, :] // CHUNK)
    later = jnp.logical_and(same, r[None, :] > r[:, None]).astype(BF16)
    earlier = jnp.logical_and(same, r[None, :] < r[:, None]).astype(BF16)
    chunk_of_row = jnp.arange(ts // CHUNK * SUBLANES)[:, None] // SUBLANES
    member = (chunk_of_row == (r[None, :] // CHUNK)).astype(BF16)
    return later, earlier, member


def _sigmoid(v):
    return 0.5 * jnp.tanh(0.5 * v) + 0.5


def _gla_fwd(z, alow, wgu, bg, gg, y_in, *, ts=512, comm=None):
    s_len = z.shape[0]
    nch = ts // CHUNK
    scale = DK ** -0.5

    tri_u, _, ind8 = _chunk_masks(ts)

    def body(q_ref, k_ref, v_ref, og_ref, al_ref, wgu_ref, bg_ref, gg_ref, tu_ref, ind_ref, yin_ref,
             y_ref, o_ref, la_ref, st_ref, state_ref, kd_ref, qs_ref, dec_ref):
        del yin_ref
        i = pl.program_id(0)

        @pl.when(i == 0)
        def _():
            state_ref[...] = jnp.zeros_like(state_ref)

        pre = jnp.dot(al_ref[...].astype(BF16), wgu_ref[...], preferred_element_type=F32) + bg_ref[...]
        la = (jnp.minimum(pre, 0.0) - jnp.log(1.0 + jnp.exp(-jnp.abs(pre)))) * (1.0 / 16.0)
        la_ref[...] = la
        kd_ref[...] = (k_ref[...] * jnp.exp(_tri_sum(tu_ref[...], la))).astype(BF16)
        qs_ref[...] = (q_ref[...] * scale).astype(BF16)
        dec_ref[...] = jnp.exp(_tri_sum(ind_ref[...], la))

        def chunk(cl, carry):
            rows = pl.ds(pl.multiple_of(cl * CHUNK, CHUNK), CHUNK)
            dec = dec_ref[pl.ds(pl.multiple_of(cl * SUBLANES, SUBLANES), 1), :]
            for h in range(HEADS):
                ks = slice(h * DK, (h + 1) * DK)
                vs = slice(h * DV, (h + 1) * DV)
                kv_t = lax.dot_general(v_ref[rows, vs].astype(BF16), kd_ref[rows, ks],
                                       (((0,), (0,)), ((), ())), preferred_element_type=F32)
                st = state_ref[h] * dec[:, ks] + kv_t
                state_ref[h] = st
                st_ref[cl, h] = st
                o_ref[rows, vs] = lax.dot_general(qs_ref[rows, ks], st.astype(BF16),
                                                  (((1,), (1,)), ((), ())), preferred_element_type=F32)
            return carry

        lax.fori_loop(0, nch, chunk, 0, unroll=2)

        ggv = gg_ref[...]
        for h in range(HEADS):
            vs = slice(h * DV, (h + 1) * DV)
            o_h = o_ref[:, vs]
            og_h = og_ref[:, vs]
            ro = lax.rsqrt(jnp.mean(o_h * o_h, axis=-1, keepdims=True) + EPS)
            y_ref[:, vs] = (o_h * ro * ggv * (og_h * _sigmoid(og_h))).astype(BF16)

    def zcol(width, j):
        return pl.BlockSpec((ts, width), lambda i, j=j: (i, j))

    full = lambda shape: pl.BlockSpec(shape, lambda i: tuple(0 for _ in shape))
    return _call(
        body, grid=(s_len // ts,),
        in_specs=[zcol(512, 6), zcol(512, 7), zcol(1024, 4), zcol(1024, 5), zcol(LANES, 0),
                  full((LANES, 512)), full((1, 512)), full((1, DV)), full(tri_u.shape), full(ind8.shape),
                  pl.BlockSpec(memory_space=pl.ANY)],
        out_specs=[zcol(1024, 1), zcol(1024, 0), zcol(512, 0),
                   pl.BlockSpec((nch, HEADS, DV, DK), lambda i: (i, 0, 0, 0))],
        out_shape=[jax.ShapeDtypeStruct((s_len, 2048), BF16), jax.ShapeDtypeStruct((s_len, 1024), F32),
                   jax.ShapeDtypeStruct((s_len, 512), F32),
                   jax.ShapeDtypeStruct((s_len // CHUNK, HEADS, DV, DK), F32)],
        scratch_shapes=[pltpu.VMEM((HEADS, DV, DK), F32), pltpu.VMEM((ts, 512), BF16),
                        pltpu.VMEM((ts, 512), BF16), pltpu.VMEM((nch * SUBLANES, 512), F32)],
        aliases={10: 0},
        operands=(z, z, z, z, alow, wgu, bg, gg, tri_u, ind8, y_in), name="gla_fwd", comm=comm)


def _gla_bwd(dy, z, o, la, st, alow, wgu, gg, dz_in, *, ts=512, comm=None, side=None, place=None):
    s_len = z.shape[0]
    n = s_len // ts
    nch = ts // CHUNK
    scale = DK ** -0.5

    tri_u, tri_l, ind8 = _chunk_masks(ts)

    def body(dy_ref, q_ref, k_ref, v_ref, og_ref, o_ref, la_ref, st_ref, stp_ref, al_ref, wgu_ref,
             gg_ref, tu_ref, tl_ref, ind_ref, dzin_ref, dz_ref, dal_ref, dwgu_ref, dbg_ref, dgg_ref,
             gt_ref, decn_ref, accw_ref, accb_ref, accg_ref, dla_ref,
             e_ref, kd_ref, kdb_ref, qs_ref, do_ref, dkd_ref, dec_ref, dbe_ref):
        del dzin_ref
        i = pl.program_id(0)
        first_tile = (n - 1 - i) == 0

        @pl.when(i == 0)
        def _():
            gt_ref[...] = jnp.zeros_like(gt_ref)
            decn_ref[...] = jnp.ones_like(decn_ref)
            accw_ref[...] = jnp.zeros_like(accw_ref)
            accb_ref[...] = jnp.zeros_like(accb_ref)
            accg_ref[...] = jnp.zeros_like(accg_ref)

        la = la_ref[...]
        e_dec = jnp.exp(_tri_sum(tu_ref[...], la))
        e_ref[...] = e_dec
        kd = k_ref[...] * e_dec
        kd_ref[...] = kd
        kdb_ref[...] = kd.astype(BF16)
        qs_ref[...] = (q_ref[...] * scale).astype(BF16)
        dec_ref[...] = jnp.exp(_tri_sum(ind_ref[...], la))
        ggv = gg_ref[...]
        for h in range(HEADS):
            vs = slice(h * DV, (h + 1) * DV)
            o_h = o_ref[:, vs]
            og_h = og_ref[:, vs]
            dy_h = dy_ref[:, vs].astype(F32)
            ro = lax.rsqrt(jnp.mean(o_h * o_h, axis=-1, keepdims=True) + EPS)
            oh = o_h * ro
            sig = _sigmoid(og_h)
            sil = og_h * sig
            accg_ref[...] += _rowsum8(dy_h * oh * sil)
            dz_ref[:, 2048 + h * DV:2048 + (h + 1) * DV] = (
                dy_h * oh * ggv * sig * (1.0 + og_h * (1.0 - sig))).astype(BF16)
            don = dy_h * ggv * sil
            do_ref[:, vs] = (ro * (don - oh * jnp.mean(don * oh, axis=-1, keepdims=True))).astype(BF16)
        keep = jnp.where(first_tile, 0.0, 1.0)

        def chunk(jrev, decn):
            cl = nch - 1 - jrev
            rows = pl.ds(pl.multiple_of(cl * CHUNK, CHUNK), CHUNK)
            one_row = pl.ds(pl.multiple_of(cl * SUBLANES, SUBLANES), 1)
            dec = dec_ref[one_row, :]
            has_prev = jnp.where(cl > 0, 1.0, 0.0)
            prev_idx = jnp.maximum(cl - 1, 0)
            for h in range(HEADS):
                ks = slice(h * DK, (h + 1) * DK)
                vs = slice(h * DV, (h + 1) * DV)
                dob = do_ref[rows, vs]
                s_c = st_ref[cl, h]
                dqs = jnp.dot(dob, s_c.astype(BF16), preferred_element_type=F32)
                dz_ref[rows, ks] = (dqs * scale).astype(BF16)
                gt = gt_ref[h] * decn[:, ks] + lax.dot_general(
                    dob, qs_ref[rows, ks], (((0,), (0,)), ((), ())), preferred_element_type=F32)
                gt_ref[h] = gt
                gb = gt.astype(BF16)
                dkd_ref[rows, ks] = jnp.dot(v_ref[rows, vs].astype(BF16), gb, preferred_element_type=F32)
                dz_ref[rows, 1024 + h * DV:1024 + (h + 1) * DV] = lax.dot_general(
                    kdb_ref[rows, ks], gb, (((1,), (1,)), ((), ())),
                    preferred_element_type=F32).astype(BF16)
                s_prev = has_prev * st_ref[prev_idx, h] + (1.0 - has_prev) * keep * stp_ref[0, h]
                dbe_ref[one_row, ks] = jnp.sum(gt * s_prev, axis=0, keepdims=True) * dec[:, ks]
            return dec

        decn_ref[0:1, :] = lax.fori_loop(0, nch, chunk, decn_ref[0:1, :], unroll=2)

        dkd = dkd_ref[...]
        dz_ref[:, 512:1024] = (dkd * e_ref[...]).astype(BF16)
        dla_ref[...] = _tri_sum(tl_ref[...], dkd * kd_ref[...])
        for c in range(nch):
            dla_ref[c * CHUNK:(c + 1) * CHUNK, :] += dbe_ref[c * SUBLANES:c * SUBLANES + 1, :]
        dpre = dla_ref[...] * (1.0 / 16.0) * (1.0 - jnp.exp(16.0 * la))
        accb_ref[...] += _rowsum8(dpre)
        dpb = dpre.astype(BF16)
        accw_ref[...] += lax.dot_general(al_ref[...].astype(BF16), dpb, (((0,), (0,)), ((), ())),
                                         preferred_element_type=F32)
        dal_ref[...] = lax.dot_general(dpb, wgu_ref[...], (((1,), (1,)), ((), ())),
                                       preferred_element_type=F32)

        @pl.when(i == n - 1)
        def _():
            dwgu_ref[...] = accw_ref[...]
            dbg_ref[...] = jnp.sum(accb_ref[...], axis=0, keepdims=True)
            dgg_ref[...] = jnp.sum(accg_ref[...], axis=0, keepdims=True)

    def zcol(width, j):
        return pl.BlockSpec((ts, width), lambda i, j=j: (n - 1 - i, j))

    full = lambda shape: pl.BlockSpec(shape, lambda i: tuple(0 for _ in shape))
    n_in, n_out, n_scr = 16, 5, 14
    s_ins, s_outs = (len(side.ins), len(side.outs)) if side else (0, 0)

    def body_with_side(*refs):
        outs_at, scr_at = n_in + s_ins, n_in + s_ins + n_out + s_outs
        if side is not None:
            side_scr = refs[scr_at + n_scr:]

            @pl.when(pl.program_id(0) == 0)
            def _():
                side.init(side_scr)

            side.body(pl.program_id(0), refs[n_in:outs_at], refs[outs_at + n_out:scr_at], side_scr)
        body(*refs[:n_in], *refs[outs_at:outs_at + n_out], *refs[scr_at:scr_at + n_scr])

    def one_axis(spec):
        return type(spec)(spec.block_shape, lambda i, *p, f=spec.index_map: f(i, 0, 0, *p))

    return _call(
        body_with_side, grid=(n,),
        in_specs=[zcol(1024, 1), zcol(512, 6), zcol(512, 7), zcol(1024, 4), zcol(1024, 5),
                  zcol(1024, 0), zcol(512, 0),
                  pl.BlockSpec((nch, HEADS, DV, DK), lambda i: (n - 1 - i, 0, 0, 0)),
                  pl.BlockSpec((1, HEADS, DV, DK),
                               lambda i: (jnp.maximum((n - 1 - i) * nch - 1, 0), 0, 0, 0)),
                  zcol(LANES, 0), full((LANES, 512)), full((1, DV)),
                  full(tri_u.shape), full(tri_l.shape), full(ind8.shape),
                  pl.BlockSpec(memory_space=pl.ANY)] + ([one_axis(s) for s in side.in_specs] if side else []),
        out_specs=([zcol(3072, 1), zcol(LANES, 0), full((LANES, 512)), full((1, 512)), full((1, DV))]
                   + ([one_axis(s) for s in side.out_specs] if side else [])),
        out_shape=[jax.ShapeDtypeStruct((s_len, 6144), BF16), jax.ShapeDtypeStruct((s_len, LANES), F32),
                   jax.ShapeDtypeStruct((LANES, 512), F32), jax.ShapeDtypeStruct((1, 512), F32),
                   jax.ShapeDtypeStruct((1, DV), F32)] + (list(side.outs) if side else []),
        scratch_shapes=[pltpu.VMEM((HEADS, DV, DK), F32), pltpu.VMEM((SUBLANES, 512), F32),
                        pltpu.VMEM((LANES, 512), F32), pltpu.VMEM((SUBLANES, 512), F32),
                        pltpu.VMEM((SUBLANES, DV), F32), pltpu.VMEM((ts, 512), F32),
                        pltpu.VMEM((ts, 512), F32), pltpu.VMEM((ts, 512), F32), pltpu.VMEM((ts, 512), BF16),
                        pltpu.VMEM((ts, 512), BF16), pltpu.VMEM((ts, 1024), BF16),
                        pltpu.VMEM((ts, 512), F32), pltpu.VMEM((nch * SUBLANES, 512), F32),
                        pltpu.VMEM((nch * SUBLANES, 512), F32)] + (list(side.scratch) if side else []),
        aliases={15: 0},
        operands=((dy, z, z, z, z, o, la, st, st, alow, wgu, gg, tri_u, tri_l, ind8, dz_in)
                  + (tuple(side.ins) if side else ())),
        name="gla_bwd", comm=comm, place=place)


def _adamw(w, g, m, v, *, name):
    rows, cols = w.shape
    tr, tc = _tile(rows, cols)

    def body(w_ref, g_ref, m_ref, v_ref, go_ref, d_ref, nm_ref, nv_ref):
        gv = g_ref[...]
        go_ref[...] = gv
        m2 = ADAM_B1 * m_ref[...] + (1.0 - ADAM_B1) * gv
        v2 = ADAM_B2 * v_ref[...] + (1.0 - ADAM_B2) * jnp.square(gv)
        m_hat = m2 / (1.0 - ADAM_B1 ** ADAM_STEP)
        v_hat = v2 / (1.0 - ADAM_B2 ** ADAM_STEP)
        d_ref[...] = -ADAM_LR * (m_hat / (jnp.sqrt(v_hat) + ADAM_EPS) + ADAM_WD * w_ref[...])
        nm_ref[...] = m2
        nv_ref[...] = v2

    blk = pl.BlockSpec((tr, tc), lambda i, j: (i, j))
    shp = jax.ShapeDtypeStruct((rows, cols), F32)
    return pl.pallas_call(
        body, grid=(rows // tr, cols // tc), in_specs=[blk] * 4, out_specs=[blk] * 4, out_shape=[shp] * 4,
        compiler_params=_params("parallel", "parallel"), name=name,
    )(w, g, m, v)


def _my_place():
    return lax.axis_index("x"), lax.axis_index("y"), lax.axis_index("c")


def _flip(v, bit):
    return 1 - v if bit else v


def _allgather_small(buf, *, reduce, name):
    rows = buf.shape[0]

    def body(in_ref, out_ref, gat_ref, send_sems, recv_sems):
        x, y, c = _my_place()
        me = 4 * x + 2 * y + c
        gat_ref[me] = in_ref[...]
        copies = []
        for m in range(1, N_DEV):
            peer = (_flip(x, m & 4), _flip(y, m & 2), _flip(c, m & 1))
            cp = pltpu.make_async_remote_copy(
                src_ref=in_ref, dst_ref=gat_ref.at[me],
                send_sem=send_sems.at[m - 1], recv_sem=recv_sems.at[m - 1],
                device_id=peer, device_id_type=MESH)
            cp.start()
            copies.append(cp)
        for m in range(1, N_DEV):
            px, py, pc = _flip(x, m & 4), _flip(y, m & 2), _flip(c, m & 1)
            src_slot = gat_ref.at[4 * px + 2 * py + pc]
            pltpu.make_async_remote_copy(
                src_ref=src_slot, dst_ref=src_slot,
                send_sem=send_sems.at[m - 1], recv_sem=recv_sems.at[m - 1],
                device_id=(px, py, pc), device_id_type=MESH).wait_recv()
        for cp in copies:
            cp.wait_send()
        if reduce:
            tot = gat_ref[0]
            for d in range(1, N_DEV):
                tot = tot + gat_ref[d]
            out_ref[...] = tot
        else:
            out_ref[...] = gat_ref[...]

    out_shape = (rows, LANES) if reduce else (N_DEV, rows, LANES)
    return pl.pallas_call(
        body,
        in_specs=[pl.BlockSpec(memory_space=pltpu.VMEM)],
        out_specs=pl.BlockSpec(memory_space=pltpu.VMEM),
        out_shape=jax.ShapeDtypeStruct(out_shape, F32),
        scratch_shapes=[pltpu.VMEM((N_DEV, rows, LANES), F32),
                        pltpu.SemaphoreType.DMA((N_DEV - 1,)), pltpu.SemaphoreType.DMA((N_DEV - 1,))],
        compiler_params=pltpu.CompilerParams(has_side_effects=True),
        name=name,
    )(buf)


def _cast_into(shard, chip_core, *, name):
    rows, cols = shard.shape
    tr, tc = _tile(rows, cols)

    def body(cc_ref, s_ref, o_ref):
        del cc_ref
        o_ref[...] = s_ref[...].astype(BF16)

    grid_spec = pltpu.PrefetchScalarGridSpec(
        num_scalar_prefetch=1, grid=(rows // tr, cols // tc),
        in_specs=[pl.BlockSpec((tr, tc), lambda r, q, cc: (r, q))],
        out_specs=pl.BlockSpec((None, tr, tc), lambda r, q, cc: (cc[0], r, q)))
    return pl.pallas_call(
        body, grid_spec=grid_spec, out_shape=jax.ShapeDtypeStruct((N_CHIPS, rows, cols), BF16),
        compiler_params=_params("arbitrary", "arbitrary"), name=name,
    )(chip_core, shard)


def _remote(src, dst, send_sems, recv_sems, k, device):
    return pltpu.make_async_remote_copy(src_ref=src, dst_ref=dst, send_sem=send_sems.at[k],
                                        recv_sem=recv_sems.at[k], device_id=device, device_id_type=MESH)


def _col_half(ref, h, *lead, rows=None):
    hc = ref.shape[-1] // 2
    mid = (slice(None),) * (len(ref.shape) - 2 - len(lead))
    row_sel = slice(None) if rows is None else pl.ds(rows[0], rows[1])
    return ref.at[tuple(lead) + mid + (row_sel, pl.ds(h * hc, hc))]


def _gather_comm(bufs, rows=None, mid_at=0.75):
    n_w, n_m = len(bufs), len(CHIP_MASKS)
    rows = rows or [None] * n_w

    def first(c_ins, c_outs, ss, rs):
        x, y, c = _my_place()
        chip = 2 * x + y
        for w in range(n_w):
            mine = _col_half(c_outs[w], c, chip, rows=rows[w])
            for mi, (mx, my) in enumerate(CHIP_MASKS):
                _remote(mine, mine, ss, rs, w * n_m + mi, (_flip(x, mx), _flip(y, my), c)).start()

    def mid(c_ins, c_outs, ss, rs):
        x, y, c = _my_place()
        for w in range(n_w):
            for mi, (mx, my) in enumerate(CHIP_MASKS):
                k = w * n_m + mi
                px, py = _flip(x, mx), _flip(y, my)
                landed = _col_half(c_outs[w], c, 2 * px + py, rows=rows[w])
                _remote(landed, landed, ss, rs, k, (px, py, c)).wait_recv()
                _remote(landed, landed, ss, rs, n_w * n_m + k, (x, y, 1 - c)).start()

    def last(c_ins, c_outs, ss, rs):
        x, y, c = _my_place()
        chip = 2 * x + y
        for w in range(n_w):
            mine = _col_half(c_outs[w], c, chip, rows=rows[w])
            for mi, (mx, my) in enumerate(CHIP_MASKS):
                k = w * n_m + mi
                px, py = _flip(x, mx), _flip(y, my)
                theirs = _col_half(c_outs[w], 1 - c, 2 * px + py, rows=rows[w])
                _remote(theirs, theirs, ss, rs, n_w * n_m + k, (x, y, 1 - c)).wait_recv()
                _remote(mine, mine, ss, rs, k, (px, py, c)).wait_send()
                _remote(mine, mine, ss, rs, n_w * n_m + k, (x, y, 1 - c)).wait_send()

    return _Comm(ins=bufs, outs=[jax.ShapeDtypeStruct(b.shape, b.dtype) for b in bufs],
                 aliases={w: w for w in range(n_w)}, n_sems=2 * n_w * n_m, first=first, mid=mid, last=last,
                 mid_at=mid_at)


def _swap_comm(grads):
    n_w = len(grads)

    def copy(c_ins, c_outs, ss, rs, w):
        x, y, c = _my_place()
        return _remote(_col_half(c_ins[w], 1 - c), c_outs[w], ss, rs, w, (x, y, 1 - c))

    def first(c_ins, c_outs, ss, rs):
        for w in range(n_w):
            copy(c_ins, c_outs, ss, rs, w).start()

    def last(c_ins, c_outs, ss, rs):
        for w in range(n_w):
            copy(c_ins, c_outs, ss, rs, w).wait()

    return _Comm(ins=grads,
                 outs=[jax.ShapeDtypeStruct(g.shape[:2] + (g.shape[2] // 2,), g.dtype) for g in grads],
                 aliases={}, n_sems=n_w, first=first, last=last)


def _add_own_half(g, other, chip_core, *, name):
    n_chip, rows, hc = other.shape
    tr, tc = _tile(rows, hc)
    per_half = hc // tc

    def body(cc_ref, g_ref, o_ref, out_ref):
        del cc_ref
        out_ref[...] = (g_ref[...].astype(F32) + o_ref[...].astype(F32)).astype(BF16)

    grid_spec = pltpu.PrefetchScalarGridSpec(
        num_scalar_prefetch=1, grid=(n_chip, rows // tr, per_half),
        in_specs=[pl.BlockSpec((None, tr, tc), lambda j, r, q, cc: (j, r, cc[1] * per_half + q)),
                  pl.BlockSpec((None, tr, tc), lambda j, r, q, cc: (j, r, q))],
        out_specs=pl.BlockSpec((None, tr, tc), lambda j, r, q, cc: (j, r, q)))
    return pl.pallas_call(
        body, grid_spec=grid_spec, out_shape=jax.ShapeDtypeStruct((n_chip, rows, hc), BF16),
        compiler_params=_params("parallel", "parallel", "parallel"), name=name,
    )(chip_core, g, other)


def _add_own_half_side(gs, others, *, n_steps, step_of):
    n_chip, rows, hc = others[0].shape
    per_chip = n_steps // n_chip
    tr = rows // per_chip
    assert all(o.shape == others[0].shape for o in others) and tr * per_chip == rows and tr % 16 == 0

    def body(step, ins, outs, scr):
        del step, scr
        for q, out_ref in enumerate(outs):
            out_ref[...] = (ins[2 * q][...].astype(F32) + ins[2 * q + 1][...].astype(F32)).astype(BF16)

    def blk(own_half):
        def index(i, j, k, place):
            s = step_of(i, j, k)
            return (s // per_chip, s % per_chip, place[1] if own_half else 0)
        return _PlaceSpec((None, tr, hc), index)

    return _Side(
        ins=[a for pair in zip(gs, others) for a in pair], in_specs=[blk(True), blk(False)] * len(gs),
        outs=[jax.ShapeDtypeStruct(o.shape, BF16) for o in others], out_specs=[blk(False)] * len(gs),
        scratch=[], init=_zero_refs, body=body)


def _cast_side(shards, *, n_steps, step_of):
    def body(step, ins, outs, scr):
        del step, scr
        for in_ref, out_ref in zip(ins, outs):
            out_ref[...] = in_ref[...].astype(BF16)

    def rows(s):
        assert s.shape[0] % (16 * n_steps) == 0, s.shape
        return s.shape[0] // n_steps

    return _Side(
        ins=list(shards),
        in_specs=[pl.BlockSpec((rows(s), s.shape[1]), lambda i, j, k: (step_of(i, j, k), 0)) for s in shards],
        outs=[jax.ShapeDtypeStruct((N_CHIPS,) + s.shape, BF16) for s in shards],
        out_specs=[_PlaceSpec((None, rows(s), s.shape[1]),
                              lambda i, j, k, place: (place[0], step_of(i, j, k), 0)) for s in shards],
        scratch=[], init=_zero_refs, body=body)


def _pow2_below(n_steps, most=16):
    return min(most, 1 << (n_steps.bit_length() - 1))


def _sum_chips_side(own, landed, *, n_blocks, step_of):
    n_chip, rows, hc = own.shape
    tr = rows // n_blocks
    assert tr * n_blocks == rows and tr % 16 == 0

    def body(step, ins, outs, scr):
        del step, scr
        o_ref, l1_ref, l2_ref, l3_ref = ins
        outs[0][...] = ((o_ref[...].astype(F32) + l1_ref[...].astype(F32))
                        + l2_ref[...].astype(F32)) + l3_ref[...].astype(F32)

    def block(i, j, k):
        return jnp.minimum(step_of(i, j, k), n_blocks - 1)

    def slot(q):
        return _PlaceSpec((None, tr, hc),
                          lambda i, j, k, place, q=q: ((place[0] + q) % n_chip, block(i, j, k), 0))

    return _Side(
        ins=[own, landed, landed, landed], in_specs=[slot(0), slot(1), slot(2), slot(3)],
        outs=[jax.ShapeDtypeStruct((rows, 2 * hc), F32)],
        out_specs=[_PlaceSpec((tr, hc), lambda i, j, k, place: (block(i, j, k), place[1]))],
        scratch=[], init=_zero_refs, body=body)


def _exchange_comm(pieces):
    n_w, n_m = len(pieces), len(CHIP_MASKS)

    def copies(c_ins, c_outs, ss, rs):
        x, y, c = _my_place()
        chip = 2 * x + y
        for w in range(n_w):
            for mi, (mx, my) in enumerate(CHIP_MASKS):
                px, py = _flip(x, mx), _flip(y, my)
                send = _remote(c_ins[w].at[2 * px + py], c_outs[w].at[chip], ss, rs, w * n_m + mi, (px, py, c))
                landed = c_outs[w].at[2 * px + py]
                yield send, _remote(landed, landed, ss, rs, w * n_m + mi, (px, py, c))

    def first(c_ins, c_outs, ss, rs):
        for send, _ in copies(c_ins, c_outs, ss, rs):
            send.start()

    def last(c_ins, c_outs, ss, rs):
        for send, arrival in copies(c_ins, c_outs, ss, rs):
            arrival.wait_recv()
            send.wait_send()

    return _Comm(ins=pieces, outs=[jax.ShapeDtypeStruct(p.shape, p.dtype) for p in pieces],
                 aliases={}, n_sems=n_w * n_m, first=first, last=last)


def _sum_chips(own, landed, chip_core, *, name):
    n_chip, rows, hc = own.shape
    tr, tc = _tile(rows, hc)
    per_half = hc // tc

    def body(cc_ref, o_ref, l1_ref, l2_ref, l3_ref, out_ref):
        del cc_ref
        out_ref[...] = ((o_ref[...].astype(F32) + l1_ref[...].astype(F32))
                        + l2_ref[...].astype(F32)) + l3_ref[...].astype(F32)

    def slot(k):
        return pl.BlockSpec((None, tr, tc), lambda r, q, cc, k=k: ((cc[0] + k) % n_chip, r, q))

    grid_spec = pltpu.PrefetchScalarGridSpec(
        num_scalar_prefetch=1, grid=(rows // tr, per_half),
        in_specs=[slot(0), slot(1), slot(2), slot(3)],
        out_specs=pl.BlockSpec((tr, tc), lambda r, q, cc: (r, cc[1] * per_half + q)))
    return pl.pallas_call(
        body, grid_spec=grid_spec, out_shape=jax.ShapeDtypeStruct((rows, 2 * hc), F32),
        compiler_params=_params("arbitrary", "arbitrary"), name=name,
    )(chip_core, own, landed, landed, landed)


def _join_comm(halves):
    n_w = len(halves)

    def first(c_ins, c_outs, ss, rs):
        x, y, c = _my_place()
        for w in range(n_w):
            mine = _col_half(c_outs[w], c)
            _remote(mine, mine, ss, rs, w, (x, y, 1 - c)).start()

    def last(c_ins, c_outs, ss, rs):
        x, y, c = _my_place()
        for w in range(n_w):
            theirs = _col_half(c_outs[w], 1 - c)
            _remote(theirs, theirs, ss, rs, w, (x, y, 1 - c)).wait()

    return _Comm(ins=halves, outs=[jax.ShapeDtypeStruct(h.shape, h.dtype) for h in halves],
                 aliases={w: w for w in range(n_w)}, n_sems=n_w, first=first, last=last)


def _standalone(comm, *, name):
    def body(o_ref):
        o_ref[...] = jnp.zeros_like(o_ref)

    return _call(body, grid=(1,), in_specs=[],
                 out_specs=[pl.BlockSpec((SUBLANES, LANES), lambda i: (0, 0))],
                 out_shape=[jax.ShapeDtypeStruct((SUBLANES, LANES), F32)], operands=(), name=name,
                 comm=comm)[1:]


def _pack(pieces):
    flat, spans, off = [], [], 0
    for p in pieces:
        v = p.reshape(-1).astype(F32)
        pad = (-v.shape[0]) % LANES
        if pad:
            v = jnp.concatenate([v, jnp.zeros((pad,), F32)])
        spans.append((off, p.size))
        off += v.shape[0]
        flat.append(v)
    tail = (-off) % (SUBLANES * LANES)
    if tail:
        flat.append(jnp.zeros((tail,), F32))
    return jnp.concatenate(flat).reshape(-1, LANES), spans


def _unpack(buf, span, shape):
    off, size = span
    return buf.reshape(-1)[off:off + size].reshape(shape)


def kernel(x, norm1_g, w_in, w_gate_up, b_gate, conv_w, conv_norm_g, gla_norm_g, w_out, norm2_g, w_ff1, w_ff2, norm_f_g, loss_target, m_norm1_g, m_w_in, m_w_gate_up, m_b_gate, m_conv_w, m_conv_norm_g, m_gla_norm_g, m_w_out, m_norm2_g, m_w_ff1, m_w_ff2, m_norm_f_g, v_norm1_g, v_w_in, v_w_gate_up, v_b_gate, v_conv_w, v_conv_norm_g, v_gla_norm_g, v_w_out, v_norm2_g, v_w_ff1, v_w_ff2, v_norm_f_g):
    xs = x[0]
    target = loss_target[0]
    s_len, d = xs.shape
    d_in = w_in.shape[2] * N_CHIPS
    d_main = d_in - GATE_RANK
    d_ff = w_ff1.shape[2] * N_CHIPS
    cx, cy, cc = _my_place()
    chip = 2 * cx + cy
    chip_core = jnp.stack([chip, cc]).astype(jnp.int32)
    n_ff = d_ff // N_CHIPS
    norm_f = norm_f_g.reshape(1, d)

    n_sh = d_in // N_CHIPS
    wi_buf = _cast_into(w_in[0].T, chip_core, name="cast_w_in")

    small_w, spans_w = _pack([w_gate_up[0], conv_w[0]])
    small_all = _allgather_small(small_w, reduce=False, name="gather_small_weights")
    chips_first = [small_all[2 * j] for j in range(N_CHIPS)]
    wgu_full = jnp.concatenate(
        [_unpack(b, spans_w[0], w_gate_up.shape[1:]) for b in chips_first], axis=1)
    convw_full = jnp.concatenate(
        [_unpack(b, spans_w[1], conv_w.shape[1:]) for b in chips_first], axis=0)
    wgu_pad = jnp.concatenate(
        [wgu_full, jnp.zeros((LANES - GATE_RANK, wgu_full.shape[1]), F32)], axis=0).astype(BF16)
    convw_t = convw_full.T

    u, wo_buf, w1_buf, w2_buf, wi_buf = _rms_fwd(
        xs, norm1_g, name="norm1_fwd", comm=_gather_comm([wi_buf]), place=chip_core,
        side=_cast_side([w_out[0], w_ff1[0], w_ff2[0]], n_steps=s_len // 512, step_of=lambda i, j, k: i))
    wi_t = wi_buf.reshape(d_in, d)
    wg_t = jnp.concatenate([wi_t[d_main:], jnp.zeros((LANES - GATE_RANK, d), BF16)], axis=0)
    z, wo_buf, w1_buf = _matmul(u, wi_t, tb=True, tm=2048, tn=1024, tk=d, out_dtypes=[F32], n_dim=d_main,
                                name="in_proj",
                                comm=_gather_comm([wo_buf, w1_buf], rows=[None, (0, d // 2)]))
    wo_full = wo_buf.reshape(d, d)
    (alow,) = _matmul(u, wg_t, tb=True, tm=1024, tn=LANES, tk=d, out_dtypes=[F32], name="in_proj_gate")
    y0 = _conv_fwd(z, convw_t, conv_norm_g)
    y, o, la, st, w1_cm = _gla_fwd(z, alow, wgu_pad, b_gate, gla_norm_g, y0,
                                   comm=_gather_comm([w1_buf], rows=[(d // 2, d // 2)], mid_at=0.9))
    x2, h = _matmul(y, wo_full, tm=512, tn=d, tk=d, out_dtypes=[F32, BF16], extras=(xs, norm2_g),
                    epilogue=_residual_norm_epilogue, name="out_proj")
    a, p, w2_buf = _matmul(
        h, w1_cm, tm=1024, tn=1024, tk=d, out_dtypes=[BF16, BF16], n_dim=d_ff,
        b_spec=pl.BlockSpec((None, d, 1024), lambda i, j, k: (j // 2, 0, j % 2)),
        epilogue=lambda acc: (acc, jnp.square(jnp.maximum(acc, 0.0))), name="ff1",
        comm=_gather_comm([w2_buf]))
    w2_full = w2_buf.reshape(d_ff, d)
    (x3,) = _matmul(p, w2_full, tm=1024, tn=1024, tk=2048, out_dtypes=[F32], extras=(x2,),
                    epilogue=_add_epilogue, name="ff2")
    dx3b, g_normf, loss_part = _loss_head(x3, norm_f, target)

    (da,) = _matmul(dx3b, w2_full, tb=True, tm=1024, tn=1024, tk=d, out_dtypes=[BF16], extras=(a,),
                    epilogue=lambda acc, av: (acc * (2.0 * jnp.maximum(av, 0.0)),), name="ff2_dx")
    (dh,) = _matmul(
        da, w1_cm, tb=True, tm=2048, tn=1024, tk=2048, out_dtypes=[BF16], n_dim=d,
        b_spec=pl.BlockSpec((None, 1024, 2048), lambda i, j, k: (k, j, 0)), name="ff1_dx")
    dw_k = s_len // 1024
    g_w2, dx2, g_norm2 = _matmul(
        p, dx3b, ta=True, tm=1024, tn=d, tk=1024, out_dtypes=[BF16], name="ff2_dw",
        side=_rms_bwd_side(dh, x2, norm2_g, dx3b, n_steps=(d_ff // 1024) * dw_k,
                           block_of_step=lambda i, j, k: i * dw_k + k))
    g_w2 = g_w2.reshape(N_CHIPS, n_ff, d)
    dy, t_w2 = _matmul(dx2, wo_full, tb=True, tm=1024, tn=1024, tk=d, out_dtypes=[BF16],
                       name="out_proj_dx", comm=_swap_comm([g_w2]))
    f1_j, f1_k = d_ff // n_ff, s_len // 2048
    g_w1, dz0, g_convw_t, g_convg = _matmul(
        h, da, ta=True, tm=1024, tn=n_ff, tk=2048, out_dtypes=[BF16], name="ff1_dw",
        out_shapes=[jax.ShapeDtypeStruct((N_CHIPS, d, n_ff), BF16)],
        out_specs=[pl.BlockSpec((None, 1024, n_ff), lambda i, j, k: (j, i, 0))],
        side=_conv_bwd_side(dy, z, convw_t, conv_norm_g, n_steps=(d // 1024) * f1_j * f1_k,
                            step_of=lambda i, j, k: (i * f1_j + j) * f1_k + k))
    ow_k = s_len // 512
    g_wo, p_w2, t_w1 = _matmul(
        y, dx2, ta=True, tm=1024, tn=d, tk=512, out_dtypes=[BF16], name="out_proj_dw",
        side=_add_own_half_side([g_w2], [t_w2], n_steps=(d // 1024) * ow_k,
                                step_of=lambda i, j, k: i * ow_k + k),
        comm=_swap_comm([g_w1]), place=chip_core)
    g_wo = g_wo.reshape(N_CHIPS, d // N_CHIPS, d)
    dz, dalow, g_wgu_pad, g_bg, g_gg, p_w1, l_w2 = _gla_bwd(
        dy, z, o, la, st, alow, wgu_pad, gla_norm_g, dz0, comm=_exchange_comm([p_w2]), place=chip_core,
        side=_add_own_half_side([g_w1], [t_w1], n_steps=s_len // 512, step_of=lambda i, j, k: i))
    dwi_k = s_len // 2048
    g_wi_t, m_w2, l_w1 = _matmul(
        dz, u, ta=True, tm=1024, tn=d, tk=2048, out_dtypes=[BF16], name="in_proj_dw",
        out_shapes=[jax.ShapeDtypeStruct((d_in, d), BF16)], comm=_exchange_comm([p_w1]),
        side=_sum_chips_side(p_w2, l_w2, n_blocks=_pow2_below(d_main // 1024 * dwi_k),
                             step_of=lambda i, j, k: i * dwi_k + k),
        place=chip_core)
    g_wi_t, m_w1 = _matmul(
        dalow, u, ta=True, tm=LANES, tn=d, tk=1024, out_dtypes=[BF16],
        epilogue=lambda acc: (acc[:GATE_RANK],), into=g_wi_t, name="in_proj_gate_dw",
        out_shapes=[jax.ShapeDtypeStruct((d_in, d), BF16)],
        out_specs=[pl.BlockSpec((GATE_RANK, d), lambda i, j, k: (d_main // GATE_RANK, 0))],
        side=_sum_chips_side(p_w1, l_w1, n_blocks=_pow2_below(s_len // 1024), step_of=lambda i, j, k: k),
        place=chip_core)
    g_wi = g_wi_t.reshape(N_CHIPS, n_sh, d)
    t_wi, t_wo = _standalone(_swap_comm([g_wi, g_wo]), name="swap_w_in_w_out")
    p_wi = _add_own_half(g_wi, t_wi, chip_core, name="pre_reduce_w_in")
    p_wo = _add_own_half(g_wo, t_wo, chip_core, name="pre_reduce_w_out")
    du, l_wi, l_wo, m_w1, m_w2 = _matmul(
        dz, wi_t, tm=1024, tn=1024, tk=2048, out_dtypes=[BF16], n_dim=d, extras=(dalow, wg_t),
        epilogue=lambda acc, dal, wg: (acc + jnp.dot(dal.astype(BF16), wg, preferred_element_type=F32),),
        name="in_proj_dx", comm=[_exchange_comm([p_wi, p_wo]), _join_comm([m_w1, m_w2])])
    m_wi = _sum_chips(p_wi, l_wi, chip_core, name="reduce_w_in")
    m_wo = _sum_chips(p_wo, l_wo, chip_core, name="reduce_w_out")
    grad_x, g_norm1 = _rms_bwd(du, xs, norm1_g, dx2, name="norm1_bwd")
    m_wi, m_wo = _standalone(_join_comm([m_wi, m_wo]), name="join_w_in_w_out")
    g_big = [m_wi, m_wo, m_w1, m_w2]

    small_g, spans_g = _pack([g_norm1, g_wgu_pad[:GATE_RANK], g_bg, g_convw_t, g_convg, g_gg, g_norm2,
                              g_normf, loss_part[:, :1]])
    tot = _allgather_small(small_g, reduce=True, name="reduce_small_grads")
    t_norm1 = _unpack(tot, spans_g[0], (1, d))
    t_wgu = _unpack(tot, spans_g[1], (GATE_RANK, HEADS * DK))
    t_bg = _unpack(tot, spans_g[2], (1, HEADS * DK))
    t_convw = _unpack(tot, spans_g[3], (3, D_CONV)).T
    t_convg = _unpack(tot, spans_g[4], (1, D_CONV))
    t_gg = _unpack(tot, spans_g[5], (1, DV))
    t_norm2 = _unpack(tot, spans_g[6], (1, d))
    t_normf = _unpack(tot, spans_g[7], (1, d))
    loss = _unpack(tot, spans_g[8], ())
    n_gu = w_gate_up.shape[2]
    n_cw = conv_w.shape[1]
    t_wgu = lax.dynamic_slice(t_wgu, (0, chip * n_gu), (GATE_RANK, n_gu))
    t_convw = lax.dynamic_slice(t_convw, (chip * n_cw, 0), (n_cw, 3))

    order = ["norm1_g", "w_in", "w_gate_up", "b_gate", "conv_w", "conv_norm_g", "gla_norm_g", "w_out",
             "norm2_g", "w_ff1", "w_ff2", "norm_f_g"]
    weights = dict(norm1_g=norm1_g, w_in=w_in, w_gate_up=w_gate_up, b_gate=b_gate, conv_w=conv_w,
                   conv_norm_g=conv_norm_g, gla_norm_g=gla_norm_g, w_out=w_out, norm2_g=norm2_g,
                   w_ff1=w_ff1, w_ff2=w_ff2, norm_f_g=norm_f_g)
    moms = dict(norm1_g=m_norm1_g, w_in=m_w_in, w_gate_up=m_w_gate_up, b_gate=m_b_gate, conv_w=m_conv_w,
                conv_norm_g=m_conv_norm_g, gla_norm_g=m_gla_norm_g, w_out=m_w_out, norm2_g=m_norm2_g,
                w_ff1=m_w_ff1, w_ff2=m_w_ff2, norm_f_g=m_norm_f_g)
    vels = dict(norm1_g=v_norm1_g, w_in=v_w_in, w_gate_up=v_w_gate_up, b_gate=v_b_gate, conv_w=v_conv_w,
                conv_norm_g=v_conv_norm_g, gla_norm_g=v_gla_norm_g, w_out=v_w_out, norm2_g=v_norm2_g,
                w_ff1=v_w_ff1, w_ff2=v_w_ff2, norm_f_g=v_norm_f_g)
    grads2d = dict(norm1_g=t_norm1, w_in=g_big[0], w_gate_up=t_wgu, b_gate=t_bg, conv_w=t_convw,
                   conv_norm_g=t_convg, gla_norm_g=t_gg, w_out=g_big[1], norm2_g=t_norm2,
                   w_ff1=g_big[2], w_ff2=g_big[3], norm_f_g=t_normf)
    out_g, out_d, out_m, out_v = [], [], [], []
    for nm in order:
        w = weights[nm]
        g2 = grads2d[nm]
        if nm == "w_in":
            to2d, back = (lambda t: t[0].T), (lambda t: t.T.reshape(w.shape))
        else:
            to2d, back = (lambda t: t.reshape(g2.shape)), (lambda t: t.reshape(w.shape))
        res = _adamw(to2d(w), g2, to2d(moms[nm]), to2d(vels[nm]), name="adamw_" + nm)
        for lst, r in zip((out_g, out_d, out_m, out_v), res):
            lst.append(back(r))
    return (loss, grad_x.reshape(x.shape), *out_g, *out_d, *out_m, *out_v)
```

```python
import functools

import jax
import jax.numpy as jnp
from jax import lax
from jax.experimental import pallas as pl
from jax.experimental.pallas import tpu as pltpu

F32 = jnp.float32
BF16 = jnp.bfloat16
MESH = pl.DeviceIdType.MESH

EPS = 1e-6
CHUNK = 64
HEADS = 4
DK = 128
DV = 256
D_CONV = 1024
GROUP = 128
GATE_RANK = 16
LANES = 128
SUBLANES = 8
N_CHIPS = 4
N_DEV = 8
CHIP_MASKS = ((1, 0), (0, 1), (1, 1))

ADAM_LR = 0.001
ADAM_B1 = 0.9
ADAM_B2 = 0.999
ADAM_EPS = 1e-08
ADAM_WD = 0.01
ADAM_STEP = 10

VMEM_LIMIT = 56 * 1024 * 1024


def _params(*sem):
    return pltpu.CompilerParams(dimension_semantics=tuple(sem), vmem_limit_bytes=VMEM_LIMIT)


def _rowsum8(v):
    r, c = v.shape
    return jnp.sum(v.reshape(r // SUBLANES, SUBLANES, c), axis=0)


def _tile(rows, cols):
    for cand in (256, 128, 64, 32, 16, 8):
        if rows % cand == 0 and rows > cand:
            return cand, cols
    if rows * cols * 4 > (2 << 20) and cols % 256 == 0:
        return rows, 256
    return rows, cols


class _Comm:
    def __init__(self, ins, outs, aliases, n_sems, first, last, mid=None, mid_at=0.75):
        self.ins = list(ins)
        self.outs = list(outs)
        self.aliases = dict(aliases)
        self.n_sems = n_sems
        self.first = first
        self.mid = mid
        self.mid_at = mid_at
        self.last = last


class _PlaceSpec:
    def __init__(self, block_shape, index_map):
        self.block_shape, self.index_map = block_shape, index_map


def _call(body, *, grid, in_specs, out_specs, out_shape, operands, name, scratch_shapes=(), sem=None,
          aliases=None, comm=None, place=None):
    aliases = dict(aliases or {})
    comms = [] if comm is None else (list(comm) if isinstance(comm, (list, tuple)) else [comm])
    n_in, n_out, n_scr = len(in_specs), len(out_specs), len(scratch_shapes)
    c_ins_all = [a for cm in comms for a in cm.ins]
    c_outs_all = [o for cm in comms for o in cm.outs]
    n_ci, n_co = len(c_ins_all), len(c_outs_all)
    n_place = 0 if place is None else 1

    def adapt(spec):
        if isinstance(spec, _PlaceSpec):
            return pl.BlockSpec(spec.block_shape, spec.index_map)
        if place is None or spec.index_map is None:
            return spec
        return pl.BlockSpec(spec.block_shape, lambda *a, f=spec.index_map: f(*a[:-1]))

    def full_body(*refs):
        refs = refs[n_place:]
        ins = refs[:n_in]
        o0 = n_in + n_ci
        outs = refs[o0:o0 + n_out]
        s0 = o0 + n_out + n_co
        scr = refs[s0:s0 + n_scr]
        sems = refs[s0 + n_scr:]
        parts, i_at, o_at = [], n_in, o0 + n_out
        for q, cm in enumerate(comms):
            parts.append((cm, refs[i_at:i_at + len(cm.ins)], refs[o_at:o_at + len(cm.outs)],
                          sems[2 * q], sems[2 * q + 1]))
            i_at += len(cm.ins)
            o_at += len(cm.outs)
        step = functools.reduce(lambda acc, ig: acc * ig[1] + pl.program_id(ig[0]), enumerate(grid), 0)
        n_steps = functools.reduce(lambda acc, g: acc * g, grid, 1)
        @pl.when(step == 0)
        def _():
            for cm, c_ins, c_outs, ss, rs in parts:
                cm.first(c_ins, c_outs, ss, rs)

        def mid_step(cm):
            ms = int(cm.mid_at * n_steps)
            return ms if 0 < ms < n_steps - 1 else None

        for cm, c_ins, c_outs, ss, rs in parts:
            if cm.mid is not None and mid_step(cm) is not None:
                @pl.when(step == mid_step(cm))
                def _(cm=cm, c_ins=c_ins, c_outs=c_outs, ss=ss, rs=rs):
                    cm.mid(c_ins, c_outs, ss, rs)

        body(*ins, *outs, *scr)

        @pl.when(step == n_steps - 1)
        def _():
            for cm, c_ins, c_outs, ss, rs in parts:
                if cm.mid is not None and mid_step(cm) is None:
                    cm.mid(c_ins, c_outs, ss, rs)
                cm.last(c_ins, c_outs, ss, rs)

    any_spec = pl.BlockSpec(memory_space=pl.ANY)
    i_at, o_at, sem_shapes = n_in, n_out, []
    for cm in comms:
        for i_in, i_out in cm.aliases.items():
            aliases[i_at + i_in] = o_at + i_out
        i_at += len(cm.ins)
        o_at += len(cm.outs)
        sem_shapes += [pltpu.SemaphoreType.DMA((cm.n_sems,)), pltpu.SemaphoreType.DMA((cm.n_sems,))]
    specs = dict(grid=grid, in_specs=[adapt(s) for s in in_specs] + [any_spec] * n_ci,
                 out_specs=[adapt(s) for s in out_specs] + [any_spec] * n_co,
                 scratch_shapes=list(scratch_shapes) + sem_shapes)
    if place is not None:
        specs = dict(grid_spec=pltpu.PrefetchScalarGridSpec(num_scalar_prefetch=1, **specs))
    if comms:
        params = pltpu.CompilerParams(dimension_semantics=("arbitrary",) * len(grid),
                                      vmem_limit_bytes=VMEM_LIMIT, has_side_effects=True)
    else:
        params = _params(*(sem or ("arbitrary",) * len(grid)))
    return pl.pallas_call(
        full_body, out_shape=list(out_shape) + c_outs_all,
        input_output_aliases={k + n_place: v for k, v in aliases.items()},
        compiler_params=params, name=name, **specs,
    )(*(() if place is None else (place,)), *operands, *c_ins_all)


def _matmul(a, b, *, ta=False, tb=False, tm, tn, tk, out_dtypes, name, extras=(), epilogue=None,
            out_shapes=None, out_specs=None, b_spec=None, n_dim=None, into=None, side=None, comm=None,
            place=None, m_blocks=None):
    n_into = 0 if into is None else 1
    if ta:
        k_dim, m_dim = a.shape
    else:
        m_dim, k_dim = a.shape
    if n_dim is None:
        n_dim = b.shape[0] if tb else b.shape[1]
        assert (b.shape[1] if tb else b.shape[0]) == k_dim
    assert m_dim % tm == 0 and n_dim % tn == 0 and k_dim % tk == 0, (name, a.shape, b.shape)
    nk = k_dim // tk
    n_ex, n_out = len(extras), len(out_dtypes)
    dims = (((0 if ta else 1,), (1 if tb else 0,)), ((), ()))
    m0, m_count = m_blocks if m_blocks is not None else (0, m_dim // tm)
    grid = (m_count, n_dim // tn, nk)
    s_ins, s_outs, s_scr = (len(side.ins), len(side.outs), len(side.scratch)) if side else (0, 0, 0)

    def body(*refs):
        a_ref, b_ref = refs[0], refs[1]
        ex_refs = refs[2:2 + n_ex]
        i0 = 2 + n_ex + n_into
        side_in = refs[i0:i0 + s_ins]
        o_refs = refs[i0 + s_ins:i0 + s_ins + n_out]
        side_out = refs[i0 + s_ins + n_out:i0 + s_ins + n_out + s_outs]
        side_scr = refs[len(refs) - s_scr:] if s_scr else ()
        if side is not None:
            step = (pl.program_id(0) * grid[1] + pl.program_id(1)) * grid[2] + pl.program_id(2)

            @pl.when(step == 0)
            def _():
                side.init(side_scr)

        def dot():
            if side is not None:
                side.body(step, side_in, side_out, side_scr)
            return lax.dot_general(a_ref[...].astype(BF16), b_ref[...].astype(BF16), dims,
                                   preferred_element_type=F32)

        def finish(acc):
            outs = epilogue(acc, *[e[...] for e in ex_refs]) if epilogue is not None else (acc,)
            for o_ref, o in zip(o_refs, outs):
                o_ref[...] = o.astype(o_ref.dtype)

        if nk == 1:
            finish(dot())
        else:
            acc_ref = refs[len(refs) - s_scr - 1]
            k = pl.program_id(2)

            @pl.when(k == 0)
            def _():
                acc_ref[...] = dot()

            @pl.when(jnp.logical_and(k > 0, k < nk - 1))
            def _():
                acc_ref[...] += dot()

            @pl.when(k == nk - 1)
            def _():
                finish(acc_ref[...] + dot())

    a_spec = (pl.BlockSpec((tk, tm), lambda i, j, k: (k, m0 + i)) if ta
              else pl.BlockSpec((tm, tk), lambda i, j, k: (m0 + i, k)))
    if b_spec is None:
        b_spec = (pl.BlockSpec((tn, tk), lambda i, j, k: (j, k)) if tb
                  else pl.BlockSpec((tk, tn), lambda i, j, k: (k, j)))
    io_spec = pl.BlockSpec((tm, tn), lambda i, j, k: (m0 + i, j))

    def extra_spec(e):
        if e.shape == (m_dim, n_dim):
            return io_spec
        if e.shape[1] == n_dim:
            return pl.BlockSpec((e.shape[0], tn), lambda i, j, k: (0, j))
        assert e.shape[0] == m_dim, (name, e.shape)
        return pl.BlockSpec((tm, e.shape[1]), lambda i, j, k: (m0 + i, 0))

    if out_shapes is None:
        out_shapes = [jax.ShapeDtypeStruct((m_dim, n_dim), dt) for dt in out_dtypes]
    if out_specs is None:
        out_specs = [io_spec] * n_out
    return _call(
        body,
        grid=grid,
        in_specs=([a_spec, b_spec] + [extra_spec(e) for e in extras]
                  + [pl.BlockSpec(memory_space=pl.ANY)] * n_into
                  + (list(side.in_specs) if side else [])),
        out_specs=list(out_specs) + (list(side.out_specs) if side else []),
        out_shape=list(out_shapes) + (list(side.outs) if side else []),
        scratch_shapes=([pltpu.VMEM((tm, tn), F32)] if nk > 1 else []) + (list(side.scratch) if side else []),
        sem=("parallel", "parallel", "arbitrary") if side is None else None,
        aliases={2 + n_ex: 0} if n_into else None,
        operands=(a, b, *extras) + ((into,) if n_into else ()) + (tuple(side.ins) if side else ()),
        name=name, comm=comm, place=place)


def _add_epilogue(acc, r):
    return (acc + r,)


def _residual_norm_epilogue(acc, x, g):
    x2 = acc + x
    r = lax.rsqrt(jnp.mean(x2 * x2, axis=-1, keepdims=True) + EPS)
    return x2, x2 * r * g


def _rms_fwd(x, g, *, name, tm=512, comm=None, side=None, place=None):
    s_len, d = x.shape
    s_ins, s_outs = (len(side.ins), len(side.outs)) if side else (0, 0)

    def body(*refs):
        x_ref, g_ref = refs[0], refs[1]
        o_ref = refs[2 + s_ins]
        xv = x_ref[...]
        r = lax.rsqrt(jnp.mean(xv * xv, axis=-1, keepdims=True) + EPS)
        o_ref[...] = (xv * r * g_ref[...]).astype(o_ref.dtype)
        if side is not None:
            scr = refs[3 + s_ins + s_outs:]

            @pl.when(pl.program_id(0) == 0)
            def _():
                side.init(scr)

            side.body(pl.program_id(0), refs[2:2 + s_ins], refs[3 + s_ins:3 + s_ins + s_outs], scr)

    row = pl.BlockSpec((tm, d), lambda i, j, k: (i, 0))
    return _call(
        body, grid=(s_len // tm, 1, 1),
        in_specs=[row, pl.BlockSpec((1, d), lambda i, j, k: (0, 0))] + (list(side.in_specs) if side else []),
        out_specs=[row] + (list(side.out_specs) if side else []),
        out_shape=[jax.ShapeDtypeStruct((s_len, d), BF16)] + (list(side.outs) if side else []),
        scratch_shapes=list(side.scratch) if side else [],
        operands=(x, g) + (tuple(side.ins) if side else ()), name=name, comm=comm, place=place)


NORM_ROWS = 16


def _zero_refs(refs):
    for r in refs:
        r[...] = jnp.zeros_like(r)


def _rms_bwd_block(dn_ref, x_ref, g_ref, res_ref, dx_ref, dg_ref, acc_ref):
    gv = g_ref[...]
    acc = acc_ref[...]
    for s in range(x_ref.shape[0] // NORM_ROWS):
        sl = slice(s * NORM_ROWS, (s + 1) * NORM_ROWS)
        xv = x_ref[sl, :]
        dnv = dn_ref[sl, :].astype(F32)
        r = lax.rsqrt(jnp.mean(xv * xv, axis=-1, keepdims=True) + EPS)
        xh = xv * r
        acc = acc + _rowsum8(dnv * xh)
        dxh = dnv * gv
        dx_ref[sl, :] = (r * (dxh - xh * jnp.mean(dxh * xh, axis=-1, keepdims=True))
                         + res_ref[sl, :].astype(F32))
    acc_ref[...] = acc
    dg_ref[...] = jnp.sum(acc, axis=0, keepdims=True)


class _Side:
    def __init__(self, ins, in_specs, outs, out_specs, scratch, init, body):
        self.ins, self.in_specs, self.outs, self.out_specs = ins, in_specs, outs, out_specs
        self.scratch = scratch
        self.init = init
        self.body = body


def _rms_bwd_side(dn, x, g, res, *, block_of_step, n_steps, n_rows=None):
    s_len, d = x.shape
    row = pl.BlockSpec(((n_rows or s_len) // n_steps, d), lambda i, j, k: (block_of_step(i, j, k), 0))
    vec = pl.BlockSpec((1, d), lambda i, j, k: (0, 0))
    return _Side(
        ins=[dn, x, g, res], in_specs=[row, row, vec, row],
        outs=[jax.ShapeDtypeStruct((s_len, d), F32), jax.ShapeDtypeStruct((1, d), F32)],
        out_specs=[row, vec], scratch=[pltpu.VMEM((SUBLANES, d), F32)],
        init=_zero_refs, body=lambda step, ins, outs, scr: _rms_bwd_block(*ins, *outs, *scr))


def _rms_bwd(dn, x, g, res, *, name, tm=512, first_row=0, into=None):
    s_len, d = x.shape
    r0, n = first_row // tm, (s_len - first_row) // tm
    n_into = 0 if into is None else 1

    def body(*refs):
        @pl.when(pl.program_id(0) == 0)
        def _():
            _zero_refs(refs[-1:])

        _rms_bwd_block(*refs[:4], *refs[4 + n_into:])

    row = pl.BlockSpec((tm, d), lambda i: (r0 + i, 0))
    vec = pl.BlockSpec((1, d), lambda i: (0, 0))
    return _call(
        body, grid=(n,),
        in_specs=[row, row, vec, row] + [pl.BlockSpec(memory_space=pl.ANY)] * n_into,
        out_specs=[row, vec],
        out_shape=[jax.ShapeDtypeStruct((s_len, d), F32), jax.ShapeDtypeStruct((1, d), F32)],
        scratch_shapes=[pltpu.VMEM((SUBLANES, d), F32)],
        aliases={4: 0} if n_into else None,
        operands=(dn, x, g, res) + ((into,) if n_into else ()), name=name)


def _loss_head(x3, g, target, *, tm=512):
    s_len, d = x3.shape
    n = s_len // tm

    def body(x_ref, g_ref, t_ref, dxb_ref, dg_ref, loss_ref, accg_ref, accl_ref):
        i = pl.program_id(0)
        xv = x_ref[...]
        gv = g_ref[...]
        r = lax.rsqrt(jnp.mean(xv * xv, axis=-1, keepdims=True) + EPS)
        xh = xv * r
        err = xh * gv - t_ref[...]

        @pl.when(i == 0)
        def _():
            accg_ref[...] = jnp.zeros_like(accg_ref)
            accl_ref[...] = jnp.zeros_like(accl_ref)

        accl_ref[...] += _rowsum8(err * err)
        dn = err * (1.0 / d)
        accg_ref[...] += _rowsum8(dn * xh)
        dxh = dn * gv
        dx = r * (dxh - xh * jnp.mean(dxh * xh, axis=-1, keepdims=True))
        dxb_ref[...] = dx.astype(BF16)

        @pl.when(i == n - 1)
        def _():
            dg_ref[...] = jnp.sum(accg_ref[...], axis=0, keepdims=True)
            tot = jnp.sum(jnp.sum(accl_ref[...], axis=0, keepdims=True), axis=1, keepdims=True)
            loss_ref[...] = jnp.broadcast_to(tot * (0.5 / d), (1, LANES))

    row = pl.BlockSpec((tm, d), lambda i: (i, 0))
    vec = pl.BlockSpec((1, d), lambda i: (0, 0))
    return pl.pallas_call(
        body, grid=(n,),
        in_specs=[row, vec, row],
        out_specs=[row, vec, pl.BlockSpec((1, LANES), lambda i: (0, 0))],
        out_shape=[jax.ShapeDtypeStruct((s_len, d), BF16),
                   jax.ShapeDtypeStruct((1, d), F32), jax.ShapeDtypeStruct((1, LANES), F32)],
        scratch_shapes=[pltpu.VMEM((SUBLANES, d), F32), pltpu.VMEM((SUBLANES, d), F32)],
        compiler_params=_params("arbitrary"), name="loss_head",
    )(x3, g, target)


def _shift_down(v, k, rows_before, row):
    out = pltpu.roll(v, k, axis=0)
    for j in range(k):
        out = jnp.where(row == j, rows_before[j], out)
    return out


def _shift_up(v, k, rows_after, row):
    t = v.shape[0]
    out = pltpu.roll(v, t - k, axis=0)
    for j in range(k):
        out = jnp.where(row == t - k + j, rows_after[j], out)
    return out


def _conv_fwd(z, w_t, gain, *, ts=512):
    s_len = z.shape[0]
    n_grp = D_CONV // GROUP

    def body(cb_ref, cc_ref, ch_ref, w_ref, g_ref, y_ref, carry_ref):
        i = pl.program_id(0)

        @pl.when(i == 0)
        def _():
            carry_ref[...] = jnp.zeros_like(carry_ref)

        row = lax.broadcasted_iota(jnp.int32, (ts, GROUP), 0)
        for g in range(n_grp):
            sl = slice(g * GROUP, (g + 1) * GROUP)
            uu = cc_ref[:, sl] * ch_ref[:, sl]
            p2 = carry_ref[6:7, sl]
            p1 = carry_ref[7:8, sl]
            u1 = _shift_down(uu, 1, [p1], row)
            u2 = _shift_down(uu, 2, [p2, p1], row)
            conv = w_ref[0:1, sl] * u2 + w_ref[1:2, sl] * u1 + w_ref[2:3, sl] * uu
            y = cb_ref[:, sl] * conv
            carry_ref[:, sl] = uu[ts - SUBLANES:ts, :]
            rg = lax.rsqrt(jnp.mean(y * y, axis=-1, keepdims=True) + EPS)
            y_ref[:, sl] = (y * rg * g_ref[:, sl]).astype(BF16)

    def col(j):
        return pl.BlockSpec((ts, D_CONV), lambda i, j=j: (i, j))

    small = lambda r: pl.BlockSpec((r, D_CONV), lambda i: (0, 0))
    return pl.pallas_call(
        body, grid=(s_len // ts,),
        in_specs=[col(0), col(1), col(2), small(3), small(1)],
        out_specs=col(0),
        out_shape=jax.ShapeDtypeStruct((s_len, 2 * D_CONV), BF16),
        scratch_shapes=[pltpu.VMEM((SUBLANES, D_CONV), F32)],
        compiler_params=_params("arbitrary"), name="conv_fwd",
    )(z, z, z, w_t, gain)


def _conv_bwd_side(dy, z, w_t, gain, *, n_steps, step_of):
    s_len = z.shape[0]
    n = n_steps
    ts = s_len // n
    n_grp = D_CONV // GROUP
    halo_blocks = ts // SUBLANES

    def body(step, ins, outs, scr):
        dy_ref, cb_ref, cc_ref, ch_ref, hcc_ref, hch_ref, w_ref, g_ref = ins
        dz_ref, dw_ref, dg_ref = outs
        carry_ref, accw_ref, accg_ref = scr
        first_tile = (n - 1 - step) == 0
        row = lax.broadcasted_iota(jnp.int32, (ts, GROUP), 0)
        keep = jnp.where(first_tile, 0.0, 1.0)
        for g in range(n_grp):
            sl = slice(g * GROUP, (g + 1) * GROUP)
            cc = cc_ref[:, sl]
            ch = ch_ref[:, sl]
            cb = cb_ref[:, sl]
            uu = cc * ch
            p2 = hcc_ref[6:7, sl] * hch_ref[6:7, sl] * keep
            p1 = hcc_ref[7:8, sl] * hch_ref[7:8, sl] * keep
            u1 = _shift_down(uu, 1, [p1], row)
            u2 = _shift_down(uu, 2, [p2, p1], row)
            w0, w1, w2 = w_ref[0:1, sl], w_ref[1:2, sl], w_ref[2:3, sl]
            conv = w0 * u2 + w1 * u1 + w2 * uu
            y = cb * conv
            rg = lax.rsqrt(jnp.mean(y * y, axis=-1, keepdims=True) + EPS)
            yh = y * rg
            dyv = dy_ref[:, sl].astype(F32)
            accg_ref[:, sl] += _rowsum8(dyv * yh)
            dyn = dyv * g_ref[:, sl]
            dpre = rg * (dyn - yh * jnp.mean(dyn * yh, axis=-1, keepdims=True))
            dz_ref[:, sl] = (dpre * conv).astype(BF16)
            dconv = dpre * cb
            accw_ref[0:8, sl] += _rowsum8(dconv * u2)
            accw_ref[8:16, sl] += _rowsum8(dconv * u1)
            accw_ref[16:24, sl] += _rowsum8(dconv * uu)
            n0 = carry_ref[0:1, sl]
            n1 = carry_ref[1:2, sl]
            d1 = _shift_up(dconv, 1, [n0], row)
            d2 = _shift_up(dconv, 2, [n0, n1], row)
            duu = w2 * dconv + w1 * d1 + w0 * d2
            carry_ref[:, sl] = dconv[0:SUBLANES, :]
            dz_ref[:, D_CONV + g * GROUP:D_CONV + (g + 1) * GROUP] = (duu * ch).astype(BF16)
            dz_ref[:, 2 * D_CONV + g * GROUP:2 * D_CONV + (g + 1) * GROUP] = (duu * cc).astype(BF16)

        for k in range(3):
            dw_ref[k:k + 1, :] = jnp.sum(accw_ref[8 * k:8 * k + 8, :], axis=0, keepdims=True)
        dg_ref[...] = jnp.sum(accg_ref[...], axis=0, keepdims=True)

    def tile(i, j, k):
        return n - 1 - step_of(i, j, k)

    def col(c):
        return pl.BlockSpec((ts, D_CONV), lambda i, j, k, c=c: (tile(i, j, k), c))

    def halo(c):
        return pl.BlockSpec((SUBLANES, D_CONV),
                            lambda i, j, k, c=c: (jnp.maximum(tile(i, j, k) * halo_blocks - 1, 0), c))

    small = lambda r: pl.BlockSpec((r, D_CONV), lambda i, j, k: (0, 0))
    return _Side(
        ins=[dy, z, z, z, z, z, w_t, gain],
        in_specs=[col(0), col(0), col(1), col(2), halo(1), halo(2), small(3), small(1)],
        outs=[jax.ShapeDtypeStruct((s_len, 6 * D_CONV), BF16),
              jax.ShapeDtypeStruct((3, D_CONV), F32), jax.ShapeDtypeStruct((1, D_CONV), F32)],
        out_specs=[pl.BlockSpec((ts, 3 * D_CONV), lambda i, j, k: (tile(i, j, k), 0)), small(3), small(1)],
        scratch=[pltpu.VMEM((SUBLANES, D_CONV), F32), pltpu.VMEM((24, D_CONV), F32),
                 pltpu.VMEM((SUBLANES, D_CONV), F32)],
        init=_zero_refs, body=body)


def _split3(v):
    hi = v.astype(BF16)
    r1 = v - hi.astype(F32)
    mid = r1.astype(BF16)
    lo = (r1 - mid.astype(F32)).astype(BF16)
    return jnp.concatenate([hi, mid, lo], axis=1)


def _tri_sum(tri, v):
    w = v.shape[1]
    dd = jnp.dot(tri, _split3(v), preferred_element_type=F32)
    return dd[:, :w] + dd[:, w:2 * w] + dd[:, 2 * w:]


def _chunk_masks(ts):
    r = jnp.arange(ts)
    same = (r[:, None] // CHUNK) == (r[None, :] // CHUNK)
    later = jnp.logical_and(same, r[None, :] > r[:, None]).astype(BF16)
    earlier = jnp.logical_and(same, r[None, :] < r[:, None]).astype(BF16)
    chunk_of_row = jnp.arange(ts // CHUNK * SUBLANES)[:, None] // SUBLANES
    member = (chunk_of_row == (r[None, :] // CHUNK)).astype(BF16)
    return later, earlier, member


def _sigmoid(v):
    return 0.5 * jnp.tanh(0.5 * v) + 0.5


def _gla_fwd(z, alow, wgu, bg, gg, y_in, *, ts=512, comm=None):
    s_len = z.shape[0]
    nch = ts // CHUNK
    scale = DK ** -0.5

    tri_u, _, ind8 = _chunk_masks(ts)

    def body(q_ref, k_ref, v_ref, og_ref, al_ref, wgu_ref, bg_ref, gg_ref, tu_ref, ind_ref, yin_ref,
             y_ref, o_ref, la_ref, st_ref, state_ref, kd_ref, qs_ref, dec_ref):
        del yin_ref
        i = pl.program_id(0)

        @pl.when(i == 0)
        def _():
            state_ref[...] = jnp.zeros_like(state_ref)

        pre = jnp.dot(al_ref[...].astype(BF16), wgu_ref[...], preferred_element_type=F32) + bg_ref[...]
        la = (jnp.minimum(pre, 0.0) - jnp.log(1.0 + jnp.exp(-jnp.abs(pre)))) * (1.0 / 16.0)
        la_ref[...] = la
        kd_ref[...] = (k_ref[...] * jnp.exp(_tri_sum(tu_ref[...], la))).astype(BF16)
        qs_ref[...] = (q_ref[...] * scale).astype(BF16)
        dec_ref[...] = jnp.exp(_tri_sum(ind_ref[...], la))

        def chunk(cl, carry):
            rows = pl.ds(pl.multiple_of(cl * CHUNK, CHUNK), CHUNK)
            dec = dec_ref[pl.ds(pl.multiple_of(cl * SUBLANES, SUBLANES), 1), :]
            for h in range(HEADS):
                ks = slice(h * DK, (h + 1) * DK)
                vs = slice(h * DV, (h + 1) * DV)
                kv_t = lax.dot_general(v_ref[rows, vs].astype(BF16), kd_ref[rows, ks],
                                       (((0,), (0,)), ((), ())), preferred_element_type=F32)
                st = state_ref[h] * dec[:, ks] + kv_t
                state_ref[h] = st
                st_ref[cl, h] = st
                o_ref[rows, vs] = lax.dot_general(qs_ref[rows, ks], st.astype(BF16),
                                                  (((1,), (1,)), ((), ())), preferred_element_type=F32)
            return carry

        lax.fori_loop(0, nch, chunk, 0, unroll=2)

        ggv = gg_ref[...]
        for h in range(HEADS):
            vs = slice(h * DV, (h + 1) * DV)
            o_h = o_ref[:, vs]
            og_h = og_ref[:, vs]
            ro = lax.rsqrt(jnp.mean(o_h * o_h, axis=-1, keepdims=True) + EPS)
            y_ref[:, vs] = (o_h * ro * ggv * (og_h * _sigmoid(og_h))).astype(BF16)

    def zcol(width, j):
        return pl.BlockSpec((ts, width), lambda i, j=j: (i, j))

    full = lambda shape: pl.BlockSpec(shape, lambda i: tuple(0 for _ in shape))
    return _call(
        body, grid=(s_len // ts,),
        in_specs=[zcol(512, 6), zcol(512, 7), zcol(1024, 4), zcol(1024, 5), zcol(LANES, 0),
                  full((LANES, 512)), full((1, 512)), full((1, DV)), full(tri_u.shape), full(ind8.shape),
                  pl.BlockSpec(memory_space=pl.ANY)],
        out_specs=[zcol(1024, 1), zcol(1024, 0), zcol(512, 0),
                   pl.BlockSpec((nch, HEADS, DV, DK), lambda i: (i, 0, 0, 0))],
        out_shape=[jax.ShapeDtypeStruct((s_len, 2048), BF16), jax.ShapeDtypeStruct((s_len, 1024), F32),
                   jax.ShapeDtypeStruct((s_len, 512), F32),
                   jax.ShapeDtypeStruct((s_len // CHUNK, HEADS, DV, DK), F32)],
        scratch_shapes=[pltpu.VMEM((HEADS, DV, DK), F32), pltpu.VMEM((ts, 512), BF16),
                        pltpu.VMEM((ts, 512), BF16), pltpu.VMEM((nch * SUBLANES, 512), F32)],
        aliases={10: 0},
        operands=(z, z, z, z, alow, wgu, bg, gg, tri_u, ind8, y_in), name="gla_fwd", comm=comm)


def _gla_bwd(dy, z, o, la, st, alow, wgu, gg, dz_in, *, ts=512, comm=None, side=None, place=None):
    s_len = z.shape[0]
    n = s_len // ts
    nch = ts // CHUNK
    scale = DK ** -0.5

    tri_u, tri_l, ind8 = _chunk_masks(ts)

    def body(dy_ref, q_ref, k_ref, v_ref, og_ref, o_ref, la_ref, st_ref, stp_ref, al_ref, wgu_ref,
             gg_ref, tu_ref, tl_ref, ind_ref, dzin_ref, dz_ref, dal_ref, dwgu_ref, dbg_ref, dgg_ref,
             gt_ref, decn_ref, accw_ref, accb_ref, accg_ref, dla_ref,
             e_ref, kd_ref, kdb_ref, qs_ref, do_ref, dkd_ref, dec_ref, dbe_ref):
        del dzin_ref
        i = pl.program_id(0)
        first_tile = (n - 1 - i) == 0

        @pl.when(i == 0)
        def _():
            gt_ref[...] = jnp.zeros_like(gt_ref)
            decn_ref[...] = jnp.ones_like(decn_ref)
            accw_ref[...] = jnp.zeros_like(accw_ref)
            accb_ref[...] = jnp.zeros_like(accb_ref)
            accg_ref[...] = jnp.zeros_like(accg_ref)

        la = la_ref[...]
        e_dec = jnp.exp(_tri_sum(tu_ref[...], la))
        e_ref[...] = e_dec
        kd = k_ref[...] * e_dec
        kd_ref[...] = kd
        kdb_ref[...] = kd.astype(BF16)
        qs_ref[...] = (q_ref[...] * scale).astype(BF16)
        dec_ref[...] = jnp.exp(_tri_sum(ind_ref[...], la))
        ggv = gg_ref[...]
        for h in range(HEADS):
            vs = slice(h * DV, (h + 1) * DV)
            o_h = o_ref[:, vs]
            og_h = og_ref[:, vs]
            dy_h = dy_ref[:, vs].astype(F32)
            ro = lax.rsqrt(jnp.mean(o_h * o_h, axis=-1, keepdims=True) + EPS)
            oh = o_h * ro
            sig = _sigmoid(og_h)
            sil = og_h * sig
            accg_ref[...] += _rowsum8(dy_h * oh * sil)
            dz_ref[:, 2048 + h * DV:2048 + (h + 1) * DV] = (
                dy_h * oh * ggv * sig * (1.0 + og_h * (1.0 - sig))).astype(BF16)
            don = dy_h * ggv * sil
            do_ref[:, vs] = (ro * (don - oh * jnp.mean(don * oh, axis=-1, keepdims=True))).astype(BF16)
        keep = jnp.where(first_tile, 0.0, 1.0)

        def chunk(jrev, decn):
            cl = nch - 1 - jrev
            rows = pl.ds(pl.multiple_of(cl * CHUNK, CHUNK), CHUNK)
            one_row = pl.ds(pl.multiple_of(cl * SUBLANES, SUBLANES), 1)
            dec = dec_ref[one_row, :]
            has_prev = jnp.where(cl > 0, 1.0, 0.0)
            prev_idx = jnp.maximum(cl - 1, 0)
            for h in range(HEADS):
                ks = slice(h * DK, (h + 1) * DK)
                vs = slice(h * DV, (h + 1) * DV)
                dob = do_ref[rows, vs]
                s_c = st_ref[cl, h]
                dqs = jnp.dot(dob, s_c.astype(BF16), preferred_element_type=F32)
                dz_ref[rows, ks] = (dqs * scale).astype(BF16)
                gt = gt_ref[h] * decn[:, ks] + lax.dot_general(
                    dob, qs_ref[rows, ks], (((0,), (0,)), ((), ())), preferred_element_type=F32)
                gt_ref[h] = gt
                gb = gt.astype(BF16)
                dkd_ref[rows, ks] = jnp.dot(v_ref[rows, vs].astype(BF16), gb, preferred_element_type=F32)
                dz_ref[rows, 1024 + h * DV:1024 + (h + 1) * DV] = lax.dot_general(
                    kdb_ref[rows, ks], gb, (((1,), (1,)), ((), ())),
                    preferred_element_type=F32).astype(BF16)
                s_prev = has_prev * st_ref[prev_idx, h] + (1.0 - has_prev) * keep * stp_ref[0, h]
                dbe_ref[one_row, ks] = jnp.sum(gt * s_prev, axis=0, keepdims=True) * dec[:, ks]
            return dec

        decn_ref[0:1, :] = lax.fori_loop(0, nch, chunk, decn_ref[0:1, :], unroll=2)

        dkd = dkd_ref[...]
        dz_ref[:, 512:1024] = (dkd * e_ref[...]).astype(BF16)
        dla_ref[...] = _tri_sum(tl_ref[...], dkd * kd_ref[...])
        for c in range(nch):
            dla_ref[c * CHUNK:(c + 1) * CHUNK, :] += dbe_ref[c * SUBLANES:c * SUBLANES + 1, :]
        dpre = dla_ref[...] * (1.0 / 16.0) * (1.0 - jnp.exp(16.0 * la))
        accb_ref[...] += _rowsum8(dpre)
        dpb = dpre.astype(BF16)
        accw_ref[...] += lax.dot_general(al_ref[...].astype(BF16), dpb, (((0,), (0,)), ((), ())),
                                         preferred_element_type=F32)
        dal_ref[...] = lax.dot_general(dpb, wgu_ref[...], (((1,), (1,)), ((), ())),
                                       preferred_element_type=F32)

        @pl.when(i == n - 1)
        def _():
            dwgu_ref[...] = accw_ref[...]
            dbg_ref[...] = jnp.sum(accb_ref[...], axis=0, keepdims=True)
            dgg_ref[...] = jnp.sum(accg_ref[...], axis=0, keepdims=True)

    def zcol(width, j):
        return pl.BlockSpec((ts, width), lambda i, j=j: (n - 1 - i, j))

    full = lambda shape: pl.BlockSpec(shape, lambda i: tuple(0 for _ in shape))
    n_in, n_out, n_scr = 16, 5, 14
    s_ins, s_outs = (len(side.ins), len(side.outs)) if side else (0, 0)

    def body_with_side(*refs):
        outs_at, scr_at = n_in + s_ins, n_in + s_ins + n_out + s_outs
        if side is not None:
            side_scr = refs[scr_at + n_scr:]

            @pl.when(pl.program_id(0) == 0)
            def _():
                side.init(side_scr)

            side.body(pl.program_id(0), refs[n_in:outs_at], refs[outs_at + n_out:scr_at], side_scr)
        body(*refs[:n_in], *refs[outs_at:outs_at + n_out], *refs[scr_at:scr_at + n_scr])

    def one_axis(spec):
        return type(spec)(spec.block_shape, lambda i, *p, f=spec.index_map: f(i, 0, 0, *p))

    return _call(
        body_with_side, grid=(n,),
        in_specs=[zcol(1024, 1), zcol(512, 6), zcol(512, 7), zcol(1024, 4), zcol(1024, 5),
                  zcol(1024, 0), zcol(512, 0),
                  pl.BlockSpec((nch, HEADS, DV, DK), lambda i: (n - 1 - i, 0, 0, 0)),
                  pl.BlockSpec((1, HEADS, DV, DK),
                               lambda i: (jnp.maximum((n - 1 - i) * nch - 1, 0), 0, 0, 0)),
                  zcol(LANES, 0), full((LANES, 512)), full((1, DV)),
                  full(tri_u.shape), full(tri_l.shape), full(ind8.shape),
                  pl.BlockSpec(memory_space=pl.ANY)] + ([one_axis(s) for s in side.in_specs] if side else []),
        out_specs=([zcol(3072, 1), zcol(LANES, 0), full((LANES, 512)), full((1, 512)), full((1, DV))]
                   + ([one_axis(s) for s in side.out_specs] if side else [])),
        out_shape=[jax.ShapeDtypeStruct((s_len, 6144), BF16), jax.ShapeDtypeStruct((s_len, LANES), F32),
                   jax.ShapeDtypeStruct((LANES, 512), F32), jax.ShapeDtypeStruct((1, 512), F32),
                   jax.ShapeDtypeStruct((1, DV), F32)] + (list(side.outs) if side else []),
        scratch_shapes=[pltpu.VMEM((HEADS, DV, DK), F32), pltpu.VMEM((SUBLANES, 512), F32),
                        pltpu.VMEM((LANES, 512), F32), pltpu.VMEM((SUBLANES, 512), F32),
                        pltpu.VMEM((SUBLANES, DV), F32), pltpu.VMEM((ts, 512), F32),
                        pltpu.VMEM((ts, 512), F32), pltpu.VMEM((ts, 512), F32), pltpu.VMEM((ts, 512), BF16),
                        pltpu.VMEM((ts, 512), BF16), pltpu.VMEM((ts, 1024), BF16),
                        pltpu.VMEM((ts, 512), F32), pltpu.VMEM((nch * SUBLANES, 512), F32),
                        pltpu.VMEM((nch * SUBLANES, 512), F32)] + (list(side.scratch) if side else []),
        aliases={15: 0},
        operands=((dy, z, z, z, z, o, la, st, st, alow, wgu, gg, tri_u, tri_l, ind8, dz_in)
                  + (tuple(side.ins) if side else ())),
        name="gla_bwd", comm=comm, place=place)


def _adamw(w, g, m, v, *, name):
    rows, cols = w.shape
    tr, tc = _tile(rows, cols)

    def body(w_ref, g_ref, m_ref, v_ref, go_ref, d_ref, nm_ref, nv_ref):
        gv = g_ref[...]
        go_ref[...] = gv
        m2 = ADAM_B1 * m_ref[...] + (1.0 - ADAM_B1) * gv
        v2 = ADAM_B2 * v_ref[...] + (1.0 - ADAM_B2) * jnp.square(gv)
        m_hat = m2 / (1.0 - ADAM_B1 ** ADAM_STEP)
        v_hat = v2 / (1.0 - ADAM_B2 ** ADAM_STEP)
        d_ref[...] = -ADAM_LR * (m_hat / (jnp.sqrt(v_hat) + ADAM_EPS) + ADAM_WD * w_ref[...])
        nm_ref[...] = m2
        nv_ref[...] = v2

    blk = pl.BlockSpec((tr, tc), lambda i, j: (i, j))
    shp = jax.ShapeDtypeStruct((rows, cols), F32)
    return pl.pallas_call(
        body, grid=(rows // tr, cols // tc), in_specs=[blk] * 4, out_specs=[blk] * 4, out_shape=[shp] * 4,
        compiler_params=_params("parallel", "parallel"), name=name,
    )(w, g, m, v)


def _my_place():
    return lax.axis_index("x"), lax.axis_index("y"), lax.axis_index("c")


def _flip(v, bit):
    return 1 - v if bit else v


def _allgather_small(buf, *, reduce, name):
    rows = buf.shape[0]

    def body(in_ref, out_ref, gat_ref, send_sems, recv_sems):
        x, y, c = _my_place()
        me = 4 * x + 2 * y + c
        gat_ref[me] = in_ref[...]
        copies = []
        for m in range(1, N_DEV):
            peer = (_flip(x, m & 4), _flip(y, m & 2), _flip(c, m & 1))
            cp = pltpu.make_async_remote_copy(
                src_ref=in_ref, dst_ref=gat_ref.at[me],
                send_sem=send_sems.at[m - 1], recv_sem=recv_sems.at[m - 1],
                device_id=peer, device_id_type=MESH)
            cp.start()
            copies.append(cp)
        for m in range(1, N_DEV):
            px, py, pc = _flip(x, m & 4), _flip(y, m & 2), _flip(c, m & 1)
            src_slot = gat_ref.at[4 * px + 2 * py + pc]
            pltpu.make_async_remote_copy(
                src_ref=src_slot, dst_ref=src_slot,
                send_sem=send_sems.at[m - 1], recv_sem=recv_sems.at[m - 1],
                device_id=(px, py, pc), device_id_type=MESH).wait_recv()
        for cp in copies:
            cp.wait_send()
        if reduce:
            tot = gat_ref[0]
            for d in range(1, N_DEV):
                tot = tot + gat_ref[d]
            out_ref[...] = tot
        else:
            out_ref[...] = gat_ref[...]

    out_shape = (rows, LANES) if reduce else (N_DEV, rows, LANES)
    return pl.pallas_call(
        body,
        in_specs=[pl.BlockSpec(memory_space=pltpu.VMEM)],
        out_specs=pl.BlockSpec(memory_space=pltpu.VMEM),
        out_shape=jax.ShapeDtypeStruct(out_shape, F32),
        scratch_shapes=[pltpu.VMEM((N_DEV, rows, LANES), F32),
                        pltpu.SemaphoreType.DMA((N_DEV - 1,)), pltpu.SemaphoreType.DMA((N_DEV - 1,))],
        compiler_params=pltpu.CompilerParams(has_side_effects=True),
        name=name,
    )(buf)


def _cast_into(shard, chip_core, *, name):
    rows, cols = shard.shape
    tr, tc = _tile(rows, cols)

    def body(cc_ref, s_ref, o_ref):
        del cc_ref
        o_ref[...] = s_ref[...].astype(BF16)

    grid_spec = pltpu.PrefetchScalarGridSpec(
        num_scalar_prefetch=1, grid=(rows // tr, cols // tc),
        in_specs=[pl.BlockSpec((tr, tc), lambda r, q, cc: (r, q))],
        out_specs=pl.BlockSpec((None, tr, tc), lambda r, q, cc: (cc[0], r, q)))
    return pl.pallas_call(
        body, grid_spec=grid_spec, out_shape=jax.ShapeDtypeStruct((N_CHIPS, rows, cols), BF16),
        compiler_params=_params("arbitrary", "arbitrary"), name=name,
    )(chip_core, shard)


def _remote(src, dst, send_sems, recv_sems, k, device):
    return pltpu.make_async_remote_copy(src_ref=src, dst_ref=dst, send_sem=send_sems.at[k],
                                        recv_sem=recv_sems.at[k], device_id=device, device_id_type=MESH)


def _col_half(ref, h, *lead, rows=None):
    hc = ref.shape[-1] // 2
    mid = (slice(None),) * (len(ref.shape) - 2 - len(lead))
    row_sel = slice(None) if rows is None else pl.ds(rows[0], rows[1])
    return ref.at[tuple(lead) + mid + (row_sel, pl.ds(h * hc, hc))]


def _gather_comm(bufs, rows=None, mid_at=0.75):
    n_w, n_m = len(bufs), len(CHIP_MASKS)
    rows = rows or [None] * n_w

    def first(c_ins, c_outs, ss, rs):
        x, y, c = _my_place()
        chip = 2 * x + y
        for w in range(n_w):
            mine = _col_half(c_outs[w], c, chip, rows=rows[w])
            for mi, (mx, my) in enumerate(CHIP_MASKS):
                _remote(mine, mine, ss, rs, w * n_m + mi, (_flip(x, mx), _flip(y, my), c)).start()

    def mid(c_ins, c_outs, ss, rs):
        x, y, c = _my_place()
        for w in range(n_w):
            for mi, (mx, my) in enumerate(CHIP_MASKS):
                k = w * n_m + mi
                px, py = _flip(x, mx), _flip(y, my)
                landed = _col_half(c_outs[w], c, 2 * px + py, rows=rows[w])
                _remote(landed, landed, ss, rs, k, (px, py, c)).wait_recv()
                _remote(landed, landed, ss, rs, n_w * n_m + k, (x, y, 1 - c)).start()

    def last(c_ins, c_outs, ss, rs):
        x, y, c = _my_place()
        chip = 2 * x + y
        for w in range(n_w):
            mine = _col_half(c_outs[w], c, chip, rows=rows[w])
            for mi, (mx, my) in enumerate(CHIP_MASKS):
                k = w * n_m + mi
                px, py = _flip(x, mx), _flip(y, my)
                theirs = _col_half(c_outs[w], 1 - c, 2 * px + py, rows=rows[w])
                _remote(theirs, theirs, ss, rs, n_w * n_m + k, (x, y, 1 - c)).wait_recv()
                _remote(mine, mine, ss, rs, k, (px, py, c)).wait_send()
                _remote(mine, mine, ss, rs, n_w * n_m + k, (x, y, 1 - c)).wait_send()

    return _Comm(ins=bufs, outs=[jax.ShapeDtypeStruct(b.shape, b.dtype) for b in bufs],
                 aliases={w: w for w in range(n_w)}, n_sems=2 * n_w * n_m, first=first, mid=mid, last=last,
                 mid_at=mid_at)


def _swap_comm(grads):
    n_w = len(grads)

    def copy(c_ins, c_outs, ss, rs, w):
        x, y, c = _my_place()
        return _remote(_col_half(c_ins[w], 1 - c), c_outs[w], ss, rs, w, (x, y, 1 - c))

    def first(c_ins, c_outs, ss, rs):
        for w in range(n_w):
            copy(c_ins, c_outs, ss, rs, w).start()

    def last(c_ins, c_outs, ss, rs):
        for w in range(n_w):
            copy(c_ins, c_outs, ss, rs, w).wait()

    return _Comm(ins=grads,
                 outs=[jax.ShapeDtypeStruct(g.shape[:2] + (g.shape[2] // 2,), g.dtype) for g in grads],
                 aliases={}, n_sems=n_w, first=first, last=last)


def _add_own_half(g, other, chip_core, *, name):
    n_chip, rows, hc = other.shape
    tr, tc = _tile(rows, hc)
    per_half = hc // tc

    def body(cc_ref, g_ref, o_ref, out_ref):
        del cc_ref
        out_ref[...] = (g_ref[...].astype(F32) + o_ref[...].astype(F32)).astype(BF16)

    grid_spec = pltpu.PrefetchScalarGridSpec(
        num_scalar_prefetch=1, grid=(n_chip, rows // tr, per_half),
        in_specs=[pl.BlockSpec((None, tr, tc), lambda j, r, q, cc: (j, r, cc[1] * per_half + q)),
                  pl.BlockSpec((None, tr, tc), lambda j, r, q, cc: (j, r, q))],
        out_specs=pl.BlockSpec((None, tr, tc), lambda j, r, q, cc: (j, r, q)))
    return pl.pallas_call(
        body, grid_spec=grid_spec, out_shape=jax.ShapeDtypeStruct((n_chip, rows, hc), BF16),
        compiler_params=_params("parallel", "parallel", "parallel"), name=name,
    )(chip_core, g, other)


def _add_own_half_side(gs, others, *, n_steps, step_of):
    n_chip, rows, hc = others[0].shape
    per_chip = n_steps // n_chip
    tr = rows // per_chip
    assert all(o.shape == others[0].shape for o in others) and tr * per_chip == rows and tr % 16 == 0

    def body(step, ins, outs, scr):
        del step, scr
        for q, out_ref in enumerate(outs):
            out_ref[...] = (ins[2 * q][...].astype(F32) + ins[2 * q + 1][...].astype(F32)).astype(BF16)

    def blk(own_half):
        def index(i, j, k, place):
            s = step_of(i, j, k)
            return (s // per_chip, s % per_chip, place[1] if own_half else 0)
        return _PlaceSpec((None, tr, hc), index)

    return _Side(
        ins=[a for pair in zip(gs, others) for a in pair], in_specs=[blk(True), blk(False)] * len(gs),
        outs=[jax.ShapeDtypeStruct(o.shape, BF16) for o in others], out_specs=[blk(False)] * len(gs),
        scratch=[], init=_zero_refs, body=body)


def _cast_side(shards, *, n_steps, step_of):
    def body(step, ins, outs, scr):
        del step, scr
        for in_ref, out_ref in zip(ins, outs):
            out_ref[...] = in_ref[...].astype(BF16)

    def rows(s):
        assert s.shape[0] % (16 * n_steps) == 0, s.shape
        return s.shape[0] // n_steps

    return _Side(
        ins=list(shards),
        in_specs=[pl.BlockSpec((rows(s), s.shape[1]), lambda i, j, k: (step_of(i, j, k), 0)) for s in shards],
        outs=[jax.ShapeDtypeStruct((N_CHIPS,) + s.shape, BF16) for s in shards],
        out_specs=[_PlaceSpec((None, rows(s), s.shape[1]),
                              lambda i, j, k, place: (place[0], step_of(i, j, k), 0)) for s in shards],
        scratch=[], init=_zero_refs, body=body)


def _pow2_below(n_steps, most=16):
    return min(most, 1 << (n_steps.bit_length() - 1))


def _sum_chips_side(own, landed, *, n_blocks, step_of):
    n_chip, rows, hc = own.shape
    tr = rows // n_blocks
    assert tr * n_blocks == rows and tr % 16 == 0

    def body(step, ins, outs, scr):
        del step, scr
        o_ref, l1_ref, l2_ref, l3_ref = ins
        outs[0][...] = ((o_ref[...].astype(F32) + l1_ref[...].astype(F32))
                        + l2_ref[...].astype(F32)) + l3_ref[...].astype(F32)

    def block(i, j, k):
        return jnp.minimum(step_of(i, j, k), n_blocks - 1)

    def slot(q):
        return _PlaceSpec((None, tr, hc),
                          lambda i, j, k, place, q=q: ((place[0] + q) % n_chip, block(i, j, k), 0))

    return _Side(
        ins=[own, landed, landed, landed], in_specs=[slot(0), slot(1), slot(2), slot(3)],
        outs=[jax.ShapeDtypeStruct((rows, 2 * hc), F32)],
        out_specs=[_PlaceSpec((tr, hc), lambda i, j, k, place: (block(i, j, k), place[1]))],
        scratch=[], init=_zero_refs, body=body)


def _exchange_comm(pieces):
    n_w, n_m = len(pieces), len(CHIP_MASKS)

    def copies(c_ins, c_outs, ss, rs):
        x, y, c = _my_place()
        chip = 2 * x + y
        for w in range(n_w):
            for mi, (mx, my) in enumerate(CHIP_MASKS):
                px, py = _flip(x, mx), _flip(y, my)
                send = _remote(c_ins[w].at[2 * px + py], c_outs[w].at[chip], ss, rs, w * n_m + mi, (px, py, c))
                landed = c_outs[w].at[2 * px + py]
                yield send, _remote(landed, landed, ss, rs, w * n_m + mi, (px, py, c))

    def first(c_ins, c_outs, ss, rs):
        for send, _ in copies(c_ins, c_outs, ss, rs):
            send.start()

    def last(c_ins, c_outs, ss, rs):
        for send, arrival in copies(c_ins, c_outs, ss, rs):
            arrival.wait_recv()
            send.wait_send()

    return _Comm(ins=pieces, outs=[jax.ShapeDtypeStruct(p.shape, p.dtype) for p in pieces],
                 aliases={}, n_sems=n_w * n_m, first=first, last=last)


def _sum_chips(own, landed, chip_core, *, name):
    n_chip, rows, hc = own.shape
    tr, tc = _tile(rows, hc)
    per_half = hc // tc

    def body(cc_ref, o_ref, l1_ref, l2_ref, l3_ref, out_ref):
        del cc_ref
        out_ref[...] = ((o_ref[...].astype(F32) + l1_ref[...].astype(F32))
                        + l2_ref[...].astype(F32)) + l3_ref[...].astype(F32)

    def slot(k):
        return pl.BlockSpec((None, tr, tc), lambda r, q, cc, k=k: ((cc[0] + k) % n_chip, r, q))

    grid_spec = pltpu.PrefetchScalarGridSpec(
        num_scalar_prefetch=1, grid=(rows // tr, per_half),
        in_specs=[slot(0), slot(1), slot(2), slot(3)],
        out_specs=pl.BlockSpec((tr, tc), lambda r, q, cc: (r, cc[1] * per_half + q)))
    return pl.pallas_call(
        body, grid_spec=grid_spec, out_shape=jax.ShapeDtypeStruct((rows, 2 * hc), F32),
        compiler_params=_params("arbitrary", "arbitrary"), name=name,
    )(chip_core, own, landed, landed, landed)


def _join_comm(halves):
    n_w = len(halves)

    def first(c_ins, c_outs, ss, rs):
        x, y, c = _my_place()
        for w in range(n_w):
            mine = _col_half(c_outs[w], c)
            _remote(mine, mine, ss, rs, w, (x, y, 1 - c)).start()

    def last(c_ins, c_outs, ss, rs):
        x, y, c = _my_place()
        for w in range(n_w):
            theirs = _col_half(c_outs[w], 1 - c)
            _remote(theirs, theirs, ss, rs, w, (x, y, 1 - c)).wait()

    return _Comm(ins=halves, outs=[jax.ShapeDtypeStruct(h.shape, h.dtype) for h in halves],
                 aliases={w: w for w in range(n_w)}, n_sems=n_w, first=first, last=last)


def _standalone(comm, *, name):
    def body(o_ref):
        o_ref[...] = jnp.zeros_like(o_ref)

    return _call(body, grid=(1,), in_specs=[],
                 out_specs=[pl.BlockSpec((SUBLANES, LANES), lambda i: (0, 0))],
                 out_shape=[jax.ShapeDtypeStruct((SUBLANES, LANES), F32)], operands=(), name=name,
                 comm=comm)[1:]


def _pack(pieces):
    flat, spans, off = [], [], 0
    for p in pieces:
        v = p.reshape(-1).astype(F32)
        pad = (-v.shape[0]) % LANES
        if pad:
            v = jnp.concatenate([v, jnp.zeros((pad,), F32)])
        spans.append((off, p.size))
        off += v.shape[0]
        flat.append(v)
    tail = (-off) % (SUBLANES * LANES)
    if tail:
        flat.append(jnp.zeros((tail,), F32))
    return jnp.concatenate(flat).reshape(-1, LANES), spans


def _unpack(buf, span, shape):
    off, size = span
    return buf.reshape(-1)[off:off + size].reshape(shape)


def kernel(x, norm1_g, w_in, w_gate_up, b_gate, conv_w, conv_norm_g, gla_norm_g, w_out, norm2_g, w_ff1, w_ff2, norm_f_g, loss_target, m_norm1_g, m_w_in, m_w_gate_up, m_b_gate, m_conv_w, m_conv_norm_g, m_gla_norm_g, m_w_out, m_norm2_g, m_w_ff1, m_w_ff2, m_norm_f_g, v_norm1_g, v_w_in, v_w_gate_up, v_b_gate, v_conv_w, v_conv_norm_g, v_gla_norm_g, v_w_out, v_norm2_g, v_w_ff1, v_w_ff2, v_norm_f_g):
    xs = x[0]
    target = loss_target[0]
    s_len, d = xs.shape
    d_in = w_in.shape[2] * N_CHIPS
    d_main = d_in - GATE_RANK
    d_ff = w_ff1.shape[2] * N_CHIPS
    cx, cy, cc = _my_place()
    chip = 2 * cx + cy
    chip_core = jnp.stack([chip, cc]).astype(jnp.int32)
    n_ff = d_ff // N_CHIPS
    norm_f = norm_f_g.reshape(1, d)

    n_sh = d_in // N_CHIPS
    wi_buf = _cast_into(w_in[0].T, chip_core, name="cast_w_in")

    small_w, spans_w = _pack([w_gate_up[0], conv_w[0]])
    small_all = _allgather_small(small_w, reduce=False, name="gather_small_weights")
    chips_first = [small_all[2 * j] for j in range(N_CHIPS)]
    wgu_full = jnp.concatenate(
        [_unpack(b, spans_w[0], w_gate_up.shape[1:]) for b in chips_first], axis=1)
    convw_full = jnp.concatenate(
        [_unpack(b, spans_w[1], conv_w.shape[1:]) for b in chips_first], axis=0)
    wgu_pad = jnp.concatenate(
        [wgu_full, jnp.zeros((LANES - GATE_RANK, wgu_full.shape[1]), F32)], axis=0).astype(BF16)
    convw_t = convw_full.T

    u, wo_buf, w1_buf, w2_buf, wi_buf = _rms_fwd(
        xs, norm1_g, name="norm1_fwd", comm=_gather_comm([wi_buf]), place=chip_core,
        side=_cast_side([w_out[0], w_ff1[0], w_ff2[0]], n_steps=s_len // 512, step_of=lambda i, j, k: i))
    wi_t = wi_buf.reshape(d_in, d)
    wg_t = jnp.concatenate([wi_t[d_main:], jnp.zeros((LANES - GATE_RANK, d), BF16)], axis=0)
    z, wo_buf, w1_buf = _matmul(u, wi_t, tb=True, tm=2048, tn=1024, tk=d, out_dtypes=[F32], n_dim=d_main,
                                name="in_proj",
                                comm=_gather_comm([wo_buf, w1_buf], rows=[None, (0, d // 2)]))
    wo_full = wo_buf.reshape(d, d)
    (alow,) = _matmul(u, wg_t, tb=True, tm=1024, tn=LANES, tk=d, out_dtypes=[F32], name="in_proj_gate")
    y0 = _conv_fwd(z, convw_t, conv_norm_g)
    y, o, la, st, w1_cm = _gla_fwd(z, alow, wgu_pad, b_gate, gla_norm_g, y0,
                                   comm=_gather_comm([w1_buf], rows=[(d // 2, d // 2)], mid_at=0.9))
    x2, h = _matmul(y, wo_full, tm=512, tn=d, tk=d, out_dtypes=[F32, BF16], extras=(xs, norm2_g),
                    epilogue=_residual_norm_epilogue, name="out_proj")
    a, p, w2_buf = _matmul(
        h, w1_cm, tm=1024, tn=1024, tk=d, out_dtypes=[BF16, BF16], n_dim=d_ff,
        b_spec=pl.BlockSpec((None, d, 1024), lambda i, j, k: (j // 2, 0, j % 2)),
        epilogue=lambda acc: (acc, jnp.square(jnp.maximum(acc, 0.0))), name="ff1",
        comm=_gather_comm([w2_buf]))
    w2_full = w2_buf.reshape(d_ff, d)
    (x3,) = _matmul(p, w2_full, tm=1024, tn=1024, tk=2048, out_dtypes=[F32], extras=(x2,),
                    epilogue=_add_epilogue, name="ff2")
    dx3b, g_normf, loss_part = _loss_head(x3, norm_f, target)

    (da,) = _matmul(dx3b, w2_full, tb=True, tm=1024, tn=1024, tk=d, out_dtypes=[BF16], extras=(a,),
                    epilogue=lambda acc, av: (acc * (2.0 * jnp.maximum(av, 0.0)),), name="ff2_dx")
    (dh,) = _matmul(
        da, w1_cm, tb=True, tm=2048, tn=1024, tk=2048, out_dtypes=[BF16], n_dim=d,
        b_spec=pl.BlockSpec((None, 1024, 2048), lambda i, j, k: (k, j, 0)), name="ff1_dx")
    dw_k = s_len // 1024
    g_w2, dx2, g_norm2 = _matmul(
        p, dx3b, ta=True, tm=1024, tn=d, tk=1024, out_dtypes=[BF16], name="ff2_dw",
        side=_rms_bwd_side(dh, x2, norm2_g, dx3b, n_steps=(d_ff // 1024) * dw_k,
                           block_of_step=lambda i, j, k: i * dw_k + k))
    g_w2 = g_w2.reshape(N_CHIPS, n_ff, d)
    dy, t_w2 = _matmul(dx2, wo_full, tb=True, tm=1024, tn=1024, tk=d, out_dtypes=[BF16],
                       name="out_proj_dx", comm=_swap_comm([g_w2]))
    f1_j, f1_k = d_ff // n_ff, s_len // 2048
    g_w1, dz0, g_convw_t, g_convg = _matmul(
        h, da, ta=True, tm=1024, tn=n_ff, tk=2048, out_dtypes=[BF16], name="ff1_dw",
        out_shapes=[jax.ShapeDtypeStruct((N_CHIPS, d, n_ff), BF16)],
        out_specs=[pl.BlockSpec((None, 1024, n_ff), lambda i, j, k: (j, i, 0))],
        side=_conv_bwd_side(dy, z, convw_t, conv_norm_g, n_steps=(d // 1024) * f1_j * f1_k,
                            step_of=lambda i, j, k: (i * f1_j + j) * f1_k + k))
    ow_k = s_len // 512
    g_wo, p_w2, t_w1 = _matmul(
        y, dx2, ta=True, tm=1024, tn=d, tk=512, out_dtypes=[BF16], name="out_proj_dw",
        side=_add_own_half_side([g_w2], [t_w2], n_steps=(d // 1024) * ow_k,
                                step_of=lambda i, j, k: i * ow_k + k),
        comm=_swap_comm([g_w1]), place=chip_core)
    g_wo = g_wo.reshape(N_CHIPS, d // N_CHIPS, d)
    dz, dalow, g_wgu_pad, g_bg, g_gg, p_w1, l_w2 = _gla_bwd(
        dy, z, o, la, st, alow, wgu_pad, gla_norm_g, dz0, comm=_exchange_comm([p_w2]), place=chip_core,
        side=_add_own_half_side([g_w1], [t_w1], n_steps=s_len // 512, step_of=lambda i, j, k: i))
    dwi_k = s_len // 2048
    g_wi_t, m_w2, l_w1 = _matmul(
        dz, u, ta=True, tm=1024, tn=d, tk=2048, out_dtypes=[BF16], name="in_proj_dw",
        out_shapes=[jax.ShapeDtypeStruct((d_in, d), BF16)], comm=_exchange_comm([p_w1]),
        side=_sum_chips_side(p_w2, l_w2, n_blocks=_pow2_below(d_main // 1024 * dwi_k),
                             step_of=lambda i, j, k: i * dwi_k + k),
        place=chip_core)
    g_wi_t, m_w1 = _matmul(
        dalow, u, ta=True, tm=LANES, tn=d, tk=1024, out_dtypes=[BF16],
        epilogue=lambda acc: (acc[:GATE_RANK],), into=g_wi_t, name="in_proj_gate_dw",
        out_shapes=[jax.ShapeDtypeStruct((d_in, d), BF16)],
        out_specs=[pl.BlockSpec((GATE_RANK, d), lambda i, j, k: (d_main // GATE_RANK, 0))],
        side=_sum_chips_side(p_w1, l_w1, n_blocks=_pow2_below(s_len // 1024), step_of=lambda i, j, k: k),
        place=chip_core)
    g_wi = g_wi_t.reshape(N_CHIPS, n_sh, d)
    t_wi, t_wo = _standalone(_swap_comm([g_wi, g_wo]), name="swap_w_in_w_out")
    p_wi = _add_own_half(g_wi, t_wi, chip_core, name="pre_reduce_w_in")
    p_wo = _add_own_half(g_wo, t_wo, chip_core, name="pre_reduce_w_out")
    half_tiles = s_len // 2048
    gate_epilogue = lambda acc, dal, wg: (acc + jnp.dot(dal.astype(BF16), wg, preferred_element_type=F32),)
    du_a, l_wo, m_w1, m_w2 = _matmul(
        dz, wi_t, tm=1024, tn=1024, tk=2048, out_dtypes=[BF16], n_dim=d, extras=(dalow, wg_t),
        epilogue=gate_epilogue, m_blocks=(0, half_tiles), name="in_proj_dx_a",
        comm=[_exchange_comm([p_wo]), _join_comm([m_w1, m_w2])])
    dxb_k = d_main // 1536
    du_b, grad_x, g_norm1_a, l_wi = _matmul(
        dz, wi_t, tm=1024, tn=1024, tk=1536, out_dtypes=[BF16], n_dim=d, extras=(dalow, wg_t),
        epilogue=gate_epilogue, m_blocks=(half_tiles, half_tiles), name="in_proj_dx_b",
        comm=_exchange_comm([p_wi]),
        side=_rms_bwd_side(du_a, xs, norm1_g, dx2, n_rows=s_len // 2, n_steps=half_tiles * 2 * dxb_k,
                           block_of_step=lambda i, j, k: (i * 2 + j) * dxb_k + k))
    m_wi = _sum_chips(p_wi, l_wi, chip_core, name="reduce_w_in")
    m_wo = _sum_chips(p_wo, l_wo, chip_core, name="reduce_w_out")
    grad_x, g_norm1_b = _rms_bwd(du_b, xs, norm1_g, dx2, first_row=s_len // 2, into=grad_x, name="norm1_bwd")
    g_norm1 = g_norm1_a + g_norm1_b
    m_wi, m_wo = _standalone(_join_comm([m_wi, m_wo]), name="join_w_in_w_out")
    g_big = [m_wi, m_wo, m_w1, m_w2]

    small_g, spans_g = _pack([g_norm1, g_wgu_pad[:GATE_RANK], g_bg, g_convw_t, g_convg, g_gg, g_norm2,
                              g_normf, loss_part[:, :1]])
    tot = _allgather_small(small_g, reduce=True, name="reduce_small_grads")
    t_norm1 = _unpack(tot, spans_g[0], (1, d))
    t_wgu = _unpack(tot, spans_g[1], (GATE_RANK, HEADS * DK))
    t_bg = _unpack(tot, spans_g[2], (1, HEADS * DK))
    t_convw = _unpack(tot, spans_g[3], (3, D_CONV)).T
    t_convg = _unpack(tot, spans_g[4], (1, D_CONV))
    t_gg = _unpack(tot, spans_g[5], (1, DV))
    t_norm2 = _unpack(tot, spans_g[6], (1, d))
    t_normf = _unpack(tot, spans_g[7], (1, d))
    loss = _unpack(tot, spans_g[8], ())
    n_gu = w_gate_up.shape[2]
    n_cw = conv_w.shape[1]
    t_wgu = lax.dynamic_slice(t_wgu, (0, chip * n_gu), (GATE_RANK, n_gu))
    t_convw = lax.dynamic_slice(t_convw, (chip * n_cw, 0), (n_cw, 3))

    order = ["norm1_g", "w_in", "w_gate_up", "b_gate", "conv_w", "conv_norm_g", "gla_norm_g", "w_out",
             "norm2_g", "w_ff1", "w_ff2", "norm_f_g"]
    weights = dict(norm1_g=norm1_g, w_in=w_in, w_gate_up=w_gate_up, b_gate=b_gate, conv_w=conv_w,
                   conv_norm_g=conv_norm_g, gla_norm_g=gla_norm_g, w_out=w_out, norm2_g=norm2_g,
                   w_ff1=w_ff1, w_ff2=w_ff2, norm_f_g=norm_f_g)
    moms = dict(norm1_g=m_norm1_g, w_in=m_w_in, w_gate_up=m_w_gate_up, b_gate=m_b_gate, conv_w=m_conv_w,
                conv_norm_g=m_conv_norm_g, gla_norm_g=m_gla_norm_g, w_out=m_w_out, norm2_g=m_norm2_g,
                w_ff1=m_w_ff1, w_ff2=m_w_ff2, norm_f_g=m_norm_f_g)
    vels = dict(norm1_g=v_norm1_g, w_in=v_w_in, w_gate_up=v_w_gate_up, b_gate=v_b_gate, conv_w=v_conv_w,
                conv_norm_g=v_conv_norm_g, gla_norm_g=v_gla_norm_g, w_out=v_w_out, norm2_g=v_norm2_g,
                w_ff1=v_w_ff1, w_ff2=v_w_ff2, norm_f_g=v_norm_f_g)
    grads2d = dict(norm1_g=t_norm1, w_in=g_big[0], w_gate_up=t_wgu, b_gate=t_bg, conv_w=t_convw,
                   conv_norm_g=t_convg, gla_norm_g=t_gg, w_out=g_big[1], norm2_g=t_norm2,
                   w_ff1=g_big[2], w_ff2=g_big[3], norm_f_g=t_normf)
    out_g, out_d, out_m, out_v = [], [], [], []
    for nm in order:
        w = weights[nm]
        g2 = grads2d[nm]
        if nm == "w_in":
            to2d, back = (lambda t: t[0].T), (lambda t: t.T.reshape(w.shape))
        else:
            to2d, back = (lambda t: t.reshape(g2.shape)), (lambda t: t.reshape(w.shape))
        res = _adamw(to2d(w), g2, to2d(moms[nm]), to2d(vels[nm]), name="adamw_" + nm)
        for lst, r in zip((out_g, out_d, out_m, out_v), res):
            lst.append(back(r))
    return (loss, grad_x.reshape(x.shape), *out_g, *out_d, *out_m, *out_v)
```

```python
import functools

import jax
import jax.numpy as jnp
from jax import lax
from jax.experimental import pallas as pl
from jax.experimental.pallas import tpu as pltpu

F32 = jnp.float32
BF16 = jnp.bfloat16
MESH = pl.DeviceIdType.MESH

EPS = 1e-6
CHUNK = 64
HEADS = 4
DK = 128
DV = 256
D_CONV = 1024
GROUP = 128
GATE_RANK = 16
LANES = 128
SUBLANES = 8
N_CHIPS = 4
N_DEV = 8
CHIP_MASKS = ((1, 0), (0, 1), (1, 1))

ADAM_LR = 0.001
ADAM_B1 = 0.9
ADAM_B2 = 0.999
ADAM_EPS = 1e-08
ADAM_WD = 0.01
ADAM_STEP = 10

VMEM_LIMIT = 56 * 1024 * 1024


def _params(*sem):
    return pltpu.CompilerParams(dimension_semantics=tuple(sem), vmem_limit_bytes=VMEM_LIMIT)


def _rowsum8(v):
    r, c = v.shape
    return jnp.sum(v.reshape(r // SUBLANES, SUBLANES, c), axis=0)


def _tile(rows, cols):
    for cand in (256, 128, 64, 32, 16, 8):
        if rows % cand == 0 and cand < rows <= 64 * cand:
            return cand, cols
    if rows * cols * 4 > (2 << 20) and cols % 256 == 0:
        return rows, 256
    return rows, cols


class _Comm:
    def __init__(self, ins, outs, aliases, n_sems, first, last, mid=None, mid_at=0.75):
        self.ins = list(ins)
        self.outs = list(outs)
        self.aliases = dict(aliases)
        self.n_sems = n_sems
        self.first = first
        self.mid = mid
        self.mid_at = mid_at
        self.last = last


class _PlaceSpec:
    def __init__(self, block_shape, index_map):
        self.block_shape, self.index_map = block_shape, index_map


def _call(body, *, grid, in_specs, out_specs, out_shape, operands, name, scratch_shapes=(), sem=None,
          aliases=None, comm=None, place=None):
    aliases = dict(aliases or {})
    comms = [] if comm is None else (list(comm) if isinstance(comm, (list, tuple)) else [comm])
    n_in, n_out, n_scr = len(in_specs), len(out_specs), len(scratch_shapes)
    c_ins_all = [a for cm in comms for a in cm.ins]
    c_outs_all = [o for cm in comms for o in cm.outs]
    n_ci, n_co = len(c_ins_all), len(c_outs_all)
    n_place = 0 if place is None else 1

    def adapt(spec):
        if isinstance(spec, _PlaceSpec):
            return pl.BlockSpec(spec.block_shape, spec.index_map)
        if place is None or spec.index_map is None:
            return spec
        return pl.BlockSpec(spec.block_shape, lambda *a, f=spec.index_map: f(*a[:-1]))

    def full_body(*refs):
        refs = refs[n_place:]
        ins = refs[:n_in]
        o0 = n_in + n_ci
        outs = refs[o0:o0 + n_out]
        s0 = o0 + n_out + n_co
        scr = refs[s0:s0 + n_scr]
        sems = refs[s0 + n_scr:]
        parts, i_at, o_at = [], n_in, o0 + n_out
        for q, cm in enumerate(comms):
            parts.append((cm, refs[i_at:i_at + len(cm.ins)], refs[o_at:o_at + len(cm.outs)],
                          sems[2 * q], sems[2 * q + 1]))
            i_at += len(cm.ins)
            o_at += len(cm.outs)
        step = functools.reduce(lambda acc, ig: acc * ig[1] + pl.program_id(ig[0]), enumerate(grid), 0)
        n_steps = functools.reduce(lambda acc, g: acc * g, grid, 1)
        @pl.when(step == 0)
        def _():
            for cm, c_ins, c_outs, ss, rs in parts:
                cm.first(c_ins, c_outs, ss, rs)

        def mid_step(cm):
            ms = int(cm.mid_at * n_steps)
            return ms if 0 < ms < n_steps - 1 else None

        for cm, c_ins, c_outs, ss, rs in parts:
            if cm.mid is not None and mid_step(cm) is not None:
                @pl.when(step == mid_step(cm))
                def _(cm=cm, c_ins=c_ins, c_outs=c_outs, ss=ss, rs=rs):
                    cm.mid(c_ins, c_outs, ss, rs)

        body(*ins, *outs, *scr)

        @pl.when(step == n_steps - 1)
        def _():
            for cm, c_ins, c_outs, ss, rs in parts:
                if cm.mid is not None and mid_step(cm) is None:
                    cm.mid(c_ins, c_outs, ss, rs)
                cm.last(c_ins, c_outs, ss, rs)

    any_spec = pl.BlockSpec(memory_space=pl.ANY)
    i_at, o_at, sem_shapes = n_in, n_out, []
    for cm in comms:
        for i_in, i_out in cm.aliases.items():
            aliases[i_at + i_in] = o_at + i_out
        i_at += len(cm.ins)
        o_at += len(cm.outs)
        sem_shapes += [pltpu.SemaphoreType.DMA((cm.n_sems,)), pltpu.SemaphoreType.DMA((cm.n_sems,))]
    specs = dict(grid=grid, in_specs=[adapt(s) for s in in_specs] + [any_spec] * n_ci,
                 out_specs=[adapt(s) for s in out_specs] + [any_spec] * n_co,
                 scratch_shapes=list(scratch_shapes) + sem_shapes)
    if place is not None:
        specs = dict(grid_spec=pltpu.PrefetchScalarGridSpec(num_scalar_prefetch=1, **specs))
    if comms:
        params = pltpu.CompilerParams(dimension_semantics=("arbitrary",) * len(grid),
                                      vmem_limit_bytes=VMEM_LIMIT, has_side_effects=True)
    else:
        params = _params(*(sem or ("arbitrary",) * len(grid)))
    return pl.pallas_call(
        full_body, out_shape=list(out_shape) + c_outs_all,
        input_output_aliases={k + n_place: v for k, v in aliases.items()},
        compiler_params=params, name=name, **specs,
    )(*(() if place is None else (place,)), *operands, *c_ins_all)


def _matmul(a, b, *, ta=False, tb=False, tm, tn, tk, out_dtypes, name, extras=(), epilogue=None,
            out_shapes=None, out_specs=None, b_spec=None, n_dim=None, into=None, side=None, comm=None,
            place=None):
    n_into = 0 if into is None else 1
    if ta:
        k_dim, m_dim = a.shape
    else:
        m_dim, k_dim = a.shape
    if n_dim is None:
        n_dim = b.shape[0] if tb else b.shape[1]
        assert (b.shape[1] if tb else b.shape[0]) == k_dim
    assert m_dim % tm == 0 and n_dim % tn == 0 and k_dim % tk == 0, (name, a.shape, b.shape)
    nk = k_dim // tk
    n_ex, n_out = len(extras), len(out_dtypes)
    dims = (((0 if ta else 1,), (1 if tb else 0,)), ((), ()))
    grid = (m_dim // tm, n_dim // tn, nk)
    s_ins, s_outs, s_scr = (len(side.ins), len(side.outs), len(side.scratch)) if side else (0, 0, 0)

    def body(*refs):
        a_ref, b_ref = refs[0], refs[1]
        ex_refs = refs[2:2 + n_ex]
        i0 = 2 + n_ex + n_into
        side_in = refs[i0:i0 + s_ins]
        o_refs = refs[i0 + s_ins:i0 + s_ins + n_out]
        side_out = refs[i0 + s_ins + n_out:i0 + s_ins + n_out + s_outs]
        side_scr = refs[len(refs) - s_scr:] if s_scr else ()
        if side is not None:
            step = (pl.program_id(0) * grid[1] + pl.program_id(1)) * grid[2] + pl.program_id(2)

            @pl.when(step == 0)
            def _():
                side.init(side_scr)

        def dot():
            if side is not None:
                side.body(step, side_in, side_out, side_scr)
            return lax.dot_general(a_ref[...].astype(BF16), b_ref[...].astype(BF16), dims,
                                   preferred_element_type=F32)

        def finish(acc):
            outs = epilogue(acc, *[e[...] for e in ex_refs]) if epilogue is not None else (acc,)
            for o_ref, o in zip(o_refs, outs):
                o_ref[...] = o.astype(o_ref.dtype)

        if nk == 1:
            finish(dot())
        else:
            acc_ref = refs[len(refs) - s_scr - 1]
            k = pl.program_id(2)

            @pl.when(k == 0)
            def _():
                acc_ref[...] = dot()

            @pl.when(jnp.logical_and(k > 0, k < nk - 1))
            def _():
                acc_ref[...] += dot()

            @pl.when(k == nk - 1)
            def _():
                finish(acc_ref[...] + dot())

    a_spec = (pl.BlockSpec((tk, tm), lambda i, j, k: (k, i)) if ta
              else pl.BlockSpec((tm, tk), lambda i, j, k: (i, k)))
    if b_spec is None:
        b_spec = (pl.BlockSpec((tn, tk), lambda i, j, k: (j, k)) if tb
                  else pl.BlockSpec((tk, tn), lambda i, j, k: (k, j)))
    io_spec = pl.BlockSpec((tm, tn), lambda i, j, k: (i, j))

    def extra_spec(e):
        if e.shape == (m_dim, n_dim):
            return io_spec
        if e.shape[1] == n_dim:
            return pl.BlockSpec((e.shape[0], tn), lambda i, j, k: (0, j))
        assert e.shape[0] == m_dim, (name, e.shape)
        return pl.BlockSpec((tm, e.shape[1]), lambda i, j, k: (i, 0))

    if out_shapes is None:
        out_shapes = [jax.ShapeDtypeStruct((m_dim, n_dim), dt) for dt in out_dtypes]
    if out_specs is None:
        out_specs = [io_spec] * n_out
    return _call(
        body,
        grid=grid,
        in_specs=([a_spec, b_spec] + [extra_spec(e) for e in extras]
                  + [pl.BlockSpec(memory_space=pl.ANY)] * n_into
                  + (list(side.in_specs) if side else [])),
        out_specs=list(out_specs) + (list(side.out_specs) if side else []),
        out_shape=list(out_shapes) + (list(side.outs) if side else []),
        scratch_shapes=([pltpu.VMEM((tm, tn), F32)] if nk > 1 else []) + (list(side.scratch) if side else []),
        sem=("parallel", "parallel", "arbitrary") if side is None else None,
        aliases={2 + n_ex: 0} if n_into else None,
        operands=(a, b, *extras) + ((into,) if n_into else ()) + (tuple(side.ins) if side else ()),
        name=name, comm=comm, place=place)


def _add_epilogue(acc, r):
    return (acc + r,)


def _residual_norm_epilogue(acc, x, g):
    x2 = acc + x
    r = lax.rsqrt(jnp.mean(x2 * x2, axis=-1, keepdims=True) + EPS)
    return x2, x2 * r * g


def _rms_fwd(x, g, *, name, tm=512, comm=None, side=None, place=None):
    s_len, d = x.shape
    s_ins, s_outs = (len(side.ins), len(side.outs)) if side else (0, 0)

    def body(*refs):
        x_ref, g_ref = refs[0], refs[1]
        o_ref = refs[2 + s_ins]
        xv = x_ref[...]
        r = lax.rsqrt(jnp.mean(xv * xv, axis=-1, keepdims=True) + EPS)
        o_ref[...] = (xv * r * g_ref[...]).astype(o_ref.dtype)
        if side is not None:
            scr = refs[3 + s_ins + s_outs:]

            @pl.when(pl.program_id(0) == 0)
            def _():
                side.init(scr)

            side.body(pl.program_id(0), refs[2:2 + s_ins], refs[3 + s_ins:3 + s_ins + s_outs], scr)

    row = pl.BlockSpec((tm, d), lambda i, j, k: (i, 0))
    return _call(
        body, grid=(s_len // tm, 1, 1),
        in_specs=[row, pl.BlockSpec((1, d), lambda i, j, k: (0, 0))] + (list(side.in_specs) if side else []),
        out_specs=[row] + (list(side.out_specs) if side else []),
        out_shape=[jax.ShapeDtypeStruct((s_len, d), BF16)] + (list(side.outs) if side else []),
        scratch_shapes=list(side.scratch) if side else [],
        operands=(x, g) + (tuple(side.ins) if side else ()), name=name, comm=comm, place=place)


NORM_ROWS = 16


def _zero_refs(refs):
    for r in refs:
        r[...] = jnp.zeros_like(r)


def _rms_bwd_block(dn_ref, x_ref, g_ref, res_ref, dx_ref, dg_ref, acc_ref):
    gv = g_ref[...]
    acc = acc_ref[...]
    for s in range(x_ref.shape[0] // NORM_ROWS):
        sl = slice(s * NORM_ROWS, (s + 1) * NORM_ROWS)
        xv = x_ref[sl, :]
        dnv = dn_ref[sl, :].astype(F32)
        r = lax.rsqrt(jnp.mean(xv * xv, axis=-1, keepdims=True) + EPS)
        xh = xv * r
        acc = acc + _rowsum8(dnv * xh)
        dxh = dnv * gv
        dx_ref[sl, :] = (r * (dxh - xh * jnp.mean(dxh * xh, axis=-1, keepdims=True))
                         + res_ref[sl, :].astype(F32))
    acc_ref[...] = acc
    dg_ref[...] = jnp.sum(acc, axis=0, keepdims=True)


class _Side:
    def __init__(self, ins, in_specs, outs, out_specs, scratch, init, body):
        self.ins, self.in_specs, self.outs, self.out_specs = ins, in_specs, outs, out_specs
        self.scratch = scratch
        self.init = init
        self.body = body


def _rms_bwd_side(dn, x, g, res, *, block_of_step, n_steps):
    s_len, d = x.shape
    row = pl.BlockSpec((s_len // n_steps, d), lambda i, j, k: (block_of_step(i, j, k), 0))
    vec = pl.BlockSpec((1, d), lambda i, j, k: (0, 0))
    return _Side(
        ins=[dn, x, g, res], in_specs=[row, row, vec, row],
        outs=[jax.ShapeDtypeStruct((s_len, d), F32), jax.ShapeDtypeStruct((1, d), F32)],
        out_specs=[row, vec], scratch=[pltpu.VMEM((SUBLANES, d), F32)],
        init=_zero_refs, body=lambda step, ins, outs, scr: _rms_bwd_block(*ins, *outs, *scr))


def _rms_bwd(dn, x, g, res, *, name, tm=512, comm=None):
    s_len, d = x.shape
    n = s_len // tm

    def body(*refs):
        @pl.when(pl.program_id(0) == 0)
        def _():
            _zero_refs(refs[-1:])

        _rms_bwd_block(*refs)

    row = pl.BlockSpec((tm, d), lambda i: (i, 0))
    vec = pl.BlockSpec((1, d), lambda i: (0, 0))
    return _call(
        body, grid=(n,),
        in_specs=[row, row, vec, row],
        out_specs=[row, vec],
        out_shape=[jax.ShapeDtypeStruct((s_len, d), F32), jax.ShapeDtypeStruct((1, d), F32)],
        scratch_shapes=[pltpu.VMEM((SUBLANES, d), F32)],
        operands=(dn, x, g, res), name=name, comm=comm)


def _loss_head(x3, g, target, *, tm=512):
    s_len, d = x3.shape
    n = s_len // tm

    def body(x_ref, g_ref, t_ref, dxb_ref, dg_ref, loss_ref, accg_ref, accl_ref):
        i = pl.program_id(0)
        xv = x_ref[...]
        gv = g_ref[...]
        r = lax.rsqrt(jnp.mean(xv * xv, axis=-1, keepdims=True) + EPS)
        xh = xv * r
        err = xh * gv - t_ref[...]

        @pl.when(i == 0)
        def _():
            accg_ref[...] = jnp.zeros_like(accg_ref)
            accl_ref[...] = jnp.zeros_like(accl_ref)

        accl_ref[...] += _rowsum8(err * err)
        dn = err * (1.0 / d)
        accg_ref[...] += _rowsum8(dn * xh)
        dxh = dn * gv
        dx = r * (dxh - xh * jnp.mean(dxh * xh, axis=-1, keepdims=True))
        dxb_ref[...] = dx.astype(BF16)

        @pl.when(i == n - 1)
        def _():
            dg_ref[...] = jnp.sum(accg_ref[...], axis=0, keepdims=True)
            tot = jnp.sum(jnp.sum(accl_ref[...], axis=0, keepdims=True), axis=1, keepdims=True)
            loss_ref[...] = jnp.broadcast_to(tot * (0.5 / d), (1, LANES))

    row = pl.BlockSpec((tm, d), lambda i: (i, 0))
    vec = pl.BlockSpec((1, d), lambda i: (0, 0))
    return pl.pallas_call(
        body, grid=(n,),
        in_specs=[row, vec, row],
        out_specs=[row, vec, pl.BlockSpec((1, LANES), lambda i: (0, 0))],
        out_shape=[jax.ShapeDtypeStruct((s_len, d), BF16),
                   jax.ShapeDtypeStruct((1, d), F32), jax.ShapeDtypeStruct((1, LANES), F32)],
        scratch_shapes=[pltpu.VMEM((SUBLANES, d), F32), pltpu.VMEM((SUBLANES, d), F32)],
        compiler_params=_params("arbitrary"), name="loss_head",
    )(x3, g, target)


def _shift_down(v, k, rows_before, row):
    out = pltpu.roll(v, k, axis=0)
    for j in range(k):
        out = jnp.where(row == j, rows_before[j], out)
    return out


def _shift_up(v, k, rows_after, row):
    t = v.shape[0]
    out = pltpu.roll(v, t - k, axis=0)
    for j in range(k):
        out = jnp.where(row == t - k + j, rows_after[j], out)
    return out


def _conv_fwd(z, w_t, gain, *, ts=512):
    s_len = z.shape[0]
    n_grp = D_CONV // GROUP

    def body(cb_ref, cc_ref, ch_ref, w_ref, g_ref, y_ref, carry_ref):
        i = pl.program_id(0)

        @pl.when(i == 0)
        def _():
            carry_ref[...] = jnp.zeros_like(carry_ref)

        row = lax.broadcasted_iota(jnp.int32, (ts, GROUP), 0)
        for g in range(n_grp):
            sl = slice(g * GROUP, (g + 1) * GROUP)
            uu = cc_ref[:, sl] * ch_ref[:, sl]
            p2 = carry_ref[6:7, sl]
            p1 = carry_ref[7:8, sl]
            u1 = _shift_down(uu, 1, [p1], row)
            u2 = _shift_down(uu, 2, [p2, p1], row)
            conv = w_ref[0:1, sl] * u2 + w_ref[1:2, sl] * u1 + w_ref[2:3, sl] * uu
            y = cb_ref[:, sl] * conv
            carry_ref[:, sl] = uu[ts - SUBLANES:ts, :]
            rg = lax.rsqrt(jnp.mean(y * y, axis=-1, keepdims=True) + EPS)
            y_ref[:, sl] = (y * rg * g_ref[:, sl]).astype(BF16)

    def col(j):
        return pl.BlockSpec((ts, D_CONV), lambda i, j=j: (i, j))

    small = lambda r: pl.BlockSpec((r, D_CONV), lambda i: (0, 0))
    return pl.pallas_call(
        body, grid=(s_len // ts,),
        in_specs=[col(0), col(1), col(2), small(3), small(1)],
        out_specs=col(0),
        out_shape=jax.ShapeDtypeStruct((s_len, 2 * D_CONV), BF16),
        scratch_shapes=[pltpu.VMEM((SUBLANES, D_CONV), F32)],
        compiler_params=_params("arbitrary"), name="conv_fwd",
    )(z, z, z, w_t, gain)


def _conv_bwd_side(dy, z, w_t, gain, *, n_steps, step_of):
    s_len = z.shape[0]
    n = n_steps
    ts = s_len // n
    n_grp = D_CONV // GROUP
    halo_blocks = ts // SUBLANES

    def body(step, ins, outs, scr):
        dy_ref, cb_ref, cc_ref, ch_ref, hcc_ref, hch_ref, w_ref, g_ref = ins
        dz_ref, dw_ref, dg_ref = outs
        carry_ref, accw_ref, accg_ref = scr
        first_tile = (n - 1 - step) == 0
        row = lax.broadcasted_iota(jnp.int32, (ts, GROUP), 0)
        keep = jnp.where(first_tile, 0.0, 1.0)
        for g in range(n_grp):
            sl = slice(g * GROUP, (g + 1) * GROUP)
            cc = cc_ref[:, sl]
            ch = ch_ref[:, sl]
            cb = cb_ref[:, sl]
            uu = cc * ch
            p2 = hcc_ref[6:7, sl] * hch_ref[6:7, sl] * keep
            p1 = hcc_ref[7:8, sl] * hch_ref[7:8, sl] * keep
            u1 = _shift_down(uu, 1, [p1], row)
            u2 = _shift_down(uu, 2, [p2, p1], row)
            w0, w1, w2 = w_ref[0:1, sl], w_ref[1:2, sl], w_ref[2:3, sl]
            conv = w0 * u2 + w1 * u1 + w2 * uu
            y = cb * conv
            rg = lax.rsqrt(jnp.mean(y * y, axis=-1, keepdims=True) + EPS)
            yh = y * rg
            dyv = dy_ref[:, sl].astype(F32)
            accg_ref[:, sl] += _rowsum8(dyv * yh)
            dyn = dyv * g_ref[:, sl]
            dpre = rg * (dyn - yh * jnp.mean(dyn * yh, axis=-1, keepdims=True))
            dz_ref[:, sl] = (dpre * conv).astype(BF16)
            dconv = dpre * cb
            accw_ref[0:8, sl] += _rowsum8(dconv * u2)
            accw_ref[8:16, sl] += _rowsum8(dconv * u1)
            accw_ref[16:24, sl] += _rowsum8(dconv * uu)
            n0 = carry_ref[0:1, sl]
            n1 = carry_ref[1:2, sl]
            d1 = _shift_up(dconv, 1, [n0], row)
            d2 = _shift_up(dconv, 2, [n0, n1], row)
            duu = w2 * dconv + w1 * d1 + w0 * d2
            carry_ref[:, sl] = dconv[0:SUBLANES, :]
            dz_ref[:, D_CONV + g * GROUP:D_CONV + (g + 1) * GROUP] = (duu * ch).astype(BF16)
            dz_ref[:, 2 * D_CONV + g * GROUP:2 * D_CONV + (g + 1) * GROUP] = (duu * cc).astype(BF16)

        for k in range(3):
            dw_ref[k:k + 1, :] = jnp.sum(accw_ref[8 * k:8 * k + 8, :], axis=0, keepdims=True)
        dg_ref[...] = jnp.sum(accg_ref[...], axis=0, keepdims=True)

    def tile(i, j, k):
        return n - 1 - step_of(i, j, k)

    def col(c):
        return pl.BlockSpec((ts, D_CONV), lambda i, j, k, c=c: (tile(i, j, k), c))

    def halo(c):
        return pl.BlockSpec((SUBLANES, D_CONV),
                            lambda i, j, k, c=c: (jnp.maximum(tile(i, j, k) * halo_blocks - 1, 0), c))

    small = lambda r: pl.BlockSpec((r, D_CONV), lambda i, j, k: (0, 0))
    return _Side(
        ins=[dy, z, z, z, z, z, w_t, gain],
        in_specs=[col(0), col(0), col(1), col(2), halo(1), halo(2), small(3), small(1)],
        outs=[jax.ShapeDtypeStruct((s_len, 6 * D_CONV), BF16),
              jax.ShapeDtypeStruct((3, D_CONV), F32), jax.ShapeDtypeStruct((1, D_CONV), F32)],
        out_specs=[pl.BlockSpec((ts, 3 * D_CONV), lambda i, j, k: (tile(i, j, k), 0)), small(3), small(1)],
        scratch=[pltpu.VMEM((SUBLANES, D_CONV), F32), pltpu.VMEM((24, D_CONV), F32),
                 pltpu.VMEM((SUBLANES, D_CONV), F32)],
        init=_zero_refs, body=body)


def _split3(v):
    hi = v.astype(BF16)
    r1 = v - hi.astype(F32)
    mid = r1.astype(BF16)
    lo = (r1 - mid.astype(F32)).astype(BF16)
    return jnp.concatenate([hi, mid, lo], axis=1)


def _tri_sum(tri, v):
    w = v.shape[1]
    dd = jnp.dot(tri, _split3(v), preferred_element_type=F32)
    return dd[:, :w] + dd[:, w:2 * w] + dd[:, 2 * w:]


def _chunk_masks(ts):
    r = jnp.arange(ts)
    same = (r[:, None] // CHUNK) == (r[None, :] // CHUNK)
    later = jnp.logical_and(same, r[None, :] > r[:, None]).astype(BF16)
    earlier = jnp.logical_and(same, r[None, :] < r[:, None]).astype(BF16)
    chunk_of_row = jnp.arange(ts // CHUNK * SUBLANES)[:, None] // SUBLANES
    member = (chunk_of_row == (r[None, :] // CHUNK)).astype(BF16)
    return later, earlier, member


def _sigmoid(v):
    return 0.5 * jnp.tanh(0.5 * v) + 0.5


def _gla_fwd(z, alow, wgu, bg, gg, y_in, *, ts=512, comm=None):
    s_len = z.shape[0]
    nch = ts // CHUNK
    scale = DK ** -0.5

    tri_u, _, ind8 = _chunk_masks(ts)

    def body(q_ref, k_ref, v_ref, og_ref, al_ref, wgu_ref, bg_ref, gg_ref, tu_ref, ind_ref, yin_ref,
             y_ref, o_ref, la_ref, st_ref, state_ref, kd_ref, qs_ref, dec_ref):
        del yin_ref
        i = pl.program_id(0)

        @pl.when(i == 0)
        def _():
            state_ref[...] = jnp.zeros_like(state_ref)

        pre = jnp.dot(al_ref[...].astype(BF16), wgu_ref[...], preferred_element_type=F32) + bg_ref[...]
        la = (jnp.minimum(pre, 0.0) - jnp.log(1.0 + jnp.exp(-jnp.abs(pre)))) * (1.0 / 16.0)
        la_ref[...] = la
        kd_ref[...] = (k_ref[...] * jnp.exp(_tri_sum(tu_ref[...], la))).astype(BF16)
        qs_ref[...] = (q_ref[...] * scale).astype(BF16)
        dec_ref[...] = jnp.exp(_tri_sum(ind_ref[...], la))

        def chunk(cl, carry):
            rows = pl.ds(pl.multiple_of(cl * CHUNK, CHUNK), CHUNK)
            dec = dec_ref[pl.ds(pl.multiple_of(cl * SUBLANES, SUBLANES), 1), :]
            for h in range(HEADS):
                ks = slice(h * DK, (h + 1) * DK)
                vs = slice(h * DV, (h + 1) * DV)
                kv_t = lax.dot_general(v_ref[rows, vs].astype(BF16), kd_ref[rows, ks],
                                       (((0,), (0,)), ((), ())), preferred_element_type=F32)
                st = state_ref[h] * dec[:, ks] + kv_t
                state_ref[h] = st
                st_ref[cl, h] = st
                o_ref[rows, vs] = lax.dot_general(qs_ref[rows, ks], st.astype(BF16),
                                                  (((1,), (1,)), ((), ())), preferred_element_type=F32)
            return carry

        lax.fori_loop(0, nch, chunk, 0, unroll=2)

        ggv = gg_ref[...]
        for h in range(HEADS):
            vs = slice(h * DV, (h + 1) * DV)
            o_h = o_ref[:, vs]
            og_h = og_ref[:, vs]
            ro = lax.rsqrt(jnp.mean(o_h * o_h, axis=-1, keepdims=True) + EPS)
            y_ref[:, vs] = (o_h * ro * ggv * (og_h * _sigmoid(og_h))).astype(BF16)

    def zcol(width, j):
        return pl.BlockSpec((ts, width), lambda i, j=j: (i, j))

    full = lambda shape: pl.BlockSpec(shape, lambda i: tuple(0 for _ in shape))
    return _call(
        body, grid=(s_len // ts,),
        in_specs=[zcol(512, 6), zcol(512, 7), zcol(1024, 4), zcol(1024, 5), zcol(LANES, 0),
                  full((LANES, 512)), full((1, 512)), full((1, DV)), full(tri_u.shape), full(ind8.shape),
                  pl.BlockSpec(memory_space=pl.ANY)],
        out_specs=[zcol(1024, 1), zcol(1024, 0), zcol(512, 0),
                   pl.BlockSpec((nch, HEADS, DV, DK), lambda i: (i, 0, 0, 0))],
        out_shape=[jax.ShapeDtypeStruct((s_len, 2048), BF16), jax.ShapeDtypeStruct((s_len, 1024), F32),
                   jax.ShapeDtypeStruct((s_len, 512), F32),
                   jax.ShapeDtypeStruct((s_len // CHUNK, HEADS, DV, DK), F32)],
        scratch_shapes=[pltpu.VMEM((HEADS, DV, DK), F32), pltpu.VMEM((ts, 512), BF16),
                        pltpu.VMEM((ts, 512), BF16), pltpu.VMEM((nch * SUBLANES, 512), F32)],
        aliases={10: 0},
        operands=(z, z, z, z, alow, wgu, bg, gg, tri_u, ind8, y_in), name="gla_fwd", comm=comm)


def _gla_bwd(dy, z, o, la, st, alow, wgu, gg, dz_in, *, ts=512, comm=None, side=None, place=None):
    s_len = z.shape[0]
    n = s_len // ts
    nch = ts // CHUNK
    scale = DK ** -0.5

    tri_u, tri_l, ind8 = _chunk_masks(ts)

    def body(dy_ref, q_ref, k_ref, v_ref, og_ref, o_ref, la_ref, st_ref, stp_ref, al_ref, wgu_ref,
             gg_ref, tu_ref, tl_ref, ind_ref, dzin_ref, dz_ref, dal_ref, dwgu_ref, dbg_ref, dgg_ref,
             gt_ref, decn_ref, accw_ref, accb_ref, accg_ref, dla_ref,
             e_ref, kd_ref, kdb_ref, qs_ref, do_ref, dkd_ref, dec_ref, dbe_ref):
        del dzin_ref
        i = pl.program_id(0)
        first_tile = (n - 1 - i) == 0

        @pl.when(i == 0)
        def _():
            gt_ref[...] = jnp.zeros_like(gt_ref)
            decn_ref[...] = jnp.ones_like(decn_ref)
            accw_ref[...] = jnp.zeros_like(accw_ref)
            accb_ref[...] = jnp.zeros_like(accb_ref)
            accg_ref[...] = jnp.zeros_like(accg_ref)

        la = la_ref[...]
        e_dec = jnp.exp(_tri_sum(tu_ref[...], la))
        e_ref[...] = e_dec
        kd = k_ref[...] * e_dec
        kd_ref[...] = kd
        kdb_ref[...] = kd.astype(BF16)
        qs_ref[...] = (q_ref[...] * scale).astype(BF16)
        dec_ref[...] = jnp.exp(_tri_sum(ind_ref[...], la))
        ggv = gg_ref[...]
        for h in range(HEADS):
            vs = slice(h * DV, (h + 1) * DV)
            o_h = o_ref[:, vs]
            og_h = og_ref[:, vs]
            dy_h = dy_ref[:, vs].astype(F32)
            ro = lax.rsqrt(jnp.mean(o_h * o_h, axis=-1, keepdims=True) + EPS)
            oh = o_h * ro
            sig = _sigmoid(og_h)
            sil = og_h * sig
            accg_ref[...] += _rowsum8(dy_h * oh * sil)
            dz_ref[:, 2048 + h * DV:2048 + (h + 1) * DV] = (
                dy_h * oh * ggv * sig * (1.0 + og_h * (1.0 - sig))).astype(BF16)
            don = dy_h * ggv * sil
            do_ref[:, vs] = (ro * (don - oh * jnp.mean(don * oh, axis=-1, keepdims=True))).astype(BF16)
        keep = jnp.where(first_tile, 0.0, 1.0)

        def chunk(jrev, decn):
            cl = nch - 1 - jrev
            rows = pl.ds(pl.multiple_of(cl * CHUNK, CHUNK), CHUNK)
            one_row = pl.ds(pl.multiple_of(cl * SUBLANES, SUBLANES), 1)
            dec = dec_ref[one_row, :]
            has_prev = jnp.where(cl > 0, 1.0, 0.0)
            prev_idx = jnp.maximum(cl - 1, 0)
            for h in range(HEADS):
                ks = slice(h * DK, (h + 1) * DK)
                vs = slice(h * DV, (h + 1) * DV)
                dob = do_ref[rows, vs]
                s_c = st_ref[cl, h]
                dqs = jnp.dot(dob, s_c.astype(BF16), preferred_element_type=F32)
                dz_ref[rows, ks] = (dqs * scale).astype(BF16)
                gt = gt_ref[h] * decn[:, ks] + lax.dot_general(
                    dob, qs_ref[rows, ks], (((0,), (0,)), ((), ())), preferred_element_type=F32)
                gt_ref[h] = gt
                gb = gt.astype(BF16)
                dkd_ref[rows, ks] = jnp.dot(v_ref[rows, vs].astype(BF16), gb, preferred_element_type=F32)
                dz_ref[rows, 1024 + h * DV:1024 + (h + 1) * DV] = lax.dot_general(
                    kdb_ref[rows, ks], gb, (((1,), (1,)), ((), ())),
                    preferred_element_type=F32).astype(BF16)
                s_prev = has_prev * st_ref[prev_idx, h] + (1.0 - has_prev) * keep * stp_ref[0, h]
                dbe_ref[one_row, ks] = jnp.sum(gt * s_prev, axis=0, keepdims=True) * dec[:, ks]
            return dec

        decn_ref[0:1, :] = lax.fori_loop(0, nch, chunk, decn_ref[0:1, :], unroll=2)

        dkd = dkd_ref[...]
        dz_ref[:, 512:1024] = (dkd * e_ref[...]).astype(BF16)
        dla_ref[...] = _tri_sum(tl_ref[...], dkd * kd_ref[...])
        for c in range(nch):
            dla_ref[c * CHUNK:(c + 1) * CHUNK, :] += dbe_ref[c * SUBLANES:c * SUBLANES + 1, :]
        dpre = dla_ref[...] * (1.0 / 16.0) * (1.0 - jnp.exp(16.0 * la))
        accb_ref[...] += _rowsum8(dpre)
        dpb = dpre.astype(BF16)
        accw_ref[...] += lax.dot_general(al_ref[...].astype(BF16), dpb, (((0,), (0,)), ((), ())),
                                         preferred_element_type=F32)
        dal_ref[...] = lax.dot_general(dpb, wgu_ref[...], (((1,), (1,)), ((), ())),
                                       preferred_element_type=F32)

        @pl.when(i == n - 1)
        def _():
            dwgu_ref[...] = accw_ref[...]
            dbg_ref[...] = jnp.sum(accb_ref[...], axis=0, keepdims=True)
            dgg_ref[...] = jnp.sum(accg_ref[...], axis=0, keepdims=True)

    def zcol(width, j):
        return pl.BlockSpec((ts, width), lambda i, j=j: (n - 1 - i, j))

    full = lambda shape: pl.BlockSpec(shape, lambda i: tuple(0 for _ in shape))
    n_in, n_out, n_scr = 16, 5, 14
    s_ins, s_outs = (len(side.ins), len(side.outs)) if side else (0, 0)

    def body_with_side(*refs):
        outs_at, scr_at = n_in + s_ins, n_in + s_ins + n_out + s_outs
        if side is not None:
            side_scr = refs[scr_at + n_scr:]

            @pl.when(pl.program_id(0) == 0)
            def _():
                side.init(side_scr)

            side.body(pl.program_id(0), refs[n_in:outs_at], refs[outs_at + n_out:scr_at], side_scr)
        body(*refs[:n_in], *refs[outs_at:outs_at + n_out], *refs[scr_at:scr_at + n_scr])

    def one_axis(spec):
        return type(spec)(spec.block_shape, lambda i, *p, f=spec.index_map: f(i, 0, 0, *p))

    return _call(
        body_with_side, grid=(n,),
        in_specs=[zcol(1024, 1), zcol(512, 6), zcol(512, 7), zcol(1024, 4), zcol(1024, 5),
                  zcol(1024, 0), zcol(512, 0),
                  pl.BlockSpec((nch, HEADS, DV, DK), lambda i: (n - 1 - i, 0, 0, 0)),
                  pl.BlockSpec((1, HEADS, DV, DK),
                               lambda i: (jnp.maximum((n - 1 - i) * nch - 1, 0), 0, 0, 0)),
                  zcol(LANES, 0), full((LANES, 512)), full((1, DV)),
                  full(tri_u.shape), full(tri_l.shape), full(ind8.shape),
                  pl.BlockSpec(memory_space=pl.ANY)] + ([one_axis(s) for s in side.in_specs] if side else []),
        out_specs=([zcol(3072, 1), zcol(LANES, 0), full((LANES, 512)), full((1, 512)), full((1, DV))]
                   + ([one_axis(s) for s in side.out_specs] if side else [])),
        out_shape=[jax.ShapeDtypeStruct((s_len, 6144), BF16), jax.ShapeDtypeStruct((s_len, LANES), F32),
                   jax.ShapeDtypeStruct((LANES, 512), F32), jax.ShapeDtypeStruct((1, 512), F32),
                   jax.ShapeDtypeStruct((1, DV), F32)] + (list(side.outs) if side else []),
        scratch_shapes=[pltpu.VMEM((HEADS, DV, DK), F32), pltpu.VMEM((SUBLANES, 512), F32),
                        pltpu.VMEM((LANES, 512), F32), pltpu.VMEM((SUBLANES, 512), F32),
                        pltpu.VMEM((SUBLANES, DV), F32), pltpu.VMEM((ts, 512), F32),
                        pltpu.VMEM((ts, 512), F32), pltpu.VMEM((ts, 512), F32), pltpu.VMEM((ts, 512), BF16),
                        pltpu.VMEM((ts, 512), BF16), pltpu.VMEM((ts, 1024), BF16),
                        pltpu.VMEM((ts, 512), F32), pltpu.VMEM((nch * SUBLANES, 512), F32),
                        pltpu.VMEM((nch * SUBLANES, 512), F32)] + (list(side.scratch) if side else []),
        aliases={15: 0},
        operands=((dy, z, z, z, z, o, la, st, st, alow, wgu, gg, tri_u, tri_l, ind8, dz_in)
                  + (tuple(side.ins) if side else ())),
        name="gla_bwd", comm=comm, place=place)


def _adamw(w, g, m, v, *, name):
    rows, cols = w.shape
    tr, tc = _tile(rows, cols)

    def body(w_ref, g_ref, m_ref, v_ref, go_ref, d_ref, nm_ref, nv_ref):
        gv = g_ref[...]
        go_ref[...] = gv
        m2 = ADAM_B1 * m_ref[...] + (1.0 - ADAM_B1) * gv
        v2 = ADAM_B2 * v_ref[...] + (1.0 - ADAM_B2) * jnp.square(gv)
        m_hat = m2 / (1.0 - ADAM_B1 ** ADAM_STEP)
        v_hat = v2 / (1.0 - ADAM_B2 ** ADAM_STEP)
        d_ref[...] = -ADAM_LR * (m_hat / (jnp.sqrt(v_hat) + ADAM_EPS) + ADAM_WD * w_ref[...])
        nm_ref[...] = m2
        nv_ref[...] = v2

    blk = pl.BlockSpec((tr, tc), lambda i, j: (i, j))
    shp = jax.ShapeDtypeStruct((rows, cols), F32)
    return pl.pallas_call(
        body, grid=(rows // tr, cols // tc), in_specs=[blk] * 4, out_specs=[blk] * 4, out_shape=[shp] * 4,
        compiler_params=_params("parallel", "parallel"), name=name,
    )(w, g, m, v)


def _my_place():
    return lax.axis_index("x"), lax.axis_index("y"), lax.axis_index("c")


def _flip(v, bit):
    return 1 - v if bit else v


def _allgather_small(buf, *, reduce, name):
    rows = buf.shape[0]

    def body(in_ref, out_ref, gat_ref, send_sems, recv_sems):
        x, y, c = _my_place()
        me = 4 * x + 2 * y + c
        gat_ref[me] = in_ref[...]
        copies = []
        for m in range(1, N_DEV):
            peer = (_flip(x, m & 4), _flip(y, m & 2), _flip(c, m & 1))
            cp = pltpu.make_async_remote_copy(
                src_ref=in_ref, dst_ref=gat_ref.at[me],
                send_sem=send_sems.at[m - 1], recv_sem=recv_sems.at[m - 1],
                device_id=peer, device_id_type=MESH)
            cp.start()
            copies.append(cp)
        for m in range(1, N_DEV):
            px, py, pc = _flip(x, m & 4), _flip(y, m & 2), _flip(c, m & 1)
            src_slot = gat_ref.at[4 * px + 2 * py + pc]
            pltpu.make_async_remote_copy(
                src_ref=src_slot, dst_ref=src_slot,
                send_sem=send_sems.at[m - 1], recv_sem=recv_sems.at[m - 1],
                device_id=(px, py, pc), device_id_type=MESH).wait_recv()
        for cp in copies:
            cp.wait_send()
        if reduce:
            tot = gat_ref[0]
            for d in range(1, N_DEV):
                tot = tot + gat_ref[d]
            out_ref[...] = tot
        else:
            out_ref[...] = gat_ref[...]

    out_shape = (rows, LANES) if reduce else (N_DEV, rows, LANES)
    return pl.pallas_call(
        body,
        in_specs=[pl.BlockSpec(memory_space=pltpu.VMEM)],
        out_specs=pl.BlockSpec(memory_space=pltpu.VMEM),
        out_shape=jax.ShapeDtypeStruct(out_shape, F32),
        scratch_shapes=[pltpu.VMEM((N_DEV, rows, LANES), F32),
                        pltpu.SemaphoreType.DMA((N_DEV - 1,)), pltpu.SemaphoreType.DMA((N_DEV - 1,))],
        compiler_params=pltpu.CompilerParams(has_side_effects=True),
        name=name,
    )(buf)


def _cast_into(shard, chip_core, *, name):
    rows, cols = shard.shape
    tr, tc = _tile(rows, cols)

    def body(cc_ref, s_ref, o_ref):
        del cc_ref
        o_ref[...] = s_ref[...].astype(BF16)

    grid_spec = pltpu.PrefetchScalarGridSpec(
        num_scalar_prefetch=1, grid=(rows // tr, cols // tc),
        in_specs=[pl.BlockSpec((tr, tc), lambda r, q, cc: (r, q))],
        out_specs=pl.BlockSpec((None, tr, tc), lambda r, q, cc: (cc[0], r, q)))
    return pl.pallas_call(
        body, grid_spec=grid_spec, out_shape=jax.ShapeDtypeStruct((N_CHIPS, rows, cols), BF16),
        compiler_params=_params("arbitrary", "arbitrary"), name=name,
    )(chip_core, shard)


def _remote(src, dst, send_sems, recv_sems, k, device):
    return pltpu.make_async_remote_copy(src_ref=src, dst_ref=dst, send_sem=send_sems.at[k],
                                        recv_sem=recv_sems.at[k], device_id=device, device_id_type=MESH)


def _col_half(ref, h, *lead, rows=None):
    hc = ref.shape[-1] // 2
    mid = (slice(None),) * (len(ref.shape) - 2 - len(lead))
    row_sel = slice(None) if rows is None else pl.ds(rows[0], rows[1])
    return ref.at[tuple(lead) + mid + (row_sel, pl.ds(h * hc, hc))]


def _gather_comm(bufs, rows=None, mid_at=0.75):
    n_w, n_m = len(bufs), len(CHIP_MASKS)
    rows = rows or [None] * n_w

    def first(c_ins, c_outs, ss, rs):
        x, y, c = _my_place()
        chip = 2 * x + y
        for w in range(n_w):
            mine = _col_half(c_outs[w], c, chip, rows=rows[w])
            for mi, (mx, my) in enumerate(CHIP_MASKS):
                _remote(mine, mine, ss, rs, w * n_m + mi, (_flip(x, mx), _flip(y, my), c)).start()

    def mid(c_ins, c_outs, ss, rs):
        x, y, c = _my_place()
        for w in range(n_w):
            for mi, (mx, my) in enumerate(CHIP_MASKS):
                k = w * n_m + mi
                px, py = _flip(x, mx), _flip(y, my)
                landed = _col_half(c_outs[w], c, 2 * px + py, rows=rows[w])
                _remote(landed, landed, ss, rs, k, (px, py, c)).wait_recv()
                _remote(landed, landed, ss, rs, n_w * n_m + k, (x, y, 1 - c)).start()

    def last(c_ins, c_outs, ss, rs):
        x, y, c = _my_place()
        chip = 2 * x + y
        for w in range(n_w):
            mine = _col_half(c_outs[w], c, chip, rows=rows[w])
            for mi, (mx, my) in enumerate(CHIP_MASKS):
                k = w * n_m + mi
                px, py = _flip(x, mx), _flip(y, my)
                theirs = _col_half(c_outs[w], 1 - c, 2 * px + py, rows=rows[w])
                _remote(theirs, theirs, ss, rs, n_w * n_m + k, (x, y, 1 - c)).wait_recv()
                _remote(mine, mine, ss, rs, k, (px, py, c)).wait_send()
                _remote(mine, mine, ss, rs, n_w * n_m + k, (x, y, 1 - c)).wait_send()

    return _Comm(ins=bufs, outs=[jax.ShapeDtypeStruct(b.shape, b.dtype) for b in bufs],
                 aliases={w: w for w in range(n_w)}, n_sems=2 * n_w * n_m, first=first, mid=mid, last=last,
                 mid_at=mid_at)


def _swap_comm(grads):
    n_w = len(grads)

    def copy(c_ins, c_outs, ss, rs, w):
        x, y, c = _my_place()
        return _remote(_col_half(c_ins[w], 1 - c), c_outs[w], ss, rs, w, (x, y, 1 - c))

    def first(c_ins, c_outs, ss, rs):
        for w in range(n_w):
            copy(c_ins, c_outs, ss, rs, w).start()

    def last(c_ins, c_outs, ss, rs):
        for w in range(n_w):
            copy(c_ins, c_outs, ss, rs, w).wait()

    return _Comm(ins=grads,
                 outs=[jax.ShapeDtypeStruct(g.shape[:-1] + (g.shape[-1] // 2,), g.dtype) for g in grads],
                 aliases={}, n_sems=n_w, first=first, last=last)


def _add_own_half(g, other, chip_core, *, name):
    n_chip, rows, hc = other.shape
    tr, tc = _tile(rows, hc)
    per_half = hc // tc

    def body(cc_ref, g_ref, o_ref, out_ref):
        del cc_ref
        out_ref[...] = (g_ref[...].astype(F32) + o_ref[...].astype(F32)).astype(BF16)

    grid_spec = pltpu.PrefetchScalarGridSpec(
        num_scalar_prefetch=1, grid=(n_chip, rows // tr, per_half),
        in_specs=[pl.BlockSpec((None, tr, tc), lambda j, r, q, cc: (j, r, cc[1] * per_half + q)),
                  pl.BlockSpec((None, tr, tc), lambda j, r, q, cc: (j, r, q))],
        out_specs=pl.BlockSpec((None, tr, tc), lambda j, r, q, cc: (j, r, q)))
    return pl.pallas_call(
        body, grid_spec=grid_spec, out_shape=jax.ShapeDtypeStruct((n_chip, rows, hc), BF16),
        compiler_params=_params("parallel", "parallel", "parallel"), name=name,
    )(chip_core, g, other)


def _add_own_half_side(gs, others, *, n_steps, step_of):
    n_chip, rows, hc = others[0].shape
    per_chip = n_steps // n_chip
    tr = rows // per_chip
    assert all(o.shape == others[0].shape for o in others) and tr * per_chip == rows and tr % 16 == 0

    def body(step, ins, outs, scr):
        del step, scr
        for q, out_ref in enumerate(outs):
            out_ref[...] = (ins[2 * q][...].astype(F32) + ins[2 * q + 1][...].astype(F32)).astype(BF16)

    def blk(own_half):
        def index(i, j, k, place):
            s = step_of(i, j, k)
            return (s // per_chip, s % per_chip, place[1] if own_half else 0)
        return _PlaceSpec((None, tr, hc), index)

    return _Side(
        ins=[a for pair in zip(gs, others) for a in pair], in_specs=[blk(True), blk(False)] * len(gs),
        outs=[jax.ShapeDtypeStruct(o.shape, BF16) for o in others], out_specs=[blk(False)] * len(gs),
        scratch=[], init=_zero_refs, body=body)


def _cast_side(shards, *, n_steps, step_of):
    def body(step, ins, outs, scr):
        del step, scr
        for in_ref, out_ref in zip(ins, outs):
            out_ref[...] = in_ref[...].astype(BF16)

    def rows(s):
        assert s.shape[0] % (16 * n_steps) == 0, s.shape
        return s.shape[0] // n_steps

    return _Side(
        ins=list(shards),
        in_specs=[pl.BlockSpec((rows(s), s.shape[1]), lambda i, j, k: (step_of(i, j, k), 0)) for s in shards],
        outs=[jax.ShapeDtypeStruct((N_CHIPS,) + s.shape, BF16) for s in shards],
        out_specs=[_PlaceSpec((None, rows(s), s.shape[1]),
                              lambda i, j, k, place: (place[0], step_of(i, j, k), 0)) for s in shards],
        scratch=[], init=_zero_refs, body=body)


def _pow2_below(n_steps, most=16):
    return min(most, 1 << (n_steps.bit_length() - 1))


def _sum_chips_side(own, landed, *, n_blocks, step_of):
    n_chip, rows, hc = own.shape
    tr = rows // n_blocks
    assert tr * n_blocks == rows and tr % 16 == 0

    def body(step, ins, outs, scr):
        del step, scr
        o_ref, l1_ref, l2_ref, l3_ref = ins
        outs[0][...] = ((o_ref[...].astype(F32) + l1_ref[...].astype(F32))
                        + l2_ref[...].astype(F32)) + l3_ref[...].astype(F32)

    def block(i, j, k):
        return jnp.minimum(step_of(i, j, k), n_blocks - 1)

    def slot(q):
        return _PlaceSpec((None, tr, hc),
                          lambda i, j, k, place, q=q: ((place[0] + q) % n_chip, block(i, j, k), 0))

    return _Side(
        ins=[own, landed, landed, landed], in_specs=[slot(0), slot(1), slot(2), slot(3)],
        outs=[jax.ShapeDtypeStruct((rows, 2 * hc), F32)],
        out_specs=[_PlaceSpec((tr, hc), lambda i, j, k, place: (block(i, j, k), place[1]))],
        scratch=[], init=_zero_refs, body=body)


def _exchange_comm(pieces):
    n_w, n_m = len(pieces), len(CHIP_MASKS)

    def copies(c_ins, c_outs, ss, rs):
        x, y, c = _my_place()
        chip = 2 * x + y
        for w in range(n_w):
            for mi, (mx, my) in enumerate(CHIP_MASKS):
                px, py = _flip(x, mx), _flip(y, my)
                send = _remote(c_ins[w].at[2 * px + py], c_outs[w].at[chip], ss, rs, w * n_m + mi, (px, py, c))
                landed = c_outs[w].at[2 * px + py]
                yield send, _remote(landed, landed, ss, rs, w * n_m + mi, (px, py, c))

    def first(c_ins, c_outs, ss, rs):
        for send, _ in copies(c_ins, c_outs, ss, rs):
            send.start()

    def last(c_ins, c_outs, ss, rs):
        for send, arrival in copies(c_ins, c_outs, ss, rs):
            arrival.wait_recv()
            send.wait_send()

    return _Comm(ins=pieces, outs=[jax.ShapeDtypeStruct(p.shape, p.dtype) for p in pieces],
                 aliases={}, n_sems=n_w * n_m, first=first, last=last)


def _sum_chips(own, landed, chip_core, *, name):
    n_chip, rows, hc = own.shape
    tr, tc = _tile(rows, hc)
    per_half = hc // tc

    def body(cc_ref, o_ref, l1_ref, l2_ref, l3_ref, out_ref):
        del cc_ref
        out_ref[...] = ((o_ref[...].astype(F32) + l1_ref[...].astype(F32))
                        + l2_ref[...].astype(F32)) + l3_ref[...].astype(F32)

    def slot(k):
        return pl.BlockSpec((None, tr, tc), lambda r, q, cc, k=k: ((cc[0] + k) % n_chip, r, q))

    grid_spec = pltpu.PrefetchScalarGridSpec(
        num_scalar_prefetch=1, grid=(rows // tr, per_half),
        in_specs=[slot(0), slot(1), slot(2), slot(3)],
        out_specs=pl.BlockSpec((tr, tc), lambda r, q, cc: (r, cc[1] * per_half + q)))
    return pl.pallas_call(
        body, grid_spec=grid_spec, out_shape=jax.ShapeDtypeStruct((rows, 2 * hc), F32),
        compiler_params=_params("arbitrary", "arbitrary"), name=name,
    )(chip_core, own, landed, landed, landed)


def _join_comm(halves):
    n_w = len(halves)

    def first(c_ins, c_outs, ss, rs):
        x, y, c = _my_place()
        for w in range(n_w):
            mine = _col_half(c_outs[w], c)
            _remote(mine, mine, ss, rs, w, (x, y, 1 - c)).start()

    def last(c_ins, c_outs, ss, rs):
        x, y, c = _my_place()
        for w in range(n_w):
            theirs = _col_half(c_outs[w], 1 - c)
            _remote(theirs, theirs, ss, rs, w, (x, y, 1 - c)).wait()

    return _Comm(ins=halves, outs=[jax.ShapeDtypeStruct(h.shape, h.dtype) for h in halves],
                 aliases={w: w for w in range(n_w)}, n_sems=n_w, first=first, last=last)


def _standalone(comm, *, name):
    def body(o_ref):
        o_ref[...] = jnp.zeros_like(o_ref)

    return _call(body, grid=(1,), in_specs=[],
                 out_specs=[pl.BlockSpec((SUBLANES, LANES), lambda i: (0, 0))],
                 out_shape=[jax.ShapeDtypeStruct((SUBLANES, LANES), F32)], operands=(), name=name,
                 comm=comm)[1:]


def _pack(pieces):
    flat, spans, off = [], [], 0
    for p in pieces:
        v = p.reshape(-1).astype(F32)
        pad = (-v.shape[0]) % LANES
        if pad:
            v = jnp.concatenate([v, jnp.zeros((pad,), F32)])
        spans.append((off, p.size))
        off += v.shape[0]
        flat.append(v)
    tail = (-off) % (SUBLANES * LANES)
    if tail:
        flat.append(jnp.zeros((tail,), F32))
    return jnp.concatenate(flat).reshape(-1, LANES), spans


def _unpack(buf, span, shape):
    off, size = span
    return buf.reshape(-1)[off:off + size].reshape(shape)


def kernel(x, norm1_g, w_in, w_gate_up, b_gate, conv_w, conv_norm_g, gla_norm_g, w_out, norm2_g, w_ff1, w_ff2, norm_f_g, loss_target, m_norm1_g, m_w_in, m_w_gate_up, m_b_gate, m_conv_w, m_conv_norm_g, m_gla_norm_g, m_w_out, m_norm2_g, m_w_ff1, m_w_ff2, m_norm_f_g, v_norm1_g, v_w_in, v_w_gate_up, v_b_gate, v_conv_w, v_conv_norm_g, v_gla_norm_g, v_w_out, v_norm2_g, v_w_ff1, v_w_ff2, v_norm_f_g):
    xs = x[0]
    target = loss_target[0]
    s_len, d = xs.shape
    d_in = w_in.shape[2] * N_CHIPS
    d_main = d_in - GATE_RANK
    d_ff = w_ff1.shape[2] * N_CHIPS
    cx, cy, cc = _my_place()
    chip = 2 * cx + cy
    chip_core = jnp.stack([chip, cc]).astype(jnp.int32)
    n_ff = d_ff // N_CHIPS
    norm_f = norm_f_g.reshape(1, d)

    n_sh = d_in // N_CHIPS
    wi_buf = _cast_into(w_in[0].T, chip_core, name="cast_w_in")

    small_w, spans_w = _pack([w_gate_up[0], conv_w[0]])
    small_all = _allgather_small(small_w, reduce=False, name="gather_small_weights")
    chips_first = [small_all[2 * j] for j in range(N_CHIPS)]
    wgu_full = jnp.concatenate(
        [_unpack(b, spans_w[0], w_gate_up.shape[1:]) for b in chips_first], axis=1)
    convw_full = jnp.concatenate(
        [_unpack(b, spans_w[1], conv_w.shape[1:]) for b in chips_first], axis=0)
    wgu_pad = jnp.concatenate(
        [wgu_full, jnp.zeros((LANES - GATE_RANK, wgu_full.shape[1]), F32)], axis=0).astype(BF16)
    convw_t = convw_full.T

    u, wo_buf, w1_buf, w2_buf, wi_buf = _rms_fwd(
        xs, norm1_g, name="norm1_fwd", comm=_gather_comm([wi_buf]), place=chip_core,
        side=_cast_side([w_out[0], w_ff1[0], w_ff2[0]], n_steps=s_len // 512, step_of=lambda i, j, k: i))
    wi_t = wi_buf.reshape(d_in, d)
    wg_t = jnp.concatenate([wi_t[d_main:], jnp.zeros((LANES - GATE_RANK, d), BF16)], axis=0)
    z, wo_buf, w1_buf = _matmul(u, wi_t, tb=True, tm=2048, tn=1024, tk=d, out_dtypes=[F32], n_dim=d_main,
                                name="in_proj",
                                comm=_gather_comm([wo_buf, w1_buf], rows=[None, (0, d // 2)]))
    wo_full = wo_buf.reshape(d, d)
    (alow,) = _matmul(u, wg_t, tb=True, tm=1024, tn=LANES, tk=d, out_dtypes=[F32], name="in_proj_gate")
    y0 = _conv_fwd(z, convw_t, conv_norm_g)
    y, o, la, st, w1_cm = _gla_fwd(z, alow, wgu_pad, b_gate, gla_norm_g, y0,
                                   comm=_gather_comm([w1_buf], rows=[(d // 2, d // 2)], mid_at=0.9))
    x2, h = _matmul(y, wo_full, tm=512, tn=d, tk=d, out_dtypes=[F32, BF16], extras=(xs, norm2_g),
                    epilogue=_residual_norm_epilogue, name="out_proj")
    a, p, w2_buf = _matmul(
        h, w1_cm, tm=2048, tn=1024, tk=d, out_dtypes=[BF16, BF16], n_dim=d_ff,
        b_spec=pl.BlockSpec((None, d, 1024), lambda i, j, k: (j // 2, 0, j % 2)),
        epilogue=lambda acc: (acc, jnp.square(jnp.maximum(acc, 0.0))), name="ff1",
        comm=_gather_comm([w2_buf]))
    w2_full = w2_buf.reshape(d_ff, d)
    (x3,) = _matmul(p, w2_full, tm=1024, tn=1024, tk=2048, out_dtypes=[F32], extras=(x2,),
                    epilogue=_add_epilogue, name="ff2")
    dx3b, g_normf, loss_part = _loss_head(x3, norm_f, target)

    (da,) = _matmul(dx3b, w2_full, tb=True, tm=2048, tn=1024, tk=d, out_dtypes=[BF16], extras=(a,),
                    epilogue=lambda acc, av: (acc * (2.0 * jnp.maximum(av, 0.0)),), name="ff2_dx")
    (dh,) = _matmul(
        da, w1_cm, tb=True, tm=2048, tn=1024, tk=2048, out_dtypes=[BF16], n_dim=d,
        b_spec=pl.BlockSpec((None, 1024, 2048), lambda i, j, k: (k, j, 0)), name="ff1_dx")
    dw_k = s_len // 1024
    g_w2, dx2, g_norm2 = _matmul(
        p, dx3b, ta=True, tm=1024, tn=d, tk=1024, out_dtypes=[BF16], name="ff2_dw",
        side=_rms_bwd_side(dh, x2, norm2_g, dx3b, n_steps=(d_ff // 1024) * dw_k,
                           block_of_step=lambda i, j, k: i * dw_k + k))
    g_w2 = g_w2.reshape(N_CHIPS, n_ff, d)
    dy, t_w2 = _matmul(dx2, wo_full, tb=True, tm=1024, tn=1024, tk=d, out_dtypes=[BF16],
                       name="out_proj_dx", comm=_swap_comm([g_w2]))
    f1_j, f1_k = d_ff // n_ff, s_len // 2048
    g_w1, dz0, g_convw_t, g_convg = _matmul(
        h, da, ta=True, tm=1024, tn=n_ff, tk=2048, out_dtypes=[BF16], name="ff1_dw",
        out_shapes=[jax.ShapeDtypeStruct((N_CHIPS, d, n_ff), BF16)],
        out_specs=[pl.BlockSpec((None, 1024, n_ff), lambda i, j, k: (j, i, 0))],
        side=_conv_bwd_side(dy, z, convw_t, conv_norm_g, n_steps=(d // 1024) * f1_j * f1_k,
                            step_of=lambda i, j, k: (i * f1_j + j) * f1_k + k))
    ow_k = s_len // 512
    g_wo, p_w2, t_w1 = _matmul(
        y, dx2, ta=True, tm=1024, tn=d, tk=512, out_dtypes=[BF16], name="out_proj_dw",
        side=_add_own_half_side([g_w2], [t_w2], n_steps=(d // 1024) * ow_k,
                                step_of=lambda i, j, k: i * ow_k + k),
        comm=_swap_comm([g_w1]), place=chip_core)
    g_wo = g_wo.reshape(N_CHIPS, d // N_CHIPS, d)
    dz, dalow, g_wgu_pad, g_bg, g_gg, p_w1, l_w2 = _gla_bwd(
        dy, z, o, la, st, alow, wgu_pad, gla_norm_g, dz0, comm=_exchange_comm([p_w2]), place=chip_core,
        side=_add_own_half_side([g_w1], [t_w1], n_steps=s_len // 512, step_of=lambda i, j, k: i))
    dwi_k = s_len // 2048
    g_wi_t, m_w2, l_w1 = _matmul(
        dz, u, ta=True, tm=1024, tn=d, tk=2048, out_dtypes=[BF16], name="in_proj_dw",
        out_shapes=[jax.ShapeDtypeStruct((d_in, d), BF16)], comm=_exchange_comm([p_w1]),
        side=_sum_chips_side(p_w2, l_w2, n_blocks=_pow2_below(d_main // 1024 * dwi_k),
                             step_of=lambda i, j, k: i * dwi_k + k),
        place=chip_core)
    g_wi_t, m_w1 = _matmul(
        dalow, u, ta=True, tm=LANES, tn=d, tk=1024, out_dtypes=[BF16],
        epilogue=lambda acc: (acc[:GATE_RANK],), into=g_wi_t, name="in_proj_gate_dw",
        out_shapes=[jax.ShapeDtypeStruct((d_in, d), BF16)],
        out_specs=[pl.BlockSpec((GATE_RANK, d), lambda i, j, k: (d_main // GATE_RANK, 0))],
        side=_sum_chips_side(p_w1, l_w1, n_blocks=_pow2_below(s_len // 1024), step_of=lambda i, j, k: k),
        place=chip_core)
    t_wi, t_wo = _standalone(_swap_comm([g_wi_t, g_wo]), name="swap_w_in_w_out")
    p_wi = _add_own_half(g_wi_t[None], t_wi[None], chip_core, name="pre_reduce_w_in")
    p_wi = p_wi.reshape(N_CHIPS, n_sh, d // 2)
    p_wo = _add_own_half(g_wo, t_wo, chip_core, name="pre_reduce_w_out")
    du, l_wi, l_wo, m_w1, m_w2 = _matmul(
        dz, wi_t, tm=1024, tn=1024, tk=2048, out_dtypes=[BF16], n_dim=d, extras=(dalow, wg_t),
        epilogue=lambda acc, dal, wg: (acc + jnp.dot(dal.astype(BF16), wg, preferred_element_type=F32),),
        name="in_proj_dx", comm=[_exchange_comm([p_wi, p_wo]), _join_comm([m_w1, m_w2])])
    m_wi = _sum_chips(p_wi, l_wi, chip_core, name="reduce_w_in")
    m_wo = _sum_chips(p_wo, l_wo, chip_core, name="reduce_w_out")
    grad_x, g_norm1 = _rms_bwd(du, xs, norm1_g, dx2, name="norm1_bwd")
    m_wi, m_wo = _standalone(_join_comm([m_wi, m_wo]), name="join_w_in_w_out")
    g_big = [m_wi, m_wo, m_w1, m_w2]

    small_g, spans_g = _pack([g_norm1, g_wgu_pad[:GATE_RANK], g_bg, g_convw_t, g_convg, g_gg, g_norm2,
                              g_normf, loss_part[:, :1]])
    tot = _allgather_small(small_g, reduce=True, name="reduce_small_grads")
    t_norm1 = _unpack(tot, spans_g[0], (1, d))
    t_wgu = _unpack(tot, spans_g[1], (GATE_RANK, HEADS * DK))
    t_bg = _unpack(tot, spans_g[2], (1, HEADS * DK))
    t_convw = _unpack(tot, spans_g[3], (3, D_CONV)).T
    t_convg = _unpack(tot, spans_g[4], (1, D_CONV))
    t_gg = _unpack(tot, spans_g[5], (1, DV))
    t_norm2 = _unpack(tot, spans_g[6], (1, d))
    t_normf = _unpack(tot, spans_g[7], (1, d))
    loss = _unpack(tot, spans_g[8], ())
    n_gu = w_gate_up.shape[2]
    n_cw = conv_w.shape[1]
    t_wgu = lax.dynamic_slice(t_wgu, (0, chip * n_gu), (GATE_RANK, n_gu))
    t_convw = lax.dynamic_slice(t_convw, (chip * n_cw, 0), (n_cw, 3))

    order = ["norm1_g", "w_in", "w_gate_up", "b_gate", "conv_w", "conv_norm_g", "gla_norm_g", "w_out",
             "norm2_g", "w_ff1", "w_ff2", "norm_f_g"]
    weights = dict(norm1_g=norm1_g, w_in=w_in, w_gate_up=w_gate_up, b_gate=b_gate, conv_w=conv_w,
                   conv_norm_g=conv_norm_g, gla_norm_g=gla_norm_g, w_out=w_out, norm2_g=norm2_g,
                   w_ff1=w_ff1, w_ff2=w_ff2, norm_f_g=norm_f_g)
    moms = dict(norm1_g=m_norm1_g, w_in=m_w_in, w_gate_up=m_w_gate_up, b_gate=m_b_gate, conv_w=m_conv_w,
                conv_norm_g=m_conv_norm_g, gla_norm_g=m_gla_norm_g, w_out=m_w_out, norm2_g=m_norm2_g,
                w_ff1=m_w_ff1, w_ff2=m_w_ff2, norm_f_g=m_norm_f_g)
    vels = dict(norm1_g=v_norm1_g, w_in=v_w_in, w_gate_up=v_w_gate_up, b_gate=v_b_gate, conv_w=v_conv_w,
                conv_norm_g=v_conv_norm_g, gla_norm_g=v_gla_norm_g, w_out=v_w_out, norm2_g=v_norm2_g,
                w_ff1=v_w_ff1, w_ff2=v_w_ff2, norm_f_g=v_norm_f_g)
    grads2d = dict(norm1_g=t_norm1, w_in=g_big[0], w_gate_up=t_wgu, b_gate=t_bg, conv_w=t_convw,
                   conv_norm_g=t_convg, gla_norm_g=t_gg, w_out=g_big[1], norm2_g=t_norm2,
                   w_ff1=g_big[2], w_ff2=g_big[3], norm_f_g=t_normf)
    out_g, out_d, out_m, out_v = [], [], [], []
    for nm in order:
        w = weights[nm]
        g2 = grads2d[nm]
        if nm == "w_in":
            to2d, back = (lambda t: t[0].T), (lambda t: t.T.reshape(w.shape))
        else:
            to2d, back = (lambda t: t.reshape(g2.shape)), (lambda t: t.reshape(w.shape))
        res = _adamw(to2d(w), g2, to2d(moms[nm]), to2d(vels[nm]), name="adamw_" + nm)
        for lst, r in zip((out_g, out_d, out_m, out_v), res):
            lst.append(back(r))
    return (loss, grad_x.reshape(x.shape), *out_g, *out_d, *out_m, *out_v)
```

```python
import functools

import jax
import jax.numpy as jnp
from jax import lax
from jax.experimental import pallas as pl
from jax.experimental.pallas import tpu as pltpu

F32 = jnp.float32
BF16 = jnp.bfloat16
MESH = pl.DeviceIdType.MESH

EPS = 1e-6
CHUNK = 64
HEADS = 4
DK = 128
DV = 256
D_CONV = 1024
GROUP = 128
GATE_RANK = 16
LANES = 128
SUBLANES = 8
N_CHIPS = 4
N_DEV = 8
CHIP_MASKS = ((1, 0), (0, 1), (1, 1))

ADAM_LR = 0.001
ADAM_B1 = 0.9
ADAM_B2 = 0.999
ADAM_EPS = 1e-08
ADAM_WD = 0.01
ADAM_STEP = 10

VMEM_LIMIT = 56 * 1024 * 1024


def _params(*sem):
    return pltpu.CompilerParams(dimension_semantics=tuple(sem), vmem_limit_bytes=VMEM_LIMIT)


def _rowsum8(v):
    r, c = v.shape
    return jnp.sum(v.reshape(r // SUBLANES, SUBLANES, c), axis=0)


def _tile(rows, cols):
    for cand in (256, 128, 64, 32, 16, 8):
        if rows % cand == 0 and cand < rows <= 64 * cand:
            return cand, cols
    if rows * cols * 4 > (2 << 20) and cols % 256 == 0:
        return rows, 256
    return rows, cols


class _Comm:
    def __init__(self, ins, outs, aliases, n_sems, first, last, mid=None, mid_at=0.75):
        self.ins = list(ins)
        self.outs = list(outs)
        self.aliases = dict(aliases)
        self.n_sems = n_sems
        self.first = first
        self.mid = mid
        self.mid_at = mid_at
        self.last = last


class _PlaceSpec:
    def __init__(self, block_shape, index_map):
        self.block_shape, self.index_map = block_shape, index_map


def _call(body, *, grid, in_specs, out_specs, out_shape, operands, name, scratch_shapes=(), sem=None,
          aliases=None, comm=None, place=None):
    aliases = dict(aliases or {})
    comms = [] if comm is None else (list(comm) if isinstance(comm, (list, tuple)) else [comm])
    n_in, n_out, n_scr = len(in_specs), len(out_specs), len(scratch_shapes)
    c_ins_all = [a for cm in comms for a in cm.ins]
    c_outs_all = [o for cm in comms for o in cm.outs]
    n_ci, n_co = len(c_ins_all), len(c_outs_all)
    n_place = 0 if place is None else 1

    def adapt(spec):
        if isinstance(spec, _PlaceSpec):
            return pl.BlockSpec(spec.block_shape, spec.index_map)
        if place is None or spec.index_map is None:
            return spec
        return pl.BlockSpec(spec.block_shape, lambda *a, f=spec.index_map: f(*a[:-1]))

    def full_body(*refs):
        refs = refs[n_place:]
        ins = refs[:n_in]
        o0 = n_in + n_ci
        outs = refs[o0:o0 + n_out]
        s0 = o0 + n_out + n_co
        scr = refs[s0:s0 + n_scr]
        sems = refs[s0 + n_scr:]
        parts, i_at, o_at = [], n_in, o0 + n_out
        for q, cm in enumerate(comms):
            parts.append((cm, refs[i_at:i_at + len(cm.ins)], refs[o_at:o_at + len(cm.outs)],
                          sems[2 * q], sems[2 * q + 1]))
            i_at += len(cm.ins)
            o_at += len(cm.outs)
        step = functools.reduce(lambda acc, ig: acc * ig[1] + pl.program_id(ig[0]), enumerate(grid), 0)
        n_steps = functools.reduce(lambda acc, g: acc * g, grid, 1)
        @pl.when(step == 0)
        def _():
            for cm, c_ins, c_outs, ss, rs in parts:
                cm.first(c_ins, c_outs, ss, rs)

        def mid_step(cm):
            ms = int(cm.mid_at * n_steps)
            return ms if 0 < ms < n_steps - 1 else None

        for cm, c_ins, c_outs, ss, rs in parts:
            if cm.mid is not None and mid_step(cm) is not None:
                @pl.when(step == mid_step(cm))
                def _(cm=cm, c_ins=c_ins, c_outs=c_outs, ss=ss, rs=rs):
                    cm.mid(c_ins, c_outs, ss, rs)

        body(*ins, *outs, *scr)

        @pl.when(step == n_steps - 1)
        def _():
            for cm, c_ins, c_outs, ss, rs in parts:
                if cm.mid is not None and mid_step(cm) is None:
                    cm.mid(c_ins, c_outs, ss, rs)
                cm.last(c_ins, c_outs, ss, rs)

    any_spec = pl.BlockSpec(memory_space=pl.ANY)
    i_at, o_at, sem_shapes = n_in, n_out, []
    for cm in comms:
        for i_in, i_out in cm.aliases.items():
            aliases[i_at + i_in] = o_at + i_out
        i_at += len(cm.ins)
        o_at += len(cm.outs)
        sem_shapes += [pltpu.SemaphoreType.DMA((cm.n_sems,)), pltpu.SemaphoreType.DMA((cm.n_sems,))]
    specs = dict(grid=grid, in_specs=[adapt(s) for s in in_specs] + [any_spec] * n_ci,
                 out_specs=[adapt(s) for s in out_specs] + [any_spec] * n_co,
                 scratch_shapes=list(scratch_shapes) + sem_shapes)
    if place is not None:
        specs = dict(grid_spec=pltpu.PrefetchScalarGridSpec(num_scalar_prefetch=1, **specs))
    if comms:
        params = pltpu.CompilerParams(dimension_semantics=("arbitrary",) * len(grid),
                                      vmem_limit_bytes=VMEM_LIMIT, has_side_effects=True)
    else:
        params = _params(*(sem or ("arbitrary",) * len(grid)))
    return pl.pallas_call(
        full_body, out_shape=list(out_shape) + c_outs_all,
        input_output_aliases={k + n_place: v for k, v in aliases.items()},
        compiler_params=params, name=name, **specs,
    )(*(() if place is None else (place,)), *operands, *c_ins_all)


def _matmul(a, b, *, ta=False, tb=False, tm, tn, tk, out_dtypes, name, extras=(), epilogue=None,
            out_shapes=None, out_specs=None, b_spec=None, n_dim=None, into=None, side=None, comm=None,
            place=None):
    n_into = 0 if into is None else 1
    if ta:
        k_dim, m_dim = a.shape
    else:
        m_dim, k_dim = a.shape
    if n_dim is None:
        n_dim = b.shape[0] if tb else b.shape[1]
        assert (b.shape[1] if tb else b.shape[0]) == k_dim
    assert m_dim % tm == 0 and n_dim % tn == 0 and k_dim % tk == 0, (name, a.shape, b.shape)
    nk = k_dim // tk
    n_ex, n_out = len(extras), len(out_dtypes)
    dims = (((0 if ta else 1,), (1 if tb else 0,)), ((), ()))
    grid = (m_dim // tm, n_dim // tn, nk)
    s_ins, s_outs, s_scr = (len(side.ins), len(side.outs), len(side.scratch)) if side else (0, 0, 0)

    def body(*refs):
        a_ref, b_ref = refs[0], refs[1]
        ex_refs = refs[2:2 + n_ex]
        i0 = 2 + n_ex + n_into
        side_in = refs[i0:i0 + s_ins]
        o_refs = refs[i0 + s_ins:i0 + s_ins + n_out]
        side_out = refs[i0 + s_ins + n_out:i0 + s_ins + n_out + s_outs]
        side_scr = refs[len(refs) - s_scr:] if s_scr else ()
        if side is not None:
            step = (pl.program_id(0) * grid[1] + pl.program_id(1)) * grid[2] + pl.program_id(2)

            @pl.when(step == 0)
            def _():
                side.init(side_scr)

        def dot():
            if side is not None:
                side.body(step, side_in, side_out, side_scr)
            return lax.dot_general(a_ref[...].astype(BF16), b_ref[...].astype(BF16), dims,
                                   preferred_element_type=F32)

        def finish(acc):
            outs = epilogue(acc, *[e[...] for e in ex_refs]) if epilogue is not None else (acc,)
            for o_ref, o in zip(o_refs, outs):
                o_ref[...] = o.astype(o_ref.dtype)

        if nk == 1:
            finish(dot())
        else:
            acc_ref = refs[len(refs) - s_scr - 1]
            k = pl.program_id(2)

            @pl.when(k == 0)
            def _():
                acc_ref[...] = dot()

            @pl.when(jnp.logical_and(k > 0, k < nk - 1))
            def _():
                acc_ref[...] += dot()

            @pl.when(k == nk - 1)
            def _():
                finish(acc_ref[...] + dot())

    a_spec = (pl.BlockSpec((tk, tm), lambda i, j, k: (k, i)) if ta
              else pl.BlockSpec((tm, tk), lambda i, j, k: (i, k)))
    if b_spec is None:
        b_spec = (pl.BlockSpec((tn, tk), lambda i, j, k: (j, k)) if tb
                  else pl.BlockSpec((tk, tn), lambda i, j, k: (k, j)))
    io_spec = pl.BlockSpec((tm, tn), lambda i, j, k: (i, j))

    def extra_spec(e):
        if e.shape == (m_dim, n_dim):
            return io_spec
        if e.shape[1] == n_dim:
            return pl.BlockSpec((e.shape[0], tn), lambda i, j, k: (0, j))
        assert e.shape[0] == m_dim, (name, e.shape)
        return pl.BlockSpec((tm, e.shape[1]), lambda i, j, k: (i, 0))

    if out_shapes is None:
        out_shapes = [jax.ShapeDtypeStruct((m_dim, n_dim), dt) for dt in out_dtypes]
    if out_specs is None:
        out_specs = [io_spec] * n_out
    return _call(
        body,
        grid=grid,
        in_specs=([a_spec, b_spec] + [extra_spec(e) for e in extras]
                  + [pl.BlockSpec(memory_space=pl.ANY)] * n_into
                  + (list(side.in_specs) if side else [])),
        out_specs=list(out_specs) + (list(side.out_specs) if side else []),
        out_shape=list(out_shapes) + (list(side.outs) if side else []),
        scratch_shapes=([pltpu.VMEM((tm, tn), F32)] if nk > 1 else []) + (list(side.scratch) if side else []),
        sem=("parallel", "parallel", "arbitrary") if side is None else None,
        aliases={2 + n_ex: 0} if n_into else None,
        operands=(a, b, *extras) + ((into,) if n_into else ()) + (tuple(side.ins) if side else ()),
        name=name, comm=comm, place=place)


def _add_epilogue(acc, r):
    return (acc + r,)


def _residual_norm_epilogue(acc, x, g):
    x2 = acc + x
    r = lax.rsqrt(jnp.mean(x2 * x2, axis=-1, keepdims=True) + EPS)
    return x2, x2 * r * g


def _rms_fwd(x, g, *, name, tm=512, comm=None, side=None, place=None):
    s_len, d = x.shape
    s_ins, s_outs = (len(side.ins), len(side.outs)) if side else (0, 0)

    def body(*refs):
        x_ref, g_ref = refs[0], refs[1]
        o_ref = refs[2 + s_ins]
        xv = x_ref[...]
        r = lax.rsqrt(jnp.mean(xv * xv, axis=-1, keepdims=True) + EPS)
        o_ref[...] = (xv * r * g_ref[...]).astype(o_ref.dtype)
        if side is not None:
            scr = refs[3 + s_ins + s_outs:]

            @pl.when(pl.program_id(0) == 0)
            def _():
                side.init(scr)

            side.body(pl.program_id(0), refs[2:2 + s_ins], refs[3 + s_ins:3 + s_ins + s_outs], scr)

    row = pl.BlockSpec((tm, d), lambda i, j, k: (i, 0))
    return _call(
        body, grid=(s_len // tm, 1, 1),
        in_specs=[row, pl.BlockSpec((1, d), lambda i, j, k: (0, 0))] + (list(side.in_specs) if side else []),
        out_specs=[row] + (list(side.out_specs) if side else []),
        out_shape=[jax.ShapeDtypeStruct((s_len, d), BF16)] + (list(side.outs) if side else []),
        scratch_shapes=list(side.scratch) if side else [],
        operands=(x, g) + (tuple(side.ins) if side else ()), name=name, comm=comm, place=place)


NORM_ROWS = 16


def _zero_refs(refs):
    for r in refs:
        r[...] = jnp.zeros_like(r)


def _rms_bwd_block(dn_ref, x_ref, g_ref, res_ref, dx_ref, dg_ref, acc_ref):
    gv = g_ref[...]
    acc = acc_ref[...]
    for s in range(x_ref.shape[0] // NORM_ROWS):
        sl = slice(s * NORM_ROWS, (s + 1) * NORM_ROWS)
        xv = x_ref[sl, :]
        dnv = dn_ref[sl, :].astype(F32)
        r = lax.rsqrt(jnp.mean(xv * xv, axis=-1, keepdims=True) + EPS)
        xh = xv * r
        acc = acc + _rowsum8(dnv * xh)
        dxh = dnv * gv
        dx_ref[sl, :] = (r * (dxh - xh * jnp.mean(dxh * xh, axis=-1, keepdims=True))
                         + res_ref[sl, :].astype(F32))
    acc_ref[...] = acc
    dg_ref[...] = jnp.sum(acc, axis=0, keepdims=True)


class _Side:
    def __init__(self, ins, in_specs, outs, out_specs, scratch, init, body):
        self.ins, self.in_specs, self.outs, self.out_specs = ins, in_specs, outs, out_specs
        self.scratch = scratch
        self.init = init
        self.body = body


def _rms_bwd_side(dn, x, g, res, *, block_of_step, n_steps):
    s_len, d = x.shape
    row = pl.BlockSpec((s_len // n_steps, d), lambda i, j, k: (block_of_step(i, j, k), 0))
    vec = pl.BlockSpec((1, d), lambda i, j, k: (0, 0))
    return _Side(
        ins=[dn, x, g, res], in_specs=[row, row, vec, row],
        outs=[jax.ShapeDtypeStruct((s_len, d), F32), jax.ShapeDtypeStruct((1, d), F32)],
        out_specs=[row, vec], scratch=[pltpu.VMEM((SUBLANES, d), F32)],
        init=_zero_refs, body=lambda step, ins, outs, scr: _rms_bwd_block(*ins, *outs, *scr))


def _rms_bwd(dn, x, g, res, *, name, tm=512, comm=None):
    s_len, d = x.shape
    n = s_len // tm

    def body(*refs):
        @pl.when(pl.program_id(0) == 0)
        def _():
            _zero_refs(refs[-1:])

        _rms_bwd_block(*refs)

    row = pl.BlockSpec((tm, d), lambda i: (i, 0))
    vec = pl.BlockSpec((1, d), lambda i: (0, 0))
    return _call(
        body, grid=(n,),
        in_specs=[row, row, vec, row],
        out_specs=[row, vec],
        out_shape=[jax.ShapeDtypeStruct((s_len, d), F32), jax.ShapeDtypeStruct((1, d), F32)],
        scratch_shapes=[pltpu.VMEM((SUBLANES, d), F32)],
        operands=(dn, x, g, res), name=name, comm=comm)


def _loss_head(x3, g, target, *, tm=512):
    s_len, d = x3.shape
    n = s_len // tm

    def body(x_ref, g_ref, t_ref, dxb_ref, dg_ref, loss_ref, accg_ref, accl_ref):
        i = pl.program_id(0)
        xv = x_ref[...]
        gv = g_ref[...]
        r = lax.rsqrt(jnp.mean(xv * xv, axis=-1, keepdims=True) + EPS)
        xh = xv * r
        err = xh * gv - t_ref[...]

        @pl.when(i == 0)
        def _():
            accg_ref[...] = jnp.zeros_like(accg_ref)
            accl_ref[...] = jnp.zeros_like(accl_ref)

        accl_ref[...] += _rowsum8(err * err)
        dn = err * (1.0 / d)
        accg_ref[...] += _rowsum8(dn * xh)
        dxh = dn * gv
        dx = r * (dxh - xh * jnp.mean(dxh * xh, axis=-1, keepdims=True))
        dxb_ref[...] = dx.astype(BF16)

        @pl.when(i == n - 1)
        def _():
            dg_ref[...] = jnp.sum(accg_ref[...], axis=0, keepdims=True)
            tot = jnp.sum(jnp.sum(accl_ref[...], axis=0, keepdims=True), axis=1, keepdims=True)
            loss_ref[...] = jnp.broadcast_to(tot * (0.5 / d), (1, LANES))

    row = pl.BlockSpec((tm, d), lambda i: (i, 0))
    vec = pl.BlockSpec((1, d), lambda i: (0, 0))
    return pl.pallas_call(
        body, grid=(n,),
        in_specs=[row, vec, row],
        out_specs=[row, vec, pl.BlockSpec((1, LANES), lambda i: (0, 0))],
        out_shape=[jax.ShapeDtypeStruct((s_len, d), BF16),
                   jax.ShapeDtypeStruct((1, d), F32), jax.ShapeDtypeStruct((1, LANES), F32)],
        scratch_shapes=[pltpu.VMEM((SUBLANES, d), F32), pltpu.VMEM((SUBLANES, d), F32)],
        compiler_params=_params("arbitrary"), name="loss_head",
    )(x3, g, target)


def _shift_down(v, k, rows_before, row):
    out = pltpu.roll(v, k, axis=0)
    for j in range(k):
        out = jnp.where(row == j, rows_before[j], out)
    return out


def _shift_up(v, k, rows_after, row):
    t = v.shape[0]
    out = pltpu.roll(v, t - k, axis=0)
    for j in range(k):
        out = jnp.where(row == t - k + j, rows_after[j], out)
    return out


def _conv_fwd(z, w_t, gain, *, ts=512):
    s_len = z.shape[0]
    n_grp = D_CONV // GROUP

    def body(cb_ref, cc_ref, ch_ref, w_ref, g_ref, y_ref, carry_ref):
        i = pl.program_id(0)

        @pl.when(i == 0)
        def _():
            carry_ref[...] = jnp.zeros_like(carry_ref)

        row = lax.broadcasted_iota(jnp.int32, (ts, GROUP), 0)
        for g in range(n_grp):
            sl = slice(g * GROUP, (g + 1) * GROUP)
            uu = cc_ref[:, sl] * ch_ref[:, sl]
            p2 = carry_ref[6:7, sl]
            p1 = carry_ref[7:8, sl]
            u1 = _shift_down(uu, 1, [p1], row)
            u2 = _shift_down(uu, 2, [p2, p1], row)
            conv = w_ref[0:1, sl] * u2 + w_ref[1:2, sl] * u1 + w_ref[2:3, sl] * uu
            y = cb_ref[:, sl] * conv
            carry_ref[:, sl] = uu[ts - SUBLANES:ts, :]
            rg = lax.rsqrt(jnp.mean(y * y, axis=-1, keepdims=True) + EPS)
            y_ref[:, sl] = (y * rg * g_ref[:, sl]).astype(BF16)

    def col(j):
        return pl.BlockSpec((ts, D_CONV), lambda i, j=j: (i, j))

    small = lambda r: pl.BlockSpec((r, D_CONV), lambda i: (0, 0))
    return pl.pallas_call(
        body, grid=(s_len // ts,),
        in_specs=[col(0), col(1), col(2), small(3), small(1)],
        out_specs=col(0),
        out_shape=jax.ShapeDtypeStruct((s_len, 2 * D_CONV), BF16),
        scratch_shapes=[pltpu.VMEM((SUBLANES, D_CONV), F32)],
        compiler_params=_params("arbitrary"), name="conv_fwd",
    )(z, z, z, w_t, gain)


def _conv_bwd_side(dy, z, w_t, gain, *, n_steps, step_of):
    s_len = z.shape[0]
    n = n_steps
    ts = s_len // n
    n_grp = D_CONV // GROUP
    halo_blocks = ts // SUBLANES

    def body(step, ins, outs, scr):
        dy_ref, cb_ref, cc_ref, ch_ref, hcc_ref, hch_ref, w_ref, g_ref = ins
        dz_ref, dw_ref, dg_ref = outs
        carry_ref, accw_ref, accg_ref = scr
        first_tile = (n - 1 - step) == 0
        row = lax.broadcasted_iota(jnp.int32, (ts, GROUP), 0)
        keep = jnp.where(first_tile, 0.0, 1.0)
        for g in range(n_grp):
            sl = slice(g * GROUP, (g + 1) * GROUP)
            cc = cc_ref[:, sl]
            ch = ch_ref[:, sl]
            cb = cb_ref[:, sl]
            uu = cc * ch
            p2 = hcc_ref[6:7, sl] * hch_ref[6:7, sl] * keep
            p1 = hcc_ref[7:8, sl] * hch_ref[7:8, sl] * keep
            u1 = _shift_down(uu, 1, [p1], row)
            u2 = _shift_down(uu, 2, [p2, p1], row)
            w0, w1, w2 = w_ref[0:1, sl], w_ref[1:2, sl], w_ref[2:3, sl]
            conv = w0 * u2 + w1 * u1 + w2 * uu
            y = cb * conv
            rg = lax.rsqrt(jnp.mean(y * y, axis=-1, keepdims=True) + EPS)
            yh = y * rg
            dyv = dy_ref[:, sl].astype(F32)
            accg_ref[:, sl] += _rowsum8(dyv * yh)
            dyn = dyv * g_ref[:, sl]
            dpre = rg * (dyn - yh * jnp.mean(dyn * yh, axis=-1, keepdims=True))
            dz_ref[:, sl] = (dpre * conv).astype(BF16)
            dconv = dpre * cb
            accw_ref[0:8, sl] += _rowsum8(dconv * u2)
            accw_ref[8:16, sl] += _rowsum8(dconv * u1)
            accw_ref[16:24, sl] += _rowsum8(dconv * uu)
            n0 = carry_ref[0:1, sl]
            n1 = carry_ref[1:2, sl]
            d1 = _shift_up(dconv, 1, [n0], row)
            d2 = _shift_up(dconv, 2, [n0, n1], row)
            duu = w2 * dconv + w1 * d1 + w0 * d2
            carry_ref[:, sl] = dconv[0:SUBLANES, :]
            dz_ref[:, D_CONV + g * GROUP:D_CONV + (g + 1) * GROUP] = (duu * ch).astype(BF16)
            dz_ref[:, 2 * D_CONV + g * GROUP:2 * D_CONV + (g + 1) * GROUP] = (duu * cc).astype(BF16)

        for k in range(3):
            dw_ref[k:k + 1, :] = jnp.sum(accw_ref[8 * k:8 * k + 8, :], axis=0, keepdims=True)
        dg_ref[...] = jnp.sum(accg_ref[...], axis=0, keepdims=True)

    def tile(i, j, k):
        return n - 1 - step_of(i, j, k)

    def col(c):
        return pl.BlockSpec((ts, D_CONV), lambda i, j, k, c=c: (tile(i, j, k), c))

    def halo(c):
        return pl.BlockSpec((SUBLANES, D_CONV),
                            lambda i, j, k, c=c: (jnp.maximum(tile(i, j, k) * halo_blocks - 1, 0), c))

    small = lambda r: pl.BlockSpec((r, D_CONV), lambda i, j, k: (0, 0))
    return _Side(
        ins=[dy, z, z, z, z, z, w_t, gain],
        in_specs=[col(0), col(0), col(1), col(2), halo(1), halo(2), small(3), small(1)],
        outs=[jax.ShapeDtypeStruct((s_len, 6 * D_CONV), BF16),
              jax.ShapeDtypeStruct((3, D_CONV), F32), jax.ShapeDtypeStruct((1, D_CONV), F32)],
        out_specs=[pl.BlockSpec((ts, 3 * D_CONV), lambda i, j, k: (tile(i, j, k), 0)), small(3), small(1)],
        scratch=[pltpu.VMEM((SUBLANES, D_CONV), F32), pltpu.VMEM((24, D_CONV), F32),
                 pltpu.VMEM((SUBLANES, D_CONV), F32)],
        init=_zero_refs, body=body)


def _split3(v):
    hi = v.astype(BF16)
    r1 = v - hi.astype(F32)
    mid = r1.astype(BF16)
    lo = (r1 - mid.astype(F32)).astype(BF16)
    return jnp.concatenate([hi, mid, lo], axis=1)


def _tri_sum(tri, v):
    w = v.shape[1]
    dd = jnp.dot(tri, _split3(v), preferred_element_type=F32)
    return dd[:, :w] + dd[:, w:2 * w] + dd[:, 2 * w:]


def _chunk_masks(ts):
    r = jnp.arange(ts)
    same = (r[:, None] // CHUNK) == (r[None, :] // CHUNK)
    later = jnp.logical_and(same, r[None, :] > r[:, None]).astype(BF16)
    earlier = jnp.logical_and(same, r[None, :] < r[:, None]).astype(BF16)
    chunk_of_row = jnp.arange(ts // CHUNK * SUBLANES)[:, None] // SUBLANES
    member = (chunk_of_row == (r[None, :] // CHUNK)).astype(BF16)
    return later, earlier, member


def _sigmoid(v):
    return 0.5 * jnp.tanh(0.5 * v) + 0.5


def _gla_fwd(z, alow, wgu, bg, gg, y_in, *, ts=512, comm=None):
    s_len = z.shape[0]
    nch = ts // CHUNK
    scale = DK ** -0.5

    tri_u, _, ind8 = _chunk_masks(ts)

    def body(q_ref, k_ref, v_ref, og_ref, al_ref, wgu_ref, bg_ref, gg_ref, tu_ref, ind_ref, yin_ref,
             y_ref, o_ref, la_ref, st_ref, state_ref, kd_ref, qs_ref, dec_ref):
        del yin_ref
        i = pl.program_id(0)

        @pl.when(i == 0)
        def _():
            state_ref[...] = jnp.zeros_like(state_ref)

        pre = jnp.dot(al_ref[...].astype(BF16), wgu_ref[...], preferred_element_type=F32) + bg_ref[...]
        la = (jnp.minimum(pre, 0.0) - jnp.log(1.0 + jnp.exp(-jnp.abs(pre)))) * (1.0 / 16.0)
        la_ref[...] = la
        kd_ref[...] = (k_ref[...] * jnp.exp(_tri_sum(tu_ref[...], la))).astype(BF16)
        qs_ref[...] = (q_ref[...] * scale).astype(BF16)
        dec_ref[...] = jnp.exp(_tri_sum(ind_ref[...], la))

        def chunk(cl, carry):
            rows = pl.ds(pl.multiple_of(cl * CHUNK, CHUNK), CHUNK)
            dec = dec_ref[pl.ds(pl.multiple_of(cl * SUBLANES, SUBLANES), 1), :]
            for h in range(HEADS):
                ks = slice(h * DK, (h + 1) * DK)
                vs = slice(h * DV, (h + 1) * DV)
                kv_t = lax.dot_general(v_ref[rows, vs].astype(BF16), kd_ref[rows, ks],
                                       (((0,), (0,)), ((), ())), preferred_element_type=F32)
                st = state_ref[h] * dec[:, ks] + kv_t
                state_ref[h] = st
                st_ref[cl, h] = st
                o_ref[rows, vs] = lax.dot_general(qs_ref[rows, ks], st.astype(BF16),
                                                  (((1,), (1,)), ((), ())), preferred_element_type=F32)
            return carry

        lax.fori_loop(0, nch, chunk, 0, unroll=2)

        ggv = gg_ref[...]
        for h in range(HEADS):
            vs = slice(h * DV, (h + 1) * DV)
            o_h = o_ref[:, vs]
            og_h = og_ref[:, vs]
            ro = lax.rsqrt(jnp.mean(o_h * o_h, axis=-1, keepdims=True) + EPS)
            y_ref[:, vs] = (o_h * ro * ggv * (og_h * _sigmoid(og_h))).astype(BF16)

    def zcol(width, j):
        return pl.BlockSpec((ts, width), lambda i, j=j: (i, j))

    full = lambda shape: pl.BlockSpec(shape, lambda i: tuple(0 for _ in shape))
    return _call(
        body, grid=(s_len // ts,),
        in_specs=[zcol(512, 6), zcol(512, 7), zcol(1024, 4), zcol(1024, 5), zcol(LANES, 0),
                  full((LANES, 512)), full((1, 512)), full((1, DV)), full(tri_u.shape), full(ind8.shape),
                  pl.BlockSpec(memory_space=pl.ANY)],
        out_specs=[zcol(1024, 1), zcol(1024, 0), zcol(512, 0),
                   pl.BlockSpec((nch, HEADS, DV, DK), lambda i: (i, 0, 0, 0))],
        out_shape=[jax.ShapeDtypeStruct((s_len, 2048), BF16), jax.ShapeDtypeStruct((s_len, 1024), F32),
                   jax.ShapeDtypeStruct((s_len, 512), F32),
                   jax.ShapeDtypeStruct((s_len // CHUNK, HEADS, DV, DK), F32)],
        scratch_shapes=[pltpu.VMEM((HEADS, DV, DK), F32), pltpu.VMEM((ts, 512), BF16),
                        pltpu.VMEM((ts, 512), BF16), pltpu.VMEM((nch * SUBLANES, 512), F32)],
        aliases={10: 0},
        operands=(z, z, z, z, alow, wgu, bg, gg, tri_u, ind8, y_in), name="gla_fwd", comm=comm)


def _gla_bwd(dy, z, o, la, st, alow, wgu, gg, dz_in, *, ts=512, comm=None, side=None, place=None):
    s_len = z.shape[0]
    n = s_len // ts
    nch = ts // CHUNK
    scale = DK ** -0.5

    tri_u, tri_l, ind8 = _chunk_masks(ts)

    def body(dy_ref, q_ref, k_ref, v_ref, og_ref, o_ref, la_ref, st_ref, stp_ref, al_ref, wgu_ref,
             gg_ref, tu_ref, tl_ref, ind_ref, dzin_ref, dz_ref, dal_ref, dwgu_ref, dbg_ref, dgg_ref,
             gt_ref, decn_ref, accw_ref, accb_ref, accg_ref, dla_ref,
             e_ref, kd_ref, kdb_ref, qs_ref, do_ref, dkd_ref, dec_ref, dbe_ref):
        del dzin_ref
        i = pl.program_id(0)
        first_tile = (n - 1 - i) == 0

        @pl.when(i == 0)
        def _():
            gt_ref[...] = jnp.zeros_like(gt_ref)
            decn_ref[...] = jnp.ones_like(decn_ref)
            accw_ref[...] = jnp.zeros_like(accw_ref)
            accb_ref[...] = jnp.zeros_like(accb_ref)
            accg_ref[...] = jnp.zeros_like(accg_ref)

        la = la_ref[...]
        e_dec = jnp.exp(_tri_sum(tu_ref[...], la))
        e_ref[...] = e_dec
        kd = k_ref[...] * e_dec
        kd_ref[...] = kd
        kdb_ref[...] = kd.astype(BF16)
        qs_ref[...] = (q_ref[...] * scale).astype(BF16)
        dec_ref[...] = jnp.exp(_tri_sum(ind_ref[...], la))
        ggv = gg_ref[...]
        for h in range(HEADS):
            vs = slice(h * DV, (h + 1) * DV)
            o_h = o_ref[:, vs]
            og_h = og_ref[:, vs]
            dy_h = dy_ref[:, vs].astype(F32)
            ro = lax.rsqrt(jnp.mean(o_h * o_h, axis=-1, keepdims=True) + EPS)
            oh = o_h * ro
            sig = _sigmoid(og_h)
            sil = og_h * sig
            accg_ref[...] += _rowsum8(dy_h * oh * sil)
            dz_ref[:, 2048 + h * DV:2048 + (h + 1) * DV] = (
                dy_h * oh * ggv * sig * (1.0 + og_h * (1.0 - sig))).astype(BF16)
            don = dy_h * ggv * sil
            do_ref[:, vs] = (ro * (don - oh * jnp.mean(don * oh, axis=-1, keepdims=True))).astype(BF16)
        keep = jnp.where(first_tile, 0.0, 1.0)

        def chunk(jrev, decn):
            cl = nch - 1 - jrev
            rows = pl.ds(pl.multiple_of(cl * CHUNK, CHUNK), CHUNK)
            one_row = pl.ds(pl.multiple_of(cl * SUBLANES, SUBLANES), 1)
            dec = dec_ref[one_row, :]
            has_prev = jnp.where(cl > 0, 1.0, 0.0)
            prev_idx = jnp.maximum(cl - 1, 0)
            for h in range(HEADS):
                ks = slice(h * DK, (h + 1) * DK)
                vs = slice(h * DV, (h + 1) * DV)
                dob = do_ref[rows, vs]
                s_c = st_ref[cl, h]
                dqs = jnp.dot(dob, s_c.astype(BF16), preferred_element_type=F32)
                dz_ref[rows, ks] = (dqs * scale).astype(BF16)
                gt = gt_ref[h] * decn[:, ks] + lax.dot_general(
                    dob, qs_ref[rows, ks], (((0,), (0,)), ((), ())), preferred_element_type=F32)
                gt_ref[h] = gt
                gb = gt.astype(BF16)
                dkd_ref[rows, ks] = jnp.dot(v_ref[rows, vs].astype(BF16), gb, preferred_element_type=F32)
                dz_ref[rows, 1024 + h * DV:1024 + (h + 1) * DV] = lax.dot_general(
                    kdb_ref[rows, ks], gb, (((1,), (1,)), ((), ())),
                    preferred_element_type=F32).astype(BF16)
                s_prev = has_prev * st_ref[prev_idx, h] + (1.0 - has_prev) * keep * stp_ref[0, h]
                dbe_ref[one_row, ks] = jnp.sum(gt * s_prev, axis=0, keepdims=True) * dec[:, ks]
            return dec

        decn_ref[0:1, :] = lax.fori_loop(0, nch, chunk, decn_ref[0:1, :], unroll=2)

        dkd = dkd_ref[...]
        dz_ref[:, 512:1024] = (dkd * e_ref[...]).astype(BF16)
        dla_ref[...] = _tri_sum(tl_ref[...], dkd * kd_ref[...])
        for c in range(nch):
            dla_ref[c * CHUNK:(c + 1) * CHUNK, :] += dbe_ref[c * SUBLANES:c * SUBLANES + 1, :]
        dpre = dla_ref[...] * (1.0 / 16.0) * (1.0 - jnp.exp(16.0 * la))
        accb_ref[...] += _rowsum8(dpre)
        dpb = dpre.astype(BF16)
        accw_ref[...] += lax.dot_general(al_ref[...].astype(BF16), dpb, (((0,), (0,)), ((), ())),
                                         preferred_element_type=F32)
        dal_ref[...] = lax.dot_general(dpb, wgu_ref[...], (((1,), (1,)), ((), ())),
                                       preferred_element_type=F32)

        @pl.when(i == n - 1)
        def _():
            dwgu_ref[...] = accw_ref[...]
            dbg_ref[...] = jnp.sum(accb_ref[...], axis=0, keepdims=True)
            dgg_ref[...] = jnp.sum(accg_ref[...], axis=0, keepdims=True)

    def zcol(width, j):
        return pl.BlockSpec((ts, width), lambda i, j=j: (n - 1 - i, j))

    full = lambda shape: pl.BlockSpec(shape, lambda i: tuple(0 for _ in shape))
    n_in, n_out, n_scr = 16, 5, 14
    s_ins, s_outs = (len(side.ins), len(side.outs)) if side else (0, 0)

    def body_with_side(*refs):
        outs_at, scr_at = n_in + s_ins, n_in + s_ins + n_out + s_outs
        if side is not None:
            side_scr = refs[scr_at + n_scr:]

            @pl.when(pl.program_id(0) == 0)
            def _():
                side.init(side_scr)

            side.body(pl.program_id(0), refs[n_in:outs_at], refs[outs_at + n_out:scr_at], side_scr)
        body(*refs[:n_in], *refs[outs_at:outs_at + n_out], *refs[scr_at:scr_at + n_scr])

    def one_axis(spec):
        return type(spec)(spec.block_shape, lambda i, *p, f=spec.index_map: f(i, 0, 0, *p))

    return _call(
        body_with_side, grid=(n,),
        in_specs=[zcol(1024, 1), zcol(512, 6), zcol(512, 7), zcol(1024, 4), zcol(1024, 5),
                  zcol(1024, 0), zcol(512, 0),
                  pl.BlockSpec((nch, HEADS, DV, DK), lambda i: (n - 1 - i, 0, 0, 0)),
                  pl.BlockSpec((1, HEADS, DV, DK),
                               lambda i: (jnp.maximum((n - 1 - i) * nch - 1, 0), 0, 0, 0)),
                  zcol(LANES, 0), full((LANES, 512)), full((1, DV)),
                  full(tri_u.shape), full(tri_l.shape), full(ind8.shape),
                  pl.BlockSpec(memory_space=pl.ANY)] + ([one_axis(s) for s in side.in_specs] if side else []),
        out_specs=([zcol(3072, 1), zcol(LANES, 0), full((LANES, 512)), full((1, 512)), full((1, DV))]
                   + ([one_axis(s) for s in side.out_specs] if side else [])),
        out_shape=[jax.ShapeDtypeStruct((s_len, 6144), BF16), jax.ShapeDtypeStruct((s_len, LANES), F32),
                   jax.ShapeDtypeStruct((LANES, 512), F32), jax.ShapeDtypeStruct((1, 512), F32),
                   jax.ShapeDtypeStruct((1, DV), F32)] + (list(side.outs) if side else []),
        scratch_shapes=[pltpu.VMEM((HEADS, DV, DK), F32), pltpu.VMEM((SUBLANES, 512), F32),
                        pltpu.VMEM((LANES, 512), F32), pltpu.VMEM((SUBLANES, 512), F32),
                        pltpu.VMEM((SUBLANES, DV), F32), pltpu.VMEM((ts, 512), F32),
                        pltpu.VMEM((ts, 512), F32), pltpu.VMEM((ts, 512), F32), pltpu.VMEM((ts, 512), BF16),
                        pltpu.VMEM((ts, 512), BF16), pltpu.VMEM((ts, 1024), BF16),
                        pltpu.VMEM((ts, 512), F32), pltpu.VMEM((nch * SUBLANES, 512), F32),
                        pltpu.VMEM((nch * SUBLANES, 512), F32)] + (list(side.scratch) if side else []),
        aliases={15: 0},
        operands=((dy, z, z, z, z, o, la, st, st, alow, wgu, gg, tri_u, tri_l, ind8, dz_in)
                  + (tuple(side.ins) if side else ())),
        name="gla_bwd", comm=comm, place=place)


def _adamw(w, g, m, v, *, name):
    rows, cols = w.shape
    tr, tc = _tile(rows, cols)

    def body(w_ref, g_ref, m_ref, v_ref, go_ref, d_ref, nm_ref, nv_ref):
        gv = g_ref[...]
        go_ref[...] = gv
        m2 = ADAM_B1 * m_ref[...] + (1.0 - ADAM_B1) * gv
        v2 = ADAM_B2 * v_ref[...] + (1.0 - ADAM_B2) * jnp.square(gv)
        m_hat = m2 / (1.0 - ADAM_B1 ** ADAM_STEP)
        v_hat = v2 / (1.0 - ADAM_B2 ** ADAM_STEP)
        d_ref[...] = -ADAM_LR * (m_hat / (jnp.sqrt(v_hat) + ADAM_EPS) + ADAM_WD * w_ref[...])
        nm_ref[...] = m2
        nv_ref[...] = v2

    blk = pl.BlockSpec((tr, tc), lambda i, j: (i, j))
    shp = jax.ShapeDtypeStruct((rows, cols), F32)
    return pl.pallas_call(
        body, grid=(rows // tr, cols // tc), in_specs=[blk] * 4, out_specs=[blk] * 4, out_shape=[shp] * 4,
        compiler_params=_params("parallel", "parallel"), name=name,
    )(w, g, m, v)


def _my_place():
    return lax.axis_index("x"), lax.axis_index("y"), lax.axis_index("c")


def _flip(v, bit):
    return 1 - v if bit else v


def _allgather_small(buf, *, reduce, name):
    rows = buf.shape[0]

    def body(in_ref, out_ref, gat_ref, send_sems, recv_sems):
        x, y, c = _my_place()
        me = 4 * x + 2 * y + c
        gat_ref[me] = in_ref[...]
        copies = []
        for m in range(1, N_DEV):
            peer = (_flip(x, m & 4), _flip(y, m & 2), _flip(c, m & 1))
            cp = pltpu.make_async_remote_copy(
                src_ref=in_ref, dst_ref=gat_ref.at[me],
                send_sem=send_sems.at[m - 1], recv_sem=recv_sems.at[m - 1],
                device_id=peer, device_id_type=MESH)
            cp.start()
            copies.append(cp)
        for m in range(1, N_DEV):
            px, py, pc = _flip(x, m & 4), _flip(y, m & 2), _flip(c, m & 1)
            src_slot = gat_ref.at[4 * px + 2 * py + pc]
            pltpu.make_async_remote_copy(
                src_ref=src_slot, dst_ref=src_slot,
                send_sem=send_sems.at[m - 1], recv_sem=recv_sems.at[m - 1],
                device_id=(px, py, pc), device_id_type=MESH).wait_recv()
        for cp in copies:
            cp.wait_send()
        if reduce:
            tot = gat_ref[0]
            for d in range(1, N_DEV):
                tot = tot + gat_ref[d]
            out_ref[...] = tot
        else:
            out_ref[...] = gat_ref[...]

    out_shape = (rows, LANES) if reduce else (N_DEV, rows, LANES)
    return pl.pallas_call(
        body,
        in_specs=[pl.BlockSpec(memory_space=pltpu.VMEM)],
        out_specs=pl.BlockSpec(memory_space=pltpu.VMEM),
        out_shape=jax.ShapeDtypeStruct(out_shape, F32),
        scratch_shapes=[pltpu.VMEM((N_DEV, rows, LANES), F32),
                        pltpu.SemaphoreType.DMA((N_DEV - 1,)), pltpu.SemaphoreType.DMA((N_DEV - 1,))],
        compiler_params=pltpu.CompilerParams(has_side_effects=True),
        name=name,
    )(buf)


def _cast_into(shard, chip_core, *, name):
    rows, cols = shard.shape
    tr, tc = _tile(rows, cols)

    def body(cc_ref, s_ref, o_ref):
        del cc_ref
        o_ref[...] = s_ref[...].astype(BF16)

    grid_spec = pltpu.PrefetchScalarGridSpec(
        num_scalar_prefetch=1, grid=(rows // tr, cols // tc),
        in_specs=[pl.BlockSpec((tr, tc), lambda r, q, cc: (r, q))],
        out_specs=pl.BlockSpec((None, tr, tc), lambda r, q, cc: (cc[0], r, q)))
    return pl.pallas_call(
        body, grid_spec=grid_spec, out_shape=jax.ShapeDtypeStruct((N_CHIPS, rows, cols), BF16),
        compiler_params=_params("arbitrary", "arbitrary"), name=name,
    )(chip_core, shard)


def _remote(src, dst, send_sems, recv_sems, k, device):
    return pltpu.make_async_remote_copy(src_ref=src, dst_ref=dst, send_sem=send_sems.at[k],
                                        recv_sem=recv_sems.at[k], device_id=device, device_id_type=MESH)


def _col_half(ref, h, *lead, rows=None):
    hc = ref.shape[-1] // 2
    mid = (slice(None),) * (len(ref.shape) - 2 - len(lead))
    row_sel = slice(None) if rows is None else pl.ds(rows[0], rows[1])
    return ref.at[tuple(lead) + mid + (row_sel, pl.ds(h * hc, hc))]


def _gather_comm(bufs, rows=None, mid_at=0.75):
    n_w, n_m = len(bufs), len(CHIP_MASKS)
    rows = rows or [None] * n_w

    def first(c_ins, c_outs, ss, rs):
        x, y, c = _my_place()
        chip = 2 * x + y
        for w in range(n_w):
            mine = _col_half(c_outs[w], c, chip, rows=rows[w])
            for mi, (mx, my) in enumerate(CHIP_MASKS):
                _remote(mine, mine, ss, rs, w * n_m + mi, (_flip(x, mx), _flip(y, my), c)).start()

    def mid(c_ins, c_outs, ss, rs):
        x, y, c = _my_place()
        for w in range(n_w):
            for mi, (mx, my) in enumerate(CHIP_MASKS):
                k = w * n_m + mi
                px, py = _flip(x, mx), _flip(y, my)
                landed = _col_half(c_outs[w], c, 2 * px + py, rows=rows[w])
                _remote(landed, landed, ss, rs, k, (px, py, c)).wait_recv()
                _remote(landed, landed, ss, rs, n_w * n_m + k, (x, y, 1 - c)).start()

    def last(c_ins, c_outs, ss, rs):
        x, y, c = _my_place()
        chip = 2 * x + y
        for w in range(n_w):
            mine = _col_half(c_outs[w], c, chip, rows=rows[w])
            for mi, (mx, my) in enumerate(CHIP_MASKS):
                k = w * n_m + mi
                px, py = _flip(x, mx), _flip(y, my)
                theirs = _col_half(c_outs[w], 1 - c, 2 * px + py, rows=rows[w])
                _remote(theirs, theirs, ss, rs, n_w * n_m + k, (x, y, 1 - c)).wait_recv()
                _remote(mine, mine, ss, rs, k, (px, py, c)).wait_send()
                _remote(mine, mine, ss, rs, n_w * n_m + k, (x, y, 1 - c)).wait_send()

    return _Comm(ins=bufs, outs=[jax.ShapeDtypeStruct(b.shape, b.dtype) for b in bufs],
                 aliases={w: w for w in range(n_w)}, n_sems=2 * n_w * n_m, first=first, mid=mid, last=last,
                 mid_at=mid_at)


def _swap_comm(grads):
    n_w = len(grads)

    def copy(c_ins, c_outs, ss, rs, w):
        x, y, c = _my_place()
        return _remote(_col_half(c_ins[w], 1 - c), c_outs[w], ss, rs, w, (x, y, 1 - c))

    def first(c_ins, c_outs, ss, rs):
        for w in range(n_w):
            copy(c_ins, c_outs, ss, rs, w).start()

    def last(c_ins, c_outs, ss, rs):
        for w in range(n_w):
            copy(c_ins, c_outs, ss, rs, w).wait()

    return _Comm(ins=grads,
                 outs=[jax.ShapeDtypeStruct(g.shape[:-1] + (g.shape[-1] // 2,), g.dtype) for g in grads],
                 aliases={}, n_sems=n_w, first=first, last=last)


def _add_own_half(g, other, chip_core, *, name):
    n_chip, rows, hc = other.shape
    tr, tc = _tile(rows, hc)
    per_half = hc // tc

    def body(cc_ref, g_ref, o_ref, out_ref):
        del cc_ref
        out_ref[...] = (g_ref[...].astype(F32) + o_ref[...].astype(F32)).astype(BF16)

    grid_spec = pltpu.PrefetchScalarGridSpec(
        num_scalar_prefetch=1, grid=(n_chip, rows // tr, per_half),
        in_specs=[pl.BlockSpec((None, tr, tc), lambda j, r, q, cc: (j, r, cc[1] * per_half + q)),
                  pl.BlockSpec((None, tr, tc), lambda j, r, q, cc: (j, r, q))],
        out_specs=pl.BlockSpec((None, tr, tc), lambda j, r, q, cc: (j, r, q)))
    return pl.pallas_call(
        body, grid_spec=grid_spec, out_shape=jax.ShapeDtypeStruct((n_chip, rows, hc), BF16),
        compiler_params=_params("parallel", "parallel", "parallel"), name=name,
    )(chip_core, g, other)


def _add_own_half_side(gs, others, *, n_steps, step_of):
    n_chip, rows, hc = others[0].shape
    per_chip = n_steps // n_chip
    tr = rows // per_chip
    assert all(o.shape == others[0].shape for o in others) and tr * per_chip == rows and tr % 16 == 0

    def body(step, ins, outs, scr):
        del step, scr
        for q, out_ref in enumerate(outs):
            out_ref[...] = (ins[2 * q][...].astype(F32) + ins[2 * q + 1][...].astype(F32)).astype(BF16)

    def blk(own_half):
        def index(i, j, k, place):
            s = step_of(i, j, k)
            return (s // per_chip, s % per_chip, place[1] if own_half else 0)
        return _PlaceSpec((None, tr, hc), index)

    return _Side(
        ins=[a for pair in zip(gs, others) for a in pair], in_specs=[blk(True), blk(False)] * len(gs),
        outs=[jax.ShapeDtypeStruct(o.shape, BF16) for o in others], out_specs=[blk(False)] * len(gs),
        scratch=[], init=_zero_refs, body=body)


def _cast_side(shards, *, n_steps, step_of):
    def body(step, ins, outs, scr):
        del step, scr
        for in_ref, out_ref in zip(ins, outs):
            out_ref[...] = in_ref[...].astype(BF16)

    def rows(s):
        assert s.shape[0] % (16 * n_steps) == 0, s.shape
        return s.shape[0] // n_steps

    return _Side(
        ins=list(shards),
        in_specs=[pl.BlockSpec((rows(s), s.shape[1]), lambda i, j, k: (step_of(i, j, k), 0)) for s in shards],
        outs=[jax.ShapeDtypeStruct((N_CHIPS,) + s.shape, BF16) for s in shards],
        out_specs=[_PlaceSpec((None, rows(s), s.shape[1]),
                              lambda i, j, k, place: (place[0], step_of(i, j, k), 0)) for s in shards],
        scratch=[], init=_zero_refs, body=body)


def _pow2_below(n_steps, most=16):
    return min(most, 1 << (n_steps.bit_length() - 1))


def _sum_chips_side(own, landed, *, n_blocks, step_of):
    n_chip, rows, hc = own.shape
    tr = rows // n_blocks
    assert tr * n_blocks == rows and tr % 16 == 0

    def body(step, ins, outs, scr):
        del step, scr
        o_ref, l1_ref, l2_ref, l3_ref = ins
        outs[0][...] = ((o_ref[...].astype(F32) + l1_ref[...].astype(F32))
                        + l2_ref[...].astype(F32)) + l3_ref[...].astype(F32)

    def block(i, j, k):
        return jnp.minimum(step_of(i, j, k), n_blocks - 1)

    def slot(q):
        return _PlaceSpec((None, tr, hc),
                          lambda i, j, k, place, q=q: ((place[0] + q) % n_chip, block(i, j, k), 0))

    return _Side(
        ins=[own, landed, landed, landed], in_specs=[slot(0), slot(1), slot(2), slot(3)],
        outs=[jax.ShapeDtypeStruct((rows, 2 * hc), F32)],
        out_specs=[_PlaceSpec((tr, hc), lambda i, j, k, place: (block(i, j, k), place[1]))],
        scratch=[], init=_zero_refs, body=body)


def _exchange_comm(pieces):
    n_w, n_m = len(pieces), len(CHIP_MASKS)

    def copies(c_ins, c_outs, ss, rs):
        x, y, c = _my_place()
        chip = 2 * x + y
        for w in range(n_w):
            for mi, (mx, my) in enumerate(CHIP_MASKS):
                px, py = _flip(x, mx), _flip(y, my)
                send = _remote(c_ins[w].at[2 * px + py], c_outs[w].at[chip], ss, rs, w * n_m + mi, (px, py, c))
                landed = c_outs[w].at[2 * px + py]
                yield send, _remote(landed, landed, ss, rs, w * n_m + mi, (px, py, c))

    def first(c_ins, c_outs, ss, rs):
        for send, _ in copies(c_ins, c_outs, ss, rs):
            send.start()

    def last(c_ins, c_outs, ss, rs):
        for send, arrival in copies(c_ins, c_outs, ss, rs):
            arrival.wait_recv()
            send.wait_send()

    return _Comm(ins=pieces, outs=[jax.ShapeDtypeStruct(p.shape, p.dtype) for p in pieces],
                 aliases={}, n_sems=n_w * n_m, first=first, last=last)


def _sum_chips(own, landed, chip_core, *, name):
    n_chip, rows, hc = own.shape
    tr, tc = _tile(rows, hc)
    per_half = hc // tc

    def body(cc_ref, o_ref, l1_ref, l2_ref, l3_ref, out_ref):
        del cc_ref
        out_ref[...] = ((o_ref[...].astype(F32) + l1_ref[...].astype(F32))
                        + l2_ref[...].astype(F32)) + l3_ref[...].astype(F32)

    def slot(k):
        return pl.BlockSpec((None, tr, tc), lambda r, q, cc, k=k: ((cc[0] + k) % n_chip, r, q))

    grid_spec = pltpu.PrefetchScalarGridSpec(
        num_scalar_prefetch=1, grid=(rows // tr, per_half),
        in_specs=[slot(0), slot(1), slot(2), slot(3)],
        out_specs=pl.BlockSpec((tr, tc), lambda r, q, cc: (r, cc[1] * per_half + q)))
    return pl.pallas_call(
        body, grid_spec=grid_spec, out_shape=jax.ShapeDtypeStruct((rows, 2 * hc), F32),
        compiler_params=_params("arbitrary", "arbitrary"), name=name,
    )(chip_core, own, landed, landed, landed)


def _join_comm(halves):
    n_w = len(halves)

    def first(c_ins, c_outs, ss, rs):
        x, y, c = _my_place()
        for w in range(n_w):
            mine = _col_half(c_outs[w], c)
            _remote(mine, mine, ss, rs, w, (x, y, 1 - c)).start()

    def last(c_ins, c_outs, ss, rs):
        x, y, c = _my_place()
        for w in range(n_w):
            theirs = _col_half(c_outs[w], 1 - c)
            _remote(theirs, theirs, ss, rs, w, (x, y, 1 - c)).wait()

    return _Comm(ins=halves, outs=[jax.ShapeDtypeStruct(h.shape, h.dtype) for h in halves],
                 aliases={w: w for w in range(n_w)}, n_sems=n_w, first=first, last=last)


def _standalone(comm, *, name):
    def body(o_ref):
        o_ref[...] = jnp.zeros_like(o_ref)

    return _call(body, grid=(1,), in_specs=[],
                 out_specs=[pl.BlockSpec((SUBLANES, LANES), lambda i: (0, 0))],
                 out_shape=[jax.ShapeDtypeStruct((SUBLANES, LANES), F32)], operands=(), name=name,
                 comm=comm)[1:]


def _pack(pieces):
    flat, spans, off = [], [], 0
    for p in pieces:
        v = p.reshape(-1).astype(F32)
        pad = (-v.shape[0]) % LANES
        if pad:
            v = jnp.concatenate([v, jnp.zeros((pad,), F32)])
        spans.append((off, p.size))
        off += v.shape[0]
        flat.append(v)
    tail = (-off) % (SUBLANES * LANES)
    if tail:
        flat.append(jnp.zeros((tail,), F32))
    return jnp.concatenate(flat).reshape(-1, LANES), spans


def _unpack(buf, span, shape):
    off, size = span
    return buf.reshape(-1)[off:off + size].reshape(shape)


def kernel(x, norm1_g, w_in, w_gate_up, b_gate, conv_w, conv_norm_g, gla_norm_g, w_out, norm2_g, w_ff1, w_ff2, norm_f_g, loss_target, m_norm1_g, m_w_in, m_w_gate_up, m_b_gate, m_conv_w, m_conv_norm_g, m_gla_norm_g, m_w_out, m_norm2_g, m_w_ff1, m_w_ff2, m_norm_f_g, v_norm1_g, v_w_in, v_w_gate_up, v_b_gate, v_conv_w, v_conv_norm_g, v_gla_norm_g, v_w_out, v_norm2_g, v_w_ff1, v_w_ff2, v_norm_f_g):
    xs = x[0]
    target = loss_target[0]
    s_len, d = xs.shape
    d_in = w_in.shape[2] * N_CHIPS
    d_main = d_in - GATE_RANK
    d_ff = w_ff1.shape[2] * N_CHIPS
    cx, cy, cc = _my_place()
    chip = 2 * cx + cy
    chip_core = jnp.stack([chip, cc]).astype(jnp.int32)
    n_ff = d_ff // N_CHIPS
    norm_f = norm_f_g.reshape(1, d)

    n_sh = d_in // N_CHIPS
    wi_buf = _cast_into(w_in[0].T, chip_core, name="cast_w_in")

    small_w, spans_w = _pack([w_gate_up[0], conv_w[0]])
    small_all = _allgather_small(small_w, reduce=False, name="gather_small_weights")
    chips_first = [small_all[2 * j] for j in range(N_CHIPS)]
    wgu_full = jnp.concatenate(
        [_unpack(b, spans_w[0], w_gate_up.shape[1:]) for b in chips_first], axis=1)
    convw_full = jnp.concatenate(
        [_unpack(b, spans_w[1], conv_w.shape[1:]) for b in chips_first], axis=0)
    wgu_pad = jnp.concatenate(
        [wgu_full, jnp.zeros((LANES - GATE_RANK, wgu_full.shape[1]), F32)], axis=0).astype(BF16)
    convw_t = convw_full.T

    u, wo_buf, w1_buf, w2_buf, wi_buf = _rms_fwd(
        xs, norm1_g, name="norm1_fwd", comm=_gather_comm([wi_buf]), place=chip_core,
        side=_cast_side([w_out[0], w_ff1[0], w_ff2[0]], n_steps=s_len // 512, step_of=lambda i, j, k: i))
    wi_t = wi_buf.reshape(d_in, d)
    wg_t = jnp.concatenate([wi_t[d_main:], jnp.zeros((LANES - GATE_RANK, d), BF16)], axis=0)
    z, wo_buf, w1_buf = _matmul(u, wi_t, tb=True, tm=2048, tn=1024, tk=d, out_dtypes=[F32], n_dim=d_main,
                                name="in_proj",
                                comm=_gather_comm([wo_buf, w1_buf], rows=[None, (0, d // 2)]))
    wo_full = wo_buf.reshape(d, d)
    (alow,) = _matmul(u, wg_t, tb=True, tm=1024, tn=LANES, tk=d, out_dtypes=[F32], name="in_proj_gate")
    y0 = _conv_fwd(z, convw_t, conv_norm_g)
    y, o, la, st, w1_cm = _gla_fwd(z, alow, wgu_pad, b_gate, gla_norm_g, y0,
                                   comm=_gather_comm([w1_buf], rows=[(d // 2, d // 2)], mid_at=0.9))
    x2, h = _matmul(y, wo_full, tm=512, tn=d, tk=d, out_dtypes=[F32, BF16], extras=(xs, norm2_g),
                    epilogue=_residual_norm_epilogue, name="out_proj")
    a, p, w2_buf = _matmul(
        h, w1_cm, tm=2048, tn=1024, tk=d, out_dtypes=[BF16, BF16], n_dim=d_ff,
        b_spec=pl.BlockSpec((None, d, 1024), lambda i, j, k: (j // 2, 0, j % 2)),
        epilogue=lambda acc: (acc, jnp.square(jnp.maximum(acc, 0.0))), name="ff1",
        comm=_gather_comm([w2_buf]))
    w2_full = w2_buf.reshape(d_ff, d)
    (x3,) = _matmul(p, w2_full, tm=1024, tn=1024, tk=2048, out_dtypes=[F32], extras=(x2,),
                    epilogue=_add_epilogue, name="ff2")
    dx3b, g_normf, loss_part = _loss_head(x3, norm_f, target)

    (da,) = _matmul(dx3b, w2_full, tb=True, tm=2048, tn=1024, tk=d, out_dtypes=[BF16], extras=(a,),
                    epilogue=lambda acc, av: (acc * (2.0 * jnp.maximum(av, 0.0)),), name="ff2_dx")
    (dh,) = _matmul(
        da, w1_cm, tb=True, tm=2048, tn=1024, tk=2048, out_dtypes=[BF16], n_dim=d,
        b_spec=pl.BlockSpec((None, 1024, 2048), lambda i, j, k: (k, j, 0)), name="ff1_dx")
    dw_k = s_len // 1024
    g_w2, dx2, g_norm2 = _matmul(
        p, dx3b, ta=True, tm=1024, tn=d, tk=1024, out_dtypes=[BF16], name="ff2_dw",
        side=_rms_bwd_side(dh, x2, norm2_g, dx3b, n_steps=(d_ff // 1024) * dw_k,
                           block_of_step=lambda i, j, k: i * dw_k + k))
    g_w2 = g_w2.reshape(N_CHIPS, n_ff, d)
    dy, t_w2 = _matmul(dx2, wo_full, tb=True, tm=1024, tn=1024, tk=d, out_dtypes=[BF16],
                       name="out_proj_dx", comm=_swap_comm([g_w2]))
    f1_j, f1_k = d_ff // n_ff, s_len // 2048
    g_w1, dz0, g_convw_t, g_convg = _matmul(
        h, da, ta=True, tm=1024, tn=n_ff, tk=2048, out_dtypes=[BF16], name="ff1_dw",
        out_shapes=[jax.ShapeDtypeStruct((N_CHIPS, d, n_ff), BF16)],
        out_specs=[pl.BlockSpec((None, 1024, n_ff), lambda i, j, k: (j, i, 0))],
        side=_conv_bwd_side(dy, z, convw_t, conv_norm_g, n_steps=(d // 1024) * f1_j * f1_k,
                            step_of=lambda i, j, k: (i * f1_j + j) * f1_k + k))
    ow_k = s_len // 512
    g_wo, p_w2, t_w1 = _matmul(
        y, dx2, ta=True, tm=1024, tn=d, tk=512, out_dtypes=[BF16], name="out_proj_dw",
        side=_add_own_half_side([g_w2], [t_w2], n_steps=(d // 1024) * ow_k,
                                step_of=lambda i, j, k: i * ow_k + k),
        comm=_swap_comm([g_w1]), place=chip_core)
    g_wo = g_wo.reshape(N_CHIPS, d // N_CHIPS, d)
    dz, dalow, g_wgu_pad, g_bg, g_gg, p_w1, l_w2 = _gla_bwd(
        dy, z, o, la, st, alow, wgu_pad, gla_norm_g, dz0, comm=_exchange_comm([p_w2]), place=chip_core,
        side=_add_own_half_side([g_w1], [t_w1], n_steps=s_len // 512, step_of=lambda i, j, k: i))
    dwi_k = s_len // 2048
    g_wi_t, m_w2, l_w1 = _matmul(
        dz, u, ta=True, tm=1024, tn=d, tk=2048, out_dtypes=[BF16], name="in_proj_dw",
        out_shapes=[jax.ShapeDtypeStruct((d_in, d), BF16)], comm=_exchange_comm([p_w1]),
        side=_sum_chips_side(p_w2, l_w2, n_blocks=_pow2_below(d_main // 1024 * dwi_k),
                             step_of=lambda i, j, k: i * dwi_k + k),
        place=chip_core)
    g_wi_t, m_w1 = _matmul(
        dalow, u, ta=True, tm=LANES, tn=d, tk=1024, out_dtypes=[BF16],
        epilogue=lambda acc: (acc[:GATE_RANK],), into=g_wi_t, name="in_proj_gate_dw",
        out_shapes=[jax.ShapeDtypeStruct((d_in, d), BF16)],
        out_specs=[pl.BlockSpec((GATE_RANK, d), lambda i, j, k: (d_main // GATE_RANK, 0))],
        side=_sum_chips_side(p_w1, l_w1, n_blocks=_pow2_below(s_len // 1024), step_of=lambda i, j, k: k),
        place=chip_core)
    t_wi, t_wo = _standalone(_swap_comm([g_wi_t, g_wo]), name="swap_w_in_w_out")
    p_wi = _add_own_half(g_wi_t[None], t_wi[None], chip_core, name="pre_reduce_w_in")
    p_wi = p_wi.reshape(N_CHIPS, n_sh, d // 2)
    p_wo = _add_own_half(g_wo, t_wo, chip_core, name="pre_reduce_w_out")
    du, l_wi, l_wo, m_w1, m_w2 = _matmul(
        dz, wi_t, tm=2048, tn=1024, tk=2048, out_dtypes=[BF16], n_dim=d, extras=(dalow, wg_t),
        epilogue=lambda acc, dal, wg: (acc + jnp.dot(dal.astype(BF16), wg, preferred_element_type=F32),),
        name="in_proj_dx", comm=[_exchange_comm([p_wi, p_wo]), _join_comm([m_w1, m_w2])])
    m_wi = _sum_chips(p_wi, l_wi, chip_core, name="reduce_w_in")
    m_wo = _sum_chips(p_wo, l_wo, chip_core, name="reduce_w_out")
    grad_x, g_norm1 = _rms_bwd(du, xs, norm1_g, dx2, name="norm1_bwd")
    m_wi, m_wo = _standalone(_join_comm([m_wi, m_wo]), name="join_w_in_w_out")
    g_big = [m_wi, m_wo, m_w1, m_w2]

    small_g, spans_g = _pack([g_norm1, g_wgu_pad[:GATE_RANK], g_bg, g_convw_t, g_convg, g_gg, g_norm2,
                              g_normf, loss_part[:, :1]])
    tot = _allgather_small(small_g, reduce=True, name="reduce_small_grads")
    t_norm1 = _unpack(tot, spans_g[0], (1, d))
    t_wgu = _unpack(tot, spans_g[1], (GATE_RANK, HEADS * DK))
    t_bg = _unpack(tot, spans_g[2], (1, HEADS * DK))
    t_convw = _unpack(tot, spans_g[3], (3, D_CONV)).T
    t_convg = _unpack(tot, spans_g[4], (1, D_CONV))
    t_gg = _unpack(tot, spans_g[5], (1, DV))
    t_norm2 = _unpack(tot, spans_g[6], (1, d))
    t_normf = _unpack(tot, spans_g[7], (1, d))
    loss = _unpack(tot, spans_g[8], ())
    n_gu = w_gate_up.shape[2]
    n_cw = conv_w.shape[1]
    t_wgu = lax.dynamic_slice(t_wgu, (0, chip * n_gu), (GATE_RANK, n_gu))
    t_convw = lax.dynamic_slice(t_convw, (chip * n_cw, 0), (n_cw, 3))

    order = ["norm1_g", "w_in", "w_gate_up", "b_gate", "conv_w", "conv_norm_g", "gla_norm_g", "w_out",
             "norm2_g", "w_ff1", "w_ff2", "norm_f_g"]
    weights = dict(norm1_g=norm1_g, w_in=w_in, w_gate_up=w_gate_up, b_gate=b_gate, conv_w=conv_w,
                   conv_norm_g=conv_norm_g, gla_norm_g=gla_norm_g, w_out=w_out, norm2_g=norm2_g,
                   w_ff1=w_ff1, w_ff2=w_ff2, norm_f_g=norm_f_g)
    moms = dict(norm1_g=m_norm1_g, w_in=m_w_in, w_gate_up=m_w_gate_up, b_gate=m_b_gate, conv_w=m_conv_w,
                conv_norm_g=m_conv_norm_g, gla_norm_g=m_gla_norm_g, w_out=m_w_out, norm2_g=m_norm2_g,
                w_ff1=m_w_ff1, w_ff2=m_w_ff2, norm_f_g=m_norm_f_g)
    vels = dict(norm1_g=v_norm1_g, w_in=v_w_in, w_gate_up=v_w_gate_up, b_gate=v_b_gate, conv_w=v_conv_w,
                conv_norm_g=v_conv_norm_g, gla_norm_g=v_gla_norm_g, w_out=v_w_out, norm2_g=v_norm2_g,
                w_ff1=v_w_ff1, w_ff2=v_w_ff2, norm_f_g=v_norm_f_g)
    grads2d = dict(norm1_g=t_norm1, w_in=g_big[0], w_gate_up=t_wgu, b_gate=t_bg, conv_w=t_convw,
                   conv_norm_g=t_convg, gla_norm_g=t_gg, w_out=g_big[1], norm2_g=t_norm2,
                   w_ff1=g_big[2], w_ff2=g_big[3], norm_f_g=t_normf)
    out_g, out_d, out_m, out_v = [], [], [], []
    for nm in order:
        w = weights[nm]
        g2 = grads2d[nm]
        if nm == "w_in":
            to2d, back = (lambda t: t[0].T), (lambda t: t.T.reshape(w.shape))
        else:
            to2d, back = (lambda t: t.reshape(g2.shape)), (lambda t: t.reshape(w.shape))
        res = _adamw(to2d(w), g2, to2d(moms[nm]), to2d(vels[nm]), name="adamw_" + nm)
        for lst, r in zip((out_g, out_d, out_m, out_v), res):
            lst.append(back(r))
    return (loss, grad_x.reshape(x.shape), *out_g, *out_d, *out_m, *out_v)
```

```python
import functools

import jax
import jax.numpy as jnp
from jax import lax
from jax.experimental import pallas as pl
from jax.experimental.pallas import tpu as pltpu

F32 = jnp.float32
BF16 = jnp.bfloat16
MESH = pl.DeviceIdType.MESH

EPS = 1e-6
CHUNK = 64
HEADS = 4
DK = 128
DV = 256
D_CONV = 1024
GROUP = 128
GATE_RANK = 16
LANES = 128
SUBLANES = 8
N_CHIPS = 4
N_DEV = 8
CHIP_MASKS = ((1, 0), (0, 1), (1, 1))

ADAM_LR = 0.001
ADAM_B1 = 0.9
ADAM_B2 = 0.999
ADAM_EPS = 1e-08
ADAM_WD = 0.01
ADAM_STEP = 10

VMEM_LIMIT = 60 * 1024 * 1024


def _params(*sem):
    return pltpu.CompilerParams(dimension_semantics=tuple(sem), vmem_limit_bytes=VMEM_LIMIT)


def _rowsum8(v):
    r, c = v.shape
    return jnp.sum(v.reshape(r // SUBLANES, SUBLANES, c), axis=0)


def _tile(rows, cols):
    for cand in (256, 128, 64, 32, 16, 8):
        if rows % cand == 0 and cand < rows <= 64 * cand:
            return cand, cols
    if rows * cols * 4 > (2 << 20) and cols % 256 == 0:
        return rows, 256
    return rows, cols


class _Comm:
    def __init__(self, ins, outs, aliases, n_sems, first, last, mid=None, mid_at=0.75):
        self.ins = list(ins)
        self.outs = list(outs)
        self.aliases = dict(aliases)
        self.n_sems = n_sems
        self.first = first
        self.mid = mid
        self.mid_at = mid_at
        self.last = last


class _PlaceSpec:
    def __init__(self, block_shape, index_map):
        self.block_shape, self.index_map = block_shape, index_map


def _call(body, *, grid, in_specs, out_specs, out_shape, operands, name, scratch_shapes=(), sem=None,
          aliases=None, comm=None, place=None):
    aliases = dict(aliases or {})
    comms = [] if comm is None else (list(comm) if isinstance(comm, (list, tuple)) else [comm])
    n_in, n_out, n_scr = len(in_specs), len(out_specs), len(scratch_shapes)
    c_ins_all = [a for cm in comms for a in cm.ins]
    c_outs_all = [o for cm in comms for o in cm.outs]
    n_ci, n_co = len(c_ins_all), len(c_outs_all)
    n_place = 0 if place is None else 1

    def adapt(spec):
        if isinstance(spec, _PlaceSpec):
            return pl.BlockSpec(spec.block_shape, spec.index_map)
        if place is None or spec.index_map is None:
            return spec
        return pl.BlockSpec(spec.block_shape, lambda *a, f=spec.index_map: f(*a[:-1]))

    def full_body(*refs):
        refs = refs[n_place:]
        ins = refs[:n_in]
        o0 = n_in + n_ci
        outs = refs[o0:o0 + n_out]
        s0 = o0 + n_out + n_co
        scr = refs[s0:s0 + n_scr]
        sems = refs[s0 + n_scr:]
        parts, i_at, o_at = [], n_in, o0 + n_out
        for q, cm in enumerate(comms):
            parts.append((cm, refs[i_at:i_at + len(cm.ins)], refs[o_at:o_at + len(cm.outs)],
                          sems[2 * q], sems[2 * q + 1]))
            i_at += len(cm.ins)
            o_at += len(cm.outs)
        step = functools.reduce(lambda acc, ig: acc * ig[1] + pl.program_id(ig[0]), enumerate(grid), 0)
        n_steps = functools.reduce(lambda acc, g: acc * g, grid, 1)
        @pl.when(step == 0)
        def _():
            for cm, c_ins, c_outs, ss, rs in parts:
                cm.first(c_ins, c_outs, ss, rs)

        def mid_step(cm):
            ms = int(cm.mid_at * n_steps)
            return ms if 0 < ms < n_steps - 1 else None

        for cm, c_ins, c_outs, ss, rs in parts:
            if cm.mid is not None and mid_step(cm) is not None:
                @pl.when(step == mid_step(cm))
                def _(cm=cm, c_ins=c_ins, c_outs=c_outs, ss=ss, rs=rs):
                    cm.mid(c_ins, c_outs, ss, rs)

        body(*ins, *outs, *scr)

        @pl.when(step == n_steps - 1)
        def _():
            for cm, c_ins, c_outs, ss, rs in parts:
                if cm.mid is not None and mid_step(cm) is None:
                    cm.mid(c_ins, c_outs, ss, rs)
                cm.last(c_ins, c_outs, ss, rs)

    any_spec = pl.BlockSpec(memory_space=pl.ANY)
    i_at, o_at, sem_shapes = n_in, n_out, []
    for cm in comms:
        for i_in, i_out in cm.aliases.items():
            aliases[i_at + i_in] = o_at + i_out
        i_at += len(cm.ins)
        o_at += len(cm.outs)
        sem_shapes += [pltpu.SemaphoreType.DMA((cm.n_sems,)), pltpu.SemaphoreType.DMA((cm.n_sems,))]
    specs = dict(grid=grid, in_specs=[adapt(s) for s in in_specs] + [any_spec] * n_ci,
                 out_specs=[adapt(s) for s in out_specs] + [any_spec] * n_co,
                 scratch_shapes=list(scratch_shapes) + sem_shapes)
    if place is not None:
        specs = dict(grid_spec=pltpu.PrefetchScalarGridSpec(num_scalar_prefetch=1, **specs))
    if comms:
        params = pltpu.CompilerParams(dimension_semantics=("arbitrary",) * len(grid),
                                      vmem_limit_bytes=VMEM_LIMIT, has_side_effects=True)
    else:
        params = _params(*(sem or ("arbitrary",) * len(grid)))
    return pl.pallas_call(
        full_body, out_shape=list(out_shape) + c_outs_all,
        input_output_aliases={k + n_place: v for k, v in aliases.items()},
        compiler_params=params, name=name, **specs,
    )(*(() if place is None else (place,)), *operands, *c_ins_all)


def _matmul(a, b, *, ta=False, tb=False, tm, tn, tk, out_dtypes, name, extras=(), epilogue=None,
            out_shapes=None, out_specs=None, b_spec=None, n_dim=None, into=None, side=None, comm=None,
            place=None):
    n_into = 0 if into is None else 1
    if ta:
        k_dim, m_dim = a.shape
    else:
        m_dim, k_dim = a.shape
    if n_dim is None:
        n_dim = b.shape[0] if tb else b.shape[1]
        assert (b.shape[1] if tb else b.shape[0]) == k_dim
    assert m_dim % tm == 0 and n_dim % tn == 0 and k_dim % tk == 0, (name, a.shape, b.shape)
    nk = k_dim // tk
    n_ex, n_out = len(extras), len(out_dtypes)
    dims = (((0 if ta else 1,), (1 if tb else 0,)), ((), ()))
    grid = (m_dim // tm, n_dim // tn, nk)
    s_ins, s_outs, s_scr = (len(side.ins), len(side.outs), len(side.scratch)) if side else (0, 0, 0)

    def body(*refs):
        a_ref, b_ref = refs[0], refs[1]
        ex_refs = refs[2:2 + n_ex]
        i0 = 2 + n_ex + n_into
        side_in = refs[i0:i0 + s_ins]
        o_refs = refs[i0 + s_ins:i0 + s_ins + n_out]
        side_out = refs[i0 + s_ins + n_out:i0 + s_ins + n_out + s_outs]
        side_scr = refs[len(refs) - s_scr:] if s_scr else ()
        if side is not None:
            step = (pl.program_id(0) * grid[1] + pl.program_id(1)) * grid[2] + pl.program_id(2)

            @pl.when(step == 0)
            def _():
                side.init(side_scr)

        def dot():
            if side is not None:
                side.body(step, side_in, side_out, side_scr)
            return lax.dot_general(a_ref[...].astype(BF16), b_ref[...].astype(BF16), dims,
                                   preferred_element_type=F32)

        def finish(acc):
            outs = epilogue(acc, *[e[...] for e in ex_refs]) if epilogue is not None else (acc,)
            for o_ref, o in zip(o_refs, outs):
                o_ref[...] = o.astype(o_ref.dtype)

        if nk == 1:
            finish(dot())
        else:
            acc_ref = refs[len(refs) - s_scr - 1]
            k = pl.program_id(2)

            @pl.when(k == 0)
            def _():
                acc_ref[...] = dot()

            @pl.when(jnp.logical_and(k > 0, k < nk - 1))
            def _():
                acc_ref[...] += dot()

            @pl.when(k == nk - 1)
            def _():
                finish(acc_ref[...] + dot())

    a_spec = (pl.BlockSpec((tk, tm), lambda i, j, k: (k, i)) if ta
              else pl.BlockSpec((tm, tk), lambda i, j, k: (i, k)))
    if b_spec is None:
        b_spec = (pl.BlockSpec((tn, tk), lambda i, j, k: (j, k)) if tb
                  else pl.BlockSpec((tk, tn), lambda i, j, k: (k, j)))
    io_spec = pl.BlockSpec((tm, tn), lambda i, j, k: (i, j))

    def extra_spec(e):
        if e.shape == (m_dim, n_dim):
            return io_spec
        if e.shape[1] == n_dim:
            return pl.BlockSpec((e.shape[0], tn), lambda i, j, k: (0, j))
        assert e.shape[0] == m_dim, (name, e.shape)
        return pl.BlockSpec((tm, e.shape[1]), lambda i, j, k: (i, 0))

    if out_shapes is None:
        out_shapes = [jax.ShapeDtypeStruct((m_dim, n_dim), dt) for dt in out_dtypes]
    if out_specs is None:
        out_specs = [io_spec] * n_out
    return _call(
        body,
        grid=grid,
        in_specs=([a_spec, b_spec] + [extra_spec(e) for e in extras]
                  + [pl.BlockSpec(memory_space=pl.ANY)] * n_into
                  + (list(side.in_specs) if side else [])),
        out_specs=list(out_specs) + (list(side.out_specs) if side else []),
        out_shape=list(out_shapes) + (list(side.outs) if side else []),
        scratch_shapes=([pltpu.VMEM((tm, tn), F32)] if nk > 1 else []) + (list(side.scratch) if side else []),
        sem=("parallel", "parallel", "arbitrary") if side is None else None,
        aliases={2 + n_ex: 0} if n_into else None,
        operands=(a, b, *extras) + ((into,) if n_into else ()) + (tuple(side.ins) if side else ()),
        name=name, comm=comm, place=place)


def _add_epilogue(acc, r):
    return (acc + r,)


def _residual_norm_epilogue(acc, x, g):
    x2 = acc + x
    r = lax.rsqrt(jnp.mean(x2 * x2, axis=-1, keepdims=True) + EPS)
    return x2, x2 * r * g


def _rms_fwd(x, g, *, name, tm=512, comm=None, side=None, place=None):
    s_len, d = x.shape
    s_ins, s_outs = (len(side.ins), len(side.outs)) if side else (0, 0)

    def body(*refs):
        x_ref, g_ref = refs[0], refs[1]
        o_ref = refs[2 + s_ins]
        xv = x_ref[...]
        r = lax.rsqrt(jnp.mean(xv * xv, axis=-1, keepdims=True) + EPS)
        o_ref[...] = (xv * r * g_ref[...]).astype(o_ref.dtype)
        if side is not None:
            scr = refs[3 + s_ins + s_outs:]

            @pl.when(pl.program_id(0) == 0)
            def _():
                side.init(scr)

            side.body(pl.program_id(0), refs[2:2 + s_ins], refs[3 + s_ins:3 + s_ins + s_outs], scr)

    row = pl.BlockSpec((tm, d), lambda i, j, k: (i, 0))
    return _call(
        body, grid=(s_len // tm, 1, 1),
        in_specs=[row, pl.BlockSpec((1, d), lambda i, j, k: (0, 0))] + (list(side.in_specs) if side else []),
        out_specs=[row] + (list(side.out_specs) if side else []),
        out_shape=[jax.ShapeDtypeStruct((s_len, d), BF16)] + (list(side.outs) if side else []),
        scratch_shapes=list(side.scratch) if side else [],
        operands=(x, g) + (tuple(side.ins) if side else ()), name=name, comm=comm, place=place)


NORM_ROWS = 16


def _zero_refs(refs):
    for r in refs:
        r[...] = jnp.zeros_like(r)


def _rms_bwd_block(dn_ref, x_ref, g_ref, res_ref, dx_ref, dg_ref, acc_ref):
    gv = g_ref[...]
    acc = acc_ref[...]
    for s in range(x_ref.shape[0] // NORM_ROWS):
        sl = slice(s * NORM_ROWS, (s + 1) * NORM_ROWS)
        xv = x_ref[sl, :]
        dnv = dn_ref[sl, :].astype(F32)
        r = lax.rsqrt(jnp.mean(xv * xv, axis=-1, keepdims=True) + EPS)
        xh = xv * r
        acc = acc + _rowsum8(dnv * xh)
        dxh = dnv * gv
        dx_ref[sl, :] = (r * (dxh - xh * jnp.mean(dxh * xh, axis=-1, keepdims=True))
                         + res_ref[sl, :].astype(F32))
    acc_ref[...] = acc
    dg_ref[...] = jnp.sum(acc, axis=0, keepdims=True)


class _Side:
    def __init__(self, ins, in_specs, outs, out_specs, scratch, init, body):
        self.ins, self.in_specs, self.outs, self.out_specs = ins, in_specs, outs, out_specs
        self.scratch = scratch
        self.init = init
        self.body = body


def _rms_bwd_side(dn, x, g, res, *, block_of_step, n_steps):
    s_len, d = x.shape
    row = pl.BlockSpec((s_len // n_steps, d), lambda i, j, k: (block_of_step(i, j, k), 0))
    vec = pl.BlockSpec((1, d), lambda i, j, k: (0, 0))
    return _Side(
        ins=[dn, x, g, res], in_specs=[row, row, vec, row],
        outs=[jax.ShapeDtypeStruct((s_len, d), F32), jax.ShapeDtypeStruct((1, d), F32)],
        out_specs=[row, vec], scratch=[pltpu.VMEM((SUBLANES, d), F32)],
        init=_zero_refs, body=lambda step, ins, outs, scr: _rms_bwd_block(*ins, *outs, *scr))


def _rms_bwd(dn, x, g, res, *, name, tm=512, comm=None):
    s_len, d = x.shape
    n = s_len // tm

    def body(*refs):
        @pl.when(pl.program_id(0) == 0)
        def _():
            _zero_refs(refs[-1:])

        _rms_bwd_block(*refs)

    row = pl.BlockSpec((tm, d), lambda i: (i, 0))
    vec = pl.BlockSpec((1, d), lambda i: (0, 0))
    return _call(
        body, grid=(n,),
        in_specs=[row, row, vec, row],
        out_specs=[row, vec],
        out_shape=[jax.ShapeDtypeStruct((s_len, d), F32), jax.ShapeDtypeStruct((1, d), F32)],
        scratch_shapes=[pltpu.VMEM((SUBLANES, d), F32)],
        operands=(dn, x, g, res), name=name, comm=comm)


def _loss_head(x3, g, target, *, tm=512):
    s_len, d = x3.shape
    n = s_len // tm

    def body(x_ref, g_ref, t_ref, dxb_ref, dg_ref, loss_ref, accg_ref, accl_ref):
        i = pl.program_id(0)
        xv = x_ref[...]
        gv = g_ref[...]
        r = lax.rsqrt(jnp.mean(xv * xv, axis=-1, keepdims=True) + EPS)
        xh = xv * r
        err = xh * gv - t_ref[...]

        @pl.when(i == 0)
        def _():
            accg_ref[...] = jnp.zeros_like(accg_ref)
            accl_ref[...] = jnp.zeros_like(accl_ref)

        accl_ref[...] += _rowsum8(err * err)
        dn = err * (1.0 / d)
        accg_ref[...] += _rowsum8(dn * xh)
        dxh = dn * gv
        dx = r * (dxh - xh * jnp.mean(dxh * xh, axis=-1, keepdims=True))
        dxb_ref[...] = dx.astype(BF16)

        @pl.when(i == n - 1)
        def _():
            dg_ref[...] = jnp.sum(accg_ref[...], axis=0, keepdims=True)
            tot = jnp.sum(jnp.sum(accl_ref[...], axis=0, keepdims=True), axis=1, keepdims=True)
            loss_ref[...] = jnp.broadcast_to(tot * (0.5 / d), (1, LANES))

    row = pl.BlockSpec((tm, d), lambda i: (i, 0))
    vec = pl.BlockSpec((1, d), lambda i: (0, 0))
    return pl.pallas_call(
        body, grid=(n,),
        in_specs=[row, vec, row],
        out_specs=[row, vec, pl.BlockSpec((1, LANES), lambda i: (0, 0))],
        out_shape=[jax.ShapeDtypeStruct((s_len, d), BF16),
                   jax.ShapeDtypeStruct((1, d), F32), jax.ShapeDtypeStruct((1, LANES), F32)],
        scratch_shapes=[pltpu.VMEM((SUBLANES, d), F32), pltpu.VMEM((SUBLANES, d), F32)],
        compiler_params=_params("arbitrary"), name="loss_head",
    )(x3, g, target)


def _shift_down(v, k, rows_before, row):
    out = pltpu.roll(v, k, axis=0)
    for j in range(k):
        out = jnp.where(row == j, rows_before[j], out)
    return out


def _shift_up(v, k, rows_after, row):
    t = v.shape[0]
    out = pltpu.roll(v, t - k, axis=0)
    for j in range(k):
        out = jnp.where(row == t - k + j, rows_after[j], out)
    return out


def _conv_fwd(z, w_t, gain, *, ts=512):
    s_len = z.shape[0]
    n_grp = D_CONV // GROUP

    def body(cb_ref, cc_ref, ch_ref, w_ref, g_ref, y_ref, carry_ref):
        i = pl.program_id(0)

        @pl.when(i == 0)
        def _():
            carry_ref[...] = jnp.zeros_like(carry_ref)

        row = lax.broadcasted_iota(jnp.int32, (ts, GROUP), 0)
        for g in range(n_grp):
            sl = slice(g * GROUP, (g + 1) * GROUP)
            uu = cc_ref[:, sl] * ch_ref[:, sl]
            p2 = carry_ref[6:7, sl]
            p1 = carry_ref[7:8, sl]
            u1 = _shift_down(uu, 1, [p1], row)
            u2 = _shift_down(uu, 2, [p2, p1], row)
            conv = w_ref[0:1, sl] * u2 + w_ref[1:2, sl] * u1 + w_ref[2:3, sl] * uu
            y = cb_ref[:, sl] * conv
            carry_ref[:, sl] = uu[ts - SUBLANES:ts, :]
            rg = lax.rsqrt(jnp.mean(y * y, axis=-1, keepdims=True) + EPS)
            y_ref[:, sl] = (y * rg * g_ref[:, sl]).astype(BF16)

    def col(j):
        return pl.BlockSpec((ts, D_CONV), lambda i, j=j: (i, j))

    small = lambda r: pl.BlockSpec((r, D_CONV), lambda i: (0, 0))
    return pl.pallas_call(
        body, grid=(s_len // ts,),
        in_specs=[col(0), col(1), col(2), small(3), small(1)],
        out_specs=col(0),
        out_shape=jax.ShapeDtypeStruct((s_len, 2 * D_CONV), BF16),
        scratch_shapes=[pltpu.VMEM((SUBLANES, D_CONV), F32)],
        compiler_params=_params("arbitrary"), name="conv_fwd",
    )(z, z, z, w_t, gain)


def _conv_bwd_side(dy, z, w_t, gain, *, n_steps, step_of):
    s_len = z.shape[0]
    n = n_steps
    ts = s_len // n
    n_grp = D_CONV // GROUP
    halo_blocks = ts // SUBLANES

    def body(step, ins, outs, scr):
        dy_ref, cb_ref, cc_ref, ch_ref, hcc_ref, hch_ref, w_ref, g_ref = ins
        dz_ref, dw_ref, dg_ref = outs
        carry_ref, accw_ref, accg_ref = scr
        first_tile = (n - 1 - step) == 0
        row = lax.broadcasted_iota(jnp.int32, (ts, GROUP), 0)
        keep = jnp.where(first_tile, 0.0, 1.0)
        for g in range(n_grp):
            sl = slice(g * GROUP, (g + 1) * GROUP)
            cc = cc_ref[:, sl]
            ch = ch_ref[:, sl]
            cb = cb_ref[:, sl]
            uu = cc * ch
            p2 = hcc_ref[6:7, sl] * hch_ref[6:7, sl] * keep
            p1 = hcc_ref[7:8, sl] * hch_ref[7:8, sl] * keep
            u1 = _shift_down(uu, 1, [p1], row)
            u2 = _shift_down(uu, 2, [p2, p1], row)
            w0, w1, w2 = w_ref[0:1, sl], w_ref[1:2, sl], w_ref[2:3, sl]
            conv = w0 * u2 + w1 * u1 + w2 * uu
            y = cb * conv
            rg = lax.rsqrt(jnp.mean(y * y, axis=-1, keepdims=True) + EPS)
            yh = y * rg
            dyv = dy_ref[:, sl].astype(F32)
            accg_ref[:, sl] += _rowsum8(dyv * yh)
            dyn = dyv * g_ref[:, sl]
            dpre = rg * (dyn - yh * jnp.mean(dyn * yh, axis=-1, keepdims=True))
            dz_ref[:, sl] = (dpre * conv).astype(BF16)
            dconv = dpre * cb
            accw_ref[0:8, sl] += _rowsum8(dconv * u2)
            accw_ref[8:16, sl] += _rowsum8(dconv * u1)
            accw_ref[16:24, sl] += _rowsum8(dconv * uu)
            n0 = carry_ref[0:1, sl]
            n1 = carry_ref[1:2, sl]
            d1 = _shift_up(dconv, 1, [n0], row)
            d2 = _shift_up(dconv, 2, [n0, n1], row)
            duu = w2 * dconv + w1 * d1 + w0 * d2
            carry_ref[:, sl] = dconv[0:SUBLANES, :]
            dz_ref[:, D_CONV + g * GROUP:D_CONV + (g + 1) * GROUP] = (duu * ch).astype(BF16)
            dz_ref[:, 2 * D_CONV + g * GROUP:2 * D_CONV + (g + 1) * GROUP] = (duu * cc).astype(BF16)

        for k in range(3):
            dw_ref[k:k + 1, :] = jnp.sum(accw_ref[8 * k:8 * k + 8, :], axis=0, keepdims=True)
        dg_ref[...] = jnp.sum(accg_ref[...], axis=0, keepdims=True)

    def tile(i, j, k):
        return n - 1 - step_of(i, j, k)

    def col(c):
        return pl.BlockSpec((ts, D_CONV), lambda i, j, k, c=c: (tile(i, j, k), c))

    def halo(c):
        return pl.BlockSpec((SUBLANES, D_CONV),
                            lambda i, j, k, c=c: (jnp.maximum(tile(i, j, k) * halo_blocks - 1, 0), c))

    small = lambda r: pl.BlockSpec((r, D_CONV), lambda i, j, k: (0, 0))
    return _Side(
        ins=[dy, z, z, z, z, z, w_t, gain],
        in_specs=[col(0), col(0), col(1), col(2), halo(1), halo(2), small(3), small(1)],
        outs=[jax.ShapeDtypeStruct((s_len, 6 * D_CONV), BF16),
              jax.ShapeDtypeStruct((3, D_CONV), F32), jax.ShapeDtypeStruct((1, D_CONV), F32)],
        out_specs=[pl.BlockSpec((ts, 3 * D_CONV), lambda i, j, k: (tile(i, j, k), 0)), small(3), small(1)],
        scratch=[pltpu.VMEM((SUBLANES, D_CONV), F32), pltpu.VMEM((24, D_CONV), F32),
                 pltpu.VMEM((SUBLANES, D_CONV), F32)],
        init=_zero_refs, body=body)


def _split3(v):
    hi = v.astype(BF16)
    r1 = v - hi.astype(F32)
    mid = r1.astype(BF16)
    lo = (r1 - mid.astype(F32)).astype(BF16)
    return jnp.concatenate([hi, mid, lo], axis=1)


def _tri_sum(tri, v):
    w = v.shape[1]
    dd = jnp.dot(tri, _split3(v), preferred_element_type=F32)
    return dd[:, :w] + dd[:, w:2 * w] + dd[:, 2 * w:]


def _chunk_masks(ts):
    r = jnp.arange(ts)
    same = (r[:, None] // CHUNK) == (r[None, :] // CHUNK)
    later = jnp.logical_and(same, r[None, :] > r[:, None]).astype(BF16)
    earlier = jnp.logical_and(same, r[None, :] < r[:, None]).astype(BF16)
    chunk_of_row = jnp.arange(ts // CHUNK * SUBLANES)[:, None] // SUBLANES
    member = (chunk_of_row == (r[None, :] // CHUNK)).astype(BF16)
    return later, earlier, member


def _sigmoid(v):
    return 0.5 * jnp.tanh(0.5 * v) + 0.5


def _gla_fwd(z, alow, wgu, bg, gg, y_in, *, ts=512, comm=None):
    s_len = z.shape[0]
    nch = ts // CHUNK
    scale = DK ** -0.5

    tri_u, _, ind8 = _chunk_masks(ts)

    def body(q_ref, k_ref, v_ref, og_ref, al_ref, wgu_ref, bg_ref, gg_ref, tu_ref, ind_ref, yin_ref,
             y_ref, o_ref, la_ref, st_ref, state_ref, kd_ref, qs_ref, dec_ref):
        del yin_ref
        i = pl.program_id(0)

        @pl.when(i == 0)
        def _():
            state_ref[...] = jnp.zeros_like(state_ref)

        pre = jnp.dot(al_ref[...].astype(BF16), wgu_ref[...], preferred_element_type=F32) + bg_ref[...]
        la = (jnp.minimum(pre, 0.0) - jnp.log(1.0 + jnp.exp(-jnp.abs(pre)))) * (1.0 / 16.0)
        la_ref[...] = la
        kd_ref[...] = (k_ref[...] * jnp.exp(_tri_sum(tu_ref[...], la))).astype(BF16)
        qs_ref[...] = (q_ref[...] * scale).astype(BF16)
        dec_ref[...] = jnp.exp(_tri_sum(ind_ref[...], la))

        def chunk(cl, carry):
            rows = pl.ds(pl.multiple_of(cl * CHUNK, CHUNK), CHUNK)
            dec = dec_ref[pl.ds(pl.multiple_of(cl * SUBLANES, SUBLANES), 1), :]
            for h in range(HEADS):
                ks = slice(h * DK, (h + 1) * DK)
                vs = slice(h * DV, (h + 1) * DV)
                kv_t = lax.dot_general(v_ref[rows, vs].astype(BF16), kd_ref[rows, ks],
                                       (((0,), (0,)), ((), ())), preferred_element_type=F32)
                st = state_ref[h] * dec[:, ks] + kv_t
                state_ref[h] = st
                st_ref[cl, h] = st
                o_ref[rows, vs] = lax.dot_general(qs_ref[rows, ks], st.astype(BF16),
                                                  (((1,), (1,)), ((), ())), preferred_element_type=F32)
            return carry

        lax.fori_loop(0, nch, chunk, 0, unroll=2)

        ggv = gg_ref[...]
        for h in range(HEADS):
            vs = slice(h * DV, (h + 1) * DV)
            o_h = o_ref[:, vs]
            og_h = og_ref[:, vs]
            ro = lax.rsqrt(jnp.mean(o_h * o_h, axis=-1, keepdims=True) + EPS)
            y_ref[:, vs] = (o_h * ro * ggv * (og_h * _sigmoid(og_h))).astype(BF16)

    def zcol(width, j):
        return pl.BlockSpec((ts, width), lambda i, j=j: (i, j))

    full = lambda shape: pl.BlockSpec(shape, lambda i: tuple(0 for _ in shape))
    return _call(
        body, grid=(s_len // ts,),
        in_specs=[zcol(512, 6), zcol(512, 7), zcol(1024, 4), zcol(1024, 5), zcol(LANES, 0),
                  full((LANES, 512)), full((1, 512)), full((1, DV)), full(tri_u.shape), full(ind8.shape),
                  pl.BlockSpec(memory_space=pl.ANY)],
        out_specs=[zcol(1024, 1), zcol(1024, 0), zcol(512, 0),
                   pl.BlockSpec((nch, HEADS, DV, DK), lambda i: (i, 0, 0, 0))],
        out_shape=[jax.ShapeDtypeStruct((s_len, 2048), BF16), jax.ShapeDtypeStruct((s_len, 1024), F32),
                   jax.ShapeDtypeStruct((s_len, 512), F32),
                   jax.ShapeDtypeStruct((s_len // CHUNK, HEADS, DV, DK), F32)],
        scratch_shapes=[pltpu.VMEM((HEADS, DV, DK), F32), pltpu.VMEM((ts, 512), BF16),
                        pltpu.VMEM((ts, 512), BF16), pltpu.VMEM((nch * SUBLANES, 512), F32)],
        aliases={10: 0},
        operands=(z, z, z, z, alow, wgu, bg, gg, tri_u, ind8, y_in), name="gla_fwd", comm=comm)


def _gla_bwd(dy, z, o, la, st, alow, wgu, gg, dz_in, *, ts=512, comm=None, side=None, place=None):
    s_len = z.shape[0]
    n = s_len // ts
    nch = ts // CHUNK
    scale = DK ** -0.5

    tri_u, tri_l, ind8 = _chunk_masks(ts)

    def body(dy_ref, q_ref, k_ref, v_ref, og_ref, o_ref, la_ref, st_ref, stp_ref, al_ref, wgu_ref,
             gg_ref, tu_ref, tl_ref, ind_ref, dzin_ref, dz_ref, dal_ref, dwgu_ref, dbg_ref, dgg_ref,
             gt_ref, decn_ref, accw_ref, accb_ref, accg_ref, dla_ref,
             e_ref, kd_ref, kdb_ref, qs_ref, do_ref, dkd_ref, dec_ref, dbe_ref):
        del dzin_ref
        i = pl.program_id(0)
        first_tile = (n - 1 - i) == 0

        @pl.when(i == 0)
        def _():
            gt_ref[...] = jnp.zeros_like(gt_ref)
            decn_ref[...] = jnp.ones_like(decn_ref)
            accw_ref[...] = jnp.zeros_like(accw_ref)
            accb_ref[...] = jnp.zeros_like(accb_ref)
            accg_ref[...] = jnp.zeros_like(accg_ref)

        la = la_ref[...]
        e_dec = jnp.exp(_tri_sum(tu_ref[...], la))
        e_ref[...] = e_dec
        kd = k_ref[...] * e_dec
        kd_ref[...] = kd
        kdb_ref[...] = kd.astype(BF16)
        qs_ref[...] = (q_ref[...] * scale).astype(BF16)
        dec_ref[...] = jnp.exp(_tri_sum(ind_ref[...], la))
        ggv = gg_ref[...]
        for h in range(HEADS):
            vs = slice(h * DV, (h + 1) * DV)
            o_h = o_ref[:, vs]
            og_h = og_ref[:, vs]
            dy_h = dy_ref[:, vs].astype(F32)
            ro = lax.rsqrt(jnp.mean(o_h * o_h, axis=-1, keepdims=True) + EPS)
            oh = o_h * ro
            sig = _sigmoid(og_h)
            sil = og_h * sig
            accg_ref[...] += _rowsum8(dy_h * oh * sil)
            dz_ref[:, 2048 + h * DV:2048 + (h + 1) * DV] = (
                dy_h * oh * ggv * sig * (1.0 + og_h * (1.0 - sig))).astype(BF16)
            don = dy_h * ggv * sil
            do_ref[:, vs] = (ro * (don - oh * jnp.mean(don * oh, axis=-1, keepdims=True))).astype(BF16)
        keep = jnp.where(first_tile, 0.0, 1.0)

        def chunk(jrev, decn):
            cl = nch - 1 - jrev
            rows = pl.ds(pl.multiple_of(cl * CHUNK, CHUNK), CHUNK)
            one_row = pl.ds(pl.multiple_of(cl * SUBLANES, SUBLANES), 1)
            dec = dec_ref[one_row, :]
            has_prev = jnp.where(cl > 0, 1.0, 0.0)
            prev_idx = jnp.maximum(cl - 1, 0)
            for h in range(HEADS):
                ks = slice(h * DK, (h + 1) * DK)
                vs = slice(h * DV, (h + 1) * DV)
                dob = do_ref[rows, vs]
                s_c = st_ref[cl, h]
                dqs = jnp.dot(dob, s_c.astype(BF16), preferred_element_type=F32)
                dz_ref[rows, ks] = (dqs * scale).astype(BF16)
                gt = gt_ref[h] * decn[:, ks] + lax.dot_general(
                    dob, qs_ref[rows, ks], (((0,), (0,)), ((), ())), preferred_element_type=F32)
                gt_ref[h] = gt
                gb = gt.astype(BF16)
                dkd_ref[rows, ks] = jnp.dot(v_ref[rows, vs].astype(BF16), gb, preferred_element_type=F32)
                dz_ref[rows, 1024 + h * DV:1024 + (h + 1) * DV] = lax.dot_general(
                    kdb_ref[rows, ks], gb, (((1,), (1,)), ((), ())),
                    preferred_element_type=F32).astype(BF16)
                s_prev = has_prev * st_ref[prev_idx, h] + (1.0 - has_prev) * keep * stp_ref[0, h]
                dbe_ref[one_row, ks] = jnp.sum(gt * s_prev, axis=0, keepdims=True) * dec[:, ks]
            return dec

        decn_ref[0:1, :] = lax.fori_loop(0, nch, chunk, decn_ref[0:1, :], unroll=2)

        dkd = dkd_ref[...]
        dz_ref[:, 512:1024] = (dkd * e_ref[...]).astype(BF16)
        dla_ref[...] = _tri_sum(tl_ref[...], dkd * kd_ref[...])
        for c in range(nch):
            dla_ref[c * CHUNK:(c + 1) * CHUNK, :] += dbe_ref[c * SUBLANES:c * SUBLANES + 1, :]
        dpre = dla_ref[...] * (1.0 / 16.0) * (1.0 - jnp.exp(16.0 * la))
        accb_ref[...] += _rowsum8(dpre)
        dpb = dpre.astype(BF16)
        accw_ref[...] += lax.dot_general(al_ref[...].astype(BF16), dpb, (((0,), (0,)), ((), ())),
                                         preferred_element_type=F32)
        dal_ref[...] = lax.dot_general(dpb, wgu_ref[...], (((1,), (1,)), ((), ())),
                                       preferred_element_type=F32)

        @pl.when(i == n - 1)
        def _():
            dwgu_ref[...] = accw_ref[...]
            dbg_ref[...] = jnp.sum(accb_ref[...], axis=0, keepdims=True)
            dgg_ref[...] = jnp.sum(accg_ref[...], axis=0, keepdims=True)

    def zcol(width, j):
        return pl.BlockSpec((ts, width), lambda i, j=j: (n - 1 - i, j))

    full = lambda shape: pl.BlockSpec(shape, lambda i: tuple(0 for _ in shape))
    n_in, n_out, n_scr = 16, 5, 14
    s_ins, s_outs = (len(side.ins), len(side.outs)) if side else (0, 0)

    def body_with_side(*refs):
        outs_at, scr_at = n_in + s_ins, n_in + s_ins + n_out + s_outs
        if side is not None:
            side_scr = refs[scr_at + n_scr:]

            @pl.when(pl.program_id(0) == 0)
            def _():
                side.init(side_scr)

            side.body(pl.program_id(0), refs[n_in:outs_at], refs[outs_at + n_out:scr_at], side_scr)
        body(*refs[:n_in], *refs[outs_at:outs_at + n_out], *refs[scr_at:scr_at + n_scr])

    def one_axis(spec):
        return type(spec)(spec.block_shape, lambda i, *p, f=spec.index_map: f(i, 0, 0, *p))

    return _call(
        body_with_side, grid=(n,),
        in_specs=[zcol(1024, 1), zcol(512, 6), zcol(512, 7), zcol(1024, 4), zcol(1024, 5),
                  zcol(1024, 0), zcol(512, 0),
                  pl.BlockSpec((nch, HEADS, DV, DK), lambda i: (n - 1 - i, 0, 0, 0)),
                  pl.BlockSpec((1, HEADS, DV, DK),
                               lambda i: (jnp.maximum((n - 1 - i) * nch - 1, 0), 0, 0, 0)),
                  zcol(LANES, 0), full((LANES, 512)), full((1, DV)),
                  full(tri_u.shape), full(tri_l.shape), full(ind8.shape),
                  pl.BlockSpec(memory_space=pl.ANY)] + ([one_axis(s) for s in side.in_specs] if side else []),
        out_specs=([zcol(3072, 1), zcol(LANES, 0), full((LANES, 512)), full((1, 512)), full((1, DV))]
                   + ([one_axis(s) for s in side.out_specs] if side else [])),
        out_shape=[jax.ShapeDtypeStruct((s_len, 6144), BF16), jax.ShapeDtypeStruct((s_len, LANES), F32),
                   jax.ShapeDtypeStruct((LANES, 512), F32), jax.ShapeDtypeStruct((1, 512), F32),
                   jax.ShapeDtypeStruct((1, DV), F32)] + (list(side.outs) if side else []),
        scratch_shapes=[pltpu.VMEM((HEADS, DV, DK), F32), pltpu.VMEM((SUBLANES, 512), F32),
                        pltpu.VMEM((LANES, 512), F32), pltpu.VMEM((SUBLANES, 512), F32),
                        pltpu.VMEM((SUBLANES, DV), F32), pltpu.VMEM((ts, 512), F32),
                        pltpu.VMEM((ts, 512), F32), pltpu.VMEM((ts, 512), F32), pltpu.VMEM((ts, 512), BF16),
                        pltpu.VMEM((ts, 512), BF16), pltpu.VMEM((ts, 1024), BF16),
                        pltpu.VMEM((ts, 512), F32), pltpu.VMEM((nch * SUBLANES, 512), F32),
                        pltpu.VMEM((nch * SUBLANES, 512), F32)] + (list(side.scratch) if side else []),
        aliases={15: 0},
        operands=((dy, z, z, z, z, o, la, st, st, alow, wgu, gg, tri_u, tri_l, ind8, dz_in)
                  + (tuple(side.ins) if side else ())),
        name="gla_bwd", comm=comm, place=place)


def _adamw(w, g, m, v, *, name):
    rows, cols = w.shape
    tr, tc = _tile(rows, cols)

    def body(w_ref, g_ref, m_ref, v_ref, go_ref, d_ref, nm_ref, nv_ref):
        gv = g_ref[...]
        go_ref[...] = gv
        m2 = ADAM_B1 * m_ref[...] + (1.0 - ADAM_B1) * gv
        v2 = ADAM_B2 * v_ref[...] + (1.0 - ADAM_B2) * jnp.square(gv)
        m_hat = m2 / (1.0 - ADAM_B1 ** ADAM_STEP)
        v_hat = v2 / (1.0 - ADAM_B2 ** ADAM_STEP)
        d_ref[...] = -ADAM_LR * (m_hat / (jnp.sqrt(v_hat) + ADAM_EPS) + ADAM_WD * w_ref[...])
        nm_ref[...] = m2
        nv_ref[...] = v2

    blk = pl.BlockSpec((tr, tc), lambda i, j: (i, j))
    shp = jax.ShapeDtypeStruct((rows, cols), F32)
    return pl.pallas_call(
        body, grid=(rows // tr, cols // tc), in_specs=[blk] * 4, out_specs=[blk] * 4, out_shape=[shp] * 4,
        compiler_params=_params("parallel", "parallel"), name=name,
    )(w, g, m, v)


def _my_place():
    return lax.axis_index("x"), lax.axis_index("y"), lax.axis_index("c")


def _flip(v, bit):
    return 1 - v if bit else v


def _allgather_small(buf, *, reduce, name):
    rows = buf.shape[0]

    def body(in_ref, out_ref, gat_ref, send_sems, recv_sems):
        x, y, c = _my_place()
        me = 4 * x + 2 * y + c
        gat_ref[me] = in_ref[...]
        copies = []
        for m in range(1, N_DEV):
            peer = (_flip(x, m & 4), _flip(y, m & 2), _flip(c, m & 1))
            cp = pltpu.make_async_remote_copy(
                src_ref=in_ref, dst_ref=gat_ref.at[me],
                send_sem=send_sems.at[m - 1], recv_sem=recv_sems.at[m - 1],
                device_id=peer, device_id_type=MESH)
            cp.start()
            copies.append(cp)
        for m in range(1, N_DEV):
            px, py, pc = _flip(x, m & 4), _flip(y, m & 2), _flip(c, m & 1)
            src_slot = gat_ref.at[4 * px + 2 * py + pc]
            pltpu.make_async_remote_copy(
                src_ref=src_slot, dst_ref=src_slot,
                send_sem=send_sems.at[m - 1], recv_sem=recv_sems.at[m - 1],
                device_id=(px, py, pc), device_id_type=MESH).wait_recv()
        for cp in copies:
            cp.wait_send()
        if reduce:
            tot = gat_ref[0]
            for d in range(1, N_DEV):
                tot = tot + gat_ref[d]
            out_ref[...] = tot
        else:
            out_ref[...] = gat_ref[...]

    out_shape = (rows, LANES) if reduce else (N_DEV, rows, LANES)
    return pl.pallas_call(
        body,
        in_specs=[pl.BlockSpec(memory_space=pltpu.VMEM)],
        out_specs=pl.BlockSpec(memory_space=pltpu.VMEM),
        out_shape=jax.ShapeDtypeStruct(out_shape, F32),
        scratch_shapes=[pltpu.VMEM((N_DEV, rows, LANES), F32),
                        pltpu.SemaphoreType.DMA((N_DEV - 1,)), pltpu.SemaphoreType.DMA((N_DEV - 1,))],
        compiler_params=pltpu.CompilerParams(has_side_effects=True),
        name=name,
    )(buf)


def _cast_into(shard, chip_core, *, name):
    rows, cols = shard.shape
    tr, tc = _tile(rows, cols)

    def body(cc_ref, s_ref, o_ref):
        del cc_ref
        o_ref[...] = s_ref[...].astype(BF16)

    grid_spec = pltpu.PrefetchScalarGridSpec(
        num_scalar_prefetch=1, grid=(rows // tr, cols // tc),
        in_specs=[pl.BlockSpec((tr, tc), lambda r, q, cc: (r, q))],
        out_specs=pl.BlockSpec((None, tr, tc), lambda r, q, cc: (cc[0], r, q)))
    return pl.pallas_call(
        body, grid_spec=grid_spec, out_shape=jax.ShapeDtypeStruct((N_CHIPS, rows, cols), BF16),
        compiler_params=_params("arbitrary", "arbitrary"), name=name,
    )(chip_core, shard)


def _remote(src, dst, send_sems, recv_sems, k, device):
    return pltpu.make_async_remote_copy(src_ref=src, dst_ref=dst, send_sem=send_sems.at[k],
                                        recv_sem=recv_sems.at[k], device_id=device, device_id_type=MESH)


def _col_half(ref, h, *lead, rows=None):
    hc = ref.shape[-1] // 2
    mid = (slice(None),) * (len(ref.shape) - 2 - len(lead))
    row_sel = slice(None) if rows is None else pl.ds(rows[0], rows[1])
    return ref.at[tuple(lead) + mid + (row_sel, pl.ds(h * hc, hc))]


def _gather_comm(bufs, rows=None, mid_at=0.75):
    n_w, n_m = len(bufs), len(CHIP_MASKS)
    rows = rows or [None] * n_w

    def first(c_ins, c_outs, ss, rs):
        x, y, c = _my_place()
        chip = 2 * x + y
        for w in range(n_w):
            mine = _col_half(c_outs[w], c, chip, rows=rows[w])
            for mi, (mx, my) in enumerate(CHIP_MASKS):
                _remote(mine, mine, ss, rs, w * n_m + mi, (_flip(x, mx), _flip(y, my), c)).start()

    def mid(c_ins, c_outs, ss, rs):
        x, y, c = _my_place()
        for w in range(n_w):
            for mi, (mx, my) in enumerate(CHIP_MASKS):
                k = w * n_m + mi
                px, py = _flip(x, mx), _flip(y, my)
                landed = _col_half(c_outs[w], c, 2 * px + py, rows=rows[w])
                _remote(landed, landed, ss, rs, k, (px, py, c)).wait_recv()
                _remote(landed, landed, ss, rs, n_w * n_m + k, (x, y, 1 - c)).start()

    def last(c_ins, c_outs, ss, rs):
        x, y, c = _my_place()
        chip = 2 * x + y
        for w in range(n_w):
            mine = _col_half(c_outs[w], c, chip, rows=rows[w])
            for mi, (mx, my) in enumerate(CHIP_MASKS):
                k = w * n_m + mi
                px, py = _flip(x, mx), _flip(y, my)
                theirs = _col_half(c_outs[w], 1 - c, 2 * px + py, rows=rows[w])
                _remote(theirs, theirs, ss, rs, n_w * n_m + k, (x, y, 1 - c)).wait_recv()
                _remote(mine, mine, ss, rs, k, (px, py, c)).wait_send()
                _remote(mine, mine, ss, rs, n_w * n_m + k, (x, y, 1 - c)).wait_send()

    return _Comm(ins=bufs, outs=[jax.ShapeDtypeStruct(b.shape, b.dtype) for b in bufs],
                 aliases={w: w for w in range(n_w)}, n_sems=2 * n_w * n_m, first=first, mid=mid, last=last,
                 mid_at=mid_at)


def _swap_comm(grads):
    n_w = len(grads)

    def copy(c_ins, c_outs, ss, rs, w):
        x, y, c = _my_place()
        return _remote(_col_half(c_ins[w], 1 - c), c_outs[w], ss, rs, w, (x, y, 1 - c))

    def first(c_ins, c_outs, ss, rs):
        for w in range(n_w):
            copy(c_ins, c_outs, ss, rs, w).start()

    def last(c_ins, c_outs, ss, rs):
        for w in range(n_w):
            copy(c_ins, c_outs, ss, rs, w).wait()

    return _Comm(ins=grads,
                 outs=[jax.ShapeDtypeStruct(g.shape[:-1] + (g.shape[-1] // 2,), g.dtype) for g in grads],
                 aliases={}, n_sems=n_w, first=first, last=last)


def _add_own_half(g, other, chip_core, *, name):
    n_chip, rows, hc = other.shape
    tr, tc = _tile(rows, hc)
    per_half = hc // tc

    def body(cc_ref, g_ref, o_ref, out_ref):
        del cc_ref
        out_ref[...] = (g_ref[...].astype(F32) + o_ref[...].astype(F32)).astype(BF16)

    grid_spec = pltpu.PrefetchScalarGridSpec(
        num_scalar_prefetch=1, grid=(n_chip, rows // tr, per_half),
        in_specs=[pl.BlockSpec((None, tr, tc), lambda j, r, q, cc: (j, r, cc[1] * per_half + q)),
                  pl.BlockSpec((None, tr, tc), lambda j, r, q, cc: (j, r, q))],
        out_specs=pl.BlockSpec((None, tr, tc), lambda j, r, q, cc: (j, r, q)))
    return pl.pallas_call(
        body, grid_spec=grid_spec, out_shape=jax.ShapeDtypeStruct((n_chip, rows, hc), BF16),
        compiler_params=_params("parallel", "parallel", "parallel"), name=name,
    )(chip_core, g, other)


def _add_own_half_side(gs, others, *, n_steps, step_of):
    n_chip, rows, hc = others[0].shape
    per_chip = n_steps // n_chip
    tr = rows // per_chip
    assert all(o.shape == others[0].shape for o in others) and tr * per_chip == rows and tr % 16 == 0

    def body(step, ins, outs, scr):
        del step, scr
        for q, out_ref in enumerate(outs):
            out_ref[...] = (ins[2 * q][...].astype(F32) + ins[2 * q + 1][...].astype(F32)).astype(BF16)

    def blk(own_half):
        def index(i, j, k, place):
            s = step_of(i, j, k)
            return (s // per_chip, s % per_chip, place[1] if own_half else 0)
        return _PlaceSpec((None, tr, hc), index)

    return _Side(
        ins=[a for pair in zip(gs, others) for a in pair], in_specs=[blk(True), blk(False)] * len(gs),
        outs=[jax.ShapeDtypeStruct(o.shape, BF16) for o in others], out_specs=[blk(False)] * len(gs),
        scratch=[], init=_zero_refs, body=body)


def _cast_side(shards, *, n_steps, step_of):
    def body(step, ins, outs, scr):
        del step, scr
        for in_ref, out_ref in zip(ins, outs):
            out_ref[...] = in_ref[...].astype(BF16)

    def rows(s):
        assert s.shape[0] % (16 * n_steps) == 0, s.shape
        return s.shape[0] // n_steps

    return _Side(
        ins=list(shards),
        in_specs=[pl.BlockSpec((rows(s), s.shape[1]), lambda i, j, k: (step_of(i, j, k), 0)) for s in shards],
        outs=[jax.ShapeDtypeStruct((N_CHIPS,) + s.shape, BF16) for s in shards],
        out_specs=[_PlaceSpec((None, rows(s), s.shape[1]),
                              lambda i, j, k, place: (place[0], step_of(i, j, k), 0)) for s in shards],
        scratch=[], init=_zero_refs, body=body)


def _pow2_below(n_steps, most=16):
    return min(most, 1 << (n_steps.bit_length() - 1))


def _sum_chips_side(own, landed, *, n_blocks, step_of):
    n_chip, rows, hc = own.shape
    tr = rows // n_blocks
    assert tr * n_blocks == rows and tr % 16 == 0

    def body(step, ins, outs, scr):
        del step, scr
        o_ref, l1_ref, l2_ref, l3_ref = ins
        outs[0][...] = ((o_ref[...].astype(F32) + l1_ref[...].astype(F32))
                        + l2_ref[...].astype(F32)) + l3_ref[...].astype(F32)

    def block(i, j, k):
        return jnp.minimum(step_of(i, j, k), n_blocks - 1)

    def slot(q):
        return _PlaceSpec((None, tr, hc),
                          lambda i, j, k, place, q=q: ((place[0] + q) % n_chip, block(i, j, k), 0))

    return _Side(
        ins=[own, landed, landed, landed], in_specs=[slot(0), slot(1), slot(2), slot(3)],
        outs=[jax.ShapeDtypeStruct((rows, 2 * hc), F32)],
        out_specs=[_PlaceSpec((tr, hc), lambda i, j, k, place: (block(i, j, k), place[1]))],
        scratch=[], init=_zero_refs, body=body)


def _exchange_comm(pieces):
    n_w, n_m = len(pieces), len(CHIP_MASKS)

    def copies(c_ins, c_outs, ss, rs):
        x, y, c = _my_place()
        chip = 2 * x + y
        for w in range(n_w):
            for mi, (mx, my) in enumerate(CHIP_MASKS):
                px, py = _flip(x, mx), _flip(y, my)
                send = _remote(c_ins[w].at[2 * px + py], c_outs[w].at[chip], ss, rs, w * n_m + mi, (px, py, c))
                landed = c_outs[w].at[2 * px + py]
                yield send, _remote(landed, landed, ss, rs, w * n_m + mi, (px, py, c))

    def first(c_ins, c_outs, ss, rs):
        for send, _ in copies(c_ins, c_outs, ss, rs):
            send.start()

    def last(c_ins, c_outs, ss, rs):
        for send, arrival in copies(c_ins, c_outs, ss, rs):
            arrival.wait_recv()
            send.wait_send()

    return _Comm(ins=pieces, outs=[jax.ShapeDtypeStruct(p.shape, p.dtype) for p in pieces],
                 aliases={}, n_sems=n_w * n_m, first=first, last=last)


def _sum_chips(own, landed, chip_core, *, name):
    n_chip, rows, hc = own.shape
    tr, tc = _tile(rows, hc)
    per_half = hc // tc

    def body(cc_ref, o_ref, l1_ref, l2_ref, l3_ref, out_ref):
        del cc_ref
        out_ref[...] = ((o_ref[...].astype(F32) + l1_ref[...].astype(F32))
                        + l2_ref[...].astype(F32)) + l3_ref[...].astype(F32)

    def slot(k):
        return pl.BlockSpec((None, tr, tc), lambda r, q, cc, k=k: ((cc[0] + k) % n_chip, r, q))

    grid_spec = pltpu.PrefetchScalarGridSpec(
        num_scalar_prefetch=1, grid=(rows // tr, per_half),
        in_specs=[slot(0), slot(1), slot(2), slot(3)],
        out_specs=pl.BlockSpec((tr, tc), lambda r, q, cc: (r, cc[1] * per_half + q)))
    return pl.pallas_call(
        body, grid_spec=grid_spec, out_shape=jax.ShapeDtypeStruct((rows, 2 * hc), F32),
        compiler_params=_params("arbitrary", "arbitrary"), name=name,
    )(chip_core, own, landed, landed, landed)


def _join_comm(halves):
    n_w = len(halves)

    def first(c_ins, c_outs, ss, rs):
        x, y, c = _my_place()
        for w in range(n_w):
            mine = _col_half(c_outs[w], c)
            _remote(mine, mine, ss, rs, w, (x, y, 1 - c)).start()

    def last(c_ins, c_outs, ss, rs):
        x, y, c = _my_place()
        for w in range(n_w):
            theirs = _col_half(c_outs[w], 1 - c)
            _remote(theirs, theirs, ss, rs, w, (x, y, 1 - c)).wait()

    return _Comm(ins=halves, outs=[jax.ShapeDtypeStruct(h.shape, h.dtype) for h in halves],
                 aliases={w: w for w in range(n_w)}, n_sems=n_w, first=first, last=last)


def _standalone(comm, *, name):
    def body(o_ref):
        o_ref[...] = jnp.zeros_like(o_ref)

    return _call(body, grid=(1,), in_specs=[],
                 out_specs=[pl.BlockSpec((SUBLANES, LANES), lambda i: (0, 0))],
                 out_shape=[jax.ShapeDtypeStruct((SUBLANES, LANES), F32)], operands=(), name=name,
                 comm=comm)[1:]


def _pack(pieces):
    flat, spans, off = [], [], 0
    for p in pieces:
        v = p.reshape(-1).astype(F32)
        pad = (-v.shape[0]) % LANES
        if pad:
            v = jnp.concatenate([v, jnp.zeros((pad,), F32)])
        spans.append((off, p.size))
        off += v.shape[0]
        flat.append(v)
    tail = (-off) % (SUBLANES * LANES)
    if tail:
        flat.append(jnp.zeros((tail,), F32))
    return jnp.concatenate(flat).reshape(-1, LANES), spans


def _unpack(buf, span, shape):
    off, size = span
    return buf.reshape(-1)[off:off + size].reshape(shape)


def kernel(x, norm1_g, w_in, w_gate_up, b_gate, conv_w, conv_norm_g, gla_norm_g, w_out, norm2_g, w_ff1, w_ff2, norm_f_g, loss_target, m_norm1_g, m_w_in, m_w_gate_up, m_b_gate, m_conv_w, m_conv_norm_g, m_gla_norm_g, m_w_out, m_norm2_g, m_w_ff1, m_w_ff2, m_norm_f_g, v_norm1_g, v_w_in, v_w_gate_up, v_b_gate, v_conv_w, v_conv_norm_g, v_gla_norm_g, v_w_out, v_norm2_g, v_w_ff1, v_w_ff2, v_norm_f_g):
    xs = x[0]
    target = loss_target[0]
    s_len, d = xs.shape
    d_in = w_in.shape[2] * N_CHIPS
    d_main = d_in - GATE_RANK
    d_ff = w_ff1.shape[2] * N_CHIPS
    cx, cy, cc = _my_place()
    chip = 2 * cx + cy
    chip_core = jnp.stack([chip, cc]).astype(jnp.int32)
    n_ff = d_ff // N_CHIPS
    norm_f = norm_f_g.reshape(1, d)

    n_sh = d_in // N_CHIPS
    wi_buf = _cast_into(w_in[0].T, chip_core, name="cast_w_in")

    small_w, spans_w = _pack([w_gate_up[0], conv_w[0]])
    small_all = _allgather_small(small_w, reduce=False, name="gather_small_weights")
    chips_first = [small_all[2 * j] for j in range(N_CHIPS)]
    wgu_full = jnp.concatenate(
        [_unpack(b, spans_w[0], w_gate_up.shape[1:]) for b in chips_first], axis=1)
    convw_full = jnp.concatenate(
        [_unpack(b, spans_w[1], conv_w.shape[1:]) for b in chips_first], axis=0)
    wgu_pad = jnp.concatenate(
        [wgu_full, jnp.zeros((LANES - GATE_RANK, wgu_full.shape[1]), F32)], axis=0).astype(BF16)
    convw_t = convw_full.T

    u, wo_buf, w1_buf, w2_buf, wi_buf = _rms_fwd(
        xs, norm1_g, name="norm1_fwd", comm=_gather_comm([wi_buf]), place=chip_core,
        side=_cast_side([w_out[0], w_ff1[0], w_ff2[0]], n_steps=s_len // 512, step_of=lambda i, j, k: i))
    wi_t = wi_buf.reshape(d_in, d)
    wg_t = jnp.concatenate([wi_t[d_main:], jnp.zeros((LANES - GATE_RANK, d), BF16)], axis=0)
    z, wo_buf, w1_buf = _matmul(u, wi_t, tb=True, tm=2048, tn=1024, tk=d, out_dtypes=[F32], n_dim=d_main,
                                name="in_proj",
                                comm=_gather_comm([wo_buf, w1_buf], rows=[None, (0, d // 2)]))
    wo_full = wo_buf.reshape(d, d)
    (alow,) = _matmul(u, wg_t, tb=True, tm=1024, tn=LANES, tk=d, out_dtypes=[F32], name="in_proj_gate")
    y0 = _conv_fwd(z, convw_t, conv_norm_g)
    y, o, la, st, w1_cm = _gla_fwd(z, alow, wgu_pad, b_gate, gla_norm_g, y0,
                                   comm=_gather_comm([w1_buf], rows=[(d // 2, d // 2)], mid_at=0.9))
    x2, h = _matmul(y, wo_full, tm=512, tn=d, tk=d, out_dtypes=[F32, BF16], extras=(xs, norm2_g),
                    epilogue=_residual_norm_epilogue, name="out_proj")
    a, p, w2_buf = _matmul(
        h, w1_cm, tm=2048, tn=1024, tk=d, out_dtypes=[BF16, BF16], n_dim=d_ff,
        b_spec=pl.BlockSpec((None, d, 1024), lambda i, j, k: (j // 2, 0, j % 2)),
        epilogue=lambda acc: (acc, jnp.square(jnp.maximum(acc, 0.0))), name="ff1",
        comm=_gather_comm([w2_buf]))
    w2_full = w2_buf.reshape(d_ff, d)
    (x3,) = _matmul(p, w2_full, tm=1024, tn=1024, tk=2048, out_dtypes=[F32], extras=(x2,),
                    epilogue=_add_epilogue, name="ff2")
    dx3b, g_normf, loss_part = _loss_head(x3, norm_f, target)

    (da,) = _matmul(dx3b, w2_full, tb=True, tm=2048, tn=1024, tk=d, out_dtypes=[BF16], extras=(a,),
                    epilogue=lambda acc, av: (acc * (2.0 * jnp.maximum(av, 0.0)),), name="ff2_dx")
    (dh,) = _matmul(
        da, w1_cm, tb=True, tm=2048, tn=1024, tk=2048, out_dtypes=[BF16], n_dim=d,
        b_spec=pl.BlockSpec((None, 1024, 2048), lambda i, j, k: (k, j, 0)), name="ff1_dx")
    dw_k = s_len // 2048
    g_w2, dx2, g_norm2 = _matmul(
        p, dx3b, ta=True, tm=1024, tn=d, tk=2048, out_dtypes=[BF16], name="ff2_dw",
        side=_rms_bwd_side(dh, x2, norm2_g, dx3b, n_steps=(d_ff // 1024) * dw_k,
                           block_of_step=lambda i, j, k: i * dw_k + k))
    g_w2 = g_w2.reshape(N_CHIPS, n_ff, d)
    dy, t_w2 = _matmul(dx2, wo_full, tb=True, tm=1024, tn=1024, tk=d, out_dtypes=[BF16],
                       name="out_proj_dx", comm=_swap_comm([g_w2]))
    f1_j, f1_k = d_ff // n_ff, s_len // 2048
    g_w1, dz0, g_convw_t, g_convg = _matmul(
        h, da, ta=True, tm=1024, tn=n_ff, tk=2048, out_dtypes=[BF16], name="ff1_dw",
        out_shapes=[jax.ShapeDtypeStruct((N_CHIPS, d, n_ff), BF16)],
        out_specs=[pl.BlockSpec((None, 1024, n_ff), lambda i, j, k: (j, i, 0))],
        side=_conv_bwd_side(dy, z, convw_t, conv_norm_g, n_steps=(d // 1024) * f1_j * f1_k,
                            step_of=lambda i, j, k: (i * f1_j + j) * f1_k + k))
    ow_k = s_len // 512
    g_wo, p_w2, t_w1 = _matmul(
        y, dx2, ta=True, tm=1024, tn=d, tk=512, out_dtypes=[BF16], name="out_proj_dw",
        side=_add_own_half_side([g_w2], [t_w2], n_steps=(d // 1024) * ow_k,
                                step_of=lambda i, j, k: i * ow_k + k),
        comm=_swap_comm([g_w1]), place=chip_core)
    g_wo = g_wo.reshape(N_CHIPS, d // N_CHIPS, d)
    dz, dalow, g_wgu_pad, g_bg, g_gg, p_w1, l_w2 = _gla_bwd(
        dy, z, o, la, st, alow, wgu_pad, gla_norm_g, dz0, comm=_exchange_comm([p_w2]), place=chip_core,
        side=_add_own_half_side([g_w1], [t_w1], n_steps=s_len // 512, step_of=lambda i, j, k: i))
    dwi_k = s_len // 2048
    g_wi_t, m_w2, l_w1 = _matmul(
        dz, u, ta=True, tm=1024, tn=d, tk=2048, out_dtypes=[BF16], name="in_proj_dw",
        out_shapes=[jax.ShapeDtypeStruct((d_in, d), BF16)], comm=_exchange_comm([p_w1]),
        side=_sum_chips_side(p_w2, l_w2, n_blocks=_pow2_below(d_main // 1024 * dwi_k),
                             step_of=lambda i, j, k: i * dwi_k + k),
        place=chip_core)
    g_wi_t, m_w1 = _matmul(
        dalow, u, ta=True, tm=LANES, tn=d, tk=1024, out_dtypes=[BF16],
        epilogue=lambda acc: (acc[:GATE_RANK],), into=g_wi_t, name="in_proj_gate_dw",
        out_shapes=[jax.ShapeDtypeStruct((d_in, d), BF16)],
        out_specs=[pl.BlockSpec((GATE_RANK, d), lambda i, j, k: (d_main // GATE_RANK, 0))],
        side=_sum_chips_side(p_w1, l_w1, n_blocks=_pow2_below(s_len // 1024), step_of=lambda i, j, k: k),
        place=chip_core)
    t_wi, t_wo = _standalone(_swap_comm([g_wi_t, g_wo]), name="swap_w_in_w_out")
    p_wi = _add_own_half(g_wi_t[None], t_wi[None], chip_core, name="pre_reduce_w_in")
    p_wi = p_wi.reshape(N_CHIPS, n_sh, d // 2)
    p_wo = _add_own_half(g_wo, t_wo, chip_core, name="pre_reduce_w_out")
    du, l_wi, l_wo, m_w1, m_w2 = _matmul(
        dz, wi_t, tm=2048, tn=1024, tk=2048, out_dtypes=[BF16], n_dim=d, extras=(dalow, wg_t),
        epilogue=lambda acc, dal, wg: (acc + jnp.dot(dal.astype(BF16), wg, preferred_element_type=F32),),
        name="in_proj_dx", comm=[_exchange_comm([p_wi, p_wo]), _join_comm([m_w1, m_w2])])
    m_wi = _sum_chips(p_wi, l_wi, chip_core, name="reduce_w_in")
    m_wo = _sum_chips(p_wo, l_wo, chip_core, name="reduce_w_out")
    grad_x, g_norm1 = _rms_bwd(du, xs, norm1_g, dx2, name="norm1_bwd")
    m_wi, m_wo = _standalone(_join_comm([m_wi, m_wo]), name="join_w_in_w_out")
    g_big = [m_wi, m_wo, m_w1, m_w2]

    small_g, spans_g = _pack([g_norm1, g_wgu_pad[:GATE_RANK], g_bg, g_convw_t, g_convg, g_gg, g_norm2,
                              g_normf, loss_part[:, :1]])
    tot = _allgather_small(small_g, reduce=True, name="reduce_small_grads")
    t_norm1 = _unpack(tot, spans_g[0], (1, d))
    t_wgu = _unpack(tot, spans_g[1], (GATE_RANK, HEADS * DK))
    t_bg = _unpack(tot, spans_g[2], (1, HEADS * DK))
    t_convw = _unpack(tot, spans_g[3], (3, D_CONV)).T
    t_convg = _unpack(tot, spans_g[4], (1, D_CONV))
    t_gg = _unpack(tot, spans_g[5], (1, DV))
    t_norm2 = _unpack(tot, spans_g[6], (1, d))
    t_normf = _unpack(tot, spans_g[7], (1, d))
    loss = _unpack(tot, spans_g[8], ())
    n_gu = w_gate_up.shape[2]
    n_cw = conv_w.shape[1]
    t_wgu = lax.dynamic_slice(t_wgu, (0, chip * n_gu), (GATE_RANK, n_gu))
    t_convw = lax.dynamic_slice(t_convw, (chip * n_cw, 0), (n_cw, 3))

    order = ["norm1_g", "w_in", "w_gate_up", "b_gate", "conv_w", "conv_norm_g", "gla_norm_g", "w_out",
             "norm2_g", "w_ff1", "w_ff2", "norm_f_g"]
    weights = dict(norm1_g=norm1_g, w_in=w_in, w_gate_up=w_gate_up, b_gate=b_gate, conv_w=conv_w,
                   conv_norm_g=conv_norm_g, gla_norm_g=gla_norm_g, w_out=w_out, norm2_g=norm2_g,
                   w_ff1=w_ff1, w_ff2=w_ff2, norm_f_g=norm_f_g)
    moms = dict(norm1_g=m_norm1_g, w_in=m_w_in, w_gate_up=m_w_gate_up, b_gate=m_b_gate, conv_w=m_conv_w,
                conv_norm_g=m_conv_norm_g, gla_norm_g=m_gla_norm_g, w_out=m_w_out, norm2_g=m_norm2_g,
                w_ff1=m_w_ff1, w_ff2=m_w_ff2, norm_f_g=m_norm_f_g)
    vels = dict(norm1_g=v_norm1_g, w_in=v_w_in, w_gate_up=v_w_gate_up, b_gate=v_b_gate, conv_w=v_conv_w,
                conv_norm_g=v_conv_norm_g, gla_norm_g=v_gla_norm_g, w_out=v_w_out, norm2_g=v_norm2_g,
                w_ff1=v_w_ff1, w_ff2=v_w_ff2, norm_f_g=v_norm_f_g)
    grads2d = dict(norm1_g=t_norm1, w_in=g_big[0], w_gate_up=t_wgu, b_gate=t_bg, conv_w=t_convw,
                   conv_norm_g=t_convg, gla_norm_g=t_gg, w_out=g_big[1], norm2_g=t_norm2,
                   w_ff1=g_big[2], w_ff2=g_big[3], norm_f_g=t_normf)
    out_g, out_d, out_m, out_v = [], [], [], []
    for nm in order:
        w = weights[nm]
        g2 = grads2d[nm]
        if nm == "w_in":
            to2d, back = (lambda t: t[0].T), (lambda t: t.T.reshape(w.shape))
        else:
            to2d, back = (lambda t: t.reshape(g2.shape)), (lambda t: t.reshape(w.shape))
        res = _adamw(to2d(w), g2, to2d(moms[nm]), to2d(vels[nm]), name="adamw_" + nm)
        for lst, r in zip((out_g, out_d, out_m, out_v), res):
            lst.append(back(r))
    return (loss, grad_x.reshape(x.shape), *out_g, *out_d, *out_m, *out_v)
```

```python
import functools

import jax
import jax.numpy as jnp
from jax import lax
from jax.experimental import pallas as pl
from jax.experimental.pallas import tpu as pltpu

F32 = jnp.float32
BF16 = jnp.bfloat16
MESH = pl.DeviceIdType.MESH

EPS = 1e-6
CHUNK = 64
HEADS = 4
DK = 128
DV = 256
D_CONV = 1024
GROUP = 128
GATE_RANK = 16
LANES = 128
SUBLANES = 8
N_CHIPS = 4
N_DEV = 8
CHIP_MASKS = ((1, 0), (0, 1), (1, 1))

ADAM_LR = 0.001
ADAM_B1 = 0.9
ADAM_B2 = 0.999
ADAM_EPS = 1e-08
ADAM_WD = 0.01
ADAM_STEP = 10

VMEM_LIMIT = 56 * 1024 * 1024


def _params(*sem):
    return pltpu.CompilerParams(dimension_semantics=tuple(sem), vmem_limit_bytes=VMEM_LIMIT)


def _rowsum8(v):
    r, c = v.shape
    return jnp.sum(v.reshape(r // SUBLANES, SUBLANES, c), axis=0)


def _tile(rows, cols):
    for cand in (256, 128, 64, 32, 16, 8):
        if rows % cand == 0 and cand < rows <= 64 * cand:
            return cand, cols
    if rows * cols * 4 > (2 << 20) and cols % 256 == 0:
        return rows, 256
    return rows, cols


class _Comm:
    def __init__(self, ins, outs, aliases, n_sems, first, last, mid=None, mid_at=0.75):
        self.ins = list(ins)
        self.outs = list(outs)
        self.aliases = dict(aliases)
        self.n_sems = n_sems
        self.first = first
        self.mid = mid
        self.mid_at = mid_at
        self.last = last


class _PlaceSpec:
    def __init__(self, block_shape, index_map):
        self.block_shape, self.index_map = block_shape, index_map


def _call(body, *, grid, in_specs, out_specs, out_shape, operands, name, scratch_shapes=(), sem=None,
          aliases=None, comm=None, place=None):
    aliases = dict(aliases or {})
    comms = [] if comm is None else (list(comm) if isinstance(comm, (list, tuple)) else [comm])
    n_in, n_out, n_scr = len(in_specs), len(out_specs), len(scratch_shapes)
    c_ins_all = [a for cm in comms for a in cm.ins]
    c_outs_all = [o for cm in comms for o in cm.outs]
    n_ci, n_co = len(c_ins_all), len(c_outs_all)
    n_place = 0 if place is None else 1

    def adapt(spec):
        if isinstance(spec, _PlaceSpec):
            return pl.BlockSpec(spec.block_shape, spec.index_map)
        if place is None or spec.index_map is None:
            return spec
        return pl.BlockSpec(spec.block_shape, lambda *a, f=spec.index_map: f(*a[:-1]))

    def full_body(*refs):
        refs = refs[n_place:]
        ins = refs[:n_in]
        o0 = n_in + n_ci
        outs = refs[o0:o0 + n_out]
        s0 = o0 + n_out + n_co
        scr = refs[s0:s0 + n_scr]
        sems = refs[s0 + n_scr:]
        parts, i_at, o_at = [], n_in, o0 + n_out
        for q, cm in enumerate(comms):
            parts.append((cm, refs[i_at:i_at + len(cm.ins)], refs[o_at:o_at + len(cm.outs)],
                          sems[2 * q], sems[2 * q + 1]))
            i_at += len(cm.ins)
            o_at += len(cm.outs)
        step = functools.reduce(lambda acc, ig: acc * ig[1] + pl.program_id(ig[0]), enumerate(grid), 0)
        n_steps = functools.reduce(lambda acc, g: acc * g, grid, 1)
        @pl.when(step == 0)
        def _():
            for cm, c_ins, c_outs, ss, rs in parts:
                cm.first(c_ins, c_outs, ss, rs)

        def mid_step(cm):
            ms = int(cm.mid_at * n_steps)
            return ms if 0 < ms < n_steps - 1 else None

        for cm, c_ins, c_outs, ss, rs in parts:
            if cm.mid is not None and mid_step(cm) is not None:
                @pl.when(step == mid_step(cm))
                def _(cm=cm, c_ins=c_ins, c_outs=c_outs, ss=ss, rs=rs):
                    cm.mid(c_ins, c_outs, ss, rs)

        body(*ins, *outs, *scr)

        @pl.when(step == n_steps - 1)
        def _():
            for cm, c_ins, c_outs, ss, rs in parts:
                if cm.mid is not None and mid_step(cm) is None:
                    cm.mid(c_ins, c_outs, ss, rs)
                cm.last(c_ins, c_outs, ss, rs)

    any_spec = pl.BlockSpec(memory_space=pl.ANY)
    i_at, o_at, sem_shapes = n_in, n_out, []
    for cm in comms:
        for i_in, i_out in cm.aliases.items():
            aliases[i_at + i_in] = o_at + i_out
        i_at += len(cm.ins)
        o_at += len(cm.outs)
        sem_shapes += [pltpu.SemaphoreType.DMA((cm.n_sems,)), pltpu.SemaphoreType.DMA((cm.n_sems,))]
    specs = dict(grid=grid, in_specs=[adapt(s) for s in in_specs] + [any_spec] * n_ci,
                 out_specs=[adapt(s) for s in out_specs] + [any_spec] * n_co,
                 scratch_shapes=list(scratch_shapes) + sem_shapes)
    if place is not None:
        specs = dict(grid_spec=pltpu.PrefetchScalarGridSpec(num_scalar_prefetch=1, **specs))
    if comms:
        params = pltpu.CompilerParams(dimension_semantics=("arbitrary",) * len(grid),
                                      vmem_limit_bytes=VMEM_LIMIT, has_side_effects=True)
    else:
        params = _params(*(sem or ("arbitrary",) * len(grid)))
    return pl.pallas_call(
        full_body, out_shape=list(out_shape) + c_outs_all,
        input_output_aliases={k + n_place: v for k, v in aliases.items()},
        compiler_params=params, name=name, **specs,
    )(*(() if place is None else (place,)), *operands, *c_ins_all)


def _matmul(a, b, *, ta=False, tb=False, tm, tn, tk, out_dtypes, name, extras=(), epilogue=None,
            out_shapes=None, out_specs=None, b_spec=None, n_dim=None, into=None, side=None, comm=None,
            place=None):
    n_into = 0 if into is None else 1
    if ta:
        k_dim, m_dim = a.shape
    else:
        m_dim, k_dim = a.shape
    if n_dim is None:
        n_dim = b.shape[0] if tb else b.shape[1]
        assert (b.shape[1] if tb else b.shape[0]) == k_dim
    assert m_dim % tm == 0 and n_dim % tn == 0 and k_dim % tk == 0, (name, a.shape, b.shape)
    nk = k_dim // tk
    n_ex, n_out = len(extras), len(out_dtypes)
    dims = (((0 if ta else 1,), (1 if tb else 0,)), ((), ()))
    grid = (m_dim // tm, n_dim // tn, nk)
    s_ins, s_outs, s_scr = (len(side.ins), len(side.outs), len(side.scratch)) if side else (0, 0, 0)

    def body(*refs):
        a_ref, b_ref = refs[0], refs[1]
        ex_refs = refs[2:2 + n_ex]
        i0 = 2 + n_ex + n_into
        side_in = refs[i0:i0 + s_ins]
        o_refs = refs[i0 + s_ins:i0 + s_ins + n_out]
        side_out = refs[i0 + s_ins + n_out:i0 + s_ins + n_out + s_outs]
        side_scr = refs[len(refs) - s_scr:] if s_scr else ()
        if side is not None:
            step = (pl.program_id(0) * grid[1] + pl.program_id(1)) * grid[2] + pl.program_id(2)

            @pl.when(step == 0)
            def _():
                side.init(side_scr)

        def dot():
            if side is not None:
                side.body(step, side_in, side_out, side_scr)
            return lax.dot_general(a_ref[...].astype(BF16), b_ref[...].astype(BF16), dims,
                                   preferred_element_type=F32)

        def finish(acc):
            outs = epilogue(acc, *[e[...] for e in ex_refs]) if epilogue is not None else (acc,)
            for o_ref, o in zip(o_refs, outs):
                o_ref[...] = o.astype(o_ref.dtype)

        if nk == 1:
            finish(dot())
        else:
            acc_ref = refs[len(refs) - s_scr - 1]
            k = pl.program_id(2)

            @pl.when(k == 0)
            def _():
                acc_ref[...] = dot()

            @pl.when(jnp.logical_and(k > 0, k < nk - 1))
            def _():
                acc_ref[...] += dot()

            @pl.when(k == nk - 1)
            def _():
                finish(acc_ref[...] + dot())

    a_spec = (pl.BlockSpec((tk, tm), lambda i, j, k: (k, i)) if ta
              else pl.BlockSpec((tm, tk), lambda i, j, k: (i, k)))
    if b_spec is None:
        b_spec = (pl.BlockSpec((tn, tk), lambda i, j, k: (j, k)) if tb
                  else pl.BlockSpec((tk, tn), lambda i, j, k: (k, j)))
    io_spec = pl.BlockSpec((tm, tn), lambda i, j, k: (i, j))

    def extra_spec(e):
        if e.shape == (m_dim, n_dim):
            return io_spec
        if e.shape[1] == n_dim:
            return pl.BlockSpec((e.shape[0], tn), lambda i, j, k: (0, j))
        assert e.shape[0] == m_dim, (name, e.shape)
        return pl.BlockSpec((tm, e.shape[1]), lambda i, j, k: (i, 0))

    if out_shapes is None:
        out_shapes = [jax.ShapeDtypeStruct((m_dim, n_dim), dt) for dt in out_dtypes]
    if out_specs is None:
        out_specs = [io_spec] * n_out
    return _call(
        body,
        grid=grid,
        in_specs=([a_spec, b_spec] + [extra_spec(e) for e in extras]
                  + [pl.BlockSpec(memory_space=pl.ANY)] * n_into
                  + (list(side.in_specs) if side else [])),
        out_specs=list(out_specs) + (list(side.out_specs) if side else []),
        out_shape=list(out_shapes) + (list(side.outs) if side else []),
        scratch_shapes=([pltpu.VMEM((tm, tn), F32)] if nk > 1 else []) + (list(side.scratch) if side else []),
        sem=("parallel", "parallel", "arbitrary") if side is None else None,
        aliases={2 + n_ex: 0} if n_into else None,
        operands=(a, b, *extras) + ((into,) if n_into else ()) + (tuple(side.ins) if side else ()),
        name=name, comm=comm, place=place)


def _add_epilogue(acc, r):
    return (acc + r,)


def _residual_norm_epilogue(acc, x, g):
    x2 = acc + x
    r = lax.rsqrt(jnp.mean(x2 * x2, axis=-1, keepdims=True) + EPS)
    return x2, x2 * r * g


def _rms_fwd(x, g, *, name, tm=512, comm=None, side=None, place=None):
    s_len, d = x.shape
    s_ins, s_outs = (len(side.ins), len(side.outs)) if side else (0, 0)

    def body(*refs):
        x_ref, g_ref = refs[0], refs[1]
        o_ref = refs[2 + s_ins]
        xv = x_ref[...]
        r = lax.rsqrt(jnp.mean(xv * xv, axis=-1, keepdims=True) + EPS)
        o_ref[...] = (xv * r * g_ref[...]).astype(o_ref.dtype)
        if side is not None:
            scr = refs[3 + s_ins + s_outs:]

            @pl.when(pl.program_id(0) == 0)
            def _():
                side.init(scr)

            side.body(pl.program_id(0), refs[2:2 + s_ins], refs[3 + s_ins:3 + s_ins + s_outs], scr)

    row = pl.BlockSpec((tm, d), lambda i, j, k: (i, 0))
    return _call(
        body, grid=(s_len // tm, 1, 1),
        in_specs=[row, pl.BlockSpec((1, d), lambda i, j, k: (0, 0))] + (list(side.in_specs) if side else []),
        out_specs=[row] + (list(side.out_specs) if side else []),
        out_shape=[jax.ShapeDtypeStruct((s_len, d), BF16)] + (list(side.outs) if side else []),
        scratch_shapes=list(side.scratch) if side else [],
        operands=(x, g) + (tuple(side.ins) if side else ()), name=name, comm=comm, place=place)


NORM_ROWS = 16


def _zero_refs(refs):
    for r in refs:
        r[...] = jnp.zeros_like(r)


def _rms_bwd_block(dn_ref, x_ref, g_ref, res_ref, dx_ref, dg_ref, acc_ref):
    gv = g_ref[...]
    acc = acc_ref[...]
    for s in range(x_ref.shape[0] // NORM_ROWS):
        sl = slice(s * NORM_ROWS, (s + 1) * NORM_ROWS)
        xv = x_ref[sl, :]
        dnv = dn_ref[sl, :].astype(F32)
        r = lax.rsqrt(jnp.mean(xv * xv, axis=-1, keepdims=True) + EPS)
        xh = xv * r
        acc = acc + _rowsum8(dnv * xh)
        dxh = dnv * gv
        dx_ref[sl, :] = (r * (dxh - xh * jnp.mean(dxh * xh, axis=-1, keepdims=True))
                         + res_ref[sl, :].astype(F32))
    acc_ref[...] = acc
    dg_ref[...] = jnp.sum(acc, axis=0, keepdims=True)


class _Side:
    def __init__(self, ins, in_specs, outs, out_specs, scratch, init, body):
        self.ins, self.in_specs, self.outs, self.out_specs = ins, in_specs, outs, out_specs
        self.scratch = scratch
        self.init = init
        self.body = body


def _rms_bwd_side(dn, x, g, res, *, block_of_step, n_steps):
    s_len, d = x.shape
    row = pl.BlockSpec((s_len // n_steps, d), lambda i, j, k: (block_of_step(i, j, k), 0))
    vec = pl.BlockSpec((1, d), lambda i, j, k: (0, 0))
    return _Side(
        ins=[dn, x, g, res], in_specs=[row, row, vec, row],
        outs=[jax.ShapeDtypeStruct((s_len, d), F32), jax.ShapeDtypeStruct((1, d), F32)],
        out_specs=[row, vec], scratch=[pltpu.VMEM((SUBLANES, d), F32)],
        init=_zero_refs, body=lambda step, ins, outs, scr: _rms_bwd_block(*ins, *outs, *scr))


def _rms_bwd(dn, x, g, res, *, name, tm=512, comm=None):
    s_len, d = x.shape
    n = s_len // tm

    def body(*refs):
        @pl.when(pl.program_id(0) == 0)
        def _():
            _zero_refs(refs[-1:])

        _rms_bwd_block(*refs)

    row = pl.BlockSpec((tm, d), lambda i: (i, 0))
    vec = pl.BlockSpec((1, d), lambda i: (0, 0))
    return _call(
        body, grid=(n,),
        in_specs=[row, row, vec, row],
        out_specs=[row, vec],
        out_shape=[jax.ShapeDtypeStruct((s_len, d), F32), jax.ShapeDtypeStruct((1, d), F32)],
        scratch_shapes=[pltpu.VMEM((SUBLANES, d), F32)],
        operands=(dn, x, g, res), name=name, comm=comm)


def _loss_head(x3, g, target, *, tm=512):
    s_len, d = x3.shape
    n = s_len // tm

    def body(x_ref, g_ref, t_ref, dxb_ref, dg_ref, loss_ref, accg_ref, accl_ref):
        i = pl.program_id(0)
        xv = x_ref[...]
        gv = g_ref[...]
        r = lax.rsqrt(jnp.mean(xv * xv, axis=-1, keepdims=True) + EPS)
        xh = xv * r
        err = xh * gv - t_ref[...]

        @pl.when(i == 0)
        def _():
            accg_ref[...] = jnp.zeros_like(accg_ref)
            accl_ref[...] = jnp.zeros_like(accl_ref)

        accl_ref[...] += _rowsum8(err * err)
        dn = err * (1.0 / d)
        accg_ref[...] += _rowsum8(dn * xh)
        dxh = dn * gv
        dx = r * (dxh - xh * jnp.mean(dxh * xh, axis=-1, keepdims=True))
        dxb_ref[...] = dx.astype(BF16)

        @pl.when(i == n - 1)
        def _():
            dg_ref[...] = jnp.sum(accg_ref[...], axis=0, keepdims=True)
            tot = jnp.sum(jnp.sum(accl_ref[...], axis=0, keepdims=True), axis=1, keepdims=True)
            loss_ref[...] = jnp.broadcast_to(tot * (0.5 / d), (1, LANES))

    row = pl.BlockSpec((tm, d), lambda i: (i, 0))
    vec = pl.BlockSpec((1, d), lambda i: (0, 0))
    return pl.pallas_call(
        body, grid=(n,),
        in_specs=[row, vec, row],
        out_specs=[row, vec, pl.BlockSpec((1, LANES), lambda i: (0, 0))],
        out_shape=[jax.ShapeDtypeStruct((s_len, d), BF16),
                   jax.ShapeDtypeStruct((1, d), F32), jax.ShapeDtypeStruct((1, LANES), F32)],
        scratch_shapes=[pltpu.VMEM((SUBLANES, d), F32), pltpu.VMEM((SUBLANES, d), F32)],
        compiler_params=_params("arbitrary"), name="loss_head",
    )(x3, g, target)


def _shift_down(v, k, rows_before, row):
    out = pltpu.roll(v, k, axis=0)
    for j in range(k):
        out = jnp.where(row == j, rows_before[j], out)
    return out


def _shift_up(v, k, rows_after, row):
    t = v.shape[0]
    out = pltpu.roll(v, t - k, axis=0)
    for j in range(k):
        out = jnp.where(row == t - k + j, rows_after[j], out)
    return out


def _conv_fwd(z, w_t, gain, *, ts=512):
    s_len = z.shape[0]
    n_grp = D_CONV // GROUP

    def body(cb_ref, cc_ref, ch_ref, w_ref, g_ref, y_ref, carry_ref):
        i = pl.program_id(0)

        @pl.when(i == 0)
        def _():
            carry_ref[...] = jnp.zeros_like(carry_ref)

        row = lax.broadcasted_iota(jnp.int32, (ts, GROUP), 0)
        for g in range(n_grp):
            sl = slice(g * GROUP, (g + 1) * GROUP)
            uu = cc_ref[:, sl] * ch_ref[:, sl]
            p2 = carry_ref[6:7, sl]
            p1 = carry_ref[7:8, sl]
            u1 = _shift_down(uu, 1, [p1], row)
            u2 = _shift_down(uu, 2, [p2, p1], row)
            conv = w_ref[0:1, sl] * u2 + w_ref[1:2, sl] * u1 + w_ref[2:3, sl] * uu
            y = cb_ref[:, sl] * conv
            carry_ref[:, sl] = uu[ts - SUBLANES:ts, :]
            rg = lax.rsqrt(jnp.mean(y * y, axis=-1, keepdims=True) + EPS)
            y_ref[:, sl] = (y * rg * g_ref[:, sl]).astype(BF16)

    def col(j):
        return pl.BlockSpec((ts, D_CONV), lambda i, j=j: (i, j))

    small = lambda r: pl.BlockSpec((r, D_CONV), lambda i: (0, 0))
    return pl.pallas_call(
        body, grid=(s_len // ts,),
        in_specs=[col(0), col(1), col(2), small(3), small(1)],
        out_specs=col(0),
        out_shape=jax.ShapeDtypeStruct((s_len, 2 * D_CONV), BF16),
        scratch_shapes=[pltpu.VMEM((SUBLANES, D_CONV), F32)],
        compiler_params=_params("arbitrary"), name="conv_fwd",
    )(z, z, z, w_t, gain)


def _conv_bwd_side(dy, z, w_t, gain, *, n_steps, step_of):
    s_len = z.shape[0]
    n = n_steps
    ts = s_len // n
    n_grp = D_CONV // GROUP
    halo_blocks = ts // SUBLANES

    def body(step, ins, outs, scr):
        dy_ref, cb_ref, cc_ref, ch_ref, hcc_ref, hch_ref, w_ref, g_ref = ins
        dz_ref, dw_ref, dg_ref = outs
        carry_ref, accw_ref, accg_ref = scr
        first_tile = (n - 1 - step) == 0
        row = lax.broadcasted_iota(jnp.int32, (ts, GROUP), 0)
        keep = jnp.where(first_tile, 0.0, 1.0)
        for g in range(n_grp):
            sl = slice(g * GROUP, (g + 1) * GROUP)
            cc = cc_ref[:, sl]
            ch = ch_ref[:, sl]
            cb = cb_ref[:, sl]
            uu = cc * ch
            p2 = hcc_ref[6:7, sl] * hch_ref[6:7, sl] * keep
            p1 = hcc_ref[7:8, sl] * hch_ref[7:8, sl] * keep
            u1 = _shift_down(uu, 1, [p1], row)
            u2 = _shift_down(uu, 2, [p2, p1], row)
            w0, w1, w2 = w_ref[0:1, sl], w_ref[1:2, sl], w_ref[2:3, sl]
            conv = w0 * u2 + w1 * u1 + w2 * uu
            y = cb * conv
            rg = lax.rsqrt(jnp.mean(y * y, axis=-1, keepdims=True) + EPS)
            yh = y * rg
            dyv = dy_ref[:, sl].astype(F32)
            accg_ref[:, sl] += _rowsum8(dyv * yh)
            dyn = dyv * g_ref[:, sl]
            dpre = rg * (dyn - yh * jnp.mean(dyn * yh, axis=-1, keepdims=True))
            dz_ref[:, sl] = (dpre * conv).astype(BF16)
            dconv = dpre * cb
            accw_ref[0:8, sl] += _rowsum8(dconv * u2)
            accw_ref[8:16, sl] += _rowsum8(dconv * u1)
            accw_ref[16:24, sl] += _rowsum8(dconv * uu)
            n0 = carry_ref[0:1, sl]
            n1 = carry_ref[1:2, sl]
            d1 = _shift_up(dconv, 1, [n0], row)
            d2 = _shift_up(dconv, 2, [n0, n1], row)
            duu = w2 * dconv + w1 * d1 + w0 * d2
            carry_ref[:, sl] = dconv[0:SUBLANES, :]
            dz_ref[:, D_CONV + g * GROUP:D_CONV + (g + 1) * GROUP] = (duu * ch).astype(BF16)
            dz_ref[:, 2 * D_CONV + g * GROUP:2 * D_CONV + (g + 1) * GROUP] = (duu * cc).astype(BF16)

        for k in range(3):
            dw_ref[k:k + 1, :] = jnp.sum(accw_ref[8 * k:8 * k + 8, :], axis=0, keepdims=True)
        dg_ref[...] = jnp.sum(accg_ref[...], axis=0, keepdims=True)

    def tile(i, j, k):
        return n - 1 - step_of(i, j, k)

    def col(c):
        return pl.BlockSpec((ts, D_CONV), lambda i, j, k, c=c: (tile(i, j, k), c))

    def halo(c):
        return pl.BlockSpec((SUBLANES, D_CONV),
                            lambda i, j, k, c=c: (jnp.maximum(tile(i, j, k) * halo_blocks - 1, 0), c))

    small = lambda r: pl.BlockSpec((r, D_CONV), lambda i, j, k: (0, 0))
    return _Side(
        ins=[dy, z, z, z, z, z, w_t, gain],
        in_specs=[col(0), col(0), col(1), col(2), halo(1), halo(2), small(3), small(1)],
        outs=[jax.ShapeDtypeStruct((s_len, 6 * D_CONV), BF16),
              jax.ShapeDtypeStruct((3, D_CONV), F32), jax.ShapeDtypeStruct((1, D_CONV), F32)],
        out_specs=[pl.BlockSpec((ts, 3 * D_CONV), lambda i, j, k: (tile(i, j, k), 0)), small(3), small(1)],
        scratch=[pltpu.VMEM((SUBLANES, D_CONV), F32), pltpu.VMEM((24, D_CONV), F32),
                 pltpu.VMEM((SUBLANES, D_CONV), F32)],
        init=_zero_refs, body=body)


def _split3(v):
    hi = v.astype(BF16)
    r1 = v - hi.astype(F32)
    mid = r1.astype(BF16)
    lo = (r1 - mid.astype(F32)).astype(BF16)
    return jnp.concatenate([hi, mid, lo], axis=1)


def _tri_sum(tri, v):
    w = v.shape[1]
    dd = jnp.dot(tri, _split3(v), preferred_element_type=F32)
    return dd[:, :w] + dd[:, w:2 * w] + dd[:, 2 * w:]


def _chunk_masks(ts):
    r = jnp.arange(ts)
    same = (r[:, None] // CHUNK) == (r[None, :] // CHUNK)
    later = jnp.logical_and(same, r[None, :] > r[:, None]).astype(BF16)
    earlier = jnp.logical_and(same, r[None, :] < r[:, None]).astype(BF16)
    chunk_of_row = jnp.arange(ts // CHUNK * SUBLANES)[:, None] // SUBLANES
    member = (chunk_of_row == (r[None, :] // CHUNK)).astype(BF16)
    return later, earlier, member


def _sigmoid(v):
    return 0.5 * jnp.tanh(0.5 * v) + 0.5


def _gla_fwd(z, alow, wgu, bg, gg, y_in, *, ts=512, comm=None):
    s_len = z.shape[0]
    nch = ts // CHUNK
    scale = DK ** -0.5

    tri_u, _, ind8 = _chunk_masks(ts)

    def body(q_ref, k_ref, v_ref, og_ref, al_ref, wgu_ref, bg_ref, gg_ref, tu_ref, ind_ref, yin_ref,
             y_ref, o_ref, la_ref, st_ref, state_ref, kd_ref, qs_ref, dec_ref):
        del yin_ref
        i = pl.program_id(0)

        @pl.when(i == 0)
        def _():
            state_ref[...] = jnp.zeros_like(state_ref)

        pre = jnp.dot(al_ref[...].astype(BF16), wgu_ref[...], preferred_element_type=F32) + bg_ref[...]
        la = (jnp.minimum(pre, 0.0) - jnp.log(1.0 + jnp.exp(-jnp.abs(pre)))) * (1.0 / 16.0)
        la_ref[...] = la
        kd_ref[...] = (k_ref[...] * jnp.exp(_tri_sum(tu_ref[...], la))).astype(BF16)
        qs_ref[...] = (q_ref[...] * scale).astype(BF16)
        dec_ref[...] = jnp.exp(_tri_sum(ind_ref[...], la))

        def chunk(cl, carry):
            rows = pl.ds(pl.multiple_of(cl * CHUNK, CHUNK), CHUNK)
            dec = dec_ref[pl.ds(pl.multiple_of(cl * SUBLANES, SUBLANES), 1), :]
            for h in range(HEADS):
                ks = slice(h * DK, (h + 1) * DK)
                vs = slice(h * DV, (h + 1) * DV)
                kv_t = lax.dot_general(v_ref[rows, vs].astype(BF16), kd_ref[rows, ks],
                                       (((0,), (0,)), ((), ())), preferred_element_type=F32)
                st = state_ref[h] * dec[:, ks] + kv_t
                state_ref[h] = st
                st_ref[cl, h] = st
                o_ref[rows, vs] = lax.dot_general(qs_ref[rows, ks], st.astype(BF16),
                                                  (((1,), (1,)), ((), ())), preferred_element_type=F32)
            return carry

        lax.fori_loop(0, nch, chunk, 0, unroll=2)

        ggv = gg_ref[...]
        for h in range(HEADS):
            vs = slice(h * DV, (h + 1) * DV)
            o_h = o_ref[:, vs]
            og_h = og_ref[:, vs]
            ro = lax.rsqrt(jnp.mean(o_h * o_h, axis=-1, keepdims=True) + EPS)
            y_ref[:, vs] = (o_h * ro * ggv * (og_h * _sigmoid(og_h))).astype(BF16)

    def zcol(width, j):
        return pl.BlockSpec((ts, width), lambda i, j=j: (i, j))

    full = lambda shape: pl.BlockSpec(shape, lambda i: tuple(0 for _ in shape))
    return _call(
        body, grid=(s_len // ts,),
        in_specs=[zcol(512, 6), zcol(512, 7), zcol(1024, 4), zcol(1024, 5), zcol(LANES, 0),
                  full((LANES, 512)), full((1, 512)), full((1, DV)), full(tri_u.shape), full(ind8.shape),
                  pl.BlockSpec(memory_space=pl.ANY)],
        out_specs=[zcol(1024, 1), zcol(1024, 0), zcol(512, 0),
                   pl.BlockSpec((nch, HEADS, DV, DK), lambda i: (i, 0, 0, 0))],
        out_shape=[jax.ShapeDtypeStruct((s_len, 2048), BF16), jax.ShapeDtypeStruct((s_len, 1024), F32),
                   jax.ShapeDtypeStruct((s_len, 512), F32),
                   jax.ShapeDtypeStruct((s_len // CHUNK, HEADS, DV, DK), F32)],
        scratch_shapes=[pltpu.VMEM((HEADS, DV, DK), F32), pltpu.VMEM((ts, 512), BF16),
                        pltpu.VMEM((ts, 512), BF16), pltpu.VMEM((nch * SUBLANES, 512), F32)],
        aliases={10: 0},
        operands=(z, z, z, z, alow, wgu, bg, gg, tri_u, ind8, y_in), name="gla_fwd", comm=comm)


def _gla_bwd(dy, z, o, la, st, alow, wgu, gg, dz_in, *, ts=512, comm=None, side=None, place=None):
    s_len = z.shape[0]
    n = s_len // ts
    nch = ts // CHUNK
    scale = DK ** -0.5

    tri_u, tri_l, ind8 = _chunk_masks(ts)

    def body(dy_ref, q_ref, k_ref, v_ref, og_ref, o_ref, la_ref, st_ref, stp_ref, al_ref, wgu_ref,
             gg_ref, tu_ref, tl_ref, ind_ref, dzin_ref, dz_ref, dal_ref, dwgu_ref, dbg_ref, dgg_ref,
             gt_ref, decn_ref, accw_ref, accb_ref, accg_ref, dla_ref,
             e_ref, kd_ref, kdb_ref, qs_ref, do_ref, dkd_ref, dec_ref, dbe_ref):
        del dzin_ref
        i = pl.program_id(0)
        first_tile = (n - 1 - i) == 0

        @pl.when(i == 0)
        def _():
            gt_ref[...] = jnp.zeros_like(gt_ref)
            decn_ref[...] = jnp.ones_like(decn_ref)
            accw_ref[...] = jnp.zeros_like(accw_ref)
            accb_ref[...] = jnp.zeros_like(accb_ref)
            accg_ref[...] = jnp.zeros_like(accg_ref)

        la = la_ref[...]
        e_dec = jnp.exp(_tri_sum(tu_ref[...], la))
        e_ref[...] = e_dec
        kd = k_ref[...] * e_dec
        kd_ref[...] = kd
        kdb_ref[...] = kd.astype(BF16)
        qs_ref[...] = (q_ref[...] * scale).astype(BF16)
        dec_ref[...] = jnp.exp(_tri_sum(ind_ref[...], la))
        ggv = gg_ref[...]
        for h in range(HEADS):
            vs = slice(h * DV, (h + 1) * DV)
            o_h = o_ref[:, vs]
            og_h = og_ref[:, vs]
            dy_h = dy_ref[:, vs].astype(F32)
            ro = lax.rsqrt(jnp.mean(o_h * o_h, axis=-1, keepdims=True) + EPS)
            oh = o_h * ro
            sig = _sigmoid(og_h)
            sil = og_h * sig
            dyo = dy_h * oh
            dyg = dy_h * ggv
            accg_ref[...] += _rowsum8(dyo * sil)
            dz_ref[:, 2048 + h * DV:2048 + (h + 1) * DV] = (
                dyo * ggv * (sig * (1.0 + og_h - sil))).astype(BF16)
            don = dyg * sil
            do_ref[:, vs] = (ro * (don - oh * jnp.mean(don * oh, axis=-1, keepdims=True))).astype(BF16)
        keep = jnp.where(first_tile, 0.0, 1.0)

        def chunk(jrev, decn):
            cl = nch - 1 - jrev
            rows = pl.ds(pl.multiple_of(cl * CHUNK, CHUNK), CHUNK)
            one_row = pl.ds(pl.multiple_of(cl * SUBLANES, SUBLANES), 1)
            dec = dec_ref[one_row, :]
            has_prev = jnp.where(cl > 0, 1.0, 0.0)
            prev_idx = jnp.maximum(cl - 1, 0)
            for h in range(HEADS):
                ks = slice(h * DK, (h + 1) * DK)
                vs = slice(h * DV, (h + 1) * DV)
                dob = do_ref[rows, vs]
                s_c = st_ref[cl, h]
                dqs = jnp.dot(dob, s_c.astype(BF16), preferred_element_type=F32)
                dz_ref[rows, ks] = (dqs * scale).astype(BF16)
                gt = gt_ref[h] * decn[:, ks] + lax.dot_general(
                    dob, qs_ref[rows, ks], (((0,), (0,)), ((), ())), preferred_element_type=F32)
                gt_ref[h] = gt
                gb = gt.astype(BF16)
                dkd_ref[rows, ks] = jnp.dot(v_ref[rows, vs].astype(BF16), gb, preferred_element_type=F32)
                dz_ref[rows, 1024 + h * DV:1024 + (h + 1) * DV] = lax.dot_general(
                    kdb_ref[rows, ks], gb, (((1,), (1,)), ((), ())),
                    preferred_element_type=F32).astype(BF16)
                s_prev = has_prev * st_ref[prev_idx, h] + (1.0 - has_prev) * keep * stp_ref[0, h]
                dbe_ref[one_row, ks] = jnp.sum(gt * s_prev, axis=0, keepdims=True) * dec[:, ks]
            return dec

        decn_ref[0:1, :] = lax.fori_loop(0, nch, chunk, decn_ref[0:1, :], unroll=2)

        dkd = dkd_ref[...]
        dz_ref[:, 512:1024] = (dkd * e_ref[...]).astype(BF16)
        dla_ref[...] = _tri_sum(tl_ref[...], dkd * kd_ref[...])
        for c in range(nch):
            dla_ref[c * CHUNK:(c + 1) * CHUNK, :] += dbe_ref[c * SUBLANES:c * SUBLANES + 1, :]
        dpre = dla_ref[...] * (1.0 / 16.0) * (1.0 - jnp.exp(16.0 * la))
        accb_ref[...] += _rowsum8(dpre)
        dpb = dpre.astype(BF16)
        accw_ref[...] += lax.dot_general(al_ref[...].astype(BF16), dpb, (((0,), (0,)), ((), ())),
                                         preferred_element_type=F32)
        dal_ref[...] = lax.dot_general(dpb, wgu_ref[...], (((1,), (1,)), ((), ())),
                                       preferred_element_type=F32)

        @pl.when(i == n - 1)
        def _():
            dwgu_ref[...] = accw_ref[...]
            dbg_ref[...] = jnp.sum(accb_ref[...], axis=0, keepdims=True)
            dgg_ref[...] = jnp.sum(accg_ref[...], axis=0, keepdims=True)

    def zcol(width, j):
        return pl.BlockSpec((ts, width), lambda i, j=j: (n - 1 - i, j))

    full = lambda shape: pl.BlockSpec(shape, lambda i: tuple(0 for _ in shape))
    n_in, n_out, n_scr = 16, 5, 14
    s_ins, s_outs = (len(side.ins), len(side.outs)) if side else (0, 0)

    def body_with_side(*refs):
        outs_at, scr_at = n_in + s_ins, n_in + s_ins + n_out + s_outs
        if side is not None:
            side_scr = refs[scr_at + n_scr:]

            @pl.when(pl.program_id(0) == 0)
            def _():
                side.init(side_scr)

            side.body(pl.program_id(0), refs[n_in:outs_at], refs[outs_at + n_out:scr_at], side_scr)
        body(*refs[:n_in], *refs[outs_at:outs_at + n_out], *refs[scr_at:scr_at + n_scr])

    def one_axis(spec):
        return type(spec)(spec.block_shape, lambda i, *p, f=spec.index_map: f(i, 0, 0, *p))

    return _call(
        body_with_side, grid=(n,),
        in_specs=[zcol(1024, 1), zcol(512, 6), zcol(512, 7), zcol(1024, 4), zcol(1024, 5),
                  zcol(1024, 0), zcol(512, 0),
                  pl.BlockSpec((nch, HEADS, DV, DK), lambda i: (n - 1 - i, 0, 0, 0)),
                  pl.BlockSpec((1, HEADS, DV, DK),
                               lambda i: (jnp.maximum((n - 1 - i) * nch - 1, 0), 0, 0, 0)),
                  zcol(LANES, 0), full((LANES, 512)), full((1, DV)),
                  full(tri_u.shape), full(tri_l.shape), full(ind8.shape),
                  pl.BlockSpec(memory_space=pl.ANY)] + ([one_axis(s) for s in side.in_specs] if side else []),
        out_specs=([zcol(3072, 1), zcol(LANES, 0), full((LANES, 512)), full((1, 512)), full((1, DV))]
                   + ([one_axis(s) for s in side.out_specs] if side else [])),
        out_shape=[jax.ShapeDtypeStruct((s_len, 6144), BF16), jax.ShapeDtypeStruct((s_len, LANES), F32),
                   jax.ShapeDtypeStruct((LANES, 512), F32), jax.ShapeDtypeStruct((1, 512), F32),
                   jax.ShapeDtypeStruct((1, DV), F32)] + (list(side.outs) if side else []),
        scratch_shapes=[pltpu.VMEM((HEADS, DV, DK), F32), pltpu.VMEM((SUBLANES, 512), F32),
                        pltpu.VMEM((LANES, 512), F32), pltpu.VMEM((SUBLANES, 512), F32),
                        pltpu.VMEM((SUBLANES, DV), F32), pltpu.VMEM((ts, 512), F32),
                        pltpu.VMEM((ts, 512), F32), pltpu.VMEM((ts, 512), F32), pltpu.VMEM((ts, 512), BF16),
                        pltpu.VMEM((ts, 512), BF16), pltpu.VMEM((ts, 1024), BF16),
                        pltpu.VMEM((ts, 512), F32), pltpu.VMEM((nch * SUBLANES, 512), F32),
                        pltpu.VMEM((nch * SUBLANES, 512), F32)] + (list(side.scratch) if side else []),
        aliases={15: 0},
        operands=((dy, z, z, z, z, o, la, st, st, alow, wgu, gg, tri_u, tri_l, ind8, dz_in)
                  + (tuple(side.ins) if side else ())),
        name="gla_bwd", comm=comm, place=place)


def _adamw(w, g, m, v, *, name):
    rows, cols = w.shape
    tr, tc = _tile(rows, cols)

    def body(w_ref, g_ref, m_ref, v_ref, go_ref, d_ref, nm_ref, nv_ref):
        gv = g_ref[...]
        go_ref[...] = gv
        m2 = ADAM_B1 * m_ref[...] + (1.0 - ADAM_B1) * gv
        v2 = ADAM_B2 * v_ref[...] + (1.0 - ADAM_B2) * jnp.square(gv)
        m_hat = m2 / (1.0 - ADAM_B1 ** ADAM_STEP)
        v_hat = v2 / (1.0 - ADAM_B2 ** ADAM_STEP)
        d_ref[...] = -ADAM_LR * (m_hat / (jnp.sqrt(v_hat) + ADAM_EPS) + ADAM_WD * w_ref[...])
        nm_ref[...] = m2
        nv_ref[...] = v2

    blk = pl.BlockSpec((tr, tc), lambda i, j: (i, j))
    shp = jax.ShapeDtypeStruct((rows, cols), F32)
    return pl.pallas_call(
        body, grid=(rows // tr, cols // tc), in_specs=[blk] * 4, out_specs=[blk] * 4, out_shape=[shp] * 4,
        compiler_params=_params("parallel", "parallel"), name=name,
    )(w, g, m, v)


def _my_place():
    return lax.axis_index("x"), lax.axis_index("y"), lax.axis_index("c")


def _flip(v, bit):
    return 1 - v if bit else v


def _allgather_small(buf, *, reduce, name):
    rows = buf.shape[0]

    def body(in_ref, out_ref, gat_ref, send_sems, recv_sems):
        x, y, c = _my_place()
        me = 4 * x + 2 * y + c
        gat_ref[me] = in_ref[...]
        copies = []
        for m in range(1, N_DEV):
            peer = (_flip(x, m & 4), _flip(y, m & 2), _flip(c, m & 1))
            cp = pltpu.make_async_remote_copy(
                src_ref=in_ref, dst_ref=gat_ref.at[me],
                send_sem=send_sems.at[m - 1], recv_sem=recv_sems.at[m - 1],
                device_id=peer, device_id_type=MESH)
            cp.start()
            copies.append(cp)
        for m in range(1, N_DEV):
            px, py, pc = _flip(x, m & 4), _flip(y, m & 2), _flip(c, m & 1)
            src_slot = gat_ref.at[4 * px + 2 * py + pc]
            pltpu.make_async_remote_copy(
                src_ref=src_slot, dst_ref=src_slot,
                send_sem=send_sems.at[m - 1], recv_sem=recv_sems.at[m - 1],
                device_id=(px, py, pc), device_id_type=MESH).wait_recv()
        for cp in copies:
            cp.wait_send()
        if reduce:
            tot = gat_ref[0]
            for d in range(1, N_DEV):
                tot = tot + gat_ref[d]
            out_ref[...] = tot
        else:
            out_ref[...] = gat_ref[...]

    out_shape = (rows, LANES) if reduce else (N_DEV, rows, LANES)
    return pl.pallas_call(
        body,
        in_specs=[pl.BlockSpec(memory_space=pltpu.VMEM)],
        out_specs=pl.BlockSpec(memory_space=pltpu.VMEM),
        out_shape=jax.ShapeDtypeStruct(out_shape, F32),
        scratch_shapes=[pltpu.VMEM((N_DEV, rows, LANES), F32),
                        pltpu.SemaphoreType.DMA((N_DEV - 1,)), pltpu.SemaphoreType.DMA((N_DEV - 1,))],
        compiler_params=pltpu.CompilerParams(has_side_effects=True),
        name=name,
    )(buf)


def _cast_into(shard, chip_core, *, name):
    rows, cols = shard.shape
    tr, tc = _tile(rows, cols)

    def body(cc_ref, s_ref, o_ref):
        del cc_ref
        o_ref[...] = s_ref[...].astype(BF16)

    grid_spec = pltpu.PrefetchScalarGridSpec(
        num_scalar_prefetch=1, grid=(rows // tr, cols // tc),
        in_specs=[pl.BlockSpec((tr, tc), lambda r, q, cc: (r, q))],
        out_specs=pl.BlockSpec((None, tr, tc), lambda r, q, cc: (cc[0], r, q)))
    return pl.pallas_call(
        body, grid_spec=grid_spec, out_shape=jax.ShapeDtypeStruct((N_CHIPS, rows, cols), BF16),
        compiler_params=_params("arbitrary", "arbitrary"), name=name,
    )(chip_core, shard)


def _remote(src, dst, send_sems, recv_sems, k, device):
    return pltpu.make_async_remote_copy(src_ref=src, dst_ref=dst, send_sem=send_sems.at[k],
                                        recv_sem=recv_sems.at[k], device_id=device, device_id_type=MESH)


def _col_half(ref, h, *lead, rows=None):
    hc = ref.shape[-1] // 2
    mid = (slice(None),) * (len(ref.shape) - 2 - len(lead))
    row_sel = slice(None) if rows is None else pl.ds(rows[0], rows[1])
    return ref.at[tuple(lead) + mid + (row_sel, pl.ds(h * hc, hc))]


def _gather_comm(bufs, rows=None, mid_at=0.75):
    n_w, n_m = len(bufs), len(CHIP_MASKS)
    rows = rows or [None] * n_w

    def first(c_ins, c_outs, ss, rs):
        x, y, c = _my_place()
        chip = 2 * x + y
        for w in range(n_w):
            mine = _col_half(c_outs[w], c, chip, rows=rows[w])
            for mi, (mx, my) in enumerate(CHIP_MASKS):
                _remote(mine, mine, ss, rs, w * n_m + mi, (_flip(x, mx), _flip(y, my), c)).start()

    def mid(c_ins, c_outs, ss, rs):
        x, y, c = _my_place()
        for w in range(n_w):
            for mi, (mx, my) in enumerate(CHIP_MASKS):
                k = w * n_m + mi
                px, py = _flip(x, mx), _flip(y, my)
                landed = _col_half(c_outs[w], c, 2 * px + py, rows=rows[w])
                _remote(landed, landed, ss, rs, k, (px, py, c)).wait_recv()
                _remote(landed, landed, ss, rs, n_w * n_m + k, (x, y, 1 - c)).start()

    def last(c_ins, c_outs, ss, rs):
        x, y, c = _my_place()
        chip = 2 * x + y
        for w in range(n_w):
            mine = _col_half(c_outs[w], c, chip, rows=rows[w])
            for mi, (mx, my) in enumerate(CHIP_MASKS):
                k = w * n_m + mi
                px, py = _flip(x, mx), _flip(y, my)
                theirs = _col_half(c_outs[w], 1 - c, 2 * px + py, rows=rows[w])
                _remote(theirs, theirs, ss, rs, n_w * n_m + k, (x, y, 1 - c)).wait_recv()
                _remote(mine, mine, ss, rs, k, (px, py, c)).wait_send()
                _remote(mine, mine, ss, rs, n_w * n_m + k, (x, y, 1 - c)).wait_send()

    return _Comm(ins=bufs, outs=[jax.ShapeDtypeStruct(b.shape, b.dtype) for b in bufs],
                 aliases={w: w for w in range(n_w)}, n_sems=2 * n_w * n_m, first=first, mid=mid, last=last,
                 mid_at=mid_at)


def _swap_comm(grads):
    n_w = len(grads)

    def copy(c_ins, c_outs, ss, rs, w):
        x, y, c = _my_place()
        return _remote(_col_half(c_ins[w], 1 - c), c_outs[w], ss, rs, w, (x, y, 1 - c))

    def first(c_ins, c_outs, ss, rs):
        for w in range(n_w):
            copy(c_ins, c_outs, ss, rs, w).start()

    def last(c_ins, c_outs, ss, rs):
        for w in range(n_w):
            copy(c_ins, c_outs, ss, rs, w).wait()

    return _Comm(ins=grads,
                 outs=[jax.ShapeDtypeStruct(g.shape[:-1] + (g.shape[-1] // 2,), g.dtype) for g in grads],
                 aliases={}, n_sems=n_w, first=first, last=last)


def _add_own_half(g, other, chip_core, *, name):
    n_chip, rows, hc = other.shape
    tr, tc = _tile(rows, hc)
    per_half = hc // tc

    def body(cc_ref, g_ref, o_ref, out_ref):
        del cc_ref
        out_ref[...] = (g_ref[...].astype(F32) + o_ref[...].astype(F32)).astype(BF16)

    grid_spec = pltpu.PrefetchScalarGridSpec(
        num_scalar_prefetch=1, grid=(n_chip, rows // tr, per_half),
        in_specs=[pl.BlockSpec((None, tr, tc), lambda j, r, q, cc: (j, r, cc[1] * per_half + q)),
                  pl.BlockSpec((None, tr, tc), lambda j, r, q, cc: (j, r, q))],
        out_specs=pl.BlockSpec((None, tr, tc), lambda j, r, q, cc: (j, r, q)))
    return pl.pallas_call(
        body, grid_spec=grid_spec, out_shape=jax.ShapeDtypeStruct((n_chip, rows, hc), BF16),
        compiler_params=_params("parallel", "parallel", "parallel"), name=name,
    )(chip_core, g, other)


def _add_own_half_side(gs, others, *, n_steps, step_of):
    n_chip, rows, hc = others[0].shape
    per_chip = n_steps // n_chip
    tr = rows // per_chip
    assert all(o.shape == others[0].shape for o in others) and tr * per_chip == rows and tr % 16 == 0

    def body(step, ins, outs, scr):
        del step, scr
        for q, out_ref in enumerate(outs):
            out_ref[...] = (ins[2 * q][...].astype(F32) + ins[2 * q + 1][...].astype(F32)).astype(BF16)

    def blk(own_half):
        def index(i, j, k, place):
            s = step_of(i, j, k)
            return (s // per_chip, s % per_chip, place[1] if own_half else 0)
        return _PlaceSpec((None, tr, hc), index)

    return _Side(
        ins=[a for pair in zip(gs, others) for a in pair], in_specs=[blk(True), blk(False)] * len(gs),
        outs=[jax.ShapeDtypeStruct(o.shape, BF16) for o in others], out_specs=[blk(False)] * len(gs),
        scratch=[], init=_zero_refs, body=body)


def _cast_side(shards, *, n_steps, step_of):
    def body(step, ins, outs, scr):
        del step, scr
        for in_ref, out_ref in zip(ins, outs):
            out_ref[...] = in_ref[...].astype(BF16)

    def rows(s):
        assert s.shape[0] % (16 * n_steps) == 0, s.shape
        return s.shape[0] // n_steps

    return _Side(
        ins=list(shards),
        in_specs=[pl.BlockSpec((rows(s), s.shape[1]), lambda i, j, k: (step_of(i, j, k), 0)) for s in shards],
        outs=[jax.ShapeDtypeStruct((N_CHIPS,) + s.shape, BF16) for s in shards],
        out_specs=[_PlaceSpec((None, rows(s), s.shape[1]),
                              lambda i, j, k, place: (place[0], step_of(i, j, k), 0)) for s in shards],
        scratch=[], init=_zero_refs, body=body)


def _pow2_below(n_steps, most=16):
    return min(most, 1 << (n_steps.bit_length() - 1))


def _sum_chips_side(own, landed, *, n_blocks, step_of):
    n_chip, rows, hc = own.shape
    tr = rows // n_blocks
    assert tr * n_blocks == rows and tr % 16 == 0

    def body(step, ins, outs, scr):
        del step, scr
        o_ref, l1_ref, l2_ref, l3_ref = ins
        outs[0][...] = ((o_ref[...].astype(F32) + l1_ref[...].astype(F32))
                        + l2_ref[...].astype(F32)) + l3_ref[...].astype(F32)

    def block(i, j, k):
        return jnp.minimum(step_of(i, j, k), n_blocks - 1)

    def slot(q):
        return _PlaceSpec((None, tr, hc),
                          lambda i, j, k, place, q=q: ((place[0] + q) % n_chip, block(i, j, k), 0))

    return _Side(
        ins=[own, landed, landed, landed], in_specs=[slot(0), slot(1), slot(2), slot(3)],
        outs=[jax.ShapeDtypeStruct((rows, 2 * hc), F32)],
        out_specs=[_PlaceSpec((tr, hc), lambda i, j, k, place: (block(i, j, k), place[1]))],
        scratch=[], init=_zero_refs, body=body)


def _exchange_comm(pieces):
    n_w, n_m = len(pieces), len(CHIP_MASKS)

    def copies(c_ins, c_outs, ss, rs):
        x, y, c = _my_place()
        chip = 2 * x + y
        for w in range(n_w):
            for mi, (mx, my) in enumerate(CHIP_MASKS):
                px, py = _flip(x, mx), _flip(y, my)
                send = _remote(c_ins[w].at[2 * px + py], c_outs[w].at[chip], ss, rs, w * n_m + mi, (px, py, c))
                landed = c_outs[w].at[2 * px + py]
                yield send, _remote(landed, landed, ss, rs, w * n_m + mi, (px, py, c))

    def first(c_ins, c_outs, ss, rs):
        for send, _ in copies(c_ins, c_outs, ss, rs):
            send.start()

    def last(c_ins, c_outs, ss, rs):
        for send, arrival in copies(c_ins, c_outs, ss, rs):
            arrival.wait_recv()
            send.wait_send()

    return _Comm(ins=pieces, outs=[jax.ShapeDtypeStruct(p.shape, p.dtype) for p in pieces],
                 aliases={}, n_sems=n_w * n_m, first=first, last=last)


def _sum_chips(own, landed, chip_core, *, name):
    n_chip, rows, hc = own.shape
    tr, tc = _tile(rows, hc)
    per_half = hc // tc

    def body(cc_ref, o_ref, l1_ref, l2_ref, l3_ref, out_ref):
        del cc_ref
        out_ref[...] = ((o_ref[...].astype(F32) + l1_ref[...].astype(F32))
                        + l2_ref[...].astype(F32)) + l3_ref[...].astype(F32)

    def slot(k):
        return pl.BlockSpec((None, tr, tc), lambda r, q, cc, k=k: ((cc[0] + k) % n_chip, r, q))

    grid_spec = pltpu.PrefetchScalarGridSpec(
        num_scalar_prefetch=1, grid=(rows // tr, per_half),
        in_specs=[slot(0), slot(1), slot(2), slot(3)],
        out_specs=pl.BlockSpec((tr, tc), lambda r, q, cc: (r, cc[1] * per_half + q)))
    return pl.pallas_call(
        body, grid_spec=grid_spec, out_shape=jax.ShapeDtypeStruct((rows, 2 * hc), F32),
        compiler_params=_params("arbitrary", "arbitrary"), name=name,
    )(chip_core, own, landed, landed, landed)


def _join_comm(halves):
    n_w = len(halves)

    def first(c_ins, c_outs, ss, rs):
        x, y, c = _my_place()
        for w in range(n_w):
            mine = _col_half(c_outs[w], c)
            _remote(mine, mine, ss, rs, w, (x, y, 1 - c)).start()

    def last(c_ins, c_outs, ss, rs):
        x, y, c = _my_place()
        for w in range(n_w):
            theirs = _col_half(c_outs[w], 1 - c)
            _remote(theirs, theirs, ss, rs, w, (x, y, 1 - c)).wait()

    return _Comm(ins=halves, outs=[jax.ShapeDtypeStruct(h.shape, h.dtype) for h in halves],
                 aliases={w: w for w in range(n_w)}, n_sems=n_w, first=first, last=last)


def _standalone(comm, *, name):
    def body(o_ref):
        o_ref[...] = jnp.zeros_like(o_ref)

    return _call(body, grid=(1,), in_specs=[],
                 out_specs=[pl.BlockSpec((SUBLANES, LANES), lambda i: (0, 0))],
                 out_shape=[jax.ShapeDtypeStruct((SUBLANES, LANES), F32)], operands=(), name=name,
                 comm=comm)[1:]


def _pack(pieces):
    flat, spans, off = [], [], 0
    for p in pieces:
        v = p.reshape(-1).astype(F32)
        pad = (-v.shape[0]) % LANES
        if pad:
            v = jnp.concatenate([v, jnp.zeros((pad,), F32)])
        spans.append((off, p.size))
        off += v.shape[0]
        flat.append(v)
    tail = (-off) % (SUBLANES * LANES)
    if tail:
        flat.append(jnp.zeros((tail,), F32))
    return jnp.concatenate(flat).reshape(-1, LANES), spans


def _unpack(buf, span, shape):
    off, size = span
    return buf.reshape(-1)[off:off + size].reshape(shape)


def kernel(x, norm1_g, w_in, w_gate_up, b_gate, conv_w, conv_norm_g, gla_norm_g, w_out, norm2_g, w_ff1, w_ff2, norm_f_g, loss_target, m_norm1_g, m_w_in, m_w_gate_up, m_b_gate, m_conv_w, m_conv_norm_g, m_gla_norm_g, m_w_out, m_norm2_g, m_w_ff1, m_w_ff2, m_norm_f_g, v_norm1_g, v_w_in, v_w_gate_up, v_b_gate, v_conv_w, v_conv_norm_g, v_gla_norm_g, v_w_out, v_norm2_g, v_w_ff1, v_w_ff2, v_norm_f_g):
    xs = x[0]
    target = loss_target[0]
    s_len, d = xs.shape
    d_in = w_in.shape[2] * N_CHIPS
    d_main = d_in - GATE_RANK
    d_ff = w_ff1.shape[2] * N_CHIPS
    cx, cy, cc = _my_place()
    chip = 2 * cx + cy
    chip_core = jnp.stack([chip, cc]).astype(jnp.int32)
    n_ff = d_ff // N_CHIPS
    norm_f = norm_f_g.reshape(1, d)

    n_sh = d_in // N_CHIPS
    wi_buf = _cast_into(w_in[0].T, chip_core, name="cast_w_in")

    small_w, spans_w = _pack([w_gate_up[0], conv_w[0]])
    small_all = _allgather_small(small_w, reduce=False, name="gather_small_weights")
    chips_first = [small_all[2 * j] for j in range(N_CHIPS)]
    wgu_full = jnp.concatenate(
        [_unpack(b, spans_w[0], w_gate_up.shape[1:]) for b in chips_first], axis=1)
    convw_full = jnp.concatenate(
        [_unpack(b, spans_w[1], conv_w.shape[1:]) for b in chips_first], axis=0)
    wgu_pad = jnp.concatenate(
        [wgu_full, jnp.zeros((LANES - GATE_RANK, wgu_full.shape[1]), F32)], axis=0).astype(BF16)
    convw_t = convw_full.T

    u, wo_buf, w1_buf, w2_buf, wi_buf = _rms_fwd(
        xs, norm1_g, name="norm1_fwd", comm=_gather_comm([wi_buf]), place=chip_core,
        side=_cast_side([w_out[0], w_ff1[0], w_ff2[0]], n_steps=s_len // 512, step_of=lambda i, j, k: i))
    wi_t = wi_buf.reshape(d_in, d)
    wg_t = jnp.concatenate([wi_t[d_main:], jnp.zeros((LANES - GATE_RANK, d), BF16)], axis=0)
    z, wo_buf, w1_buf = _matmul(u, wi_t, tb=True, tm=2048, tn=1024, tk=d, out_dtypes=[F32], n_dim=d_main,
                                name="in_proj",
                                comm=_gather_comm([wo_buf, w1_buf], rows=[None, (0, d // 2)]))
    wo_full = wo_buf.reshape(d, d)
    (alow,) = _matmul(u, wg_t, tb=True, tm=1024, tn=LANES, tk=d, out_dtypes=[F32], name="in_proj_gate")
    y0 = _conv_fwd(z, convw_t, conv_norm_g)
    y, o, la, st, w1_cm = _gla_fwd(z, alow, wgu_pad, b_gate, gla_norm_g, y0,
                                   comm=_gather_comm([w1_buf], rows=[(d // 2, d // 2)], mid_at=0.9))
    x2, h = _matmul(y, wo_full, tm=512, tn=d, tk=d, out_dtypes=[F32, BF16], extras=(xs, norm2_g),
                    epilogue=_residual_norm_epilogue, name="out_proj")
    a, p, w2_buf = _matmul(
        h, w1_cm, tm=2048, tn=1024, tk=d, out_dtypes=[BF16, BF16], n_dim=d_ff,
        b_spec=pl.BlockSpec((None, d, 1024), lambda i, j, k: (j // 2, 0, j % 2)),
        epilogue=lambda acc: (acc, jnp.square(jnp.maximum(acc, 0.0))), name="ff1",
        comm=_gather_comm([w2_buf]))
    w2_full = w2_buf.reshape(d_ff, d)
    (x3,) = _matmul(p, w2_full, tm=1024, tn=1024, tk=2048, out_dtypes=[F32], extras=(x2,),
                    epilogue=_add_epilogue, name="ff2")
    dx3b, g_normf, loss_part = _loss_head(x3, norm_f, target)

    (da,) = _matmul(dx3b, w2_full, tb=True, tm=2048, tn=1024, tk=d, out_dtypes=[BF16], extras=(a,),
                    epilogue=lambda acc, av: (acc * (2.0 * jnp.maximum(av, 0.0)),), name="ff2_dx")
    (dh,) = _matmul(
        da, w1_cm, tb=True, tm=2048, tn=1024, tk=2048, out_dtypes=[BF16], n_dim=d,
        b_spec=pl.BlockSpec((None, 1024, 2048), lambda i, j, k: (k, j, 0)), name="ff1_dx")
    dw_k = s_len // 1024
    g_w2, dx2, g_norm2 = _matmul(
        p, dx3b, ta=True, tm=1024, tn=d, tk=1024, out_dtypes=[BF16], name="ff2_dw",
        side=_rms_bwd_side(dh, x2, norm2_g, dx3b, n_steps=(d_ff // 1024) * dw_k,
                           block_of_step=lambda i, j, k: i * dw_k + k))
    g_w2 = g_w2.reshape(N_CHIPS, n_ff, d)
    dy, t_w2 = _matmul(dx2, wo_full, tb=True, tm=1024, tn=1024, tk=d, out_dtypes=[BF16],
                       name="out_proj_dx", comm=_swap_comm([g_w2]))
    f1_j, f1_k = d_ff // n_ff, s_len // 2048
    g_w1, dz0, g_convw_t, g_convg = _matmul(
        h, da, ta=True, tm=1024, tn=n_ff, tk=2048, out_dtypes=[BF16], name="ff1_dw",
        out_shapes=[jax.ShapeDtypeStruct((N_CHIPS, d, n_ff), BF16)],
        out_specs=[pl.BlockSpec((None, 1024, n_ff), lambda i, j, k: (j, i, 0))],
        side=_conv_bwd_side(dy, z, convw_t, conv_norm_g, n_steps=(d // 1024) * f1_j * f1_k,
                            step_of=lambda i, j, k: (i * f1_j + j) * f1_k + k))
    ow_k = s_len // 512
    g_wo, p_w2, t_w1 = _matmul(
        y, dx2, ta=True, tm=1024, tn=d, tk=512, out_dtypes=[BF16], name="out_proj_dw",
        side=_add_own_half_side([g_w2], [t_w2], n_steps=(d // 1024) * ow_k,
                                step_of=lambda i, j, k: i * ow_k + k),
        comm=_swap_comm([g_w1]), place=chip_core)
    g_wo = g_wo.reshape(N_CHIPS, d // N_CHIPS, d)
    dz, dalow, g_wgu_pad, g_bg, g_gg, p_w1, l_w2 = _gla_bwd(
        dy, z, o, la, st, alow, wgu_pad, gla_norm_g, dz0, comm=_exchange_comm([p_w2]), place=chip_core,
        side=_add_own_half_side([g_w1], [t_w1], n_steps=s_len // 512, step_of=lambda i, j, k: i))
    dwi_k = s_len // 2048
    g_wi_t, m_w2, l_w1 = _matmul(
        dz, u, ta=True, tm=1024, tn=d, tk=2048, out_dtypes=[BF16], name="in_proj_dw",
        out_shapes=[jax.ShapeDtypeStruct((d_in, d), BF16)], comm=_exchange_comm([p_w1]),
        side=_sum_chips_side(p_w2, l_w2, n_blocks=_pow2_below(d_main // 1024 * dwi_k),
                             step_of=lambda i, j, k: i * dwi_k + k),
        place=chip_core)
    g_wi_t, m_w1 = _matmul(
        dalow, u, ta=True, tm=LANES, tn=d, tk=1024, out_dtypes=[BF16],
        epilogue=lambda acc: (acc[:GATE_RANK],), into=g_wi_t, name="in_proj_gate_dw",
        out_shapes=[jax.ShapeDtypeStruct((d_in, d), BF16)],
        out_specs=[pl.BlockSpec((GATE_RANK, d), lambda i, j, k: (d_main // GATE_RANK, 0))],
        side=_sum_chips_side(p_w1, l_w1, n_blocks=_pow2_below(s_len // 1024), step_of=lambda i, j, k: k),
        place=chip_core)
    t_wi, t_wo = _standalone(_swap_comm([g_wi_t, g_wo]), name="swap_w_in_w_out")
    p_wi = _add_own_half(g_wi_t[None], t_wi[None], chip_core, name="pre_reduce_w_in")
    p_wi = p_wi.reshape(N_CHIPS, n_sh, d // 2)
    p_wo = _add_own_half(g_wo, t_wo, chip_core, name="pre_reduce_w_out")
    du, l_wi, l_wo, m_w1, m_w2 = _matmul(
        dz, wi_t, tm=2048, tn=1024, tk=2048, out_dtypes=[BF16], n_dim=d, extras=(dalow, wg_t),
        epilogue=lambda acc, dal, wg: (acc + jnp.dot(dal.astype(BF16), wg, preferred_element_type=F32),),
        name="in_proj_dx", comm=[_exchange_comm([p_wi, p_wo]), _join_comm([m_w1, m_w2])])
    m_wi = _sum_chips(p_wi, l_wi, chip_core, name="reduce_w_in")
    m_wo = _sum_chips(p_wo, l_wo, chip_core, name="reduce_w_out")
    grad_x, g_norm1 = _rms_bwd(du, xs, norm1_g, dx2, name="norm1_bwd")
    m_wi, m_wo = _standalone(_join_comm([m_wi, m_wo]), name="join_w_in_w_out")
    g_big = [m_wi, m_wo, m_w1, m_w2]

    small_g, spans_g = _pack([g_norm1, g_wgu_pad[:GATE_RANK], g_bg, g_convw_t, g_convg, g_gg, g_norm2,
                              g_normf, loss_part[:, :1]])
    tot = _allgather_small(small_g, reduce=True, name="reduce_small_grads")
    t_norm1 = _unpack(tot, spans_g[0], (1, d))
    t_wgu = _unpack(tot, spans_g[1], (GATE_RANK, HEADS * DK))
    t_bg = _unpack(tot, spans_g[2], (1, HEADS * DK))
    t_convw = _unpack(tot, spans_g[3], (3, D_CONV)).T
    t_convg = _unpack(tot, spans_g[4], (1, D_CONV))
    t_gg = _unpack(tot, spans_g[5], (1, DV))
    t_norm2 = _unpack(tot, spans_g[6], (1, d))
    t_normf = _unpack(tot, spans_g[7], (1, d))
    loss = _unpack(tot, spans_g[8], ())
    n_gu = w_gate_up.shape[2]
    n_cw = conv_w.shape[1]
    t_wgu = lax.dynamic_slice(t_wgu, (0, chip * n_gu), (GATE_RANK, n_gu))
    t_convw = lax.dynamic_slice(t_convw, (chip * n_cw, 0), (n_cw, 3))

    order = ["norm1_g", "w_in", "w_gate_up", "b_gate", "conv_w", "conv_norm_g", "gla_norm_g", "w_out",
             "norm2_g", "w_ff1", "w_ff2", "norm_f_g"]
    weights = dict(norm1_g=norm1_g, w_in=w_in, w_gate_up=w_gate_up, b_gate=b_gate, conv_w=conv_w,
                   conv_norm_g=conv_norm_g, gla_norm_g=gla_norm_g, w_out=w_out, norm2_g=norm2_g,
                   w_ff1=w_ff1, w_ff2=w_ff2, norm_f_g=norm_f_g)
    moms = dict(norm1_g=m_norm1_g, w_in=m_w_in, w_gate_up=m_w_gate_up, b_gate=m_b_gate, conv_w=m_conv_w,
                conv_norm_g=m_conv_norm_g, gla_norm_g=m_gla_norm_g, w_out=m_w_out, norm2_g=m_norm2_g,
                w_ff1=m_w_ff1, w_ff2=m_w_ff2, norm_f_g=m_norm_f_g)
    vels = dict(norm1_g=v_norm1_g, w_in=v_w_in, w_gate_up=v_w_gate_up, b_gate=v_b_gate, conv_w=v_conv_w,
                conv_norm_g=v_conv_norm_g, gla_norm_g=v_gla_norm_g, w_out=v_w_out, norm2_g=v_norm2_g,
                w_ff1=v_w_ff1, w_ff2=v_w_ff2, norm_f_g=v_norm_f_g)
    grads2d = dict(norm1_g=t_norm1, w_in=g_big[0], w_gate_up=t_wgu, b_gate=t_bg, conv_w=t_convw,
                   conv_norm_g=t_convg, gla_norm_g=t_gg, w_out=g_big[1], norm2_g=t_norm2,
                   w_ff1=g_big[2], w_ff2=g_big[3], norm_f_g=t_normf)
    out_g, out_d, out_m, out_v = [], [], [], []
    for nm in order:
        w = weights[nm]
        g2 = grads2d[nm]
        if nm == "w_in":
            to2d, back = (lambda t: t[0].T), (lambda t: t.T.reshape(w.shape))
        else:
            to2d, back = (lambda t: t.reshape(g2.shape)), (lambda t: t.reshape(w.shape))
        res = _adamw(to2d(w), g2, to2d(moms[nm]), to2d(vels[nm]), name="adamw_" + nm)
        for lst, r in zip((out_g, out_d, out_m, out_v), res):
            lst.append(back(r))
    return (loss, grad_x.reshape(x.shape), *out_g, *out_d, *out_m, *out_v)
```
